```python
import math
import jax
import jax.numpy as jnp
from jax import lax
import numpy as np


D_MODEL = 1024
BATCH = 4
SEQ = 8192
DEPTH = 2

GRID_W = 64
CTX_LEN = 256
N_MIXERS = 2
N_ATTN_LAYERS = (DEPTH + 1) // 2
N_POOL_LAYERS = DEPTH // 2
N_MOD = 6
N_HEADS = 8
HEAD_DIM = 64
V_DIM = 2 * HEAD_DIM
ROPE_THETA = 10000.0
Q_BLOCK = 128
POOL_WINDOWS = (2, 4, 8, 16)
N_POOL_GROUPS = 4
POOL_GROUP_DIM = D_MODEL // N_POOL_GROUPS
N_EXPERTS = 32
N_EXPERT_GROUPS = 4
EXPERTS_PER_GROUP = N_EXPERTS // N_EXPERT_GROUPS
TOP_K = 2
D_EXPERT = D_MODEL // 2
MOE_BLOCK = 256
NORM_EPS = 1e-6

kernel_name = 'hybrid_diffattn_pool_moe_dit'


def rms_norm(x, gain):
    xf = x.astype(jnp.float32)
    y = xf * lax.rsqrt(jnp.mean(xf * xf, axis=-1, keepdims=True) + NORM_EPS)
    return (y * gain.astype(jnp.float32)).astype(x.dtype)


def modulate(h, shift, scale):
    return h * (1 + scale) + shift


def rope_2d_tables(n_tokens, dtype):
    rows = n_tokens // GRID_W
    row = jnp.repeat(jnp.arange(rows, dtype=jnp.float32), GRID_W)
    col = jnp.tile(jnp.arange(GRID_W, dtype=jnp.float32), rows)
    half = HEAD_DIM // 2
    inv_freq = ROPE_THETA ** (-jnp.arange(0, half, 2, dtype=jnp.float32) / half)
    ang_r = row[:, None] * inv_freq
    ang_c = col[:, None] * inv_freq
    ang = jnp.concatenate([ang_r, ang_r, ang_c, ang_c], axis=-1)
    return jnp.cos(ang).astype(dtype), jnp.sin(ang).astype(dtype)


def apply_rope_2d(x, cos, sin):
    x1, x2, x3, x4 = jnp.split(x, 4, axis=-1)
    rot = jnp.concatenate([-x2, x1, -x4, x3], axis=-1)
    return x * cos + rot * sin


def diff_attention(h, hc, w_in, w_out, q_gain, k_gain, lam, lam_init, sub_gain, cos, sin, ctx_out):
    B, L, D = h.shape
    C = hc.shape[1]

    def qk_heads(t, n):
        return t.reshape(B, n, N_HEADS, 2, HEAD_DIM).transpose(0, 2, 3, 1, 4)

    def v_heads(t, n):
        return t.reshape(B, n, N_HEADS, V_DIM).transpose(0, 2, 1, 3)

    q, k, v = jnp.split(h @ w_in, 3, axis=-1)
    q = apply_rope_2d(rms_norm(qk_heads(q, L), q_gain), cos, sin)
    k = apply_rope_2d(rms_norm(qk_heads(k, L), k_gain), cos, sin)
    v = v_heads(v, L)
    kc, vc = jnp.split(hc @ w_in[:, D:], 2, axis=-1)
    kc = rms_norm(qk_heads(kc, C), k_gain)
    vc = v_heads(vc, C)
    k_all = jnp.concatenate([k, kc], axis=3)
    v_all = jnp.concatenate([v, vc], axis=2)
    scale = HEAD_DIM ** -0.5

    def attend(q_blk, keys, vals):
        s = jnp.einsum('bhmqd,bhmkd->bhmqk', q_blk, keys).astype(jnp.float32) * scale
        p = jax.nn.softmax(s, axis=-1)
        w = (p[:, :, 0] - lam * p[:, :, 1]).astype(vals.dtype)
        return jnp.einsum('bhqk,bhkv->bhqv', w, vals)

    def finish(o):
        o = rms_norm(o, sub_gain) * (1 - lam_init)
        return o.reshape(o.shape[0], o.shape[1], N_HEADS * V_DIM) @ w_out

    n_blk = L // Q_BLOCK
    q_blocks = q.reshape(B, N_HEADS, 2, n_blk, Q_BLOCK, HEAD_DIM).transpose(3, 0, 1, 2, 4, 5)
    o = lax.map(lambda qb: attend(qb, k_all, v_all), q_blocks)
    y = finish(o.transpose(1, 0, 3, 2, 4).reshape(B, L, N_HEADS, V_DIM))
    yc = None
    if ctx_out:
        qc = rms_norm(qk_heads(hc @ w_in[:, :D], C), q_gain)
        yc = finish(attend(qc, kc, vc).transpose(0, 2, 1, 3))
    return y, yc


def pool_mixer(h, w_in, w_group, ch_scale, w_out):
    B, L, D = h.shape
    u = h @ w_in
    cs = jnp.concatenate([jnp.zeros((B, 1, D), jnp.float32),
                          jnp.cumsum(u.astype(jnp.float32), axis=1)], axis=1)
    t = jnp.arange(L)
    outs = []
    for g, win in enumerate(POOL_WINDOWS):
        sl = slice(g * POOL_GROUP_DIM, (g + 1) * POOL_GROUP_DIM)
        lo = jnp.maximum(t - win // 2, 0)
        hi = jnp.minimum(t + win // 2, L)
        cs_g = cs[:, :, sl]
        mean = (jnp.take(cs_g, hi, axis=1) - jnp.take(cs_g, lo, axis=1)) / (hi - lo).astype(jnp.float32)[None, :, None]
        d = (mean - u[:, :, sl].astype(jnp.float32)).astype(h.dtype)
        outs.append(d @ w_group[g])
    y = jnp.concatenate(outs, axis=-1) * ch_scale
    return y @ w_out


def route(h_flat, router_w, router_b):
    n = h_flat.shape[0]
    logits = (h_flat @ router_w).astype(jnp.float32) + router_b.astype(jnp.float32)
    probs = jax.nn.softmax(logits, axis=-1).reshape(n, N_EXPERT_GROUPS, EXPERTS_PER_GROUP)
    group_score = lax.top_k(probs, TOP_K)[0].sum(-1)
    g_sel = jnp.argmax(group_score, axis=-1)
    in_group = probs[jnp.arange(n), g_sel]
    vals, idx = lax.top_k(in_group, TOP_K)
    experts = g_sel[:, None] * EXPERTS_PER_GROUP + idx
    gates = vals / jnp.sum(vals, axis=-1, keepdims=True)
    return experts, gates


def moe(h_flat, router_w, router_b, w_gate, w_up, w_down):
    n, d = h_flat.shape
    experts, gates = route(h_flat, router_w, router_b)
    a = n * TOP_K
    flat_e = experts.reshape(-1)
    flat_tok = jnp.repeat(jnp.arange(n, dtype=jnp.int32), TOP_K)
    flat_gate = gates.reshape(-1)
    order = jnp.argsort(flat_e)
    sorted_e = flat_e[order]
    counts = jnp.bincount(flat_e, length=N_EXPERTS)
    padded = (counts + MOE_BLOCK - 1) // MOE_BLOCK * MOE_BLOCK
    start = jnp.cumsum(counts) - counts
    pad_end = jnp.cumsum(padded)
    pad_start = pad_end - padded
    dest = pad_start[sorted_e] + jnp.arange(a) - start[sorted_e]
    n_blocks = -(-a // MOE_BLOCK) + N_EXPERTS
    n_slots = n_blocks * MOE_BLOCK
    slot_tok = jnp.full((n_slots,), n, jnp.int32).at[dest].set(flat_tok[order])
    slot_gate = jnp.zeros((n_slots,), jnp.float32).at[dest].set(flat_gate[order])
    block_start = jnp.arange(n_blocks) * MOE_BLOCK
    block_e = jnp.minimum(jnp.sum(pad_end[None, :] <= block_start[:, None], axis=1), N_EXPERTS - 1)
    h_pad = jnp.concatenate([h_flat, jnp.zeros((1, d), h_flat.dtype)], axis=0)
    xs = h_pad[slot_tok].reshape(n_blocks, MOE_BLOCK, d)

    def expert_block(args):
        xb, e = args
        return (jax.nn.silu(xb @ w_gate[e]) * (xb @ w_up[e])) @ w_down[e]

    ys = lax.map(expert_block, (xs, block_e)).reshape(n_slots, d)
    ys = ys * slot_gate[:, None].astype(ys.dtype)
    return jax.ops.segment_sum(ys, slot_tok, num_segments=n + 1)[:n]


def setup_inputs(seed: int = 0) -> dict:
    key = jax.random.key(seed)
    ks = jax.random.split(key, 26)
    f32 = jnp.float32
    D = D_MODEL
    G = POOL_GROUP_DIM

    def nrm(k, shape, scale):
        return jax.random.normal(k, shape, f32) * scale

    return {
        'x': nrm(ks[0], (BATCH, SEQ, D), 1.0),
        'c': nrm(ks[1], (BATCH, D), 1.0),
        'ctx': nrm(ks[2], (BATCH, CTX_LEN, D), 1.0),
        'c_ctx': nrm(ks[3], (D,), 1.0),
        'ada_w': nrm(ks[4], (DEPTH, D, N_MOD * D), 0.5 * D ** -0.5),
        'ada_b': nrm(ks[5], (DEPTH, N_MOD * D), 0.01),
        'norm1_g': 1.0 + nrm(ks[6], (DEPTH, D), 0.05),
        'norm2_g': 1.0 + nrm(ks[7], (DEPTH, D), 0.05),
        'attn_w_in': nrm(ks[8], (N_ATTN_LAYERS, D, 3 * D), D ** -0.5),
        'attn_w_out': nrm(ks[9], (N_ATTN_LAYERS, N_HEADS * V_DIM, D), (N_HEADS * V_DIM) ** -0.5),
        'attn_q_gain': 1.0 + nrm(ks[10], (N_ATTN_LAYERS, HEAD_DIM), 0.05),
        'attn_k_gain': 1.0 + nrm(ks[11], (N_ATTN_LAYERS, HEAD_DIM), 0.05),
        'attn_lq1': nrm(ks[12], (N_ATTN_LAYERS, HEAD_DIM), 0.1),
        'attn_lk1': nrm(ks[13], (N_ATTN_LAYERS, HEAD_DIM), 0.1),
        'attn_lq2': nrm(ks[14], (N_ATTN_LAYERS, HEAD_DIM), 0.1),
        'attn_lk2': nrm(ks[15], (N_ATTN_LAYERS, HEAD_DIM), 0.1),
        'attn_sub_gain': 1.0 + nrm(ks[16], (N_ATTN_LAYERS, V_DIM), 0.05),
        'pool_w_in': nrm(ks[17], (N_POOL_LAYERS, D, D), D ** -0.5),
        'pool_w_group': nrm(ks[18], (N_POOL_LAYERS, N_POOL_GROUPS, G, G), G ** -0.5),
        'pool_scale': 1.0 + nrm(ks[19], (N_POOL_LAYERS, D), 0.1),
        'pool_w_out': nrm(ks[20], (N_POOL_LAYERS, D, D), D ** -0.5),
        'router_w': nrm(ks[21], (D, N_EXPERTS), D ** -0.5),
        'router_b': nrm(ks[22], (N_EXPERTS,), 0.01),
        'moe_w_gate': nrm(ks[23], (DEPTH, N_EXPERTS, D, D_EXPERT), D ** -0.5),
        'moe_w_up': nrm(ks[24], (DEPTH, N_EXPERTS, D, D_EXPERT), D ** -0.5),
        'moe_w_down': nrm(ks[25], (DEPTH, N_EXPERTS, D_EXPERT, D), D_EXPERT ** -0.5),
    }


def reference(x, c, ctx, c_ctx, ada_w, ada_b, norm1_g, norm2_g,
              attn_w_in, attn_w_out, attn_q_gain, attn_k_gain,
              attn_lq1, attn_lk1, attn_lq2, attn_lk2, attn_sub_gain,
              pool_w_in, pool_w_group, pool_scale, pool_w_out,
              router_w, router_b, moe_w_gate, moe_w_up, moe_w_down):
    B, L, D = x.shape
    cos, sin = rope_2d_tables(L, x.dtype)
    c_act = jax.nn.silu(c)
    cc_act = jax.nn.silu(c_ctx)
    for i in range(DEPTH):
        kind = i % N_MIXERS
        j = i // N_MIXERS
        update_ctx = any(l % N_MIXERS == 0 for l in range(i + 1, DEPTH))
        need_ctx_in = (kind == 0) or update_ctx
        mod = (c_act @ ada_w[i] + ada_b[i])[:, None, :]
        sh1, s1, g1, sh2, s2, g2 = jnp.split(mod, N_MOD, axis=-1)
        h = modulate(rms_norm(x, norm1_g[i]), sh1, s1)
        if need_ctx_in:
            mod_c = cc_act @ ada_w[i] + ada_b[i]
            csh1, cs1, cg1, csh2, cs2, cg2 = jnp.split(mod_c, N_MOD, axis=-1)
            hc = modulate(rms_norm(ctx, norm1_g[i]), csh1, cs1)
        if kind == 0:
            lam_init = 0.8 - 0.6 * math.exp(-0.3 * i)
            f32 = jnp.float32
            lam = (jnp.exp(jnp.sum(attn_lq1[j].astype(f32) * attn_lk1[j].astype(f32)))
                   - jnp.exp(jnp.sum(attn_lq2[j].astype(f32) * attn_lk2[j].astype(f32))) + lam_init)
            y, yc = diff_attention(h, hc, attn_w_in[j], attn_w_out[j], attn_q_gain[j], attn_k_gain[j],
                                   lam, lam_init, attn_sub_gain[j], cos, sin, update_ctx)
        else:
            y = pool_mixer(h, pool_w_in[j], pool_w_group[j], pool_scale[j], pool_w_out[j])
            yc = pool_mixer(hc, pool_w_in[j], pool_w_group[j], pool_scale[j], pool_w_out[j]) if update_ctx else None
        x = x + g1 * y
        h2 = modulate(rms_norm(x, norm2_g[i]), sh2, s2)
        if update_ctx:
            ctx = ctx + cg1 * yc
            h2c = modulate(rms_norm(ctx, norm2_g[i]), csh2, cs2)
            tokens = jnp.concatenate([h2.reshape(-1, D), h2c.reshape(-1, D)], axis=0)
            out = moe(tokens, router_w, router_b, moe_w_gate[i], moe_w_up[i], moe_w_down[i])
            x = x + g2 * out[:B * L].reshape(B, L, D)
            ctx = ctx + cg2 * out[B * L:].reshape(ctx.shape)
        else:
            out = moe(h2.reshape(-1, D), router_w, router_b, moe_w_gate[i], moe_w_up[i], moe_w_down[i])
            x = x + g2 * out.reshape(B, L, D)
    return x
```

```python
import functools
import math

import jax
import jax.numpy as jnp
from jax import lax
from jax.experimental import pallas as pl
from jax.experimental.pallas import tpu as pltpu

F32 = jnp.float32
BF16 = jnp.bfloat16

LANES = 128
SUBLANES = 8
N_HEADS = 8
HEAD_DIM = 64
V_DIM = 2 * HEAD_DIM
GRID_W = 64
ROPE_THETA = 10000.0
NORM_EPS = 1e-6
N_MOD = 6
POOL_WINDOWS = (2, 4, 8, 16)
POOL_HALO = max(POOL_WINDOWS) // 2
N_EXPERTS = 32
N_EXPERT_GROUPS = 4
EXPERTS_PER_GROUP = N_EXPERTS // N_EXPERT_GROUPS
TOP_K = 2
SLOT_ROWS = 256
VMEM_LIMIT = 48 * 1024 * 1024
NT_DIMS = (((1,), (1,)), ((), ()))


def _params(*sem):
    return pltpu.CompilerParams(dimension_semantics=sem, vmem_limit_bytes=VMEM_LIMIT)


def _rms_mod(x, gain, shift, scale):
    h = x * lax.rsqrt(jnp.mean(x * x, axis=-1, keepdims=True) + NORM_EPS) * gain
    return h * (1.0 + scale) + shift


def _mod_kernel(c_ref, w_ref, b_ref, o_ref):
    c = c_ref[...]
    a = c / (1.0 + jnp.exp(-c))
    o_ref[0] = jnp.dot(a, w_ref[0], precision=lax.Precision.HIGHEST,
                       preferred_element_type=F32) + b_ref[0]


def _adaln_mod(c, c_ctx, ada_w, ada_b):
    depth, d, n_out = ada_w.shape
    b = c.shape[0]
    assert b + 1 <= SUBLANES
    rows = jnp.concatenate([c, c_ctx[None], jnp.zeros((SUBLANES - b - 1, d), F32)], axis=0)
    tn = n_out // 4
    return pl.pallas_call(
        _mod_kernel,
        grid=(depth, n_out // tn),
        in_specs=[pl.BlockSpec((SUBLANES, d), lambda i, j: (0, 0)),
                  pl.BlockSpec((1, d, tn), lambda i, j: (i, 0, j)),
                  pl.BlockSpec((1, 1, tn), lambda i, j: (i, 0, j))],
        out_specs=pl.BlockSpec((1, SUBLANES, tn), lambda i, j: (i, 0, j)),
        out_shape=jax.ShapeDtypeStruct((depth, SUBLANES, n_out), F32),
        compiler_params=_params("arbitrary", "arbitrary"),
        name="adaln_mod",
    )(rows, ada_w, ada_b.reshape(depth, 1, n_out))


def _qkv_kernel(x_ref, sh_ref, sc_ref, g_ref, w_ref, qg_ref, kg_ref, cos_ref, sin_ref, *out_refs,
                n_qk, rope):
    d = x_ref.shape[2]
    hb = _rms_mod(x_ref[0], g_ref[...], sh_ref[0], sc_ref[0]).astype(BF16)
    lane = lax.broadcasted_iota(jnp.int32, (1, LANES), 1)
    lane_lo = lane < HEAD_DIM
    lane_b4 = (lane & 16) == 0
    gains = (qg_ref[...], kg_ref[...])[2 - n_qk:]
    for t in range(n_qk):
        for j in range(0, d, 2 * LANES):
            acc = jnp.dot(hb, w_ref[:, t * d + j:t * d + j + 2 * LANES], preferred_element_type=F32)
            for half in range(2):
                blk = acc[:, half * LANES:(half + 1) * LANES]
                sq = blk * blk
                lo = jnp.sum(jnp.where(lane_lo, sq, 0.0), axis=-1, keepdims=True)
                hi = jnp.sum(jnp.where(lane_lo, 0.0, sq), axis=-1, keepdims=True)
                ms = jnp.where(lane_lo, lo, hi) * (1.0 / HEAD_DIM)
                y = blk * lax.rsqrt(ms + NORM_EPS) * gains[t]
                if rope:
                    rot = jnp.where(lane_b4, pltpu.roll(y, LANES - 16, 1), pltpu.roll(y, 16, 1))
                    y = y * cos_ref[...] + rot * sin_ref[...]
                c0 = j + half * LANES
                out_refs[t][0, :, c0:c0 + LANES] = y.astype(BF16)
    for j in range(0, d, 2 * LANES):
        acc = jnp.dot(hb, w_ref[:, n_qk * d + j:n_qk * d + j + 2 * LANES], preferred_element_type=F32)
        out_refs[n_qk][0, :, j:j + 2 * LANES] = acc.astype(BF16)


def _qkv_proj(x, shift, scale, gain, w, q_gain, k_gain, cos, sin, *, n_qk, rope, tm):
    b, l, d = x.shape
    n_out = n_qk + 1
    kern = functools.partial(_qkv_kernel, n_qk=n_qk, rope=rope)
    row = lambda bi, i: (bi, 0, 0)
    fix = lambda bi, i: (0, 0)
    return pl.pallas_call(
        kern,
        grid=(b, l // tm),
        in_specs=[pl.BlockSpec((1, tm, d), lambda bi, i: (bi, i, 0)),
                  pl.BlockSpec((1, 1, d), row), pl.BlockSpec((1, 1, d), row),
                  pl.BlockSpec((1, d), fix),
                  pl.BlockSpec((d, n_out * d), fix),
                  pl.BlockSpec((1, LANES), fix), pl.BlockSpec((1, LANES), fix),
                  pl.BlockSpec((tm, LANES), lambda bi, i: (i, 0)),
                  pl.BlockSpec((tm, LANES), lambda bi, i: (i, 0))],
        out_specs=[pl.BlockSpec((1, tm, d), lambda bi, i: (bi, i, 0))] * n_out,
        out_shape=[jax.ShapeDtypeStruct((b, l, d), BF16)] * n_out,
        compiler_params=_params("arbitrary", "arbitrary"),
        name="qkv_proj",
    )(x, shift, scale, gain, w, q_gain, k_gain, cos, sin)


def _rope_tables(n_tokens):
    rows = n_tokens // GRID_W
    row = jnp.repeat(jnp.arange(rows, dtype=F32), GRID_W)
    col = jnp.tile(jnp.arange(GRID_W, dtype=F32), rows)
    half = HEAD_DIM // 2
    inv_freq = ROPE_THETA ** (-jnp.arange(0, half, 2, dtype=F32) / half)
    ang_r = row[:, None] * inv_freq
    ang_c = col[:, None] * inv_freq
    ang = jnp.concatenate([ang_r, ang_r, ang_c, ang_c], axis=-1)
    ang = jnp.concatenate([ang, ang], axis=-1)
    sign = jnp.where((jnp.arange(LANES) & 16) == 0, -1.0, 1.0).astype(F32)
    return jnp.cos(ang), jnp.sin(ang) * sign


def _attn_kernel(q_ref, k_ref, vt_ref, lp_ref, sg_ref, o_ref, acc_ref, *, tk, lam_init):
    tq = q_ref.shape[1]
    n_chunks = k_ref.shape[1] // tk
    qt = (q_ref[0].astype(F32) * HEAD_DIM ** -0.5).T
    sub = lax.broadcasted_iota(jnp.int32, (V_DIM, 1), 0)
    first = sub < HEAD_DIM
    qz = jnp.concatenate([jnp.where(first, qt, 0.0), jnp.where(first, 0.0, qt)], axis=1).astype(BF16)
    m = jnp.full((1, 2 * tq), -jnp.inf, F32)
    l = jnp.zeros((1, 2 * tq), F32)
    for c in range(n_chunks):
        st = jnp.dot(k_ref[0, c * tk:(c + 1) * tk, :], qz, preferred_element_type=F32)
        m_new = jnp.maximum(m, jnp.max(st, axis=0, keepdims=True))
        alpha = jnp.exp(m - m_new)
        p = jnp.exp(st - m_new)
        l = alpha * l + jnp.sum(p, axis=0, keepdims=True)
        pv = jnp.dot(vt_ref[0, 0, :, c * tk:(c + 1) * tk], p.astype(BF16), preferred_element_type=F32)
        if c == 0:
            acc_ref[...] = pv
        else:
            acc_ref[...] = acc_ref[...] * alpha + pv
        m = m_new
    acc = acc_ref[...] / l
    lp = lp_ref[...]
    lam = (jnp.exp(jnp.sum(lp[0:1] * lp[1:2], axis=-1, keepdims=True))
           - jnp.exp(jnp.sum(lp[2:3] * lp[3:4], axis=-1, keepdims=True)) + lam_init)
    o = acc[:, :tq] - lam * acc[:, tq:]
    o = o * lax.rsqrt(jnp.mean(o * o, axis=0, keepdims=True) + NORM_EPS) * sg_ref[...] * (1.0 - lam_init)
    o_ref[0] = o.T.astype(BF16)


def _attn_chunk(n_keys):
    for tk in (1024, 768, 512, 256, 128):
        if n_keys % tk == 0:
            return tk
    raise ValueError(f"key count {n_keys} is not a multiple of {LANES}")


def _diff_attention(q, k_all, vt_all, lam_params, sub_gain, *, lam_init, tq):
    b, l, d = q.shape
    n_keys = k_all.shape[1]
    tk = _attn_chunk(n_keys)
    kern = functools.partial(_attn_kernel, tk=tk, lam_init=lam_init)
    return pl.pallas_call(
        kern,
        grid=(b, N_HEADS, l // tq),
        in_specs=[pl.BlockSpec((1, tq, V_DIM), lambda bi, h, i: (bi, i, h)),
                  pl.BlockSpec((1, n_keys, V_DIM), lambda bi, h, i: (bi, 0, h)),
                  pl.BlockSpec((1, 1, V_DIM, n_keys), lambda bi, h, i: (bi, h, 0, 0)),
                  pl.BlockSpec((4, HEAD_DIM), lambda bi, h, i: (0, 0)),
                  pl.BlockSpec((V_DIM, 1), lambda bi, h, i: (0, 0))],
        out_specs=pl.BlockSpec((1, tq, V_DIM), lambda bi, h, i: (bi, i, h)),
        out_shape=jax.ShapeDtypeStruct((b, l, d), BF16),
        scratch_shapes=[pltpu.VMEM((V_DIM, 2 * tq), F32)],
        compiler_params=_params("arbitrary", "arbitrary", "arbitrary"),
        name="diff_attention",
    )(q, k_all, vt_all, lam_params, sub_gain)


def _route(h2, rwh_ref, rwl_ref, rb_ref, carry_ref, ridx_ref, gcol_ref, cnt_ref, is_first):
    tm = h2.shape[0]
    hh = h2.astype(BF16)
    hl = (h2 - hh.astype(F32)).astype(BF16)
    rw2 = jnp.concatenate([rwh_ref[...], rwl_ref[...]], axis=0)
    part = lax.dot_general(rw2, hh, NT_DIMS, preferred_element_type=F32)
    logits = (part[:N_EXPERTS] + part[N_EXPERTS:]
              + lax.dot_general(rwh_ref[...], hl, NT_DIMS, preferred_element_type=F32) + rb_ref[...])
    groups = [logits[g * EXPERTS_PER_GROUP:(g + 1) * EXPERTS_PER_GROUP] for g in range(N_EXPERT_GROUPS)]
    top = groups[0]
    for g in range(1, N_EXPERT_GROUPS):
        top = jnp.maximum(top, groups[g])
    top = jnp.max(top, axis=0, keepdims=True)
    sub = lax.broadcasted_iota(jnp.int32, (EXPERTS_PER_GROUP, tm), 0)
    best = None
    for g in range(N_EXPERT_GROUPS):
        ex = jnp.exp(groups[g] - top)
        v1 = jnp.max(ex, axis=0, keepdims=True)
        i1 = jnp.min(jnp.where(ex == v1, sub, EXPERTS_PER_GROUP), axis=0, keepdims=True)
        rest = jnp.where(sub == i1, -1.0, ex)
        v2 = jnp.max(rest, axis=0, keepdims=True)
        i2 = jnp.min(jnp.where(rest == v2, sub, EXPERTS_PER_GROUP), axis=0, keepdims=True)
        cand = (v1 + v2, v1, v2, i1 + g * EXPERTS_PER_GROUP, i2 + g * EXPERTS_PER_GROUP)
        if best is None:
            best = cand
        else:
            better = cand[0] > best[0]
            best = tuple(jnp.where(better, new, old) for new, old in zip(cand, best))
    _, v1, v2, e0, e1 = best
    gate0 = v1 / (v1 + v2)
    gate1 = v2 / (v1 + v2)

    @pl.when(is_first)
    def _():
        carry_ref[...] = jnp.zeros_like(carry_ref)

    erow = lax.broadcasted_iota(jnp.int32, (N_EXPERTS, tm), 0)
    oh0 = erow == e0
    oh1 = erow == e1
    chosen = jnp.where(oh0 | oh1, 1.0, 0.0)
    before = (lax.broadcasted_iota(jnp.int32, (tm, tm), 0)
              < lax.broadcasted_iota(jnp.int32, (tm, tm), 1)).astype(BF16)
    prior = carry_ref[:, 0:1] + jnp.dot(chosen.astype(BF16), before, preferred_element_type=F32)
    r0 = jnp.sum(jnp.where(oh0, prior, 0.0), axis=0, keepdims=True).astype(jnp.int32)
    r1 = jnp.sum(jnp.where(oh1, prior, 0.0), axis=0, keepdims=True).astype(jnp.int32)
    carry_ref[...] = carry_ref[...] + jnp.sum(chosen, axis=1, keepdims=True)
    cnt_ref[...] = carry_ref[...]
    rid = lax.broadcasted_iota(jnp.int32, (SUBLANES, tm), 0)
    ridx_ref[...] = jnp.where(rid == 0, e0, jnp.where(rid == 1, e1, jnp.where(rid == 2, r0, jnp.where(rid == 3, r1, 0))))
    gid = lax.broadcasted_iota(jnp.int32, (LANES, tm), 0)
    gcol_ref[...] = jnp.where(gid == 0, gate0, jnp.where(gid == 1, gate1, 0.0)).T


def _tail(y, x_ref, g1_ref, sh2_ref, s2_ref, n2_ref, rwh_ref, rwl_ref, rb_ref,
          x_out, h2_out, ridx_ref, gcol_ref, cnt_ref, carry_ref):
    x1 = x_ref[0] + g1_ref[0] * y
    x_out[0] = x1
    h2 = _rms_mod(x1, n2_ref[...], sh2_ref[0], s2_ref[0])
    h2_out[0] = h2
    is_first = (pl.program_id(0) == 0) & (pl.program_id(1) == 0)
    _route(h2, rwh_ref, rwl_ref, rb_ref, carry_ref, ridx_ref, gcol_ref, cnt_ref, is_first)


def _attn_tail_kernel(a_ref, wo_ref, *rest):
    y = jnp.dot(a_ref[0], wo_ref[...], preferred_element_type=F32)
    _tail(y, *rest)


def _pool_tail_kernel(u_ref, up_ref, un_ref, wg_ref, cs_ref, wo_ref, *rest, seq_len):
    *tail_refs, ubuf = rest
    tm = u_ref.shape[1]
    i = pl.program_id(1)
    u = u_ref[0]
    ubuf[0:POOL_HALO] = jnp.where(i > 0, up_ref[0], 0.0)
    ubuf[POOL_HALO:POOL_HALO + tm] = u
    ubuf[POOL_HALO + tm:2 * POOL_HALO + tm] = jnp.where(i < pl.num_programs(1) - 1, un_ref[0], 0.0)
    pos = i * tm + lax.broadcasted_iota(jnp.int32, (tm, 1), 0)
    gd = wg_ref.shape[1]
    outs = []
    for g, win in enumerate(POOL_WINDOWS):
        half = win // 2
        cols = slice(g * gd, (g + 1) * gd)
        s = ubuf[POOL_HALO - half:POOL_HALO - half + tm, cols]
        for j in range(1 - half, half):
            s = s + ubuf[POOL_HALO + j:POOL_HALO + j + tm, cols]
        cnt = (jnp.minimum(pos + half, seq_len) - jnp.maximum(pos - half, 0)).astype(F32)
        dlt = (s / cnt - u[:, cols]).astype(BF16)
        outs.append(jnp.dot(dlt, wg_ref[g], preferred_element_type=F32))
    z = (jnp.concatenate(outs, axis=-1) * cs_ref[...]).astype(BF16)
    y = jnp.dot(z, wo_ref[...], preferred_element_type=F32)
    _tail(y, *tail_refs)


def _mixer_tail(front_args, front_specs, kern, x, g1, sh2, s2, n2g, rwh, rwl, rb, *, tm, scratch=()):
    b, l, d = x.shape
    nt = l // tm
    n = b * l
    row = lambda bi, i: (bi, 0, 0)
    fix = lambda bi, i: (0, 0)
    tile = lambda bi, i: (bi, i, 0)
    in_specs = list(front_specs) + [
        pl.BlockSpec((1, tm, d), tile),
        pl.BlockSpec((1, 1, d), row), pl.BlockSpec((1, 1, d), row), pl.BlockSpec((1, 1, d), row),
        pl.BlockSpec((1, d), fix),
        pl.BlockSpec((N_EXPERTS, d), fix), pl.BlockSpec((N_EXPERTS, d), fix),
        pl.BlockSpec((N_EXPERTS, 1), fix)]
    out_specs = [pl.BlockSpec((1, tm, d), tile), pl.BlockSpec((1, tm, d), tile),
                 pl.BlockSpec((SUBLANES, tm), lambda bi, i: (0, bi * nt + i)),
                 pl.BlockSpec((tm, LANES), lambda bi, i: (bi * nt + i, 0)),
                 pl.BlockSpec((N_EXPERTS, LANES), fix)]
    out_shape = [jax.ShapeDtypeStruct((b, l, d), F32), jax.ShapeDtypeStruct((b, l, d), F32),
                 jax.ShapeDtypeStruct((SUBLANES, n), jnp.int32), jax.ShapeDtypeStruct((n, LANES), F32),
                 jax.ShapeDtypeStruct((N_EXPERTS, LANES), F32)]
    return pl.pallas_call(
        kern,
        grid=(b, nt),
        in_specs=in_specs, out_specs=out_specs, out_shape=out_shape,
        scratch_shapes=[pltpu.VMEM((N_EXPERTS, LANES), F32)] + list(scratch),
        compiler_params=_params("arbitrary", "arbitrary"),
        name="mixer_tail",
    )(*front_args, x, g1, sh2, s2, n2g, rwh, rwl, rb)


def _slot_of(ps_ref, ridx_ref, t, k):
    return ps_ref[ridx_ref[k, t]] + ridx_ref[TOP_K + k, t]


def _scatter_kernel(ps_ref, ridx_ref, h2_hbm, xs_in, xs_hbm, sem, *, ts):
    del xs_in
    base = pl.program_id(0) * ts

    def row_copy(t, k):
        return pltpu.make_async_copy(h2_hbm.at[pl.ds(base + t, 1)],
                                     xs_hbm.at[pl.ds(_slot_of(ps_ref, ridx_ref, t, k), 1)], sem)

    def issue(t, carry):
        for k in range(TOP_K):
            row_copy(t, k).start()
        return carry

    def drain(t, carry):
        for k in range(TOP_K):
            row_copy(t, k).wait()
        return carry

    lax.fori_loop(0, ts, issue, 0)
    lax.fori_loop(0, ts, drain, 0)


def _dispatch(pad_start, ridx, h2_flat, n_slots, *, ts):
    n, d = h2_flat.shape
    kern = functools.partial(_scatter_kernel, ts=ts)
    return pl.pallas_call(
        kern,
        grid_spec=pltpu.PrefetchScalarGridSpec(
            num_scalar_prefetch=1,
            grid=(n // ts,),
            in_specs=[pl.BlockSpec((SUBLANES, ts), lambda i, ps: (0, i), memory_space=pltpu.SMEM),
                      pl.BlockSpec(memory_space=pl.ANY),
                      pl.BlockSpec(memory_space=pl.ANY)],
            out_specs=pl.BlockSpec(memory_space=pl.ANY),
            scratch_shapes=[pltpu.SemaphoreType.DMA(())]),
        out_shape=jax.ShapeDtypeStruct((n_slots, d), F32),
        input_output_aliases={3: 0},
        compiler_params=_params("arbitrary"),
        name="moe_dispatch",
    )(pad_start, ridx, h2_flat, jnp.zeros((n_slots, d), F32))


def _ffn_kernel(be_ref, nu_ref, xs_ref, wg_ref, wu_ref, wd_ref, ys_ref):
    del be_ref

    @pl.when(pl.program_id(0) < nu_ref[0])
    def _():
        xb = xs_ref[...].astype(BF16)
        g = jnp.dot(xb, wg_ref[0].astype(BF16), preferred_element_type=F32)
        u = jnp.dot(xb, wu_ref[0].astype(BF16), preferred_element_type=F32)
        a = (g / (1.0 + jnp.exp(-g)) * u).astype(BF16)
        ys_ref[...] = jnp.dot(a, wd_ref[0].astype(BF16), preferred_element_type=F32)

    @pl.when(pl.program_id(0) >= nu_ref[0])
    def _():
        ys_ref[...] = jnp.zeros_like(ys_ref)


def _expert_ffn(block_e, n_used, xs, w_gate, w_up, w_down):
    n_slots, d = xs.shape
    de = w_gate.shape[2]
    return pl.pallas_call(
        _ffn_kernel,
        grid_spec=pltpu.PrefetchScalarGridSpec(
            num_scalar_prefetch=2,
            grid=(n_slots // SLOT_ROWS,),
            in_specs=[pl.BlockSpec((SLOT_ROWS, d), lambda j, be, nu: (j, 0)),
                      pl.BlockSpec((1, d, de), lambda j, be, nu: (be[j], 0, 0)),
                      pl.BlockSpec((1, d, de), lambda j, be, nu: (be[j], 0, 0)),
                      pl.BlockSpec((1, de, d), lambda j, be, nu: (be[j], 0, 0))],
            out_specs=pl.BlockSpec((SLOT_ROWS, d), lambda j, be, nu: (j, 0))),
        out_shape=jax.ShapeDtypeStruct((n_slots, d), F32),
        compiler_params=_params("arbitrary"),
        name="expert_ffn",
    )(block_e, n_used, xs, w_gate, w_up, w_down)


def _combine_kernel(ps_ref, ridx_ref, ys_hbm, gcol_ref, x_ref, g2_ref, *rest, tc, pool_in):
    if pool_in:
        sh_ref, sc_ref, n1_ref, wi_ref, x_out, u_out, buf0, buf1, sem = rest
    else:
        x_out, buf0, buf1, sem = rest
    bufs = (buf0, buf1)

    def row_copy(t, k):
        return pltpu.make_async_copy(ys_hbm.at[pl.ds(_slot_of(ps_ref, ridx_ref, t, k), 1)],
                                     bufs[k].at[pl.ds(t, 1)], sem)

    def issue(t, carry):
        for k in range(TOP_K):
            row_copy(t, k).start()
        return carry

    def drain(t, carry):
        for k in range(TOP_K):
            row_copy(t, k).wait()
        return carry

    lax.fori_loop(0, tc, issue, 0)
    lax.fori_loop(0, tc, drain, 0)
    gc = gcol_ref[...]
    out = gc[:, 0:1] * buf0[...] + gc[:, 1:2] * buf1[...]
    x2 = x_ref[0] + g2_ref[0] * out
    x_out[0] = x2
    if pool_in:
        hb = _rms_mod(x2, n1_ref[...], sh_ref[0], sc_ref[0]).astype(BF16)
        u_out[0] = jnp.dot(hb, wi_ref[...], preferred_element_type=F32)


def _combine(pad_start, ridx, ys, gcol, x, g2, pool_args=None, *, tc):
    b, l, d = x.shape
    nt = l // tc
    pool_in = pool_args is not None
    kern = functools.partial(_combine_kernel, tc=tc, pool_in=pool_in)
    row = lambda bi, i, ps: (bi, 0, 0)
    fix = lambda bi, i, ps: (0, 0)
    tile = lambda bi, i, ps: (bi, i, 0)
    in_specs = [pl.BlockSpec((SUBLANES, tc), lambda bi, i, ps: (0, bi * nt + i), memory_space=pltpu.SMEM),
                pl.BlockSpec(memory_space=pl.ANY),
                pl.BlockSpec((tc, LANES), lambda bi, i, ps: (bi * nt + i, 0)),
                pl.BlockSpec((1, tc, d), tile),
                pl.BlockSpec((1, 1, d), row)]
    out_specs = [pl.BlockSpec((1, tc, d), tile)]
    out_shape = [jax.ShapeDtypeStruct((b, l, d), F32)]
    args = [pad_start, ridx, ys, gcol, x, g2]
    if pool_in:
        in_specs += [pl.BlockSpec((1, 1, d), row), pl.BlockSpec((1, 1, d), row),
                     pl.BlockSpec((1, d), fix), pl.BlockSpec((d, d), fix)]
        out_specs.append(pl.BlockSpec((1, tc, d), tile))
        out_shape.append(jax.ShapeDtypeStruct((b, l, d), F32))
        args += list(pool_args)
    return pl.pallas_call(
        kern,
        grid_spec=pltpu.PrefetchScalarGridSpec(
            num_scalar_prefetch=1,
            grid=(b, nt),
            in_specs=in_specs, out_specs=out_specs,
            scratch_shapes=[pltpu.VMEM((tc, d), F32), pltpu.VMEM((tc, d), F32),
                            pltpu.SemaphoreType.DMA(())]),
        out_shape=out_shape,
        compiler_params=_params("arbitrary", "arbitrary"),
        name="moe_combine",
    )(*args)


def _moe(h2, ridx, gcol, counts, x1, g2, w_gate, w_up, w_down, pool_args=None):
    b, l, d = x1.shape
    n = b * l
    n_blocks = (n * TOP_K) // SLOT_ROWS + N_EXPERTS
    cnt = counts[:, 0].astype(jnp.int32)
    padded = (cnt + SLOT_ROWS - 1) // SLOT_ROWS * SLOT_ROWS
    pad_end = jnp.cumsum(padded)
    pad_start = (pad_end - padded).astype(jnp.int32)
    block_start = jnp.arange(n_blocks, dtype=jnp.int32) * SLOT_ROWS
    block_e = jnp.minimum(jnp.sum(pad_end[None, :] <= block_start[:, None], axis=1), N_EXPERTS - 1).astype(jnp.int32)
    n_used = (pad_end[-1:] // SLOT_ROWS).astype(jnp.int32)
    xs = _dispatch(pad_start, ridx, h2.reshape(n, d), n_blocks * SLOT_ROWS, ts=512)
    ys = _expert_ffn(block_e, n_used, xs, w_gate, w_up, w_down)
    return _combine(pad_start, ridx, ys, gcol, x1, g2, pool_args, tc=256)


def kernel(x, c, ctx, c_ctx, ada_w, ada_b, norm1_g, norm2_g, attn_w_in, attn_w_out, attn_q_gain, attn_k_gain,
           attn_lq1, attn_lk1, attn_lq2, attn_lk2, attn_sub_gain, pool_w_in, pool_w_group, pool_scale, pool_w_out,
           router_w, router_b, moe_w_gate, moe_w_up, moe_w_down):
    b, l, d = x.shape
    n_ctx = ctx.shape[1]
    depth = ada_w.shape[0]
    assert depth == 2 and d == N_HEADS * V_DIM
    tm = 512

    mod = _adaln_mod(c, c_ctx, ada_w, ada_b)
    mods = [[mod[i, :b, None, j * d:(j + 1) * d] for j in range(N_MOD)] for i in range(depth)]
    mod_ctx = [jnp.broadcast_to(mod[0, b, j * d:(j + 1) * d], (b, 1, d)) for j in range(2)]

    rwt = router_w.T
    rwh = rwt.astype(BF16)
    rwl = (rwt - rwh.astype(F32)).astype(BF16)
    rb = router_b.reshape(N_EXPERTS, 1)

    sh1, s1, g1, sh2, s2, g2 = mods[0]
    cos, sin = _rope_tables(l)
    gain2 = lambda g: jnp.concatenate([g, g])[None]
    qg, kg = gain2(attn_q_gain[0]), gain2(attn_k_gain[0])
    w_in = attn_w_in[0].astype(BF16)
    q, k, v = _qkv_proj(x, sh1, s1, norm1_g[0][None], w_in, qg, kg, cos, sin, n_qk=2, rope=True, tm=tm)
    kc, vc = _qkv_proj(ctx, mod_ctx[0], mod_ctx[1], norm1_g[0][None], w_in[:, d:], qg, kg,
                       cos[:n_ctx], sin[:n_ctx], n_qk=1, rope=False, tm=n_ctx)
    k_all = jnp.concatenate([k, kc], axis=1)
    v_all = jnp.concatenate([v, vc], axis=1)
    vt_all = v_all.reshape(b, l + n_ctx, N_HEADS, V_DIM).transpose(0, 2, 3, 1)
    lam_init = 0.8 - 0.6 * math.exp(-0.3 * 0)
    lam_params = jnp.stack([attn_lq1[0], attn_lk1[0], attn_lq2[0], attn_lk2[0]])
    o = _diff_attention(q, k_all, vt_all, lam_params, attn_sub_gain[0][:, None], lam_init=lam_init, tq=256)

    fix = lambda bi, i: (0, 0)
    x1, h2, ridx, gcol, counts = _mixer_tail(
        (o, attn_w_out[0].astype(BF16)),
        (pl.BlockSpec((1, tm, d), lambda bi, i: (bi, i, 0)), pl.BlockSpec((d, d), fix)),
        _attn_tail_kernel, x, g1, sh2, s2, norm2_g[0][None], rwh, rwl, rb, tm=tm)

    sh1b, s1b, g1b, sh2b, s2b, g2b = mods[1]
    x2, u = _moe(h2, ridx, gcol, counts, x1, g2, moe_w_gate[0], moe_w_up[0], moe_w_down[0],
                 pool_args=(sh1b, s1b, norm1_g[1][None], pool_w_in[0].astype(BF16)))
    gd = pool_w_group.shape[2]
    nh = tm // POOL_HALO
    front_specs = (
        pl.BlockSpec((1, tm, d), lambda bi, i: (bi, i, 0)),
        pl.BlockSpec((1, POOL_HALO, d), lambda bi, i: (bi, jnp.maximum(i * nh - 1, 0), 0)),
        pl.BlockSpec((1, POOL_HALO, d), lambda bi, i: (bi, jnp.minimum((i + 1) * nh, l // POOL_HALO - 1), 0)),
        pl.BlockSpec((len(POOL_WINDOWS), gd, gd), lambda bi, i: (0, 0, 0)),
        pl.BlockSpec((1, d), fix),
        pl.BlockSpec((d, d), fix))
    x3, h2b, ridx_b, gcol_b, counts_b = _mixer_tail(
        (u, u, u, pool_w_group[0].astype(BF16), pool_scale[0][None], pool_w_out[0].astype(BF16)),
        front_specs, functools.partial(_pool_tail_kernel, seq_len=l),
        x2, g1b, sh2b, s2b, norm2_g[1][None], rwh, rwl, rb, tm=tm,
        scratch=[pltpu.VMEM((tm + 2 * POOL_HALO, d), F32)])
    (out,) = _moe(h2b, ridx_b, gcol_b, counts_b, x3, g2b, moe_w_gate[1], moe_w_up[1], moe_w_down[1])
    return out
```

```python
import functools
import math

import jax
import jax.numpy as jnp
from jax import lax
from jax.experimental import pallas as pl
from jax.experimental.pallas import tpu as pltpu

F32 = jnp.float32
BF16 = jnp.bfloat16

LANES = 128
SUBLANES = 8
N_HEADS = 8
HEAD_DIM = 64
V_DIM = 2 * HEAD_DIM
V_ROWS = V_DIM + 16
GRID_W = 64
ROPE_THETA = 10000.0
NORM_EPS = 1e-6
N_MOD = 6
POOL_WINDOWS = (2, 4, 8, 16)
POOL_HALO = max(POOL_WINDOWS) // 2
N_EXPERTS = 32
N_EXPERT_GROUPS = 4
EXPERTS_PER_GROUP = N_EXPERTS // N_EXPERT_GROUPS
TOP_K = 2
SLOT_ROWS = 256
ATTN_GROUP_CHUNKS = 11
VMEM_LIMIT = 48 * 1024 * 1024
NT_DIMS = (((1,), (1,)), ((), ()))


def _params(*sem):
    return pltpu.CompilerParams(dimension_semantics=sem, vmem_limit_bytes=VMEM_LIMIT)


def _rms_mod(x, gain, shift, scale):
    h = x * lax.rsqrt(jnp.mean(x * x, axis=-1, keepdims=True) + NORM_EPS) * gain
    return h * (1.0 + scale) + shift


def _mod_kernel(c_ref, w_ref, b_ref, o_ref):
    c = c_ref[...]
    a = c / (1.0 + jnp.exp(-c))
    o_ref[0] = jnp.dot(a, w_ref[0], precision=lax.Precision.HIGHEST,
                       preferred_element_type=F32) + b_ref[0]


def _adaln_mod(c, c_ctx, ada_w, ada_b):
    depth, d, n_out = ada_w.shape
    b = c.shape[0]
    assert b + 1 <= SUBLANES
    rows = jnp.concatenate([c, c_ctx[None], jnp.zeros((SUBLANES - b - 1, d), F32)], axis=0)
    tn = n_out // 4
    return pl.pallas_call(
        _mod_kernel,
        grid=(depth, n_out // tn),
        in_specs=[pl.BlockSpec((SUBLANES, d), lambda i, j: (0, 0)),
                  pl.BlockSpec((1, d, tn), lambda i, j: (i, 0, j)),
                  pl.BlockSpec((1, 1, tn), lambda i, j: (i, 0, j))],
        out_specs=pl.BlockSpec((1, SUBLANES, tn), lambda i, j: (i, 0, j)),
        out_shape=jax.ShapeDtypeStruct((depth, SUBLANES, n_out), F32),
        compiler_params=_params("arbitrary", "arbitrary"),
        name="adaln_mod",
    )(rows, ada_w, ada_b.reshape(depth, 1, n_out))


def _qkv_kernel(x_ref, sh_ref, sc_ref, g_ref, w_ref, qg_ref, kg_ref, cos_ref, sin_ref, *out_refs,
                n_qk, rope):
    d = x_ref.shape[2]
    hb = _rms_mod(x_ref[0], g_ref[...], sh_ref[0], sc_ref[0]).astype(BF16)
    lane = lax.broadcasted_iota(jnp.int32, (1, LANES), 1)
    lane_lo = lane < HEAD_DIM
    lane_b4 = (lane & 16) == 0
    gains = (qg_ref[...], kg_ref[...])[2 - n_qk:]
    for t in range(n_qk):
        for j in range(0, d, 2 * LANES):
            acc = jnp.dot(hb, w_ref[:, t * d + j:t * d + j + 2 * LANES], preferred_element_type=F32)
            for half in range(2):
                blk = acc[:, half * LANES:(half + 1) * LANES]
                sq = blk * blk
                lo = jnp.sum(jnp.where(lane_lo, sq, 0.0), axis=-1, keepdims=True)
                hi = jnp.sum(jnp.where(lane_lo, 0.0, sq), axis=-1, keepdims=True)
                ms = jnp.where(lane_lo, lo, hi) * (1.0 / HEAD_DIM)
                y = blk * lax.rsqrt(ms + NORM_EPS) * gains[t]
                if rope:
                    rot = jnp.where(lane_b4, pltpu.roll(y, LANES - 16, 1), pltpu.roll(y, 16, 1))
                    y = y * cos_ref[...] + rot * sin_ref[...]
                c0 = j + half * LANES
                out_refs[t][0, :, c0:c0 + LANES] = y.astype(BF16)
    for j in range(0, d, 2 * LANES):
        acc = jnp.dot(hb, w_ref[:, n_qk * d + j:n_qk * d + j + 2 * LANES], preferred_element_type=F32)
        out_refs[n_qk][0, :, j:j + 2 * LANES] = acc.astype(BF16)


def _qkv_proj(x, shift, scale, gain, w, q_gain, k_gain, cos, sin, *, n_qk, rope, tm):
    b, l, d = x.shape
    n_out = n_qk + 1
    kern = functools.partial(_qkv_kernel, n_qk=n_qk, rope=rope)
    row = lambda bi, i: (bi, 0, 0)
    fix = lambda bi, i: (0, 0)
    return pl.pallas_call(
        kern,
        grid=(b, l // tm),
        in_specs=[pl.BlockSpec((1, tm, d), lambda bi, i: (bi, i, 0)),
                  pl.BlockSpec((1, 1, d), row), pl.BlockSpec((1, 1, d), row),
                  pl.BlockSpec((1, d), fix),
                  pl.BlockSpec((d, n_out * d), fix),
                  pl.BlockSpec((1, LANES), fix), pl.BlockSpec((1, LANES), fix),
                  pl.BlockSpec((tm, LANES), lambda bi, i: (i, 0)),
                  pl.BlockSpec((tm, LANES), lambda bi, i: (i, 0))],
        out_specs=[pl.BlockSpec((1, tm, d), lambda bi, i: (bi, i, 0))] * n_out,
        out_shape=[jax.ShapeDtypeStruct((b, l, d), BF16)] * n_out,
        compiler_params=_params("arbitrary", "arbitrary"),
        name="qkv_proj",
    )(x, shift, scale, gain, w, q_gain, k_gain, cos, sin)


def _rope_tables(n_tokens):
    rows = n_tokens // GRID_W
    row = jnp.repeat(jnp.arange(rows, dtype=F32), GRID_W)
    col = jnp.tile(jnp.arange(GRID_W, dtype=F32), rows)
    half = HEAD_DIM // 2
    inv_freq = ROPE_THETA ** (-jnp.arange(0, half, 2, dtype=F32) / half)
    ang_r = row[:, None] * inv_freq
    ang_c = col[:, None] * inv_freq
    ang = jnp.concatenate([ang_r, ang_r, ang_c, ang_c], axis=-1)
    ang = jnp.concatenate([ang, ang], axis=-1)
    sign = jnp.where((jnp.arange(LANES) & 16) == 0, -1.0, 1.0).astype(F32)
    return jnp.cos(ang), jnp.sin(ang) * sign


def _attn_kernel(q_ref, k_ref, vt_ref, lp_ref, sg_ref, o_ref, s_ref, *, tk, group, lam_init):
    tq = q_ref.shape[1]
    n_chunks = k_ref.shape[1] // tk
    qt = (q_ref[0].astype(F32) * (HEAD_DIM ** -0.5 * math.log2(math.e))).T
    sub = lax.broadcasted_iota(jnp.int32, (V_DIM, 1), 0)
    first = sub < HEAD_DIM
    qz = jnp.concatenate([jnp.where(first, qt, 0.0), jnp.where(first, 0.0, qt)], axis=1).astype(BF16)
    def score_chunk(c, m_grp):
        st = jnp.dot(k_ref[0, c * tk:(c + 1) * tk, :], qz, preferred_element_type=F32).astype(BF16)
        s_ref[c * tk:(c + 1) * tk, :] = st
        mc = jnp.max(st, axis=0, keepdims=True)
        return mc if m_grp is None else jnp.maximum(m_grp, mc)

    def value_chunk(c, m_ref, part):
        p = jnp.exp2(s_ref[c * tk:(c + 1) * tk, :] - m_ref)
        pv = jnp.dot(vt_ref[0, 0, :, c * tk:(c + 1) * tk], p, preferred_element_type=F32)
        return pv if part is None else part + pv

    groups = [list(range(g0, min(g0 + group, n_chunks))) for g0 in range(0, n_chunks, group)]
    m_grp = None
    for c in groups[0]:
        m_grp = score_chunk(c, m_grp)
    m = None
    acc = None
    for gi, cur in enumerate(groups):
        nxt = groups[gi + 1] if gi + 1 < len(groups) else []
        m_new = m_grp if m is None else jnp.maximum(m, m_grp)
        m_grp = None
        part = None
        for i in range(max(len(cur), len(nxt))):
            if i < len(nxt):
                m_grp = score_chunk(nxt[i], m_grp)
            if i < len(cur):
                part = value_chunk(cur[i], m_new, part)
        acc = part if acc is None else acc * jnp.exp2(m.astype(F32) - m_new.astype(F32)) + part
        m = m_new
    acc = acc[:V_DIM] / acc[V_DIM:V_DIM + 1]
    lp = lp_ref[...]
    lam = (jnp.exp(jnp.sum(lp[0:1] * lp[1:2], axis=-1, keepdims=True))
           - jnp.exp(jnp.sum(lp[2:3] * lp[3:4], axis=-1, keepdims=True)) + lam_init)
    o = acc[:, :tq] - lam * acc[:, tq:]
    o = o * lax.rsqrt(jnp.mean(o * o, axis=0, keepdims=True) + NORM_EPS) * sg_ref[...] * (1.0 - lam_init)
    o_ref[0] = o.T.astype(BF16)


def _attn_chunk(n_keys):
    for tk in (256, 128):
        if n_keys % tk == 0:
            return tk
    raise ValueError(f"key count {n_keys} is not a multiple of {LANES}")


def _diff_attention(q, k_all, vt_all, lam_params, sub_gain, *, lam_init, tq):
    b, l, d = q.shape
    n_keys = k_all.shape[1]
    tk = _attn_chunk(n_keys)
    kern = functools.partial(_attn_kernel, tk=tk, group=ATTN_GROUP_CHUNKS, lam_init=lam_init)
    return pl.pallas_call(
        kern,
        grid=(b, N_HEADS, l // tq),
        in_specs=[pl.BlockSpec((1, tq, V_DIM), lambda bi, h, i: (bi, i, h)),
                  pl.BlockSpec((1, n_keys, V_DIM), lambda bi, h, i: (bi, 0, h)),
                  pl.BlockSpec((1, 1, V_ROWS, n_keys), lambda bi, h, i: (bi, h, 0, 0)),
                  pl.BlockSpec((4, HEAD_DIM), lambda bi, h, i: (0, 0)),
                  pl.BlockSpec((V_DIM, 1), lambda bi, h, i: (0, 0))],
        out_specs=pl.BlockSpec((1, tq, V_DIM), lambda bi, h, i: (bi, i, h)),
        out_shape=jax.ShapeDtypeStruct((b, l, d), BF16),
        scratch_shapes=[pltpu.VMEM((n_keys, 2 * tq), BF16)],
        compiler_params=_params("arbitrary", "arbitrary", "arbitrary"),
        name="diff_attention",
    )(q, k_all, vt_all, lam_params, sub_gain)


def _route(h2, rwh_ref, rwl_ref, rb_ref, carry_ref, ridx_ref, gcol_ref, cnt_ref, is_first):
    tm = h2.shape[0]
    hh = h2.astype(BF16)
    hl = (h2 - hh.astype(F32)).astype(BF16)
    rw2 = jnp.concatenate([rwh_ref[...], rwl_ref[...]], axis=0)
    part = lax.dot_general(rw2, hh, NT_DIMS, preferred_element_type=F32)
    logits = (part[:N_EXPERTS] + part[N_EXPERTS:]
              + lax.dot_general(rwh_ref[...], hl, NT_DIMS, preferred_element_type=F32) + rb_ref[...])
    groups = [logits[g * EXPERTS_PER_GROUP:(g + 1) * EXPERTS_PER_GROUP] for g in range(N_EXPERT_GROUPS)]
    top = groups[0]
    for g in range(1, N_EXPERT_GROUPS):
        top = jnp.maximum(top, groups[g])
    top = jnp.max(top, axis=0, keepdims=True)
    sub = lax.broadcasted_iota(jnp.int32, (EXPERTS_PER_GROUP, tm), 0)
    best = None
    for g in range(N_EXPERT_GROUPS):
        ex = jnp.exp(groups[g] - top)
        v1 = jnp.max(ex, axis=0, keepdims=True)
        i1 = jnp.min(jnp.where(ex == v1, sub, EXPERTS_PER_GROUP), axis=0, keepdims=True)
        rest = jnp.where(sub == i1, -1.0, ex)
        v2 = jnp.max(rest, axis=0, keepdims=True)
        i2 = jnp.min(jnp.where(rest == v2, sub, EXPERTS_PER_GROUP), axis=0, keepdims=True)
        cand = (v1 + v2, v1, v2, i1 + g * EXPERTS_PER_GROUP, i2 + g * EXPERTS_PER_GROUP)
        if best is None:
            best = cand
        else:
            better = cand[0] > best[0]
            best = tuple(jnp.where(better, new, old) for new, old in zip(cand, best))
    _, v1, v2, e0, e1 = best
    gate0 = v1 / (v1 + v2)
    gate1 = v2 / (v1 + v2)

    @pl.when(is_first)
    def _():
        carry_ref[...] = jnp.zeros_like(carry_ref)

    erow = lax.broadcasted_iota(jnp.int32, (N_EXPERTS, tm), 0)
    oh0 = erow == e0
    oh1 = erow == e1
    chosen = jnp.where(oh0 | oh1, 1.0, 0.0)
    before = (lax.broadcasted_iota(jnp.int32, (tm, tm), 0)
              < lax.broadcasted_iota(jnp.int32, (tm, tm), 1)).astype(BF16)
    prior = carry_ref[:, 0:1] + jnp.dot(chosen.astype(BF16), before, preferred_element_type=F32)
    r0 = jnp.sum(jnp.where(oh0, prior, 0.0), axis=0, keepdims=True).astype(jnp.int32)
    r1 = jnp.sum(jnp.where(oh1, prior, 0.0), axis=0, keepdims=True).astype(jnp.int32)
    carry_ref[...] = carry_ref[...] + jnp.sum(chosen, axis=1, keepdims=True)
    cnt_ref[...] = carry_ref[...]
    rid = lax.broadcasted_iota(jnp.int32, (SUBLANES, tm), 0)
    ridx_ref[...] = jnp.where(rid == 0, e0, jnp.where(rid == 1, e1, jnp.where(rid == 2, r0, jnp.where(rid == 3, r1, 0))))
    gid = lax.broadcasted_iota(jnp.int32, (LANES, tm), 0)
    gcol_ref[...] = jnp.where(gid == 0, gate0, jnp.where(gid == 1, gate1, 0.0)).T


def _tail(y, x_ref, g1_ref, sh2_ref, s2_ref, n2_ref, rwh_ref, rwl_ref, rb_ref,
          x_out, h2_out, ridx_ref, gcol_ref, cnt_ref, carry_ref):
    x1 = x_ref[0] + g1_ref[0] * y
    x_out[0] = x1
    h2 = _rms_mod(x1, n2_ref[...], sh2_ref[0], s2_ref[0])
    h2_out[0] = h2
    is_first = (pl.program_id(0) == 0) & (pl.program_id(1) == 0)
    _route(h2, rwh_ref, rwl_ref, rb_ref, carry_ref, ridx_ref, gcol_ref, cnt_ref, is_first)


def _attn_tail_kernel(a_ref, wo_ref, *rest):
    y = jnp.dot(a_ref[0], wo_ref[...], preferred_element_type=F32)
    _tail(y, *rest)


def _pool_tail_kernel(u_ref, up_ref, un_ref, wg_ref, cs_ref, wo_ref, *rest, seq_len):
    *tail_refs, ubuf = rest
    tm = u_ref.shape[1]
    i = pl.program_id(1)
    u = u_ref[0]
    ubuf[0:POOL_HALO] = jnp.where(i > 0, up_ref[0], 0.0)
    ubuf[POOL_HALO:POOL_HALO + tm] = u
    ubuf[POOL_HALO + tm:2 * POOL_HALO + tm] = jnp.where(i < pl.num_programs(1) - 1, un_ref[0], 0.0)
    pos = i * tm + lax.broadcasted_iota(jnp.int32, (tm, 1), 0)
    gd = wg_ref.shape[1]
    outs = []
    for g, win in enumerate(POOL_WINDOWS):
        half = win // 2
        cols = slice(g * gd, (g + 1) * gd)
        s = ubuf[POOL_HALO - half:POOL_HALO - half + tm, cols]
        for j in range(1 - half, half):
            s = s + ubuf[POOL_HALO + j:POOL_HALO + j + tm, cols]
        cnt = (jnp.minimum(pos + half, seq_len) - jnp.maximum(pos - half, 0)).astype(F32)
        dlt = (s / cnt - u[:, cols]).astype(BF16)
        outs.append(jnp.dot(dlt, wg_ref[g], preferred_element_type=F32))
    z = (jnp.concatenate(outs, axis=-1) * cs_ref[...]).astype(BF16)
    y = jnp.dot(z, wo_ref[...], preferred_element_type=F32)
    _tail(y, *tail_refs)


def _mixer_tail(front_args, front_specs, kern, x, g1, sh2, s2, n2g, rwh, rwl, rb, *, tm, scratch=()):
    b, l, d = x.shape
    nt = l // tm
    n = b * l
    row = lambda bi, i: (bi, 0, 0)
    fix = lambda bi, i: (0, 0)
    tile = lambda bi, i: (bi, i, 0)
    in_specs = list(front_specs) + [
        pl.BlockSpec((1, tm, d), tile),
        pl.BlockSpec((1, 1, d), row), pl.BlockSpec((1, 1, d), row), pl.BlockSpec((1, 1, d), row),
        pl.BlockSpec((1, d), fix),
        pl.BlockSpec((N_EXPERTS, d), fix), pl.BlockSpec((N_EXPERTS, d), fix),
        pl.BlockSpec((N_EXPERTS, 1), fix)]
    out_specs = [pl.BlockSpec((1, tm, d), tile), pl.BlockSpec((1, tm, d), tile),
                 pl.BlockSpec((SUBLANES, tm), lambda bi, i: (0, bi * nt + i)),
                 pl.BlockSpec((tm, LANES), lambda bi, i: (bi * nt + i, 0)),
                 pl.BlockSpec((N_EXPERTS, LANES), fix)]
    out_shape = [jax.ShapeDtypeStruct((b, l, d), F32), jax.ShapeDtypeStruct((b, l, d), F32),
                 jax.ShapeDtypeStruct((SUBLANES, n), jnp.int32), jax.ShapeDtypeStruct((n, LANES), F32),
                 jax.ShapeDtypeStruct((N_EXPERTS, LANES), F32)]
    return pl.pallas_call(
        kern,
        grid=(b, nt),
        in_specs=in_specs, out_specs=out_specs, out_shape=out_shape,
        scratch_shapes=[pltpu.VMEM((N_EXPERTS, LANES), F32)] + list(scratch),
        compiler_params=_params("arbitrary", "arbitrary"),
        name="mixer_tail",
    )(*front_args, x, g1, sh2, s2, n2g, rwh, rwl, rb)


def _slot_of(ps_ref, ridx_ref, t, k):
    return ps_ref[ridx_ref[k, t]] + ridx_ref[TOP_K + k, t]


def _scatter_kernel(ps_ref, ridx_ref, h2_ref, xs_in, xs_hbm, sem, *, ts):
    del xs_in

    def row_copy(t, k):
        return pltpu.make_async_copy(h2_ref.at[pl.ds(t, 1)],
                                     xs_hbm.at[pl.ds(_slot_of(ps_ref, ridx_ref, t, k), 1)], sem)

    def issue(t, carry):
        for k in range(TOP_K):
            row_copy(t, k).start()
        return carry

    def drain(t, carry):
        for k in range(TOP_K):
            row_copy(t, k).wait()
        return carry

    lax.fori_loop(0, ts, issue, 0)
    lax.fori_loop(0, ts, drain, 0)


def _dispatch(pad_start, ridx, h2_flat, n_slots, *, ts):
    n, d = h2_flat.shape
    kern = functools.partial(_scatter_kernel, ts=ts)
    return pl.pallas_call(
        kern,
        grid_spec=pltpu.PrefetchScalarGridSpec(
            num_scalar_prefetch=1,
            grid=(n // ts,),
            in_specs=[pl.BlockSpec((SUBLANES, ts), lambda i, ps: (0, i), memory_space=pltpu.SMEM),
                      pl.BlockSpec((ts, d), lambda i, ps: (i, 0)),
                      pl.BlockSpec(memory_space=pl.ANY)],
            out_specs=pl.BlockSpec(memory_space=pl.ANY),
            scratch_shapes=[pltpu.SemaphoreType.DMA(())]),
        out_shape=jax.ShapeDtypeStruct((n_slots, d), F32),
        input_output_aliases={3: 0},
        compiler_params=_params("arbitrary"),
        name="moe_dispatch",
    )(pad_start, ridx, h2_flat, jnp.zeros((n_slots, d), F32))


def _ffn_kernel(be_ref, nu_ref, xs_ref, wg_ref, wu_ref, wd_ref, ys_ref):
    del be_ref

    @pl.when(pl.program_id(0) < nu_ref[0])
    def _():
        xb = xs_ref[...].astype(BF16)
        g = jnp.dot(xb, wg_ref[0].astype(BF16), preferred_element_type=F32)
        u = jnp.dot(xb, wu_ref[0].astype(BF16), preferred_element_type=F32)
        a = (g / (1.0 + jnp.exp(-g)) * u).astype(BF16)
        ys_ref[...] = jnp.dot(a, wd_ref[0].astype(BF16), preferred_element_type=F32)

    @pl.when(pl.program_id(0) >= nu_ref[0])
    def _():
        ys_ref[...] = jnp.zeros_like(ys_ref)


def _expert_ffn(block_e, n_used, xs, w_gate, w_up, w_down):
    n_slots, d = xs.shape
    de = w_gate.shape[2]
    return pl.pallas_call(
        _ffn_kernel,
        grid_spec=pltpu.PrefetchScalarGridSpec(
            num_scalar_prefetch=2,
            grid=(n_slots // SLOT_ROWS,),
            in_specs=[pl.BlockSpec((SLOT_ROWS, d), lambda j, be, nu: (j, 0)),
                      pl.BlockSpec((1, d, de), lambda j, be, nu: (be[j], 0, 0)),
                      pl.BlockSpec((1, d, de), lambda j, be, nu: (be[j], 0, 0)),
                      pl.BlockSpec((1, de, d), lambda j, be, nu: (be[j], 0, 0))],
            out_specs=pl.BlockSpec((SLOT_ROWS, d), lambda j, be, nu: (j, 0))),
        out_shape=jax.ShapeDtypeStruct((n_slots, d), F32),
        compiler_params=_params("arbitrary"),
        name="expert_ffn",
    )(block_e, n_used, xs, w_gate, w_up, w_down)


def _combine_kernel(ps_ref, ridx_ref, ys_hbm, gcol_ref, x_ref, g2_ref, *rest, tc, pool_in):
    if pool_in:
        sh_ref, sc_ref, n1_ref, wi_ref, x_out, u_out, buf0, buf1, sem = rest
    else:
        x_out, buf0, buf1, sem = rest
    bufs = (buf0, buf1)

    def row_copy(t, k):
        return pltpu.make_async_copy(ys_hbm.at[pl.ds(_slot_of(ps_ref, ridx_ref, t, k), 1)],
                                     bufs[k].at[pl.ds(t, 1)], sem)

    def issue(t, carry):
        for k in range(TOP_K):
            row_copy(t, k).start()
        return carry

    def drain(t, carry):
        for k in range(TOP_K):
            row_copy(t, k).wait()
        return carry

    lax.fori_loop(0, tc, issue, 0)
    lax.fori_loop(0, tc, drain, 0)
    gc = gcol_ref[...]
    out = gc[:, 0:1] * buf0[...] + gc[:, 1:2] * buf1[...]
    x2 = x_ref[0] + g2_ref[0] * out
    x_out[0] = x2
    if pool_in:
        hb = _rms_mod(x2, n1_ref[...], sh_ref[0], sc_ref[0]).astype(BF16)
        u_out[0] = jnp.dot(hb, wi_ref[...], preferred_element_type=F32)


def _combine(pad_start, ridx, ys, gcol, x, g2, pool_args=None, *, tc):
    b, l, d = x.shape
    nt = l // tc
    pool_in = pool_args is not None
    kern = functools.partial(_combine_kernel, tc=tc, pool_in=pool_in)
    row = lambda bi, i, ps: (bi, 0, 0)
    fix = lambda bi, i, ps: (0, 0)
    tile = lambda bi, i, ps: (bi, i, 0)
    in_specs = [pl.BlockSpec((SUBLANES, tc), lambda bi, i, ps: (0, bi * nt + i), memory_space=pltpu.SMEM),
                pl.BlockSpec(memory_space=pl.ANY),
                pl.BlockSpec((tc, LANES), lambda bi, i, ps: (bi * nt + i, 0)),
                pl.BlockSpec((1, tc, d), tile),
                pl.BlockSpec((1, 1, d), row)]
    out_specs = [pl.BlockSpec((1, tc, d), tile)]
    out_shape = [jax.ShapeDtypeStruct((b, l, d), F32)]
    args = [pad_start, ridx, ys, gcol, x, g2]
    if pool_in:
        in_specs += [pl.BlockSpec((1, 1, d), row), pl.BlockSpec((1, 1, d), row),
                     pl.BlockSpec((1, d), fix), pl.BlockSpec((d, d), fix)]
        out_specs.append(pl.BlockSpec((1, tc, d), tile))
        out_shape.append(jax.ShapeDtypeStruct((b, l, d), F32))
        args += list(pool_args)
    return pl.pallas_call(
        kern,
        grid_spec=pltpu.PrefetchScalarGridSpec(
            num_scalar_prefetch=1,
            grid=(b, nt),
            in_specs=in_specs, out_specs=out_specs,
            scratch_shapes=[pltpu.VMEM((tc, d), F32), pltpu.VMEM((tc, d), F32),
                            pltpu.SemaphoreType.DMA(())]),
        out_shape=out_shape,
        compiler_params=_params("arbitrary", "arbitrary"),
        name="moe_combine",
    )(*args)


def _moe(h2, ridx, gcol, counts, x1, g2, w_gate, w_up, w_down, pool_args=None):
    b, l, d = x1.shape
    n = b * l
    n_blocks = (n * TOP_K) // SLOT_ROWS + N_EXPERTS
    cnt = counts[:, 0].astype(jnp.int32)
    padded = (cnt + SLOT_ROWS - 1) // SLOT_ROWS * SLOT_ROWS
    pad_end = jnp.cumsum(padded)
    pad_start = (pad_end - padded).astype(jnp.int32)
    block_start = jnp.arange(n_blocks, dtype=jnp.int32) * SLOT_ROWS
    block_e = jnp.minimum(jnp.sum(pad_end[None, :] <= block_start[:, None], axis=1), N_EXPERTS - 1).astype(jnp.int32)
    n_used = (pad_end[-1:] // SLOT_ROWS).astype(jnp.int32)
    xs = _dispatch(pad_start, ridx, h2.reshape(n, d), n_blocks * SLOT_ROWS, ts=512)
    ys = _expert_ffn(block_e, n_used, xs, w_gate, w_up, w_down)
    return _combine(pad_start, ridx, ys, gcol, x1, g2, pool_args, tc=256)


def kernel(x, c, ctx, c_ctx, ada_w, ada_b, norm1_g, norm2_g, attn_w_in, attn_w_out, attn_q_gain, attn_k_gain,
           attn_lq1, attn_lk1, attn_lq2, attn_lk2, attn_sub_gain, pool_w_in, pool_w_group, pool_scale, pool_w_out,
           router_w, router_b, moe_w_gate, moe_w_up, moe_w_down):
    b, l, d = x.shape
    n_ctx = ctx.shape[1]
    depth = ada_w.shape[0]
    assert depth == 2 and d == N_HEADS * V_DIM
    tm = 512

    mod = _adaln_mod(c, c_ctx, ada_w, ada_b)
    mods = [[mod[i, :b, None, j * d:(j + 1) * d] for j in range(N_MOD)] for i in range(depth)]
    mod_ctx = [jnp.broadcast_to(mod[0, b, j * d:(j + 1) * d], (b, 1, d)) for j in range(2)]

    rwt = router_w.T
    rwh = rwt.astype(BF16)
    rwl = (rwt - rwh.astype(F32)).astype(BF16)
    rb = router_b.reshape(N_EXPERTS, 1)

    sh1, s1, g1, sh2, s2, g2 = mods[0]
    cos, sin = _rope_tables(l)
    gain2 = lambda g: jnp.concatenate([g, g])[None]
    qg, kg = gain2(attn_q_gain[0]), gain2(attn_k_gain[0])
    w_in = attn_w_in[0].astype(BF16)
    q, k, v = _qkv_proj(x, sh1, s1, norm1_g[0][None], w_in, qg, kg, cos, sin, n_qk=2, rope=True, tm=tm)
    kc, vc = _qkv_proj(ctx, mod_ctx[0], mod_ctx[1], norm1_g[0][None], w_in[:, d:], qg, kg,
                       cos[:n_ctx], sin[:n_ctx], n_qk=1, rope=False, tm=n_ctx)
    k_all = jnp.concatenate([k, kc], axis=1)
    v_all = jnp.concatenate([v, vc], axis=1)
    vt_all = v_all.reshape(b, l + n_ctx, N_HEADS, V_DIM).transpose(0, 2, 3, 1)
    ones_rows = jnp.zeros((b, N_HEADS, V_ROWS - V_DIM, l + n_ctx), BF16).at[:, :, 0].set(1.0)
    vt_all = jnp.concatenate([vt_all, ones_rows], axis=2)
    lam_init = 0.8 - 0.6 * math.exp(-0.3 * 0)
    lam_params = jnp.stack([attn_lq1[0], attn_lk1[0], attn_lq2[0], attn_lk2[0]])
    o = _diff_attention(q, k_all, vt_all, lam_params, attn_sub_gain[0][:, None], lam_init=lam_init, tq=256)

    fix = lambda bi, i: (0, 0)
    x1, h2, ridx, gcol, counts = _mixer_tail(
        (o, attn_w_out[0].astype(BF16)),
        (pl.BlockSpec((1, tm, d), lambda bi, i: (bi, i, 0)), pl.BlockSpec((d, d), fix)),
        _attn_tail_kernel, x, g1, sh2, s2, norm2_g[0][None], rwh, rwl, rb, tm=tm)

    sh1b, s1b, g1b, sh2b, s2b, g2b = mods[1]
    x2, u = _moe(h2, ridx, gcol, counts, x1, g2, moe_w_gate[0], moe_w_up[0], moe_w_down[0],
                 pool_args=(sh1b, s1b, norm1_g[1][None], pool_w_in[0].astype(BF16)))
    gd = pool_w_group.shape[2]
    nh = tm // POOL_HALO
    front_specs = (
        pl.BlockSpec((1, tm, d), lambda bi, i: (bi, i, 0)),
        pl.BlockSpec((1, POOL_HALO, d), lambda bi, i: (bi, jnp.maximum(i * nh - 1, 0), 0)),
        pl.BlockSpec((1, POOL_HALO, d), lambda bi, i: (bi, jnp.minimum((i + 1) * nh, l // POOL_HALO - 1), 0)),
        pl.BlockSpec((len(POOL_WINDOWS), gd, gd), lambda bi, i: (0, 0, 0)),
        pl.BlockSpec((1, d), fix),
        pl.BlockSpec((d, d), fix))
    x3, h2b, ridx_b, gcol_b, counts_b = _mixer_tail(
        (u, u, u, pool_w_group[0].astype(BF16), pool_scale[0][None], pool_w_out[0].astype(BF16)),
        front_specs, functools.partial(_pool_tail_kernel, seq_len=l),
        x2, g1b, sh2b, s2b, norm2_g[1][None], rwh, rwl, rb, tm=tm,
        scratch=[pltpu.VMEM((tm + 2 * POOL_HALO, d), F32)])
    (out,) = _moe(h2b, ridx_b, gcol_b, counts_b, x3, g2b, moe_w_gate[1], moe_w_up[1], moe_w_down[1])
    return out
```

```python
import functools
import math

import jax
import jax.numpy as jnp
from jax import lax
from jax.experimental import pallas as pl
from jax.experimental.pallas import tpu as pltpu
from jax.experimental.pallas import tpu_sc as plsc

F32 = jnp.float32
BF16 = jnp.bfloat16

LANES = 128
SUBLANES = 8
N_HEADS = 8
HEAD_DIM = 64
V_DIM = 2 * HEAD_DIM
V_ROWS = V_DIM + 16
GRID_W = 64
ROPE_THETA = 10000.0
NORM_EPS = 1e-6
N_MOD = 6
POOL_WINDOWS = (2, 4, 8, 16)
POOL_HALO = max(POOL_WINDOWS) // 2
N_EXPERTS = 32
N_EXPERT_GROUPS = 4
EXPERTS_PER_GROUP = N_EXPERTS // N_EXPERT_GROUPS
TOP_K = 2
SLOT_ROWS = 256
SC_CORES = 2
SC_SUBCORES = 16
SC_WORKERS = SC_CORES * SC_SUBCORES
SC_WINDOW = 32
ATTN_GROUP_CHUNKS = 1
VMEM_LIMIT = 48 * 1024 * 1024
NT_DIMS = (((1,), (1,)), ((), ()))


def _params(*sem):
    return pltpu.CompilerParams(dimension_semantics=sem, vmem_limit_bytes=VMEM_LIMIT)


def _rms_mod(x, gain, shift, scale):
    h = x * lax.rsqrt(jnp.mean(x * x, axis=-1, keepdims=True) + NORM_EPS) * gain
    return h * (1.0 + scale) + shift


def _mod_kernel(c_ref, w_ref, b_ref, o_ref):
    c = c_ref[...]
    a = c / (1.0 + jnp.exp(-c))
    o_ref[0] = jnp.dot(a, w_ref[0], precision=lax.Precision.HIGHEST,
                       preferred_element_type=F32) + b_ref[0]


def _adaln_mod(c, c_ctx, ada_w, ada_b):
    depth, d, n_out = ada_w.shape
    b = c.shape[0]
    assert b + 1 <= SUBLANES
    rows = jnp.concatenate([c, c_ctx[None], jnp.zeros((SUBLANES - b - 1, d), F32)], axis=0)
    tn = n_out // 4
    return pl.pallas_call(
        _mod_kernel,
        grid=(depth, n_out // tn),
        in_specs=[pl.BlockSpec((SUBLANES, d), lambda i, j: (0, 0)),
                  pl.BlockSpec((1, d, tn), lambda i, j: (i, 0, j)),
                  pl.BlockSpec((1, 1, tn), lambda i, j: (i, 0, j))],
        out_specs=pl.BlockSpec((1, SUBLANES, tn), lambda i, j: (i, 0, j)),
        out_shape=jax.ShapeDtypeStruct((depth, SUBLANES, n_out), F32),
        compiler_params=_params("arbitrary", "arbitrary"),
        name="adaln_mod",
    )(rows, ada_w, ada_b.reshape(depth, 1, n_out))


def _qkv_kernel(x_ref, sh_ref, sc_ref, g_ref, w_ref, qg_ref, kg_ref, cos_ref, sin_ref, *out_refs,
                n_qk, rope):
    d = x_ref.shape[2]
    hb = _rms_mod(x_ref[0], g_ref[...], sh_ref[0], sc_ref[0]).astype(BF16)
    lane = lax.broadcasted_iota(jnp.int32, (1, LANES), 1)
    lane_lo = lane < HEAD_DIM
    lane_b4 = (lane & 16) == 0
    gains = (qg_ref[...], kg_ref[...])[2 - n_qk:]
    for t in range(n_qk):
        for j in range(0, d, 2 * LANES):
            acc = jnp.dot(hb, w_ref[:, t * d + j:t * d + j + 2 * LANES], preferred_element_type=F32)
            for half in range(2):
                blk = acc[:, half * LANES:(half + 1) * LANES]
                sq = blk * blk
                lo = jnp.sum(jnp.where(lane_lo, sq, 0.0), axis=-1, keepdims=True)
                hi = jnp.sum(jnp.where(lane_lo, 0.0, sq), axis=-1, keepdims=True)
                ms = jnp.where(lane_lo, lo, hi) * (1.0 / HEAD_DIM)
                y = blk * lax.rsqrt(ms + NORM_EPS) * gains[t]
                if rope:
                    rot = jnp.where(lane_b4, pltpu.roll(y, LANES - 16, 1), pltpu.roll(y, 16, 1))
                    y = y * cos_ref[...] + rot * sin_ref[...]
                c0 = j + half * LANES
                out_refs[t][0, :, c0:c0 + LANES] = y.astype(BF16)
    for j in range(0, d, 2 * LANES):
        acc = jnp.dot(hb, w_ref[:, n_qk * d + j:n_qk * d + j + 2 * LANES], preferred_element_type=F32)
        out_refs[n_qk][0, :, j:j + 2 * LANES] = acc.astype(BF16)


def _qkv_proj(x, shift, scale, gain, w, q_gain, k_gain, cos, sin, *, n_qk, rope, tm):
    b, l, d = x.shape
    n_out = n_qk + 1
    kern = functools.partial(_qkv_kernel, n_qk=n_qk, rope=rope)
    row = lambda bi, i: (bi, 0, 0)
    fix = lambda bi, i: (0, 0)
    return pl.pallas_call(
        kern,
        grid=(b, l // tm),
        in_specs=[pl.BlockSpec((1, tm, d), lambda bi, i: (bi, i, 0)),
                  pl.BlockSpec((1, 1, d), row), pl.BlockSpec((1, 1, d), row),
                  pl.BlockSpec((1, d), fix),
                  pl.BlockSpec((d, n_out * d), fix),
                  pl.BlockSpec((1, LANES), fix), pl.BlockSpec((1, LANES), fix),
                  pl.BlockSpec((tm, LANES), lambda bi, i: (i, 0)),
                  pl.BlockSpec((tm, LANES), lambda bi, i: (i, 0))],
        out_specs=[pl.BlockSpec((1, tm, d), lambda bi, i: (bi, i, 0))] * n_out,
        out_shape=[jax.ShapeDtypeStruct((b, l, d), BF16)] * n_out,
        compiler_params=_params("arbitrary", "arbitrary"),
        name="qkv_proj",
    )(x, shift, scale, gain, w, q_gain, k_gain, cos, sin)


def _rope_tables(n_tokens):
    rows = n_tokens // GRID_W
    row = jnp.repeat(jnp.arange(rows, dtype=F32), GRID_W)
    col = jnp.tile(jnp.arange(GRID_W, dtype=F32), rows)
    half = HEAD_DIM // 2
    inv_freq = ROPE_THETA ** (-jnp.arange(0, half, 2, dtype=F32) / half)
    ang_r = row[:, None] * inv_freq
    ang_c = col[:, None] * inv_freq
    ang = jnp.concatenate([ang_r, ang_r, ang_c, ang_c], axis=-1)
    ang = jnp.concatenate([ang, ang], axis=-1)
    sign = jnp.where((jnp.arange(LANES) & 16) == 0, -1.0, 1.0).astype(F32)
    return jnp.cos(ang), jnp.sin(ang) * sign


def _attn_kernel(q_ref, k_ref, vt_ref, lp_ref, sg_ref, o_ref, s_ref, *, tk, group, lam_init):
    tq = q_ref.shape[1]
    n_chunks = k_ref.shape[1] // tk
    qt = (q_ref[0].astype(F32) * (HEAD_DIM ** -0.5 * math.log2(math.e))).T
    sub = lax.broadcasted_iota(jnp.int32, (V_DIM, 1), 0)
    first = sub < HEAD_DIM
    qz = jnp.concatenate([jnp.where(first, qt, 0.0), jnp.where(first, 0.0, qt)], axis=1).astype(BF16)
    def score_chunk(c, m_grp):
        st = jnp.dot(k_ref[0, c * tk:(c + 1) * tk, :], qz, preferred_element_type=F32).astype(BF16)
        s_ref[c * tk:(c + 1) * tk, :] = st
        mc = jnp.max(st, axis=0, keepdims=True)
        return mc if m_grp is None else jnp.maximum(m_grp, mc)

    def value_chunk(c, m_ref, part):
        p = jnp.exp2(s_ref[c * tk:(c + 1) * tk, :] - m_ref)
        pv = jnp.dot(vt_ref[0, 0, :, c * tk:(c + 1) * tk], p, preferred_element_type=F32)
        return pv if part is None else part + pv

    groups = [list(range(g0, min(g0 + group, n_chunks))) for g0 in range(0, n_chunks, group)]
    m_grp = None
    for c in groups[0]:
        m_grp = score_chunk(c, m_grp)
    m = None
    acc = None
    for gi, cur in enumerate(groups):
        nxt = groups[gi + 1] if gi + 1 < len(groups) else []
        m_new = m_grp if m is None else jnp.maximum(m, m_grp)
        m_grp = None
        part = None
        for i in range(max(len(cur), len(nxt))):
            if i < len(nxt):
                m_grp = score_chunk(nxt[i], m_grp)
            if i < len(cur):
                part = value_chunk(cur[i], m_new, part)
        acc = part if acc is None else acc * jnp.exp2(m.astype(F32) - m_new.astype(F32)) + part
        m = m_new
    acc = acc[:V_DIM] / acc[V_DIM:V_DIM + 1]
    lp = lp_ref[...]
    lam = (jnp.exp(jnp.sum(lp[0:1] * lp[1:2], axis=-1, keepdims=True))
           - jnp.exp(jnp.sum(lp[2:3] * lp[3:4], axis=-1, keepdims=True)) + lam_init)
    o = acc[:, :tq] - lam * acc[:, tq:]
    o = o * lax.rsqrt(jnp.mean(o * o, axis=0, keepdims=True) + NORM_EPS) * sg_ref[...] * (1.0 - lam_init)
    o_ref[0] = o.T.astype(BF16)


def _attn_chunk(n_keys):
    for tk in (768, 512, 256, 128):
        if n_keys % tk == 0:
            return tk
    raise ValueError(f"key count {n_keys} is not a multiple of {LANES}")


def _diff_attention(q, k_all, vt_all, lam_params, sub_gain, *, lam_init, tq):
    b, l, d = q.shape
    n_keys = k_all.shape[1]
    tk = _attn_chunk(n_keys)
    kern = functools.partial(_attn_kernel, tk=tk, group=ATTN_GROUP_CHUNKS, lam_init=lam_init)
    return pl.pallas_call(
        kern,
        grid=(b, N_HEADS, l // tq),
        in_specs=[pl.BlockSpec((1, tq, V_DIM), lambda bi, h, i: (bi, i, h)),
                  pl.BlockSpec((1, n_keys, V_DIM), lambda bi, h, i: (bi, 0, h)),
                  pl.BlockSpec((1, 1, V_ROWS, n_keys), lambda bi, h, i: (bi, h, 0, 0)),
                  pl.BlockSpec((4, HEAD_DIM), lambda bi, h, i: (0, 0)),
                  pl.BlockSpec((V_DIM, 1), lambda bi, h, i: (0, 0))],
        out_specs=pl.BlockSpec((1, tq, V_DIM), lambda bi, h, i: (bi, i, h)),
        out_shape=jax.ShapeDtypeStruct((b, l, d), BF16),
        scratch_shapes=[pltpu.VMEM((n_keys, 2 * tq), BF16)],
        compiler_params=_params("arbitrary", "arbitrary", "arbitrary"),
        name="diff_attention",
    )(q, k_all, vt_all, lam_params, sub_gain)


def _route(h2, rwh_ref, rwl_ref, rb_ref, carry_ref, ridx_ref, gcol_ref, cnt_ref, is_first):
    tm = h2.shape[0]
    hh = h2.astype(BF16)
    hl = (h2 - hh.astype(F32)).astype(BF16)
    rw2 = jnp.concatenate([rwh_ref[...], rwl_ref[...]], axis=0)
    part = lax.dot_general(rw2, hh, NT_DIMS, preferred_element_type=F32)
    logits = (part[:N_EXPERTS] + part[N_EXPERTS:]
              + lax.dot_general(rwh_ref[...], hl, NT_DIMS, preferred_element_type=F32) + rb_ref[...])
    groups = [logits[g * EXPERTS_PER_GROUP:(g + 1) * EXPERTS_PER_GROUP] for g in range(N_EXPERT_GROUPS)]
    top = groups[0]
    for g in range(1, N_EXPERT_GROUPS):
        top = jnp.maximum(top, groups[g])
    top = jnp.max(top, axis=0, keepdims=True)
    sub = lax.broadcasted_iota(jnp.int32, (EXPERTS_PER_GROUP, tm), 0)
    best = None
    for g in range(N_EXPERT_GROUPS):
        ex = jnp.exp(groups[g] - top)
        v1 = jnp.max(ex, axis=0, keepdims=True)
        i1 = jnp.min(jnp.where(ex == v1, sub, EXPERTS_PER_GROUP), axis=0, keepdims=True)
        rest = jnp.where(sub == i1, -1.0, ex)
        v2 = jnp.max(rest, axis=0, keepdims=True)
        i2 = jnp.min(jnp.where(rest == v2, sub, EXPERTS_PER_GROUP), axis=0, keepdims=True)
        cand = (v1 + v2, v1, v2, i1 + g * EXPERTS_PER_GROUP, i2 + g * EXPERTS_PER_GROUP)
        if best is None:
            best = cand
        else:
            better = cand[0] > best[0]
            best = tuple(jnp.where(better, new, old) for new, old in zip(cand, best))
    _, v1, v2, e0, e1 = best
    gate0 = v1 / (v1 + v2)
    gate1 = v2 / (v1 + v2)

    @pl.when(is_first)
    def _():
        carry_ref[...] = jnp.zeros_like(carry_ref)

    erow = lax.broadcasted_iota(jnp.int32, (N_EXPERTS, tm), 0)
    oh0 = erow == e0
    oh1 = erow == e1
    chosen = jnp.where(oh0 | oh1, 1.0, 0.0)
    before = (lax.broadcasted_iota(jnp.int32, (tm, tm), 0)
              < lax.broadcasted_iota(jnp.int32, (tm, tm), 1)).astype(BF16)
    prior = carry_ref[:, 0:1] + jnp.dot(chosen.astype(BF16), before, preferred_element_type=F32)
    r0 = jnp.sum(jnp.where(oh0, prior, 0.0), axis=0, keepdims=True).astype(jnp.int32)
    r1 = jnp.sum(jnp.where(oh1, prior, 0.0), axis=0, keepdims=True).astype(jnp.int32)
    carry_ref[...] = carry_ref[...] + jnp.sum(chosen, axis=1, keepdims=True)
    cnt_ref[...] = carry_ref[...]
    rid = lax.broadcasted_iota(jnp.int32, (SUBLANES, tm), 0)
    ridx_ref[...] = jnp.where(rid == 0, e0, jnp.where(rid == 1, e1, jnp.where(rid == 2, r0, jnp.where(rid == 3, r1, 0))))
    gid = lax.broadcasted_iota(jnp.int32, (LANES, tm), 0)
    gcol_ref[...] = jnp.where(gid == 0, gate0, jnp.where(gid == 1, gate1, 0.0)).T


def _tail(y, x_ref, g1_ref, sh2_ref, s2_ref, n2_ref, rwh_ref, rwl_ref, rb_ref,
          x_out, h2_out, ridx_ref, gcol_ref, cnt_ref, carry_ref):
    x1 = x_ref[0] + g1_ref[0] * y
    x_out[0] = x1
    h2 = _rms_mod(x1, n2_ref[...], sh2_ref[0], s2_ref[0])
    h2_out[0] = h2
    is_first = (pl.program_id(0) == 0) & (pl.program_id(1) == 0)
    _route(h2, rwh_ref, rwl_ref, rb_ref, carry_ref, ridx_ref, gcol_ref, cnt_ref, is_first)


def _attn_tail_kernel(a_ref, wo_ref, *rest):
    y = jnp.dot(a_ref[0], wo_ref[...], preferred_element_type=F32)
    _tail(y, *rest)


def _pool_tail_kernel(u_ref, up_ref, un_ref, wg_ref, cs_ref, wo_ref, *rest, seq_len):
    *tail_refs, ubuf = rest
    tm = u_ref.shape[1]
    i = pl.program_id(1)
    u = u_ref[0]
    ubuf[0:POOL_HALO] = jnp.where(i > 0, up_ref[0], 0.0)
    ubuf[POOL_HALO:POOL_HALO + tm] = u
    ubuf[POOL_HALO + tm:2 * POOL_HALO + tm] = jnp.where(i < pl.num_programs(1) - 1, un_ref[0], 0.0)
    pos = i * tm + lax.broadcasted_iota(jnp.int32, (tm, 1), 0)
    gd = wg_ref.shape[1]
    outs = []
    for g, win in enumerate(POOL_WINDOWS):
        half = win // 2
        cols = slice(g * gd, (g + 1) * gd)
        s = ubuf[POOL_HALO - half:POOL_HALO - half + tm, cols]
        for j in range(1 - half, half):
            s = s + ubuf[POOL_HALO + j:POOL_HALO + j + tm, cols]
        cnt = (jnp.minimum(pos + half, seq_len) - jnp.maximum(pos - half, 0)).astype(F32)
        dlt = (s / cnt - u[:, cols]).astype(BF16)
        outs.append(jnp.dot(dlt, wg_ref[g], preferred_element_type=F32))
    z = (jnp.concatenate(outs, axis=-1) * cs_ref[...]).astype(BF16)
    y = jnp.dot(z, wo_ref[...], preferred_element_type=F32)
    _tail(y, *tail_refs)


def _mixer_tail(front_args, front_specs, kern, x, g1, sh2, s2, n2g, rwh, rwl, rb, *, tm, scratch=()):
    b, l, d = x.shape
    nt = l // tm
    n = b * l
    row = lambda bi, i: (bi, 0, 0)
    fix = lambda bi, i: (0, 0)
    tile = lambda bi, i: (bi, i, 0)
    in_specs = list(front_specs) + [
        pl.BlockSpec((1, tm, d), tile),
        pl.BlockSpec((1, 1, d), row), pl.BlockSpec((1, 1, d), row), pl.BlockSpec((1, 1, d), row),
        pl.BlockSpec((1, d), fix),
        pl.BlockSpec((N_EXPERTS, d), fix), pl.BlockSpec((N_EXPERTS, d), fix),
        pl.BlockSpec((N_EXPERTS, 1), fix)]
    out_specs = [pl.BlockSpec((1, tm, d), tile), pl.BlockSpec((1, tm, d), tile),
                 pl.BlockSpec((SUBLANES, tm), lambda bi, i: (0, bi * nt + i)),
                 pl.BlockSpec((tm, LANES), lambda bi, i: (bi * nt + i, 0)),
                 pl.BlockSpec((N_EXPERTS, LANES), fix)]
    out_shape = [jax.ShapeDtypeStruct((b, l, d), F32), jax.ShapeDtypeStruct((b, l, d), F32),
                 jax.ShapeDtypeStruct((SUBLANES, n), jnp.int32), jax.ShapeDtypeStruct((n, LANES), F32),
                 jax.ShapeDtypeStruct((N_EXPERTS, LANES), F32)]
    return pl.pallas_call(
        kern,
        grid=(b, nt),
        in_specs=in_specs, out_specs=out_specs, out_shape=out_shape,
        scratch_shapes=[pltpu.VMEM((N_EXPERTS, LANES), F32)] + list(scratch),
        compiler_params=_params("arbitrary", "arbitrary"),
        name="mixer_tail",
    )(*front_args, x, g1, sh2, s2, n2g, rwh, rwl, rb)


def _slot_kernel(ps_ref, ridx_ref, dest_ref):
    ridx = ridx_ref[...]
    ps = ps_ref[...]
    erow = lax.broadcasted_iota(jnp.int32, (N_EXPERTS, ridx.shape[1]), 0)
    rows = []
    for k in range(TOP_K):
        start = jnp.sum(jnp.where(erow == ridx[k:k + 1], ps, 0), axis=0, keepdims=True)
        rows.append(start + ridx[TOP_K + k:TOP_K + k + 1])
    rid = lax.broadcasted_iota(jnp.int32, ridx.shape, 0)
    dest_ref[...] = jnp.where(rid == 0, rows[0], jnp.where(rid == 1, rows[1], 0))


def _slot_index(pad_start, ridx, *, tn):
    n = ridx.shape[1]
    return pl.pallas_call(
        _slot_kernel,
        grid=(n // tn,),
        in_specs=[pl.BlockSpec((N_EXPERTS, 1), lambda i: (0, 0)),
                  pl.BlockSpec((SUBLANES, tn), lambda i: (0, i))],
        out_specs=pl.BlockSpec((SUBLANES, tn), lambda i: (0, i)),
        out_shape=jax.ShapeDtypeStruct((SUBLANES, n), jnp.int32),
        compiler_params=_params("arbitrary"),
        name="slot_index",
    )(pad_start.reshape(N_EXPERTS, 1), ridx)


def _sc_mesh():
    return plsc.VectorSubcoreMesh(core_axis_name="c", subcore_axis_name="s",
                                  num_cores=SC_CORES, num_subcores=SC_SUBCORES)


def _sc_worker_base(per_worker):
    return (lax.axis_index("s") * SC_CORES + lax.axis_index("c")) * per_worker


def _sc_scatter_rows(rows, idx0, idx1, n_slots):
    n, d = rows.shape
    per_worker = n // SC_WORKERS
    assert per_worker % SC_WINDOW == 0

    def body(rows_hbm, i0_hbm, i1_hbm, out_hbm, i0_v, i1_v, rows_v):
        base = _sc_worker_base(per_worker)

        @pl.loop(0, per_worker // SC_WINDOW)
        def _(j):
            off = pl.multiple_of(base + j * SC_WINDOW, SC_WINDOW)
            pltpu.sync_copy(i0_hbm.at[pl.ds(off, SC_WINDOW)], i0_v)
            pltpu.sync_copy(i1_hbm.at[pl.ds(off, SC_WINDOW)], i1_v)
            pltpu.sync_copy(rows_hbm.at[pl.ds(off, SC_WINDOW)], rows_v)
            pltpu.sync_copy(rows_v, out_hbm.at[i0_v])
            pltpu.sync_copy(rows_v, out_hbm.at[i1_v])

    return pl.kernel(
        body, out_type=jax.ShapeDtypeStruct((n_slots, d), rows.dtype), mesh=_sc_mesh(),
        scratch_types=[pltpu.VMEM((SC_WINDOW,), jnp.int32), pltpu.VMEM((SC_WINDOW,), jnp.int32),
                       pltpu.VMEM((SC_WINDOW, d), rows.dtype)],
        name="sc_scatter_rows",
    )(rows, idx0, idx1)


def _sc_gather_rows(table, idx):
    n = idx.shape[0]
    d = table.shape[1]
    per_worker = n // SC_WORKERS
    assert per_worker % SC_WINDOW == 0

    def body(table_hbm, idx_hbm, out_hbm, idx_v, rows_v, sem):
        base = _sc_worker_base(per_worker)

        @pl.loop(0, per_worker // SC_WINDOW)
        def _(j):
            off = pl.multiple_of(base + j * SC_WINDOW, SC_WINDOW)
            pltpu.sync_copy(idx_hbm.at[pl.ds(off, SC_WINDOW)], idx_v)
            pltpu.async_copy(table_hbm.at[idx_v], rows_v, sem).wait()
            pltpu.sync_copy(rows_v, out_hbm.at[pl.ds(off, SC_WINDOW)])

    return pl.kernel(
        body, out_type=jax.ShapeDtypeStruct((n, d), table.dtype), mesh=_sc_mesh(),
        scratch_types=[pltpu.VMEM((SC_WINDOW,), jnp.int32), pltpu.VMEM((SC_WINDOW, d), table.dtype),
                       pltpu.SemaphoreType.DMA],
        name="sc_gather_rows",
    )(table, idx)


def _ffn_kernel(be_ref, nv_ref, xs_ref, wg_ref, wu_ref, wd_ref, ys_ref):
    del be_ref
    valid = nv_ref[pl.program_id(0)]

    @pl.when(valid > 0)
    def _():
        row = lax.broadcasted_iota(jnp.int32, (SLOT_ROWS, 1), 0)
        xb = jnp.where(row < valid, xs_ref[...], 0.0).astype(BF16)
        g = jnp.dot(xb, wg_ref[0].astype(BF16), preferred_element_type=F32)
        u = jnp.dot(xb, wu_ref[0].astype(BF16), preferred_element_type=F32)
        a = (g / (1.0 + jnp.exp(-g)) * u).astype(BF16)
        ys_ref[...] = jnp.dot(a, wd_ref[0].astype(BF16), preferred_element_type=F32)

    @pl.when(valid <= 0)
    def _():
        ys_ref[...] = jnp.zeros_like(ys_ref)


def _expert_ffn(block_e, n_valid, xs, w_gate, w_up, w_down):
    n_slots, d = xs.shape
    de = w_gate.shape[2]
    return pl.pallas_call(
        _ffn_kernel,
        grid_spec=pltpu.PrefetchScalarGridSpec(
            num_scalar_prefetch=2,
            grid=(n_slots // SLOT_ROWS,),
            in_specs=[pl.BlockSpec((SLOT_ROWS, d), lambda j, be, nv: (j, 0)),
                      pl.BlockSpec((1, d, de), lambda j, be, nv: (be[j], 0, 0)),
                      pl.BlockSpec((1, d, de), lambda j, be, nv: (be[j], 0, 0)),
                      pl.BlockSpec((1, de, d), lambda j, be, nv: (be[j], 0, 0))],
            out_specs=pl.BlockSpec((SLOT_ROWS, d), lambda j, be, nv: (j, 0))),
        out_shape=jax.ShapeDtypeStruct((n_slots, d), F32),
        compiler_params=_params("arbitrary"),
        name="expert_ffn",
    )(block_e, n_valid, xs, w_gate, w_up, w_down)


def _combine_kernel(y0_ref, y1_ref, gcol_ref, x_ref, g2_ref, *rest, pool_in):
    if pool_in:
        sh_ref, sc_ref, n1_ref, wi_ref, x_out, u_out = rest
    else:
        (x_out,) = rest
    gc = gcol_ref[...]
    out = gc[:, 0:1] * y0_ref[0] + gc[:, 1:2] * y1_ref[0]
    x2 = x_ref[0] + g2_ref[0] * out
    x_out[0] = x2
    if pool_in:
        hb = _rms_mod(x2, n1_ref[...], sh_ref[0], sc_ref[0]).astype(BF16)
        u_out[0] = jnp.dot(hb, wi_ref[...], preferred_element_type=F32)


def _combine(yg, gcol, x, g2, pool_args=None, *, tc):
    b, l, d = x.shape
    nt = l // tc
    pool_in = pool_args is not None
    kern = functools.partial(_combine_kernel, pool_in=pool_in)
    row = lambda bi, i: (bi, 0, 0)
    fix = lambda bi, i: (0, 0)
    tile = lambda bi, i: (bi, i, 0)
    in_specs = [pl.BlockSpec((1, tc, d), lambda bi, i: (0, bi * nt + i, 0)),
                pl.BlockSpec((1, tc, d), lambda bi, i: (1, bi * nt + i, 0)),
                pl.BlockSpec((tc, LANES), lambda bi, i: (bi * nt + i, 0)),
                pl.BlockSpec((1, tc, d), tile),
                pl.BlockSpec((1, 1, d), row)]
    out_specs = [pl.BlockSpec((1, tc, d), tile)]
    out_shape = [jax.ShapeDtypeStruct((b, l, d), F32)]
    args = [yg, yg, gcol, x, g2]
    if pool_in:
        in_specs += [pl.BlockSpec((1, 1, d), row), pl.BlockSpec((1, 1, d), row),
                     pl.BlockSpec((1, d), fix), pl.BlockSpec((d, d), fix)]
        out_specs.append(pl.BlockSpec((1, tc, d), tile))
        out_shape.append(jax.ShapeDtypeStruct((b, l, d), F32))
        args += list(pool_args)
    return pl.pallas_call(
        kern,
        grid=(b, nt),
        in_specs=in_specs, out_specs=out_specs, out_shape=out_shape,
        compiler_params=_params("arbitrary", "arbitrary"),
        name="moe_combine",
    )(*args)


def _moe(h2, ridx, gcol, counts, x1, g2, w_gate, w_up, w_down, pool_args=None):
    b, l, d = x1.shape
    n = b * l
    n_blocks = (n * TOP_K) // SLOT_ROWS + N_EXPERTS
    cnt = counts[:, 0].astype(jnp.int32)
    padded = (cnt + SLOT_ROWS - 1) // SLOT_ROWS * SLOT_ROWS
    pad_end = jnp.cumsum(padded)
    pad_start = (pad_end - padded).astype(jnp.int32)
    block_start = jnp.arange(n_blocks, dtype=jnp.int32) * SLOT_ROWS
    block_e = jnp.minimum(jnp.sum(pad_end[None, :] <= block_start[:, None], axis=1), N_EXPERTS - 1).astype(jnp.int32)
    n_valid = jnp.clip(pad_start[block_e] + cnt[block_e] - block_start, 0, SLOT_ROWS).astype(jnp.int32)
    dest = _slot_index(pad_start, ridx, tn=2048)
    xs = _sc_scatter_rows(h2.reshape(n, d), dest[0], dest[1], n_blocks * SLOT_ROWS)
    ys = _expert_ffn(block_e, n_valid, xs, w_gate, w_up, w_down)
    yg = _sc_gather_rows(ys, dest[:TOP_K].reshape(TOP_K * n)).reshape(TOP_K, n, d)
    return _combine(yg, gcol, x1, g2, pool_args, tc=512)


def kernel(x, c, ctx, c_ctx, ada_w, ada_b, norm1_g, norm2_g, attn_w_in, attn_w_out, attn_q_gain, attn_k_gain,
           attn_lq1, attn_lk1, attn_lq2, attn_lk2, attn_sub_gain, pool_w_in, pool_w_group, pool_scale, pool_w_out,
           router_w, router_b, moe_w_gate, moe_w_up, moe_w_down):
    b, l, d = x.shape
    n_ctx = ctx.shape[1]
    depth = ada_w.shape[0]
    assert depth == 2 and d == N_HEADS * V_DIM
    tm = 512

    mod = _adaln_mod(c, c_ctx, ada_w, ada_b)
    mods = [[mod[i, :b, None, j * d:(j + 1) * d] for j in range(N_MOD)] for i in range(depth)]
    mod_ctx = [jnp.broadcast_to(mod[0, b, j * d:(j + 1) * d], (b, 1, d)) for j in range(2)]

    rwt = router_w.T
    rwh = rwt.astype(BF16)
    rwl = (rwt - rwh.astype(F32)).astype(BF16)
    rb = router_b.reshape(N_EXPERTS, 1)

    sh1, s1, g1, sh2, s2, g2 = mods[0]
    cos, sin = _rope_tables(l)
    gain2 = lambda g: jnp.concatenate([g, g])[None]
    qg, kg = gain2(attn_q_gain[0]), gain2(attn_k_gain[0])
    w_in = attn_w_in[0].astype(BF16)
    q, k, v = _qkv_proj(x, sh1, s1, norm1_g[0][None], w_in, qg, kg, cos, sin, n_qk=2, rope=True, tm=tm)
    kc, vc = _qkv_proj(ctx, mod_ctx[0], mod_ctx[1], norm1_g[0][None], w_in[:, d:], qg, kg,
                       cos[:n_ctx], sin[:n_ctx], n_qk=1, rope=False, tm=n_ctx)
    k_all = jnp.concatenate([k, kc], axis=1)
    v_all = jnp.concatenate([v, vc], axis=1)
    vt_all = v_all.reshape(b, l + n_ctx, N_HEADS, V_DIM).transpose(0, 2, 3, 1)
    ones_rows = jnp.zeros((b, N_HEADS, V_ROWS - V_DIM, l + n_ctx), BF16).at[:, :, 0].set(1.0)
    vt_all = jnp.concatenate([vt_all, ones_rows], axis=2)
    lam_init = 0.8 - 0.6 * math.exp(-0.3 * 0)
    lam_params = jnp.stack([attn_lq1[0], attn_lk1[0], attn_lq2[0], attn_lk2[0]])
    o = _diff_attention(q, k_all, vt_all, lam_params, attn_sub_gain[0][:, None], lam_init=lam_init, tq=256)

    fix = lambda bi, i: (0, 0)
    x1, h2, ridx, gcol, counts = _mixer_tail(
        (o, attn_w_out[0].astype(BF16)),
        (pl.BlockSpec((1, tm, d), lambda bi, i: (bi, i, 0)), pl.BlockSpec((d, d), fix)),
        _attn_tail_kernel, x, g1, sh2, s2, norm2_g[0][None], rwh, rwl, rb, tm=tm)

    sh1b, s1b, g1b, sh2b, s2b, g2b = mods[1]
    x2, u = _moe(h2, ridx, gcol, counts, x1, g2, moe_w_gate[0], moe_w_up[0], moe_w_down[0],
                 pool_args=(sh1b, s1b, norm1_g[1][None], pool_w_in[0].astype(BF16)))
    gd = pool_w_group.shape[2]
    nh = tm // POOL_HALO
    front_specs = (
        pl.BlockSpec((1, tm, d), lambda bi, i: (bi, i, 0)),
        pl.BlockSpec((1, POOL_HALO, d), lambda bi, i: (bi, jnp.maximum(i * nh - 1, 0), 0)),
        pl.BlockSpec((1, POOL_HALO, d), lambda bi, i: (bi, jnp.minimum((i + 1) * nh, l // POOL_HALO - 1), 0)),
        pl.BlockSpec((len(POOL_WINDOWS), gd, gd), lambda bi, i: (0, 0, 0)),
        pl.BlockSpec((1, d), fix),
        pl.BlockSpec((d, d), fix))
    x3, h2b, ridx_b, gcol_b, counts_b = _mixer_tail(
        (u, u, u, pool_w_group[0].astype(BF16), pool_scale[0][None], pool_w_out[0].astype(BF16)),
        front_specs, functools.partial(_pool_tail_kernel, seq_len=l),
        x2, g1b, sh2b, s2b, norm2_g[1][None], rwh, rwl, rb, tm=tm,
        scratch=[pltpu.VMEM((tm + 2 * POOL_HALO, d), F32)])
    (out,) = _moe(h2b, ridx_b, gcol_b, counts_b, x3, g2b, moe_w_gate[1], moe_w_up[1], moe_w_down[1])
    return out
```

```python
import functools
import math

import jax
import jax.numpy as jnp
from jax import lax
from jax.experimental import pallas as pl
from jax.experimental.pallas import tpu as pltpu
from jax.experimental.pallas import tpu_sc as plsc

F32 = jnp.float32
BF16 = jnp.bfloat16
F8 = jnp.float8_e4m3fn

LANES = 128
SUBLANES = 8
N_HEADS = 8
HEAD_DIM = 64
V_DIM = 2 * HEAD_DIM
V_ROWS = V_DIM + 32
P_SHIFT = 8.0
GRID_W = 64
ROPE_THETA = 10000.0
NORM_EPS = 1e-6
N_MOD = 6
POOL_WINDOWS = (2, 4, 8, 16)
POOL_HALO = max(POOL_WINDOWS) // 2
N_EXPERTS = 32
N_EXPERT_GROUPS = 4
EXPERTS_PER_GROUP = N_EXPERTS // N_EXPERT_GROUPS
TOP_K = 2
SLOT_ROWS = 256
SC_CORES = 2
SC_SUBCORES = 16
SC_WORKERS = SC_CORES * SC_SUBCORES
SC_WINDOW = 32
ATTN_GROUP_CHUNKS = 1
VMEM_LIMIT = 48 * 1024 * 1024
NT_DIMS = (((1,), (1,)), ((), ()))


def _params(*sem):
    return pltpu.CompilerParams(dimension_semantics=sem, vmem_limit_bytes=VMEM_LIMIT)


def _rms_mod(x, gain, shift, scale):
    h = x * lax.rsqrt(jnp.mean(x * x, axis=-1, keepdims=True) + NORM_EPS) * gain
    return h * (1.0 + scale) + shift


def _mod_kernel(c_ref, w_ref, b_ref, o_ref):
    c = c_ref[...]
    a = c / (1.0 + jnp.exp(-c))
    o_ref[0] = jnp.dot(a, w_ref[0], precision=lax.Precision.HIGHEST,
                       preferred_element_type=F32) + b_ref[0]


def _adaln_mod(c, c_ctx, ada_w, ada_b):
    depth, d, n_out = ada_w.shape
    b = c.shape[0]
    assert b + 1 <= SUBLANES
    rows = jnp.concatenate([c, c_ctx[None], jnp.zeros((SUBLANES - b - 1, d), F32)], axis=0)
    tn = n_out // 4
    return pl.pallas_call(
        _mod_kernel,
        grid=(depth, n_out // tn),
        in_specs=[pl.BlockSpec((SUBLANES, d), lambda i, j: (0, 0)),
                  pl.BlockSpec((1, d, tn), lambda i, j: (i, 0, j)),
                  pl.BlockSpec((1, 1, tn), lambda i, j: (i, 0, j))],
        out_specs=pl.BlockSpec((1, SUBLANES, tn), lambda i, j: (i, 0, j)),
        out_shape=jax.ShapeDtypeStruct((depth, SUBLANES, n_out), F32),
        compiler_params=_params("arbitrary", "arbitrary"),
        name="adaln_mod",
    )(rows, ada_w, ada_b.reshape(depth, 1, n_out))


def _qkv_kernel(x_ref, sh_ref, sc_ref, g_ref, w_ref, qg_ref, kg_ref, cos_ref, sin_ref, *out_refs,
                n_qk, rope):
    d = x_ref.shape[2]
    hb = _rms_mod(x_ref[0], g_ref[...], sh_ref[0], sc_ref[0]).astype(BF16)
    lane = lax.broadcasted_iota(jnp.int32, (1, LANES), 1)
    lane_lo = lane < HEAD_DIM
    lane_b4 = (lane & 16) == 0
    gains = (qg_ref[...], kg_ref[...])[2 - n_qk:]
    for t in range(n_qk):
        for j in range(0, d, 2 * LANES):
            acc = jnp.dot(hb, w_ref[:, t * d + j:t * d + j + 2 * LANES], preferred_element_type=F32)
            for half in range(2):
                blk = acc[:, half * LANES:(half + 1) * LANES]
                sq = blk * blk
                lo = jnp.sum(jnp.where(lane_lo, sq, 0.0), axis=-1, keepdims=True)
                hi = jnp.sum(jnp.where(lane_lo, 0.0, sq), axis=-1, keepdims=True)
                ms = jnp.where(lane_lo, lo, hi) * (1.0 / HEAD_DIM)
                y = blk * lax.rsqrt(ms + NORM_EPS) * gains[t]
                if rope:
                    rot = jnp.where(lane_b4, pltpu.roll(y, LANES - 16, 1), pltpu.roll(y, 16, 1))
                    y = y * cos_ref[...] + rot * sin_ref[...]
                c0 = j + half * LANES
                out_refs[t][0, :, c0:c0 + LANES] = y.astype(out_refs[t].dtype)
    for j in range(0, d, 2 * LANES):
        acc = jnp.dot(hb, w_ref[:, n_qk * d + j:n_qk * d + j + 2 * LANES], preferred_element_type=F32)
        out_refs[n_qk][0, :, j:j + 2 * LANES] = acc.astype(BF16)


def _qkv_proj(x, shift, scale, gain, w, q_gain, k_gain, cos, sin, *, n_qk, rope, tm):
    b, l, d = x.shape
    n_out = n_qk + 1
    out_dtypes = [BF16] * (n_qk - 1) + [F8, BF16]
    kern = functools.partial(_qkv_kernel, n_qk=n_qk, rope=rope)
    row = lambda bi, i: (bi, 0, 0)
    fix = lambda bi, i: (0, 0)
    return pl.pallas_call(
        kern,
        grid=(b, l // tm),
        in_specs=[pl.BlockSpec((1, tm, d), lambda bi, i: (bi, i, 0)),
                  pl.BlockSpec((1, 1, d), row), pl.BlockSpec((1, 1, d), row),
                  pl.BlockSpec((1, d), fix),
                  pl.BlockSpec((d, n_out * d), fix),
                  pl.BlockSpec((1, LANES), fix), pl.BlockSpec((1, LANES), fix),
                  pl.BlockSpec((tm, LANES), lambda bi, i: (i, 0)),
                  pl.BlockSpec((tm, LANES), lambda bi, i: (i, 0))],
        out_specs=[pl.BlockSpec((1, tm, d), lambda bi, i: (bi, i, 0))] * n_out,
        out_shape=[jax.ShapeDtypeStruct((b, l, d), dt) for dt in out_dtypes],
        compiler_params=_params("arbitrary", "arbitrary"),
        name="qkv_proj",
    )(x, shift, scale, gain, w, q_gain, k_gain, cos, sin)


def _rope_tables(n_tokens):
    rows = n_tokens // GRID_W
    row = jnp.repeat(jnp.arange(rows, dtype=F32), GRID_W)
    col = jnp.tile(jnp.arange(GRID_W, dtype=F32), rows)
    half = HEAD_DIM // 2
    inv_freq = ROPE_THETA ** (-jnp.arange(0, half, 2, dtype=F32) / half)
    ang_r = row[:, None] * inv_freq
    ang_c = col[:, None] * inv_freq
    ang = jnp.concatenate([ang_r, ang_r, ang_c, ang_c], axis=-1)
    ang = jnp.concatenate([ang, ang], axis=-1)
    sign = jnp.where((jnp.arange(LANES) & 16) == 0, -1.0, 1.0).astype(F32)
    return jnp.cos(ang), jnp.sin(ang) * sign


def _attn_kernel(q_ref, k_ref, vt_ref, lp_ref, sg_ref, o_ref, s_ref, *, tk, group, lam_init):
    tq = q_ref.shape[1]
    n_chunks = k_ref.shape[1] // tk
    qt = (q_ref[0].astype(F32) * (HEAD_DIM ** -0.5 * math.log2(math.e))).T
    sub = lax.broadcasted_iota(jnp.int32, (V_DIM, 1), 0)
    first = sub < HEAD_DIM
    qz = jnp.concatenate([jnp.where(first, qt, 0.0), jnp.where(first, 0.0, qt)], axis=1).astype(F8)
    def score_chunk(c, m_grp):
        st = jnp.dot(k_ref[0, c * tk:(c + 1) * tk, :], qz, preferred_element_type=F32).astype(BF16)
        s_ref[c * tk:(c + 1) * tk, :] = st
        mc = jnp.max(st, axis=0, keepdims=True)
        return mc if m_grp is None else jnp.maximum(m_grp, mc)

    def value_chunk(c, m_ref, part):
        p = jnp.exp2(s_ref[c * tk:(c + 1) * tk, :] - (m_ref - P_SHIFT)).astype(F8)
        pv = jnp.dot(vt_ref[0, 0, :, c * tk:(c + 1) * tk], p, preferred_element_type=F32)
        return pv if part is None else part + pv

    groups = [list(range(g0, min(g0 + group, n_chunks))) for g0 in range(0, n_chunks, group)]
    m_grp = None
    for c in groups[0]:
        m_grp = score_chunk(c, m_grp)
    m = None
    acc = None
    for gi, cur in enumerate(groups):
        nxt = groups[gi + 1] if gi + 1 < len(groups) else []
        m_new = m_grp if m is None else jnp.maximum(m, m_grp)
        m_grp = None
        part = None
        for i in range(max(len(cur), len(nxt))):
            if i < len(nxt):
                m_grp = score_chunk(nxt[i], m_grp)
            if i < len(cur):
                part = value_chunk(cur[i], m_new, part)
        acc = part if acc is None else acc * jnp.exp2(m.astype(F32) - m_new.astype(F32)) + part
        m = m_new
    acc = acc[:V_DIM] / acc[V_DIM:V_DIM + 1]
    lp = lp_ref[...]
    lam = (jnp.exp(jnp.sum(lp[0:1] * lp[1:2], axis=-1, keepdims=True))
           - jnp.exp(jnp.sum(lp[2:3] * lp[3:4], axis=-1, keepdims=True)) + lam_init)
    o = acc[:, :tq] - lam * acc[:, tq:]
    o = o * lax.rsqrt(jnp.mean(o * o, axis=0, keepdims=True) + NORM_EPS) * sg_ref[...] * (1.0 - lam_init)
    o_ref[0] = o.T.astype(BF16)


def _attn_chunk(n_keys):
    for tk in (768, 512, 256, 128):
        if n_keys % tk == 0:
            return tk
    raise ValueError(f"key count {n_keys} is not a multiple of {LANES}")


def _diff_attention(q, k_all, vt_all, lam_params, sub_gain, *, lam_init, tq):
    b, l, d = q.shape
    n_keys = k_all.shape[1]
    tk = _attn_chunk(n_keys)
    kern = functools.partial(_attn_kernel, tk=tk, group=ATTN_GROUP_CHUNKS, lam_init=lam_init)
    return pl.pallas_call(
        kern,
        grid=(b, N_HEADS, l // tq),
        in_specs=[pl.BlockSpec((1, tq, V_DIM), lambda bi, h, i: (bi, i, h)),
                  pl.BlockSpec((1, n_keys, V_DIM), lambda bi, h, i: (bi, 0, h)),
                  pl.BlockSpec((1, 1, V_ROWS, n_keys), lambda bi, h, i: (bi, h, 0, 0)),
                  pl.BlockSpec((4, HEAD_DIM), lambda bi, h, i: (0, 0)),
                  pl.BlockSpec((V_DIM, 1), lambda bi, h, i: (0, 0))],
        out_specs=pl.BlockSpec((1, tq, V_DIM), lambda bi, h, i: (bi, i, h)),
        out_shape=jax.ShapeDtypeStruct((b, l, d), BF16),
        scratch_shapes=[pltpu.VMEM((n_keys, 2 * tq), BF16)],
        compiler_params=_params("arbitrary", "arbitrary", "arbitrary"),
        name="diff_attention",
    )(q, k_all, vt_all, lam_params, sub_gain)


def _route(h2, rwh_ref, rwl_ref, rb_ref, carry_ref, ridx_ref, gcol_ref, cnt_ref, is_first):
    tm = h2.shape[0]
    hh = h2.astype(BF16)
    hl = (h2 - hh.astype(F32)).astype(BF16)
    rw2 = jnp.concatenate([rwh_ref[...], rwl_ref[...]], axis=0)
    part = lax.dot_general(rw2, hh, NT_DIMS, preferred_element_type=F32)
    logits = (part[:N_EXPERTS] + part[N_EXPERTS:]
              + lax.dot_general(rwh_ref[...], hl, NT_DIMS, preferred_element_type=F32) + rb_ref[...])
    groups = [logits[g * EXPERTS_PER_GROUP:(g + 1) * EXPERTS_PER_GROUP] for g in range(N_EXPERT_GROUPS)]
    top = groups[0]
    for g in range(1, N_EXPERT_GROUPS):
        top = jnp.maximum(top, groups[g])
    top = jnp.max(top, axis=0, keepdims=True)
    sub = lax.broadcasted_iota(jnp.int32, (EXPERTS_PER_GROUP, tm), 0)
    best = None
    for g in range(N_EXPERT_GROUPS):
        ex = jnp.exp(groups[g] - top)
        v1 = jnp.max(ex, axis=0, keepdims=True)
        i1 = jnp.min(jnp.where(ex == v1, sub, EXPERTS_PER_GROUP), axis=0, keepdims=True)
        rest = jnp.where(sub == i1, -1.0, ex)
        v2 = jnp.max(rest, axis=0, keepdims=True)
        i2 = jnp.min(jnp.where(rest == v2, sub, EXPERTS_PER_GROUP), axis=0, keepdims=True)
        cand = (v1 + v2, v1, v2, i1 + g * EXPERTS_PER_GROUP, i2 + g * EXPERTS_PER_GROUP)
        if best is None:
            best = cand
        else:
            better = cand[0] > best[0]
            best = tuple(jnp.where(better, new, old) for new, old in zip(cand, best))
    _, v1, v2, e0, e1 = best
    gate0 = v1 / (v1 + v2)
    gate1 = v2 / (v1 + v2)

    @pl.when(is_first)
    def _():
        carry_ref[...] = jnp.zeros_like(carry_ref)

    erow = lax.broadcasted_iota(jnp.int32, (N_EXPERTS, tm), 0)
    oh0 = erow == e0
    oh1 = erow == e1
    chosen = jnp.where(oh0 | oh1, 1.0, 0.0)
    before = (lax.broadcasted_iota(jnp.int32, (tm, tm), 0)
              < lax.broadcasted_iota(jnp.int32, (tm, tm), 1)).astype(BF16)
    prior = carry_ref[:, 0:1] + jnp.dot(chosen.astype(BF16), before, preferred_element_type=F32)
    r0 = jnp.sum(jnp.where(oh0, prior, 0.0), axis=0, keepdims=True).astype(jnp.int32)
    r1 = jnp.sum(jnp.where(oh1, prior, 0.0), axis=0, keepdims=True).astype(jnp.int32)
    carry_ref[...] = carry_ref[...] + jnp.sum(chosen, axis=1, keepdims=True)
    cnt_ref[...] = carry_ref[...]
    rid = lax.broadcasted_iota(jnp.int32, (SUBLANES, tm), 0)
    ridx_ref[...] = jnp.where(rid == 0, e0, jnp.where(rid == 1, e1, jnp.where(rid == 2, r0, jnp.where(rid == 3, r1, 0))))
    gid = lax.broadcasted_iota(jnp.int32, (LANES, tm), 0)
    gcol_ref[...] = jnp.where(gid == 0, gate0, jnp.where(gid == 1, gate1, 0.0)).T


def _tail(y, x_ref, g1_ref, sh2_ref, s2_ref, n2_ref, rwh_ref, rwl_ref, rb_ref,
          x_out, h2_out, ridx_ref, gcol_ref, cnt_ref, carry_ref):
    x1 = x_ref[0] + g1_ref[0] * y
    x_out[0] = x1
    h2 = _rms_mod(x1, n2_ref[...], sh2_ref[0], s2_ref[0])
    h2_out[0] = h2
    is_first = (pl.program_id(0) == 0) & (pl.program_id(1) == 0)
    _route(h2, rwh_ref, rwl_ref, rb_ref, carry_ref, ridx_ref, gcol_ref, cnt_ref, is_first)


def _attn_tail_kernel(a_ref, wo_ref, *rest):
    y = jnp.dot(a_ref[0], wo_ref[...], preferred_element_type=F32)
    _tail(y, *rest)


def _pool_tail_kernel(u_ref, up_ref, un_ref, wg_ref, cs_ref, wo_ref, *rest, seq_len):
    *tail_refs, ubuf = rest
    tm = u_ref.shape[1]
    i = pl.program_id(1)
    u = u_ref[0]
    ubuf[0:POOL_HALO] = jnp.where(i > 0, up_ref[0], 0.0)
    ubuf[POOL_HALO:POOL_HALO + tm] = u
    ubuf[POOL_HALO + tm:2 * POOL_HALO + tm] = jnp.where(i < pl.num_programs(1) - 1, un_ref[0], 0.0)
    pos = i * tm + lax.broadcasted_iota(jnp.int32, (tm, 1), 0)
    gd = wg_ref.shape[1]
    outs = []
    for g, win in enumerate(POOL_WINDOWS):
        half = win // 2
        cols = slice(g * gd, (g + 1) * gd)
        s = ubuf[POOL_HALO - half:POOL_HALO - half + tm, cols]
        for j in range(1 - half, half):
            s = s + ubuf[POOL_HALO + j:POOL_HALO + j + tm, cols]
        cnt = (jnp.minimum(pos + half, seq_len) - jnp.maximum(pos - half, 0)).astype(F32)
        dlt = (s / cnt - u[:, cols]).astype(BF16)
        outs.append(jnp.dot(dlt, wg_ref[g], preferred_element_type=F32))
    z = (jnp.concatenate(outs, axis=-1) * cs_ref[...]).astype(BF16)
    y = jnp.dot(z, wo_ref[...], preferred_element_type=F32)
    _tail(y, *tail_refs)


def _mixer_tail(front_args, front_specs, kern, x, g1, sh2, s2, n2g, rwh, rwl, rb, *, tm, scratch=()):
    b, l, d = x.shape
    nt = l // tm
    n = b * l
    row = lambda bi, i: (bi, 0, 0)
    fix = lambda bi, i: (0, 0)
    tile = lambda bi, i: (bi, i, 0)
    in_specs = list(front_specs) + [
        pl.BlockSpec((1, tm, d), tile),
        pl.BlockSpec((1, 1, d), row), pl.BlockSpec((1, 1, d), row), pl.BlockSpec((1, 1, d), row),
        pl.BlockSpec((1, d), fix),
        pl.BlockSpec((N_EXPERTS, d), fix), pl.BlockSpec((N_EXPERTS, d), fix),
        pl.BlockSpec((N_EXPERTS, 1), fix)]
    out_specs = [pl.BlockSpec((1, tm, d), tile), pl.BlockSpec((1, tm, d), tile),
                 pl.BlockSpec((SUBLANES, tm), lambda bi, i: (0, bi * nt + i)),
                 pl.BlockSpec((tm, LANES), lambda bi, i: (bi * nt + i, 0)),
                 pl.BlockSpec((N_EXPERTS, LANES), fix)]
    out_shape = [jax.ShapeDtypeStruct((b, l, d), F32), jax.ShapeDtypeStruct((b, l, d), F32),
                 jax.ShapeDtypeStruct((SUBLANES, n), jnp.int32), jax.ShapeDtypeStruct((n, LANES), F32),
                 jax.ShapeDtypeStruct((N_EXPERTS, LANES), F32)]
    return pl.pallas_call(
        kern,
        grid=(b, nt),
        in_specs=in_specs, out_specs=out_specs, out_shape=out_shape,
        scratch_shapes=[pltpu.VMEM((N_EXPERTS, LANES), F32)] + list(scratch),
        compiler_params=_params("arbitrary", "arbitrary"),
        name="mixer_tail",
    )(*front_args, x, g1, sh2, s2, n2g, rwh, rwl, rb)


def _slot_kernel(ps_ref, ridx_ref, dest_ref):
    ridx = ridx_ref[...]
    ps = ps_ref[...]
    erow = lax.broadcasted_iota(jnp.int32, (N_EXPERTS, ridx.shape[1]), 0)
    rows = []
    for k in range(TOP_K):
        start = jnp.sum(jnp.where(erow == ridx[k:k + 1], ps, 0), axis=0, keepdims=True)
        rows.append(start + ridx[TOP_K + k:TOP_K + k + 1])
    rid = lax.broadcasted_iota(jnp.int32, ridx.shape, 0)
    dest_ref[...] = jnp.where(rid == 0, rows[0], jnp.where(rid == 1, rows[1], 0))


def _slot_index(pad_start, ridx, *, tn):
    n = ridx.shape[1]
    return pl.pallas_call(
        _slot_kernel,
        grid=(n // tn,),
        in_specs=[pl.BlockSpec((N_EXPERTS, 1), lambda i: (0, 0)),
                  pl.BlockSpec((SUBLANES, tn), lambda i: (0, i))],
        out_specs=pl.BlockSpec((SUBLANES, tn), lambda i: (0, i)),
        out_shape=jax.ShapeDtypeStruct((SUBLANES, n), jnp.int32),
        compiler_params=_params("arbitrary"),
        name="slot_index",
    )(pad_start.reshape(N_EXPERTS, 1), ridx)


def _sc_mesh():
    return plsc.VectorSubcoreMesh(core_axis_name="c", subcore_axis_name="s",
                                  num_cores=SC_CORES, num_subcores=SC_SUBCORES)


def _sc_worker_base(per_worker):
    return (lax.axis_index("s") * SC_CORES + lax.axis_index("c")) * per_worker


def _sc_scatter_rows(rows, idx0, idx1, n_slots):
    n, d = rows.shape
    per_worker = n // SC_WORKERS
    assert per_worker % SC_WINDOW == 0

    def body(rows_hbm, i0_hbm, i1_hbm, out_hbm, i0_v, i1_v, rows_v):
        base = _sc_worker_base(per_worker)

        @pl.loop(0, per_worker // SC_WINDOW)
        def _(j):
            off = pl.multiple_of(base + j * SC_WINDOW, SC_WINDOW)
            pltpu.sync_copy(i0_hbm.at[pl.ds(off, SC_WINDOW)], i0_v)
            pltpu.sync_copy(i1_hbm.at[pl.ds(off, SC_WINDOW)], i1_v)
            pltpu.sync_copy(rows_hbm.at[pl.ds(off, SC_WINDOW)], rows_v)
            pltpu.sync_copy(rows_v, out_hbm.at[i0_v])
            pltpu.sync_copy(rows_v, out_hbm.at[i1_v])

    return pl.kernel(
        body, out_type=jax.ShapeDtypeStruct((n_slots, d), rows.dtype), mesh=_sc_mesh(),
        scratch_types=[pltpu.VMEM((SC_WINDOW,), jnp.int32), pltpu.VMEM((SC_WINDOW,), jnp.int32),
                       pltpu.VMEM((SC_WINDOW, d), rows.dtype)],
        name="sc_scatter_rows",
    )(rows, idx0, idx1)


def _sc_gather_rows(table, idx):
    n = idx.shape[0]
    d = table.shape[1]
    per_worker = n // SC_WORKERS
    assert per_worker % SC_WINDOW == 0

    def body(table_hbm, idx_hbm, out_hbm, idx_v, rows_v, sem):
        base = _sc_worker_base(per_worker)

        @pl.loop(0, per_worker // SC_WINDOW)
        def _(j):
            off = pl.multiple_of(base + j * SC_WINDOW, SC_WINDOW)
            pltpu.sync_copy(idx_hbm.at[pl.ds(off, SC_WINDOW)], idx_v)
            pltpu.async_copy(table_hbm.at[idx_v], rows_v, sem).wait()
            pltpu.sync_copy(rows_v, out_hbm.at[pl.ds(off, SC_WINDOW)])

    return pl.kernel(
        body, out_type=jax.ShapeDtypeStruct((n, d), table.dtype), mesh=_sc_mesh(),
        scratch_types=[pltpu.VMEM((SC_WINDOW,), jnp.int32), pltpu.VMEM((SC_WINDOW, d), table.dtype),
                       pltpu.SemaphoreType.DMA],
        name="sc_gather_rows",
    )(table, idx)


def _ffn_kernel(be_ref, nv_ref, xs_ref, wg_ref, wu_ref, wd_ref, ys_ref):
    del be_ref
    valid = nv_ref[pl.program_id(0)]

    @pl.when(valid > 0)
    def _():
        row = lax.broadcasted_iota(jnp.int32, (SLOT_ROWS, 1), 0)
        xb = jnp.where(row < valid, xs_ref[...], 0.0).astype(BF16)
        g = jnp.dot(xb, wg_ref[0].astype(BF16), preferred_element_type=F32)
        u = jnp.dot(xb, wu_ref[0].astype(BF16), preferred_element_type=F32)
        a = (g / (1.0 + jnp.exp(-g)) * u).astype(BF16)
        ys_ref[...] = jnp.dot(a, wd_ref[0].astype(BF16), preferred_element_type=F32)

    @pl.when(valid <= 0)
    def _():
        ys_ref[...] = jnp.zeros_like(ys_ref)


def _expert_ffn(block_e, n_valid, xs, w_gate, w_up, w_down):
    n_slots, d = xs.shape
    de = w_gate.shape[2]
    return pl.pallas_call(
        _ffn_kernel,
        grid_spec=pltpu.PrefetchScalarGridSpec(
            num_scalar_prefetch=2,
            grid=(n_slots // SLOT_ROWS,),
            in_specs=[pl.BlockSpec((SLOT_ROWS, d), lambda j, be, nv: (j, 0)),
                      pl.BlockSpec((1, d, de), lambda j, be, nv: (be[j], 0, 0)),
                      pl.BlockSpec((1, d, de), lambda j, be, nv: (be[j], 0, 0)),
                      pl.BlockSpec((1, de, d), lambda j, be, nv: (be[j], 0, 0))],
            out_specs=pl.BlockSpec((SLOT_ROWS, d), lambda j, be, nv: (j, 0))),
        out_shape=jax.ShapeDtypeStruct((n_slots, d), F32),
        compiler_params=_params("arbitrary"),
        name="expert_ffn",
    )(block_e, n_valid, xs, w_gate, w_up, w_down)


def _combine_kernel(y0_ref, y1_ref, gcol_ref, x_ref, g2_ref, *rest, pool_in):
    if pool_in:
        sh_ref, sc_ref, n1_ref, wi_ref, x_out, u_out = rest
    else:
        (x_out,) = rest
    gc = gcol_ref[...]
    out = gc[:, 0:1] * y0_ref[0] + gc[:, 1:2] * y1_ref[0]
    x2 = x_ref[0] + g2_ref[0] * out
    x_out[0] = x2
    if pool_in:
        hb = _rms_mod(x2, n1_ref[...], sh_ref[0], sc_ref[0]).astype(BF16)
        u_out[0] = jnp.dot(hb, wi_ref[...], preferred_element_type=F32)


def _combine(yg, gcol, x, g2, pool_args=None, *, tc):
    b, l, d = x.shape
    nt = l // tc
    pool_in = pool_args is not None
    kern = functools.partial(_combine_kernel, pool_in=pool_in)
    row = lambda bi, i: (bi, 0, 0)
    fix = lambda bi, i: (0, 0)
    tile = lambda bi, i: (bi, i, 0)
    in_specs = [pl.BlockSpec((1, tc, d), lambda bi, i: (0, bi * nt + i, 0)),
                pl.BlockSpec((1, tc, d), lambda bi, i: (1, bi * nt + i, 0)),
                pl.BlockSpec((tc, LANES), lambda bi, i: (bi * nt + i, 0)),
                pl.BlockSpec((1, tc, d), tile),
                pl.BlockSpec((1, 1, d), row)]
    out_specs = [pl.BlockSpec((1, tc, d), tile)]
    out_shape = [jax.ShapeDtypeStruct((b, l, d), F32)]
    args = [yg, yg, gcol, x, g2]
    if pool_in:
        in_specs += [pl.BlockSpec((1, 1, d), row), pl.BlockSpec((1, 1, d), row),
                     pl.BlockSpec((1, d), fix), pl.BlockSpec((d, d), fix)]
        out_specs.append(pl.BlockSpec((1, tc, d), tile))
        out_shape.append(jax.ShapeDtypeStruct((b, l, d), F32))
        args += list(pool_args)
    return pl.pallas_call(
        kern,
        grid=(b, nt),
        in_specs=in_specs, out_specs=out_specs, out_shape=out_shape,
        compiler_params=_params("arbitrary", "arbitrary"),
        name="moe_combine",
    )(*args)


def _moe(h2, ridx, gcol, counts, x1, g2, w_gate, w_up, w_down, pool_args=None):
    b, l, d = x1.shape
    n = b * l
    n_blocks = (n * TOP_K) // SLOT_ROWS + N_EXPERTS
    cnt = counts[:, 0].astype(jnp.int32)
    padded = (cnt + SLOT_ROWS - 1) // SLOT_ROWS * SLOT_ROWS
    pad_end = jnp.cumsum(padded)
    pad_start = (pad_end - padded).astype(jnp.int32)
    block_start = jnp.arange(n_blocks, dtype=jnp.int32) * SLOT_ROWS
    block_e = jnp.minimum(jnp.sum(pad_end[None, :] <= block_start[:, None], axis=1), N_EXPERTS - 1).astype(jnp.int32)
    n_valid = jnp.clip(pad_start[block_e] + cnt[block_e] - block_start, 0, SLOT_ROWS).astype(jnp.int32)
    dest = _slot_index(pad_start, ridx, tn=2048)
    xs = _sc_scatter_rows(h2.reshape(n, d), dest[0], dest[1], n_blocks * SLOT_ROWS)
    ys = _expert_ffn(block_e, n_valid, xs, w_gate, w_up, w_down)
    yg = _sc_gather_rows(ys, dest[:TOP_K].reshape(TOP_K * n)).reshape(TOP_K, n, d)
    return _combine(yg, gcol, x1, g2, pool_args, tc=512)


def kernel(x, c, ctx, c_ctx, ada_w, ada_b, norm1_g, norm2_g, attn_w_in, attn_w_out, attn_q_gain, attn_k_gain,
           attn_lq1, attn_lk1, attn_lq2, attn_lk2, attn_sub_gain, pool_w_in, pool_w_group, pool_scale, pool_w_out,
           router_w, router_b, moe_w_gate, moe_w_up, moe_w_down):
    b, l, d = x.shape
    n_ctx = ctx.shape[1]
    depth = ada_w.shape[0]
    assert depth == 2 and d == N_HEADS * V_DIM
    tm = 512

    mod = _adaln_mod(c, c_ctx, ada_w, ada_b)
    mods = [[mod[i, :b, None, j * d:(j + 1) * d] for j in range(N_MOD)] for i in range(depth)]
    mod_ctx = [jnp.broadcast_to(mod[0, b, j * d:(j + 1) * d], (b, 1, d)) for j in range(2)]

    rwt = router_w.T
    rwh = rwt.astype(BF16)
    rwl = (rwt - rwh.astype(F32)).astype(BF16)
    rb = router_b.reshape(N_EXPERTS, 1)

    sh1, s1, g1, sh2, s2, g2 = mods[0]
    cos, sin = _rope_tables(l)
    gain2 = lambda g: jnp.concatenate([g, g])[None]
    qg, kg = gain2(attn_q_gain[0]), gain2(attn_k_gain[0])
    w_in = attn_w_in[0].astype(BF16)
    q, k, v = _qkv_proj(x, sh1, s1, norm1_g[0][None], w_in, qg, kg, cos, sin, n_qk=2, rope=True, tm=tm)
    kc, vc = _qkv_proj(ctx, mod_ctx[0], mod_ctx[1], norm1_g[0][None], w_in[:, d:], qg, kg,
                       cos[:n_ctx], sin[:n_ctx], n_qk=1, rope=False, tm=n_ctx)
    k_all = jnp.concatenate([k, kc], axis=1)
    v_all = jnp.concatenate([v, vc], axis=1)
    vt_all = v_all.reshape(b, l + n_ctx, N_HEADS, V_DIM).transpose(0, 2, 3, 1)
    ones_rows = jnp.zeros((b, N_HEADS, V_ROWS - V_DIM, l + n_ctx), BF16).at[:, :, 0].set(1.0)
    vt_all = jnp.concatenate([vt_all, ones_rows], axis=2).astype(F8)
    lam_init = 0.8 - 0.6 * math.exp(-0.3 * 0)
    lam_params = jnp.stack([attn_lq1[0], attn_lk1[0], attn_lq2[0], attn_lk2[0]])
    o = _diff_attention(q, k_all, vt_all, lam_params, attn_sub_gain[0][:, None], lam_init=lam_init, tq=512)

    fix = lambda bi, i: (0, 0)
    x1, h2, ridx, gcol, counts = _mixer_tail(
        (o, attn_w_out[0].astype(BF16)),
        (pl.BlockSpec((1, tm, d), lambda bi, i: (bi, i, 0)), pl.BlockSpec((d, d), fix)),
        _attn_tail_kernel, x, g1, sh2, s2, norm2_g[0][None], rwh, rwl, rb, tm=tm)

    sh1b, s1b, g1b, sh2b, s2b, g2b = mods[1]
    x2, u = _moe(h2, ridx, gcol, counts, x1, g2, moe_w_gate[0], moe_w_up[0], moe_w_down[0],
                 pool_args=(sh1b, s1b, norm1_g[1][None], pool_w_in[0].astype(BF16)))
    gd = pool_w_group.shape[2]
    nh = tm // POOL_HALO
    front_specs = (
        pl.BlockSpec((1, tm, d), lambda bi, i: (bi, i, 0)),
        pl.BlockSpec((1, POOL_HALO, d), lambda bi, i: (bi, jnp.maximum(i * nh - 1, 0), 0)),
        pl.BlockSpec((1, POOL_HALO, d), lambda bi, i: (bi, jnp.minimum((i + 1) * nh, l // POOL_HALO - 1), 0)),
        pl.BlockSpec((len(POOL_WINDOWS), gd, gd), lambda bi, i: (0, 0, 0)),
        pl.BlockSpec((1, d), fix),
        pl.BlockSpec((d, d), fix))
    x3, h2b, ridx_b, gcol_b, counts_b = _mixer_tail(
        (u, u, u, pool_w_group[0].astype(BF16), pool_scale[0][None], pool_w_out[0].astype(BF16)),
        front_specs, functools.partial(_pool_tail_kernel, seq_len=l),
        x2, g1b, sh2b, s2b, norm2_g[1][None], rwh, rwl, rb, tm=tm,
        scratch=[pltpu.VMEM((tm + 2 * POOL_HALO, d), F32)])
    (out,) = _moe(h2b, ridx_b, gcol_b, counts_b, x3, g2b, moe_w_gate[1], moe_w_up[1], moe_w_down[1])
    return out
```

```python
import functools
import math

import jax
import jax.numpy as jnp
from jax import lax
from jax.experimental import pallas as pl
from jax.experimental.pallas import tpu as pltpu
from jax.experimental.pallas import tpu_sc as plsc

F32 = jnp.float32
BF16 = jnp.bfloat16
F8 = jnp.float8_e4m3fn

LANES = 128
SUBLANES = 8
N_HEADS = 8
HEAD_DIM = 64
V_DIM = 2 * HEAD_DIM
V_ROWS = V_DIM + 32
P_SHIFT = 8.0
GRID_W = 64
ROPE_THETA = 10000.0
NORM_EPS = 1e-6
N_MOD = 6
POOL_WINDOWS = (2, 4, 8, 16)
POOL_HALO = max(POOL_WINDOWS) // 2
N_EXPERTS = 32
N_EXPERT_GROUPS = 4
EXPERTS_PER_GROUP = N_EXPERTS // N_EXPERT_GROUPS
TOP_K = 2
SLOT_ROWS = 512
SC_CORES = 2
SC_SUBCORES = 16
SC_WORKERS = SC_CORES * SC_SUBCORES
SC_WINDOW = 32
ATTN_GROUP_CHUNKS = 1
VMEM_LIMIT = 48 * 1024 * 1024
NT_DIMS = (((1,), (1,)), ((), ()))


def _params(*sem):
    return pltpu.CompilerParams(dimension_semantics=sem, vmem_limit_bytes=VMEM_LIMIT)


def _rms_mod(x, gain, shift, scale):
    h = x * lax.rsqrt(jnp.mean(x * x, axis=-1, keepdims=True) + NORM_EPS) * gain
    return h * (1.0 + scale) + shift


def _mod_kernel(c_ref, w_ref, b_ref, o_ref):
    c = c_ref[...]
    a = c / (1.0 + jnp.exp(-c))
    o_ref[0] = jnp.dot(a, w_ref[0], precision=lax.Precision.HIGHEST,
                       preferred_element_type=F32) + b_ref[0]


def _adaln_mod(c, c_ctx, ada_w, ada_b):
    depth, d, n_out = ada_w.shape
    b = c.shape[0]
    assert b + 1 <= SUBLANES
    rows = jnp.concatenate([c, c_ctx[None], jnp.zeros((SUBLANES - b - 1, d), F32)], axis=0)
    tn = n_out // 4
    return pl.pallas_call(
        _mod_kernel,
        grid=(depth, n_out // tn),
        in_specs=[pl.BlockSpec((SUBLANES, d), lambda i, j: (0, 0)),
                  pl.BlockSpec((1, d, tn), lambda i, j: (i, 0, j)),
                  pl.BlockSpec((1, 1, tn), lambda i, j: (i, 0, j))],
        out_specs=pl.BlockSpec((1, SUBLANES, tn), lambda i, j: (i, 0, j)),
        out_shape=jax.ShapeDtypeStruct((depth, SUBLANES, n_out), F32),
        compiler_params=_params("arbitrary", "arbitrary"),
        name="adaln_mod",
    )(rows, ada_w, ada_b.reshape(depth, 1, n_out))


def _qkv_kernel(x_ref, sh_ref, sc_ref, g_ref, w_ref, qg_ref, kg_ref, cos_ref, sin_ref, *refs,
                n_qk, rope):
    out_refs = refs[-(n_qk + 1):]
    tm, d = x_ref.shape[1:]
    hb = _rms_mod(x_ref[0], g_ref[...], sh_ref[0], sc_ref[0]).astype(BF16)
    lane = lax.broadcasted_iota(jnp.int32, (1, LANES), 1)
    lane_lo = lane < HEAD_DIM
    lane_b4 = (lane & 16) == 0
    gains = (qg_ref[...], kg_ref[...])[2 - n_qk:]
    for t in range(n_qk):
        for j in range(0, d, 2 * LANES):
            acc = jnp.dot(hb, w_ref[:, t * d + j:t * d + j + 2 * LANES], preferred_element_type=F32)
            for half in range(2):
                blk = acc[:, half * LANES:(half + 1) * LANES]
                sq = blk * blk
                lo = jnp.sum(jnp.where(lane_lo, sq, 0.0), axis=-1, keepdims=True)
                hi = jnp.sum(jnp.where(lane_lo, 0.0, sq), axis=-1, keepdims=True)
                ms = jnp.where(lane_lo, lo, hi) * (1.0 / HEAD_DIM)
                y = blk * lax.rsqrt(ms + NORM_EPS) * gains[t]
                if rope:
                    rot = jnp.where(lane_b4, pltpu.roll(y, LANES - 16, 1), pltpu.roll(y, 16, 1))
                    y = y * cos_ref[...] + rot * sin_ref[...]
                c0 = j + half * LANES
                out_refs[t][0, :, c0:c0 + LANES] = y.astype(out_refs[t].dtype)
    vt_ref = out_refs[n_qk]
    ones = jnp.where(lax.broadcasted_iota(jnp.int32, (V_ROWS - V_DIM, tm), 0) == 0, 1.0, 0.0).astype(F8)
    for j in range(0, d, 2 * LANES):
        acc = jnp.dot(hb, w_ref[:, n_qk * d + j:n_qk * d + j + 2 * LANES], preferred_element_type=F32)
        acc_t = acc.T
        for half in range(2):
            h = j // LANES + half
            vt_ref[0, h, 0:V_DIM, :] = acc_t[half * V_DIM:(half + 1) * V_DIM].astype(F8)
            vt_ref[0, h, V_DIM:V_ROWS, :] = ones


def _qkv_proj(x, shift, scale, gain, w, q_gain, k_gain, cos, sin, *, n_qk, rope, tm, n_keys, key_row0, kv=None):
    b, l, d = x.shape
    n_out = n_qk + 1
    kern = functools.partial(_qkv_kernel, n_qk=n_qk, rope=rope)
    row = lambda bi, i: (bi, 0, 0)
    fix = lambda bi, i: (0, 0)
    kb = key_row0 // tm
    tile = pl.BlockSpec((1, tm, d), lambda bi, i: (bi, i, 0))
    k_spec = pl.BlockSpec((1, tm, d), lambda bi, i: (bi, kb + i, 0))
    vt_spec = pl.BlockSpec((1, N_HEADS, V_ROWS, tm), lambda bi, i: (bi, 0, 0, kb + i))
    k_shape = jax.ShapeDtypeStruct((b, n_keys, d), F8)
    vt_shape = jax.ShapeDtypeStruct((b, N_HEADS, V_ROWS, n_keys), F8)
    in_specs = [tile,
                pl.BlockSpec((1, 1, d), row), pl.BlockSpec((1, 1, d), row),
                pl.BlockSpec((1, d), fix),
                pl.BlockSpec((d, n_out * d), fix),
                pl.BlockSpec((1, LANES), fix), pl.BlockSpec((1, LANES), fix),
                pl.BlockSpec((tm, LANES), lambda bi, i: (i, 0)),
                pl.BlockSpec((tm, LANES), lambda bi, i: (i, 0))]
    args = [x, shift, scale, gain, w, q_gain, k_gain, cos, sin]
    aliases = {}
    if kv is not None:
        aliases = {len(args): n_qk - 1, len(args) + 1: n_qk}
        in_specs += [pl.BlockSpec(memory_space=pl.ANY)] * 2
        args += list(kv)
    return pl.pallas_call(
        kern,
        grid=(b, l // tm),
        in_specs=in_specs,
        out_specs=[tile] * (n_qk - 1) + [k_spec, vt_spec],
        out_shape=[jax.ShapeDtypeStruct((b, l, d), BF16)] * (n_qk - 1) + [k_shape, vt_shape],
        input_output_aliases=aliases,
        compiler_params=_params("arbitrary", "arbitrary"),
        name="qkv_proj",
    )(*args)


def _rope_tables(n_tokens):
    rows = n_tokens // GRID_W
    row = jnp.repeat(jnp.arange(rows, dtype=F32), GRID_W)
    col = jnp.tile(jnp.arange(GRID_W, dtype=F32), rows)
    half = HEAD_DIM // 2
    inv_freq = ROPE_THETA ** (-jnp.arange(0, half, 2, dtype=F32) / half)
    ang_r = row[:, None] * inv_freq
    ang_c = col[:, None] * inv_freq
    ang = jnp.concatenate([ang_r, ang_r, ang_c, ang_c], axis=-1)
    ang = jnp.concatenate([ang, ang], axis=-1)
    sign = jnp.where((jnp.arange(LANES) & 16) == 0, -1.0, 1.0).astype(F32)
    return jnp.cos(ang), jnp.sin(ang) * sign


def _attn_kernel(q_ref, k_ref, vt_ref, lp_ref, sg_ref, o_ref, s_ref, *, tk, group, lam_init):
    tq = q_ref.shape[1]
    n_chunks = k_ref.shape[1] // tk
    qt = (q_ref[0].astype(F32) * (HEAD_DIM ** -0.5 * math.log2(math.e))).T
    sub = lax.broadcasted_iota(jnp.int32, (V_DIM, 1), 0)
    first = sub < HEAD_DIM
    qz = jnp.concatenate([jnp.where(first, qt, 0.0), jnp.where(first, 0.0, qt)], axis=1).astype(F8)
    def score_chunk(c, m_grp):
        st = jnp.dot(k_ref[0, c * tk:(c + 1) * tk, :], qz, preferred_element_type=F32).astype(BF16)
        s_ref[c * tk:(c + 1) * tk, :] = st
        mc = jnp.max(st, axis=0, keepdims=True)
        return mc if m_grp is None else jnp.maximum(m_grp, mc)

    def value_chunk(c, m_ref, part):
        p = jnp.exp2(s_ref[c * tk:(c + 1) * tk, :] - (m_ref - P_SHIFT)).astype(F8)
        pv = jnp.dot(vt_ref[0, 0, :, c * tk:(c + 1) * tk], p, preferred_element_type=F32)
        return pv if part is None else part + pv

    groups = [list(range(g0, min(g0 + group, n_chunks))) for g0 in range(0, n_chunks, group)]
    m_grp = None
    for c in groups[0]:
        m_grp = score_chunk(c, m_grp)
    m = None
    acc = None
    for gi, cur in enumerate(groups):
        nxt = groups[gi + 1] if gi + 1 < len(groups) else []
        m_new = m_grp if m is None else jnp.maximum(m, m_grp)
        m_grp = None
        part = None
        for i in range(max(len(cur), len(nxt))):
            if i < len(nxt):
                m_grp = score_chunk(nxt[i], m_grp)
            if i < len(cur):
                part = value_chunk(cur[i], m_new, part)
        acc = part if acc is None else acc * jnp.exp2(m.astype(F32) - m_new.astype(F32)) + part
        m = m_new
    acc = acc[:V_DIM] / acc[V_DIM:V_DIM + 1]
    lp = lp_ref[...]
    lam = (jnp.exp(jnp.sum(lp[0:1] * lp[1:2], axis=-1, keepdims=True))
           - jnp.exp(jnp.sum(lp[2:3] * lp[3:4], axis=-1, keepdims=True)) + lam_init)
    o = acc[:, :tq] - lam * acc[:, tq:]
    o = o * lax.rsqrt(jnp.mean(o * o, axis=0, keepdims=True) + NORM_EPS) * sg_ref[...] * (1.0 - lam_init)
    o_ref[0] = o.T.astype(BF16)


def _attn_chunk(n_keys):
    for tk in (768, 512, 256, 128):
        if n_keys % tk == 0:
            return tk
    raise ValueError(f"key count {n_keys} is not a multiple of {LANES}")


def _diff_attention(q, k_all, vt_all, lam_params, sub_gain, *, lam_init, tq):
    b, l, d = q.shape
    n_keys = k_all.shape[1]
    tk = _attn_chunk(n_keys)
    kern = functools.partial(_attn_kernel, tk=tk, group=ATTN_GROUP_CHUNKS, lam_init=lam_init)
    return pl.pallas_call(
        kern,
        grid=(b, N_HEADS, l // tq),
        in_specs=[pl.BlockSpec((1, tq, V_DIM), lambda bi, h, i: (bi, i, h)),
                  pl.BlockSpec((1, n_keys, V_DIM), lambda bi, h, i: (bi, 0, h)),
                  pl.BlockSpec((1, 1, V_ROWS, n_keys), lambda bi, h, i: (bi, h, 0, 0)),
                  pl.BlockSpec((4, HEAD_DIM), lambda bi, h, i: (0, 0)),
                  pl.BlockSpec((V_DIM, 1), lambda bi, h, i: (0, 0))],
        out_specs=pl.BlockSpec((1, tq, V_DIM), lambda bi, h, i: (bi, i, h)),
        out_shape=jax.ShapeDtypeStruct((b, l, d), BF16),
        scratch_shapes=[pltpu.VMEM((n_keys, 2 * tq), BF16)],
        compiler_params=_params("arbitrary", "arbitrary", "arbitrary"),
        name="diff_attention",
    )(q, k_all, vt_all, lam_params, sub_gain)


def _route(h2, rwh_ref, rwl_ref, rb_ref, carry_ref, ridx_ref, gcol_ref, cnt_ref, is_first):
    tm = h2.shape[0]
    hh = h2.astype(BF16)
    hl = (h2 - hh.astype(F32)).astype(BF16)
    rw2 = jnp.concatenate([rwh_ref[...], rwl_ref[...]], axis=0)
    part = lax.dot_general(rw2, hh, NT_DIMS, preferred_element_type=F32)
    logits = (part[:N_EXPERTS] + part[N_EXPERTS:]
              + lax.dot_general(rwh_ref[...], hl, NT_DIMS, preferred_element_type=F32) + rb_ref[...])
    groups = [logits[g * EXPERTS_PER_GROUP:(g + 1) * EXPERTS_PER_GROUP] for g in range(N_EXPERT_GROUPS)]
    top = groups[0]
    for g in range(1, N_EXPERT_GROUPS):
        top = jnp.maximum(top, groups[g])
    top = jnp.max(top, axis=0, keepdims=True)
    sub = lax.broadcasted_iota(jnp.int32, (EXPERTS_PER_GROUP, tm), 0)
    best = None
    for g in range(N_EXPERT_GROUPS):
        ex = jnp.exp(groups[g] - top)
        v1 = jnp.max(ex, axis=0, keepdims=True)
        i1 = jnp.min(jnp.where(ex == v1, sub, EXPERTS_PER_GROUP), axis=0, keepdims=True)
        rest = jnp.where(sub == i1, -1.0, ex)
        v2 = jnp.max(rest, axis=0, keepdims=True)
        i2 = jnp.min(jnp.where(rest == v2, sub, EXPERTS_PER_GROUP), axis=0, keepdims=True)
        cand = (v1 + v2, v1, v2, i1 + g * EXPERTS_PER_GROUP, i2 + g * EXPERTS_PER_GROUP)
        if best is None:
            best = cand
        else:
            better = cand[0] > best[0]
            best = tuple(jnp.where(better, new, old) for new, old in zip(cand, best))
    _, v1, v2, e0, e1 = best
    gate0 = v1 / (v1 + v2)
    gate1 = v2 / (v1 + v2)

    @pl.when(is_first)
    def _():
        carry_ref[...] = jnp.zeros_like(carry_ref)

    erow = lax.broadcasted_iota(jnp.int32, (N_EXPERTS, tm), 0)
    oh0 = erow == e0
    oh1 = erow == e1
    chosen = jnp.where(oh0 | oh1, 1.0, 0.0)
    before = (lax.broadcasted_iota(jnp.int32, (tm, tm), 0)
              < lax.broadcasted_iota(jnp.int32, (tm, tm), 1)).astype(BF16)
    prior = carry_ref[:, 0:1] + jnp.dot(chosen.astype(BF16), before, preferred_element_type=F32)
    r0 = jnp.sum(jnp.where(oh0, prior, 0.0), axis=0, keepdims=True).astype(jnp.int32)
    r1 = jnp.sum(jnp.where(oh1, prior, 0.0), axis=0, keepdims=True).astype(jnp.int32)
    carry_ref[...] = carry_ref[...] + jnp.sum(chosen, axis=1, keepdims=True)
    cnt_ref[...] = carry_ref[...]
    rid = lax.broadcasted_iota(jnp.int32, (SUBLANES, tm), 0)
    ridx_ref[...] = jnp.where(rid == 0, e0, jnp.where(rid == 1, e1, jnp.where(rid == 2, r0, jnp.where(rid == 3, r1, 0))))
    gid = lax.broadcasted_iota(jnp.int32, (LANES, tm), 0)
    gcol_ref[...] = jnp.where(gid == 0, gate0, jnp.where(gid == 1, gate1, 0.0)).T


def _tail(y, x_ref, g1_ref, sh2_ref, s2_ref, n2_ref, rwh_ref, rwl_ref, rb_ref,
          x_out, h2_out, ridx_ref, gcol_ref, cnt_ref, carry_ref):
    x1 = x_ref[0] + g1_ref[0] * y
    x_out[0] = x1
    h2 = _rms_mod(x1, n2_ref[...], sh2_ref[0], s2_ref[0])
    h2_out[0] = h2
    is_first = (pl.program_id(0) == 0) & (pl.program_id(1) == 0)
    _route(h2, rwh_ref, rwl_ref, rb_ref, carry_ref, ridx_ref, gcol_ref, cnt_ref, is_first)


def _attn_tail_kernel(a_ref, wo_ref, *rest):
    y = jnp.dot(a_ref[0], wo_ref[...], preferred_element_type=F32)
    _tail(y, *rest)


def _pool_tail_kernel(u_ref, up_ref, un_ref, wg_ref, cs_ref, wo_ref, *rest, seq_len):
    *tail_refs, ubuf = rest
    tm = u_ref.shape[1]
    i = pl.program_id(1)
    u = u_ref[0]
    ubuf[0:POOL_HALO] = jnp.where(i > 0, up_ref[0], 0.0)
    ubuf[POOL_HALO:POOL_HALO + tm] = u
    ubuf[POOL_HALO + tm:2 * POOL_HALO + tm] = jnp.where(i < pl.num_programs(1) - 1, un_ref[0], 0.0)
    pos = i * tm + lax.broadcasted_iota(jnp.int32, (tm, 1), 0)
    gd = wg_ref.shape[1]
    outs = []
    for g, win in enumerate(POOL_WINDOWS):
        half = win // 2
        cols = slice(g * gd, (g + 1) * gd)
        s = ubuf[POOL_HALO - half:POOL_HALO - half + tm, cols]
        for j in range(1 - half, half):
            s = s + ubuf[POOL_HALO + j:POOL_HALO + j + tm, cols]
        cnt = (jnp.minimum(pos + half, seq_len) - jnp.maximum(pos - half, 0)).astype(F32)
        dlt = (s / cnt - u[:, cols]).astype(BF16)
        outs.append(jnp.dot(dlt, wg_ref[g], preferred_element_type=F32))
    z = (jnp.concatenate(outs, axis=-1) * cs_ref[...]).astype(BF16)
    y = jnp.dot(z, wo_ref[...], preferred_element_type=F32)
    _tail(y, *tail_refs)


def _mixer_tail(front_args, front_specs, kern, x, g1, sh2, s2, n2g, rwh, rwl, rb, *, tm, scratch=()):
    b, l, d = x.shape
    nt = l // tm
    n = b * l
    row = lambda bi, i: (bi, 0, 0)
    fix = lambda bi, i: (0, 0)
    tile = lambda bi, i: (bi, i, 0)
    in_specs = list(front_specs) + [
        pl.BlockSpec((1, tm, d), tile),
        pl.BlockSpec((1, 1, d), row), pl.BlockSpec((1, 1, d), row), pl.BlockSpec((1, 1, d), row),
        pl.BlockSpec((1, d), fix),
        pl.BlockSpec((N_EXPERTS, d), fix), pl.BlockSpec((N_EXPERTS, d), fix),
        pl.BlockSpec((N_EXPERTS, 1), fix)]
    out_specs = [pl.BlockSpec((1, tm, d), tile), pl.BlockSpec((1, tm, d), tile),
                 pl.BlockSpec((SUBLANES, tm), lambda bi, i: (0, bi * nt + i)),
                 pl.BlockSpec((tm, LANES), lambda bi, i: (bi * nt + i, 0)),
                 pl.BlockSpec((N_EXPERTS, LANES), fix)]
    out_shape = [jax.ShapeDtypeStruct((b, l, d), F32), jax.ShapeDtypeStruct((b, l, d), F32),
                 jax.ShapeDtypeStruct((SUBLANES, n), jnp.int32), jax.ShapeDtypeStruct((n, LANES), F32),
                 jax.ShapeDtypeStruct((N_EXPERTS, LANES), F32)]
    return pl.pallas_call(
        kern,
        grid=(b, nt),
        in_specs=in_specs, out_specs=out_specs, out_shape=out_shape,
        scratch_shapes=[pltpu.VMEM((N_EXPERTS, LANES), F32)] + list(scratch),
        compiler_params=_params("arbitrary", "arbitrary"),
        name="mixer_tail",
    )(*front_args, x, g1, sh2, s2, n2g, rwh, rwl, rb)


def _slot_kernel(ps_ref, ridx_ref, dest_ref):
    ridx = ridx_ref[...]
    ps = ps_ref[...]
    erow = lax.broadcasted_iota(jnp.int32, (N_EXPERTS, ridx.shape[1]), 0)
    rows = []
    for k in range(TOP_K):
        start = jnp.sum(jnp.where(erow == ridx[k:k + 1], ps, 0), axis=0, keepdims=True)
        rows.append(start + ridx[TOP_K + k:TOP_K + k + 1])
    rid = lax.broadcasted_iota(jnp.int32, ridx.shape, 0)
    dest_ref[...] = jnp.where(rid == 0, rows[0], jnp.where(rid == 1, rows[1], 0))


def _slot_index(pad_start, ridx, *, tn):
    n = ridx.shape[1]
    return pl.pallas_call(
        _slot_kernel,
        grid=(n // tn,),
        in_specs=[pl.BlockSpec((N_EXPERTS, 1), lambda i: (0, 0)),
                  pl.BlockSpec((SUBLANES, tn), lambda i: (0, i))],
        out_specs=pl.BlockSpec((SUBLANES, tn), lambda i: (0, i)),
        out_shape=jax.ShapeDtypeStruct((SUBLANES, n), jnp.int32),
        compiler_params=_params("arbitrary"),
        name="slot_index",
    )(pad_start.reshape(N_EXPERTS, 1), ridx)


def _sc_mesh():
    return plsc.VectorSubcoreMesh(core_axis_name="c", subcore_axis_name="s",
                                  num_cores=SC_CORES, num_subcores=SC_SUBCORES)


def _sc_worker_base(per_worker):
    return (lax.axis_index("s") * SC_CORES + lax.axis_index("c")) * per_worker


def _sc_scatter_rows(rows, idx0, idx1, n_slots):
    n, d = rows.shape
    per_worker = n // SC_WORKERS
    assert per_worker % SC_WINDOW == 0

    def body(rows_hbm, i0_hbm, i1_hbm, out_hbm, i0_v, i1_v, rows_v):
        base = _sc_worker_base(per_worker)

        @pl.loop(0, per_worker // SC_WINDOW)
        def _(j):
            off = pl.multiple_of(base + j * SC_WINDOW, SC_WINDOW)
            pltpu.sync_copy(i0_hbm.at[pl.ds(off, SC_WINDOW)], i0_v)
            pltpu.sync_copy(i1_hbm.at[pl.ds(off, SC_WINDOW)], i1_v)
            pltpu.sync_copy(rows_hbm.at[pl.ds(off, SC_WINDOW)], rows_v)
            pltpu.sync_copy(rows_v, out_hbm.at[i0_v])
            pltpu.sync_copy(rows_v, out_hbm.at[i1_v])

    return pl.kernel(
        body, out_type=jax.ShapeDtypeStruct((n_slots, d), rows.dtype), mesh=_sc_mesh(),
        scratch_types=[pltpu.VMEM((SC_WINDOW,), jnp.int32), pltpu.VMEM((SC_WINDOW,), jnp.int32),
                       pltpu.VMEM((SC_WINDOW, d), rows.dtype)],
        name="sc_scatter_rows",
    )(rows, idx0, idx1)


def _sc_gather_rows(table, idx):
    n = idx.shape[0]
    d = table.shape[1]
    per_worker = n // SC_WORKERS
    assert per_worker % SC_WINDOW == 0

    def body(table_hbm, idx_hbm, out_hbm, idx_v, rows_v, sem):
        base = _sc_worker_base(per_worker)

        @pl.loop(0, per_worker // SC_WINDOW)
        def _(j):
            off = pl.multiple_of(base + j * SC_WINDOW, SC_WINDOW)
            pltpu.sync_copy(idx_hbm.at[pl.ds(off, SC_WINDOW)], idx_v)
            pltpu.async_copy(table_hbm.at[idx_v], rows_v, sem).wait()
            pltpu.sync_copy(rows_v, out_hbm.at[pl.ds(off, SC_WINDOW)])

    return pl.kernel(
        body, out_type=jax.ShapeDtypeStruct((n, d), table.dtype), mesh=_sc_mesh(),
        scratch_types=[pltpu.VMEM((SC_WINDOW,), jnp.int32), pltpu.VMEM((SC_WINDOW, d), table.dtype),
                       pltpu.SemaphoreType.DMA],
        name="sc_gather_rows",
    )(table, idx)


def _ffn_kernel(be_ref, nv_ref, xs_ref, wg_ref, wu_ref, wd_ref, ys_ref, wg_b, wu_b, wd_b):
    j = pl.program_id(0)
    valid = nv_ref[j]

    @pl.when((valid > 0) & ((j == 0) | (be_ref[j] != be_ref[jnp.maximum(j - 1, 0)])))
    def _():
        wg_b[...] = wg_ref[0].astype(BF16)
        wu_b[...] = wu_ref[0].astype(BF16)
        wd_b[...] = wd_ref[0].astype(BF16)

    @pl.when(valid > 0)
    def _():
        row = lax.broadcasted_iota(jnp.int32, (SLOT_ROWS, 1), 0)
        xb = jnp.where(row < valid, xs_ref[...], 0.0).astype(BF16)
        g = jnp.dot(xb, wg_b[...], preferred_element_type=F32)
        u = jnp.dot(xb, wu_b[...], preferred_element_type=F32)
        a = (g / (1.0 + jnp.exp(-g)) * u).astype(BF16)
        ys_ref[...] = jnp.dot(a, wd_b[...], preferred_element_type=F32)

    @pl.when(valid <= 0)
    def _():
        ys_ref[...] = jnp.zeros_like(ys_ref)


def _expert_ffn(block_e, n_valid, xs, w_gate, w_up, w_down):
    n_slots, d = xs.shape
    de = w_gate.shape[2]
    return pl.pallas_call(
        _ffn_kernel,
        grid_spec=pltpu.PrefetchScalarGridSpec(
            num_scalar_prefetch=2,
            grid=(n_slots // SLOT_ROWS,),
            in_specs=[pl.BlockSpec((SLOT_ROWS, d), lambda j, be, nv: (j, 0)),
                      pl.BlockSpec((1, d, de), lambda j, be, nv: (be[j], 0, 0)),
                      pl.BlockSpec((1, d, de), lambda j, be, nv: (be[j], 0, 0)),
                      pl.BlockSpec((1, de, d), lambda j, be, nv: (be[j], 0, 0))],
            out_specs=pl.BlockSpec((SLOT_ROWS, d), lambda j, be, nv: (j, 0)),
            scratch_shapes=[pltpu.VMEM((d, de), BF16), pltpu.VMEM((d, de), BF16), pltpu.VMEM((de, d), BF16)]),
        out_shape=jax.ShapeDtypeStruct((n_slots, d), F32),
        compiler_params=_params("arbitrary"),
        name="expert_ffn",
    )(block_e, n_valid, xs, w_gate, w_up, w_down)


def _combine_kernel(y0_ref, y1_ref, gcol_ref, x_ref, g2_ref, *rest, pool_in):
    if pool_in:
        sh_ref, sc_ref, n1_ref, wi_ref, x_out, u_out = rest
    else:
        (x_out,) = rest
    gc = gcol_ref[...]
    out = gc[:, 0:1] * y0_ref[0] + gc[:, 1:2] * y1_ref[0]
    x2 = x_ref[0] + g2_ref[0] * out
    x_out[0] = x2
    if pool_in:
        hb = _rms_mod(x2, n1_ref[...], sh_ref[0], sc_ref[0]).astype(BF16)
        u_out[0] = jnp.dot(hb, wi_ref[...], preferred_element_type=F32)


def _combine(yg, gcol, x, g2, pool_args=None, *, tc):
    b, l, d = x.shape
    nt = l // tc
    pool_in = pool_args is not None
    kern = functools.partial(_combine_kernel, pool_in=pool_in)
    row = lambda bi, i: (bi, 0, 0)
    fix = lambda bi, i: (0, 0)
    tile = lambda bi, i: (bi, i, 0)
    in_specs = [pl.BlockSpec((1, tc, d), lambda bi, i: (0, bi * nt + i, 0)),
                pl.BlockSpec((1, tc, d), lambda bi, i: (1, bi * nt + i, 0)),
                pl.BlockSpec((tc, LANES), lambda bi, i: (bi * nt + i, 0)),
                pl.BlockSpec((1, tc, d), tile),
                pl.BlockSpec((1, 1, d), row)]
    out_specs = [pl.BlockSpec((1, tc, d), tile)]
    out_shape = [jax.ShapeDtypeStruct((b, l, d), F32)]
    args = [yg, yg, gcol, x, g2]
    if pool_in:
        in_specs += [pl.BlockSpec((1, 1, d), row), pl.BlockSpec((1, 1, d), row),
                     pl.BlockSpec((1, d), fix), pl.BlockSpec((d, d), fix)]
        out_specs.append(pl.BlockSpec((1, tc, d), tile))
        out_shape.append(jax.ShapeDtypeStruct((b, l, d), F32))
        args += list(pool_args)
    return pl.pallas_call(
        kern,
        grid=(b, nt),
        in_specs=in_specs, out_specs=out_specs, out_shape=out_shape,
        compiler_params=_params("arbitrary", "arbitrary"),
        name="moe_combine",
    )(*args)


def _moe(h2, ridx, gcol, counts, x1, g2, w_gate, w_up, w_down, pool_args=None):
    b, l, d = x1.shape
    n = b * l
    n_blocks = (n * TOP_K) // SLOT_ROWS + N_EXPERTS
    cnt = counts[:, 0].astype(jnp.int32)
    padded = (cnt + SLOT_ROWS - 1) // SLOT_ROWS * SLOT_ROWS
    pad_end = jnp.cumsum(padded)
    pad_start = (pad_end - padded).astype(jnp.int32)
    block_start = jnp.arange(n_blocks, dtype=jnp.int32) * SLOT_ROWS
    block_e = jnp.minimum(jnp.sum(pad_end[None, :] <= block_start[:, None], axis=1), N_EXPERTS - 1).astype(jnp.int32)
    n_valid = jnp.clip(pad_start[block_e] + cnt[block_e] - block_start, 0, SLOT_ROWS).astype(jnp.int32)
    dest = _slot_index(pad_start, ridx, tn=2048)
    xs = _sc_scatter_rows(h2.reshape(n, d), dest[0], dest[1], n_blocks * SLOT_ROWS)
    ys = _expert_ffn(block_e, n_valid, xs, w_gate, w_up, w_down)
    yg = _sc_gather_rows(ys, dest[:TOP_K].reshape(TOP_K * n)).reshape(TOP_K, n, d)
    return _combine(yg, gcol, x1, g2, pool_args, tc=512)


def kernel(x, c, ctx, c_ctx, ada_w, ada_b, norm1_g, norm2_g, attn_w_in, attn_w_out, attn_q_gain, attn_k_gain,
           attn_lq1, attn_lk1, attn_lq2, attn_lk2, attn_sub_gain, pool_w_in, pool_w_group, pool_scale, pool_w_out,
           router_w, router_b, moe_w_gate, moe_w_up, moe_w_down):
    b, l, d = x.shape
    n_ctx = ctx.shape[1]
    depth = ada_w.shape[0]
    assert depth == 2 and d == N_HEADS * V_DIM
    tm = 512

    mod = _adaln_mod(c, c_ctx, ada_w, ada_b)
    mods = [[mod[i, :b, None, j * d:(j + 1) * d] for j in range(N_MOD)] for i in range(depth)]
    mod_ctx = [jnp.broadcast_to(mod[0, b, j * d:(j + 1) * d], (b, 1, d)) for j in range(2)]

    rwt = router_w.T
    rwh = rwt.astype(BF16)
    rwl = (rwt - rwh.astype(F32)).astype(BF16)
    rb = router_b.reshape(N_EXPERTS, 1)

    sh1, s1, g1, sh2, s2, g2 = mods[0]
    cos, sin = _rope_tables(l)
    gain2 = lambda g: jnp.concatenate([g, g])[None]
    qg, kg = gain2(attn_q_gain[0]), gain2(attn_k_gain[0])
    w_in = attn_w_in[0].astype(BF16)
    assert l % tm == 0 and l % n_ctx == 0
    q, k_all, vt_all = _qkv_proj(x, sh1, s1, norm1_g[0][None], w_in, qg, kg, cos, sin, n_qk=2, rope=True,
                                 tm=tm, n_keys=l + n_ctx, key_row0=0)
    k_all, vt_all = _qkv_proj(ctx, mod_ctx[0], mod_ctx[1], norm1_g[0][None], w_in[:, d:], qg, kg,
                              cos[:n_ctx], sin[:n_ctx], n_qk=1, rope=False,
                              tm=n_ctx, n_keys=l + n_ctx, key_row0=l, kv=(k_all, vt_all))
    lam_init = 0.8 - 0.6 * math.exp(-0.3 * 0)
    lam_params = jnp.stack([attn_lq1[0], attn_lk1[0], attn_lq2[0], attn_lk2[0]])
    o = _diff_attention(q, k_all, vt_all, lam_params, attn_sub_gain[0][:, None], lam_init=lam_init, tq=512)

    fix = lambda bi, i: (0, 0)
    x1, h2, ridx, gcol, counts = _mixer_tail(
        (o, attn_w_out[0].astype(BF16)),
        (pl.BlockSpec((1, tm, d), lambda bi, i: (bi, i, 0)), pl.BlockSpec((d, d), fix)),
        _attn_tail_kernel, x, g1, sh2, s2, norm2_g[0][None], rwh, rwl, rb, tm=tm)

    sh1b, s1b, g1b, sh2b, s2b, g2b = mods[1]
    x2, u = _moe(h2, ridx, gcol, counts, x1, g2, moe_w_gate[0], moe_w_up[0], moe_w_down[0],
                 pool_args=(sh1b, s1b, norm1_g[1][None], pool_w_in[0].astype(BF16)))
    gd = pool_w_group.shape[2]
    nh = tm // POOL_HALO
    front_specs = (
        pl.BlockSpec((1, tm, d), lambda bi, i: (bi, i, 0)),
        pl.BlockSpec((1, POOL_HALO, d), lambda bi, i: (bi, jnp.maximum(i * nh - 1, 0), 0)),
        pl.BlockSpec((1, POOL_HALO, d), lambda bi, i: (bi, jnp.minimum((i + 1) * nh, l // POOL_HALO - 1), 0)),
        pl.BlockSpec((len(POOL_WINDOWS), gd, gd), lambda bi, i: (0, 0, 0)),
        pl.BlockSpec((1, d), fix),
        pl.BlockSpec((d, d), fix))
    x3, h2b, ridx_b, gcol_b, counts_b = _mixer_tail(
        (u, u, u, pool_w_group[0].astype(BF16), pool_scale[0][None], pool_w_out[0].astype(BF16)),
        front_specs, functools.partial(_pool_tail_kernel, seq_len=l),
        x2, g1b, sh2b, s2b, norm2_g[1][None], rwh, rwl, rb, tm=tm,
        scratch=[pltpu.VMEM((tm + 2 * POOL_HALO, d), F32)])
    (out,) = _moe(h2b, ridx_b, gcol_b, counts_b, x3, g2b, moe_w_gate[1], moe_w_up[1], moe_w_down[1])
    return out
```

```python
import functools
import math

import jax
import jax.numpy as jnp
from jax import lax
from jax.experimental import pallas as pl
from jax.experimental.pallas import tpu as pltpu
from jax.experimental.pallas import tpu_sc as plsc

F32 = jnp.float32
BF16 = jnp.bfloat16
F8 = jnp.float8_e4m3fn

LANES = 128
SUBLANES = 8
N_HEADS = 8
HEAD_DIM = 64
V_DIM = 2 * HEAD_DIM
V_ROWS = V_DIM + 32
P_SHIFT = 8.0
GRID_W = 64
ROPE_THETA = 10000.0
NORM_EPS = 1e-6
N_MOD = 6
POOL_WINDOWS = (2, 4, 8, 16)
POOL_HALO = max(POOL_WINDOWS) // 2
N_EXPERTS = 32
N_EXPERT_GROUPS = 4
EXPERTS_PER_GROUP = N_EXPERTS // N_EXPERT_GROUPS
TOP_K = 2
SLOT_ROWS = 512
SC_CORES = 2
SC_SUBCORES = 16
SC_WORKERS = SC_CORES * SC_SUBCORES
SC_WINDOW = 32
ATTN_GROUP_CHUNKS = 1
ATTN_SCORE_AHEAD = 1
VMEM_LIMIT = 48 * 1024 * 1024
NT_DIMS = (((1,), (1,)), ((), ()))


def _params(*sem):
    return pltpu.CompilerParams(dimension_semantics=sem, vmem_limit_bytes=VMEM_LIMIT)


def _rms_mod(x, gain, shift, scale):
    h = x * lax.rsqrt(jnp.mean(x * x, axis=-1, keepdims=True) + NORM_EPS) * gain
    return h * (1.0 + scale) + shift


def _mod_kernel(c_ref, w_ref, b_ref, o_ref):
    c = c_ref[...]
    a = c / (1.0 + jnp.exp(-c))
    o_ref[0] = jnp.dot(a, w_ref[0], precision=lax.Precision.HIGHEST,
                       preferred_element_type=F32) + b_ref[0]


def _adaln_mod(c, c_ctx, ada_w, ada_b):
    depth, d, n_out = ada_w.shape
    b = c.shape[0]
    assert b + 1 <= SUBLANES
    rows = jnp.concatenate([c, c_ctx[None], jnp.zeros((SUBLANES - b - 1, d), F32)], axis=0)
    tn = n_out // 4
    return pl.pallas_call(
        _mod_kernel,
        grid=(depth, n_out // tn),
        in_specs=[pl.BlockSpec((SUBLANES, d), lambda i, j: (0, 0)),
                  pl.BlockSpec((1, d, tn), lambda i, j: (i, 0, j)),
                  pl.BlockSpec((1, 1, tn), lambda i, j: (i, 0, j))],
        out_specs=pl.BlockSpec((1, SUBLANES, tn), lambda i, j: (i, 0, j)),
        out_shape=jax.ShapeDtypeStruct((depth, SUBLANES, n_out), F32),
        compiler_params=_params("arbitrary", "arbitrary"),
        name="adaln_mod",
    )(rows, ada_w, ada_b.reshape(depth, 1, n_out))


def _qkv_kernel(x_ref, sh_ref, sc_ref, g_ref, w_ref, qg_ref, kg_ref, cos_ref, sin_ref, *refs,
                n_qk, rope):
    out_refs = refs[-(n_qk + 1):]
    tm, d = x_ref.shape[1:]
    hb = _rms_mod(x_ref[0], g_ref[...], sh_ref[0], sc_ref[0]).astype(BF16)
    lane = lax.broadcasted_iota(jnp.int32, (1, LANES), 1)
    lane_lo = lane < HEAD_DIM
    lane_b4 = (lane & 16) == 0
    gains = (qg_ref[...], kg_ref[...])[2 - n_qk:]
    for t in range(n_qk):
        for j in range(0, d, 2 * LANES):
            acc = jnp.dot(hb, w_ref[:, t * d + j:t * d + j + 2 * LANES], preferred_element_type=F32)
            for half in range(2):
                blk = acc[:, half * LANES:(half + 1) * LANES]
                sq = blk * blk
                lo = jnp.sum(jnp.where(lane_lo, sq, 0.0), axis=-1, keepdims=True)
                hi = jnp.sum(jnp.where(lane_lo, 0.0, sq), axis=-1, keepdims=True)
                ms = jnp.where(lane_lo, lo, hi) * (1.0 / HEAD_DIM)
                y = blk * lax.rsqrt(ms + NORM_EPS) * gains[t]
                if rope:
                    rot = jnp.where(lane_b4, pltpu.roll(y, LANES - 16, 1), pltpu.roll(y, 16, 1))
                    y = y * cos_ref[...] + rot * sin_ref[...]
                c0 = j + half * LANES
                out_refs[t][0, :, c0:c0 + LANES] = y.astype(out_refs[t].dtype)
    vt_ref = out_refs[n_qk]
    ones = jnp.where(lax.broadcasted_iota(jnp.int32, (V_ROWS - V_DIM, tm), 0) == 0, 1.0, 0.0).astype(F8)
    for j in range(0, d, 2 * LANES):
        acc = jnp.dot(hb, w_ref[:, n_qk * d + j:n_qk * d + j + 2 * LANES], preferred_element_type=F32)
        acc_t = acc.T
        for half in range(2):
            h = j // LANES + half
            vt_ref[0, h, 0:V_DIM, :] = acc_t[half * V_DIM:(half + 1) * V_DIM].astype(F8)
            vt_ref[0, h, V_DIM:V_ROWS, :] = ones


def _qkv_proj(x, shift, scale, gain, w, q_gain, k_gain, cos, sin, *, n_qk, rope, tm, n_keys, key_row0, kv=None):
    b, l, d = x.shape
    n_out = n_qk + 1
    kern = functools.partial(_qkv_kernel, n_qk=n_qk, rope=rope)
    row = lambda bi, i: (bi, 0, 0)
    fix = lambda bi, i: (0, 0)
    kb = key_row0 // tm
    tile = pl.BlockSpec((1, tm, d), lambda bi, i: (bi, i, 0))
    k_spec = pl.BlockSpec((1, tm, d), lambda bi, i: (bi, kb + i, 0))
    vt_spec = pl.BlockSpec((1, N_HEADS, V_ROWS, tm), lambda bi, i: (bi, 0, 0, kb + i))
    k_shape = jax.ShapeDtypeStruct((b, n_keys, d), F8)
    vt_shape = jax.ShapeDtypeStruct((b, N_HEADS, V_ROWS, n_keys), F8)
    in_specs = [tile,
                pl.BlockSpec((1, 1, d), row), pl.BlockSpec((1, 1, d), row),
                pl.BlockSpec((1, d), fix),
                pl.BlockSpec((d, n_out * d), fix),
                pl.BlockSpec((1, LANES), fix), pl.BlockSpec((1, LANES), fix),
                pl.BlockSpec((tm, LANES), lambda bi, i: (i, 0)),
                pl.BlockSpec((tm, LANES), lambda bi, i: (i, 0))]
    args = [x, shift, scale, gain, w, q_gain, k_gain, cos, sin]
    aliases = {}
    if kv is not None:
        aliases = {len(args): n_qk - 1, len(args) + 1: n_qk}
        in_specs += [pl.BlockSpec(memory_space=pl.ANY)] * 2
        args += list(kv)
    return pl.pallas_call(
        kern,
        grid=(b, l // tm),
        in_specs=in_specs,
        out_specs=[tile] * (n_qk - 1) + [k_spec, vt_spec],
        out_shape=[jax.ShapeDtypeStruct((b, l, d), BF16)] * (n_qk - 1) + [k_shape, vt_shape],
        input_output_aliases=aliases,
        compiler_params=_params("arbitrary", "arbitrary"),
        name="qkv_proj",
    )(*args)


def _rope_tables(n_tokens):
    rows = n_tokens // GRID_W
    row = jnp.repeat(jnp.arange(rows, dtype=F32), GRID_W)
    col = jnp.tile(jnp.arange(GRID_W, dtype=F32), rows)
    half = HEAD_DIM // 2
    inv_freq = ROPE_THETA ** (-jnp.arange(0, half, 2, dtype=F32) / half)
    ang_r = row[:, None] * inv_freq
    ang_c = col[:, None] * inv_freq
    ang = jnp.concatenate([ang_r, ang_r, ang_c, ang_c], axis=-1)
    ang = jnp.concatenate([ang, ang], axis=-1)
    sign = jnp.where((jnp.arange(LANES) & 16) == 0, -1.0, 1.0).astype(F32)
    return jnp.cos(ang), jnp.sin(ang) * sign


def _attn_kernel(q_ref, k_ref, vt_ref, lp_ref, sg_ref, o_ref, s_ref, *, tk, group, ahead, lam_init):
    tq = q_ref.shape[1]
    n_chunks = k_ref.shape[1] // tk
    qt = (q_ref[0].astype(F32) * (HEAD_DIM ** -0.5 * math.log2(math.e))).T
    sub = lax.broadcasted_iota(jnp.int32, (V_DIM, 1), 0)
    first = sub < HEAD_DIM
    qz = jnp.concatenate([jnp.where(first, qt, 0.0), jnp.where(first, 0.0, qt)], axis=1).astype(F8)
    n_slots = (ahead + 1) * group

    def score_chunk(c, m_grp):
        st = jnp.dot(k_ref[0, c * tk:(c + 1) * tk, :], qz, preferred_element_type=F32).astype(BF16)
        slot = c % n_slots
        s_ref[slot * tk:(slot + 1) * tk, :] = st
        mc = jnp.max(st, axis=0, keepdims=True)
        return mc if m_grp is None else jnp.maximum(m_grp, mc)

    def value_chunk(c, m_ref, part):
        slot = c % n_slots
        p = jnp.exp2(s_ref[slot * tk:(slot + 1) * tk, :] - (m_ref - P_SHIFT)).astype(F8)
        pv = jnp.dot(vt_ref[0, 0, :, c * tk:(c + 1) * tk], p, preferred_element_type=F32)
        return pv if part is None else part + pv

    groups = [list(range(g0, min(g0 + group, n_chunks))) for g0 in range(0, n_chunks, group)]
    m_of = {}
    for g in range(min(ahead, len(groups))):
        for c in groups[g]:
            m_of[g] = score_chunk(c, m_of.get(g))
    m = None
    acc = None
    for gi, cur in enumerate(groups):
        nxt = groups[gi + ahead] if gi + ahead < len(groups) else []
        m_new = m_of[gi] if m is None else jnp.maximum(m, m_of[gi])
        part = None
        for i in range(max(len(cur), len(nxt))):
            if i < len(nxt):
                m_of[gi + ahead] = score_chunk(nxt[i], m_of.get(gi + ahead))
            if i < len(cur):
                part = value_chunk(cur[i], m_new, part)
        acc = part if acc is None else acc * jnp.exp2(m.astype(F32) - m_new.astype(F32)) + part
        m = m_new
    acc = acc[:V_DIM] / acc[V_DIM:V_DIM + 1]
    lp = lp_ref[...]
    lam = (jnp.exp(jnp.sum(lp[0:1] * lp[1:2], axis=-1, keepdims=True))
           - jnp.exp(jnp.sum(lp[2:3] * lp[3:4], axis=-1, keepdims=True)) + lam_init)
    o = acc[:, :tq] - lam * acc[:, tq:]
    o = o * lax.rsqrt(jnp.mean(o * o, axis=0, keepdims=True) + NORM_EPS) * sg_ref[...] * (1.0 - lam_init)
    o_ref[0] = o.T.astype(BF16)


def _attn_chunk(n_keys):
    for tk in (768, 512, 256, 128):
        if n_keys % tk == 0:
            return tk
    raise ValueError(f"key count {n_keys} is not a multiple of {LANES}")


def _diff_attention(q, k_all, vt_all, lam_params, sub_gain, *, lam_init, tq):
    b, l, d = q.shape
    n_keys = k_all.shape[1]
    tk = _attn_chunk(n_keys)
    kern = functools.partial(_attn_kernel, tk=tk, group=ATTN_GROUP_CHUNKS, ahead=ATTN_SCORE_AHEAD, lam_init=lam_init)
    return pl.pallas_call(
        kern,
        grid=(b, N_HEADS, l // tq),
        in_specs=[pl.BlockSpec((1, tq, V_DIM), lambda bi, h, i: (bi, i, h)),
                  pl.BlockSpec((1, n_keys, V_DIM), lambda bi, h, i: (bi, 0, h)),
                  pl.BlockSpec((1, 1, V_ROWS, n_keys), lambda bi, h, i: (bi, h, 0, 0)),
                  pl.BlockSpec((4, HEAD_DIM), lambda bi, h, i: (0, 0)),
                  pl.BlockSpec((V_DIM, 1), lambda bi, h, i: (0, 0))],
        out_specs=pl.BlockSpec((1, tq, V_DIM), lambda bi, h, i: (bi, i, h)),
        out_shape=jax.ShapeDtypeStruct((b, l, d), BF16),
        scratch_shapes=[pltpu.VMEM(((ATTN_SCORE_AHEAD + 1) * ATTN_GROUP_CHUNKS * tk, 2 * tq), BF16)],
        compiler_params=_params("arbitrary", "arbitrary", "arbitrary"),
        name="diff_attention",
    )(q, k_all, vt_all, lam_params, sub_gain)


def _route(h2, rwh_ref, rwl_ref, rb_ref, carry_ref, ridx_ref, gcol_ref, cnt_ref, is_first):
    tm = h2.shape[0]
    hh = h2.astype(BF16)
    hl = (h2 - hh.astype(F32)).astype(BF16)
    rw2 = jnp.concatenate([rwh_ref[...], rwl_ref[...]], axis=0)
    part = lax.dot_general(rw2, hh, NT_DIMS, preferred_element_type=F32)
    logits = (part[:N_EXPERTS] + part[N_EXPERTS:]
              + lax.dot_general(rwh_ref[...], hl, NT_DIMS, preferred_element_type=F32) + rb_ref[...])
    groups = [logits[g * EXPERTS_PER_GROUP:(g + 1) * EXPERTS_PER_GROUP] for g in range(N_EXPERT_GROUPS)]
    top = groups[0]
    for g in range(1, N_EXPERT_GROUPS):
        top = jnp.maximum(top, groups[g])
    top = jnp.max(top, axis=0, keepdims=True)
    sub = lax.broadcasted_iota(jnp.int32, (EXPERTS_PER_GROUP, tm), 0)
    best = None
    for g in range(N_EXPERT_GROUPS):
        ex = jnp.exp(groups[g] - top)
        v1 = jnp.max(ex, axis=0, keepdims=True)
        i1 = jnp.min(jnp.where(ex == v1, sub, EXPERTS_PER_GROUP), axis=0, keepdims=True)
        rest = jnp.where(sub == i1, -1.0, ex)
        v2 = jnp.max(rest, axis=0, keepdims=True)
        i2 = jnp.min(jnp.where(rest == v2, sub, EXPERTS_PER_GROUP), axis=0, keepdims=True)
        cand = (v1 + v2, v1, v2, i1 + g * EXPERTS_PER_GROUP, i2 + g * EXPERTS_PER_GROUP)
        if best is None:
            best = cand
        else:
            better = cand[0] > best[0]
            best = tuple(jnp.where(better, new, old) for new, old in zip(cand, best))
    _, v1, v2, e0, e1 = best
    gate0 = v1 / (v1 + v2)
    gate1 = v2 / (v1 + v2)

    @pl.when(is_first)
    def _():
        carry_ref[...] = jnp.zeros_like(carry_ref)

    erow = lax.broadcasted_iota(jnp.int32, (N_EXPERTS, tm), 0)
    oh0 = erow == e0
    oh1 = erow == e1
    chosen = jnp.where(oh0 | oh1, 1.0, 0.0)
    before = (lax.broadcasted_iota(jnp.int32, (tm, tm), 0)
              < lax.broadcasted_iota(jnp.int32, (tm, tm), 1)).astype(BF16)
    prior = carry_ref[:, 0:1] + jnp.dot(chosen.astype(BF16), before, preferred_element_type=F32)
    r0 = jnp.sum(jnp.where(oh0, prior, 0.0), axis=0, keepdims=True).astype(jnp.int32)
    r1 = jnp.sum(jnp.where(oh1, prior, 0.0), axis=0, keepdims=True).astype(jnp.int32)
    carry_ref[...] = carry_ref[...] + jnp.sum(chosen, axis=1, keepdims=True)
    cnt_ref[...] = carry_ref[...]
    rid = lax.broadcasted_iota(jnp.int32, (SUBLANES, tm), 0)
    ridx_ref[...] = jnp.where(rid == 0, e0, jnp.where(rid == 1, e1, jnp.where(rid == 2, r0, jnp.where(rid == 3, r1, 0))))
    gid = lax.broadcasted_iota(jnp.int32, (LANES, tm), 0)
    gcol_ref[...] = jnp.where(gid == 0, gate0, jnp.where(gid == 1, gate1, 0.0)).T


def _tail(y, x_ref, g1_ref, sh2_ref, s2_ref, n2_ref, rwh_ref, rwl_ref, rb_ref,
          x_out, h2_out, ridx_ref, gcol_ref, cnt_ref, carry_ref):
    x1 = x_ref[0] + g1_ref[0] * y
    x_out[0] = x1
    h2 = _rms_mod(x1, n2_ref[...], sh2_ref[0], s2_ref[0])
    h2_out[0] = h2
    is_first = (pl.program_id(0) == 0) & (pl.program_id(1) == 0)
    _route(h2, rwh_ref, rwl_ref, rb_ref, carry_ref, ridx_ref, gcol_ref, cnt_ref, is_first)


def _attn_tail_kernel(a_ref, wo_ref, *rest):
    y = jnp.dot(a_ref[0], wo_ref[...], preferred_element_type=F32)
    _tail(y, *rest)


def _pool_tail_kernel(u_ref, up_ref, un_ref, wg_ref, cs_ref, wo_ref, *rest, seq_len):
    *tail_refs, ubuf = rest
    tm = u_ref.shape[1]
    i = pl.program_id(1)
    u = u_ref[0]
    ubuf[0:POOL_HALO] = jnp.where(i > 0, up_ref[0], 0.0)
    ubuf[POOL_HALO:POOL_HALO + tm] = u
    ubuf[POOL_HALO + tm:2 * POOL_HALO + tm] = jnp.where(i < pl.num_programs(1) - 1, un_ref[0], 0.0)
    pos = i * tm + lax.broadcasted_iota(jnp.int32, (tm, 1), 0)
    gd = wg_ref.shape[1]
    outs = []
    for g, win in enumerate(POOL_WINDOWS):
        half = win // 2
        cols = slice(g * gd, (g + 1) * gd)
        s = ubuf[POOL_HALO - half:POOL_HALO - half + tm, cols]
        for j in range(1 - half, half):
            s = s + ubuf[POOL_HALO + j:POOL_HALO + j + tm, cols]
        cnt = (jnp.minimum(pos + half, seq_len) - jnp.maximum(pos - half, 0)).astype(F32)
        dlt = (s / cnt - u[:, cols]).astype(BF16)
        outs.append(jnp.dot(dlt, wg_ref[g], preferred_element_type=F32))
    z = (jnp.concatenate(outs, axis=-1) * cs_ref[...]).astype(BF16)
    y = jnp.dot(z, wo_ref[...], preferred_element_type=F32)
    _tail(y, *tail_refs)


def _mixer_tail(front_args, front_specs, kern, x, g1, sh2, s2, n2g, rwh, rwl, rb, *, tm, scratch=()):
    b, l, d = x.shape
    nt = l // tm
    n = b * l
    row = lambda bi, i: (bi, 0, 0)
    fix = lambda bi, i: (0, 0)
    tile = lambda bi, i: (bi, i, 0)
    in_specs = list(front_specs) + [
        pl.BlockSpec((1, tm, d), tile),
        pl.BlockSpec((1, 1, d), row), pl.BlockSpec((1, 1, d), row), pl.BlockSpec((1, 1, d), row),
        pl.BlockSpec((1, d), fix),
        pl.BlockSpec((N_EXPERTS, d), fix), pl.BlockSpec((N_EXPERTS, d), fix),
        pl.BlockSpec((N_EXPERTS, 1), fix)]
    out_specs = [pl.BlockSpec((1, tm, d), tile), pl.BlockSpec((1, tm, d), tile),
                 pl.BlockSpec((SUBLANES, tm), lambda bi, i: (0, bi * nt + i)),
                 pl.BlockSpec((tm, LANES), lambda bi, i: (bi * nt + i, 0)),
                 pl.BlockSpec((N_EXPERTS, LANES), fix)]
    out_shape = [jax.ShapeDtypeStruct((b, l, d), F32), jax.ShapeDtypeStruct((b, l, d), F32),
                 jax.ShapeDtypeStruct((SUBLANES, n), jnp.int32), jax.ShapeDtypeStruct((n, LANES), F32),
                 jax.ShapeDtypeStruct((N_EXPERTS, LANES), F32)]
    return pl.pallas_call(
        kern,
        grid=(b, nt),
        in_specs=in_specs, out_specs=out_specs, out_shape=out_shape,
        scratch_shapes=[pltpu.VMEM((N_EXPERTS, LANES), F32)] + list(scratch),
        compiler_params=_params("arbitrary", "arbitrary"),
        name="mixer_tail",
    )(*front_args, x, g1, sh2, s2, n2g, rwh, rwl, rb)


def _slot_kernel(ps_ref, ridx_ref, dest_ref):
    ridx = ridx_ref[...]
    ps = ps_ref[...]
    erow = lax.broadcasted_iota(jnp.int32, (N_EXPERTS, ridx.shape[1]), 0)
    rows = []
    for k in range(TOP_K):
        start = jnp.sum(jnp.where(erow == ridx[k:k + 1], ps, 0), axis=0, keepdims=True)
        rows.append(start + ridx[TOP_K + k:TOP_K + k + 1])
    rid = lax.broadcasted_iota(jnp.int32, ridx.shape, 0)
    dest_ref[...] = jnp.where(rid == 0, rows[0], jnp.where(rid == 1, rows[1], 0))


def _slot_index(pad_start, ridx, *, tn):
    n = ridx.shape[1]
    return pl.pallas_call(
        _slot_kernel,
        grid=(n // tn,),
        in_specs=[pl.BlockSpec((N_EXPERTS, 1), lambda i: (0, 0)),
                  pl.BlockSpec((SUBLANES, tn), lambda i: (0, i))],
        out_specs=pl.BlockSpec((SUBLANES, tn), lambda i: (0, i)),
        out_shape=jax.ShapeDtypeStruct((SUBLANES, n), jnp.int32),
        compiler_params=_params("arbitrary"),
        name="slot_index",
    )(pad_start.reshape(N_EXPERTS, 1), ridx)


def _sc_mesh():
    return plsc.VectorSubcoreMesh(core_axis_name="c", subcore_axis_name="s",
                                  num_cores=SC_CORES, num_subcores=SC_SUBCORES)


def _sc_worker_base(per_worker):
    return (lax.axis_index("s") * SC_CORES + lax.axis_index("c")) * per_worker


def _sc_scatter_rows(rows, idx0, idx1, n_slots):
    n, d = rows.shape
    per_worker = n // SC_WORKERS
    assert per_worker % SC_WINDOW == 0

    def body(rows_hbm, i0_hbm, i1_hbm, out_hbm, i0_v, i1_v, rows_v):
        base = _sc_worker_base(per_worker)

        @pl.loop(0, per_worker // SC_WINDOW)
        def _(j):
            off = pl.multiple_of(base + j * SC_WINDOW, SC_WINDOW)
            pltpu.sync_copy(i0_hbm.at[pl.ds(off, SC_WINDOW)], i0_v)
            pltpu.sync_copy(i1_hbm.at[pl.ds(off, SC_WINDOW)], i1_v)
            pltpu.sync_copy(rows_hbm.at[pl.ds(off, SC_WINDOW)], rows_v)
            pltpu.sync_copy(rows_v, out_hbm.at[i0_v])
            pltpu.sync_copy(rows_v, out_hbm.at[i1_v])

    return pl.kernel(
        body, out_type=jax.ShapeDtypeStruct((n_slots, d), rows.dtype), mesh=_sc_mesh(),
        scratch_types=[pltpu.VMEM((SC_WINDOW,), jnp.int32), pltpu.VMEM((SC_WINDOW,), jnp.int32),
                       pltpu.VMEM((SC_WINDOW, d), rows.dtype)],
        name="sc_scatter_rows",
    )(rows, idx0, idx1)


def _sc_gather_rows(table, idx):
    n = idx.shape[0]
    d = table.shape[1]
    per_worker = n // SC_WORKERS
    assert per_worker % SC_WINDOW == 0

    def body(table_hbm, idx_hbm, out_hbm, idx_v, rows_v, sem):
        base = _sc_worker_base(per_worker)

        @pl.loop(0, per_worker // SC_WINDOW)
        def _(j):
            off = pl.multiple_of(base + j * SC_WINDOW, SC_WINDOW)
            pltpu.sync_copy(idx_hbm.at[pl.ds(off, SC_WINDOW)], idx_v)
            pltpu.async_copy(table_hbm.at[idx_v], rows_v, sem).wait()
            pltpu.sync_copy(rows_v, out_hbm.at[pl.ds(off, SC_WINDOW)])

    return pl.kernel(
        body, out_type=jax.ShapeDtypeStruct((n, d), table.dtype), mesh=_sc_mesh(),
        scratch_types=[pltpu.VMEM((SC_WINDOW,), jnp.int32), pltpu.VMEM((SC_WINDOW, d), table.dtype),
                       pltpu.SemaphoreType.DMA],
        name="sc_gather_rows",
    )(table, idx)


def _ffn_kernel(be_ref, nv_ref, xs_ref, wg_ref, wu_ref, wd_ref, ys_ref, wg_b, wu_b, wd_b):
    j = pl.program_id(0)
    valid = nv_ref[j]

    @pl.when((valid > 0) & ((j == 0) | (be_ref[j] != be_ref[jnp.maximum(j - 1, 0)])))
    def _():
        wg_b[...] = wg_ref[0, 0].astype(BF16)
        wu_b[...] = wu_ref[0, 0].astype(BF16)
        wd_b[...] = wd_ref[0, 0].astype(BF16)

    @pl.when(valid > 0)
    def _():
        row = lax.broadcasted_iota(jnp.int32, (SLOT_ROWS, 1), 0)
        xb = jnp.where(row < valid, xs_ref[...], 0.0).astype(BF16)
        g = jnp.dot(xb, wg_b[...], preferred_element_type=F32)
        u = jnp.dot(xb, wu_b[...], preferred_element_type=F32)
        a = (g / (1.0 + jnp.exp(-g)) * u).astype(BF16)
        ys_ref[...] = jnp.dot(a, wd_b[...], preferred_element_type=F32)

    @pl.when(valid <= 0)
    def _():
        ys_ref[...] = jnp.zeros_like(ys_ref)


def _expert_ffn(block_e, n_valid, xs, w_gate, w_up, w_down, layer):
    n_slots, d = xs.shape
    de = w_gate.shape[3]
    w_idx = lambda j, be, nv: (layer, be[j], 0, 0)
    return pl.pallas_call(
        _ffn_kernel,
        grid_spec=pltpu.PrefetchScalarGridSpec(
            num_scalar_prefetch=2,
            grid=(n_slots // SLOT_ROWS,),
            in_specs=[pl.BlockSpec((SLOT_ROWS, d), lambda j, be, nv: (j, 0)),
                      pl.BlockSpec((1, 1, d, de), w_idx),
                      pl.BlockSpec((1, 1, d, de), w_idx),
                      pl.BlockSpec((1, 1, de, d), w_idx)],
            out_specs=pl.BlockSpec((SLOT_ROWS, d), lambda j, be, nv: (j, 0)),
            scratch_shapes=[pltpu.VMEM((d, de), BF16), pltpu.VMEM((d, de), BF16), pltpu.VMEM((de, d), BF16)]),
        out_shape=jax.ShapeDtypeStruct((n_slots, d), F32),
        compiler_params=_params("arbitrary"),
        name="expert_ffn",
    )(block_e, n_valid, xs, w_gate, w_up, w_down)


def _combine_kernel(y0_ref, y1_ref, gcol_ref, x_ref, g2_ref, *rest, pool_in):
    if pool_in:
        sh_ref, sc_ref, n1_ref, wi_ref, x_out, u_out = rest
    else:
        (x_out,) = rest
    gc = gcol_ref[...]
    out = gc[:, 0:1] * y0_ref[0] + gc[:, 1:2] * y1_ref[0]
    x2 = x_ref[0] + g2_ref[0] * out
    x_out[0] = x2
    if pool_in:
        hb = _rms_mod(x2, n1_ref[...], sh_ref[0], sc_ref[0]).astype(BF16)
        u_out[0] = jnp.dot(hb, wi_ref[...], preferred_element_type=F32)


def _combine(yg, gcol, x, g2, pool_args=None, *, tc):
    b, l, d = x.shape
    nt = l // tc
    pool_in = pool_args is not None
    kern = functools.partial(_combine_kernel, pool_in=pool_in)
    row = lambda bi, i: (bi, 0, 0)
    fix = lambda bi, i: (0, 0)
    tile = lambda bi, i: (bi, i, 0)
    in_specs = [pl.BlockSpec((1, tc, d), lambda bi, i: (0, bi * nt + i, 0)),
                pl.BlockSpec((1, tc, d), lambda bi, i: (1, bi * nt + i, 0)),
                pl.BlockSpec((tc, LANES), lambda bi, i: (bi * nt + i, 0)),
                pl.BlockSpec((1, tc, d), tile),
                pl.BlockSpec((1, 1, d), row)]
    out_specs = [pl.BlockSpec((1, tc, d), tile)]
    out_shape = [jax.ShapeDtypeStruct((b, l, d), F32)]
    args = [yg, yg, gcol, x, g2]
    if pool_in:
        in_specs += [pl.BlockSpec((1, 1, d), row), pl.BlockSpec((1, 1, d), row),
                     pl.BlockSpec((1, d), fix), pl.BlockSpec((d, d), fix)]
        out_specs.append(pl.BlockSpec((1, tc, d), tile))
        out_shape.append(jax.ShapeDtypeStruct((b, l, d), F32))
        args += list(pool_args)
    return pl.pallas_call(
        kern,
        grid=(b, nt),
        in_specs=in_specs, out_specs=out_specs, out_shape=out_shape,
        compiler_params=_params("arbitrary", "arbitrary"),
        name="moe_combine",
    )(*args)


def _moe(h2, ridx, gcol, counts, x1, g2, w_gate, w_up, w_down, layer, pool_args=None):
    b, l, d = x1.shape
    n = b * l
    n_blocks = (n * TOP_K) // SLOT_ROWS + N_EXPERTS
    cnt = counts[:, 0].astype(jnp.int32)
    padded = (cnt + SLOT_ROWS - 1) // SLOT_ROWS * SLOT_ROWS
    pad_end = jnp.cumsum(padded)
    pad_start = (pad_end - padded).astype(jnp.int32)
    block_start = jnp.arange(n_blocks, dtype=jnp.int32) * SLOT_ROWS
    block_e = jnp.minimum(jnp.sum(pad_end[None, :] <= block_start[:, None], axis=1), N_EXPERTS - 1).astype(jnp.int32)
    n_valid = jnp.clip(pad_start[block_e] + cnt[block_e] - block_start, 0, SLOT_ROWS).astype(jnp.int32)
    dest = _slot_index(pad_start, ridx, tn=2048)
    xs = _sc_scatter_rows(h2.reshape(n, d), dest[0], dest[1], n_blocks * SLOT_ROWS)
    ys = _expert_ffn(block_e, n_valid, xs, w_gate, w_up, w_down, layer)
    yg = _sc_gather_rows(ys, dest[:TOP_K].reshape(TOP_K * n)).reshape(TOP_K, n, d)
    return _combine(yg, gcol, x1, g2, pool_args, tc=512)


def kernel(x, c, ctx, c_ctx, ada_w, ada_b, norm1_g, norm2_g, attn_w_in, attn_w_out, attn_q_gain, attn_k_gain,
           attn_lq1, attn_lk1, attn_lq2, attn_lk2, attn_sub_gain, pool_w_in, pool_w_group, pool_scale, pool_w_out,
           router_w, router_b, moe_w_gate, moe_w_up, moe_w_down):
    b, l, d = x.shape
    n_ctx = ctx.shape[1]
    depth = ada_w.shape[0]
    assert depth == 2 and d == N_HEADS * V_DIM
    tm = 512

    mod = _adaln_mod(c, c_ctx, ada_w, ada_b)
    mods = [[mod[i, :b, None, j * d:(j + 1) * d] for j in range(N_MOD)] for i in range(depth)]
    mod_ctx = [jnp.broadcast_to(mod[0, b, j * d:(j + 1) * d], (b, 1, d)) for j in range(2)]

    rwt = router_w.T
    rwh = rwt.astype(BF16)
    rwl = (rwt - rwh.astype(F32)).astype(BF16)
    rb = router_b.reshape(N_EXPERTS, 1)

    sh1, s1, g1, sh2, s2, g2 = mods[0]
    cos, sin = _rope_tables(l)
    gain2 = lambda g: jnp.concatenate([g, g])[None]
    qg, kg = gain2(attn_q_gain[0]), gain2(attn_k_gain[0])
    w_in = attn_w_in[0].astype(BF16)
    assert l % tm == 0 and l % n_ctx == 0
    q, k_all, vt_all = _qkv_proj(x, sh1, s1, norm1_g[0][None], w_in, qg, kg, cos, sin, n_qk=2, rope=True,
                                 tm=tm, n_keys=l + n_ctx, key_row0=0)
    k_all, vt_all = _qkv_proj(ctx, mod_ctx[0], mod_ctx[1], norm1_g[0][None], w_in[:, d:], qg, kg,
                              cos[:n_ctx], sin[:n_ctx], n_qk=1, rope=False,
                              tm=n_ctx, n_keys=l + n_ctx, key_row0=l, kv=(k_all, vt_all))
    lam_init = 0.8 - 0.6 * math.exp(-0.3 * 0)
    lam_params = jnp.stack([attn_lq1[0], attn_lk1[0], attn_lq2[0], attn_lk2[0]])
    o = _diff_attention(q, k_all, vt_all, lam_params, attn_sub_gain[0][:, None], lam_init=lam_init, tq=1024)

    fix = lambda bi, i: (0, 0)
    x1, h2, ridx, gcol, counts = _mixer_tail(
        (o, attn_w_out[0].astype(BF16)),
        (pl.BlockSpec((1, tm, d), lambda bi, i: (bi, i, 0)), pl.BlockSpec((d, d), fix)),
        _attn_tail_kernel, x, g1, sh2, s2, norm2_g[0][None], rwh, rwl, rb, tm=tm)

    sh1b, s1b, g1b, sh2b, s2b, g2b = mods[1]
    x2, u = _moe(h2, ridx, gcol, counts, x1, g2, moe_w_gate, moe_w_up, moe_w_down, 0,
                 pool_args=(sh1b, s1b, norm1_g[1][None], pool_w_in[0].astype(BF16)))
    gd = pool_w_group.shape[2]
    nh = tm // POOL_HALO
    front_specs = (
        pl.BlockSpec((1, tm, d), lambda bi, i: (bi, i, 0)),
        pl.BlockSpec((1, POOL_HALO, d), lambda bi, i: (bi, jnp.maximum(i * nh - 1, 0), 0)),
        pl.BlockSpec((1, POOL_HALO, d), lambda bi, i: (bi, jnp.minimum((i + 1) * nh, l // POOL_HALO - 1), 0)),
        pl.BlockSpec((len(POOL_WINDOWS), gd, gd), lambda bi, i: (0, 0, 0)),
        pl.BlockSpec((1, d), fix),
        pl.BlockSpec((d, d), fix))
    x3, h2b, ridx_b, gcol_b, counts_b = _mixer_tail(
        (u, u, u, pool_w_group[0].astype(BF16), pool_scale[0][None], pool_w_out[0].astype(BF16)),
        front_specs, functools.partial(_pool_tail_kernel, seq_len=l),
        x2, g1b, sh2b, s2b, norm2_g[1][None], rwh, rwl, rb, tm=tm,
        scratch=[pltpu.VMEM((tm + 2 * POOL_HALO, d), F32)])
    (out,) = _moe(h2b, ridx_b, gcol_b, counts_b, x3, g2b, moe_w_gate, moe_w_up, moe_w_down, 1)
    return out
```

```python
import functools
import math

import jax
import jax.numpy as jnp
from jax import lax
from jax.experimental import pallas as pl
from jax.experimental.pallas import tpu as pltpu
from jax.experimental.pallas import tpu_sc as plsc

F32 = jnp.float32
BF16 = jnp.bfloat16
F8 = jnp.float8_e4m3fn

LANES = 128
SUBLANES = 8
N_HEADS = 8
HEAD_DIM = 64
V_DIM = 2 * HEAD_DIM
V_ROWS = V_DIM + 32
P_SHIFT = 8.0
GRID_W = 64
ROPE_THETA = 10000.0
NORM_EPS = 1e-6
N_MOD = 6
POOL_WINDOWS = (2, 4, 8, 16)
POOL_HALO = max(POOL_WINDOWS) // 2
N_EXPERTS = 32
N_EXPERT_GROUPS = 4
EXPERTS_PER_GROUP = N_EXPERTS // N_EXPERT_GROUPS
TOP_K = 2
SLOT_ROWS = 512
SC_CORES = 2
SC_SUBCORES = 16
SC_WORKERS = SC_CORES * SC_SUBCORES
SC_WINDOW = 64
ATTN_GROUP_CHUNKS = 1
ATTN_SCORE_AHEAD = 1
VMEM_LIMIT = 48 * 1024 * 1024
NT_DIMS = (((1,), (1,)), ((), ()))


def _params(*sem):
    return pltpu.CompilerParams(dimension_semantics=sem, vmem_limit_bytes=VMEM_LIMIT)


def _rms_mod(x, gain, shift, scale):
    h = x * lax.rsqrt(jnp.mean(x * x, axis=-1, keepdims=True) + NORM_EPS) * gain
    return h * (1.0 + scale) + shift


def _pack_bf16_pairs(x):
    c = x.shape[1] // 2
    hi = lax.bitcast_convert_type(x[:, :c].astype(BF16).astype(F32), jnp.uint32)
    lo = lax.bitcast_convert_type(x[:, c:].astype(BF16).astype(F32), jnp.uint32)
    return lax.bitcast_convert_type(hi | (lo >> 16), jnp.int32)


def _unpack_bf16_pairs(w):
    u = lax.bitcast_convert_type(w, jnp.uint32)
    hi = lax.bitcast_convert_type(u & jnp.uint32(0xFFFF0000), F32)
    lo = lax.bitcast_convert_type(u << 16, F32)
    return jnp.concatenate([hi, lo], axis=1)


def _mod_kernel(c_ref, w_ref, b_ref, o_ref):
    c = c_ref[...]
    a = c / (1.0 + jnp.exp(-c))
    o_ref[0] = jnp.dot(a, w_ref[0], precision=lax.Precision.HIGHEST,
                       preferred_element_type=F32) + b_ref[0]


def _adaln_mod(c, c_ctx, ada_w, ada_b):
    depth, d, n_out = ada_w.shape
    b = c.shape[0]
    assert b + 1 <= SUBLANES
    rows = jnp.concatenate([c, c_ctx[None], jnp.zeros((SUBLANES - b - 1, d), F32)], axis=0)
    tn = n_out // 4
    return pl.pallas_call(
        _mod_kernel,
        grid=(depth, n_out // tn),
        in_specs=[pl.BlockSpec((SUBLANES, d), lambda i, j: (0, 0)),
                  pl.BlockSpec((1, d, tn), lambda i, j: (i, 0, j)),
                  pl.BlockSpec((1, 1, tn), lambda i, j: (i, 0, j))],
        out_specs=pl.BlockSpec((1, SUBLANES, tn), lambda i, j: (i, 0, j)),
        out_shape=jax.ShapeDtypeStruct((depth, SUBLANES, n_out), F32),
        compiler_params=_params("arbitrary", "arbitrary"),
        name="adaln_mod",
    )(rows, ada_w, ada_b.reshape(depth, 1, n_out))


def _qkv_kernel(x_ref, sh_ref, sc_ref, g_ref, w_ref, qg_ref, kg_ref, cos_ref, sin_ref, *refs,
                n_qk, rope):
    out_refs = refs[-(n_qk + 1):]
    tm, d = x_ref.shape[1:]
    hb = _rms_mod(x_ref[0], g_ref[...], sh_ref[0], sc_ref[0]).astype(BF16)
    lane = lax.broadcasted_iota(jnp.int32, (1, LANES), 1)
    lane_lo = lane < HEAD_DIM
    lane_b4 = (lane & 16) == 0
    gains = (qg_ref[...], kg_ref[...])[2 - n_qk:]
    for t in range(n_qk):
        for j in range(0, d, 2 * LANES):
            acc = jnp.dot(hb, w_ref[:, t * d + j:t * d + j + 2 * LANES], preferred_element_type=F32)
            for half in range(2):
                blk = acc[:, half * LANES:(half + 1) * LANES]
                sq = blk * blk
                lo = jnp.sum(jnp.where(lane_lo, sq, 0.0), axis=-1, keepdims=True)
                hi = jnp.sum(jnp.where(lane_lo, 0.0, sq), axis=-1, keepdims=True)
                ms = jnp.where(lane_lo, lo, hi) * (1.0 / HEAD_DIM)
                y = blk * lax.rsqrt(ms + NORM_EPS) * gains[t]
                if rope:
                    rot = jnp.where(lane_b4, pltpu.roll(y, LANES - 16, 1), pltpu.roll(y, 16, 1))
                    y = y * cos_ref[...] + rot * sin_ref[...]
                c0 = j + half * LANES
                out_refs[t][0, :, c0:c0 + LANES] = y.astype(out_refs[t].dtype)
    vt_ref = out_refs[n_qk]
    ones = jnp.where(lax.broadcasted_iota(jnp.int32, (V_ROWS - V_DIM, tm), 0) == 0, 1.0, 0.0).astype(F8)
    for j in range(0, d, 2 * LANES):
        acc = jnp.dot(hb, w_ref[:, n_qk * d + j:n_qk * d + j + 2 * LANES], preferred_element_type=F32)
        acc_t = acc.T
        for half in range(2):
            h = j // LANES + half
            vt_ref[0, h, 0:V_DIM, :] = acc_t[half * V_DIM:(half + 1) * V_DIM].astype(F8)
            vt_ref[0, h, V_DIM:V_ROWS, :] = ones


def _qkv_proj(x, shift, scale, gain, w, q_gain, k_gain, cos, sin, *, n_qk, rope, tm, n_keys, key_row0, kv=None):
    b, l, d = x.shape
    n_out = n_qk + 1
    kern = functools.partial(_qkv_kernel, n_qk=n_qk, rope=rope)
    row = lambda bi, i: (bi, 0, 0)
    fix = lambda bi, i: (0, 0)
    kb = key_row0 // tm
    tile = pl.BlockSpec((1, tm, d), lambda bi, i: (bi, i, 0))
    k_spec = pl.BlockSpec((1, tm, d), lambda bi, i: (bi, kb + i, 0))
    vt_spec = pl.BlockSpec((1, N_HEADS, V_ROWS, tm), lambda bi, i: (bi, 0, 0, kb + i))
    k_shape = jax.ShapeDtypeStruct((b, n_keys, d), F8)
    vt_shape = jax.ShapeDtypeStruct((b, N_HEADS, V_ROWS, n_keys), F8)
    in_specs = [tile,
                pl.BlockSpec((1, 1, d), row), pl.BlockSpec((1, 1, d), row),
                pl.BlockSpec((1, d), fix),
                pl.BlockSpec((d, n_out * d), fix),
                pl.BlockSpec((1, LANES), fix), pl.BlockSpec((1, LANES), fix),
                pl.BlockSpec((tm, LANES), lambda bi, i: (i, 0)),
                pl.BlockSpec((tm, LANES), lambda bi, i: (i, 0))]
    args = [x, shift, scale, gain, w, q_gain, k_gain, cos, sin]
    aliases = {}
    if kv is not None:
        aliases = {len(args): n_qk - 1, len(args) + 1: n_qk}
        in_specs += [pl.BlockSpec(memory_space=pl.ANY)] * 2
        args += list(kv)
    return pl.pallas_call(
        kern,
        grid=(b, l // tm),
        in_specs=in_specs,
        out_specs=[tile] * (n_qk - 1) + [k_spec, vt_spec],
        out_shape=[jax.ShapeDtypeStruct((b, l, d), BF16)] * (n_qk - 1) + [k_shape, vt_shape],
        input_output_aliases=aliases,
        compiler_params=_params("arbitrary", "arbitrary"),
        name="qkv_proj",
    )(*args)


def _rope_tables(n_tokens):
    rows = n_tokens // GRID_W
    row = jnp.repeat(jnp.arange(rows, dtype=F32), GRID_W)
    col = jnp.tile(jnp.arange(GRID_W, dtype=F32), rows)
    half = HEAD_DIM // 2
    inv_freq = ROPE_THETA ** (-jnp.arange(0, half, 2, dtype=F32) / half)
    ang_r = row[:, None] * inv_freq
    ang_c = col[:, None] * inv_freq
    ang = jnp.concatenate([ang_r, ang_r, ang_c, ang_c], axis=-1)
    ang = jnp.concatenate([ang, ang], axis=-1)
    sign = jnp.where((jnp.arange(LANES) & 16) == 0, -1.0, 1.0).astype(F32)
    return jnp.cos(ang), jnp.sin(ang) * sign


def _attn_kernel(q_ref, k_ref, vt_ref, lp_ref, sg_ref, o_ref, s_ref, *, tk, group, ahead, lam_init):
    tq = q_ref.shape[1]
    n_chunks = k_ref.shape[1] // tk
    qt = (q_ref[0].astype(F32) * (HEAD_DIM ** -0.5 * math.log2(math.e))).T
    sub = lax.broadcasted_iota(jnp.int32, (V_DIM, 1), 0)
    first = sub < HEAD_DIM
    qz = jnp.concatenate([jnp.where(first, qt, 0.0), jnp.where(first, 0.0, qt)], axis=1).astype(F8)
    n_slots = (ahead + 1) * group

    def score_chunk(c, m_grp):
        st = jnp.dot(k_ref[0, c * tk:(c + 1) * tk, :], qz, preferred_element_type=F32).astype(BF16)
        slot = c % n_slots
        s_ref[slot * tk:(slot + 1) * tk, :] = st
        mc = jnp.max(st, axis=0, keepdims=True)
        return mc if m_grp is None else jnp.maximum(m_grp, mc)

    def value_chunk(c, m_ref, part):
        slot = c % n_slots
        p = jnp.exp2(s_ref[slot * tk:(slot + 1) * tk, :] - (m_ref - P_SHIFT)).astype(F8)
        pv = jnp.dot(vt_ref[0, 0, :, c * tk:(c + 1) * tk], p, preferred_element_type=F32)
        return pv if part is None else part + pv

    groups = [list(range(g0, min(g0 + group, n_chunks))) for g0 in range(0, n_chunks, group)]
    m_of = {}
    for g in range(min(ahead, len(groups))):
        for c in groups[g]:
            m_of[g] = score_chunk(c, m_of.get(g))
    m = None
    acc = None
    for gi, cur in enumerate(groups):
        nxt = groups[gi + ahead] if gi + ahead < len(groups) else []
        m_new = m_of[gi] if m is None else jnp.maximum(m, m_of[gi])
        part = None
        for i in range(max(len(cur), len(nxt))):
            if i < len(nxt):
                m_of[gi + ahead] = score_chunk(nxt[i], m_of.get(gi + ahead))
            if i < len(cur):
                part = value_chunk(cur[i], m_new, part)
        acc = part if acc is None else acc * jnp.exp2(m.astype(F32) - m_new.astype(F32)) + part
        m = m_new
    acc = acc[:V_DIM] / acc[V_DIM:V_DIM + 1]
    lp = lp_ref[...]
    lam = (jnp.exp(jnp.sum(lp[0:1] * lp[1:2], axis=-1, keepdims=True))
           - jnp.exp(jnp.sum(lp[2:3] * lp[3:4], axis=-1, keepdims=True)) + lam_init)
    o = acc[:, :tq] - lam * acc[:, tq:]
    o = o * lax.rsqrt(jnp.mean(o * o, axis=0, keepdims=True) + NORM_EPS) * sg_ref[...] * (1.0 - lam_init)
    o_ref[0] = o.T.astype(BF16)


def _attn_chunk(n_keys):
    for tk in (768, 512, 256, 128):
        if n_keys % tk == 0:
            return tk
    raise ValueError(f"key count {n_keys} is not a multiple of {LANES}")


def _diff_attention(q, k_all, vt_all, lam_params, sub_gain, *, lam_init, tq):
    b, l, d = q.shape
    n_keys = k_all.shape[1]
    tk = _attn_chunk(n_keys)
    kern = functools.partial(_attn_kernel, tk=tk, group=ATTN_GROUP_CHUNKS, ahead=ATTN_SCORE_AHEAD, lam_init=lam_init)
    return pl.pallas_call(
        kern,
        grid=(b, N_HEADS, l // tq),
        in_specs=[pl.BlockSpec((1, tq, V_DIM), lambda bi, h, i: (bi, i, h)),
                  pl.BlockSpec((1, n_keys, V_DIM), lambda bi, h, i: (bi, 0, h)),
                  pl.BlockSpec((1, 1, V_ROWS, n_keys), lambda bi, h, i: (bi, h, 0, 0)),
                  pl.BlockSpec((4, HEAD_DIM), lambda bi, h, i: (0, 0)),
                  pl.BlockSpec((V_DIM, 1), lambda bi, h, i: (0, 0))],
        out_specs=pl.BlockSpec((1, tq, V_DIM), lambda bi, h, i: (bi, i, h)),
        out_shape=jax.ShapeDtypeStruct((b, l, d), BF16),
        scratch_shapes=[pltpu.VMEM(((ATTN_SCORE_AHEAD + 1) * ATTN_GROUP_CHUNKS * tk, 2 * tq), BF16)],
        compiler_params=_params("arbitrary", "arbitrary", "arbitrary"),
        name="diff_attention",
    )(q, k_all, vt_all, lam_params, sub_gain)


def _route(h2, rwh_ref, rwl_ref, rb_ref, carry_ref, ridx_ref, gcol_ref, cnt_ref, is_first):
    tm = h2.shape[0]
    hh = h2.astype(BF16)
    hl = (h2 - hh.astype(F32)).astype(BF16)
    rw2 = jnp.concatenate([rwh_ref[...], rwl_ref[...]], axis=0)
    part = lax.dot_general(rw2, hh, NT_DIMS, preferred_element_type=F32)
    logits = (part[:N_EXPERTS] + part[N_EXPERTS:]
              + lax.dot_general(rwh_ref[...], hl, NT_DIMS, preferred_element_type=F32) + rb_ref[...])
    groups = [logits[g * EXPERTS_PER_GROUP:(g + 1) * EXPERTS_PER_GROUP] for g in range(N_EXPERT_GROUPS)]
    top = groups[0]
    for g in range(1, N_EXPERT_GROUPS):
        top = jnp.maximum(top, groups[g])
    top = jnp.max(top, axis=0, keepdims=True)
    sub = lax.broadcasted_iota(jnp.int32, (EXPERTS_PER_GROUP, tm), 0)
    best = None
    for g in range(N_EXPERT_GROUPS):
        ex = jnp.exp(groups[g] - top)
        v1 = jnp.max(ex, axis=0, keepdims=True)
        i1 = jnp.min(jnp.where(ex == v1, sub, EXPERTS_PER_GROUP), axis=0, keepdims=True)
        rest = jnp.where(sub == i1, -1.0, ex)
        v2 = jnp.max(rest, axis=0, keepdims=True)
        i2 = jnp.min(jnp.where(rest == v2, sub, EXPERTS_PER_GROUP), axis=0, keepdims=True)
        cand = (v1 + v2, v1, v2, i1 + g * EXPERTS_PER_GROUP, i2 + g * EXPERTS_PER_GROUP)
        if best is None:
            best = cand
        else:
            better = cand[0] > best[0]
            best = tuple(jnp.where(better, new, old) for new, old in zip(cand, best))
    _, v1, v2, e0, e1 = best
    gate0 = v1 / (v1 + v2)
    gate1 = v2 / (v1 + v2)

    @pl.when(is_first)
    def _():
        carry_ref[...] = jnp.zeros_like(carry_ref)

    erow = lax.broadcasted_iota(jnp.int32, (N_EXPERTS, tm), 0)
    oh0 = erow == e0
    oh1 = erow == e1
    chosen = jnp.where(oh0 | oh1, 1.0, 0.0)
    before = (lax.broadcasted_iota(jnp.int32, (tm, tm), 0)
              < lax.broadcasted_iota(jnp.int32, (tm, tm), 1)).astype(BF16)
    prior = carry_ref[:, 0:1] + jnp.dot(chosen.astype(BF16), before, preferred_element_type=F32)
    r0 = jnp.sum(jnp.where(oh0, prior, 0.0), axis=0, keepdims=True).astype(jnp.int32)
    r1 = jnp.sum(jnp.where(oh1, prior, 0.0), axis=0, keepdims=True).astype(jnp.int32)
    carry_ref[...] = carry_ref[...] + jnp.sum(chosen, axis=1, keepdims=True)
    cnt_ref[...] = carry_ref[...]
    rid = lax.broadcasted_iota(jnp.int32, (SUBLANES, tm), 0)
    ridx_ref[...] = jnp.where(rid == 0, e0, jnp.where(rid == 1, e1, jnp.where(rid == 2, r0, jnp.where(rid == 3, r1, 0))))
    gid = lax.broadcasted_iota(jnp.int32, (LANES, tm), 0)
    gcol_ref[...] = jnp.where(gid == 0, gate0, jnp.where(gid == 1, gate1, 0.0)).T


def _tail(y, x_ref, g1_ref, sh2_ref, s2_ref, n2_ref, rwh_ref, rwl_ref, rb_ref,
          x_out, h2_out, ridx_ref, gcol_ref, cnt_ref, carry_ref):
    x1 = x_ref[0] + g1_ref[0] * y
    x_out[0] = x1
    h2 = _rms_mod(x1, n2_ref[...], sh2_ref[0], s2_ref[0])
    h2_out[0] = _pack_bf16_pairs(h2)
    is_first = (pl.program_id(0) == 0) & (pl.program_id(1) == 0)
    _route(h2, rwh_ref, rwl_ref, rb_ref, carry_ref, ridx_ref, gcol_ref, cnt_ref, is_first)


def _attn_tail_kernel(a_ref, wo_ref, *rest):
    y = jnp.dot(a_ref[0], wo_ref[...], preferred_element_type=F32)
    _tail(y, *rest)


def _pool_tail_kernel(u_ref, up_ref, un_ref, wg_ref, cs_ref, wo_ref, *rest, seq_len):
    *tail_refs, ubuf = rest
    tm = u_ref.shape[1]
    i = pl.program_id(1)
    u = u_ref[0]
    ubuf[0:POOL_HALO] = jnp.where(i > 0, up_ref[0], 0.0)
    ubuf[POOL_HALO:POOL_HALO + tm] = u
    ubuf[POOL_HALO + tm:2 * POOL_HALO + tm] = jnp.where(i < pl.num_programs(1) - 1, un_ref[0], 0.0)
    pos = i * tm + lax.broadcasted_iota(jnp.int32, (tm, 1), 0)
    gd = wg_ref.shape[1]
    outs = []
    for g, win in enumerate(POOL_WINDOWS):
        half = win // 2
        cols = slice(g * gd, (g + 1) * gd)
        s = ubuf[POOL_HALO - half:POOL_HALO - half + tm, cols]
        for j in range(1 - half, half):
            s = s + ubuf[POOL_HALO + j:POOL_HALO + j + tm, cols]
        cnt = (jnp.minimum(pos + half, seq_len) - jnp.maximum(pos - half, 0)).astype(F32)
        dlt = (s / cnt - u[:, cols]).astype(BF16)
        outs.append(jnp.dot(dlt, wg_ref[g], preferred_element_type=F32))
    z = (jnp.concatenate(outs, axis=-1) * cs_ref[...]).astype(BF16)
    y = jnp.dot(z, wo_ref[...], preferred_element_type=F32)
    _tail(y, *tail_refs)


def _mixer_tail(front_args, front_specs, kern, x, g1, sh2, s2, n2g, rwh, rwl, rb, *, tm, scratch=()):
    b, l, d = x.shape
    nt = l // tm
    n = b * l
    row = lambda bi, i: (bi, 0, 0)
    fix = lambda bi, i: (0, 0)
    tile = lambda bi, i: (bi, i, 0)
    in_specs = list(front_specs) + [
        pl.BlockSpec((1, tm, d), tile),
        pl.BlockSpec((1, 1, d), row), pl.BlockSpec((1, 1, d), row), pl.BlockSpec((1, 1, d), row),
        pl.BlockSpec((1, d), fix),
        pl.BlockSpec((N_EXPERTS, d), fix), pl.BlockSpec((N_EXPERTS, d), fix),
        pl.BlockSpec((N_EXPERTS, 1), fix)]
    out_specs = [pl.BlockSpec((1, tm, d), tile), pl.BlockSpec((1, tm, d // 2), tile),
                 pl.BlockSpec((SUBLANES, tm), lambda bi, i: (0, bi * nt + i)),
                 pl.BlockSpec((tm, LANES), lambda bi, i: (bi * nt + i, 0)),
                 pl.BlockSpec((N_EXPERTS, LANES), fix)]
    out_shape = [jax.ShapeDtypeStruct((b, l, d), F32), jax.ShapeDtypeStruct((b, l, d // 2), jnp.int32),
                 jax.ShapeDtypeStruct((SUBLANES, n), jnp.int32), jax.ShapeDtypeStruct((n, LANES), F32),
                 jax.ShapeDtypeStruct((N_EXPERTS, LANES), F32)]
    return pl.pallas_call(
        kern,
        grid=(b, nt),
        in_specs=in_specs, out_specs=out_specs, out_shape=out_shape,
        scratch_shapes=[pltpu.VMEM((N_EXPERTS, LANES), F32)] + list(scratch),
        compiler_params=_params("arbitrary", "arbitrary"),
        name="mixer_tail",
    )(*front_args, x, g1, sh2, s2, n2g, rwh, rwl, rb)


def _slot_kernel(ps_ref, ridx_ref, dest_ref):
    ridx = ridx_ref[...]
    ps = ps_ref[...]
    erow = lax.broadcasted_iota(jnp.int32, (N_EXPERTS, ridx.shape[1]), 0)
    rows = []
    for k in range(TOP_K):
        start = jnp.sum(jnp.where(erow == ridx[k:k + 1], ps, 0), axis=0, keepdims=True)
        rows.append(start + ridx[TOP_K + k:TOP_K + k + 1])
    rid = lax.broadcasted_iota(jnp.int32, ridx.shape, 0)
    dest_ref[...] = jnp.where(rid == 0, rows[0], jnp.where(rid == 1, rows[1], 0))


def _slot_index(pad_start, ridx, *, tn):
    n = ridx.shape[1]
    return pl.pallas_call(
        _slot_kernel,
        grid=(n // tn,),
        in_specs=[pl.BlockSpec((N_EXPERTS, 1), lambda i: (0, 0)),
                  pl.BlockSpec((SUBLANES, tn), lambda i: (0, i))],
        out_specs=pl.BlockSpec((SUBLANES, tn), lambda i: (0, i)),
        out_shape=jax.ShapeDtypeStruct((SUBLANES, n), jnp.int32),
        compiler_params=_params("arbitrary"),
        name="slot_index",
    )(pad_start.reshape(N_EXPERTS, 1), ridx)


def _sc_mesh():
    return plsc.VectorSubcoreMesh(core_axis_name="c", subcore_axis_name="s",
                                  num_cores=SC_CORES, num_subcores=SC_SUBCORES)


def _sc_worker_base(per_worker):
    return (lax.axis_index("s") * SC_CORES + lax.axis_index("c")) * per_worker


def _sc_scatter_rows(rows, idx0, idx1, n_slots):
    n, d = rows.shape
    per_worker = n // SC_WORKERS
    assert per_worker % SC_WINDOW == 0

    def body(rows_hbm, i0_hbm, i1_hbm, out_hbm, i0_v, i1_v, rows_v):
        base = _sc_worker_base(per_worker)

        @pl.loop(0, per_worker // SC_WINDOW)
        def _(j):
            off = pl.multiple_of(base + j * SC_WINDOW, SC_WINDOW)
            pltpu.sync_copy(i0_hbm.at[pl.ds(off, SC_WINDOW)], i0_v)
            pltpu.sync_copy(i1_hbm.at[pl.ds(off, SC_WINDOW)], i1_v)
            pltpu.sync_copy(rows_hbm.at[pl.ds(off, SC_WINDOW)], rows_v)
            pltpu.sync_copy(rows_v, out_hbm.at[i0_v])
            pltpu.sync_copy(rows_v, out_hbm.at[i1_v])

    return pl.kernel(
        body, out_type=jax.ShapeDtypeStruct((n_slots, d), rows.dtype), mesh=_sc_mesh(),
        scratch_types=[pltpu.VMEM((SC_WINDOW,), jnp.int32), pltpu.VMEM((SC_WINDOW,), jnp.int32),
                       pltpu.VMEM((SC_WINDOW, d), rows.dtype)],
        name="sc_scatter_rows",
    )(rows, idx0, idx1)


def _sc_gather_rows(table, idx):
    n = idx.shape[0]
    d = table.shape[1]
    per_worker = n // SC_WORKERS
    assert per_worker % SC_WINDOW == 0

    def body(table_hbm, idx_hbm, out_hbm, idx_v, rows_v, sem):
        base = _sc_worker_base(per_worker)

        @pl.loop(0, per_worker // SC_WINDOW)
        def _(j):
            off = pl.multiple_of(base + j * SC_WINDOW, SC_WINDOW)
            pltpu.sync_copy(idx_hbm.at[pl.ds(off, SC_WINDOW)], idx_v)
            pltpu.async_copy(table_hbm.at[idx_v], rows_v, sem).wait()
            pltpu.sync_copy(rows_v, out_hbm.at[pl.ds(off, SC_WINDOW)])

    return pl.kernel(
        body, out_type=jax.ShapeDtypeStruct((n, d), table.dtype), mesh=_sc_mesh(),
        scratch_types=[pltpu.VMEM((SC_WINDOW,), jnp.int32), pltpu.VMEM((SC_WINDOW, d), table.dtype),
                       pltpu.SemaphoreType.DMA],
        name="sc_gather_rows",
    )(table, idx)


def _ffn_kernel(be_ref, nv_ref, xs_ref, wg_ref, wu_ref, wd_ref, ys_ref, wg_b, wu_b, wd_b):
    j = pl.program_id(0)
    valid = nv_ref[j]

    @pl.when((valid > 0) & ((j == 0) | (be_ref[j] != be_ref[jnp.maximum(j - 1, 0)])))
    def _():
        wg_b[...] = wg_ref[0, 0].astype(BF16)
        wu_b[...] = wu_ref[0, 0].astype(BF16)
        wd_b[...] = wd_ref[0, 0].astype(BF16)

    @pl.when(valid > 0)
    def _():
        row = lax.broadcasted_iota(jnp.int32, (SLOT_ROWS, 1), 0)
        xw = jnp.where(row < valid, xs_ref[...], 0)
        xb = _unpack_bf16_pairs(xw).astype(BF16)
        g = jnp.dot(xb, wg_b[...], preferred_element_type=F32)
        u = jnp.dot(xb, wu_b[...], preferred_element_type=F32)
        a = (g / (1.0 + jnp.exp(-g)) * u).astype(BF16)
        ys_ref[...] = _pack_bf16_pairs(jnp.dot(a, wd_b[...], preferred_element_type=F32))

    @pl.when(valid <= 0)
    def _():
        ys_ref[...] = jnp.zeros_like(ys_ref)


def _expert_ffn(block_e, n_valid, xs, w_gate, w_up, w_down, layer):
    n_slots, dw = xs.shape
    d, de = w_gate.shape[2:]
    w_idx = lambda j, be, nv: (layer, be[j], 0, 0)
    return pl.pallas_call(
        _ffn_kernel,
        grid_spec=pltpu.PrefetchScalarGridSpec(
            num_scalar_prefetch=2,
            grid=(n_slots // SLOT_ROWS,),
            in_specs=[pl.BlockSpec((SLOT_ROWS, dw), lambda j, be, nv: (j, 0)),
                      pl.BlockSpec((1, 1, d, de), w_idx),
                      pl.BlockSpec((1, 1, d, de), w_idx),
                      pl.BlockSpec((1, 1, de, d), w_idx)],
            out_specs=pl.BlockSpec((SLOT_ROWS, dw), lambda j, be, nv: (j, 0)),
            scratch_shapes=[pltpu.VMEM((d, de), BF16), pltpu.VMEM((d, de), BF16), pltpu.VMEM((de, d), BF16)]),
        out_shape=jax.ShapeDtypeStruct((n_slots, dw), jnp.int32),
        compiler_params=_params("arbitrary"),
        name="expert_ffn",
    )(block_e, n_valid, xs, w_gate, w_up, w_down)


def _combine_kernel(y0_ref, y1_ref, gcol_ref, x_ref, g2_ref, *rest, pool_in):
    if pool_in:
        sh_ref, sc_ref, n1_ref, wi_ref, x_out, u_out = rest
    else:
        (x_out,) = rest
    gc = gcol_ref[...]
    out = gc[:, 0:1] * _unpack_bf16_pairs(y0_ref[0]) + gc[:, 1:2] * _unpack_bf16_pairs(y1_ref[0])
    x2 = x_ref[0] + g2_ref[0] * out
    x_out[0] = x2
    if pool_in:
        hb = _rms_mod(x2, n1_ref[...], sh_ref[0], sc_ref[0]).astype(BF16)
        u_out[0] = jnp.dot(hb, wi_ref[...], preferred_element_type=F32)


def _combine(yg, gcol, x, g2, pool_args=None, *, tc):
    b, l, d = x.shape
    nt = l // tc
    pool_in = pool_args is not None
    kern = functools.partial(_combine_kernel, pool_in=pool_in)
    row = lambda bi, i: (bi, 0, 0)
    fix = lambda bi, i: (0, 0)
    tile = lambda bi, i: (bi, i, 0)
    in_specs = [pl.BlockSpec((1, tc, d // 2), lambda bi, i: (0, bi * nt + i, 0)),
                pl.BlockSpec((1, tc, d // 2), lambda bi, i: (1, bi * nt + i, 0)),
                pl.BlockSpec((tc, LANES), lambda bi, i: (bi * nt + i, 0)),
                pl.BlockSpec((1, tc, d), tile),
                pl.BlockSpec((1, 1, d), row)]
    out_specs = [pl.BlockSpec((1, tc, d), tile)]
    out_shape = [jax.ShapeDtypeStruct((b, l, d), F32)]
    args = [yg, yg, gcol, x, g2]
    if pool_in:
        in_specs += [pl.BlockSpec((1, 1, d), row), pl.BlockSpec((1, 1, d), row),
                     pl.BlockSpec((1, d), fix), pl.BlockSpec((d, d), fix)]
        out_specs.append(pl.BlockSpec((1, tc, d), tile))
        out_shape.append(jax.ShapeDtypeStruct((b, l, d), F32))
        args += list(pool_args)
    return pl.pallas_call(
        kern,
        grid=(b, nt),
        in_specs=in_specs, out_specs=out_specs, out_shape=out_shape,
        compiler_params=_params("arbitrary", "arbitrary"),
        name="moe_combine",
    )(*args)


def _moe(h2, ridx, gcol, counts, x1, g2, w_gate, w_up, w_down, layer, pool_args=None):
    b, l, d = x1.shape
    n = b * l
    n_blocks = (n * TOP_K) // SLOT_ROWS + N_EXPERTS
    cnt = counts[:, 0].astype(jnp.int32)
    padded = (cnt + SLOT_ROWS - 1) // SLOT_ROWS * SLOT_ROWS
    pad_end = jnp.cumsum(padded)
    pad_start = (pad_end - padded).astype(jnp.int32)
    block_start = jnp.arange(n_blocks, dtype=jnp.int32) * SLOT_ROWS
    block_e = jnp.minimum(jnp.sum(pad_end[None, :] <= block_start[:, None], axis=1), N_EXPERTS - 1).astype(jnp.int32)
    n_valid = jnp.clip(pad_start[block_e] + cnt[block_e] - block_start, 0, SLOT_ROWS).astype(jnp.int32)
    dest = _slot_index(pad_start, ridx, tn=2048)
    xs = _sc_scatter_rows(h2.reshape(n, d // 2), dest[0], dest[1], n_blocks * SLOT_ROWS)
    ys = _expert_ffn(block_e, n_valid, xs, w_gate, w_up, w_down, layer)
    yg = _sc_gather_rows(ys, dest[:TOP_K].reshape(TOP_K * n)).reshape(TOP_K, n, d // 2)
    return _combine(yg, gcol, x1, g2, pool_args, tc=512)


def kernel(x, c, ctx, c_ctx, ada_w, ada_b, norm1_g, norm2_g, attn_w_in, attn_w_out, attn_q_gain, attn_k_gain,
           attn_lq1, attn_lk1, attn_lq2, attn_lk2, attn_sub_gain, pool_w_in, pool_w_group, pool_scale, pool_w_out,
           router_w, router_b, moe_w_gate, moe_w_up, moe_w_down):
    b, l, d = x.shape
    n_ctx = ctx.shape[1]
    depth = ada_w.shape[0]
    assert depth == 2 and d == N_HEADS * V_DIM
    tm = 512

    mod = _adaln_mod(c, c_ctx, ada_w, ada_b)
    mods = [[mod[i, :b, None, j * d:(j + 1) * d] for j in range(N_MOD)] for i in range(depth)]
    mod_ctx = [jnp.broadcast_to(mod[0, b, j * d:(j + 1) * d], (b, 1, d)) for j in range(2)]

    rwt = router_w.T
    rwh = rwt.astype(BF16)
    rwl = (rwt - rwh.astype(F32)).astype(BF16)
    rb = router_b.reshape(N_EXPERTS, 1)

    sh1, s1, g1, sh2, s2, g2 = mods[0]
    cos, sin = _rope_tables(l)
    gain2 = lambda g: jnp.concatenate([g, g])[None]
    qg, kg = gain2(attn_q_gain[0]), gain2(attn_k_gain[0])
    w_in = attn_w_in[0].astype(BF16)
    assert l % tm == 0 and l % n_ctx == 0
    q, k_all, vt_all = _qkv_proj(x, sh1, s1, norm1_g[0][None], w_in, qg, kg, cos, sin, n_qk=2, rope=True,
                                 tm=tm, n_keys=l + n_ctx, key_row0=0)
    k_all, vt_all = _qkv_proj(ctx, mod_ctx[0], mod_ctx[1], norm1_g[0][None], w_in[:, d:], qg, kg,
                              cos[:n_ctx], sin[:n_ctx], n_qk=1, rope=False,
                              tm=n_ctx, n_keys=l + n_ctx, key_row0=l, kv=(k_all, vt_all))
    lam_init = 0.8 - 0.6 * math.exp(-0.3 * 0)
    lam_params = jnp.stack([attn_lq1[0], attn_lk1[0], attn_lq2[0], attn_lk2[0]])
    o = _diff_attention(q, k_all, vt_all, lam_params, attn_sub_gain[0][:, None], lam_init=lam_init, tq=512)

    fix = lambda bi, i: (0, 0)
    x1, h2, ridx, gcol, counts = _mixer_tail(
        (o, attn_w_out[0].astype(BF16)),
        (pl.BlockSpec((1, tm, d), lambda bi, i: (bi, i, 0)), pl.BlockSpec((d, d), fix)),
        _attn_tail_kernel, x, g1, sh2, s2, norm2_g[0][None], rwh, rwl, rb, tm=tm)

    sh1b, s1b, g1b, sh2b, s2b, g2b = mods[1]
    x2, u = _moe(h2, ridx, gcol, counts, x1, g2, moe_w_gate, moe_w_up, moe_w_down, 0,
                 pool_args=(sh1b, s1b, norm1_g[1][None], pool_w_in[0].astype(BF16)))
    gd = pool_w_group.shape[2]
    nh = tm // POOL_HALO
    front_specs = (
        pl.BlockSpec((1, tm, d), lambda bi, i: (bi, i, 0)),
        pl.BlockSpec((1, POOL_HALO, d), lambda bi, i: (bi, jnp.maximum(i * nh - 1, 0), 0)),
        pl.BlockSpec((1, POOL_HALO, d), lambda bi, i: (bi, jnp.minimum((i + 1) * nh, l // POOL_HALO - 1), 0)),
        pl.BlockSpec((len(POOL_WINDOWS), gd, gd), lambda bi, i: (0, 0, 0)),
        pl.BlockSpec((1, d), fix),
        pl.BlockSpec((d, d), fix))
    x3, h2b, ridx_b, gcol_b, counts_b = _mixer_tail(
        (u, u, u, pool_w_group[0].astype(BF16), pool_scale[0][None], pool_w_out[0].astype(BF16)),
        front_specs, functools.partial(_pool_tail_kernel, seq_len=l),
        x2, g1b, sh2b, s2b, norm2_g[1][None], rwh, rwl, rb, tm=tm,
        scratch=[pltpu.VMEM((tm + 2 * POOL_HALO, d), F32)])
    (out,) = _moe(h2b, ridx_b, gcol_b, counts_b, x3, g2b, moe_w_gate, moe_w_up, moe_w_down, 1)
    return out
```

```python
import functools
import math

import jax
import jax.numpy as jnp
from jax import lax
from jax.experimental import pallas as pl
from jax.experimental.pallas import tpu as pltpu
from jax.experimental.pallas import tpu_sc as plsc

F32 = jnp.float32
BF16 = jnp.bfloat16
F8 = jnp.float8_e4m3fn

LANES = 128
SUBLANES = 8
N_HEADS = 8
HEAD_DIM = 64
V_DIM = 2 * HEAD_DIM
V_ROWS = V_DIM + 32
P_SHIFT = 8.0
GRID_W = 64
ROPE_THETA = 10000.0
NORM_EPS = 1e-6
N_MOD = 6
POOL_WINDOWS = (2, 4, 8, 16)
POOL_HALO = max(POOL_WINDOWS) // 2
N_EXPERTS = 32
N_EXPERT_GROUPS = 4
EXPERTS_PER_GROUP = N_EXPERTS // N_EXPERT_GROUPS
TOP_K = 2
SLOT_ROWS = 512
SC_CORES = 2
SC_SUBCORES = 16
SC_WORKERS = SC_CORES * SC_SUBCORES
SC_WINDOW = 64
ATTN_GROUP_CHUNKS = 1
ATTN_SCORE_AHEAD = 1
VMEM_LIMIT = 48 * 1024 * 1024
NT_DIMS = (((1,), (1,)), ((), ()))


def _params(*sem):
    return pltpu.CompilerParams(dimension_semantics=sem, vmem_limit_bytes=VMEM_LIMIT)


def _rms_mod(x, gain, shift, scale):
    h = x * lax.rsqrt(jnp.mean(x * x, axis=-1, keepdims=True) + NORM_EPS) * gain
    return h * (1.0 + scale) + shift


def _pack_bf16_pairs(x):
    c = x.shape[1] // 2
    hi = lax.bitcast_convert_type(x[:, :c].astype(BF16).astype(F32), jnp.uint32)
    lo = lax.bitcast_convert_type(x[:, c:].astype(BF16).astype(F32), jnp.uint32)
    return lax.bitcast_convert_type(hi | (lo >> 16), jnp.int32)


def _unpack_bf16_pairs(w):
    u = lax.bitcast_convert_type(w, jnp.uint32)
    hi = lax.bitcast_convert_type(u & jnp.uint32(0xFFFF0000), F32)
    lo = lax.bitcast_convert_type(u << 16, F32)
    return jnp.concatenate([hi, lo], axis=1)


def _mod_kernel(c_ref, w_ref, b_ref, o_ref):
    c = c_ref[...]
    a = c / (1.0 + jnp.exp(-c))
    o_ref[0] = jnp.dot(a, w_ref[0], precision=lax.Precision.HIGHEST,
                       preferred_element_type=F32) + b_ref[0]


def _adaln_mod(c, c_ctx, ada_w, ada_b):
    depth, d, n_out = ada_w.shape
    b = c.shape[0]
    assert b + 1 <= SUBLANES
    rows = jnp.concatenate([c, c_ctx[None], jnp.zeros((SUBLANES - b - 1, d), F32)], axis=0)
    tn = n_out // 4
    return pl.pallas_call(
        _mod_kernel,
        grid=(depth, n_out // tn),
        in_specs=[pl.BlockSpec((SUBLANES, d), lambda i, j: (0, 0)),
                  pl.BlockSpec((1, d, tn), lambda i, j: (i, 0, j)),
                  pl.BlockSpec((1, 1, tn), lambda i, j: (i, 0, j))],
        out_specs=pl.BlockSpec((1, SUBLANES, tn), lambda i, j: (i, 0, j)),
        out_shape=jax.ShapeDtypeStruct((depth, SUBLANES, n_out), F32),
        compiler_params=_params("arbitrary", "arbitrary"),
        name="adaln_mod",
    )(rows, ada_w, ada_b.reshape(depth, 1, n_out))


def _qkv_kernel(x_ref, sh_ref, sc_ref, g_ref, w_ref, qg_ref, kg_ref, cos_ref, sin_ref, *refs,
                n_qk, rope):
    out_refs = refs[-(n_qk + 1):]
    tm, d = x_ref.shape[1:]
    hb = _rms_mod(x_ref[0], g_ref[...], sh_ref[0], sc_ref[0]).astype(BF16)
    lane = lax.broadcasted_iota(jnp.int32, (1, LANES), 1)
    lane_lo = lane < HEAD_DIM
    lane_b4 = (lane & 16) == 0
    gains = (qg_ref[...], kg_ref[...])[2 - n_qk:]
    for t in range(n_qk):
        for j in range(0, d, 2 * LANES):
            acc = jnp.dot(hb, w_ref[:, t * d + j:t * d + j + 2 * LANES], preferred_element_type=F32)
            for half in range(2):
                blk = acc[:, half * LANES:(half + 1) * LANES]
                sq = blk * blk
                lo = jnp.sum(jnp.where(lane_lo, sq, 0.0), axis=-1, keepdims=True)
                hi = jnp.sum(jnp.where(lane_lo, 0.0, sq), axis=-1, keepdims=True)
                ms = jnp.where(lane_lo, lo, hi) * (1.0 / HEAD_DIM)
                y = blk * lax.rsqrt(ms + NORM_EPS) * gains[t]
                if rope:
                    rot = jnp.where(lane_b4, pltpu.roll(y, LANES - 16, 1), pltpu.roll(y, 16, 1))
                    y = y * cos_ref[...] + rot * sin_ref[...]
                c0 = j + half * LANES
                if t < n_qk - 1:
                    yt = (y * (HEAD_DIM ** -0.5 * math.log2(math.e))).T
                    first = lax.broadcasted_iota(jnp.int32, (V_DIM, 1), 0) < HEAD_DIM
                    out_refs[t][0, c0 // LANES, 0, :, 0:tm] = jnp.where(first, yt, 0.0).astype(F8)
                    out_refs[t][0, c0 // LANES, 0, :, tm:2 * tm] = jnp.where(first, 0.0, yt).astype(F8)
                else:
                    out_refs[t][0, :, c0:c0 + LANES] = y.astype(F8)
    vt_ref = out_refs[n_qk]
    ones = jnp.where(lax.broadcasted_iota(jnp.int32, (V_ROWS - V_DIM, tm), 0) == 0, 1.0, 0.0).astype(F8)
    for j in range(0, d, 2 * LANES):
        acc = jnp.dot(hb, w_ref[:, n_qk * d + j:n_qk * d + j + 2 * LANES], preferred_element_type=F32)
        acc_t = acc.T
        for half in range(2):
            h = j // LANES + half
            vt_ref[0, h, 0:V_DIM, :] = acc_t[half * V_DIM:(half + 1) * V_DIM].astype(F8)
            vt_ref[0, h, V_DIM:V_ROWS, :] = ones


def _qkv_proj(x, shift, scale, gain, w, q_gain, k_gain, cos, sin, *, n_qk, rope, tm, n_keys, key_row0, kv=None):
    b, l, d = x.shape
    n_out = n_qk + 1
    kern = functools.partial(_qkv_kernel, n_qk=n_qk, rope=rope)
    row = lambda bi, i: (bi, 0, 0)
    fix = lambda bi, i: (0, 0)
    kb = key_row0 // tm
    tile = pl.BlockSpec((1, tm, d), lambda bi, i: (bi, i, 0))
    k_spec = pl.BlockSpec((1, tm, d), lambda bi, i: (bi, kb + i, 0))
    vt_spec = pl.BlockSpec((1, N_HEADS, V_ROWS, tm), lambda bi, i: (bi, 0, 0, kb + i))
    q_spec = pl.BlockSpec((1, N_HEADS, 1, V_DIM, 2 * tm), lambda bi, i: (bi, 0, i, 0, 0))
    q_shape = jax.ShapeDtypeStruct((b, N_HEADS, l // tm, V_DIM, 2 * tm), F8)
    k_shape = jax.ShapeDtypeStruct((b, n_keys, d), F8)
    vt_shape = jax.ShapeDtypeStruct((b, N_HEADS, V_ROWS, n_keys), F8)
    in_specs = [tile,
                pl.BlockSpec((1, 1, d), row), pl.BlockSpec((1, 1, d), row),
                pl.BlockSpec((1, d), fix),
                pl.BlockSpec((d, n_out * d), fix),
                pl.BlockSpec((1, LANES), fix), pl.BlockSpec((1, LANES), fix),
                pl.BlockSpec((tm, LANES), lambda bi, i: (i, 0)),
                pl.BlockSpec((tm, LANES), lambda bi, i: (i, 0))]
    args = [x, shift, scale, gain, w, q_gain, k_gain, cos, sin]
    aliases = {}
    if kv is not None:
        aliases = {len(args): n_qk - 1, len(args) + 1: n_qk}
        in_specs += [pl.BlockSpec(memory_space=pl.ANY)] * 2
        args += list(kv)
    return pl.pallas_call(
        kern,
        grid=(b, l // tm),
        in_specs=in_specs,
        out_specs=[q_spec] * (n_qk - 1) + [k_spec, vt_spec],
        out_shape=[q_shape] * (n_qk - 1) + [k_shape, vt_shape],
        input_output_aliases=aliases,
        compiler_params=_params("arbitrary", "arbitrary"),
        name="qkv_proj",
    )(*args)


def _rope_tables(n_tokens):
    rows = n_tokens // GRID_W
    row = jnp.repeat(jnp.arange(rows, dtype=F32), GRID_W)
    col = jnp.tile(jnp.arange(GRID_W, dtype=F32), rows)
    half = HEAD_DIM // 2
    inv_freq = ROPE_THETA ** (-jnp.arange(0, half, 2, dtype=F32) / half)
    ang_r = row[:, None] * inv_freq
    ang_c = col[:, None] * inv_freq
    ang = jnp.concatenate([ang_r, ang_r, ang_c, ang_c], axis=-1)
    ang = jnp.concatenate([ang, ang], axis=-1)
    sign = jnp.where((jnp.arange(LANES) & 16) == 0, -1.0, 1.0).astype(F32)
    return jnp.cos(ang), jnp.sin(ang) * sign


def _attn_kernel(q_ref, k_ref, vt_ref, lp_ref, sg_ref, o_ref, s_ref, *, tk, group, ahead, lam_init):
    tq = q_ref.shape[4] // 2
    n_chunks = k_ref.shape[1] // tk
    qz = q_ref[0, 0, 0]
    n_slots = (ahead + 1) * group

    def score_chunk(c, m_grp):
        st = jnp.dot(k_ref[0, c * tk:(c + 1) * tk, :], qz, preferred_element_type=F32).astype(BF16)
        slot = c % n_slots
        s_ref[slot * tk:(slot + 1) * tk, :] = st
        mc = jnp.max(st, axis=0, keepdims=True)
        return mc if m_grp is None else jnp.maximum(m_grp, mc)

    def value_chunk(c, m_ref, part):
        slot = c % n_slots
        p = jnp.exp2(s_ref[slot * tk:(slot + 1) * tk, :] - (m_ref - P_SHIFT)).astype(F8)
        pv = jnp.dot(vt_ref[0, 0, :, c * tk:(c + 1) * tk], p, preferred_element_type=F32)
        return pv if part is None else part + pv

    groups = [list(range(g0, min(g0 + group, n_chunks))) for g0 in range(0, n_chunks, group)]
    m_of = {}
    for g in range(min(ahead, len(groups))):
        for c in groups[g]:
            m_of[g] = score_chunk(c, m_of.get(g))
    m = None
    acc = None
    for gi, cur in enumerate(groups):
        nxt = groups[gi + ahead] if gi + ahead < len(groups) else []
        m_new = m_of[gi] if m is None else jnp.maximum(m, m_of[gi])
        part = None
        for i in range(max(len(cur), len(nxt))):
            if i < len(nxt):
                m_of[gi + ahead] = score_chunk(nxt[i], m_of.get(gi + ahead))
            if i < len(cur):
                part = value_chunk(cur[i], m_new, part)
        acc = part if acc is None else acc * jnp.exp2(m.astype(F32) - m_new.astype(F32)) + part
        m = m_new
    acc = acc[:V_DIM] / acc[V_DIM:V_DIM + 1]
    lp = lp_ref[...]
    lam = (jnp.exp(jnp.sum(lp[0:1] * lp[1:2], axis=-1, keepdims=True))
           - jnp.exp(jnp.sum(lp[2:3] * lp[3:4], axis=-1, keepdims=True)) + lam_init)
    o = acc[:, :tq] - lam * acc[:, tq:]
    o = o * lax.rsqrt(jnp.mean(o * o, axis=0, keepdims=True) + NORM_EPS) * sg_ref[...] * (1.0 - lam_init)
    o_ref[0] = o.T.astype(BF16)


def _attn_chunk(n_keys):
    for tk in (768, 512, 256, 128):
        if n_keys % tk == 0:
            return tk
    raise ValueError(f"key count {n_keys} is not a multiple of {LANES}")


def _diff_attention(qz, k_all, vt_all, lam_params, sub_gain, *, lam_init):
    b, _, n_tiles, _, tq2 = qz.shape
    tq = tq2 // 2
    l, d = n_tiles * tq, N_HEADS * V_DIM
    n_keys = k_all.shape[1]
    tk = _attn_chunk(n_keys)
    kern = functools.partial(_attn_kernel, tk=tk, group=ATTN_GROUP_CHUNKS, ahead=ATTN_SCORE_AHEAD, lam_init=lam_init)
    return pl.pallas_call(
        kern,
        grid=(b, N_HEADS, l // tq),
        in_specs=[pl.BlockSpec((1, 1, 1, V_DIM, 2 * tq), lambda bi, h, i: (bi, h, i, 0, 0)),
                  pl.BlockSpec((1, n_keys, V_DIM), lambda bi, h, i: (bi, 0, h)),
                  pl.BlockSpec((1, 1, V_ROWS, n_keys), lambda bi, h, i: (bi, h, 0, 0)),
                  pl.BlockSpec((4, HEAD_DIM), lambda bi, h, i: (0, 0)),
                  pl.BlockSpec((V_DIM, 1), lambda bi, h, i: (0, 0))],
        out_specs=pl.BlockSpec((1, tq, V_DIM), lambda bi, h, i: (bi, i, h)),
        out_shape=jax.ShapeDtypeStruct((b, l, d), BF16),
        scratch_shapes=[pltpu.VMEM(((ATTN_SCORE_AHEAD + 1) * ATTN_GROUP_CHUNKS * tk, 2 * tq), BF16)],
        compiler_params=_params("arbitrary", "arbitrary", "arbitrary"),
        name="diff_attention",
    )(qz, k_all, vt_all, lam_params, sub_gain)


def _route(h2, rwh_ref, rwl_ref, rb_ref, carry_ref, ridx_ref, gcol_ref, cnt_ref, is_first):
    tm = h2.shape[0]
    hh = h2.astype(BF16)
    hl = (h2 - hh.astype(F32)).astype(BF16)
    rw2 = jnp.concatenate([rwh_ref[...], rwl_ref[...]], axis=0)
    part = lax.dot_general(rw2, hh, NT_DIMS, preferred_element_type=F32)
    logits = (part[:N_EXPERTS] + part[N_EXPERTS:]
              + lax.dot_general(rwh_ref[...], hl, NT_DIMS, preferred_element_type=F32) + rb_ref[...])
    groups = [logits[g * EXPERTS_PER_GROUP:(g + 1) * EXPERTS_PER_GROUP] for g in range(N_EXPERT_GROUPS)]
    top = groups[0]
    for g in range(1, N_EXPERT_GROUPS):
        top = jnp.maximum(top, groups[g])
    top = jnp.max(top, axis=0, keepdims=True)
    sub = lax.broadcasted_iota(jnp.int32, (EXPERTS_PER_GROUP, tm), 0)
    best = None
    for g in range(N_EXPERT_GROUPS):
        ex = jnp.exp(groups[g] - top)
        v1 = jnp.max(ex, axis=0, keepdims=True)
        i1 = jnp.min(jnp.where(ex == v1, sub, EXPERTS_PER_GROUP), axis=0, keepdims=True)
        rest = jnp.where(sub == i1, -1.0, ex)
        v2 = jnp.max(rest, axis=0, keepdims=True)
        i2 = jnp.min(jnp.where(rest == v2, sub, EXPERTS_PER_GROUP), axis=0, keepdims=True)
        cand = (v1 + v2, v1, v2, i1 + g * EXPERTS_PER_GROUP, i2 + g * EXPERTS_PER_GROUP)
        if best is None:
            best = cand
        else:
            better = cand[0] > best[0]
            best = tuple(jnp.where(better, new, old) for new, old in zip(cand, best))
    _, v1, v2, e0, e1 = best
    gate0 = v1 / (v1 + v2)
    gate1 = v2 / (v1 + v2)

    @pl.when(is_first)
    def _():
        carry_ref[...] = jnp.zeros_like(carry_ref)

    erow = lax.broadcasted_iota(jnp.int32, (N_EXPERTS, tm), 0)
    oh0 = erow == e0
    oh1 = erow == e1
    chosen = jnp.where(oh0 | oh1, 1.0, 0.0)
    before = (lax.broadcasted_iota(jnp.int32, (tm, tm), 0)
              < lax.broadcasted_iota(jnp.int32, (tm, tm), 1)).astype(BF16)
    prior = carry_ref[:, 0:1] + jnp.dot(chosen.astype(BF16), before, preferred_element_type=F32)
    r0 = jnp.sum(jnp.where(oh0, prior, 0.0), axis=0, keepdims=True).astype(jnp.int32)
    r1 = jnp.sum(jnp.where(oh1, prior, 0.0), axis=0, keepdims=True).astype(jnp.int32)
    carry_ref[...] = carry_ref[...] + jnp.sum(chosen, axis=1, keepdims=True)
    cnt_ref[...] = carry_ref[...]
    rid = lax.broadcasted_iota(jnp.int32, (SUBLANES, tm), 0)
    ridx_ref[...] = jnp.where(rid == 0, e0, jnp.where(rid == 1, e1, jnp.where(rid == 2, r0, jnp.where(rid == 3, r1, 0))))
    gid = lax.broadcasted_iota(jnp.int32, (LANES, tm), 0)
    gcol_ref[...] = jnp.where(gid == 0, gate0, jnp.where(gid == 1, gate1, 0.0)).T


def _tail(y, x_ref, g1_ref, sh2_ref, s2_ref, n2_ref, rwh_ref, rwl_ref, rb_ref,
          x_out, h2_out, ridx_ref, gcol_ref, cnt_ref, carry_ref):
    x1 = x_ref[0] + g1_ref[0] * y
    x_out[0] = x1
    h2 = _rms_mod(x1, n2_ref[...], sh2_ref[0], s2_ref[0])
    h2_out[0] = _pack_bf16_pairs(h2)
    is_first = (pl.program_id(0) == 0) & (pl.program_id(1) == 0)
    _route(h2, rwh_ref, rwl_ref, rb_ref, carry_ref, ridx_ref, gcol_ref, cnt_ref, is_first)


def _attn_tail_kernel(a_ref, wo_ref, *rest):
    y = jnp.dot(a_ref[0], wo_ref[...], preferred_element_type=F32)
    _tail(y, *rest)


def _pool_tail_kernel(u_ref, up_ref, un_ref, wg_ref, cs_ref, wo_ref, *rest, seq_len):
    *tail_refs, ubuf = rest
    tm = u_ref.shape[1]
    i = pl.program_id(1)
    u = u_ref[0]
    ubuf[0:POOL_HALO] = jnp.where(i > 0, up_ref[0], 0.0)
    ubuf[POOL_HALO:POOL_HALO + tm] = u
    ubuf[POOL_HALO + tm:2 * POOL_HALO + tm] = jnp.where(i < pl.num_programs(1) - 1, un_ref[0], 0.0)
    pos = i * tm + lax.broadcasted_iota(jnp.int32, (tm, 1), 0)
    gd = wg_ref.shape[1]
    outs = []
    for g, win in enumerate(POOL_WINDOWS):
        half = win // 2
        cols = slice(g * gd, (g + 1) * gd)
        s = ubuf[POOL_HALO - half:POOL_HALO - half + tm, cols]
        for j in range(1 - half, half):
            s = s + ubuf[POOL_HALO + j:POOL_HALO + j + tm, cols]
        cnt = (jnp.minimum(pos + half, seq_len) - jnp.maximum(pos - half, 0)).astype(F32)
        dlt = (s / cnt - u[:, cols]).astype(BF16)
        outs.append(jnp.dot(dlt, wg_ref[g], preferred_element_type=F32))
    z = (jnp.concatenate(outs, axis=-1) * cs_ref[...]).astype(BF16)
    y = jnp.dot(z, wo_ref[...], preferred_element_type=F32)
    _tail(y, *tail_refs)


def _mixer_tail(front_args, front_specs, kern, x, g1, sh2, s2, n2g, rwh, rwl, rb, *, tm, scratch=()):
    b, l, d = x.shape
    nt = l // tm
    n = b * l
    row = lambda bi, i: (bi, 0, 0)
    fix = lambda bi, i: (0, 0)
    tile = lambda bi, i: (bi, i, 0)
    in_specs = list(front_specs) + [
        pl.BlockSpec((1, tm, d), tile),
        pl.BlockSpec((1, 1, d), row), pl.BlockSpec((1, 1, d), row), pl.BlockSpec((1, 1, d), row),
        pl.BlockSpec((1, d), fix),
        pl.BlockSpec((N_EXPERTS, d), fix), pl.BlockSpec((N_EXPERTS, d), fix),
        pl.BlockSpec((N_EXPERTS, 1), fix)]
    out_specs = [pl.BlockSpec((1, tm, d), tile), pl.BlockSpec((1, tm, d // 2), tile),
                 pl.BlockSpec((SUBLANES, tm), lambda bi, i: (0, bi * nt + i)),
                 pl.BlockSpec((tm, LANES), lambda bi, i: (bi * nt + i, 0)),
                 pl.BlockSpec((N_EXPERTS, LANES), fix)]
    out_shape = [jax.ShapeDtypeStruct((b, l, d), F32), jax.ShapeDtypeStruct((b, l, d // 2), jnp.int32),
                 jax.ShapeDtypeStruct((SUBLANES, n), jnp.int32), jax.ShapeDtypeStruct((n, LANES), F32),
                 jax.ShapeDtypeStruct((N_EXPERTS, LANES), F32)]
    return pl.pallas_call(
        kern,
        grid=(b, nt),
        in_specs=in_specs, out_specs=out_specs, out_shape=out_shape,
        scratch_shapes=[pltpu.VMEM((N_EXPERTS, LANES), F32)] + list(scratch),
        compiler_params=_params("arbitrary", "arbitrary"),
        name="mixer_tail",
    )(*front_args, x, g1, sh2, s2, n2g, rwh, rwl, rb)


def _slot_kernel(ps_ref, ridx_ref, dest_ref):
    ridx = ridx_ref[...]
    ps = ps_ref[...]
    erow = lax.broadcasted_iota(jnp.int32, (N_EXPERTS, ridx.shape[1]), 0)
    rows = []
    for k in range(TOP_K):
        start = jnp.sum(jnp.where(erow == ridx[k:k + 1], ps, 0), axis=0, keepdims=True)
        rows.append(start + ridx[TOP_K + k:TOP_K + k + 1])
    rid = lax.broadcasted_iota(jnp.int32, ridx.shape, 0)
    dest_ref[...] = jnp.where(rid == 0, rows[0], jnp.where(rid == 1, rows[1], 0))


def _slot_index(pad_start, ridx, *, tn):
    n = ridx.shape[1]
    return pl.pallas_call(
        _slot_kernel,
        grid=(n // tn,),
        in_specs=[pl.BlockSpec((N_EXPERTS, 1), lambda i: (0, 0)),
                  pl.BlockSpec((SUBLANES, tn), lambda i: (0, i))],
        out_specs=pl.BlockSpec((SUBLANES, tn), lambda i: (0, i)),
        out_shape=jax.ShapeDtypeStruct((SUBLANES, n), jnp.int32),
        compiler_params=_params("arbitrary"),
        name="slot_index",
    )(pad_start.reshape(N_EXPERTS, 1), ridx)


def _sc_mesh():
    return plsc.VectorSubcoreMesh(core_axis_name="c", subcore_axis_name="s",
                                  num_cores=SC_CORES, num_subcores=SC_SUBCORES)


def _sc_worker_base(per_worker):
    return (lax.axis_index("s") * SC_CORES + lax.axis_index("c")) * per_worker


def _sc_scatter_rows(rows, idx0, idx1, n_slots):
    n, d = rows.shape
    per_worker = n // SC_WORKERS
    assert per_worker % SC_WINDOW == 0

    def body(rows_hbm, i0_hbm, i1_hbm, out_hbm, i0_v, i1_v, rows_v):
        base = _sc_worker_base(per_worker)

        @pl.loop(0, per_worker // SC_WINDOW)
        def _(j):
            off = pl.multiple_of(base + j * SC_WINDOW, SC_WINDOW)
            pltpu.sync_copy(i0_hbm.at[pl.ds(off, SC_WINDOW)], i0_v)
            pltpu.sync_copy(i1_hbm.at[pl.ds(off, SC_WINDOW)], i1_v)
            pltpu.sync_copy(rows_hbm.at[pl.ds(off, SC_WINDOW)], rows_v)
            pltpu.sync_copy(rows_v, out_hbm.at[i0_v])
            pltpu.sync_copy(rows_v, out_hbm.at[i1_v])

    return pl.kernel(
        body, out_type=jax.ShapeDtypeStruct((n_slots, d), rows.dtype), mesh=_sc_mesh(),
        scratch_types=[pltpu.VMEM((SC_WINDOW,), jnp.int32), pltpu.VMEM((SC_WINDOW,), jnp.int32),
                       pltpu.VMEM((SC_WINDOW, d), rows.dtype)],
        name="sc_scatter_rows",
    )(rows, idx0, idx1)


def _sc_gather_rows(table, idx):
    n = idx.shape[0]
    d = table.shape[1]
    per_worker = n // SC_WORKERS
    assert per_worker % SC_WINDOW == 0

    def body(table_hbm, idx_hbm, out_hbm, idx_v, rows_v, sem):
        base = _sc_worker_base(per_worker)

        @pl.loop(0, per_worker // SC_WINDOW)
        def _(j):
            off = pl.multiple_of(base + j * SC_WINDOW, SC_WINDOW)
            pltpu.sync_copy(idx_hbm.at[pl.ds(off, SC_WINDOW)], idx_v)
            pltpu.async_copy(table_hbm.at[idx_v], rows_v, sem).wait()
            pltpu.sync_copy(rows_v, out_hbm.at[pl.ds(off, SC_WINDOW)])

    return pl.kernel(
        body, out_type=jax.ShapeDtypeStruct((n, d), table.dtype), mesh=_sc_mesh(),
        scratch_types=[pltpu.VMEM((SC_WINDOW,), jnp.int32), pltpu.VMEM((SC_WINDOW, d), table.dtype),
                       pltpu.SemaphoreType.DMA],
        name="sc_gather_rows",
    )(table, idx)


def _ffn_kernel(be_ref, nv_ref, xs_ref, wg_ref, wu_ref, wd_ref, ys_ref, wg_b, wu_b, wd_b):
    j = pl.program_id(0)
    valid = nv_ref[j]

    @pl.when((valid > 0) & ((j == 0) | (be_ref[j] != be_ref[jnp.maximum(j - 1, 0)])))
    def _():
        wg_b[...] = wg_ref[0, 0].astype(BF16)
        wu_b[...] = wu_ref[0, 0].astype(BF16)
        wd_b[...] = wd_ref[0, 0].astype(BF16)

    @pl.when(valid > 0)
    def _():
        row = lax.broadcasted_iota(jnp.int32, (SLOT_ROWS, 1), 0)
        xw = jnp.where(row < valid, xs_ref[...], 0)
        xb = _unpack_bf16_pairs(xw).astype(BF16)
        g = jnp.dot(xb, wg_b[...], preferred_element_type=F32)
        u = jnp.dot(xb, wu_b[...], preferred_element_type=F32)
        a = (g / (1.0 + jnp.exp(-g)) * u).astype(BF16)
        ys_ref[...] = _pack_bf16_pairs(jnp.dot(a, wd_b[...], preferred_element_type=F32))

    @pl.when(valid <= 0)
    def _():
        ys_ref[...] = jnp.zeros_like(ys_ref)


def _expert_ffn(block_e, n_valid, xs, w_gate, w_up, w_down, layer):
    n_slots, dw = xs.shape
    d, de = w_gate.shape[2:]
    w_idx = lambda j, be, nv: (layer, be[j], 0, 0)
    return pl.pallas_call(
        _ffn_kernel,
        grid_spec=pltpu.PrefetchScalarGridSpec(
            num_scalar_prefetch=2,
            grid=(n_slots // SLOT_ROWS,),
            in_specs=[pl.BlockSpec((SLOT_ROWS, dw), lambda j, be, nv: (j, 0)),
                      pl.BlockSpec((1, 1, d, de), w_idx),
                      pl.BlockSpec((1, 1, d, de), w_idx),
                      pl.BlockSpec((1, 1, de, d), w_idx)],
            out_specs=pl.BlockSpec((SLOT_ROWS, dw), lambda j, be, nv: (j, 0)),
            scratch_shapes=[pltpu.VMEM((d, de), BF16), pltpu.VMEM((d, de), BF16), pltpu.VMEM((de, d), BF16)]),
        out_shape=jax.ShapeDtypeStruct((n_slots, dw), jnp.int32),
        compiler_params=_params("arbitrary"),
        name="expert_ffn",
    )(block_e, n_valid, xs, w_gate, w_up, w_down)


def _combine_kernel(y0_ref, y1_ref, gcol_ref, x_ref, g2_ref, *rest, pool_in):
    if pool_in:
        sh_ref, sc_ref, n1_ref, wi_ref, x_out, u_out = rest
    else:
        (x_out,) = rest
    gc = gcol_ref[...]
    out = gc[:, 0:1] * _unpack_bf16_pairs(y0_ref[0]) + gc[:, 1:2] * _unpack_bf16_pairs(y1_ref[0])
    x2 = x_ref[0] + g2_ref[0] * out
    x_out[0] = x2
    if pool_in:
        hb = _rms_mod(x2, n1_ref[...], sh_ref[0], sc_ref[0]).astype(BF16)
        u_out[0] = jnp.dot(hb, wi_ref[...], preferred_element_type=F32)


def _combine(yg, gcol, x, g2, pool_args=None, *, tc):
    b, l, d = x.shape
    nt = l // tc
    pool_in = pool_args is not None
    kern = functools.partial(_combine_kernel, pool_in=pool_in)
    row = lambda bi, i: (bi, 0, 0)
    fix = lambda bi, i: (0, 0)
    tile = lambda bi, i: (bi, i, 0)
    in_specs = [pl.BlockSpec((1, tc, d // 2), lambda bi, i: (0, bi * nt + i, 0)),
                pl.BlockSpec((1, tc, d // 2), lambda bi, i: (1, bi * nt + i, 0)),
                pl.BlockSpec((tc, LANES), lambda bi, i: (bi * nt + i, 0)),
                pl.BlockSpec((1, tc, d), tile),
                pl.BlockSpec((1, 1, d), row)]
    out_specs = [pl.BlockSpec((1, tc, d), tile)]
    out_shape = [jax.ShapeDtypeStruct((b, l, d), F32)]
    args = [yg, yg, gcol, x, g2]
    if pool_in:
        in_specs += [pl.BlockSpec((1, 1, d), row), pl.BlockSpec((1, 1, d), row),
                     pl.BlockSpec((1, d), fix), pl.BlockSpec((d, d), fix)]
        out_specs.append(pl.BlockSpec((1, tc, d), tile))
        out_shape.append(jax.ShapeDtypeStruct((b, l, d), F32))
        args += list(pool_args)
    return pl.pallas_call(
        kern,
        grid=(b, nt),
        in_specs=in_specs, out_specs=out_specs, out_shape=out_shape,
        compiler_params=_params("arbitrary", "arbitrary"),
        name="moe_combine",
    )(*args)


def _moe(h2, ridx, gcol, counts, x1, g2, w_gate, w_up, w_down, layer, pool_args=None):
    b, l, d = x1.shape
    n = b * l
    n_blocks = (n * TOP_K) // SLOT_ROWS + N_EXPERTS
    cnt = counts[:, 0].astype(jnp.int32)
    padded = (cnt + SLOT_ROWS - 1) // SLOT_ROWS * SLOT_ROWS
    earlier = jnp.arange(N_EXPERTS)[None, :] < jnp.arange(N_EXPERTS)[:, None]
    pad_start = jnp.sum(jnp.where(earlier, padded[None, :], 0), axis=1).astype(jnp.int32)
    pad_end = pad_start + padded
    block_start = jnp.arange(n_blocks, dtype=jnp.int32) * SLOT_ROWS
    block_e = jnp.minimum(jnp.sum(pad_end[None, :] <= block_start[:, None], axis=1), N_EXPERTS - 1).astype(jnp.int32)
    n_valid = jnp.clip(pad_start[block_e] + cnt[block_e] - block_start, 0, SLOT_ROWS).astype(jnp.int32)
    dest = _slot_index(pad_start, ridx, tn=2048)
    xs = _sc_scatter_rows(h2.reshape(n, d // 2), dest[0], dest[1], n_blocks * SLOT_ROWS)
    ys = _expert_ffn(block_e, n_valid, xs, w_gate, w_up, w_down, layer)
    yg = _sc_gather_rows(ys, dest[:TOP_K].reshape(TOP_K * n)).reshape(TOP_K, n, d // 2)
    return _combine(yg, gcol, x1, g2, pool_args, tc=512)


def kernel(x, c, ctx, c_ctx, ada_w, ada_b, norm1_g, norm2_g, attn_w_in, attn_w_out, attn_q_gain, attn_k_gain,
           attn_lq1, attn_lk1, attn_lq2, attn_lk2, attn_sub_gain, pool_w_in, pool_w_group, pool_scale, pool_w_out,
           router_w, router_b, moe_w_gate, moe_w_up, moe_w_down):
    b, l, d = x.shape
    n_ctx = ctx.shape[1]
    depth = ada_w.shape[0]
    assert depth == 2 and d == N_HEADS * V_DIM
    tm = 512

    mod = _adaln_mod(c, c_ctx, ada_w, ada_b)
    mods = [[mod[i, :b, None, j * d:(j + 1) * d] for j in range(N_MOD)] for i in range(depth)]
    mod_ctx = [jnp.broadcast_to(mod[0, b, j * d:(j + 1) * d], (b, 1, d)) for j in range(2)]

    rwt = router_w.T
    rwh = rwt.astype(BF16)
    rwl = (rwt - rwh.astype(F32)).astype(BF16)
    rb = router_b.reshape(N_EXPERTS, 1)

    sh1, s1, g1, sh2, s2, g2 = mods[0]
    cos, sin = _rope_tables(l)
    gain2 = lambda g: jnp.concatenate([g, g])[None]
    qg, kg = gain2(attn_q_gain[0]), gain2(attn_k_gain[0])
    w_in = attn_w_in[0].astype(BF16)
    assert l % tm == 0 and l % n_ctx == 0
    q, k_all, vt_all = _qkv_proj(x, sh1, s1, norm1_g[0][None], w_in, qg, kg, cos, sin, n_qk=2, rope=True,
                                 tm=tm, n_keys=l + n_ctx, key_row0=0)
    k_all, vt_all = _qkv_proj(ctx, mod_ctx[0], mod_ctx[1], norm1_g[0][None], w_in[:, d:], qg, kg,
                              cos[:n_ctx], sin[:n_ctx], n_qk=1, rope=False,
                              tm=n_ctx, n_keys=l + n_ctx, key_row0=l, kv=(k_all, vt_all))
    lam_init = 0.8 - 0.6 * math.exp(-0.3 * 0)
    lam_params = jnp.stack([attn_lq1[0], attn_lk1[0], attn_lq2[0], attn_lk2[0]])
    o = _diff_attention(q, k_all, vt_all, lam_params, attn_sub_gain[0][:, None], lam_init=lam_init)

    fix = lambda bi, i: (0, 0)
    x1, h2, ridx, gcol, counts = _mixer_tail(
        (o, attn_w_out[0].astype(BF16)),
        (pl.BlockSpec((1, tm, d), lambda bi, i: (bi, i, 0)), pl.BlockSpec((d, d), fix)),
        _attn_tail_kernel, x, g1, sh2, s2, norm2_g[0][None], rwh, rwl, rb, tm=tm)

    sh1b, s1b, g1b, sh2b, s2b, g2b = mods[1]
    x2, u = _moe(h2, ridx, gcol, counts, x1, g2, moe_w_gate, moe_w_up, moe_w_down, 0,
                 pool_args=(sh1b, s1b, norm1_g[1][None], pool_w_in[0].astype(BF16)))
    gd = pool_w_group.shape[2]
    nh = tm // POOL_HALO
    front_specs = (
        pl.BlockSpec((1, tm, d), lambda bi, i: (bi, i, 0)),
        pl.BlockSpec((1, POOL_HALO, d), lambda bi, i: (bi, jnp.maximum(i * nh - 1, 0), 0)),
        pl.BlockSpec((1, POOL_HALO, d), lambda bi, i: (bi, jnp.minimum((i + 1) * nh, l // POOL_HALO - 1), 0)),
        pl.BlockSpec((len(POOL_WINDOWS), gd, gd), lambda bi, i: (0, 0, 0)),
        pl.BlockSpec((1, d), fix),
        pl.BlockSpec((d, d), fix))
    x3, h2b, ridx_b, gcol_b, counts_b = _mixer_tail(
        (u, u, u, pool_w_group[0].astype(BF16), pool_scale[0][None], pool_w_out[0].astype(BF16)),
        front_specs, functools.partial(_pool_tail_kernel, seq_len=l),
        x2, g1b, sh2b, s2b, norm2_g[1][None], rwh, rwl, rb, tm=tm,
        scratch=[pltpu.VMEM((tm + 2 * POOL_HALO, d), F32)])
    (out,) = _moe(h2b, ridx_b, gcol_b, counts_b, x3, g2b, moe_w_gate, moe_w_up, moe_w_down, 1)
    return out
```

```python
import functools
import math

import jax
import jax.numpy as jnp
import numpy as np
from jax import lax
from jax.experimental import pallas as pl
from jax.experimental.pallas import tpu as pltpu
from jax.experimental.pallas import tpu_sc as plsc

F32 = jnp.float32
BF16 = jnp.bfloat16
F8 = jnp.float8_e4m3fn

LANES = 128
SUBLANES = 8
N_HEADS = 8
HEAD_DIM = 64
V_DIM = 2 * HEAD_DIM
V_ROWS = V_DIM + 32
P_SHIFT = 8.0
GRID_W = 64
ROPE_THETA = 10000.0
NORM_EPS = 1e-6
N_MOD = 6
POOL_WINDOWS = (2, 4, 8, 16)
POOL_HALO = max(POOL_WINDOWS) // 2
N_EXPERTS = 32
N_EXPERT_GROUPS = 4
EXPERTS_PER_GROUP = N_EXPERTS // N_EXPERT_GROUPS
TOP_K = 2
SLOT_ROWS = 512
SC_CORES = 2
SC_SUBCORES = 16
SC_WORKERS = SC_CORES * SC_SUBCORES
SC_WINDOW = 64
ATTN_GROUP_CHUNKS = 1
ATTN_SCORE_AHEAD = 1
VMEM_LIMIT = 48 * 1024 * 1024
NT_DIMS = (((1,), (1,)), ((), ()))


def _params(*sem):
    return pltpu.CompilerParams(dimension_semantics=sem, vmem_limit_bytes=VMEM_LIMIT)


def _rms_mod(x, gain, shift, scale):
    h = x * lax.rsqrt(jnp.mean(x * x, axis=-1, keepdims=True) + NORM_EPS) * gain
    return h * (1.0 + scale) + shift


def _pack_bf16_pairs(x):
    c = x.shape[1] // 2
    hi = lax.bitcast_convert_type(x[:, :c].astype(BF16).astype(F32), jnp.uint32)
    lo = lax.bitcast_convert_type(x[:, c:].astype(BF16).astype(F32), jnp.uint32)
    return lax.bitcast_convert_type(hi | (lo >> 16), jnp.int32)


def _unpack_bf16_pairs(w):
    u = lax.bitcast_convert_type(w, jnp.uint32)
    hi = lax.bitcast_convert_type(u & jnp.uint32(0xFFFF0000), F32)
    lo = lax.bitcast_convert_type(u << 16, F32)
    return jnp.concatenate([hi, lo], axis=1)


def _mod_kernel(c_ref, w_ref, b_ref, o_ref):
    c = c_ref[...]
    a = c / (1.0 + jnp.exp(-c))
    o_ref[0] = jnp.dot(a, w_ref[0], precision=lax.Precision.HIGHEST,
                       preferred_element_type=F32) + b_ref[0]


def _adaln_mod(c, c_ctx, ada_w, ada_b):
    depth, d, n_out = ada_w.shape
    b = c.shape[0]
    assert b + 1 <= SUBLANES
    rows = jnp.concatenate([c, c_ctx[None], jnp.zeros((SUBLANES - b - 1, d), F32)], axis=0)
    tn = n_out // 4
    return pl.pallas_call(
        _mod_kernel,
        grid=(depth, n_out // tn),
        in_specs=[pl.BlockSpec((SUBLANES, d), lambda i, j: (0, 0)),
                  pl.BlockSpec((1, d, tn), lambda i, j: (i, 0, j)),
                  pl.BlockSpec((1, 1, tn), lambda i, j: (i, 0, j))],
        out_specs=pl.BlockSpec((1, SUBLANES, tn), lambda i, j: (i, 0, j)),
        out_shape=jax.ShapeDtypeStruct((depth, SUBLANES, n_out), F32),
        compiler_params=_params("arbitrary", "arbitrary"),
        name="adaln_mod",
    )(rows, ada_w, ada_b.reshape(depth, 1, n_out))


def _qkv_kernel(x_ref, sh_ref, sc_ref, g_ref, w_ref, qg_ref, kg_ref, cos_ref, sin_ref, *refs,
                n_qk, rope):
    out_refs = refs[-(n_qk + 1):]
    tm, d = x_ref.shape[1:]
    hb = _rms_mod(x_ref[0], g_ref[...], sh_ref[0], sc_ref[0]).astype(BF16)
    lane = lax.broadcasted_iota(jnp.int32, (1, LANES), 1)
    lane_lo = lane < HEAD_DIM
    lane_b4 = (lane & 16) == 0
    gains = (qg_ref[...], kg_ref[...])[2 - n_qk:]
    for t in range(n_qk):
        for j in range(0, d, 2 * LANES):
            acc = jnp.dot(hb, w_ref[:, t * d + j:t * d + j + 2 * LANES], preferred_element_type=F32)
            for half in range(2):
                blk = acc[:, half * LANES:(half + 1) * LANES]
                sq = blk * blk
                lo = jnp.sum(jnp.where(lane_lo, sq, 0.0), axis=-1, keepdims=True)
                hi = jnp.sum(jnp.where(lane_lo, 0.0, sq), axis=-1, keepdims=True)
                ms = jnp.where(lane_lo, lo, hi) * (1.0 / HEAD_DIM)
                y = blk * lax.rsqrt(ms + NORM_EPS) * gains[t]
                if rope:
                    rot = jnp.where(lane_b4, pltpu.roll(y, LANES - 16, 1), pltpu.roll(y, 16, 1))
                    y = y * cos_ref[...] + rot * sin_ref[...]
                c0 = j + half * LANES
                if t < n_qk - 1:
                    yt = (y * (HEAD_DIM ** -0.5 * math.log2(math.e))).T
                    first = lax.broadcasted_iota(jnp.int32, (V_DIM, 1), 0) < HEAD_DIM
                    out_refs[t][0, c0 // LANES, 0, :, 0:tm] = jnp.where(first, yt, 0.0).astype(F8)
                    out_refs[t][0, c0 // LANES, 0, :, tm:2 * tm] = jnp.where(first, 0.0, yt).astype(F8)
                else:
                    out_refs[t][0, :, c0:c0 + LANES] = y.astype(F8)
    vt_ref = out_refs[n_qk]
    ones = jnp.where(lax.broadcasted_iota(jnp.int32, (V_ROWS - V_DIM, tm), 0) == 0, 1.0, 0.0).astype(F8)
    for j in range(0, d, 2 * LANES):
        acc = jnp.dot(hb, w_ref[:, n_qk * d + j:n_qk * d + j + 2 * LANES], preferred_element_type=F32)
        acc_t = acc.T
        for half in range(2):
            h = j // LANES + half
            vt_ref[0, h, 0:V_DIM, :] = acc_t[half * V_DIM:(half + 1) * V_DIM].astype(F8)
            vt_ref[0, h, V_DIM:V_ROWS, :] = ones


def _qkv_proj(x, shift, scale, gain, w, q_gain, k_gain, cos, sin, *, n_qk, rope, tm, n_keys, key_row0, kv=None):
    b, l, d = x.shape
    n_out = n_qk + 1
    kern = functools.partial(_qkv_kernel, n_qk=n_qk, rope=rope)
    row = lambda bi, i: (bi, 0, 0)
    fix = lambda bi, i: (0, 0)
    kb = key_row0 // tm
    tile = pl.BlockSpec((1, tm, d), lambda bi, i: (bi, i, 0))
    k_spec = pl.BlockSpec((1, tm, d), lambda bi, i: (bi, kb + i, 0))
    vt_spec = pl.BlockSpec((1, N_HEADS, V_ROWS, tm), lambda bi, i: (bi, 0, 0, kb + i))
    q_spec = pl.BlockSpec((1, N_HEADS, 1, V_DIM, 2 * tm), lambda bi, i: (bi, 0, i, 0, 0))
    q_shape = jax.ShapeDtypeStruct((b, N_HEADS, l // tm, V_DIM, 2 * tm), F8)
    k_shape = jax.ShapeDtypeStruct((b, n_keys, d), F8)
    vt_shape = jax.ShapeDtypeStruct((b, N_HEADS, V_ROWS, n_keys), F8)
    in_specs = [tile,
                pl.BlockSpec((1, 1, d), row), pl.BlockSpec((1, 1, d), row),
                pl.BlockSpec((1, d), fix),
                pl.BlockSpec((d, n_out * d), fix),
                pl.BlockSpec((1, LANES), fix), pl.BlockSpec((1, LANES), fix),
                pl.BlockSpec((tm, LANES), lambda bi, i: (i, 0)),
                pl.BlockSpec((tm, LANES), lambda bi, i: (i, 0))]
    args = [x, shift, scale, gain, w, q_gain, k_gain, cos, sin]
    aliases = {}
    if kv is not None:
        aliases = {len(args): n_qk - 1, len(args) + 1: n_qk}
        in_specs += [pl.BlockSpec(memory_space=pl.ANY)] * 2
        args += list(kv)
    return pl.pallas_call(
        kern,
        grid=(b, l // tm),
        in_specs=in_specs,
        out_specs=[q_spec] * (n_qk - 1) + [k_spec, vt_spec],
        out_shape=[q_shape] * (n_qk - 1) + [k_shape, vt_shape],
        input_output_aliases=aliases,
        compiler_params=_params("arbitrary", "arbitrary"),
        name="qkv_proj",
    )(*args)


def _rope_tables(n_tokens):
    rows = n_tokens // GRID_W
    row = np.repeat(np.arange(rows, dtype=np.float32), GRID_W)
    col = np.tile(np.arange(GRID_W, dtype=np.float32), rows)
    half = HEAD_DIM // 2
    inv_freq = (np.float32(ROPE_THETA) ** (-np.arange(0, half, 2, dtype=np.float32) / half)).astype(np.float32)
    ang_r = row[:, None] * inv_freq
    ang_c = col[:, None] * inv_freq
    ang = np.concatenate([ang_r, ang_r, ang_c, ang_c] * 2, axis=-1)
    sign = np.where((np.arange(LANES) & 16) == 0, -1.0, 1.0).astype(np.float32)
    return jnp.asarray(np.cos(ang), F32), jnp.asarray(np.sin(ang) * sign, F32)


def _attn_kernel(q_ref, k_ref, vt_ref, lp_ref, sg_ref, o_ref, s_ref, *, tk, group, ahead, lam_init):
    tq = q_ref.shape[4] // 2
    n_chunks = k_ref.shape[1] // tk
    qz = q_ref[0, 0, 0]
    n_slots = (ahead + 1) * group

    def score_chunk(c, m_grp):
        st = jnp.dot(k_ref[0, c * tk:(c + 1) * tk, :], qz, preferred_element_type=F32).astype(BF16)
        slot = c % n_slots
        s_ref[slot * tk:(slot + 1) * tk, :] = st
        mc = jnp.max(st, axis=0, keepdims=True)
        return mc if m_grp is None else jnp.maximum(m_grp, mc)

    def value_chunk(c, m_ref, part):
        slot = c % n_slots
        p = jnp.exp2(s_ref[slot * tk:(slot + 1) * tk, :] - (m_ref - P_SHIFT)).astype(F8)
        pv = jnp.dot(vt_ref[0, 0, :, c * tk:(c + 1) * tk], p, preferred_element_type=F32)
        return pv if part is None else part + pv

    groups = [list(range(g0, min(g0 + group, n_chunks))) for g0 in range(0, n_chunks, group)]
    m_of = {}
    for g in range(min(ahead, len(groups))):
        for c in groups[g]:
            m_of[g] = score_chunk(c, m_of.get(g))
    m = None
    acc = None
    for gi, cur in enumerate(groups):
        nxt = groups[gi + ahead] if gi + ahead < len(groups) else []
        m_new = m_of[gi] if m is None else jnp.maximum(m, m_of[gi])
        part = None
        for i in range(max(len(cur), len(nxt))):
            if i < len(nxt):
                m_of[gi + ahead] = score_chunk(nxt[i], m_of.get(gi + ahead))
            if i < len(cur):
                part = value_chunk(cur[i], m_new, part)
        acc = part if acc is None else acc * jnp.exp2(m.astype(F32) - m_new.astype(F32)) + part
        m = m_new
    acc = acc[:V_DIM] / acc[V_DIM:V_DIM + 1]
    lp = lp_ref[...]
    lam = (jnp.exp(jnp.sum(lp[0:1] * lp[1:2], axis=-1, keepdims=True))
           - jnp.exp(jnp.sum(lp[2:3] * lp[3:4], axis=-1, keepdims=True)) + lam_init)
    o = acc[:, :tq] - lam * acc[:, tq:]
    o = o * lax.rsqrt(jnp.mean(o * o, axis=0, keepdims=True) + NORM_EPS) * sg_ref[...] * (1.0 - lam_init)
    o_ref[0] = o.T.astype(BF16)


def _attn_chunk(n_keys):
    for tk in (768, 512, 256, 128):
        if n_keys % tk == 0:
            return tk
    raise ValueError(f"key count {n_keys} is not a multiple of {LANES}")


def _diff_attention(qz, k_all, vt_all, lam_params, sub_gain, *, lam_init):
    b, _, n_tiles, _, tq2 = qz.shape
    tq = tq2 // 2
    l, d = n_tiles * tq, N_HEADS * V_DIM
    n_keys = k_all.shape[1]
    tk = _attn_chunk(n_keys)
    kern = functools.partial(_attn_kernel, tk=tk, group=ATTN_GROUP_CHUNKS, ahead=ATTN_SCORE_AHEAD, lam_init=lam_init)
    return pl.pallas_call(
        kern,
        grid=(b, N_HEADS, l // tq),
        in_specs=[pl.BlockSpec((1, 1, 1, V_DIM, 2 * tq), lambda bi, h, i: (bi, h, i, 0, 0)),
                  pl.BlockSpec((1, n_keys, V_DIM), lambda bi, h, i: (bi, 0, h)),
                  pl.BlockSpec((1, 1, V_ROWS, n_keys), lambda bi, h, i: (bi, h, 0, 0)),
                  pl.BlockSpec((4, HEAD_DIM), lambda bi, h, i: (0, 0)),
                  pl.BlockSpec((V_DIM, 1), lambda bi, h, i: (0, 0))],
        out_specs=pl.BlockSpec((1, tq, V_DIM), lambda bi, h, i: (bi, i, h)),
        out_shape=jax.ShapeDtypeStruct((b, l, d), BF16),
        scratch_shapes=[pltpu.VMEM(((ATTN_SCORE_AHEAD + 1) * ATTN_GROUP_CHUNKS * tk, 2 * tq), BF16)],
        compiler_params=_params("arbitrary", "arbitrary", "arbitrary"),
        name="diff_attention",
    )(qz, k_all, vt_all, lam_params, sub_gain)


def _route(h2, rwh_ref, rwl_ref, rb_ref, carry_ref, ridx_ref, gcol_ref, cnt_ref, is_first):
    tm = h2.shape[0]
    hh = h2.astype(BF16)
    hl = (h2 - hh.astype(F32)).astype(BF16)
    rw2 = jnp.concatenate([rwh_ref[...], rwl_ref[...]], axis=0)
    part = lax.dot_general(rw2, hh, NT_DIMS, preferred_element_type=F32)
    logits = (part[:N_EXPERTS] + part[N_EXPERTS:]
              + lax.dot_general(rwh_ref[...], hl, NT_DIMS, preferred_element_type=F32) + rb_ref[...])
    groups = [logits[g * EXPERTS_PER_GROUP:(g + 1) * EXPERTS_PER_GROUP] for g in range(N_EXPERT_GROUPS)]
    top = groups[0]
    for g in range(1, N_EXPERT_GROUPS):
        top = jnp.maximum(top, groups[g])
    top = jnp.max(top, axis=0, keepdims=True)
    sub = lax.broadcasted_iota(jnp.int32, (EXPERTS_PER_GROUP, tm), 0)
    best = None
    for g in range(N_EXPERT_GROUPS):
        ex = jnp.exp(groups[g] - top)
        v1 = jnp.max(ex, axis=0, keepdims=True)
        i1 = jnp.min(jnp.where(ex == v1, sub, EXPERTS_PER_GROUP), axis=0, keepdims=True)
        rest = jnp.where(sub == i1, -1.0, ex)
        v2 = jnp.max(rest, axis=0, keepdims=True)
        i2 = jnp.min(jnp.where(rest == v2, sub, EXPERTS_PER_GROUP), axis=0, keepdims=True)
        cand = (v1 + v2, v1, v2, i1 + g * EXPERTS_PER_GROUP, i2 + g * EXPERTS_PER_GROUP)
        if best is None:
            best = cand
        else:
            better = cand[0] > best[0]
            best = tuple(jnp.where(better, new, old) for new, old in zip(cand, best))
    _, v1, v2, e0, e1 = best
    gate0 = v1 / (v1 + v2)
    gate1 = v2 / (v1 + v2)

    @pl.when(is_first)
    def _():
        carry_ref[...] = jnp.zeros_like(carry_ref)

    erow = lax.broadcasted_iota(jnp.int32, (N_EXPERTS, tm), 0)
    oh0 = erow == e0
    oh1 = erow == e1
    chosen = jnp.where(oh0 | oh1, 1.0, 0.0)
    before = (lax.broadcasted_iota(jnp.int32, (tm, tm), 0)
              < lax.broadcasted_iota(jnp.int32, (tm, tm), 1)).astype(BF16)
    prior = carry_ref[:, 0:1] + jnp.dot(chosen.astype(BF16), before, preferred_element_type=F32)
    r0 = jnp.sum(jnp.where(oh0, prior, 0.0), axis=0, keepdims=True).astype(jnp.int32)
    r1 = jnp.sum(jnp.where(oh1, prior, 0.0), axis=0, keepdims=True).astype(jnp.int32)
    carry_ref[...] = carry_ref[...] + jnp.sum(chosen, axis=1, keepdims=True)
    cnt_ref[...] = carry_ref[...]
    rid = lax.broadcasted_iota(jnp.int32, (SUBLANES, tm), 0)
    ridx_ref[...] = jnp.where(rid == 0, e0, jnp.where(rid == 1, e1, jnp.where(rid == 2, r0, jnp.where(rid == 3, r1, 0))))
    gid = lax.broadcasted_iota(jnp.int32, (LANES, tm), 0)
    gcol_ref[...] = jnp.where(gid == 0, gate0, jnp.where(gid == 1, gate1, 0.0)).T


def _tail(y, x_ref, g1_ref, sh2_ref, s2_ref, n2_ref, rwh_ref, rwl_ref, rb_ref,
          x_out, h2_out, ridx_ref, gcol_ref, cnt_ref, carry_ref):
    x1 = x_ref[0] + g1_ref[0] * y
    x_out[0] = x1
    h2 = _rms_mod(x1, n2_ref[...], sh2_ref[0], s2_ref[0])
    h2_out[0] = _pack_bf16_pairs(h2)
    is_first = (pl.program_id(0) == 0) & (pl.program_id(1) == 0)
    _route(h2, rwh_ref, rwl_ref, rb_ref, carry_ref, ridx_ref, gcol_ref, cnt_ref, is_first)


def _attn_tail_kernel(a_ref, wo_ref, *rest):
    y = jnp.dot(a_ref[0], wo_ref[...], preferred_element_type=F32)
    _tail(y, *rest)


def _pool_tail_kernel(u_ref, up_ref, un_ref, wg_ref, cs_ref, wo_ref, *rest, seq_len):
    *tail_refs, ubuf = rest
    tm = u_ref.shape[1]
    i = pl.program_id(1)
    u = u_ref[0]
    ubuf[0:POOL_HALO] = jnp.where(i > 0, up_ref[0], 0.0)
    ubuf[POOL_HALO:POOL_HALO + tm] = u
    ubuf[POOL_HALO + tm:2 * POOL_HALO + tm] = jnp.where(i < pl.num_programs(1) - 1, un_ref[0], 0.0)
    pos = i * tm + lax.broadcasted_iota(jnp.int32, (tm, 1), 0)
    gd = wg_ref.shape[1]
    outs = []
    for g, win in enumerate(POOL_WINDOWS):
        half = win // 2
        cols = slice(g * gd, (g + 1) * gd)
        s = ubuf[POOL_HALO - half:POOL_HALO - half + tm, cols]
        for j in range(1 - half, half):
            s = s + ubuf[POOL_HALO + j:POOL_HALO + j + tm, cols]
        cnt = (jnp.minimum(pos + half, seq_len) - jnp.maximum(pos - half, 0)).astype(F32)
        dlt = (s / cnt - u[:, cols]).astype(BF16)
        outs.append(jnp.dot(dlt, wg_ref[g], preferred_element_type=F32))
    z = (jnp.concatenate(outs, axis=-1) * cs_ref[...]).astype(BF16)
    y = jnp.dot(z, wo_ref[...], preferred_element_type=F32)
    _tail(y, *tail_refs)


def _mixer_tail(front_args, front_specs, kern, x, g1, sh2, s2, n2g, rwh, rwl, rb, *, tm, scratch=()):
    b, l, d = x.shape
    nt = l // tm
    n = b * l
    row = lambda bi, i: (bi, 0, 0)
    fix = lambda bi, i: (0, 0)
    tile = lambda bi, i: (bi, i, 0)
    in_specs = list(front_specs) + [
        pl.BlockSpec((1, tm, d), tile),
        pl.BlockSpec((1, 1, d), row), pl.BlockSpec((1, 1, d), row), pl.BlockSpec((1, 1, d), row),
        pl.BlockSpec((1, d), fix),
        pl.BlockSpec((N_EXPERTS, d), fix), pl.BlockSpec((N_EXPERTS, d), fix),
        pl.BlockSpec((N_EXPERTS, 1), fix)]
    out_specs = [pl.BlockSpec((1, tm, d), tile), pl.BlockSpec((1, tm, d // 2), tile),
                 pl.BlockSpec((SUBLANES, tm), lambda bi, i: (0, bi * nt + i)),
                 pl.BlockSpec((tm, LANES), lambda bi, i: (bi * nt + i, 0)),
                 pl.BlockSpec((N_EXPERTS, LANES), fix)]
    out_shape = [jax.ShapeDtypeStruct((b, l, d), F32), jax.ShapeDtypeStruct((b, l, d // 2), jnp.int32),
                 jax.ShapeDtypeStruct((SUBLANES, n), jnp.int32), jax.ShapeDtypeStruct((n, LANES), F32),
                 jax.ShapeDtypeStruct((N_EXPERTS, LANES), F32)]
    return pl.pallas_call(
        kern,
        grid=(b, nt),
        in_specs=in_specs, out_specs=out_specs, out_shape=out_shape,
        scratch_shapes=[pltpu.VMEM((N_EXPERTS, LANES), F32)] + list(scratch),
        compiler_params=_params("arbitrary", "arbitrary"),
        name="mixer_tail",
    )(*front_args, x, g1, sh2, s2, n2g, rwh, rwl, rb)


def _slot_kernel(ps_ref, ridx_ref, dest_ref):
    ridx = ridx_ref[...]
    ps = ps_ref[...]
    erow = lax.broadcasted_iota(jnp.int32, (N_EXPERTS, ridx.shape[1]), 0)
    rows = []
    for k in range(TOP_K):
        start = jnp.sum(jnp.where(erow == ridx[k:k + 1], ps, 0), axis=0, keepdims=True)
        rows.append(start + ridx[TOP_K + k:TOP_K + k + 1])
    rid = lax.broadcasted_iota(jnp.int32, ridx.shape, 0)
    dest_ref[...] = jnp.where(rid == 0, rows[0], jnp.where(rid == 1, rows[1], 0))


def _slot_index(pad_start, ridx, *, tn):
    n = ridx.shape[1]
    return pl.pallas_call(
        _slot_kernel,
        grid=(n // tn,),
        in_specs=[pl.BlockSpec((N_EXPERTS, 1), lambda i: (0, 0)),
                  pl.BlockSpec((SUBLANES, tn), lambda i: (0, i))],
        out_specs=pl.BlockSpec((SUBLANES, tn), lambda i: (0, i)),
        out_shape=jax.ShapeDtypeStruct((SUBLANES, n), jnp.int32),
        compiler_params=_params("arbitrary"),
        name="slot_index",
    )(pad_start.reshape(N_EXPERTS, 1), ridx)


def _sc_mesh():
    return plsc.VectorSubcoreMesh(core_axis_name="c", subcore_axis_name="s",
                                  num_cores=SC_CORES, num_subcores=SC_SUBCORES)


def _sc_worker_base(per_worker):
    return (lax.axis_index("s") * SC_CORES + lax.axis_index("c")) * per_worker


def _sc_scatter_rows(rows, idx0, idx1, n_slots):
    n, d = rows.shape
    per_worker = n // SC_WORKERS
    assert per_worker % SC_WINDOW == 0

    def body(rows_hbm, i0_hbm, i1_hbm, out_hbm, i0_v, i1_v, rows_v):
        base = _sc_worker_base(per_worker)

        @pl.loop(0, per_worker // SC_WINDOW)
        def _(j):
            off = pl.multiple_of(base + j * SC_WINDOW, SC_WINDOW)
            pltpu.sync_copy(i0_hbm.at[pl.ds(off, SC_WINDOW)], i0_v)
            pltpu.sync_copy(i1_hbm.at[pl.ds(off, SC_WINDOW)], i1_v)
            pltpu.sync_copy(rows_hbm.at[pl.ds(off, SC_WINDOW)], rows_v)
            pltpu.sync_copy(rows_v, out_hbm.at[i0_v])
            pltpu.sync_copy(rows_v, out_hbm.at[i1_v])

    return pl.kernel(
        body, out_type=jax.ShapeDtypeStruct((n_slots, d), rows.dtype), mesh=_sc_mesh(),
        scratch_types=[pltpu.VMEM((SC_WINDOW,), jnp.int32), pltpu.VMEM((SC_WINDOW,), jnp.int32),
                       pltpu.VMEM((SC_WINDOW, d), rows.dtype)],
        name="sc_scatter_rows",
    )(rows, idx0, idx1)


def _sc_gather_rows(table, idx):
    n = idx.shape[0]
    d = table.shape[1]
    per_worker = n // SC_WORKERS
    assert per_worker % SC_WINDOW == 0

    n_win = per_worker // SC_WINDOW
    assert n_win % 2 == 0

    def body(table_hbm, idx_hbm, out_hbm, idx_a, idx_b, rows_a, rows_b, sem_a, sem_b):
        base = _sc_worker_base(per_worker)

        def offset(j):
            return pl.multiple_of(base + j * SC_WINDOW, SC_WINDOW)

        def start(j, idx_v, rows_v, sem):
            pltpu.sync_copy(idx_hbm.at[pl.ds(offset(j), SC_WINDOW)], idx_v)
            pltpu.async_copy(table_hbm.at[idx_v], rows_v, sem)

        def finish(j, idx_v, rows_v, sem):
            pltpu.make_async_copy(table_hbm.at[idx_v], rows_v, sem).wait()
            pltpu.sync_copy(rows_v, out_hbm.at[pl.ds(offset(j), SC_WINDOW)])

        start(0, idx_a, rows_a, sem_a)

        @pl.loop(0, n_win, step=2)
        def _(j):
            start(j + 1, idx_b, rows_b, sem_b)
            finish(j, idx_a, rows_a, sem_a)

            @pl.when(j + 2 < n_win)
            def _():
                start(j + 2, idx_a, rows_a, sem_a)

            finish(j + 1, idx_b, rows_b, sem_b)

    return pl.kernel(
        body, out_type=jax.ShapeDtypeStruct((n, d), table.dtype), mesh=_sc_mesh(),
        scratch_types=[pltpu.VMEM((SC_WINDOW,), jnp.int32), pltpu.VMEM((SC_WINDOW,), jnp.int32),
                       pltpu.VMEM((SC_WINDOW, d), table.dtype), pltpu.VMEM((SC_WINDOW, d), table.dtype),
                       pltpu.SemaphoreType.DMA, pltpu.SemaphoreType.DMA],
        name="sc_gather_rows",
    )(table, idx)


def _ffn_kernel(be_ref, nv_ref, xs_ref, wg_ref, wu_ref, wd_ref, ys_ref, wg_b, wu_b, wd_b):
    j = pl.program_id(0)
    valid = nv_ref[j]

    @pl.when((valid > 0) & ((j == 0) | (be_ref[j] != be_ref[jnp.maximum(j - 1, 0)])))
    def _():
        wg_b[...] = wg_ref[0, 0].astype(BF16)
        wu_b[...] = wu_ref[0, 0].astype(BF16)
        wd_b[...] = wd_ref[0, 0].astype(BF16)

    @pl.when(valid > 0)
    def _():
        row = lax.broadcasted_iota(jnp.int32, (SLOT_ROWS, 1), 0)
        xw = jnp.where(row < valid, xs_ref[...], 0)
        xb = _unpack_bf16_pairs(xw).astype(BF16)
        g = jnp.dot(xb, wg_b[...], preferred_element_type=F32)
        u = jnp.dot(xb, wu_b[...], preferred_element_type=F32)
        a = (g / (1.0 + jnp.exp(-g)) * u).astype(BF16)
        ys_ref[...] = _pack_bf16_pairs(jnp.dot(a, wd_b[...], preferred_element_type=F32))

    @pl.when(valid <= 0)
    def _():
        ys_ref[...] = jnp.zeros_like(ys_ref)


def _expert_ffn(block_e, n_valid, xs, w_gate, w_up, w_down, layer):
    n_slots, dw = xs.shape
    d, de = w_gate.shape[2:]
    w_idx = lambda j, be, nv: (layer, be[j], 0, 0)
    return pl.pallas_call(
        _ffn_kernel,
        grid_spec=pltpu.PrefetchScalarGridSpec(
            num_scalar_prefetch=2,
            grid=(n_slots // SLOT_ROWS,),
            in_specs=[pl.BlockSpec((SLOT_ROWS, dw), lambda j, be, nv: (j, 0)),
                      pl.BlockSpec((1, 1, d, de), w_idx),
                      pl.BlockSpec((1, 1, d, de), w_idx),
                      pl.BlockSpec((1, 1, de, d), w_idx)],
            out_specs=pl.BlockSpec((SLOT_ROWS, dw), lambda j, be, nv: (j, 0)),
            scratch_shapes=[pltpu.VMEM((d, de), BF16), pltpu.VMEM((d, de), BF16), pltpu.VMEM((de, d), BF16)]),
        out_shape=jax.ShapeDtypeStruct((n_slots, dw), jnp.int32),
        compiler_params=_params("arbitrary"),
        name="expert_ffn",
    )(block_e, n_valid, xs, w_gate, w_up, w_down)


def _combine_kernel(y0_ref, y1_ref, gcol_ref, x_ref, g2_ref, *rest, pool_in):
    if pool_in:
        sh_ref, sc_ref, n1_ref, wi_ref, x_out, u_out = rest
    else:
        (x_out,) = rest
    gc = gcol_ref[...]
    out = gc[:, 0:1] * _unpack_bf16_pairs(y0_ref[0]) + gc[:, 1:2] * _unpack_bf16_pairs(y1_ref[0])
    x2 = x_ref[0] + g2_ref[0] * out
    x_out[0] = x2
    if pool_in:
        hb = _rms_mod(x2, n1_ref[...], sh_ref[0], sc_ref[0]).astype(BF16)
        u_out[0] = jnp.dot(hb, wi_ref[...], preferred_element_type=F32)


def _combine(yg, gcol, x, g2, pool_args=None, *, tc):
    b, l, d = x.shape
    nt = l // tc
    pool_in = pool_args is not None
    kern = functools.partial(_combine_kernel, pool_in=pool_in)
    row = lambda bi, i: (bi, 0, 0)
    fix = lambda bi, i: (0, 0)
    tile = lambda bi, i: (bi, i, 0)
    in_specs = [pl.BlockSpec((1, tc, d // 2), lambda bi, i: (0, bi * nt + i, 0)),
                pl.BlockSpec((1, tc, d // 2), lambda bi, i: (1, bi * nt + i, 0)),
                pl.BlockSpec((tc, LANES), lambda bi, i: (bi * nt + i, 0)),
                pl.BlockSpec((1, tc, d), tile),
                pl.BlockSpec((1, 1, d), row)]
    out_specs = [pl.BlockSpec((1, tc, d), tile)]
    out_shape = [jax.ShapeDtypeStruct((b, l, d), F32)]
    args = [yg, yg, gcol, x, g2]
    if pool_in:
        in_specs += [pl.BlockSpec((1, 1, d), row), pl.BlockSpec((1, 1, d), row),
                     pl.BlockSpec((1, d), fix), pl.BlockSpec((d, d), fix)]
        out_specs.append(pl.BlockSpec((1, tc, d), tile))
        out_shape.append(jax.ShapeDtypeStruct((b, l, d), F32))
        args += list(pool_args)
    return pl.pallas_call(
        kern,
        grid=(b, nt),
        in_specs=in_specs, out_specs=out_specs, out_shape=out_shape,
        compiler_params=_params("arbitrary", "arbitrary"),
        name="moe_combine",
    )(*args)


def _moe(h2, ridx, gcol, counts, x1, g2, w_gate, w_up, w_down, layer, pool_args=None):
    b, l, d = x1.shape
    n = b * l
    n_blocks = (n * TOP_K) // SLOT_ROWS + N_EXPERTS
    cnt = counts[:, 0].astype(jnp.int32)
    padded = (cnt + SLOT_ROWS - 1) // SLOT_ROWS * SLOT_ROWS
    earlier = jnp.arange(N_EXPERTS)[None, :] < jnp.arange(N_EXPERTS)[:, None]
    pad_start = jnp.sum(jnp.where(earlier, padded[None, :], 0), axis=1).astype(jnp.int32)
    pad_end = pad_start + padded
    block_start = jnp.arange(n_blocks, dtype=jnp.int32) * SLOT_ROWS
    block_e = jnp.minimum(jnp.sum(pad_end[None, :] <= block_start[:, None], axis=1), N_EXPERTS - 1).astype(jnp.int32)
    own = block_e[:, None] == jnp.arange(N_EXPERTS)[None, :]
    data_end = jnp.sum(jnp.where(own, (pad_start + cnt)[None, :], 0), axis=1)
    n_valid = jnp.clip(data_end - block_start, 0, SLOT_ROWS).astype(jnp.int32)
    dest = _slot_index(pad_start, ridx, tn=2048)
    xs = _sc_scatter_rows(h2.reshape(n, d // 2), dest[0], dest[1], n_blocks * SLOT_ROWS)
    ys = _expert_ffn(block_e, n_valid, xs, w_gate, w_up, w_down, layer)
    yg = _sc_gather_rows(ys, dest[:TOP_K].reshape(TOP_K * n)).reshape(TOP_K, n, d // 2)
    return _combine(yg, gcol, x1, g2, pool_args, tc=512)


def kernel(x, c, ctx, c_ctx, ada_w, ada_b, norm1_g, norm2_g, attn_w_in, attn_w_out, attn_q_gain, attn_k_gain,
           attn_lq1, attn_lk1, attn_lq2, attn_lk2, attn_sub_gain, pool_w_in, pool_w_group, pool_scale, pool_w_out,
           router_w, router_b, moe_w_gate, moe_w_up, moe_w_down):
    b, l, d = x.shape
    n_ctx = ctx.shape[1]
    depth = ada_w.shape[0]
    assert depth == 2 and d == N_HEADS * V_DIM
    tm = 512

    mod = _adaln_mod(c, c_ctx, ada_w, ada_b)
    mods = [[mod[i, :b, None, j * d:(j + 1) * d] for j in range(N_MOD)] for i in range(depth)]
    mod_ctx = [jnp.broadcast_to(mod[0, b, j * d:(j + 1) * d], (b, 1, d)) for j in range(2)]

    rwt = router_w.T
    rwh = rwt.astype(BF16)
    rwl = (rwt - rwh.astype(F32)).astype(BF16)
    rb = router_b.reshape(N_EXPERTS, 1)

    sh1, s1, g1, sh2, s2, g2 = mods[0]
    cos, sin = _rope_tables(l)
    gain2 = lambda g: jnp.concatenate([g, g])[None]
    qg, kg = gain2(attn_q_gain[0]), gain2(attn_k_gain[0])
    w_in = attn_w_in[0].astype(BF16)
    assert l % tm == 0 and l % n_ctx == 0
    q, k_all, vt_all = _qkv_proj(x, sh1, s1, norm1_g[0][None], w_in, qg, kg, cos, sin, n_qk=2, rope=True,
                                 tm=tm, n_keys=l + n_ctx, key_row0=0)
    k_all, vt_all = _qkv_proj(ctx, mod_ctx[0], mod_ctx[1], norm1_g[0][None], w_in[:, d:], qg, kg,
                              cos[:n_ctx], sin[:n_ctx], n_qk=1, rope=False,
                              tm=n_ctx, n_keys=l + n_ctx, key_row0=l, kv=(k_all, vt_all))
    lam_init = 0.8 - 0.6 * math.exp(-0.3 * 0)
    lam_params = jnp.stack([attn_lq1[0], attn_lk1[0], attn_lq2[0], attn_lk2[0]])
    o = _diff_attention(q, k_all, vt_all, lam_params, attn_sub_gain[0][:, None], lam_init=lam_init)

    fix = lambda bi, i: (0, 0)
    x1, h2, ridx, gcol, counts = _mixer_tail(
        (o, attn_w_out[0].astype(BF16)),
        (pl.BlockSpec((1, tm, d), lambda bi, i: (bi, i, 0)), pl.BlockSpec((d, d), fix)),
        _attn_tail_kernel, x, g1, sh2, s2, norm2_g[0][None], rwh, rwl, rb, tm=tm)

    sh1b, s1b, g1b, sh2b, s2b, g2b = mods[1]
    x2, u = _moe(h2, ridx, gcol, counts, x1, g2, moe_w_gate, moe_w_up, moe_w_down, 0,
                 pool_args=(sh1b, s1b, norm1_g[1][None], pool_w_in[0].astype(BF16)))
    gd = pool_w_group.shape[2]
    nh = tm // POOL_HALO
    front_specs = (
        pl.BlockSpec((1, tm, d), lambda bi, i: (bi, i, 0)),
        pl.BlockSpec((1, POOL_HALO, d), lambda bi, i: (bi, jnp.maximum(i * nh - 1, 0), 0)),
        pl.BlockSpec((1, POOL_HALO, d), lambda bi, i: (bi, jnp.minimum((i + 1) * nh, l // POOL_HALO - 1), 0)),
        pl.BlockSpec((len(POOL_WINDOWS), gd, gd), lambda bi, i: (0, 0, 0)),
        pl.BlockSpec((1, d), fix),
        pl.BlockSpec((d, d), fix))
    x3, h2b, ridx_b, gcol_b, counts_b = _mixer_tail(
        (u, u, u, pool_w_group[0].astype(BF16), pool_scale[0][None], pool_w_out[0].astype(BF16)),
        front_specs, functools.partial(_pool_tail_kernel, seq_len=l),
        x2, g1b, sh2b, s2b, norm2_g[1][None], rwh, rwl, rb, tm=tm,
        scratch=[pltpu.VMEM((tm + 2 * POOL_HALO, d), F32)])
    (out,) = _moe(h2b, ridx_b, gcol_b, counts_b, x3, g2b, moe_w_gate, moe_w_up, moe_w_down, 1)
    return out
```

```python
import functools
import math

import jax
import jax.numpy as jnp
import numpy as np
from jax import lax
from jax.experimental import pallas as pl
from jax.experimental.pallas import tpu as pltpu
from jax.experimental.pallas import tpu_sc as plsc

F32 = jnp.float32
BF16 = jnp.bfloat16
F8 = jnp.float8_e4m3fn

LANES = 128
SUBLANES = 8
N_HEADS = 8
HEAD_DIM = 64
V_DIM = 2 * HEAD_DIM
V_ROWS = V_DIM + 32
P_SHIFT = 8.0
GRID_W = 64
ROPE_THETA = 10000.0
NORM_EPS = 1e-6
N_MOD = 6
POOL_WINDOWS = (2, 4, 8, 16)
POOL_HALO = max(POOL_WINDOWS) // 2
N_EXPERTS = 32
N_EXPERT_GROUPS = 4
EXPERTS_PER_GROUP = N_EXPERTS // N_EXPERT_GROUPS
TOP_K = 2
SLOT_ROWS = 512
SC_CORES = 2
SC_SUBCORES = 16
SC_WORKERS = SC_CORES * SC_SUBCORES
SC_WINDOW = 64
ATTN_GROUP_CHUNKS = 1
ATTN_SCORE_AHEAD = 1
VMEM_LIMIT = 48 * 1024 * 1024
NT_DIMS = (((1,), (1,)), ((), ()))


def _params(*sem):
    return pltpu.CompilerParams(dimension_semantics=sem, vmem_limit_bytes=VMEM_LIMIT)


def _rms_mod(x, gain, shift, scale):
    h = x * lax.rsqrt(jnp.mean(x * x, axis=-1, keepdims=True) + NORM_EPS) * gain
    return h * (1.0 + scale) + shift


def _pack_bf16_pairs(x):
    c = x.shape[1] // 2
    hi = lax.bitcast_convert_type(x[:, :c].astype(BF16).astype(F32), jnp.uint32)
    lo = lax.bitcast_convert_type(x[:, c:].astype(BF16).astype(F32), jnp.uint32)
    return lax.bitcast_convert_type(hi | (lo >> 16), jnp.int32)


def _unpack_bf16_pairs(w):
    u = lax.bitcast_convert_type(w, jnp.uint32)
    hi = lax.bitcast_convert_type(u & jnp.uint32(0xFFFF0000), F32)
    lo = lax.bitcast_convert_type(u << 16, F32)
    return jnp.concatenate([hi, lo], axis=1)


def _mod_kernel(c_ref, w_ref, b_ref, o_ref):
    c = c_ref[...]
    a = c / (1.0 + jnp.exp(-c))
    o_ref[0] = jnp.dot(a, w_ref[0], precision=lax.Precision.HIGHEST,
                       preferred_element_type=F32) + b_ref[0]


def _adaln_mod(c, c_ctx, ada_w, ada_b):
    depth, d, n_out = ada_w.shape
    b = c.shape[0]
    assert b + 1 <= SUBLANES
    rows = jnp.concatenate([c, c_ctx[None], jnp.zeros((SUBLANES - b - 1, d), F32)], axis=0)
    tn = n_out // 4
    return pl.pallas_call(
        _mod_kernel,
        grid=(depth, n_out // tn),
        in_specs=[pl.BlockSpec((SUBLANES, d), lambda i, j: (0, 0)),
                  pl.BlockSpec((1, d, tn), lambda i, j: (i, 0, j)),
                  pl.BlockSpec((1, 1, tn), lambda i, j: (i, 0, j))],
        out_specs=pl.BlockSpec((1, SUBLANES, tn), lambda i, j: (i, 0, j)),
        out_shape=jax.ShapeDtypeStruct((depth, SUBLANES, n_out), F32),
        compiler_params=_params("arbitrary", "arbitrary"),
        name="adaln_mod",
    )(rows, ada_w, ada_b.reshape(depth, 1, n_out))


def _qkv_kernel(x_ref, sh_ref, sc_ref, g_ref, w_ref, gains_ref, cos_ref, sin_ref, *refs, n_qk, rope):
    out_refs = refs[-(n_qk + 1):]
    tm, d = x_ref.shape[1:]
    hb = _rms_mod(x_ref[0], g_ref[...], sh_ref[0], sc_ref[0]).astype(BF16)
    lane_b4 = (lax.broadcasted_iota(jnp.int32, (1, LANES), 1) & 16) == 0
    chunk_r = lax.broadcasted_iota(jnp.int32, (2 * LANES, 2 * LANES), 0) // HEAD_DIM
    chunk_c = lax.broadcasted_iota(jnp.int32, (2 * LANES, 2 * LANES), 1) // HEAD_DIM
    same_chunk = (chunk_r == chunk_c).astype(BF16)
    gains = gains_ref[...]
    for t in range(n_qk):
        r = 2 * (t + 2 - n_qk)
        if rope:
            cos_t = cos_ref[...] * gains[r:r + 1]
            sin_t = sin_ref[...] * gains[r + 1:r + 2]
        for j in range(0, d, 2 * LANES):
            acc = jnp.dot(hb, w_ref[:, t * d + j:t * d + j + 2 * LANES], preferred_element_type=F32)
            ssq = jnp.dot((acc * acc).astype(BF16), same_chunk, preferred_element_type=F32)
            nrm = acc * lax.rsqrt(ssq * (1.0 / HEAD_DIM) + NORM_EPS)
            for half in range(2):
                blk = nrm[:, half * LANES:(half + 1) * LANES]
                if rope:
                    rot = jnp.where(lane_b4, pltpu.roll(blk, LANES - 16, 1), pltpu.roll(blk, 16, 1))
                    y = blk * cos_t + rot * sin_t
                else:
                    y = blk * gains[r:r + 1]
                c0 = j + half * LANES
                if t < n_qk - 1:
                    yt = y.T
                    first = lax.broadcasted_iota(jnp.int32, (V_DIM, 1), 0) < HEAD_DIM
                    out_refs[t][0, c0 // LANES, 0, :, 0:tm] = jnp.where(first, yt, 0.0).astype(F8)
                    out_refs[t][0, c0 // LANES, 0, :, tm:2 * tm] = jnp.where(first, 0.0, yt).astype(F8)
                else:
                    out_refs[t][0, :, c0:c0 + LANES] = y.astype(F8)
    vt_ref = out_refs[n_qk]
    ones = jnp.where(lax.broadcasted_iota(jnp.int32, (V_ROWS - V_DIM, tm), 0) == 0, 1.0, 0.0).astype(F8)
    for j in range(0, d, 2 * LANES):
        acc = jnp.dot(hb, w_ref[:, n_qk * d + j:n_qk * d + j + 2 * LANES], preferred_element_type=F32)
        acc_t = acc.T
        for half in range(2):
            h = j // LANES + half
            vt_ref[0, h, 0:V_DIM, :] = acc_t[half * V_DIM:(half + 1) * V_DIM].astype(F8)
            vt_ref[0, h, V_DIM:V_ROWS, :] = ones


def _qkv_proj(x, shift, scale, gain, w, gains, cos, sin, *, n_qk, rope, tm, n_keys, key_row0, kv=None):
    b, l, d = x.shape
    n_out = n_qk + 1
    kern = functools.partial(_qkv_kernel, n_qk=n_qk, rope=rope)
    row = lambda bi, i: (bi, 0, 0)
    fix = lambda bi, i: (0, 0)
    kb = key_row0 // tm
    tile = pl.BlockSpec((1, tm, d), lambda bi, i: (bi, i, 0))
    k_spec = pl.BlockSpec((1, tm, d), lambda bi, i: (bi, kb + i, 0))
    vt_spec = pl.BlockSpec((1, N_HEADS, V_ROWS, tm), lambda bi, i: (bi, 0, 0, kb + i))
    q_spec = pl.BlockSpec((1, N_HEADS, 1, V_DIM, 2 * tm), lambda bi, i: (bi, 0, i, 0, 0))
    q_shape = jax.ShapeDtypeStruct((b, N_HEADS, l // tm, V_DIM, 2 * tm), F8)
    k_shape = jax.ShapeDtypeStruct((b, n_keys, d), F8)
    vt_shape = jax.ShapeDtypeStruct((b, N_HEADS, V_ROWS, n_keys), F8)
    in_specs = [tile,
                pl.BlockSpec((1, 1, d), row), pl.BlockSpec((1, 1, d), row),
                pl.BlockSpec((1, d), fix),
                pl.BlockSpec((d, n_out * d), fix),
                pl.BlockSpec((4, LANES), fix),
                pl.BlockSpec((tm, LANES), lambda bi, i: (i, 0)),
                pl.BlockSpec((tm, LANES), lambda bi, i: (i, 0))]
    args = [x, shift, scale, gain, w, gains, cos, sin]
    aliases = {}
    if kv is not None:
        aliases = {len(args): n_qk - 1, len(args) + 1: n_qk}
        in_specs += [pl.BlockSpec(memory_space=pl.ANY)] * 2
        args += list(kv)
    return pl.pallas_call(
        kern,
        grid=(b, l // tm),
        in_specs=in_specs,
        out_specs=[q_spec] * (n_qk - 1) + [k_spec, vt_spec],
        out_shape=[q_shape] * (n_qk - 1) + [k_shape, vt_shape],
        input_output_aliases=aliases,
        compiler_params=_params("arbitrary", "arbitrary"),
        name="qkv_proj",
    )(*args)


def _rope_tables(n_tokens):
    rows = n_tokens // GRID_W
    row = np.repeat(np.arange(rows, dtype=np.float32), GRID_W)
    col = np.tile(np.arange(GRID_W, dtype=np.float32), rows)
    half = HEAD_DIM // 2
    inv_freq = (np.float32(ROPE_THETA) ** (-np.arange(0, half, 2, dtype=np.float32) / half)).astype(np.float32)
    ang_r = row[:, None] * inv_freq
    ang_c = col[:, None] * inv_freq
    ang = np.concatenate([ang_r, ang_r, ang_c, ang_c] * 2, axis=-1)
    sign = np.where((np.arange(LANES) & 16) == 0, -1.0, 1.0).astype(np.float32)
    return jnp.asarray(np.cos(ang), F32), jnp.asarray(np.sin(ang) * sign, F32)


def _attn_kernel(q_ref, k_ref, vt_ref, lp_ref, sg_ref, o_ref, s_ref, *, tk, group, ahead, lam_init):
    tq = q_ref.shape[4] // 2
    n_chunks = k_ref.shape[1] // tk
    qz = q_ref[0, 0, 0]
    n_slots = (ahead + 1) * group

    def score_chunk(c, m_grp):
        st = jnp.dot(k_ref[0, c * tk:(c + 1) * tk, :], qz, preferred_element_type=F32).astype(BF16)
        slot = c % n_slots
        s_ref[slot * tk:(slot + 1) * tk, :] = st
        mc = jnp.max(st, axis=0, keepdims=True)
        return mc if m_grp is None else jnp.maximum(m_grp, mc)

    def value_chunk(c, m_ref, part):
        slot = c % n_slots
        p = jnp.exp2(s_ref[slot * tk:(slot + 1) * tk, :] - (m_ref - P_SHIFT)).astype(F8)
        pv = jnp.dot(vt_ref[0, 0, :, c * tk:(c + 1) * tk], p, preferred_element_type=F32)
        return pv if part is None else part + pv

    groups = [list(range(g0, min(g0 + group, n_chunks))) for g0 in range(0, n_chunks, group)]
    m_of = {}
    for g in range(min(ahead, len(groups))):
        for c in groups[g]:
            m_of[g] = score_chunk(c, m_of.get(g))
    m = None
    acc = None
    for gi, cur in enumerate(groups):
        nxt = groups[gi + ahead] if gi + ahead < len(groups) else []
        m_new = m_of[gi] if m is None else jnp.maximum(m, m_of[gi])
        part = None
        for i in range(max(len(cur), len(nxt))):
            if i < len(nxt):
                m_of[gi + ahead] = score_chunk(nxt[i], m_of.get(gi + ahead))
            if i < len(cur):
                part = value_chunk(cur[i], m_new, part)
        acc = part if acc is None else acc * jnp.exp2(m.astype(F32) - m_new.astype(F32)) + part
        m = m_new
    acc = acc[:V_DIM] / acc[V_DIM:V_DIM + 1]
    lp = lp_ref[...]
    lam = (jnp.exp(jnp.sum(lp[0:1] * lp[1:2], axis=-1, keepdims=True))
           - jnp.exp(jnp.sum(lp[2:3] * lp[3:4], axis=-1, keepdims=True)) + lam_init)
    o = acc[:, :tq] - lam * acc[:, tq:]
    o = o * lax.rsqrt(jnp.mean(o * o, axis=0, keepdims=True) + NORM_EPS) * sg_ref[...] * (1.0 - lam_init)
    o_ref[0] = o.T.astype(BF16)


def _attn_chunk(n_keys):
    for tk in (768, 512, 256, 128):
        if n_keys % tk == 0:
            return tk
    raise ValueError(f"key count {n_keys} is not a multiple of {LANES}")


def _diff_attention(qz, k_all, vt_all, lam_params, sub_gain, *, lam_init):
    b, _, n_tiles, _, tq2 = qz.shape
    tq = tq2 // 2
    l, d = n_tiles * tq, N_HEADS * V_DIM
    n_keys = k_all.shape[1]
    tk = _attn_chunk(n_keys)
    kern = functools.partial(_attn_kernel, tk=tk, group=ATTN_GROUP_CHUNKS, ahead=ATTN_SCORE_AHEAD, lam_init=lam_init)
    return pl.pallas_call(
        kern,
        grid=(b, N_HEADS, l // tq),
        in_specs=[pl.BlockSpec((1, 1, 1, V_DIM, 2 * tq), lambda bi, h, i: (bi, h, i, 0, 0)),
                  pl.BlockSpec((1, n_keys, V_DIM), lambda bi, h, i: (bi, 0, h)),
                  pl.BlockSpec((1, 1, V_ROWS, n_keys), lambda bi, h, i: (bi, h, 0, 0)),
                  pl.BlockSpec((4, HEAD_DIM), lambda bi, h, i: (0, 0)),
                  pl.BlockSpec((V_DIM, 1), lambda bi, h, i: (0, 0))],
        out_specs=pl.BlockSpec((1, tq, V_DIM), lambda bi, h, i: (bi, i, h)),
        out_shape=jax.ShapeDtypeStruct((b, l, d), BF16),
        scratch_shapes=[pltpu.VMEM(((ATTN_SCORE_AHEAD + 1) * ATTN_GROUP_CHUNKS * tk, 2 * tq), BF16)],
        compiler_params=_params("arbitrary", "arbitrary", "arbitrary"),
        name="diff_attention",
    )(qz, k_all, vt_all, lam_params, sub_gain)


def _route(h2, rwh_ref, rwl_ref, rb_ref, carry_ref, ridx_ref, gcol_ref, cnt_ref, is_first):
    tm = h2.shape[0]
    hh = h2.astype(BF16)
    hl = (h2 - hh.astype(F32)).astype(BF16)
    rw2 = jnp.concatenate([rwh_ref[...], rwl_ref[...]], axis=0)
    part = lax.dot_general(rw2, hh, NT_DIMS, preferred_element_type=F32)
    logits = (part[:N_EXPERTS] + part[N_EXPERTS:]
              + lax.dot_general(rwh_ref[...], hl, NT_DIMS, preferred_element_type=F32) + rb_ref[...])
    groups = [logits[g * EXPERTS_PER_GROUP:(g + 1) * EXPERTS_PER_GROUP] for g in range(N_EXPERT_GROUPS)]
    top = groups[0]
    for g in range(1, N_EXPERT_GROUPS):
        top = jnp.maximum(top, groups[g])
    top = jnp.max(top, axis=0, keepdims=True)
    sub = lax.broadcasted_iota(jnp.int32, (EXPERTS_PER_GROUP, tm), 0)
    best = None
    for g in range(N_EXPERT_GROUPS):
        ex = jnp.exp(groups[g] - top)
        v1 = jnp.max(ex, axis=0, keepdims=True)
        i1 = jnp.min(jnp.where(ex == v1, sub, EXPERTS_PER_GROUP), axis=0, keepdims=True)
        rest = jnp.where(sub == i1, -1.0, ex)
        v2 = jnp.max(rest, axis=0, keepdims=True)
        i2 = jnp.min(jnp.where(rest == v2, sub, EXPERTS_PER_GROUP), axis=0, keepdims=True)
        cand = (v1 + v2, v1, v2, i1 + g * EXPERTS_PER_GROUP, i2 + g * EXPERTS_PER_GROUP)
        if best is None:
            best = cand
        else:
            better = cand[0] > best[0]
            best = tuple(jnp.where(better, new, old) for new, old in zip(cand, best))
    _, v1, v2, e0, e1 = best
    gate0 = v1 / (v1 + v2)
    gate1 = v2 / (v1 + v2)

    @pl.when(is_first)
    def _():
        carry_ref[...] = jnp.zeros_like(carry_ref)

    erow = lax.broadcasted_iota(jnp.int32, (N_EXPERTS, tm), 0)
    oh0 = erow == e0
    oh1 = erow == e1
    chosen = jnp.where(oh0 | oh1, 1.0, 0.0)
    before = (lax.broadcasted_iota(jnp.int32, (tm, tm), 0)
              < lax.broadcasted_iota(jnp.int32, (tm, tm), 1)).astype(BF16)
    prior = carry_ref[:, 0:1] + jnp.dot(chosen.astype(BF16), before, preferred_element_type=F32)
    r0 = jnp.sum(jnp.where(oh0, prior, 0.0), axis=0, keepdims=True).astype(jnp.int32)
    r1 = jnp.sum(jnp.where(oh1, prior, 0.0), axis=0, keepdims=True).astype(jnp.int32)
    carry_ref[...] = carry_ref[...] + jnp.sum(chosen, axis=1, keepdims=True)
    cnt_ref[...] = carry_ref[...]
    rid = lax.broadcasted_iota(jnp.int32, (SUBLANES, tm), 0)
    ridx_ref[...] = jnp.where(rid == 0, e0, jnp.where(rid == 1, e1, jnp.where(rid == 2, r0, jnp.where(rid == 3, r1, 0))))
    gid = lax.broadcasted_iota(jnp.int32, (LANES, tm), 0)
    gcol_ref[...] = jnp.where(gid == 0, gate0, jnp.where(gid == 1, gate1, 0.0)).T


def _tail(y, x_ref, g1_ref, sh2_ref, s2_ref, n2_ref, rwh_ref, rwl_ref, rb_ref,
          x_out, h2_out, ridx_ref, gcol_ref, cnt_ref, carry_ref):
    x1 = x_ref[0] + g1_ref[0] * y
    x_out[0] = x1
    h2 = _rms_mod(x1, n2_ref[...], sh2_ref[0], s2_ref[0])
    h2_out[0] = _pack_bf16_pairs(h2)
    is_first = (pl.program_id(0) == 0) & (pl.program_id(1) == 0)
    _route(h2, rwh_ref, rwl_ref, rb_ref, carry_ref, ridx_ref, gcol_ref, cnt_ref, is_first)


def _attn_tail_kernel(a_ref, wo_ref, *rest):
    y = jnp.dot(a_ref[0], wo_ref[...], preferred_element_type=F32)
    _tail(y, *rest)


def _pool_tail_kernel(u_ref, up_ref, un_ref, wg_ref, cs_ref, wo_ref, *rest, seq_len):
    *tail_refs, ubuf = rest
    tm = u_ref.shape[1]
    i = pl.program_id(1)
    u = u_ref[0]
    ubuf[0:POOL_HALO] = jnp.where(i > 0, up_ref[0], 0.0)
    ubuf[POOL_HALO:POOL_HALO + tm] = u
    ubuf[POOL_HALO + tm:2 * POOL_HALO + tm] = jnp.where(i < pl.num_programs(1) - 1, un_ref[0], 0.0)
    pos = i * tm + lax.broadcasted_iota(jnp.int32, (tm, 1), 0)
    gd = wg_ref.shape[1]
    outs = []
    for g, win in enumerate(POOL_WINDOWS):
        half = win // 2
        cols = slice(g * gd, (g + 1) * gd)
        s = ubuf[POOL_HALO - half:POOL_HALO - half + tm, cols]
        for j in range(1 - half, half):
            s = s + ubuf[POOL_HALO + j:POOL_HALO + j + tm, cols]
        cnt = (jnp.minimum(pos + half, seq_len) - jnp.maximum(pos - half, 0)).astype(F32)
        dlt = (s / cnt - u[:, cols]).astype(BF16)
        outs.append(jnp.dot(dlt, wg_ref[g], preferred_element_type=F32))
    z = (jnp.concatenate(outs, axis=-1) * cs_ref[...]).astype(BF16)
    y = jnp.dot(z, wo_ref[...], preferred_element_type=F32)
    _tail(y, *tail_refs)


def _mixer_tail(front_args, front_specs, kern, x, g1, sh2, s2, n2g, rwh, rwl, rb, *, tm, scratch=()):
    b, l, d = x.shape
    nt = l // tm
    n = b * l
    row = lambda bi, i: (bi, 0, 0)
    fix = lambda bi, i: (0, 0)
    tile = lambda bi, i: (bi, i, 0)
    in_specs = list(front_specs) + [
        pl.BlockSpec((1, tm, d), tile),
        pl.BlockSpec((1, 1, d), row), pl.BlockSpec((1, 1, d), row), pl.BlockSpec((1, 1, d), row),
        pl.BlockSpec((1, d), fix),
        pl.BlockSpec((N_EXPERTS, d), fix), pl.BlockSpec((N_EXPERTS, d), fix),
        pl.BlockSpec((N_EXPERTS, 1), fix)]
    out_specs = [pl.BlockSpec((1, tm, d), tile), pl.BlockSpec((1, tm, d // 2), tile),
                 pl.BlockSpec((SUBLANES, tm), lambda bi, i: (0, bi * nt + i)),
                 pl.BlockSpec((tm, LANES), lambda bi, i: (bi * nt + i, 0)),
                 pl.BlockSpec((N_EXPERTS, LANES), fix)]
    out_shape = [jax.ShapeDtypeStruct((b, l, d), F32), jax.ShapeDtypeStruct((b, l, d // 2), jnp.int32),
                 jax.ShapeDtypeStruct((SUBLANES, n), jnp.int32), jax.ShapeDtypeStruct((n, LANES), F32),
                 jax.ShapeDtypeStruct((N_EXPERTS, LANES), F32)]
    return pl.pallas_call(
        kern,
        grid=(b, nt),
        in_specs=in_specs, out_specs=out_specs, out_shape=out_shape,
        scratch_shapes=[pltpu.VMEM((N_EXPERTS, LANES), F32)] + list(scratch),
        compiler_params=_params("arbitrary", "arbitrary"),
        name="mixer_tail",
    )(*front_args, x, g1, sh2, s2, n2g, rwh, rwl, rb)


def _slot_kernel(ps_ref, ridx_ref, dest_ref):
    ridx = ridx_ref[...]
    ps = ps_ref[...]
    erow = lax.broadcasted_iota(jnp.int32, (N_EXPERTS, ridx.shape[1]), 0)
    rows = []
    for k in range(TOP_K):
        start = jnp.sum(jnp.where(erow == ridx[k:k + 1], ps, 0), axis=0, keepdims=True)
        rows.append(start + ridx[TOP_K + k:TOP_K + k + 1])
    rid = lax.broadcasted_iota(jnp.int32, ridx.shape, 0)
    dest_ref[...] = jnp.where(rid == 0, rows[0], jnp.where(rid == 1, rows[1], 0))


def _slot_index(pad_start, ridx, *, tn):
    n = ridx.shape[1]
    return pl.pallas_call(
        _slot_kernel,
        grid=(n // tn,),
        in_specs=[pl.BlockSpec((N_EXPERTS, 1), lambda i: (0, 0)),
                  pl.BlockSpec((SUBLANES, tn), lambda i: (0, i))],
        out_specs=pl.BlockSpec((SUBLANES, tn), lambda i: (0, i)),
        out_shape=jax.ShapeDtypeStruct((SUBLANES, n), jnp.int32),
        compiler_params=_params("arbitrary"),
        name="slot_index",
    )(pad_start.reshape(N_EXPERTS, 1), ridx)


def _sc_mesh():
    return plsc.VectorSubcoreMesh(core_axis_name="c", subcore_axis_name="s",
                                  num_cores=SC_CORES, num_subcores=SC_SUBCORES)


def _sc_worker_base(per_worker):
    return (lax.axis_index("s") * SC_CORES + lax.axis_index("c")) * per_worker


def _sc_scatter_rows(rows, idx0, idx1, n_slots):
    n, d = rows.shape
    per_worker = n // SC_WORKERS
    assert per_worker % SC_WINDOW == 0

    n_win = per_worker // SC_WINDOW
    assert n_win % 2 == 0

    def body(rows_hbm, i0_hbm, i1_hbm, out_hbm, i0_a, i1_a, rows_a, i0_b, i1_b, rows_b, sem_a, sem_b):
        base = _sc_worker_base(per_worker)

        def offset(j):
            return pl.multiple_of(base + j * SC_WINDOW, SC_WINDOW)

        def start(j, i0_v, i1_v, rows_v, sem):
            pltpu.sync_copy(i0_hbm.at[pl.ds(offset(j), SC_WINDOW)], i0_v)
            pltpu.sync_copy(i1_hbm.at[pl.ds(offset(j), SC_WINDOW)], i1_v)
            pltpu.async_copy(rows_hbm.at[pl.ds(offset(j), SC_WINDOW)], rows_v, sem)

        def finish(j, i0_v, i1_v, rows_v, sem):
            pltpu.make_async_copy(rows_hbm.at[pl.ds(offset(j), SC_WINDOW)], rows_v, sem).wait()
            pltpu.sync_copy(rows_v, out_hbm.at[i0_v])
            pltpu.sync_copy(rows_v, out_hbm.at[i1_v])

        start(0, i0_a, i1_a, rows_a, sem_a)

        @pl.loop(0, n_win, step=2)
        def _(j):
            start(j + 1, i0_b, i1_b, rows_b, sem_b)
            finish(j, i0_a, i1_a, rows_a, sem_a)

            @pl.when(j + 2 < n_win)
            def _():
                start(j + 2, i0_a, i1_a, rows_a, sem_a)

            finish(j + 1, i0_b, i1_b, rows_b, sem_b)

    window = [pltpu.VMEM((SC_WINDOW,), jnp.int32), pltpu.VMEM((SC_WINDOW,), jnp.int32),
              pltpu.VMEM((SC_WINDOW, d), rows.dtype)]
    return pl.kernel(
        body, out_type=jax.ShapeDtypeStruct((n_slots, d), rows.dtype), mesh=_sc_mesh(),
        scratch_types=window + window + [pltpu.SemaphoreType.DMA, pltpu.SemaphoreType.DMA],
        name="sc_scatter_rows",
    )(rows, idx0, idx1)


def _sc_gather_rows(table, idx):
    n = idx.shape[0]
    d = table.shape[1]
    per_worker = n // SC_WORKERS
    assert per_worker % SC_WINDOW == 0

    n_win = per_worker // SC_WINDOW
    assert n_win % 2 == 0

    def body(table_hbm, idx_hbm, out_hbm, idx_a, idx_b, rows_a, rows_b, sem_a, sem_b):
        base = _sc_worker_base(per_worker)

        def offset(j):
            return pl.multiple_of(base + j * SC_WINDOW, SC_WINDOW)

        def start(j, idx_v, rows_v, sem):
            pltpu.sync_copy(idx_hbm.at[pl.ds(offset(j), SC_WINDOW)], idx_v)
            pltpu.async_copy(table_hbm.at[idx_v], rows_v, sem)

        def finish(j, idx_v, rows_v, sem):
            pltpu.make_async_copy(table_hbm.at[idx_v], rows_v, sem).wait()
            pltpu.sync_copy(rows_v, out_hbm.at[pl.ds(offset(j), SC_WINDOW)])

        start(0, idx_a, rows_a, sem_a)

        @pl.loop(0, n_win, step=2)
        def _(j):
            start(j + 1, idx_b, rows_b, sem_b)
            finish(j, idx_a, rows_a, sem_a)

            @pl.when(j + 2 < n_win)
            def _():
                start(j + 2, idx_a, rows_a, sem_a)

            finish(j + 1, idx_b, rows_b, sem_b)

    return pl.kernel(
        body, out_type=jax.ShapeDtypeStruct((n, d), table.dtype), mesh=_sc_mesh(),
        scratch_types=[pltpu.VMEM((SC_WINDOW,), jnp.int32), pltpu.VMEM((SC_WINDOW,), jnp.int32),
                       pltpu.VMEM((SC_WINDOW, d), table.dtype), pltpu.VMEM((SC_WINDOW, d), table.dtype),
                       pltpu.SemaphoreType.DMA, pltpu.SemaphoreType.DMA],
        name="sc_gather_rows",
    )(table, idx)


def _ffn_kernel(be_ref, nv_ref, xs_ref, wg_ref, wu_ref, wd_ref, ys_ref, wg_b, wu_b, wd_b):
    j = pl.program_id(0)
    valid = nv_ref[j]

    @pl.when((valid > 0) & ((j == 0) | (be_ref[j] != be_ref[jnp.maximum(j - 1, 0)])))
    def _():
        wg_b[...] = wg_ref[0, 0].astype(BF16)
        wu_b[...] = wu_ref[0, 0].astype(BF16)
        wd_b[...] = wd_ref[0, 0].astype(BF16)

    @pl.when(valid > 0)
    def _():
        row = lax.broadcasted_iota(jnp.int32, (SLOT_ROWS, 1), 0)
        xw = jnp.where(row < valid, xs_ref[...], 0)
        xb = _unpack_bf16_pairs(xw).astype(BF16)
        g = jnp.dot(xb, wg_b[...], preferred_element_type=F32)
        u = jnp.dot(xb, wu_b[...], preferred_element_type=F32)
        a = (g / (1.0 + jnp.exp(-g)) * u).astype(BF16)
        ys_ref[...] = _pack_bf16_pairs(jnp.dot(a, wd_b[...], preferred_element_type=F32))

    @pl.when(valid <= 0)
    def _():
        ys_ref[...] = jnp.zeros_like(ys_ref)


def _expert_ffn(block_e, n_valid, xs, w_gate, w_up, w_down, layer):
    n_slots, dw = xs.shape
    d, de = w_gate.shape[2:]
    w_idx = lambda j, be, nv: (layer, be[j], 0, 0)
    return pl.pallas_call(
        _ffn_kernel,
        grid_spec=pltpu.PrefetchScalarGridSpec(
            num_scalar_prefetch=2,
            grid=(n_slots // SLOT_ROWS,),
            in_specs=[pl.BlockSpec((SLOT_ROWS, dw), lambda j, be, nv: (j, 0)),
                      pl.BlockSpec((1, 1, d, de), w_idx),
                      pl.BlockSpec((1, 1, d, de), w_idx),
                      pl.BlockSpec((1, 1, de, d), w_idx)],
            out_specs=pl.BlockSpec((SLOT_ROWS, dw), lambda j, be, nv: (j, 0)),
            scratch_shapes=[pltpu.VMEM((d, de), BF16), pltpu.VMEM((d, de), BF16), pltpu.VMEM((de, d), BF16)]),
        out_shape=jax.ShapeDtypeStruct((n_slots, dw), jnp.int32),
        compiler_params=_params("arbitrary"),
        name="expert_ffn",
    )(block_e, n_valid, xs, w_gate, w_up, w_down)


def _combine_kernel(y0_ref, y1_ref, gcol_ref, x_ref, g2_ref, *rest, pool_in):
    if pool_in:
        sh_ref, sc_ref, n1_ref, wi_ref, x_out, u_out = rest
    else:
        (x_out,) = rest
    gc = gcol_ref[...]
    out = gc[:, 0:1] * _unpack_bf16_pairs(y0_ref[0]) + gc[:, 1:2] * _unpack_bf16_pairs(y1_ref[0])
    x2 = x_ref[0] + g2_ref[0] * out
    x_out[0] = x2
    if pool_in:
        hb = _rms_mod(x2, n1_ref[...], sh_ref[0], sc_ref[0]).astype(BF16)
        u_out[0] = jnp.dot(hb, wi_ref[...], preferred_element_type=F32)


def _combine(yg, gcol, x, g2, pool_args=None, *, tc):
    b, l, d = x.shape
    nt = l // tc
    pool_in = pool_args is not None
    kern = functools.partial(_combine_kernel, pool_in=pool_in)
    row = lambda bi, i: (bi, 0, 0)
    fix = lambda bi, i: (0, 0)
    tile = lambda bi, i: (bi, i, 0)
    in_specs = [pl.BlockSpec((1, tc, d // 2), lambda bi, i: (0, bi * nt + i, 0)),
                pl.BlockSpec((1, tc, d // 2), lambda bi, i: (1, bi * nt + i, 0)),
                pl.BlockSpec((tc, LANES), lambda bi, i: (bi * nt + i, 0)),
                pl.BlockSpec((1, tc, d), tile),
                pl.BlockSpec((1, 1, d), row)]
    out_specs = [pl.BlockSpec((1, tc, d), tile)]
    out_shape = [jax.ShapeDtypeStruct((b, l, d), F32)]
    args = [yg, yg, gcol, x, g2]
    if pool_in:
        in_specs += [pl.BlockSpec((1, 1, d), row), pl.BlockSpec((1, 1, d), row),
                     pl.BlockSpec((1, d), fix), pl.BlockSpec((d, d), fix)]
        out_specs.append(pl.BlockSpec((1, tc, d), tile))
        out_shape.append(jax.ShapeDtypeStruct((b, l, d), F32))
        args += list(pool_args)
    return pl.pallas_call(
        kern,
        grid=(b, nt),
        in_specs=in_specs, out_specs=out_specs, out_shape=out_shape,
        compiler_params=_params("arbitrary", "arbitrary"),
        name="moe_combine",
    )(*args)


def _moe(h2, ridx, gcol, counts, x1, g2, w_gate, w_up, w_down, layer, pool_args=None):
    b, l, d = x1.shape
    n = b * l
    n_blocks = (n * TOP_K) // SLOT_ROWS + N_EXPERTS
    cnt = counts[:, 0].astype(jnp.int32)
    padded = (cnt + SLOT_ROWS - 1) // SLOT_ROWS * SLOT_ROWS
    earlier = jnp.arange(N_EXPERTS)[None, :] < jnp.arange(N_EXPERTS)[:, None]
    pad_start = jnp.sum(jnp.where(earlier, padded[None, :], 0), axis=1).astype(jnp.int32)
    pad_end = pad_start + padded
    block_start = jnp.arange(n_blocks, dtype=jnp.int32) * SLOT_ROWS
    block_e = jnp.minimum(jnp.sum(pad_end[None, :] <= block_start[:, None], axis=1), N_EXPERTS - 1).astype(jnp.int32)
    own = block_e[:, None] == jnp.arange(N_EXPERTS)[None, :]
    data_end = jnp.sum(jnp.where(own, (pad_start + cnt)[None, :], 0), axis=1)
    n_valid = jnp.clip(data_end - block_start, 0, SLOT_ROWS).astype(jnp.int32)
    dest = _slot_index(pad_start, ridx, tn=2048)
    xs = _sc_scatter_rows(h2.reshape(n, d // 2), dest[0], dest[1], n_blocks * SLOT_ROWS)
    ys = _expert_ffn(block_e, n_valid, xs, w_gate, w_up, w_down, layer)
    yg = _sc_gather_rows(ys, dest[:TOP_K].reshape(TOP_K * n)).reshape(TOP_K, n, d // 2)
    return _combine(yg, gcol, x1, g2, pool_args, tc=512)


def kernel(x, c, ctx, c_ctx, ada_w, ada_b, norm1_g, norm2_g, attn_w_in, attn_w_out, attn_q_gain, attn_k_gain,
           attn_lq1, attn_lk1, attn_lq2, attn_lk2, attn_sub_gain, pool_w_in, pool_w_group, pool_scale, pool_w_out,
           router_w, router_b, moe_w_gate, moe_w_up, moe_w_down):
    b, l, d = x.shape
    n_ctx = ctx.shape[1]
    depth = ada_w.shape[0]
    assert depth == 2 and d == N_HEADS * V_DIM
    tm = 512

    mod = _adaln_mod(c, c_ctx, ada_w, ada_b)
    mods = [[mod[i, :b, None, j * d:(j + 1) * d] for j in range(N_MOD)] for i in range(depth)]
    mod_ctx = [jnp.broadcast_to(mod[0, b, j * d:(j + 1) * d], (b, 1, d)) for j in range(2)]

    rwt = router_w.T
    rwh = rwt.astype(BF16)
    rwl = (rwt - rwh.astype(F32)).astype(BF16)
    rb = router_b.reshape(N_EXPERTS, 1)

    sh1, s1, g1, sh2, s2, g2 = mods[0]
    cos, sin = _rope_tables(l)
    pair_up = (jnp.arange(LANES) & 16) == 0

    def gain_rows(g, factor):
        g2 = jnp.concatenate([g, g]) * factor
        return [g2, jnp.where(pair_up, jnp.roll(g2, -16), jnp.roll(g2, 16))]

    gains = jnp.stack(gain_rows(attn_q_gain[0], HEAD_DIM ** -0.5 * math.log2(math.e))
                      + gain_rows(attn_k_gain[0], 1.0))
    w_in = attn_w_in[0].astype(BF16)
    assert l % tm == 0 and l % n_ctx == 0
    q, k_all, vt_all = _qkv_proj(x, sh1, s1, norm1_g[0][None], w_in, gains, cos, sin, n_qk=2, rope=True,
                                 tm=tm, n_keys=l + n_ctx, key_row0=0)
    k_all, vt_all = _qkv_proj(ctx, mod_ctx[0], mod_ctx[1], norm1_g[0][None], w_in[:, d:], gains,
                              cos[:n_ctx], sin[:n_ctx], n_qk=1, rope=False,
                              tm=n_ctx, n_keys=l + n_ctx, key_row0=l, kv=(k_all, vt_all))
    lam_init = 0.8 - 0.6 * math.exp(-0.3 * 0)
    lam_params = jnp.stack([attn_lq1[0], attn_lk1[0], attn_lq2[0], attn_lk2[0]])
    o = _diff_attention(q, k_all, vt_all, lam_params, attn_sub_gain[0][:, None], lam_init=lam_init)

    fix = lambda bi, i: (0, 0)
    x1, h2, ridx, gcol, counts = _mixer_tail(
        (o, attn_w_out[0].astype(BF16)),
        (pl.BlockSpec((1, tm, d), lambda bi, i: (bi, i, 0)), pl.BlockSpec((d, d), fix)),
        _attn_tail_kernel, x, g1, sh2, s2, norm2_g[0][None], rwh, rwl, rb, tm=tm)

    sh1b, s1b, g1b, sh2b, s2b, g2b = mods[1]
    x2, u = _moe(h2, ridx, gcol, counts, x1, g2, moe_w_gate, moe_w_up, moe_w_down, 0,
                 pool_args=(sh1b, s1b, norm1_g[1][None], pool_w_in[0].astype(BF16)))
    gd = pool_w_group.shape[2]
    nh = tm // POOL_HALO
    front_specs = (
        pl.BlockSpec((1, tm, d), lambda bi, i: (bi, i, 0)),
        pl.BlockSpec((1, POOL_HALO, d), lambda bi, i: (bi, jnp.maximum(i * nh - 1, 0), 0)),
        pl.BlockSpec((1, POOL_HALO, d), lambda bi, i: (bi, jnp.minimum((i + 1) * nh, l // POOL_HALO - 1), 0)),
        pl.BlockSpec((len(POOL_WINDOWS), gd, gd), lambda bi, i: (0, 0, 0)),
        pl.BlockSpec((1, d), fix),
        pl.BlockSpec((d, d), fix))
    x3, h2b, ridx_b, gcol_b, counts_b = _mixer_tail(
        (u, u, u, pool_w_group[0].astype(BF16), pool_scale[0][None], pool_w_out[0].astype(BF16)),
        front_specs, functools.partial(_pool_tail_kernel, seq_len=l),
        x2, g1b, sh2b, s2b, norm2_g[1][None], rwh, rwl, rb, tm=tm,
        scratch=[pltpu.VMEM((tm + 2 * POOL_HALO, d), F32)])
    (out,) = _moe(h2b, ridx_b, gcol_b, counts_b, x3, g2b, moe_w_gate, moe_w_up, moe_w_down, 1)
    return out
```

```python
import functools
import math

import jax
import jax.numpy as jnp
import numpy as np
from jax import lax
from jax.experimental import pallas as pl
from jax.experimental.pallas import tpu as pltpu
from jax.experimental.pallas import tpu_sc as plsc

F32 = jnp.float32
BF16 = jnp.bfloat16
F8 = jnp.float8_e4m3fn
F8_MAX = float(jnp.finfo(F8).max)
F32_TINY = float(jnp.finfo(F32).tiny)

LANES = 128
SUBLANES = 8
N_HEADS = 8
HEAD_DIM = 64
V_DIM = 2 * HEAD_DIM
V_ROWS = V_DIM + 32
P_SHIFT = 8.0
GRID_W = 64
ROPE_THETA = 10000.0
NORM_EPS = 1e-6
N_MOD = 6
POOL_WINDOWS = (2, 4, 8, 16)
POOL_HALO = max(POOL_WINDOWS) // 2
N_EXPERTS = 32
N_EXPERT_GROUPS = 4
EXPERTS_PER_GROUP = N_EXPERTS // N_EXPERT_GROUPS
TOP_K = 2
SLOT_ROWS = 512
SC_CORES = 2
SC_SUBCORES = 16
SC_WORKERS = SC_CORES * SC_SUBCORES
SC_WINDOW = 64
ATTN_GROUP_CHUNKS = 1
ATTN_SCORE_AHEAD = 1
VMEM_LIMIT = 48 * 1024 * 1024
NT_DIMS = (((1,), (1,)), ((), ()))


def _params(*sem):
    return pltpu.CompilerParams(dimension_semantics=sem, vmem_limit_bytes=VMEM_LIMIT)


def _rms_mod(x, gain, shift, scale):
    h = x * lax.rsqrt(jnp.mean(x * x, axis=-1, keepdims=True) + NORM_EPS) * gain
    return h * (1.0 + scale) + shift


def _pack_bf16_pairs(x):
    c = x.shape[1] // 2
    hi = lax.bitcast_convert_type(x[:, :c].astype(BF16).astype(F32), jnp.uint32)
    lo = lax.bitcast_convert_type(x[:, c:].astype(BF16).astype(F32), jnp.uint32)
    return lax.bitcast_convert_type(hi | (lo >> 16), jnp.int32)


def _unpack_bf16_pairs(w):
    u = lax.bitcast_convert_type(w, jnp.uint32)
    hi = lax.bitcast_convert_type(u & jnp.uint32(0xFFFF0000), F32)
    lo = lax.bitcast_convert_type(u << 16, F32)
    return jnp.concatenate([hi, lo], axis=1)


def _mod_kernel(c_ref, w_ref, b_ref, o_ref):
    c = c_ref[...]
    a = c / (1.0 + jnp.exp(-c))
    o_ref[0] = jnp.dot(a, w_ref[0], precision=lax.Precision.HIGHEST,
                       preferred_element_type=F32) + b_ref[0]


def _adaln_mod(c, c_ctx, ada_w, ada_b):
    depth, d, n_out = ada_w.shape
    b = c.shape[0]
    assert b + 1 <= SUBLANES
    rows = jnp.concatenate([c, c_ctx[None], jnp.zeros((SUBLANES - b - 1, d), F32)], axis=0)
    tn = n_out // 4
    return pl.pallas_call(
        _mod_kernel,
        grid=(depth, n_out // tn),
        in_specs=[pl.BlockSpec((SUBLANES, d), lambda i, j: (0, 0)),
                  pl.BlockSpec((1, d, tn), lambda i, j: (i, 0, j)),
                  pl.BlockSpec((1, 1, tn), lambda i, j: (i, 0, j))],
        out_specs=pl.BlockSpec((1, SUBLANES, tn), lambda i, j: (i, 0, j)),
        out_shape=jax.ShapeDtypeStruct((depth, SUBLANES, n_out), F32),
        compiler_params=_params("arbitrary", "arbitrary"),
        name="adaln_mod",
    )(rows, ada_w, ada_b.reshape(depth, 1, n_out))


def _qkv_kernel(x_ref, sh_ref, sc_ref, g_ref, w_ref, gains_ref, cos_ref, sin_ref, *refs, n_qk, rope):
    out_refs = refs[-(n_qk + 1):]
    tm, d = x_ref.shape[1:]
    hb = _rms_mod(x_ref[0], g_ref[...], sh_ref[0], sc_ref[0]).astype(BF16)
    lane_b4 = (lax.broadcasted_iota(jnp.int32, (1, LANES), 1) & 16) == 0
    chunk_r = lax.broadcasted_iota(jnp.int32, (2 * LANES, 2 * LANES), 0) // HEAD_DIM
    chunk_c = lax.broadcasted_iota(jnp.int32, (2 * LANES, 2 * LANES), 1) // HEAD_DIM
    same_chunk = (chunk_r == chunk_c).astype(BF16)
    gains = gains_ref[...]
    for t in range(n_qk):
        r = 2 * (t + 2 - n_qk)
        if rope:
            cos_t = cos_ref[...] * gains[r:r + 1]
            sin_t = sin_ref[...] * gains[r + 1:r + 2]
        for j in range(0, d, 2 * LANES):
            acc = jnp.dot(hb, w_ref[:, t * d + j:t * d + j + 2 * LANES], preferred_element_type=F32)
            ssq = jnp.dot((acc * acc).astype(BF16), same_chunk, preferred_element_type=F32)
            nrm = acc * lax.rsqrt(ssq * (1.0 / HEAD_DIM) + NORM_EPS)
            for half in range(2):
                blk = nrm[:, half * LANES:(half + 1) * LANES]
                if rope:
                    rot = jnp.where(lane_b4, pltpu.roll(blk, LANES - 16, 1), pltpu.roll(blk, 16, 1))
                    y = blk * cos_t + rot * sin_t
                else:
                    y = blk * gains[r:r + 1]
                c0 = j + half * LANES
                if t < n_qk - 1:
                    yt = y.T
                    first = lax.broadcasted_iota(jnp.int32, (V_DIM, 1), 0) < HEAD_DIM
                    out_refs[t][0, c0 // LANES, 0, :, 0:tm] = jnp.where(first, yt, 0.0).astype(F8)
                    out_refs[t][0, c0 // LANES, 0, :, tm:2 * tm] = jnp.where(first, 0.0, yt).astype(F8)
                else:
                    out_refs[t][0, :, c0:c0 + LANES] = y.astype(F8)
    vt_ref = out_refs[n_qk]
    ones = jnp.where(lax.broadcasted_iota(jnp.int32, (V_ROWS - V_DIM, tm), 0) == 0, 1.0, 0.0).astype(F8)
    for j in range(0, d, 2 * LANES):
        acc = jnp.dot(hb, w_ref[:, n_qk * d + j:n_qk * d + j + 2 * LANES], preferred_element_type=F32)
        acc_t = acc.T
        for half in range(2):
            h = j // LANES + half
            vt_ref[0, h, 0:V_DIM, :] = acc_t[half * V_DIM:(half + 1) * V_DIM].astype(F8)
            vt_ref[0, h, V_DIM:V_ROWS, :] = ones


def _qkv_proj(x, shift, scale, gain, w, gains, cos, sin, *, n_qk, rope, tm, n_keys, key_row0, kv=None):
    b, l, d = x.shape
    n_out = n_qk + 1
    kern = functools.partial(_qkv_kernel, n_qk=n_qk, rope=rope)
    row = lambda bi, i: (bi, 0, 0)
    fix = lambda bi, i: (0, 0)
    kb = key_row0 // tm
    tile = pl.BlockSpec((1, tm, d), lambda bi, i: (bi, i, 0))
    k_spec = pl.BlockSpec((1, tm, d), lambda bi, i: (bi, kb + i, 0))
    vt_spec = pl.BlockSpec((1, N_HEADS, V_ROWS, tm), lambda bi, i: (bi, 0, 0, kb + i))
    q_spec = pl.BlockSpec((1, N_HEADS, 1, V_DIM, 2 * tm), lambda bi, i: (bi, 0, i, 0, 0))
    q_shape = jax.ShapeDtypeStruct((b, N_HEADS, l // tm, V_DIM, 2 * tm), F8)
    k_shape = jax.ShapeDtypeStruct((b, n_keys, d), F8)
    vt_shape = jax.ShapeDtypeStruct((b, N_HEADS, V_ROWS, n_keys), F8)
    in_specs = [tile,
                pl.BlockSpec((1, 1, d), row), pl.BlockSpec((1, 1, d), row),
                pl.BlockSpec((1, d), fix),
                pl.BlockSpec((d, n_out * d), fix),
                pl.BlockSpec((4, LANES), fix),
                pl.BlockSpec((tm, LANES), lambda bi, i: (i, 0)),
                pl.BlockSpec((tm, LANES), lambda bi, i: (i, 0))]
    args = [x, shift, scale, gain, w, gains, cos, sin]
    aliases = {}
    if kv is not None:
        aliases = {len(args): n_qk - 1, len(args) + 1: n_qk}
        in_specs += [pl.BlockSpec(memory_space=pl.ANY)] * 2
        args += list(kv)
    return pl.pallas_call(
        kern,
        grid=(b, l // tm),
        in_specs=in_specs,
        out_specs=[q_spec] * (n_qk - 1) + [k_spec, vt_spec],
        out_shape=[q_shape] * (n_qk - 1) + [k_shape, vt_shape],
        input_output_aliases=aliases,
        compiler_params=_params("arbitrary", "arbitrary"),
        name="qkv_proj",
    )(*args)


def _rope_tables(n_tokens):
    rows = n_tokens // GRID_W
    row = np.repeat(np.arange(rows, dtype=np.float32), GRID_W)
    col = np.tile(np.arange(GRID_W, dtype=np.float32), rows)
    half = HEAD_DIM // 2
    inv_freq = (np.float32(ROPE_THETA) ** (-np.arange(0, half, 2, dtype=np.float32) / half)).astype(np.float32)
    ang_r = row[:, None] * inv_freq
    ang_c = col[:, None] * inv_freq
    ang = np.concatenate([ang_r, ang_r, ang_c, ang_c] * 2, axis=-1)
    sign = np.where((np.arange(LANES) & 16) == 0, -1.0, 1.0).astype(np.float32)
    return jnp.asarray(np.cos(ang), F32), jnp.asarray(np.sin(ang) * sign, F32)


def _attn_kernel(q_ref, k_ref, vt_ref, lp_ref, sg_ref, o_ref, s_ref, *, tk, group, ahead, lam_init):
    tq = q_ref.shape[4] // 2
    n_chunks = k_ref.shape[1] // tk
    qz = q_ref[0, 0, 0]
    n_slots = (ahead + 1) * group

    def score_chunk(c, m_grp):
        st = jnp.dot(k_ref[0, c * tk:(c + 1) * tk, :], qz, preferred_element_type=F32).astype(BF16)
        slot = c % n_slots
        s_ref[slot * tk:(slot + 1) * tk, :] = st
        mc = jnp.max(st, axis=0, keepdims=True)
        return mc if m_grp is None else jnp.maximum(m_grp, mc)

    def value_chunk(c, m_ref, part):
        slot = c % n_slots
        p = jnp.exp2(s_ref[slot * tk:(slot + 1) * tk, :] - (m_ref - P_SHIFT)).astype(F8)
        pv = jnp.dot(vt_ref[0, 0, :, c * tk:(c + 1) * tk], p, preferred_element_type=F32)
        return pv if part is None else part + pv

    groups = [list(range(g0, min(g0 + group, n_chunks))) for g0 in range(0, n_chunks, group)]
    m_of = {}
    for g in range(min(ahead, len(groups))):
        for c in groups[g]:
            m_of[g] = score_chunk(c, m_of.get(g))
    m = None
    acc = None
    for gi, cur in enumerate(groups):
        nxt = groups[gi + ahead] if gi + ahead < len(groups) else []
        m_new = m_of[gi] if m is None else jnp.maximum(m, m_of[gi])
        part = None
        for i in range(max(len(cur), len(nxt))):
            if i < len(nxt):
                m_of[gi + ahead] = score_chunk(nxt[i], m_of.get(gi + ahead))
            if i < len(cur):
                part = value_chunk(cur[i], m_new, part)
        acc = part if acc is None else acc * jnp.exp2(m.astype(F32) - m_new.astype(F32)) + part
        m = m_new
    acc = acc[:V_DIM] / acc[V_DIM:V_DIM + 1]
    lp = lp_ref[...]
    lam = (jnp.exp(jnp.sum(lp[0:1] * lp[1:2], axis=-1, keepdims=True))
           - jnp.exp(jnp.sum(lp[2:3] * lp[3:4], axis=-1, keepdims=True)) + lam_init)
    o = (acc[:, :tq] - lam * acc[:, tq:]) * lp[4:5, 0:1]
    o = o * lax.rsqrt(jnp.mean(o * o, axis=0, keepdims=True) + NORM_EPS) * sg_ref[...] * (1.0 - lam_init)
    o_ref[0] = o.T.astype(BF16)


def _attn_chunk(n_keys):
    for tk in (768, 512, 256, 128):
        if n_keys % tk == 0:
            return tk
    raise ValueError(f"key count {n_keys} is not a multiple of {LANES}")


def _diff_attention(qz, k_all, vt_all, lam_params, sub_gain, *, lam_init):
    b, _, n_tiles, _, tq2 = qz.shape
    tq = tq2 // 2
    l, d = n_tiles * tq, N_HEADS * V_DIM
    n_keys = k_all.shape[1]
    tk = _attn_chunk(n_keys)
    kern = functools.partial(_attn_kernel, tk=tk, group=ATTN_GROUP_CHUNKS, ahead=ATTN_SCORE_AHEAD, lam_init=lam_init)
    return pl.pallas_call(
        kern,
        grid=(b, N_HEADS, l // tq),
        in_specs=[pl.BlockSpec((1, 1, 1, V_DIM, 2 * tq), lambda bi, h, i: (bi, h, i, 0, 0)),
                  pl.BlockSpec((1, n_keys, V_DIM), lambda bi, h, i: (bi, 0, h)),
                  pl.BlockSpec((1, 1, V_ROWS, n_keys), lambda bi, h, i: (bi, h, 0, 0)),
                  pl.BlockSpec((5, HEAD_DIM), lambda bi, h, i: (0, 0)),
                  pl.BlockSpec((V_DIM, 1), lambda bi, h, i: (0, 0))],
        out_specs=pl.BlockSpec((1, tq, V_DIM), lambda bi, h, i: (bi, i, h)),
        out_shape=jax.ShapeDtypeStruct((b, l, d), BF16),
        scratch_shapes=[pltpu.VMEM(((ATTN_SCORE_AHEAD + 1) * ATTN_GROUP_CHUNKS * tk, 2 * tq), BF16)],
        compiler_params=_params("arbitrary", "arbitrary", "arbitrary"),
        name="diff_attention",
    )(qz, k_all, vt_all, lam_params, sub_gain)


def _route(h2, rwh_ref, rwl_ref, rb_ref, carry_ref, ridx_ref, gcol_ref, cnt_ref, is_first):
    tm = h2.shape[0]
    hh = h2.astype(BF16)
    hl = (h2 - hh.astype(F32)).astype(BF16)
    rw2 = jnp.concatenate([rwh_ref[...], rwl_ref[...]], axis=0)
    part = lax.dot_general(rw2, hh, NT_DIMS, preferred_element_type=F32)
    logits = (part[:N_EXPERTS] + part[N_EXPERTS:]
              + lax.dot_general(rwh_ref[...], hl, NT_DIMS, preferred_element_type=F32) + rb_ref[...])
    groups = [logits[g * EXPERTS_PER_GROUP:(g + 1) * EXPERTS_PER_GROUP] for g in range(N_EXPERT_GROUPS)]
    top = groups[0]
    for g in range(1, N_EXPERT_GROUPS):
        top = jnp.maximum(top, groups[g])
    top = jnp.max(top, axis=0, keepdims=True)
    sub = lax.broadcasted_iota(jnp.int32, (EXPERTS_PER_GROUP, tm), 0)
    best = None
    for g in range(N_EXPERT_GROUPS):
        ex = jnp.exp(groups[g] - top)
        v1 = jnp.max(ex, axis=0, keepdims=True)
        i1 = jnp.min(jnp.where(ex == v1, sub, EXPERTS_PER_GROUP), axis=0, keepdims=True)
        rest = jnp.where(sub == i1, -1.0, ex)
        v2 = jnp.max(rest, axis=0, keepdims=True)
        i2 = jnp.min(jnp.where(rest == v2, sub, EXPERTS_PER_GROUP), axis=0, keepdims=True)
        cand = (v1 + v2, v1, v2, i1 + g * EXPERTS_PER_GROUP, i2 + g * EXPERTS_PER_GROUP)
        if best is None:
            best = cand
        else:
            better = cand[0] > best[0]
            best = tuple(jnp.where(better, new, old) for new, old in zip(cand, best))
    _, v1, v2, e0, e1 = best
    gate0 = v1 / (v1 + v2)
    gate1 = v2 / (v1 + v2)

    @pl.when(is_first)
    def _():
        carry_ref[...] = jnp.zeros_like(carry_ref)

    erow = lax.broadcasted_iota(jnp.int32, (N_EXPERTS, tm), 0)
    oh0 = erow == e0
    oh1 = erow == e1
    chosen = jnp.where(oh0 | oh1, 1.0, 0.0)
    before = (lax.broadcasted_iota(jnp.int32, (tm, tm), 0)
              < lax.broadcasted_iota(jnp.int32, (tm, tm), 1)).astype(BF16)
    prior = carry_ref[:, 0:1] + jnp.dot(chosen.astype(BF16), before, preferred_element_type=F32)
    r0 = jnp.sum(jnp.where(oh0, prior, 0.0), axis=0, keepdims=True).astype(jnp.int32)
    r1 = jnp.sum(jnp.where(oh1, prior, 0.0), axis=0, keepdims=True).astype(jnp.int32)
    carry_ref[...] = carry_ref[...] + jnp.sum(chosen, axis=1, keepdims=True)
    cnt_ref[...] = carry_ref[...]
    rid = lax.broadcasted_iota(jnp.int32, (SUBLANES, tm), 0)
    ridx_ref[...] = jnp.where(rid == 0, e0, jnp.where(rid == 1, e1, jnp.where(rid == 2, r0, jnp.where(rid == 3, r1, 0))))
    gid = lax.broadcasted_iota(jnp.int32, (LANES, tm), 0)
    gcol_ref[...] = jnp.where(gid == 0, gate0, jnp.where(gid == 1, gate1, 0.0)).T


def _tail(y, x_ref, g1_ref, sh2_ref, s2_ref, n2_ref, rwh_ref, rwl_ref, rb_ref,
          x_out, h2_out, ridx_ref, gcol_ref, cnt_ref, carry_ref):
    x1 = x_ref[0] + g1_ref[0] * y
    x_out[0] = x1
    h2 = _rms_mod(x1, n2_ref[...], sh2_ref[0], s2_ref[0])
    h2_out[0] = _pack_bf16_pairs(h2)
    is_first = (pl.program_id(0) == 0) & (pl.program_id(1) == 0)
    _route(h2, rwh_ref, rwl_ref, rb_ref, carry_ref, ridx_ref, gcol_ref, cnt_ref, is_first)


def _attn_tail_kernel(a_ref, wo_ref, *rest):
    y = jnp.dot(a_ref[0], wo_ref[...], preferred_element_type=F32)
    _tail(y, *rest)


def _pool_tail_kernel(u_ref, up_ref, un_ref, wg_ref, cs_ref, wo_ref, *rest, seq_len):
    *tail_refs, ubuf = rest
    tm = u_ref.shape[1]
    i = pl.program_id(1)
    u = u_ref[0]
    ubuf[0:POOL_HALO] = jnp.where(i > 0, up_ref[0], 0.0)
    ubuf[POOL_HALO:POOL_HALO + tm] = u
    ubuf[POOL_HALO + tm:2 * POOL_HALO + tm] = jnp.where(i < pl.num_programs(1) - 1, un_ref[0], 0.0)
    pos = i * tm + lax.broadcasted_iota(jnp.int32, (tm, 1), 0)
    gd = wg_ref.shape[1]
    outs = []
    for g, win in enumerate(POOL_WINDOWS):
        half = win // 2
        cols = slice(g * gd, (g + 1) * gd)
        s = ubuf[POOL_HALO - half:POOL_HALO - half + tm, cols]
        for j in range(1 - half, half):
            s = s + ubuf[POOL_HALO + j:POOL_HALO + j + tm, cols]
        cnt = (jnp.minimum(pos + half, seq_len) - jnp.maximum(pos - half, 0)).astype(F32)
        dlt = (s / cnt - u[:, cols]).astype(BF16)
        outs.append(jnp.dot(dlt, wg_ref[g], preferred_element_type=F32))
    z = (jnp.concatenate(outs, axis=-1) * cs_ref[...]).astype(BF16)
    y = jnp.dot(z, wo_ref[...], preferred_element_type=F32)
    _tail(y, *tail_refs)


def _mixer_tail(front_args, front_specs, kern, x, g1, sh2, s2, n2g, rwh, rwl, rb, *, tm, scratch=()):
    b, l, d = x.shape
    nt = l // tm
    n = b * l
    row = lambda bi, i: (bi, 0, 0)
    fix = lambda bi, i: (0, 0)
    tile = lambda bi, i: (bi, i, 0)
    in_specs = list(front_specs) + [
        pl.BlockSpec((1, tm, d), tile),
        pl.BlockSpec((1, 1, d), row), pl.BlockSpec((1, 1, d), row), pl.BlockSpec((1, 1, d), row),
        pl.BlockSpec((1, d), fix),
        pl.BlockSpec((N_EXPERTS, d), fix), pl.BlockSpec((N_EXPERTS, d), fix),
        pl.BlockSpec((N_EXPERTS, 1), fix)]
    out_specs = [pl.BlockSpec((1, tm, d), tile), pl.BlockSpec((1, tm, d // 2), tile),
                 pl.BlockSpec((SUBLANES, tm), lambda bi, i: (0, bi * nt + i)),
                 pl.BlockSpec((tm, LANES), lambda bi, i: (bi * nt + i, 0)),
                 pl.BlockSpec((N_EXPERTS, LANES), fix)]
    out_shape = [jax.ShapeDtypeStruct((b, l, d), F32), jax.ShapeDtypeStruct((b, l, d // 2), jnp.int32),
                 jax.ShapeDtypeStruct((SUBLANES, n), jnp.int32), jax.ShapeDtypeStruct((n, LANES), F32),
                 jax.ShapeDtypeStruct((N_EXPERTS, LANES), F32)]
    return pl.pallas_call(
        kern,
        grid=(b, nt),
        in_specs=in_specs, out_specs=out_specs, out_shape=out_shape,
        scratch_shapes=[pltpu.VMEM((N_EXPERTS, LANES), F32)] + list(scratch),
        compiler_params=_params("arbitrary", "arbitrary"),
        name="mixer_tail",
    )(*front_args, x, g1, sh2, s2, n2g, rwh, rwl, rb)


def _slot_kernel(ps_ref, ridx_ref, dest_ref):
    ridx = ridx_ref[...]
    ps = ps_ref[...]
    erow = lax.broadcasted_iota(jnp.int32, (N_EXPERTS, ridx.shape[1]), 0)
    rows = []
    for k in range(TOP_K):
        start = jnp.sum(jnp.where(erow == ridx[k:k + 1], ps, 0), axis=0, keepdims=True)
        rows.append(start + ridx[TOP_K + k:TOP_K + k + 1])
    rid = lax.broadcasted_iota(jnp.int32, ridx.shape, 0)
    dest_ref[...] = jnp.where(rid == 0, rows[0], jnp.where(rid == 1, rows[1], 0))


def _slot_index(pad_start, ridx, *, tn):
    n = ridx.shape[1]
    return pl.pallas_call(
        _slot_kernel,
        grid=(n // tn,),
        in_specs=[pl.BlockSpec((N_EXPERTS, 1), lambda i: (0, 0)),
                  pl.BlockSpec((SUBLANES, tn), lambda i: (0, i))],
        out_specs=pl.BlockSpec((SUBLANES, tn), lambda i: (0, i)),
        out_shape=jax.ShapeDtypeStruct((SUBLANES, n), jnp.int32),
        compiler_params=_params("arbitrary"),
        name="slot_index",
    )(pad_start.reshape(N_EXPERTS, 1), ridx)


def _sc_mesh():
    return plsc.VectorSubcoreMesh(core_axis_name="c", subcore_axis_name="s",
                                  num_cores=SC_CORES, num_subcores=SC_SUBCORES)


def _sc_worker_base(per_worker):
    return (lax.axis_index("s") * SC_CORES + lax.axis_index("c")) * per_worker


def _sc_scatter_rows(rows, idx0, idx1, n_slots):
    n, d = rows.shape
    per_worker = n // SC_WORKERS
    assert per_worker % SC_WINDOW == 0

    n_win = per_worker // SC_WINDOW
    assert n_win % 2 == 0

    def body(rows_hbm, i0_hbm, i1_hbm, out_hbm, i0_a, i1_a, rows_a, i0_b, i1_b, rows_b, sem_a, sem_b):
        base = _sc_worker_base(per_worker)

        def offset(j):
            return pl.multiple_of(base + j * SC_WINDOW, SC_WINDOW)

        def start(j, i0_v, i1_v, rows_v, sem):
            pltpu.sync_copy(i0_hbm.at[pl.ds(offset(j), SC_WINDOW)], i0_v)
            pltpu.sync_copy(i1_hbm.at[pl.ds(offset(j), SC_WINDOW)], i1_v)
            pltpu.async_copy(rows_hbm.at[pl.ds(offset(j), SC_WINDOW)], rows_v, sem)

        def finish(j, i0_v, i1_v, rows_v, sem):
            pltpu.make_async_copy(rows_hbm.at[pl.ds(offset(j), SC_WINDOW)], rows_v, sem).wait()
            pltpu.sync_copy(rows_v, out_hbm.at[i0_v])
            pltpu.sync_copy(rows_v, out_hbm.at[i1_v])

        start(0, i0_a, i1_a, rows_a, sem_a)

        @pl.loop(0, n_win, step=2)
        def _(j):
            start(j + 1, i0_b, i1_b, rows_b, sem_b)
            finish(j, i0_a, i1_a, rows_a, sem_a)

            @pl.when(j + 2 < n_win)
            def _():
                start(j + 2, i0_a, i1_a, rows_a, sem_a)

            finish(j + 1, i0_b, i1_b, rows_b, sem_b)

    window = [pltpu.VMEM((SC_WINDOW,), jnp.int32), pltpu.VMEM((SC_WINDOW,), jnp.int32),
              pltpu.VMEM((SC_WINDOW, d), rows.dtype)]
    return pl.kernel(
        body, out_type=jax.ShapeDtypeStruct((n_slots, d), rows.dtype), mesh=_sc_mesh(),
        scratch_types=window + window + [pltpu.SemaphoreType.DMA, pltpu.SemaphoreType.DMA],
        name="sc_scatter_rows",
    )(rows, idx0, idx1)


def _sc_gather_rows(table, idx):
    n = idx.shape[0]
    d = table.shape[1]
    per_worker = n // SC_WORKERS
    assert per_worker % SC_WINDOW == 0

    n_win = per_worker // SC_WINDOW
    assert n_win % 2 == 0

    def body(table_hbm, idx_hbm, out_hbm, idx_a, idx_b, rows_a, rows_b, sem_a, sem_b):
        base = _sc_worker_base(per_worker)

        def offset(j):
            return pl.multiple_of(base + j * SC_WINDOW, SC_WINDOW)

        def start(j, idx_v, rows_v, sem):
            pltpu.sync_copy(idx_hbm.at[pl.ds(offset(j), SC_WINDOW)], idx_v)
            pltpu.async_copy(table_hbm.at[idx_v], rows_v, sem)

        def finish(j, idx_v, rows_v, sem):
            pltpu.make_async_copy(table_hbm.at[idx_v], rows_v, sem).wait()
            pltpu.sync_copy(rows_v, out_hbm.at[pl.ds(offset(j), SC_WINDOW)])

        start(0, idx_a, rows_a, sem_a)

        @pl.loop(0, n_win, step=2)
        def _(j):
            start(j + 1, idx_b, rows_b, sem_b)
            finish(j, idx_a, rows_a, sem_a)

            @pl.when(j + 2 < n_win)
            def _():
                start(j + 2, idx_a, rows_a, sem_a)

            finish(j + 1, idx_b, rows_b, sem_b)

    return pl.kernel(
        body, out_type=jax.ShapeDtypeStruct((n, d), table.dtype), mesh=_sc_mesh(),
        scratch_types=[pltpu.VMEM((SC_WINDOW,), jnp.int32), pltpu.VMEM((SC_WINDOW,), jnp.int32),
                       pltpu.VMEM((SC_WINDOW, d), table.dtype), pltpu.VMEM((SC_WINDOW, d), table.dtype),
                       pltpu.SemaphoreType.DMA, pltpu.SemaphoreType.DMA],
        name="sc_gather_rows",
    )(table, idx)


def _ffn_kernel(be_ref, nv_ref, xs_ref, wg_ref, wu_ref, wd_ref, ys_ref, wg_b, wu_b, wd_b):
    j = pl.program_id(0)
    valid = nv_ref[j]

    @pl.when((valid > 0) & ((j == 0) | (be_ref[j] != be_ref[jnp.maximum(j - 1, 0)])))
    def _():
        wg_b[...] = wg_ref[0, 0].astype(BF16)
        wu_b[...] = wu_ref[0, 0].astype(BF16)
        wd_b[...] = wd_ref[0, 0].astype(BF16)

    @pl.when(valid > 0)
    def _():
        row = lax.broadcasted_iota(jnp.int32, (SLOT_ROWS, 1), 0)
        xw = jnp.where(row < valid, xs_ref[...], 0)
        xb = _unpack_bf16_pairs(xw).astype(BF16)
        g = jnp.dot(xb, wg_b[...], preferred_element_type=F32)
        u = jnp.dot(xb, wu_b[...], preferred_element_type=F32)
        a = (g / (1.0 + jnp.exp(-g)) * u).astype(BF16)
        ys_ref[...] = _pack_bf16_pairs(jnp.dot(a, wd_b[...], preferred_element_type=F32))

    @pl.when(valid <= 0)
    def _():
        ys_ref[...] = jnp.zeros_like(ys_ref)


def _expert_ffn(block_e, n_valid, xs, w_gate, w_up, w_down, layer):
    n_slots, dw = xs.shape
    d, de = w_gate.shape[2:]
    w_idx = lambda j, be, nv: (layer, be[j], 0, 0)
    return pl.pallas_call(
        _ffn_kernel,
        grid_spec=pltpu.PrefetchScalarGridSpec(
            num_scalar_prefetch=2,
            grid=(n_slots // SLOT_ROWS,),
            in_specs=[pl.BlockSpec((SLOT_ROWS, dw), lambda j, be, nv: (j, 0)),
                      pl.BlockSpec((1, 1, d, de), w_idx),
                      pl.BlockSpec((1, 1, d, de), w_idx),
                      pl.BlockSpec((1, 1, de, d), w_idx)],
            out_specs=pl.BlockSpec((SLOT_ROWS, dw), lambda j, be, nv: (j, 0)),
            scratch_shapes=[pltpu.VMEM((d, de), BF16), pltpu.VMEM((d, de), BF16), pltpu.VMEM((de, d), BF16)]),
        out_shape=jax.ShapeDtypeStruct((n_slots, dw), jnp.int32),
        compiler_params=_params("arbitrary"),
        name="expert_ffn",
    )(block_e, n_valid, xs, w_gate, w_up, w_down)


def _combine_kernel(y0_ref, y1_ref, gcol_ref, x_ref, g2_ref, *rest, pool_in):
    if pool_in:
        sh_ref, sc_ref, n1_ref, wi_ref, x_out, u_out = rest
    else:
        (x_out,) = rest
    gc = gcol_ref[...]
    out = gc[:, 0:1] * _unpack_bf16_pairs(y0_ref[0]) + gc[:, 1:2] * _unpack_bf16_pairs(y1_ref[0])
    x2 = x_ref[0] + g2_ref[0] * out
    x_out[0] = x2
    if pool_in:
        hb = _rms_mod(x2, n1_ref[...], sh_ref[0], sc_ref[0]).astype(BF16)
        u_out[0] = jnp.dot(hb, wi_ref[...], preferred_element_type=F32)


def _combine(yg, gcol, x, g2, pool_args=None, *, tc):
    b, l, d = x.shape
    nt = l // tc
    pool_in = pool_args is not None
    kern = functools.partial(_combine_kernel, pool_in=pool_in)
    row = lambda bi, i: (bi, 0, 0)
    fix = lambda bi, i: (0, 0)
    tile = lambda bi, i: (bi, i, 0)
    in_specs = [pl.BlockSpec((1, tc, d // 2), lambda bi, i: (0, bi * nt + i, 0)),
                pl.BlockSpec((1, tc, d // 2), lambda bi, i: (1, bi * nt + i, 0)),
                pl.BlockSpec((tc, LANES), lambda bi, i: (bi * nt + i, 0)),
                pl.BlockSpec((1, tc, d), tile),
                pl.BlockSpec((1, 1, d), row)]
    out_specs = [pl.BlockSpec((1, tc, d), tile)]
    out_shape = [jax.ShapeDtypeStruct((b, l, d), F32)]
    args = [yg, yg, gcol, x, g2]
    if pool_in:
        in_specs += [pl.BlockSpec((1, 1, d), row), pl.BlockSpec((1, 1, d), row),
                     pl.BlockSpec((1, d), fix), pl.BlockSpec((d, d), fix)]
        out_specs.append(pl.BlockSpec((1, tc, d), tile))
        out_shape.append(jax.ShapeDtypeStruct((b, l, d), F32))
        args += list(pool_args)
    return pl.pallas_call(
        kern,
        grid=(b, nt),
        in_specs=in_specs, out_specs=out_specs, out_shape=out_shape,
        compiler_params=_params("arbitrary", "arbitrary"),
        name="moe_combine",
    )(*args)


def _moe(h2, ridx, gcol, counts, x1, g2, w_gate, w_up, w_down, layer, pool_args=None):
    b, l, d = x1.shape
    n = b * l
    n_blocks = (n * TOP_K) // SLOT_ROWS + N_EXPERTS
    cnt = counts[:, 0].astype(jnp.int32)
    padded = (cnt + SLOT_ROWS - 1) // SLOT_ROWS * SLOT_ROWS
    earlier = jnp.arange(N_EXPERTS)[None, :] < jnp.arange(N_EXPERTS)[:, None]
    pad_start = jnp.sum(jnp.where(earlier, padded[None, :], 0), axis=1).astype(jnp.int32)
    pad_end = pad_start + padded
    block_start = jnp.arange(n_blocks, dtype=jnp.int32) * SLOT_ROWS
    block_e = jnp.minimum(jnp.sum(pad_end[None, :] <= block_start[:, None], axis=1), N_EXPERTS - 1).astype(jnp.int32)
    own = block_e[:, None] == jnp.arange(N_EXPERTS)[None, :]
    data_end = jnp.sum(jnp.where(own, (pad_start + cnt)[None, :], 0), axis=1)
    n_valid = jnp.clip(data_end - block_start, 0, SLOT_ROWS).astype(jnp.int32)
    dest = _slot_index(pad_start, ridx, tn=2048)
    xs = _sc_scatter_rows(h2.reshape(n, d // 2), dest[0], dest[1], n_blocks * SLOT_ROWS)
    ys = _expert_ffn(block_e, n_valid, xs, w_gate, w_up, w_down, layer)
    yg = _sc_gather_rows(ys, dest[:TOP_K].reshape(TOP_K * n)).reshape(TOP_K, n, d // 2)
    return _combine(yg, gcol, x1, g2, pool_args, tc=512)


def kernel(x, c, ctx, c_ctx, ada_w, ada_b, norm1_g, norm2_g, attn_w_in, attn_w_out, attn_q_gain, attn_k_gain,
           attn_lq1, attn_lk1, attn_lq2, attn_lk2, attn_sub_gain, pool_w_in, pool_w_group, pool_scale, pool_w_out,
           router_w, router_b, moe_w_gate, moe_w_up, moe_w_down):
    b, l, d = x.shape
    n_ctx = ctx.shape[1]
    depth = ada_w.shape[0]
    assert depth == 2 and d == N_HEADS * V_DIM
    tm = 512

    mod = _adaln_mod(c, c_ctx, ada_w, ada_b)
    mods = [[mod[i, :b, None, j * d:(j + 1) * d] for j in range(N_MOD)] for i in range(depth)]
    mod_ctx = [jnp.broadcast_to(mod[0, b, j * d:(j + 1) * d], (b, 1, d)) for j in range(2)]

    rwt = router_w.T
    rwh = rwt.astype(BF16)
    rwl = (rwt - rwh.astype(F32)).astype(BF16)
    rb = router_b.reshape(N_EXPERTS, 1)

    sh1, s1, g1, sh2, s2, g2 = mods[0]
    cos, sin = _rope_tables(l)
    pair_up = (jnp.arange(LANES) & 16) == 0

    def gain_rows(g, factor):
        g2 = jnp.concatenate([g, g]) * factor
        return [g2, jnp.where(pair_up, jnp.roll(g2, -16), jnp.roll(g2, 16))]

    q_scale = HEAD_DIM ** -0.5 * math.log2(math.e)
    q_max = jnp.maximum(jnp.max(jnp.abs(attn_q_gain[0])) * (HEAD_DIM ** 0.5 * q_scale), F32_TINY)
    k_max = jnp.maximum(jnp.max(jnp.abs(attn_k_gain[0])) * HEAD_DIM ** 0.5, F32_TINY)
    need = jnp.ceil(jnp.log2(k_max / F8_MAX))
    room = jnp.floor(jnp.log2(F8_MAX / q_max))
    trade = jnp.exp2(jnp.clip(jnp.clip(0.0, need, jnp.maximum(need, room)), -60.0, 60.0))
    h_max = jnp.float32(0.0)
    for shift_, scale_ in ((sh1, s1), (mod_ctx[0], mod_ctx[1])):
        h_max = jnp.maximum(h_max, jnp.max(d ** 0.5 * jnp.max(jnp.abs(norm1_g[0] * (1.0 + scale_)), axis=-1)
                                           + jnp.sqrt(jnp.sum(shift_ * shift_, axis=-1))))
    w_v = attn_w_in[0][:, 2 * d:]
    v_max = 1.02 * h_max * jnp.sqrt(jnp.max(jnp.sum(w_v * w_v, axis=0)))
    v_grow = jnp.exp2(jnp.clip(jnp.ceil(jnp.log2(jnp.maximum(v_max, F32_TINY) / F8_MAX)), 0.0, 60.0))

    gains = jnp.stack(gain_rows(attn_q_gain[0], q_scale * trade) + gain_rows(attn_k_gain[0], 1.0 / trade))
    w_in = jnp.concatenate([attn_w_in[0][:, :2 * d], w_v / v_grow], axis=1).astype(BF16)
    assert l % tm == 0 and l % n_ctx == 0
    q, k_all, vt_all = _qkv_proj(x, sh1, s1, norm1_g[0][None], w_in, gains, cos, sin, n_qk=2, rope=True,
                                 tm=tm, n_keys=l + n_ctx, key_row0=0)
    k_all, vt_all = _qkv_proj(ctx, mod_ctx[0], mod_ctx[1], norm1_g[0][None], w_in[:, d:], gains,
                              cos[:n_ctx], sin[:n_ctx], n_qk=1, rope=False,
                              tm=n_ctx, n_keys=l + n_ctx, key_row0=l, kv=(k_all, vt_all))
    lam_init = 0.8 - 0.6 * math.exp(-0.3 * 0)
    lam_params = jnp.stack([attn_lq1[0], attn_lk1[0], attn_lq2[0], attn_lk2[0],
                            jnp.full((HEAD_DIM,), v_grow, F32)])
    o = _diff_attention(q, k_all, vt_all, lam_params, attn_sub_gain[0][:, None], lam_init=lam_init)

    fix = lambda bi, i: (0, 0)
    x1, h2, ridx, gcol, counts = _mixer_tail(
        (o, attn_w_out[0].astype(BF16)),
        (pl.BlockSpec((1, tm, d), lambda bi, i: (bi, i, 0)), pl.BlockSpec((d, d), fix)),
        _attn_tail_kernel, x, g1, sh2, s2, norm2_g[0][None], rwh, rwl, rb, tm=tm)

    sh1b, s1b, g1b, sh2b, s2b, g2b = mods[1]
    x2, u = _moe(h2, ridx, gcol, counts, x1, g2, moe_w_gate, moe_w_up, moe_w_down, 0,
                 pool_args=(sh1b, s1b, norm1_g[1][None], pool_w_in[0].astype(BF16)))
    gd = pool_w_group.shape[2]
    nh = tm // POOL_HALO
    front_specs = (
        pl.BlockSpec((1, tm, d), lambda bi, i: (bi, i, 0)),
        pl.BlockSpec((1, POOL_HALO, d), lambda bi, i: (bi, jnp.maximum(i * nh - 1, 0), 0)),
        pl.BlockSpec((1, POOL_HALO, d), lambda bi, i: (bi, jnp.minimum((i + 1) * nh, l // POOL_HALO - 1), 0)),
        pl.BlockSpec((len(POOL_WINDOWS), gd, gd), lambda bi, i: (0, 0, 0)),
        pl.BlockSpec((1, d), fix),
        pl.BlockSpec((d, d), fix))
    x3, h2b, ridx_b, gcol_b, counts_b = _mixer_tail(
        (u, u, u, pool_w_group[0].astype(BF16), pool_scale[0][None], pool_w_out[0].astype(BF16)),
        front_specs, functools.partial(_pool_tail_kernel, seq_len=l),
        x2, g1b, sh2b, s2b, norm2_g[1][None], rwh, rwl, rb, tm=tm,
        scratch=[pltpu.VMEM((tm + 2 * POOL_HALO, d), F32)])
    (out,) = _moe(h2b, ridx_b, gcol_b, counts_b, x3, g2b, moe_w_gate, moe_w_up, moe_w_down, 1)
    return out
```

```python
import functools
import math

import jax
import jax.numpy as jnp
import numpy as np
from jax import lax
from jax.experimental import pallas as pl
from jax.experimental.pallas import tpu as pltpu
from jax.experimental.pallas import tpu_sc as plsc

F32 = jnp.float32
BF16 = jnp.bfloat16
F8 = jnp.float8_e4m3fn
F8_MAX = float(jnp.finfo(F8).max)
F32_TINY = float(jnp.finfo(F32).tiny)

LANES = 128
SUBLANES = 8
N_HEADS = 8
HEAD_DIM = 64
V_DIM = 2 * HEAD_DIM
V_ROWS = V_DIM + 32
VT_KEY_ALIGN = 512
P_SHIFT = 8.0
GRID_W = 64
ROPE_THETA = 10000.0
NORM_EPS = 1e-6
N_MOD = 6
POOL_WINDOWS = (2, 4, 8, 16)
POOL_HALO = max(POOL_WINDOWS) // 2
N_EXPERTS = 32
N_EXPERT_GROUPS = 4
EXPERTS_PER_GROUP = N_EXPERTS // N_EXPERT_GROUPS
TOP_K = 2
SLOT_ROWS = 512
SC_CORES = 2
SC_SUBCORES = 16
SC_WORKERS = SC_CORES * SC_SUBCORES
SC_WINDOW = 64
ATTN_GROUP_CHUNKS = 1
ATTN_SCORE_AHEAD = 1
VMEM_LIMIT = 48 * 1024 * 1024
NT_DIMS = (((1,), (1,)), ((), ()))


def _params(*sem):
    return pltpu.CompilerParams(dimension_semantics=sem, vmem_limit_bytes=VMEM_LIMIT)


def _rms_mod(x, gain, shift, scale):
    h = x * lax.rsqrt(jnp.mean(x * x, axis=-1, keepdims=True) + NORM_EPS) * gain
    return h * (1.0 + scale) + shift


def _pack_bf16_pairs(x):
    c = x.shape[1] // 2
    hi = lax.bitcast_convert_type(x[:, :c].astype(BF16).astype(F32), jnp.uint32)
    lo = lax.bitcast_convert_type(x[:, c:].astype(BF16).astype(F32), jnp.uint32)
    return lax.bitcast_convert_type(hi | (lo >> 16), jnp.int32)


def _unpack_bf16_pairs(w):
    u = lax.bitcast_convert_type(w, jnp.uint32)
    hi = lax.bitcast_convert_type(u & jnp.uint32(0xFFFF0000), F32)
    lo = lax.bitcast_convert_type(u << 16, F32)
    return jnp.concatenate([hi, lo], axis=1)


def _mod_kernel(c_ref, w_ref, b_ref, o_ref):
    c = c_ref[...]
    a = c / (1.0 + jnp.exp(-c))
    o_ref[0] = jnp.dot(a, w_ref[0], precision=lax.Precision.HIGHEST,
                       preferred_element_type=F32) + b_ref[0]


def _adaln_mod(c, c_ctx, ada_w, ada_b):
    depth, d, n_out = ada_w.shape
    b = c.shape[0]
    assert b + 1 <= SUBLANES
    rows = jnp.concatenate([c, c_ctx[None], jnp.zeros((SUBLANES - b - 1, d), F32)], axis=0)
    tn = n_out // 4
    return pl.pallas_call(
        _mod_kernel,
        grid=(depth, n_out // tn),
        in_specs=[pl.BlockSpec((SUBLANES, d), lambda i, j: (0, 0)),
                  pl.BlockSpec((1, d, tn), lambda i, j: (i, 0, j)),
                  pl.BlockSpec((1, 1, tn), lambda i, j: (i, 0, j))],
        out_specs=pl.BlockSpec((1, SUBLANES, tn), lambda i, j: (i, 0, j)),
        out_shape=jax.ShapeDtypeStruct((depth, SUBLANES, n_out), F32),
        compiler_params=_params("arbitrary", "arbitrary"),
        name="adaln_mod",
    )(rows, ada_w, ada_b.reshape(depth, 1, n_out))


def _qkv_kernel(x_ref, sh_ref, sc_ref, g_ref, w_ref, gains_ref, cos_ref, sin_ref, *refs, n_qk, rope):
    out_refs = refs[-(n_qk + 1):]
    tm, d = x_ref.shape[1:]
    hb = _rms_mod(x_ref[0], g_ref[...], sh_ref[0], sc_ref[0]).astype(BF16)
    lane_b4 = (lax.broadcasted_iota(jnp.int32, (1, LANES), 1) & 16) == 0
    chunk_r = lax.broadcasted_iota(jnp.int32, (2 * LANES, 2 * LANES), 0) // HEAD_DIM
    chunk_c = lax.broadcasted_iota(jnp.int32, (2 * LANES, 2 * LANES), 1) // HEAD_DIM
    same_chunk = (chunk_r == chunk_c).astype(BF16)
    gains = gains_ref[...]
    for t in range(n_qk):
        r = 2 * (t + 2 - n_qk)
        if rope:
            cos_t = cos_ref[...] * gains[r:r + 1]
            sin_t = sin_ref[...] * gains[r + 1:r + 2]
        for j in range(0, d, 2 * LANES):
            acc = jnp.dot(hb, w_ref[:, t * d + j:t * d + j + 2 * LANES], preferred_element_type=F32)
            ssq = jnp.dot((acc * acc).astype(BF16), same_chunk, preferred_element_type=F32)
            nrm = acc * lax.rsqrt(ssq * (1.0 / HEAD_DIM) + NORM_EPS)
            for half in range(2):
                blk = nrm[:, half * LANES:(half + 1) * LANES]
                if rope:
                    rot = jnp.where(lane_b4, pltpu.roll(blk, LANES - 16, 1), pltpu.roll(blk, 16, 1))
                    y = blk * cos_t + rot * sin_t
                else:
                    y = blk * gains[r:r + 1]
                c0 = j + half * LANES
                if t < n_qk - 1:
                    yt = y.T
                    first = lax.broadcasted_iota(jnp.int32, (V_DIM, 1), 0) < HEAD_DIM
                    out_refs[t][0, c0 // LANES, 0, :, 0:tm] = jnp.where(first, yt, 0.0).astype(F8)
                    out_refs[t][0, c0 // LANES, 0, :, tm:2 * tm] = jnp.where(first, 0.0, yt).astype(F8)
                else:
                    out_refs[t][0, :, c0:c0 + LANES] = y.astype(F8)
    vt_ref = out_refs[n_qk]
    ones = jnp.where(lax.broadcasted_iota(jnp.int32, (V_ROWS - V_DIM, tm), 0) == 0, 1.0, 0.0).astype(F8)
    for j in range(0, d, 2 * LANES):
        acc = jnp.dot(hb, w_ref[:, n_qk * d + j:n_qk * d + j + 2 * LANES], preferred_element_type=F32)
        acc_t = acc.T
        for half in range(2):
            h = j // LANES + half
            vt_ref[0, h, 0:V_DIM, :] = acc_t[half * V_DIM:(half + 1) * V_DIM].astype(F8)
            vt_ref[0, h, V_DIM:V_ROWS, :] = ones


def _qkv_proj(x, shift, scale, gain, w, gains, cos, sin, *, n_qk, rope, tm, n_keys, key_row0, kv=None):
    b, l, d = x.shape
    n_out = n_qk + 1
    kern = functools.partial(_qkv_kernel, n_qk=n_qk, rope=rope)
    row = lambda bi, i: (bi, 0, 0)
    fix = lambda bi, i: (0, 0)
    kb = key_row0 // tm
    tile = pl.BlockSpec((1, tm, d), lambda bi, i: (bi, i, 0))
    k_spec = pl.BlockSpec((1, tm, d), lambda bi, i: (bi, kb + i, 0))
    vt_spec = pl.BlockSpec((1, N_HEADS, V_ROWS, tm), lambda bi, i: (bi, 0, 0, kb + i))
    q_spec = pl.BlockSpec((1, N_HEADS, 1, V_DIM, 2 * tm), lambda bi, i: (bi, 0, i, 0, 0))
    q_shape = jax.ShapeDtypeStruct((b, N_HEADS, l // tm, V_DIM, 2 * tm), F8)
    k_shape = jax.ShapeDtypeStruct((b, n_keys, d), F8)
    vt_shape = jax.ShapeDtypeStruct((b, N_HEADS, V_ROWS, -(-n_keys // VT_KEY_ALIGN) * VT_KEY_ALIGN), F8)
    in_specs = [tile,
                pl.BlockSpec((1, 1, d), row), pl.BlockSpec((1, 1, d), row),
                pl.BlockSpec((1, d), fix),
                pl.BlockSpec((d, n_out * d), fix),
                pl.BlockSpec((4, LANES), fix),
                pl.BlockSpec((tm, LANES), lambda bi, i: (i, 0)),
                pl.BlockSpec((tm, LANES), lambda bi, i: (i, 0))]
    args = [x, shift, scale, gain, w, gains, cos, sin]
    aliases = {}
    if kv is not None:
        aliases = {len(args): n_qk - 1, len(args) + 1: n_qk}
        in_specs += [pl.BlockSpec(memory_space=pl.ANY)] * 2
        args += list(kv)
    return pl.pallas_call(
        kern,
        grid=(b, l // tm),
        in_specs=in_specs,
        out_specs=[q_spec] * (n_qk - 1) + [k_spec, vt_spec],
        out_shape=[q_shape] * (n_qk - 1) + [k_shape, vt_shape],
        input_output_aliases=aliases,
        compiler_params=_params("arbitrary", "arbitrary"),
        name="qkv_proj",
    )(*args)


def _rope_tables(n_tokens):
    rows = n_tokens // GRID_W
    row = np.repeat(np.arange(rows, dtype=np.float32), GRID_W)
    col = np.tile(np.arange(GRID_W, dtype=np.float32), rows)
    half = HEAD_DIM // 2
    inv_freq = (np.float32(ROPE_THETA) ** (-np.arange(0, half, 2, dtype=np.float32) / half)).astype(np.float32)
    ang_r = row[:, None] * inv_freq
    ang_c = col[:, None] * inv_freq
    ang = np.concatenate([ang_r, ang_r, ang_c, ang_c] * 2, axis=-1)
    sign = np.where((np.arange(LANES) & 16) == 0, -1.0, 1.0).astype(np.float32)
    return jnp.asarray(np.cos(ang), F32), jnp.asarray(np.sin(ang) * sign, F32)


def _attn_kernel(q_ref, k_ref, vt_ref, lp_ref, sg_ref, o_ref, s_ref, *, tk, group, ahead, lam_init):
    tq = q_ref.shape[4] // 2
    n_chunks = k_ref.shape[1] // tk
    qz = q_ref[0, 0, 0]
    n_slots = (ahead + 1) * group

    def score_chunk(c, m_grp):
        st = jnp.dot(k_ref[0, c * tk:(c + 1) * tk, :], qz, preferred_element_type=F32).astype(BF16)
        slot = c % n_slots
        s_ref[slot * tk:(slot + 1) * tk, :] = st
        mc = jnp.max(st, axis=0, keepdims=True)
        return mc if m_grp is None else jnp.maximum(m_grp, mc)

    def value_chunk(c, m_ref, part):
        slot = c % n_slots
        p = jnp.exp2(s_ref[slot * tk:(slot + 1) * tk, :] - (m_ref - P_SHIFT)).astype(F8)
        pv = jnp.dot(vt_ref[0, 0, :, c * tk:(c + 1) * tk], p, preferred_element_type=F32)
        return pv if part is None else part + pv

    groups = [list(range(g0, min(g0 + group, n_chunks))) for g0 in range(0, n_chunks, group)]
    m_of = {}
    for g in range(min(ahead, len(groups))):
        for c in groups[g]:
            m_of[g] = score_chunk(c, m_of.get(g))
    m = None
    acc = None
    for gi, cur in enumerate(groups):
        nxt = groups[gi + ahead] if gi + ahead < len(groups) else []
        m_new = m_of[gi] if m is None else jnp.maximum(m, m_of[gi])
        part = None
        for i in range(max(len(cur), len(nxt))):
            if i < len(nxt):
                m_of[gi + ahead] = score_chunk(nxt[i], m_of.get(gi + ahead))
            if i < len(cur):
                part = value_chunk(cur[i], m_new, part)
        acc = part if acc is None else acc * jnp.exp2(m.astype(F32) - m_new.astype(F32)) + part
        m = m_new
    acc = acc[:V_DIM] / acc[V_DIM:V_DIM + 1]
    lp = lp_ref[...]
    lam = (jnp.exp(jnp.sum(lp[0:1] * lp[1:2], axis=-1, keepdims=True))
           - jnp.exp(jnp.sum(lp[2:3] * lp[3:4], axis=-1, keepdims=True)) + lam_init)
    o = (acc[:, :tq] - lam * acc[:, tq:]) * lp[4:5, 0:1]
    o = o * lax.rsqrt(jnp.mean(o * o, axis=0, keepdims=True) + NORM_EPS) * sg_ref[...] * (1.0 - lam_init)
    o_ref[0] = o.T.astype(BF16)


def _attn_chunk(n_keys):
    for tk in (768, 512, 256, 128):
        if n_keys % tk == 0:
            return tk
    raise ValueError(f"key count {n_keys} is not a multiple of {LANES}")


def _diff_attention(qz, k_all, vt_all, lam_params, sub_gain, *, lam_init):
    b, _, n_tiles, _, tq2 = qz.shape
    tq = tq2 // 2
    l, d = n_tiles * tq, N_HEADS * V_DIM
    n_keys = k_all.shape[1]
    tk = _attn_chunk(n_keys)
    kern = functools.partial(_attn_kernel, tk=tk, group=ATTN_GROUP_CHUNKS, ahead=ATTN_SCORE_AHEAD, lam_init=lam_init)
    return pl.pallas_call(
        kern,
        grid=(b, N_HEADS, l // tq),
        in_specs=[pl.BlockSpec((1, 1, 1, V_DIM, 2 * tq), lambda bi, h, i: (bi, h, i, 0, 0)),
                  pl.BlockSpec((1, n_keys, V_DIM), lambda bi, h, i: (bi, 0, h)),
                  pl.BlockSpec((1, 1, V_ROWS, vt_all.shape[3]), lambda bi, h, i: (bi, h, 0, 0)),
                  pl.BlockSpec((5, HEAD_DIM), lambda bi, h, i: (0, 0)),
                  pl.BlockSpec((V_DIM, 1), lambda bi, h, i: (0, 0))],
        out_specs=pl.BlockSpec((1, tq, V_DIM), lambda bi, h, i: (bi, i, h)),
        out_shape=jax.ShapeDtypeStruct((b, l, d), BF16),
        scratch_shapes=[pltpu.VMEM(((ATTN_SCORE_AHEAD + 1) * ATTN_GROUP_CHUNKS * tk, 2 * tq), BF16)],
        compiler_params=_params("arbitrary", "arbitrary", "arbitrary"),
        name="diff_attention",
    )(qz, k_all, vt_all, lam_params, sub_gain)


def _route(h2, rwh_ref, rwl_ref, rb_ref, carry_ref, ridx_ref, gcol_ref, cnt_ref, is_first):
    tm = h2.shape[0]
    hh = h2.astype(BF16)
    hl = (h2 - hh.astype(F32)).astype(BF16)
    rw2 = jnp.concatenate([rwh_ref[...], rwl_ref[...]], axis=0)
    part = lax.dot_general(rw2, hh, NT_DIMS, preferred_element_type=F32)
    logits = (part[:N_EXPERTS] + part[N_EXPERTS:]
              + lax.dot_general(rwh_ref[...], hl, NT_DIMS, preferred_element_type=F32) + rb_ref[...])
    groups = [logits[g * EXPERTS_PER_GROUP:(g + 1) * EXPERTS_PER_GROUP] for g in range(N_EXPERT_GROUPS)]
    top = groups[0]
    for g in range(1, N_EXPERT_GROUPS):
        top = jnp.maximum(top, groups[g])
    top = jnp.max(top, axis=0, keepdims=True)
    sub = lax.broadcasted_iota(jnp.int32, (EXPERTS_PER_GROUP, tm), 0)
    best = None
    for g in range(N_EXPERT_GROUPS):
        ex = jnp.exp(groups[g] - top)
        v1 = jnp.max(ex, axis=0, keepdims=True)
        i1 = jnp.min(jnp.where(ex == v1, sub, EXPERTS_PER_GROUP), axis=0, keepdims=True)
        rest = jnp.where(sub == i1, -1.0, ex)
        v2 = jnp.max(rest, axis=0, keepdims=True)
        i2 = jnp.min(jnp.where(rest == v2, sub, EXPERTS_PER_GROUP), axis=0, keepdims=True)
        cand = (v1 + v2, v1, v2, i1 + g * EXPERTS_PER_GROUP, i2 + g * EXPERTS_PER_GROUP)
        if best is None:
            best = cand
        else:
            better = cand[0] > best[0]
            best = tuple(jnp.where(better, new, old) for new, old in zip(cand, best))
    _, v1, v2, e0, e1 = best
    gate0 = v1 / (v1 + v2)
    gate1 = v2 / (v1 + v2)

    @pl.when(is_first)
    def _():
        carry_ref[...] = jnp.zeros_like(carry_ref)

    erow = lax.broadcasted_iota(jnp.int32, (N_EXPERTS, tm), 0)
    oh0 = erow == e0
    oh1 = erow == e1
    chosen = jnp.where(oh0 | oh1, 1.0, 0.0)
    before = (lax.broadcasted_iota(jnp.int32, (tm, tm), 0)
              < lax.broadcasted_iota(jnp.int32, (tm, tm), 1)).astype(BF16)
    prior = carry_ref[:, 0:1] + jnp.dot(chosen.astype(BF16), before, preferred_element_type=F32)
    r0 = jnp.sum(jnp.where(oh0, prior, 0.0), axis=0, keepdims=True).astype(jnp.int32)
    r1 = jnp.sum(jnp.where(oh1, prior, 0.0), axis=0, keepdims=True).astype(jnp.int32)
    carry_ref[...] = carry_ref[...] + jnp.sum(chosen, axis=1, keepdims=True)
    cnt_ref[...] = carry_ref[...]
    rid = lax.broadcasted_iota(jnp.int32, (SUBLANES, tm), 0)
    ridx_ref[...] = jnp.where(rid == 0, e0, jnp.where(rid == 1, e1, jnp.where(rid == 2, r0, jnp.where(rid == 3, r1, 0))))
    gid = lax.broadcasted_iota(jnp.int32, (LANES, tm), 0)
    gcol_ref[...] = jnp.where(gid == 0, gate0, jnp.where(gid == 1, gate1, 0.0)).T


def _tail(y, x_ref, g1_ref, sh2_ref, s2_ref, n2_ref, rwh_ref, rwl_ref, rb_ref,
          x_out, h2_out, ridx_ref, gcol_ref, cnt_ref, carry_ref):
    x1 = x_ref[0] + g1_ref[0] * y
    x_out[0] = x1
    h2 = _rms_mod(x1, n2_ref[...], sh2_ref[0], s2_ref[0])
    h2_out[0] = _pack_bf16_pairs(h2)
    is_first = (pl.program_id(0) == 0) & (pl.program_id(1) == 0)
    _route(h2, rwh_ref, rwl_ref, rb_ref, carry_ref, ridx_ref, gcol_ref, cnt_ref, is_first)


def _attn_tail_kernel(a_ref, wo_ref, *rest):
    y = jnp.dot(a_ref[0], wo_ref[...], preferred_element_type=F32)
    _tail(y, *rest)


def _pool_tail_kernel(u_ref, up_ref, un_ref, wg_ref, cs_ref, wo_ref, *rest, seq_len):
    *tail_refs, ubuf = rest
    tm = u_ref.shape[1]
    i = pl.program_id(1)
    u = u_ref[0]
    ubuf[0:POOL_HALO] = jnp.where(i > 0, up_ref[0], 0.0)
    ubuf[POOL_HALO:POOL_HALO + tm] = u
    ubuf[POOL_HALO + tm:2 * POOL_HALO + tm] = jnp.where(i < pl.num_programs(1) - 1, un_ref[0], 0.0)
    pos = i * tm + lax.broadcasted_iota(jnp.int32, (tm, 1), 0)
    gd = wg_ref.shape[1]
    outs = []
    for g, win in enumerate(POOL_WINDOWS):
        half = win // 2
        cols = slice(g * gd, (g + 1) * gd)
        s = ubuf[POOL_HALO - half:POOL_HALO - half + tm, cols]
        for j in range(1 - half, half):
            s = s + ubuf[POOL_HALO + j:POOL_HALO + j + tm, cols]
        cnt = (jnp.minimum(pos + half, seq_len) - jnp.maximum(pos - half, 0)).astype(F32)
        dlt = (s / cnt - u[:, cols]).astype(BF16)
        outs.append(jnp.dot(dlt, wg_ref[g], preferred_element_type=F32))
    z = (jnp.concatenate(outs, axis=-1) * cs_ref[...]).astype(BF16)
    y = jnp.dot(z, wo_ref[...], preferred_element_type=F32)
    _tail(y, *tail_refs)


def _mixer_tail(front_args, front_specs, kern, x, g1, sh2, s2, n2g, rwh, rwl, rb, *, tm, scratch=()):
    b, l, d = x.shape
    nt = l // tm
    n = b * l
    row = lambda bi, i: (bi, 0, 0)
    fix = lambda bi, i: (0, 0)
    tile = lambda bi, i: (bi, i, 0)
    in_specs = list(front_specs) + [
        pl.BlockSpec((1, tm, d), tile),
        pl.BlockSpec((1, 1, d), row), pl.BlockSpec((1, 1, d), row), pl.BlockSpec((1, 1, d), row),
        pl.BlockSpec((1, d), fix),
        pl.BlockSpec((N_EXPERTS, d), fix), pl.BlockSpec((N_EXPERTS, d), fix),
        pl.BlockSpec((N_EXPERTS, 1), fix)]
    out_specs = [pl.BlockSpec((1, tm, d), tile), pl.BlockSpec((1, tm, d // 2), tile),
                 pl.BlockSpec((SUBLANES, tm), lambda bi, i: (0, bi * nt + i)),
                 pl.BlockSpec((tm, LANES), lambda bi, i: (bi * nt + i, 0)),
                 pl.BlockSpec((N_EXPERTS, LANES), fix)]
    out_shape = [jax.ShapeDtypeStruct((b, l, d), F32), jax.ShapeDtypeStruct((b, l, d // 2), jnp.int32),
                 jax.ShapeDtypeStruct((SUBLANES, n), jnp.int32), jax.ShapeDtypeStruct((n, LANES), F32),
                 jax.ShapeDtypeStruct((N_EXPERTS, LANES), F32)]
    return pl.pallas_call(
        kern,
        grid=(b, nt),
        in_specs=in_specs, out_specs=out_specs, out_shape=out_shape,
        scratch_shapes=[pltpu.VMEM((N_EXPERTS, LANES), F32)] + list(scratch),
        compiler_params=_params("arbitrary", "arbitrary"),
        name="mixer_tail",
    )(*front_args, x, g1, sh2, s2, n2g, rwh, rwl, rb)


def _slot_kernel(ps_ref, ridx_ref, dest_ref):
    ridx = ridx_ref[...]
    ps = ps_ref[...]
    erow = lax.broadcasted_iota(jnp.int32, (N_EXPERTS, ridx.shape[1]), 0)
    rows = []
    for k in range(TOP_K):
        start = jnp.sum(jnp.where(erow == ridx[k:k + 1], ps, 0), axis=0, keepdims=True)
        rows.append(start + ridx[TOP_K + k:TOP_K + k + 1])
    rid = lax.broadcasted_iota(jnp.int32, ridx.shape, 0)
    dest_ref[...] = jnp.where(rid == 0, rows[0], jnp.where(rid == 1, rows[1], 0))


def _slot_index(pad_start, ridx, *, tn):
    n = ridx.shape[1]
    return pl.pallas_call(
        _slot_kernel,
        grid=(n // tn,),
        in_specs=[pl.BlockSpec((N_EXPERTS, 1), lambda i: (0, 0)),
                  pl.BlockSpec((SUBLANES, tn), lambda i: (0, i))],
        out_specs=pl.BlockSpec((SUBLANES, tn), lambda i: (0, i)),
        out_shape=jax.ShapeDtypeStruct((SUBLANES, n), jnp.int32),
        compiler_params=_params("arbitrary"),
        name="slot_index",
    )(pad_start.reshape(N_EXPERTS, 1), ridx)


def _sc_mesh():
    return plsc.VectorSubcoreMesh(core_axis_name="c", subcore_axis_name="s",
                                  num_cores=SC_CORES, num_subcores=SC_SUBCORES)


def _sc_worker_base(per_worker):
    return (lax.axis_index("s") * SC_CORES + lax.axis_index("c")) * per_worker


def _sc_scatter_rows(rows, idx0, idx1, n_slots):
    n, d = rows.shape
    per_worker = n // SC_WORKERS
    assert per_worker % SC_WINDOW == 0

    n_win = per_worker // SC_WINDOW
    assert n_win % 2 == 0

    def body(rows_hbm, i0_hbm, i1_hbm, out_hbm, i0_a, i1_a, rows_a, i0_b, i1_b, rows_b, sem_a, sem_b):
        base = _sc_worker_base(per_worker)

        def offset(j):
            return pl.multiple_of(base + j * SC_WINDOW, SC_WINDOW)

        def start(j, i0_v, i1_v, rows_v, sem):
            pltpu.sync_copy(i0_hbm.at[pl.ds(offset(j), SC_WINDOW)], i0_v)
            pltpu.sync_copy(i1_hbm.at[pl.ds(offset(j), SC_WINDOW)], i1_v)
            pltpu.async_copy(rows_hbm.at[pl.ds(offset(j), SC_WINDOW)], rows_v, sem)

        def finish(j, i0_v, i1_v, rows_v, sem):
            pltpu.make_async_copy(rows_hbm.at[pl.ds(offset(j), SC_WINDOW)], rows_v, sem).wait()
            pltpu.sync_copy(rows_v, out_hbm.at[i0_v])
            pltpu.sync_copy(rows_v, out_hbm.at[i1_v])

        start(0, i0_a, i1_a, rows_a, sem_a)

        @pl.loop(0, n_win, step=2)
        def _(j):
            start(j + 1, i0_b, i1_b, rows_b, sem_b)
            finish(j, i0_a, i1_a, rows_a, sem_a)

            @pl.when(j + 2 < n_win)
            def _():
                start(j + 2, i0_a, i1_a, rows_a, sem_a)

            finish(j + 1, i0_b, i1_b, rows_b, sem_b)

    window = [pltpu.VMEM((SC_WINDOW,), jnp.int32), pltpu.VMEM((SC_WINDOW,), jnp.int32),
              pltpu.VMEM((SC_WINDOW, d), rows.dtype)]
    return pl.kernel(
        body, out_type=jax.ShapeDtypeStruct((n_slots, d), rows.dtype), mesh=_sc_mesh(),
        scratch_types=window + window + [pltpu.SemaphoreType.DMA, pltpu.SemaphoreType.DMA],
        name="sc_scatter_rows",
    )(rows, idx0, idx1)


def _sc_gather_rows(table, idx):
    n = idx.shape[0]
    d = table.shape[1]
    per_worker = n // SC_WORKERS
    assert per_worker % SC_WINDOW == 0

    n_win = per_worker // SC_WINDOW
    assert n_win % 2 == 0

    def body(table_hbm, idx_hbm, out_hbm, idx_a, idx_b, rows_a, rows_b, sem_a, sem_b):
        base = _sc_worker_base(per_worker)

        def offset(j):
            return pl.multiple_of(base + j * SC_WINDOW, SC_WINDOW)

        def start(j, idx_v, rows_v, sem):
            pltpu.sync_copy(idx_hbm.at[pl.ds(offset(j), SC_WINDOW)], idx_v)
            pltpu.async_copy(table_hbm.at[idx_v], rows_v, sem)

        def finish(j, idx_v, rows_v, sem):
            pltpu.make_async_copy(table_hbm.at[idx_v], rows_v, sem).wait()
            pltpu.sync_copy(rows_v, out_hbm.at[pl.ds(offset(j), SC_WINDOW)])

        start(0, idx_a, rows_a, sem_a)

        @pl.loop(0, n_win, step=2)
        def _(j):
            start(j + 1, idx_b, rows_b, sem_b)
            finish(j, idx_a, rows_a, sem_a)

            @pl.when(j + 2 < n_win)
            def _():
                start(j + 2, idx_a, rows_a, sem_a)

            finish(j + 1, idx_b, rows_b, sem_b)

    return pl.kernel(
        body, out_type=jax.ShapeDtypeStruct((n, d), table.dtype), mesh=_sc_mesh(),
        scratch_types=[pltpu.VMEM((SC_WINDOW,), jnp.int32), pltpu.VMEM((SC_WINDOW,), jnp.int32),
                       pltpu.VMEM((SC_WINDOW, d), table.dtype), pltpu.VMEM((SC_WINDOW, d), table.dtype),
                       pltpu.SemaphoreType.DMA, pltpu.SemaphoreType.DMA],
        name="sc_gather_rows",
    )(table, idx)


def _ffn_kernel(be_ref, nv_ref, xs_ref, wg_ref, wu_ref, wd_ref, ys_ref, wg_b, wu_b, wd_b):
    j = pl.program_id(0)
    valid = nv_ref[j]

    @pl.when((valid > 0) & ((j == 0) | (be_ref[j] != be_ref[jnp.maximum(j - 1, 0)])))
    def _():
        wg_b[...] = wg_ref[0, 0].astype(BF16)
        wu_b[...] = wu_ref[0, 0].astype(BF16)
        wd_b[...] = wd_ref[0, 0].astype(BF16)

    @pl.when(valid > 0)
    def _():
        row = lax.broadcasted_iota(jnp.int32, (SLOT_ROWS, 1), 0)
        xw = jnp.where(row < valid, xs_ref[...], 0)
        xb = _unpack_bf16_pairs(xw).astype(BF16)
        g = jnp.dot(xb, wg_b[...], preferred_element_type=F32)
        u = jnp.dot(xb, wu_b[...], preferred_element_type=F32)
        a = (g / (1.0 + jnp.exp(-g)) * u).astype(BF16)
        ys_ref[...] = _pack_bf16_pairs(jnp.dot(a, wd_b[...], preferred_element_type=F32))

    @pl.when(valid <= 0)
    def _():
        ys_ref[...] = jnp.zeros_like(ys_ref)


def _expert_ffn(block_e, n_valid, xs, w_gate, w_up, w_down, layer):
    n_slots, dw = xs.shape
    d, de = w_gate.shape[2:]
    w_idx = lambda j, be, nv: (layer, be[j], 0, 0)
    return pl.pallas_call(
        _ffn_kernel,
        grid_spec=pltpu.PrefetchScalarGridSpec(
            num_scalar_prefetch=2,
            grid=(n_slots // SLOT_ROWS,),
            in_specs=[pl.BlockSpec((SLOT_ROWS, dw), lambda j, be, nv: (j, 0)),
                      pl.BlockSpec((1, 1, d, de), w_idx),
                      pl.BlockSpec((1, 1, d, de), w_idx),
                      pl.BlockSpec((1, 1, de, d), w_idx)],
            out_specs=pl.BlockSpec((SLOT_ROWS, dw), lambda j, be, nv: (j, 0)),
            scratch_shapes=[pltpu.VMEM((d, de), BF16), pltpu.VMEM((d, de), BF16), pltpu.VMEM((de, d), BF16)]),
        out_shape=jax.ShapeDtypeStruct((n_slots, dw), jnp.int32),
        compiler_params=_params("arbitrary"),
        name="expert_ffn",
    )(block_e, n_valid, xs, w_gate, w_up, w_down)


def _combine_kernel(y0_ref, y1_ref, gcol_ref, x_ref, g2_ref, *rest, pool_in):
    if pool_in:
        sh_ref, sc_ref, n1_ref, wi_ref, x_out, u_out = rest
    else:
        (x_out,) = rest
    gc = gcol_ref[...]
    out = gc[:, 0:1] * _unpack_bf16_pairs(y0_ref[0]) + gc[:, 1:2] * _unpack_bf16_pairs(y1_ref[0])
    x2 = x_ref[0] + g2_ref[0] * out
    x_out[0] = x2
    if pool_in:
        hb = _rms_mod(x2, n1_ref[...], sh_ref[0], sc_ref[0]).astype(BF16)
        u_out[0] = jnp.dot(hb, wi_ref[...], preferred_element_type=F32)


def _combine(yg, gcol, x, g2, pool_args=None, *, tc):
    b, l, d = x.shape
    nt = l // tc
    pool_in = pool_args is not None
    kern = functools.partial(_combine_kernel, pool_in=pool_in)
    row = lambda bi, i: (bi, 0, 0)
    fix = lambda bi, i: (0, 0)
    tile = lambda bi, i: (bi, i, 0)
    in_specs = [pl.BlockSpec((1, tc, d // 2), lambda bi, i: (0, bi * nt + i, 0)),
                pl.BlockSpec((1, tc, d // 2), lambda bi, i: (1, bi * nt + i, 0)),
                pl.BlockSpec((tc, LANES), lambda bi, i: (bi * nt + i, 0)),
                pl.BlockSpec((1, tc, d), tile),
                pl.BlockSpec((1, 1, d), row)]
    out_specs = [pl.BlockSpec((1, tc, d), tile)]
    out_shape = [jax.ShapeDtypeStruct((b, l, d), F32)]
    args = [yg, yg, gcol, x, g2]
    if pool_in:
        in_specs += [pl.BlockSpec((1, 1, d), row), pl.BlockSpec((1, 1, d), row),
                     pl.BlockSpec((1, d), fix), pl.BlockSpec((d, d), fix)]
        out_specs.append(pl.BlockSpec((1, tc, d), tile))
        out_shape.append(jax.ShapeDtypeStruct((b, l, d), F32))
        args += list(pool_args)
    return pl.pallas_call(
        kern,
        grid=(b, nt),
        in_specs=in_specs, out_specs=out_specs, out_shape=out_shape,
        compiler_params=_params("arbitrary", "arbitrary"),
        name="moe_combine",
    )(*args)


def _moe(h2, ridx, gcol, counts, x1, g2, w_gate, w_up, w_down, layer, pool_args=None):
    b, l, d = x1.shape
    n = b * l
    n_blocks = (n * TOP_K) // SLOT_ROWS + N_EXPERTS
    cnt = counts[:, 0].astype(jnp.int32)
    padded = (cnt + SLOT_ROWS - 1) // SLOT_ROWS * SLOT_ROWS
    earlier = jnp.arange(N_EXPERTS)[None, :] < jnp.arange(N_EXPERTS)[:, None]
    pad_start = jnp.sum(jnp.where(earlier, padded[None, :], 0), axis=1).astype(jnp.int32)
    pad_end = pad_start + padded
    block_start = jnp.arange(n_blocks, dtype=jnp.int32) * SLOT_ROWS
    block_e = jnp.minimum(jnp.sum(pad_end[None, :] <= block_start[:, None], axis=1), N_EXPERTS - 1).astype(jnp.int32)
    own = block_e[:, None] == jnp.arange(N_EXPERTS)[None, :]
    data_end = jnp.sum(jnp.where(own, (pad_start + cnt)[None, :], 0), axis=1)
    n_valid = jnp.clip(data_end - block_start, 0, SLOT_ROWS).astype(jnp.int32)
    dest = _slot_index(pad_start, ridx, tn=2048)
    xs = _sc_scatter_rows(h2.reshape(n, d // 2), dest[0], dest[1], n_blocks * SLOT_ROWS)
    ys = _expert_ffn(block_e, n_valid, xs, w_gate, w_up, w_down, layer)
    yg = _sc_gather_rows(ys, dest[:TOP_K].reshape(TOP_K * n)).reshape(TOP_K, n, d // 2)
    return _combine(yg, gcol, x1, g2, pool_args, tc=512)


def kernel(x, c, ctx, c_ctx, ada_w, ada_b, norm1_g, norm2_g, attn_w_in, attn_w_out, attn_q_gain, attn_k_gain,
           attn_lq1, attn_lk1, attn_lq2, attn_lk2, attn_sub_gain, pool_w_in, pool_w_group, pool_scale, pool_w_out,
           router_w, router_b, moe_w_gate, moe_w_up, moe_w_down):
    b, l, d = x.shape
    n_ctx = ctx.shape[1]
    depth = ada_w.shape[0]
    assert depth == 2 and d == N_HEADS * V_DIM
    tm = 512

    mod = _adaln_mod(c, c_ctx, ada_w, ada_b)
    mods = [[mod[i, :b, None, j * d:(j + 1) * d] for j in range(N_MOD)] for i in range(depth)]
    mod_ctx = [jnp.broadcast_to(mod[0, b, j * d:(j + 1) * d], (b, 1, d)) for j in range(2)]

    rwt = router_w.T
    rwh = rwt.astype(BF16)
    rwl = (rwt - rwh.astype(F32)).astype(BF16)
    rb = router_b.reshape(N_EXPERTS, 1)

    sh1, s1, g1, sh2, s2, g2 = mods[0]
    cos, sin = _rope_tables(l)
    pair_up = (jnp.arange(LANES) & 16) == 0

    def gain_rows(g, factor):
        g2 = jnp.concatenate([g, g]) * factor
        return [g2, jnp.where(pair_up, jnp.roll(g2, -16), jnp.roll(g2, 16))]

    q_scale = HEAD_DIM ** -0.5 * math.log2(math.e)
    q_max = jnp.maximum(jnp.max(jnp.abs(attn_q_gain[0])) * (HEAD_DIM ** 0.5 * q_scale), F32_TINY)
    k_max = jnp.maximum(jnp.max(jnp.abs(attn_k_gain[0])) * HEAD_DIM ** 0.5, F32_TINY)
    need = jnp.ceil(jnp.log2(k_max / F8_MAX))
    room = jnp.floor(jnp.log2(F8_MAX / q_max))
    trade = jnp.exp2(jnp.clip(jnp.clip(0.0, need, jnp.maximum(need, room)), -60.0, 60.0))
    h_max = jnp.float32(0.0)
    for shift_, scale_ in ((sh1, s1), (mod_ctx[0], mod_ctx[1])):
        h_max = jnp.maximum(h_max, jnp.max(d ** 0.5 * jnp.max(jnp.abs(norm1_g[0] * (1.0 + scale_)), axis=-1)
                                           + jnp.sqrt(jnp.sum(shift_ * shift_, axis=-1))))
    w_v = attn_w_in[0][:, 2 * d:]
    v_max = 1.02 * h_max * jnp.sqrt(jnp.max(jnp.sum(w_v * w_v, axis=0)))
    v_grow = jnp.exp2(jnp.clip(jnp.ceil(jnp.log2(jnp.maximum(v_max, F32_TINY) / F8_MAX)), 0.0, 60.0))

    gains = jnp.stack(gain_rows(attn_q_gain[0], q_scale * trade) + gain_rows(attn_k_gain[0], 1.0 / trade))
    w_in = jnp.concatenate([attn_w_in[0][:, :2 * d], w_v / v_grow], axis=1).astype(BF16)
    assert l % tm == 0 and l % n_ctx == 0
    q, k_all, vt_all = _qkv_proj(x, sh1, s1, norm1_g[0][None], w_in, gains, cos, sin, n_qk=2, rope=True,
                                 tm=tm, n_keys=l + n_ctx, key_row0=0)
    k_all, vt_all = _qkv_proj(ctx, mod_ctx[0], mod_ctx[1], norm1_g[0][None], w_in[:, d:], gains,
                              cos[:n_ctx], sin[:n_ctx], n_qk=1, rope=False,
                              tm=n_ctx, n_keys=l + n_ctx, key_row0=l, kv=(k_all, vt_all))
    lam_init = 0.8 - 0.6 * math.exp(-0.3 * 0)
    lam_params = jnp.stack([attn_lq1[0], attn_lk1[0], attn_lq2[0], attn_lk2[0],
                            jnp.full((HEAD_DIM,), v_grow, F32)])
    o = _diff_attention(q, k_all, vt_all, lam_params, attn_sub_gain[0][:, None], lam_init=lam_init)

    fix = lambda bi, i: (0, 0)
    x1, h2, ridx, gcol, counts = _mixer_tail(
        (o, attn_w_out[0].astype(BF16)),
        (pl.BlockSpec((1, tm, d), lambda bi, i: (bi, i, 0)), pl.BlockSpec((d, d), fix)),
        _attn_tail_kernel, x, g1, sh2, s2, norm2_g[0][None], rwh, rwl, rb, tm=tm)

    sh1b, s1b, g1b, sh2b, s2b, g2b = mods[1]
    x2, u = _moe(h2, ridx, gcol, counts, x1, g2, moe_w_gate, moe_w_up, moe_w_down, 0,
                 pool_args=(sh1b, s1b, norm1_g[1][None], pool_w_in[0].astype(BF16)))
    gd = pool_w_group.shape[2]
    nh = tm // POOL_HALO
    front_specs = (
        pl.BlockSpec((1, tm, d), lambda bi, i: (bi, i, 0)),
        pl.BlockSpec((1, POOL_HALO, d), lambda bi, i: (bi, jnp.maximum(i * nh - 1, 0), 0)),
        pl.BlockSpec((1, POOL_HALO, d), lambda bi, i: (bi, jnp.minimum((i + 1) * nh, l // POOL_HALO - 1), 0)),
        pl.BlockSpec((len(POOL_WINDOWS), gd, gd), lambda bi, i: (0, 0, 0)),
        pl.BlockSpec((1, d), fix),
        pl.BlockSpec((d, d), fix))
    x3, h2b, ridx_b, gcol_b, counts_b = _mixer_tail(
        (u, u, u, pool_w_group[0].astype(BF16), pool_scale[0][None], pool_w_out[0].astype(BF16)),
        front_specs, functools.partial(_pool_tail_kernel, seq_len=l),
        x2, g1b, sh2b, s2b, norm2_g[1][None], rwh, rwl, rb, tm=tm,
        scratch=[pltpu.VMEM((tm + 2 * POOL_HALO, d), F32)])
    (out,) = _moe(h2b, ridx_b, gcol_b, counts_b, x3, g2b, moe_w_gate, moe_w_up, moe_w_down, 1)
    return out
```

```python
import functools
import math

import jax
import jax.numpy as jnp
import numpy as np
from jax import lax
from jax.experimental import pallas as pl
from jax.experimental.pallas import tpu as pltpu
from jax.experimental.pallas import tpu_sc as plsc

F32 = jnp.float32
BF16 = jnp.bfloat16
F8 = jnp.float8_e4m3fn
F8_MAX = float(jnp.finfo(F8).max)
F32_TINY = float(jnp.finfo(F32).tiny)

LANES = 128
SUBLANES = 8
N_HEADS = 8
HEAD_DIM = 64
V_DIM = 2 * HEAD_DIM
V_ROWS = V_DIM + 32
P_SHIFT = 8.0
GRID_W = 64
ROPE_THETA = 10000.0
NORM_EPS = 1e-6
N_MOD = 6
POOL_WINDOWS = (2, 4, 8, 16)
POOL_HALO = max(POOL_WINDOWS) // 2
HALO_ROWS = 16
N_EXPERTS = 32
N_EXPERT_GROUPS = 4
EXPERTS_PER_GROUP = N_EXPERTS // N_EXPERT_GROUPS
TOP_K = 2
SLOT_ROWS = 512
SC_CORES = 2
SC_SUBCORES = 16
SC_WORKERS = SC_CORES * SC_SUBCORES
SC_WINDOW = 64
ATTN_GROUP_CHUNKS = 1
ATTN_SCORE_AHEAD = 1
VMEM_LIMIT = 48 * 1024 * 1024
NT_DIMS = (((1,), (1,)), ((), ()))


def _params(*sem):
    return pltpu.CompilerParams(dimension_semantics=sem, vmem_limit_bytes=VMEM_LIMIT)


def _rms_mod(x, gain, shift, scale):
    h = x * lax.rsqrt(jnp.mean(x * x, axis=-1, keepdims=True) + NORM_EPS) * gain
    return h * (1.0 + scale) + shift


def _pack_bf16_pairs(x):
    c = x.shape[1] // 2
    hi = lax.bitcast_convert_type(x[:, :c].astype(BF16).astype(F32), jnp.uint32)
    lo = lax.bitcast_convert_type(x[:, c:].astype(BF16).astype(F32), jnp.uint32)
    return lax.bitcast_convert_type(hi | (lo >> 16), jnp.int32)


def _unpack_bf16_pairs(w):
    u = lax.bitcast_convert_type(w, jnp.uint32)
    hi = lax.bitcast_convert_type(u & jnp.uint32(0xFFFF0000), F32)
    lo = lax.bitcast_convert_type(u << 16, F32)
    return jnp.concatenate([hi, lo], axis=1)


def _mod_kernel(c_ref, w_ref, b_ref, o_ref):
    c = c_ref[...]
    a = c / (1.0 + jnp.exp(-c))
    o_ref[0] = jnp.dot(a, w_ref[0], precision=lax.Precision.HIGHEST,
                       preferred_element_type=F32) + b_ref[0]


def _adaln_mod(c, c_ctx, ada_w, ada_b):
    depth, d, n_out = ada_w.shape
    b = c.shape[0]
    assert b + 1 <= SUBLANES
    rows = jnp.concatenate([c, c_ctx[None], jnp.zeros((SUBLANES - b - 1, d), F32)], axis=0)
    tn = n_out // 4
    return pl.pallas_call(
        _mod_kernel,
        grid=(depth, n_out // tn),
        in_specs=[pl.BlockSpec((SUBLANES, d), lambda i, j: (0, 0)),
                  pl.BlockSpec((1, d, tn), lambda i, j: (i, 0, j)),
                  pl.BlockSpec((1, 1, tn), lambda i, j: (i, 0, j))],
        out_specs=pl.BlockSpec((1, SUBLANES, tn), lambda i, j: (i, 0, j)),
        out_shape=jax.ShapeDtypeStruct((depth, SUBLANES, n_out), F32),
        compiler_params=_params("arbitrary", "arbitrary"),
        name="adaln_mod",
    )(rows, ada_w, ada_b.reshape(depth, 1, n_out))


def _qkv_kernel(x_ref, sh_ref, sc_ref, g_ref, w_ref, gains_ref, cos_ref, sin_ref, *refs, n_qk, rope):
    out_refs = refs[-(n_qk + 1):]
    tm, d = x_ref.shape[1:]
    hb = _rms_mod(x_ref[0], g_ref[...], sh_ref[0], sc_ref[0]).astype(BF16)
    lane_b4 = (lax.broadcasted_iota(jnp.int32, (1, LANES), 1) & 16) == 0
    chunk_r = lax.broadcasted_iota(jnp.int32, (2 * LANES, 2 * LANES), 0) // HEAD_DIM
    chunk_c = lax.broadcasted_iota(jnp.int32, (2 * LANES, 2 * LANES), 1) // HEAD_DIM
    same_chunk = (chunk_r == chunk_c).astype(BF16)
    gains = gains_ref[...]
    for t in range(n_qk):
        r = 2 * (t + 2 - n_qk)
        if rope:
            cos_t = cos_ref[...] * gains[r:r + 1]
            sin_t = sin_ref[...] * gains[r + 1:r + 2]
        for j in range(0, d, 2 * LANES):
            acc = jnp.dot(hb, w_ref[:, t * d + j:t * d + j + 2 * LANES], preferred_element_type=F32)
            ssq = jnp.dot((acc * acc).astype(BF16), same_chunk, preferred_element_type=F32)
            nrm = acc * lax.rsqrt(ssq * (1.0 / HEAD_DIM) + NORM_EPS)
            for half in range(2):
                blk = nrm[:, half * LANES:(half + 1) * LANES]
                if rope:
                    rot = jnp.where(lane_b4, pltpu.roll(blk, LANES - 16, 1), pltpu.roll(blk, 16, 1))
                    y = blk * cos_t + rot * sin_t
                else:
                    y = blk * gains[r:r + 1]
                c0 = j + half * LANES
                if t < n_qk - 1:
                    yt = y.T
                    first = lax.broadcasted_iota(jnp.int32, (V_DIM, 1), 0) < HEAD_DIM
                    out_refs[t][0, c0 // LANES, 0, :, 0:tm] = jnp.where(first, yt, 0.0).astype(F8)
                    out_refs[t][0, c0 // LANES, 0, :, tm:2 * tm] = jnp.where(first, 0.0, yt).astype(F8)
                else:
                    out_refs[t][0, :, c0:c0 + LANES] = y.astype(F8)
    vt_ref = out_refs[n_qk]
    ones = jnp.where(lax.broadcasted_iota(jnp.int32, (V_ROWS - V_DIM, tm), 0) == 0, 1.0, 0.0).astype(F8)
    for j in range(0, d, 2 * LANES):
        acc = jnp.dot(hb, w_ref[:, n_qk * d + j:n_qk * d + j + 2 * LANES], preferred_element_type=F32)
        acc_t = acc.T
        for half in range(2):
            h = j // LANES + half
            vt_ref[0, h, 0:V_DIM, :] = acc_t[half * V_DIM:(half + 1) * V_DIM].astype(F8)
            vt_ref[0, h, V_DIM:V_ROWS, :] = ones


def _qkv_proj(x, shift, scale, gain, w, gains, cos, sin, *, n_qk, rope, tm, n_keys, key_row0, kv=None):
    b, l, d = x.shape
    n_out = n_qk + 1
    kern = functools.partial(_qkv_kernel, n_qk=n_qk, rope=rope)
    row = lambda bi, i: (bi, 0, 0)
    fix = lambda bi, i: (0, 0)
    kb = key_row0 // tm
    tile = pl.BlockSpec((1, tm, d), lambda bi, i: (bi, i, 0))
    k_spec = pl.BlockSpec((1, tm, d), lambda bi, i: (bi, kb + i, 0))
    vt_spec = pl.BlockSpec((1, N_HEADS, V_ROWS, tm), lambda bi, i: (bi, 0, 0, kb + i))
    q_spec = pl.BlockSpec((1, N_HEADS, 1, V_DIM, 2 * tm), lambda bi, i: (bi, 0, i, 0, 0))
    q_shape = jax.ShapeDtypeStruct((b, N_HEADS, l // tm, V_DIM, 2 * tm), F8)
    k_shape = jax.ShapeDtypeStruct((b, n_keys, d), F8)
    vt_shape = jax.ShapeDtypeStruct((b, N_HEADS, V_ROWS, n_keys), F8)
    in_specs = [tile,
                pl.BlockSpec((1, 1, d), row), pl.BlockSpec((1, 1, d), row),
                pl.BlockSpec((1, d), fix),
                pl.BlockSpec((d, n_out * d), fix),
                pl.BlockSpec((4, LANES), fix),
                pl.BlockSpec((tm, LANES), lambda bi, i: (i, 0)),
                pl.BlockSpec((tm, LANES), lambda bi, i: (i, 0))]
    args = [x, shift, scale, gain, w, gains, cos, sin]
    aliases = {}
    if kv is not None:
        aliases = {len(args): n_qk - 1, len(args) + 1: n_qk}
        in_specs += [pl.BlockSpec(memory_space=pl.ANY)] * 2
        args += list(kv)
    return pl.pallas_call(
        kern,
        grid=(b, l // tm),
        in_specs=in_specs,
        out_specs=[q_spec] * (n_qk - 1) + [k_spec, vt_spec],
        out_shape=[q_shape] * (n_qk - 1) + [k_shape, vt_shape],
        input_output_aliases=aliases,
        compiler_params=_params("arbitrary", "arbitrary"),
        name="qkv_proj",
    )(*args)


def _rope_tables(n_tokens):
    rows = n_tokens // GRID_W
    row = np.repeat(np.arange(rows, dtype=np.float32), GRID_W)
    col = np.tile(np.arange(GRID_W, dtype=np.float32), rows)
    half = HEAD_DIM // 2
    inv_freq = (np.float32(ROPE_THETA) ** (-np.arange(0, half, 2, dtype=np.float32) / half)).astype(np.float32)
    ang_r = row[:, None] * inv_freq
    ang_c = col[:, None] * inv_freq
    ang = np.concatenate([ang_r, ang_r, ang_c, ang_c] * 2, axis=-1)
    sign = np.where((np.arange(LANES) & 16) == 0, -1.0, 1.0).astype(np.float32)
    return jnp.asarray(np.cos(ang), F32), jnp.asarray(np.sin(ang) * sign, F32)


def _attn_kernel(q_ref, k_ref, vt_ref, lp_ref, sg_ref, o_ref, s_ref, *, tk, group, ahead, lam_init):
    tq = q_ref.shape[4] // 2
    n_chunks = k_ref.shape[1] // tk
    qz = q_ref[0, 0, 0]
    n_slots = (ahead + 1) * group

    def score_chunk(c, m_grp):
        st = jnp.dot(k_ref[0, c * tk:(c + 1) * tk, :], qz, preferred_element_type=F32).astype(BF16)
        slot = c % n_slots
        s_ref[slot * tk:(slot + 1) * tk, :] = st
        mc = jnp.max(st, axis=0, keepdims=True)
        return mc if m_grp is None else jnp.maximum(m_grp, mc)

    def value_chunk(c, m_ref, part):
        slot = c % n_slots
        p = jnp.exp2(s_ref[slot * tk:(slot + 1) * tk, :] - (m_ref - P_SHIFT)).astype(F8)
        pv = jnp.dot(vt_ref[0, 0, :, c * tk:(c + 1) * tk], p, preferred_element_type=F32)
        return pv if part is None else part + pv

    groups = [list(range(g0, min(g0 + group, n_chunks))) for g0 in range(0, n_chunks, group)]
    m_of = {}
    for g in range(min(ahead, len(groups))):
        for c in groups[g]:
            m_of[g] = score_chunk(c, m_of.get(g))
    m = None
    acc = None
    for gi, cur in enumerate(groups):
        nxt = groups[gi + ahead] if gi + ahead < len(groups) else []
        m_new = m_of[gi] if m is None else jnp.maximum(m, m_of[gi])
        part = None
        for i in range(max(len(cur), len(nxt))):
            if i < len(nxt):
                m_of[gi + ahead] = score_chunk(nxt[i], m_of.get(gi + ahead))
            if i < len(cur):
                part = value_chunk(cur[i], m_new, part)
        acc = part if acc is None else acc * jnp.exp2(m.astype(F32) - m_new.astype(F32)) + part
        m = m_new
    acc = acc[:V_DIM] / acc[V_DIM:V_DIM + 1]
    lp = lp_ref[...]
    lam = (jnp.exp(jnp.sum(lp[0:1] * lp[1:2], axis=-1, keepdims=True))
           - jnp.exp(jnp.sum(lp[2:3] * lp[3:4], axis=-1, keepdims=True)) + lam_init)
    o = (acc[:, :tq] - lam * acc[:, tq:]) * lp[4:5, 0:1]
    o = o * lax.rsqrt(jnp.mean(o * o, axis=0, keepdims=True) + NORM_EPS) * sg_ref[...] * (1.0 - lam_init)
    o_ref[0] = o.T.astype(BF16)


def _attn_chunk(n_keys):
    for tk in (768, 512, 256, 128):
        if n_keys % tk == 0:
            return tk
    raise ValueError(f"key count {n_keys} is not a multiple of {LANES}")


def _diff_attention(qz, k_all, vt_all, lam_params, sub_gain, *, lam_init):
    b, _, n_tiles, _, tq2 = qz.shape
    tq = tq2 // 2
    l, d = n_tiles * tq, N_HEADS * V_DIM
    n_keys = k_all.shape[1]
    tk = _attn_chunk(n_keys)
    kern = functools.partial(_attn_kernel, tk=tk, group=ATTN_GROUP_CHUNKS, ahead=ATTN_SCORE_AHEAD, lam_init=lam_init)
    return pl.pallas_call(
        kern,
        grid=(b, N_HEADS, l // tq),
        in_specs=[pl.BlockSpec((1, 1, 1, V_DIM, 2 * tq), lambda bi, h, i: (bi, h, i, 0, 0)),
                  pl.BlockSpec((1, n_keys, V_DIM), lambda bi, h, i: (bi, 0, h)),
                  pl.BlockSpec((1, 1, V_ROWS, n_keys), lambda bi, h, i: (bi, h, 0, 0)),
                  pl.BlockSpec((5, HEAD_DIM), lambda bi, h, i: (0, 0)),
                  pl.BlockSpec((V_DIM, 1), lambda bi, h, i: (0, 0))],
        out_specs=pl.BlockSpec((1, tq, V_DIM), lambda bi, h, i: (bi, i, h)),
        out_shape=jax.ShapeDtypeStruct((b, l, d), BF16),
        scratch_shapes=[pltpu.VMEM(((ATTN_SCORE_AHEAD + 1) * ATTN_GROUP_CHUNKS * tk, 2 * tq), BF16)],
        compiler_params=_params("arbitrary", "arbitrary", "arbitrary"),
        name="diff_attention",
    )(qz, k_all, vt_all, lam_params, sub_gain)


def _route(h2, rwh_ref, rwl_ref, rb_ref, before_ref, carry_ref, ridx_ref, gcol_ref, cnt_ref, is_first):
    tm = h2.shape[0]
    hh = h2.astype(BF16)
    hl = (h2 - hh.astype(F32)).astype(BF16)
    rw2 = jnp.concatenate([rwh_ref[...], rwl_ref[...]], axis=0)
    part = lax.dot_general(rw2, hh, NT_DIMS, preferred_element_type=F32)
    logits = (part[:N_EXPERTS] + part[N_EXPERTS:]
              + lax.dot_general(rwh_ref[...], hl, NT_DIMS, preferred_element_type=F32) + rb_ref[...])
    groups = [logits[g * EXPERTS_PER_GROUP:(g + 1) * EXPERTS_PER_GROUP] for g in range(N_EXPERT_GROUPS)]
    top = groups[0]
    for g in range(1, N_EXPERT_GROUPS):
        top = jnp.maximum(top, groups[g])
    top = jnp.max(top, axis=0, keepdims=True)
    sub = lax.broadcasted_iota(jnp.int32, (EXPERTS_PER_GROUP, tm), 0)
    best = None
    for g in range(N_EXPERT_GROUPS):
        ex = jnp.exp(groups[g] - top)
        v1 = jnp.max(ex, axis=0, keepdims=True)
        i1 = jnp.min(jnp.where(ex == v1, sub, EXPERTS_PER_GROUP), axis=0, keepdims=True)
        rest = jnp.where(sub == i1, -1.0, ex)
        v2 = jnp.max(rest, axis=0, keepdims=True)
        i2 = jnp.min(jnp.where(rest == v2, sub, EXPERTS_PER_GROUP), axis=0, keepdims=True)
        cand = (v1 + v2, v1, v2, i1 + g * EXPERTS_PER_GROUP, i2 + g * EXPERTS_PER_GROUP)
        if best is None:
            best = cand
        else:
            better = cand[0] > best[0]
            best = tuple(jnp.where(better, new, old) for new, old in zip(cand, best))
    _, v1, v2, e0, e1 = best
    gate0 = v1 / (v1 + v2)
    gate1 = v2 / (v1 + v2)

    @pl.when(is_first)
    def _():
        carry_ref[...] = jnp.zeros_like(carry_ref)

    erow = lax.broadcasted_iota(jnp.int32, (N_EXPERTS, tm), 0)
    oh0 = erow == e0
    oh1 = erow == e1
    chosen = jnp.where(oh0 | oh1, 1.0, 0.0)
    prior = carry_ref[:, 0:1] + jnp.dot(chosen.astype(BF16), before_ref[...], preferred_element_type=F32)
    r0 = jnp.sum(jnp.where(oh0, prior, 0.0), axis=0, keepdims=True).astype(jnp.int32)
    r1 = jnp.sum(jnp.where(oh1, prior, 0.0), axis=0, keepdims=True).astype(jnp.int32)
    carry_ref[...] = carry_ref[...] + jnp.sum(chosen, axis=1, keepdims=True)
    cnt_ref[...] = carry_ref[...]
    rid = lax.broadcasted_iota(jnp.int32, (SUBLANES, tm), 0)
    ridx_ref[...] = jnp.where(rid == 0, e0, jnp.where(rid == 1, e1, jnp.where(rid == 2, r0, jnp.where(rid == 3, r1, 0))))
    gid = lax.broadcasted_iota(jnp.int32, (LANES, tm), 0)
    gcol_ref[...] = jnp.where(gid == 0, gate0, jnp.where(gid == 1, gate1, 0.0)).T


def _tail(y, x_ref, g1_ref, sh2_ref, s2_ref, n2_ref, rwh_ref, rwl_ref, rb_ref, before_ref,
          x_out, h2_out, ridx_ref, gcol_ref, cnt_ref, carry_ref):
    x1 = x_ref[0] + g1_ref[0] * y
    x_out[0] = x1
    h2 = _rms_mod(x1, n2_ref[...], sh2_ref[0], s2_ref[0])
    h2_out[0] = _pack_bf16_pairs(h2)
    is_first = (pl.program_id(0) == 0) & (pl.program_id(1) == 0)
    _route(h2, rwh_ref, rwl_ref, rb_ref, before_ref, carry_ref, ridx_ref, gcol_ref, cnt_ref, is_first)


def _attn_tail_kernel(a_ref, wo_ref, *rest):
    y = jnp.dot(a_ref[0], wo_ref[...], preferred_element_type=F32)
    _tail(y, *rest)


def _pool_tail_kernel(u_ref, up_ref, un_ref, wg_ref, cs_ref, wo_ref, *rest, seq_len):
    *tail_refs, ubuf = rest
    tm = u_ref.shape[1]
    i = pl.program_id(1)
    u = u_ref[0].astype(F32)
    ubuf[0:POOL_HALO] = jnp.where(i > 0, up_ref[0].astype(F32)[HALO_ROWS - POOL_HALO:], 0.0)
    ubuf[POOL_HALO:POOL_HALO + tm] = u
    ubuf[POOL_HALO + tm:2 * POOL_HALO + tm] = jnp.where(i < pl.num_programs(1) - 1,
                                                        un_ref[0].astype(F32)[:POOL_HALO], 0.0)
    pos = i * tm + lax.broadcasted_iota(jnp.int32, (tm, 1), 0)
    gd = wg_ref.shape[1]
    outs = []
    for g, win in enumerate(POOL_WINDOWS):
        half = win // 2
        cols = slice(g * gd, (g + 1) * gd)
        s = ubuf[POOL_HALO - half:POOL_HALO - half + tm, cols]
        for j in range(1 - half, half):
            s = s + ubuf[POOL_HALO + j:POOL_HALO + j + tm, cols]
        inv_cnt = 1.0 / (jnp.minimum(pos + half, seq_len) - jnp.maximum(pos - half, 0)).astype(F32)
        dlt = (s * inv_cnt - u[:, cols]).astype(BF16)
        outs.append(jnp.dot(dlt, wg_ref[g], preferred_element_type=F32))
    z = (jnp.concatenate(outs, axis=-1) * cs_ref[...]).astype(BF16)
    y = jnp.dot(z, wo_ref[...], preferred_element_type=F32)
    _tail(y, *tail_refs)


def _mixer_tail(front_args, front_specs, kern, x, g1, sh2, s2, n2g, rwh, rwl, rb, *, tm, scratch=()):
    b, l, d = x.shape
    nt = l // tm
    n = b * l
    row = lambda bi, i: (bi, 0, 0)
    fix = lambda bi, i: (0, 0)
    tile = lambda bi, i: (bi, i, 0)
    in_specs = list(front_specs) + [
        pl.BlockSpec((1, tm, d), tile),
        pl.BlockSpec((1, 1, d), row), pl.BlockSpec((1, 1, d), row), pl.BlockSpec((1, 1, d), row),
        pl.BlockSpec((1, d), fix),
        pl.BlockSpec((N_EXPERTS, d), fix), pl.BlockSpec((N_EXPERTS, d), fix),
        pl.BlockSpec((N_EXPERTS, 1), fix),
        pl.BlockSpec((tm, tm), fix)]
    before = jnp.asarray(np.triu(np.ones((tm, tm), np.float32), k=1), BF16)
    out_specs = [pl.BlockSpec((1, tm, d), tile), pl.BlockSpec((1, tm, d // 2), tile),
                 pl.BlockSpec((SUBLANES, tm), lambda bi, i: (0, bi * nt + i)),
                 pl.BlockSpec((tm, LANES), lambda bi, i: (bi * nt + i, 0)),
                 pl.BlockSpec((N_EXPERTS, LANES), fix)]
    out_shape = [jax.ShapeDtypeStruct((b, l, d), F32), jax.ShapeDtypeStruct((b, l, d // 2), jnp.int32),
                 jax.ShapeDtypeStruct((SUBLANES, n), jnp.int32), jax.ShapeDtypeStruct((n, LANES), F32),
                 jax.ShapeDtypeStruct((N_EXPERTS, LANES), F32)]
    return pl.pallas_call(
        kern,
        grid=(b, nt),
        in_specs=in_specs, out_specs=out_specs, out_shape=out_shape,
        scratch_shapes=[pltpu.VMEM((N_EXPERTS, LANES), F32)] + list(scratch),
        compiler_params=_params("arbitrary", "arbitrary"),
        name="mixer_tail",
    )(*front_args, x, g1, sh2, s2, n2g, rwh, rwl, rb, before)


def _slot_kernel(ps_ref, ridx_ref, dest_ref):
    ridx = ridx_ref[...]
    ps = ps_ref[...]
    erow = lax.broadcasted_iota(jnp.int32, (N_EXPERTS, ridx.shape[1]), 0)
    rows = []
    for k in range(TOP_K):
        start = jnp.sum(jnp.where(erow == ridx[k:k + 1], ps, 0), axis=0, keepdims=True)
        rows.append(start + ridx[TOP_K + k:TOP_K + k + 1])
    rid = lax.broadcasted_iota(jnp.int32, ridx.shape, 0)
    dest_ref[...] = jnp.where(rid == 0, rows[0], jnp.where(rid == 1, rows[1], 0))


def _slot_index(pad_start, ridx, *, tn):
    n = ridx.shape[1]
    return pl.pallas_call(
        _slot_kernel,
        grid=(n // tn,),
        in_specs=[pl.BlockSpec((N_EXPERTS, 1), lambda i: (0, 0)),
                  pl.BlockSpec((SUBLANES, tn), lambda i: (0, i))],
        out_specs=pl.BlockSpec((SUBLANES, tn), lambda i: (0, i)),
        out_shape=jax.ShapeDtypeStruct((SUBLANES, n), jnp.int32),
        compiler_params=_params("arbitrary"),
        name="slot_index",
    )(pad_start.reshape(N_EXPERTS, 1), ridx)


def _sc_mesh():
    return plsc.VectorSubcoreMesh(core_axis_name="c", subcore_axis_name="s",
                                  num_cores=SC_CORES, num_subcores=SC_SUBCORES)


def _sc_worker_base(per_worker):
    return (lax.axis_index("s") * SC_CORES + lax.axis_index("c")) * per_worker


def _sc_scatter_rows(rows, idx0, idx1, n_slots):
    n, d = rows.shape
    per_worker = n // SC_WORKERS
    assert per_worker % SC_WINDOW == 0

    n_win = per_worker // SC_WINDOW
    assert n_win % 2 == 0

    def body(rows_hbm, i0_hbm, i1_hbm, out_hbm, i0_a, i1_a, rows_a, i0_b, i1_b, rows_b, sem_a, sem_b):
        base = _sc_worker_base(per_worker)

        def offset(j):
            return pl.multiple_of(base + j * SC_WINDOW, SC_WINDOW)

        def start(j, i0_v, i1_v, rows_v, sem):
            pltpu.sync_copy(i0_hbm.at[pl.ds(offset(j), SC_WINDOW)], i0_v)
            pltpu.sync_copy(i1_hbm.at[pl.ds(offset(j), SC_WINDOW)], i1_v)
            pltpu.async_copy(rows_hbm.at[pl.ds(offset(j), SC_WINDOW)], rows_v, sem)

        def finish(j, i0_v, i1_v, rows_v, sem):
            pltpu.make_async_copy(rows_hbm.at[pl.ds(offset(j), SC_WINDOW)], rows_v, sem).wait()
            pltpu.sync_copy(rows_v, out_hbm.at[i0_v])
            pltpu.sync_copy(rows_v, out_hbm.at[i1_v])

        start(0, i0_a, i1_a, rows_a, sem_a)

        @pl.loop(0, n_win, step=2)
        def _(j):
            start(j + 1, i0_b, i1_b, rows_b, sem_b)
            finish(j, i0_a, i1_a, rows_a, sem_a)

            @pl.when(j + 2 < n_win)
            def _():
                start(j + 2, i0_a, i1_a, rows_a, sem_a)

            finish(j + 1, i0_b, i1_b, rows_b, sem_b)

    window = [pltpu.VMEM((SC_WINDOW,), jnp.int32), pltpu.VMEM((SC_WINDOW,), jnp.int32),
              pltpu.VMEM((SC_WINDOW, d), rows.dtype)]
    return pl.kernel(
        body, out_type=jax.ShapeDtypeStruct((n_slots, d), rows.dtype), mesh=_sc_mesh(),
        scratch_types=window + window + [pltpu.SemaphoreType.DMA, pltpu.SemaphoreType.DMA],
        name="sc_scatter_rows",
    )(rows, idx0, idx1)


def _sc_gather_rows(table, idx):
    n = idx.shape[0]
    d = table.shape[1]
    per_worker = n // SC_WORKERS
    assert per_worker % SC_WINDOW == 0

    n_win = per_worker // SC_WINDOW
    assert n_win % 2 == 0

    def body(table_hbm, idx_hbm, out_hbm, idx_a, idx_b, rows_a, rows_b, sem_a, sem_b):
        base = _sc_worker_base(per_worker)

        def offset(j):
            return pl.multiple_of(base + j * SC_WINDOW, SC_WINDOW)

        def start(j, idx_v, rows_v, sem):
            pltpu.sync_copy(idx_hbm.at[pl.ds(offset(j), SC_WINDOW)], idx_v)
            pltpu.async_copy(table_hbm.at[idx_v], rows_v, sem)

        def finish(j, idx_v, rows_v, sem):
            pltpu.make_async_copy(table_hbm.at[idx_v], rows_v, sem).wait()
            pltpu.sync_copy(rows_v, out_hbm.at[pl.ds(offset(j), SC_WINDOW)])

        start(0, idx_a, rows_a, sem_a)

        @pl.loop(0, n_win, step=2)
        def _(j):
            start(j + 1, idx_b, rows_b, sem_b)
            finish(j, idx_a, rows_a, sem_a)

            @pl.when(j + 2 < n_win)
            def _():
                start(j + 2, idx_a, rows_a, sem_a)

            finish(j + 1, idx_b, rows_b, sem_b)

    return pl.kernel(
        body, out_type=jax.ShapeDtypeStruct((n, d), table.dtype), mesh=_sc_mesh(),
        scratch_types=[pltpu.VMEM((SC_WINDOW,), jnp.int32), pltpu.VMEM((SC_WINDOW,), jnp.int32),
                       pltpu.VMEM((SC_WINDOW, d), table.dtype), pltpu.VMEM((SC_WINDOW, d), table.dtype),
                       pltpu.SemaphoreType.DMA, pltpu.SemaphoreType.DMA],
        name="sc_gather_rows",
    )(table, idx)


def _ffn_kernel(be_ref, nv_ref, xs_ref, wg_ref, wu_ref, wd_ref, ys_ref, wg_b, wu_b, wd_b):
    j = pl.program_id(0)
    valid = nv_ref[j]

    @pl.when((valid > 0) & ((j == 0) | (be_ref[j] != be_ref[jnp.maximum(j - 1, 0)])))
    def _():
        wg_b[...] = wg_ref[0, 0].astype(BF16)
        wu_b[...] = wu_ref[0, 0].astype(BF16)
        wd_b[...] = wd_ref[0, 0].astype(BF16)

    @pl.when(valid > 0)
    def _():
        row = lax.broadcasted_iota(jnp.int32, (SLOT_ROWS, 1), 0)
        xw = jnp.where(row < valid, xs_ref[...], 0)
        xb = _unpack_bf16_pairs(xw).astype(BF16)
        g = jnp.dot(xb, wg_b[...], preferred_element_type=F32)
        u = jnp.dot(xb, wu_b[...], preferred_element_type=F32)
        a = (g / (1.0 + jnp.exp(-g)) * u).astype(BF16)
        ys_ref[...] = _pack_bf16_pairs(jnp.dot(a, wd_b[...], preferred_element_type=F32))

    @pl.when(valid <= 0)
    def _():
        ys_ref[...] = jnp.zeros_like(ys_ref)


def _expert_ffn(block_e, n_valid, xs, w_gate, w_up, w_down, layer):
    n_slots, dw = xs.shape
    d, de = w_gate.shape[2:]
    w_idx = lambda j, be, nv: (layer, be[j], 0, 0)
    return pl.pallas_call(
        _ffn_kernel,
        grid_spec=pltpu.PrefetchScalarGridSpec(
            num_scalar_prefetch=2,
            grid=(n_slots // SLOT_ROWS,),
            in_specs=[pl.BlockSpec((SLOT_ROWS, dw), lambda j, be, nv: (j, 0)),
                      pl.BlockSpec((1, 1, d, de), w_idx),
                      pl.BlockSpec((1, 1, d, de), w_idx),
                      pl.BlockSpec((1, 1, de, d), w_idx)],
            out_specs=pl.BlockSpec((SLOT_ROWS, dw), lambda j, be, nv: (j, 0)),
            scratch_shapes=[pltpu.VMEM((d, de), BF16), pltpu.VMEM((d, de), BF16), pltpu.VMEM((de, d), BF16)]),
        out_shape=jax.ShapeDtypeStruct((n_slots, dw), jnp.int32),
        compiler_params=_params("arbitrary"),
        name="expert_ffn",
    )(block_e, n_valid, xs, w_gate, w_up, w_down)


def _combine_kernel(y0_ref, y1_ref, gcol_ref, x_ref, g2_ref, *rest, pool_in):
    if pool_in:
        sh_ref, sc_ref, n1_ref, wi_ref, x_out, u_out = rest
    else:
        (x_out,) = rest
    gc = gcol_ref[...]
    out = gc[:, 0:1] * _unpack_bf16_pairs(y0_ref[0]) + gc[:, 1:2] * _unpack_bf16_pairs(y1_ref[0])
    x2 = x_ref[0] + g2_ref[0] * out
    x_out[0] = x2
    if pool_in:
        hb = _rms_mod(x2, n1_ref[...], sh_ref[0], sc_ref[0]).astype(BF16)
        u_out[0] = jnp.dot(hb, wi_ref[...], preferred_element_type=F32).astype(BF16)


def _combine(yg, gcol, x, g2, pool_args=None, *, tc):
    b, l, d = x.shape
    nt = l // tc
    pool_in = pool_args is not None
    kern = functools.partial(_combine_kernel, pool_in=pool_in)
    row = lambda bi, i: (bi, 0, 0)
    fix = lambda bi, i: (0, 0)
    tile = lambda bi, i: (bi, i, 0)
    in_specs = [pl.BlockSpec((1, tc, d // 2), lambda bi, i: (0, bi * nt + i, 0)),
                pl.BlockSpec((1, tc, d // 2), lambda bi, i: (1, bi * nt + i, 0)),
                pl.BlockSpec((tc, LANES), lambda bi, i: (bi * nt + i, 0)),
                pl.BlockSpec((1, tc, d), tile),
                pl.BlockSpec((1, 1, d), row)]
    out_specs = [pl.BlockSpec((1, tc, d), tile)]
    out_shape = [jax.ShapeDtypeStruct((b, l, d), F32)]
    args = [yg, yg, gcol, x, g2]
    if pool_in:
        in_specs += [pl.BlockSpec((1, 1, d), row), pl.BlockSpec((1, 1, d), row),
                     pl.BlockSpec((1, d), fix), pl.BlockSpec((d, d), fix)]
        out_specs.append(pl.BlockSpec((1, tc, d), tile))
        out_shape.append(jax.ShapeDtypeStruct((b, l, d), BF16))
        args += list(pool_args)
    return pl.pallas_call(
        kern,
        grid=(b, nt),
        in_specs=in_specs, out_specs=out_specs, out_shape=out_shape,
        compiler_params=_params("arbitrary", "arbitrary"),
        name="moe_combine",
    )(*args)


def _moe(h2, ridx, gcol, counts, x1, g2, w_gate, w_up, w_down, layer, pool_args=None):
    b, l, d = x1.shape
    n = b * l
    n_blocks = (n * TOP_K) // SLOT_ROWS + N_EXPERTS
    cnt = counts[:, 0].astype(jnp.int32)
    padded = (cnt + SLOT_ROWS - 1) // SLOT_ROWS * SLOT_ROWS
    earlier = jnp.arange(N_EXPERTS)[None, :] < jnp.arange(N_EXPERTS)[:, None]
    pad_start = jnp.sum(jnp.where(earlier, padded[None, :], 0), axis=1).astype(jnp.int32)
    pad_end = pad_start + padded
    block_start = jnp.arange(n_blocks, dtype=jnp.int32) * SLOT_ROWS
    block_e = jnp.minimum(jnp.sum(pad_end[None, :] <= block_start[:, None], axis=1), N_EXPERTS - 1).astype(jnp.int32)
    own = block_e[:, None] == jnp.arange(N_EXPERTS)[None, :]
    data_end = jnp.sum(jnp.where(own, (pad_start + cnt)[None, :], 0), axis=1)
    n_valid = jnp.clip(data_end - block_start, 0, SLOT_ROWS).astype(jnp.int32)
    dest = _slot_index(pad_start, ridx, tn=2048)
    xs = _sc_scatter_rows(h2.reshape(n, d // 2), dest[0], dest[1], n_blocks * SLOT_ROWS)
    ys = _expert_ffn(block_e, n_valid, xs, w_gate, w_up, w_down, layer)
    yg = _sc_gather_rows(ys, dest[:TOP_K].reshape(TOP_K * n)).reshape(TOP_K, n, d // 2)
    return _combine(yg, gcol, x1, g2, pool_args, tc=512)


def kernel(x, c, ctx, c_ctx, ada_w, ada_b, norm1_g, norm2_g, attn_w_in, attn_w_out, attn_q_gain, attn_k_gain,
           attn_lq1, attn_lk1, attn_lq2, attn_lk2, attn_sub_gain, pool_w_in, pool_w_group, pool_scale, pool_w_out,
           router_w, router_b, moe_w_gate, moe_w_up, moe_w_down):
    b, l, d = x.shape
    n_ctx = ctx.shape[1]
    depth = ada_w.shape[0]
    assert depth == 2 and d == N_HEADS * V_DIM
    tm = 512

    mod = _adaln_mod(c, c_ctx, ada_w, ada_b)
    mods = [[mod[i, :b, None, j * d:(j + 1) * d] for j in range(N_MOD)] for i in range(depth)]
    mod_ctx = [jnp.broadcast_to(mod[0, b, j * d:(j + 1) * d], (b, 1, d)) for j in range(2)]

    rwt = router_w.T
    rwh = rwt.astype(BF16)
    rwl = (rwt - rwh.astype(F32)).astype(BF16)
    rb = router_b.reshape(N_EXPERTS, 1)

    sh1, s1, g1, sh2, s2, g2 = mods[0]
    cos, sin = _rope_tables(l)
    pair_up = (jnp.arange(LANES) & 16) == 0

    def gain_rows(g, factor):
        g2 = jnp.concatenate([g, g]) * factor
        return [g2, jnp.where(pair_up, jnp.roll(g2, -16), jnp.roll(g2, 16))]

    q_scale = HEAD_DIM ** -0.5 * math.log2(math.e)
    q_max = jnp.maximum(jnp.max(jnp.abs(attn_q_gain[0])) * (HEAD_DIM ** 0.5 * q_scale), F32_TINY)
    k_max = jnp.maximum(jnp.max(jnp.abs(attn_k_gain[0])) * HEAD_DIM ** 0.5, F32_TINY)
    need = jnp.ceil(jnp.log2(k_max / F8_MAX))
    room = jnp.floor(jnp.log2(F8_MAX / q_max))
    trade = jnp.exp2(jnp.clip(jnp.clip(0.0, need, jnp.maximum(need, room)), -60.0, 60.0))
    h_max = jnp.float32(0.0)
    for shift_, scale_ in ((sh1, s1), (mod_ctx[0], mod_ctx[1])):
        h_max = jnp.maximum(h_max, jnp.max(d ** 0.5 * jnp.max(jnp.abs(norm1_g[0] * (1.0 + scale_)), axis=-1)
                                           + jnp.sqrt(jnp.sum(shift_ * shift_, axis=-1))))
    w_v = attn_w_in[0][:, 2 * d:]
    v_max = 1.02 * h_max * jnp.sqrt(jnp.max(jnp.sum(w_v * w_v, axis=0)))
    v_grow = jnp.exp2(jnp.clip(jnp.ceil(jnp.log2(jnp.maximum(v_max, F32_TINY) / F8_MAX)), 0.0, 60.0))

    gains = jnp.stack(gain_rows(attn_q_gain[0], q_scale * trade) + gain_rows(attn_k_gain[0], 1.0 / trade))
    w_in = jnp.concatenate([attn_w_in[0][:, :2 * d], w_v / v_grow], axis=1).astype(BF16)
    assert l % tm == 0 and l % n_ctx == 0
    q, k_all, vt_all = _qkv_proj(x, sh1, s1, norm1_g[0][None], w_in, gains, cos, sin, n_qk=2, rope=True,
                                 tm=tm, n_keys=l + n_ctx, key_row0=0)
    k_all, vt_all = _qkv_proj(ctx, mod_ctx[0], mod_ctx[1], norm1_g[0][None], w_in[:, d:], gains,
                              cos[:n_ctx], sin[:n_ctx], n_qk=1, rope=False,
                              tm=n_ctx, n_keys=l + n_ctx, key_row0=l, kv=(k_all, vt_all))
    lam_init = 0.8 - 0.6 * math.exp(-0.3 * 0)
    lam_params = jnp.stack([attn_lq1[0], attn_lk1[0], attn_lq2[0], attn_lk2[0],
                            jnp.full((HEAD_DIM,), v_grow, F32)])
    o = _diff_attention(q, k_all, vt_all, lam_params, attn_sub_gain[0][:, None], lam_init=lam_init)

    fix = lambda bi, i: (0, 0)
    x1, h2, ridx, gcol, counts = _mixer_tail(
        (o, attn_w_out[0].astype(BF16)),
        (pl.BlockSpec((1, tm, d), lambda bi, i: (bi, i, 0)), pl.BlockSpec((d, d), fix)),
        _attn_tail_kernel, x, g1, sh2, s2, norm2_g[0][None], rwh, rwl, rb, tm=tm)

    sh1b, s1b, g1b, sh2b, s2b, g2b = mods[1]
    x2, u = _moe(h2, ridx, gcol, counts, x1, g2, moe_w_gate, moe_w_up, moe_w_down, 0,
                 pool_args=(sh1b, s1b, norm1_g[1][None], pool_w_in[0].astype(BF16)))
    gd = pool_w_group.shape[2]
    nh = tm // HALO_ROWS
    front_specs = (
        pl.BlockSpec((1, tm, d), lambda bi, i: (bi, i, 0)),
        pl.BlockSpec((1, HALO_ROWS, d), lambda bi, i: (bi, jnp.maximum(i * nh - 1, 0), 0)),
        pl.BlockSpec((1, HALO_ROWS, d), lambda bi, i: (bi, jnp.minimum((i + 1) * nh, l // HALO_ROWS - 1), 0)),
        pl.BlockSpec((len(POOL_WINDOWS), gd, gd), lambda bi, i: (0, 0, 0)),
        pl.BlockSpec((1, d), fix),
        pl.BlockSpec((d, d), fix))
    x3, h2b, ridx_b, gcol_b, counts_b = _mixer_tail(
        (u, u, u, pool_w_group[0].astype(BF16), pool_scale[0][None], pool_w_out[0].astype(BF16)),
        front_specs, functools.partial(_pool_tail_kernel, seq_len=l),
        x2, g1b, sh2b, s2b, norm2_g[1][None], rwh, rwl, rb, tm=tm,
        scratch=[pltpu.VMEM((tm + 2 * POOL_HALO, d), F32)])
    (out,) = _moe(h2b, ridx_b, gcol_b, counts_b, x3, g2b, moe_w_gate, moe_w_up, moe_w_down, 1)
    return out
```

```python
import functools
import math

import jax
import jax.numpy as jnp
import numpy as np
from jax import lax
from jax.experimental import pallas as pl
from jax.experimental.pallas import tpu as pltpu
from jax.experimental.pallas import tpu_sc as plsc

F32 = jnp.float32
BF16 = jnp.bfloat16
F8 = jnp.float8_e4m3fn
F8_MAX = float(jnp.finfo(F8).max)
F32_TINY = float(jnp.finfo(F32).tiny)

LANES = 128
SUBLANES = 8
N_HEADS = 8
HEAD_DIM = 64
V_DIM = 2 * HEAD_DIM
V_ROWS = V_DIM + 32
P_SHIFT = 8.0
GRID_W = 64
ROPE_THETA = 10000.0
NORM_EPS = 1e-6
N_MOD = 6
POOL_WINDOWS = (2, 4, 8, 16)
POOL_HALO = max(POOL_WINDOWS) // 2
HALO_ROWS = 16
N_EXPERTS = 32
N_EXPERT_GROUPS = 4
EXPERTS_PER_GROUP = N_EXPERTS // N_EXPERT_GROUPS
TOP_K = 2
SLOT_ROWS = 512
SC_CORES = 2
SC_SUBCORES = 16
SC_WORKERS = SC_CORES * SC_SUBCORES
SC_WINDOW = 64
ATTN_GROUP_CHUNKS = 1
ATTN_SCORE_AHEAD = 1
VMEM_LIMIT = 48 * 1024 * 1024
NT_DIMS = (((1,), (1,)), ((), ()))


def _params(*sem):
    return pltpu.CompilerParams(dimension_semantics=sem, vmem_limit_bytes=VMEM_LIMIT)


def _rms_mod(x, gain, shift, scale):
    h = x * lax.rsqrt(jnp.mean(x * x, axis=-1, keepdims=True) + NORM_EPS) * gain
    return h * (1.0 + scale) + shift


def _pack_bf16_pairs(x):
    c = x.shape[1] // 2
    hi = lax.bitcast_convert_type(x[:, :c].astype(BF16).astype(F32), jnp.uint32)
    lo = lax.bitcast_convert_type(x[:, c:].astype(BF16).astype(F32), jnp.uint32)
    return lax.bitcast_convert_type(hi | (lo >> 16), jnp.int32)


def _unpack_bf16_pairs(w):
    u = lax.bitcast_convert_type(w, jnp.uint32)
    hi = lax.bitcast_convert_type(u & jnp.uint32(0xFFFF0000), F32)
    lo = lax.bitcast_convert_type(u << 16, F32)
    return jnp.concatenate([hi, lo], axis=1)


def _mod_kernel(c_ref, w_ref, b_ref, o_ref):
    c = c_ref[...]
    a = c / (1.0 + jnp.exp(-c))
    o_ref[0] = jnp.dot(a, w_ref[0], precision=lax.Precision.HIGHEST,
                       preferred_element_type=F32) + b_ref[0]


def _adaln_mod(c, c_ctx, ada_w, ada_b):
    depth, d, n_out = ada_w.shape
    b = c.shape[0]
    assert b + 1 <= SUBLANES
    rows = jnp.concatenate([c, c_ctx[None], jnp.zeros((SUBLANES - b - 1, d), F32)], axis=0)
    tn = n_out // 4
    return pl.pallas_call(
        _mod_kernel,
        grid=(depth, n_out // tn),
        in_specs=[pl.BlockSpec((SUBLANES, d), lambda i, j: (0, 0)),
                  pl.BlockSpec((1, d, tn), lambda i, j: (i, 0, j)),
                  pl.BlockSpec((1, 1, tn), lambda i, j: (i, 0, j))],
        out_specs=pl.BlockSpec((1, SUBLANES, tn), lambda i, j: (i, 0, j)),
        out_shape=jax.ShapeDtypeStruct((depth, SUBLANES, n_out), F32),
        compiler_params=_params("arbitrary", "arbitrary"),
        name="adaln_mod",
    )(rows, ada_w, ada_b.reshape(depth, 1, n_out))


def _qkv_kernel(x_ref, sh_ref, sc_ref, g_ref, w_ref, gains_ref, cos_ref, sin_ref, *refs, n_qk, rope):
    out_refs = refs[-(n_qk + 1):]
    tm, d = x_ref.shape[1:]
    hb = _rms_mod(x_ref[0], g_ref[...], sh_ref[0], sc_ref[0]).astype(BF16)
    lane_b4 = (lax.broadcasted_iota(jnp.int32, (1, LANES), 1) & 16) == 0
    chunk_r = lax.broadcasted_iota(jnp.int32, (2 * LANES, 2 * LANES), 0) // HEAD_DIM
    chunk_c = lax.broadcasted_iota(jnp.int32, (2 * LANES, 2 * LANES), 1) // HEAD_DIM
    same_chunk = (chunk_r == chunk_c).astype(BF16)
    gains = gains_ref[...]
    for t in range(n_qk):
        r = 2 * (t + 2 - n_qk)
        if rope:
            cos_t = cos_ref[...] * gains[r:r + 1]
            sin_t = sin_ref[...] * gains[r + 1:r + 2]
        for j in range(0, d, 2 * LANES):
            acc = jnp.dot(hb, w_ref[:, t * d + j:t * d + j + 2 * LANES], preferred_element_type=F32)
            ssq = jnp.dot((acc * acc).astype(BF16), same_chunk, preferred_element_type=F32)
            nrm = acc * lax.rsqrt(ssq * (1.0 / HEAD_DIM) + NORM_EPS)
            for half in range(2):
                blk = nrm[:, half * LANES:(half + 1) * LANES]
                if rope:
                    rot = jnp.where(lane_b4, pltpu.roll(blk, LANES - 16, 1), pltpu.roll(blk, 16, 1))
                    y = blk * cos_t + rot * sin_t
                else:
                    y = blk * gains[r:r + 1]
                c0 = j + half * LANES
                if t < n_qk - 1:
                    yt = y.T
                    first = lax.broadcasted_iota(jnp.int32, (V_DIM, 1), 0) < HEAD_DIM
                    out_refs[t][0, c0 // LANES, 0, :, 0:tm] = jnp.where(first, yt, 0.0).astype(F8)
                    out_refs[t][0, c0 // LANES, 0, :, tm:2 * tm] = jnp.where(first, 0.0, yt).astype(F8)
                else:
                    out_refs[t][0, :, c0:c0 + LANES] = y.astype(F8)
    vt_ref = out_refs[n_qk]
    ones = jnp.where(lax.broadcasted_iota(jnp.int32, (V_ROWS - V_DIM, tm), 0) == 0, 1.0, 0.0).astype(F8)
    for j in range(0, d, 2 * LANES):
        acc = jnp.dot(hb, w_ref[:, n_qk * d + j:n_qk * d + j + 2 * LANES], preferred_element_type=F32)
        acc_t = acc.T
        for half in range(2):
            h = j // LANES + half
            vt_ref[0, h, 0:V_DIM, :] = acc_t[half * V_DIM:(half + 1) * V_DIM].astype(F8)
            vt_ref[0, h, V_DIM:V_ROWS, :] = ones


def _qkv_proj(x, shift, scale, gain, w, gains, cos, sin, *, n_qk, rope, tm, n_keys, key_row0, kv=None):
    b, l, d = x.shape
    n_out = n_qk + 1
    kern = functools.partial(_qkv_kernel, n_qk=n_qk, rope=rope)
    row = lambda bi, i: (bi, 0, 0)
    fix = lambda bi, i: (0, 0)
    kb = key_row0 // tm
    tile = pl.BlockSpec((1, tm, d), lambda bi, i: (bi, i, 0))
    k_spec = pl.BlockSpec((1, tm, d), lambda bi, i: (bi, kb + i, 0))
    vt_spec = pl.BlockSpec((1, N_HEADS, V_ROWS, tm), lambda bi, i: (bi, 0, 0, kb + i))
    q_spec = pl.BlockSpec((1, N_HEADS, 1, V_DIM, 2 * tm), lambda bi, i: (bi, 0, i, 0, 0))
    q_shape = jax.ShapeDtypeStruct((b, N_HEADS, l // tm, V_DIM, 2 * tm), F8)
    k_shape = jax.ShapeDtypeStruct((b, n_keys, d), F8)
    vt_shape = jax.ShapeDtypeStruct((b, N_HEADS, V_ROWS, n_keys), F8)
    in_specs = [tile,
                pl.BlockSpec((1, 1, d), row), pl.BlockSpec((1, 1, d), row),
                pl.BlockSpec((1, d), fix),
                pl.BlockSpec((d, n_out * d), fix),
                pl.BlockSpec((4, LANES), fix),
                pl.BlockSpec((tm, LANES), lambda bi, i: (i, 0)),
                pl.BlockSpec((tm, LANES), lambda bi, i: (i, 0))]
    args = [x, shift, scale, gain, w, gains, cos, sin]
    aliases = {}
    if kv is not None:
        aliases = {len(args): n_qk - 1, len(args) + 1: n_qk}
        in_specs += [pl.BlockSpec(memory_space=pl.ANY)] * 2
        args += list(kv)
    return pl.pallas_call(
        kern,
        grid=(b, l // tm),
        in_specs=in_specs,
        out_specs=[q_spec] * (n_qk - 1) + [k_spec, vt_spec],
        out_shape=[q_shape] * (n_qk - 1) + [k_shape, vt_shape],
        input_output_aliases=aliases,
        compiler_params=_params("arbitrary", "arbitrary"),
        name="qkv_proj",
    )(*args)


def _rope_tables(n_tokens):
    rows = n_tokens // GRID_W
    row = np.repeat(np.arange(rows, dtype=np.float32), GRID_W)
    col = np.tile(np.arange(GRID_W, dtype=np.float32), rows)
    half = HEAD_DIM // 2
    inv_freq = (np.float32(ROPE_THETA) ** (-np.arange(0, half, 2, dtype=np.float32) / half)).astype(np.float32)
    ang_r = row[:, None] * inv_freq
    ang_c = col[:, None] * inv_freq
    ang = np.concatenate([ang_r, ang_r, ang_c, ang_c] * 2, axis=-1)
    sign = np.where((np.arange(LANES) & 16) == 0, -1.0, 1.0).astype(np.float32)
    return jnp.asarray(np.cos(ang), F32), jnp.asarray(np.sin(ang) * sign, F32)


def _attn_kernel(q_ref, k_ref, vt_ref, lp_ref, sg_ref, o_ref, s_ref, *, tk, group, ahead, lam_init):
    tq = q_ref.shape[4] // 2
    n_chunks = k_ref.shape[1] // tk
    qz = q_ref[0, 0, 0]
    n_slots = (ahead + 1) * group

    def score_chunk(c, m_grp):
        st = jnp.dot(k_ref[0, c * tk:(c + 1) * tk, :], qz, preferred_element_type=F32).astype(BF16)
        slot = c % n_slots
        s_ref[slot * tk:(slot + 1) * tk, :] = st
        mc = jnp.max(st, axis=0, keepdims=True)
        return mc if m_grp is None else jnp.maximum(m_grp, mc)

    def value_chunk(c, m_ref, part):
        slot = c % n_slots
        p = jnp.exp2(s_ref[slot * tk:(slot + 1) * tk, :] - (m_ref - P_SHIFT)).astype(F8)
        pv = jnp.dot(vt_ref[0, 0, :, c * tk:(c + 1) * tk], p, preferred_element_type=F32)
        return pv if part is None else part + pv

    groups = [list(range(g0, min(g0 + group, n_chunks))) for g0 in range(0, n_chunks, group)]
    m_of = {}
    for g in range(min(ahead, len(groups))):
        for c in groups[g]:
            m_of[g] = score_chunk(c, m_of.get(g))
    m = None
    acc = None
    for gi, cur in enumerate(groups):
        nxt = groups[gi + ahead] if gi + ahead < len(groups) else []
        m_new = m_of[gi] if m is None else jnp.maximum(m, m_of[gi])
        part = None
        for i in range(max(len(cur), len(nxt))):
            if i < len(nxt):
                m_of[gi + ahead] = score_chunk(nxt[i], m_of.get(gi + ahead))
            if i < len(cur):
                part = value_chunk(cur[i], m_new, part)
        acc = part if acc is None else acc * jnp.exp2(m.astype(F32) - m_new.astype(F32)) + part
        m = m_new
    acc = acc[:V_DIM] / acc[V_DIM:V_DIM + 1]
    lp = lp_ref[...]
    lam = (jnp.exp(jnp.sum(lp[0:1] * lp[1:2], axis=-1, keepdims=True))
           - jnp.exp(jnp.sum(lp[2:3] * lp[3:4], axis=-1, keepdims=True)) + lam_init)
    o = (acc[:, :tq] - lam * acc[:, tq:]) * lp[4:5, 0:1]
    o = o * lax.rsqrt(jnp.mean(o * o, axis=0, keepdims=True) + NORM_EPS) * sg_ref[...] * (1.0 - lam_init)
    o_ref[0] = o.T.astype(BF16)


def _attn_chunk(n_keys):
    for tk in (768, 512, 256, 128):
        if n_keys % tk == 0:
            return tk
    raise ValueError(f"key count {n_keys} is not a multiple of {LANES}")


def _diff_attention(qz, k_all, vt_all, lam_params, sub_gain, *, lam_init):
    b, _, n_tiles, _, tq2 = qz.shape
    tq = tq2 // 2
    l, d = n_tiles * tq, N_HEADS * V_DIM
    n_keys = k_all.shape[1]
    tk = _attn_chunk(n_keys)
    kern = functools.partial(_attn_kernel, tk=tk, group=ATTN_GROUP_CHUNKS, ahead=ATTN_SCORE_AHEAD, lam_init=lam_init)
    return pl.pallas_call(
        kern,
        grid=(b, N_HEADS, l // tq),
        in_specs=[pl.BlockSpec((1, 1, 1, V_DIM, 2 * tq), lambda bi, h, i: (bi, h, i, 0, 0)),
                  pl.BlockSpec((1, n_keys, V_DIM), lambda bi, h, i: (bi, 0, h)),
                  pl.BlockSpec((1, 1, V_ROWS, n_keys), lambda bi, h, i: (bi, h, 0, 0)),
                  pl.BlockSpec((5, HEAD_DIM), lambda bi, h, i: (0, 0)),
                  pl.BlockSpec((V_DIM, 1), lambda bi, h, i: (0, 0))],
        out_specs=pl.BlockSpec((1, tq, V_DIM), lambda bi, h, i: (bi, i, h)),
        out_shape=jax.ShapeDtypeStruct((b, l, d), BF16),
        scratch_shapes=[pltpu.VMEM(((ATTN_SCORE_AHEAD + 1) * ATTN_GROUP_CHUNKS * tk, 2 * tq), BF16)],
        compiler_params=_params("arbitrary", "arbitrary", "arbitrary"),
        name="diff_attention",
    )(qz, k_all, vt_all, lam_params, sub_gain)


def _route(h2, rwh_ref, rwl_ref, rb_ref, before_ref, carry_ref, ridx_ref, gcol_ref, cnt_ref, is_first):
    tm = h2.shape[0]
    hh = h2.astype(BF16)
    hl = (h2 - hh.astype(F32)).astype(BF16)
    rw2 = jnp.concatenate([rwh_ref[...], rwl_ref[...]], axis=0)
    part = lax.dot_general(rw2, hh, NT_DIMS, preferred_element_type=F32)
    logits = (part[:N_EXPERTS] + part[N_EXPERTS:]
              + lax.dot_general(rwh_ref[...], hl, NT_DIMS, preferred_element_type=F32) + rb_ref[...])
    groups = [logits[g * EXPERTS_PER_GROUP:(g + 1) * EXPERTS_PER_GROUP] for g in range(N_EXPERT_GROUPS)]
    top = groups[0]
    for g in range(1, N_EXPERT_GROUPS):
        top = jnp.maximum(top, groups[g])
    top = jnp.max(top, axis=0, keepdims=True)
    sub = lax.broadcasted_iota(jnp.int32, (EXPERTS_PER_GROUP, tm), 0)
    best = None
    for g in range(N_EXPERT_GROUPS):
        ex = jnp.exp(groups[g] - top)
        v1 = jnp.max(ex, axis=0, keepdims=True)
        i1 = jnp.min(jnp.where(ex == v1, sub, EXPERTS_PER_GROUP), axis=0, keepdims=True)
        rest = jnp.where(sub == i1, -1.0, ex)
        v2 = jnp.max(rest, axis=0, keepdims=True)
        i2 = jnp.min(jnp.where(rest == v2, sub, EXPERTS_PER_GROUP), axis=0, keepdims=True)
        cand = (v1 + v2, v1, v2, i1 + g * EXPERTS_PER_GROUP, i2 + g * EXPERTS_PER_GROUP)
        if best is None:
            best = cand
        else:
            better = cand[0] > best[0]
            best = tuple(jnp.where(better, new, old) for new, old in zip(cand, best))
    _, v1, v2, e0, e1 = best
    gate0 = v1 / (v1 + v2)
    gate1 = v2 / (v1 + v2)

    @pl.when(is_first)
    def _():
        carry_ref[...] = jnp.zeros_like(carry_ref)

    erow = lax.broadcasted_iota(jnp.int32, (N_EXPERTS, tm), 0)
    oh0 = erow == e0
    oh1 = erow == e1
    chosen = jnp.where(oh0 | oh1, 1.0, 0.0)
    prior = carry_ref[:, 0:1] + jnp.dot(chosen.astype(BF16), before_ref[...], preferred_element_type=F32)
    r0 = jnp.sum(jnp.where(oh0, prior, 0.0), axis=0, keepdims=True).astype(jnp.int32)
    r1 = jnp.sum(jnp.where(oh1, prior, 0.0), axis=0, keepdims=True).astype(jnp.int32)
    carry_ref[...] = carry_ref[...] + jnp.sum(chosen, axis=1, keepdims=True)
    cnt_ref[...] = carry_ref[...]
    rid = lax.broadcasted_iota(jnp.int32, (SUBLANES, tm), 0)
    ridx_ref[...] = jnp.where(rid == 0, e0, jnp.where(rid == 1, e1, jnp.where(rid == 2, r0, jnp.where(rid == 3, r1, 0))))
    gid = lax.broadcasted_iota(jnp.int32, (LANES, tm), 0)
    gcol_ref[...] = jnp.where(gid == 0, gate0, jnp.where(gid == 1, gate1, 0.0)).T


def _tail(y, x_ref, g1_ref, sh2_ref, s2_ref, n2_ref, rwh_ref, rwl_ref, rb_ref, before_ref,
          x_out, h2_out, ridx_ref, gcol_ref, cnt_ref, carry_ref):
    x1 = x_ref[0] + g1_ref[0] * y
    x_out[0] = x1
    h2 = _rms_mod(x1, n2_ref[...], sh2_ref[0], s2_ref[0])
    h2_out[0] = _pack_bf16_pairs(h2)
    is_first = (pl.program_id(0) == 0) & (pl.program_id(1) == 0)
    _route(h2, rwh_ref, rwl_ref, rb_ref, before_ref, carry_ref, ridx_ref, gcol_ref, cnt_ref, is_first)


def _attn_tail_kernel(a_ref, wo_ref, *rest):
    y = jnp.dot(a_ref[0], wo_ref[...], preferred_element_type=F32)
    _tail(y, *rest)


def _pool_tail_kernel(u_ref, up_ref, un_ref, wg_ref, cs_ref, wo_ref, *rest, seq_len):
    *tail_refs, ubuf, abuf = rest
    tm = u_ref.shape[1]
    i = pl.program_id(1)
    u = u_ref[0].astype(F32)
    ubuf[0:POOL_HALO] = jnp.where(i > 0, up_ref[0].astype(F32)[HALO_ROWS - POOL_HALO:], 0.0)
    ubuf[POOL_HALO:POOL_HALO + tm] = u
    ubuf[POOL_HALO + tm:2 * POOL_HALO + tm] = jnp.where(i < pl.num_programs(1) - 1,
                                                        un_ref[0].astype(F32)[:POOL_HALO], 0.0)
    pos = i * tm + lax.broadcasted_iota(jnp.int32, (tm, 1), 0)
    gd = wg_ref.shape[1]
    assert all(win == 2 ** (g + 1) for g, win in enumerate(POOL_WINDOWS))
    n_ext = tm + 2 * POOL_HALO
    bufs = (ubuf, abuf)
    abuf[1:n_ext, :] = ubuf[0:n_ext - 1, :] + ubuf[1:n_ext, :]
    lo, hi = 1, n_ext
    for g in range(1, len(POOL_WINDOWS)):
        src, dst = bufs[g % 2], bufs[(g + 1) % 2]
        sh = POOL_WINDOWS[g] // 4
        dst[lo + sh:hi - sh, g * gd:] = src[lo:hi - 2 * sh, g * gd:] + src[lo + 2 * sh:hi, g * gd:]
        lo, hi = lo + sh, hi - sh
    outs = []
    for g, win in enumerate(POOL_WINDOWS):
        half = win // 2
        cols = slice(g * gd, (g + 1) * gd)
        s = bufs[(g + 1) % 2][POOL_HALO:POOL_HALO + tm, cols]
        inv_cnt = 1.0 / (jnp.minimum(pos + half, seq_len) - jnp.maximum(pos - half, 0)).astype(F32)
        dlt = (s * inv_cnt - u[:, cols]).astype(BF16)
        outs.append(jnp.dot(dlt, wg_ref[g], preferred_element_type=F32))
    z = (jnp.concatenate(outs, axis=-1) * cs_ref[...]).astype(BF16)
    y = jnp.dot(z, wo_ref[...], preferred_element_type=F32)
    _tail(y, *tail_refs)


def _mixer_tail(front_args, front_specs, kern, x, g1, sh2, s2, n2g, rwh, rwl, rb, *, tm, scratch=()):
    b, l, d = x.shape
    nt = l // tm
    n = b * l
    row = lambda bi, i: (bi, 0, 0)
    fix = lambda bi, i: (0, 0)
    tile = lambda bi, i: (bi, i, 0)
    in_specs = list(front_specs) + [
        pl.BlockSpec((1, tm, d), tile),
        pl.BlockSpec((1, 1, d), row), pl.BlockSpec((1, 1, d), row), pl.BlockSpec((1, 1, d), row),
        pl.BlockSpec((1, d), fix),
        pl.BlockSpec((N_EXPERTS, d), fix), pl.BlockSpec((N_EXPERTS, d), fix),
        pl.BlockSpec((N_EXPERTS, 1), fix),
        pl.BlockSpec((tm, tm), fix)]
    before = jnp.asarray(np.triu(np.ones((tm, tm), np.float32), k=1), BF16)
    out_specs = [pl.BlockSpec((1, tm, d), tile), pl.BlockSpec((1, tm, d // 2), tile),
                 pl.BlockSpec((SUBLANES, tm), lambda bi, i: (0, bi * nt + i)),
                 pl.BlockSpec((tm, LANES), lambda bi, i: (bi * nt + i, 0)),
                 pl.BlockSpec((N_EXPERTS, LANES), fix)]
    out_shape = [jax.ShapeDtypeStruct((b, l, d), F32), jax.ShapeDtypeStruct((b, l, d // 2), jnp.int32),
                 jax.ShapeDtypeStruct((SUBLANES, n), jnp.int32), jax.ShapeDtypeStruct((n, LANES), F32),
                 jax.ShapeDtypeStruct((N_EXPERTS, LANES), F32)]
    return pl.pallas_call(
        kern,
        grid=(b, nt),
        in_specs=in_specs, out_specs=out_specs, out_shape=out_shape,
        scratch_shapes=[pltpu.VMEM((N_EXPERTS, LANES), F32)] + list(scratch),
        compiler_params=_params("arbitrary", "arbitrary"),
        name="mixer_tail",
    )(*front_args, x, g1, sh2, s2, n2g, rwh, rwl, rb, before)


def _slot_kernel(ps_ref, ridx_ref, dest_ref):
    ridx = ridx_ref[...]
    ps = ps_ref[...]
    erow = lax.broadcasted_iota(jnp.int32, (N_EXPERTS, ridx.shape[1]), 0)
    rows = []
    for k in range(TOP_K):
        start = jnp.sum(jnp.where(erow == ridx[k:k + 1], ps, 0), axis=0, keepdims=True)
        rows.append(start + ridx[TOP_K + k:TOP_K + k + 1])
    rid = lax.broadcasted_iota(jnp.int32, ridx.shape, 0)
    dest_ref[...] = jnp.where(rid == 0, rows[0], jnp.where(rid == 1, rows[1], 0))


def _slot_index(pad_start, ridx, *, tn):
    n = ridx.shape[1]
    return pl.pallas_call(
        _slot_kernel,
        grid=(n // tn,),
        in_specs=[pl.BlockSpec((N_EXPERTS, 1), lambda i: (0, 0)),
                  pl.BlockSpec((SUBLANES, tn), lambda i: (0, i))],
        out_specs=pl.BlockSpec((SUBLANES, tn), lambda i: (0, i)),
        out_shape=jax.ShapeDtypeStruct((SUBLANES, n), jnp.int32),
        compiler_params=_params("arbitrary"),
        name="slot_index",
    )(pad_start.reshape(N_EXPERTS, 1), ridx)


def _sc_mesh():
    return plsc.VectorSubcoreMesh(core_axis_name="c", subcore_axis_name="s",
                                  num_cores=SC_CORES, num_subcores=SC_SUBCORES)


def _sc_worker_base(per_worker):
    return (lax.axis_index("s") * SC_CORES + lax.axis_index("c")) * per_worker


def _sc_scatter_rows(rows, idx0, idx1, n_slots):
    n, d = rows.shape
    per_worker = n // SC_WORKERS
    assert per_worker % SC_WINDOW == 0

    n_win = per_worker // SC_WINDOW
    assert n_win % 2 == 0

    def body(rows_hbm, i0_hbm, i1_hbm, out_hbm, i0_a, i1_a, rows_a, i0_b, i1_b, rows_b, sem_a, sem_b):
        base = _sc_worker_base(per_worker)

        def offset(j):
            return pl.multiple_of(base + j * SC_WINDOW, SC_WINDOW)

        def start(j, i0_v, i1_v, rows_v, sem):
            pltpu.sync_copy(i0_hbm.at[pl.ds(offset(j), SC_WINDOW)], i0_v)
            pltpu.sync_copy(i1_hbm.at[pl.ds(offset(j), SC_WINDOW)], i1_v)
            pltpu.async_copy(rows_hbm.at[pl.ds(offset(j), SC_WINDOW)], rows_v, sem)

        def finish(j, i0_v, i1_v, rows_v, sem):
            pltpu.make_async_copy(rows_hbm.at[pl.ds(offset(j), SC_WINDOW)], rows_v, sem).wait()
            pltpu.sync_copy(rows_v, out_hbm.at[i0_v])
            pltpu.sync_copy(rows_v, out_hbm.at[i1_v])

        start(0, i0_a, i1_a, rows_a, sem_a)

        @pl.loop(0, n_win, step=2)
        def _(j):
            start(j + 1, i0_b, i1_b, rows_b, sem_b)
            finish(j, i0_a, i1_a, rows_a, sem_a)

            @pl.when(j + 2 < n_win)
            def _():
                start(j + 2, i0_a, i1_a, rows_a, sem_a)

            finish(j + 1, i0_b, i1_b, rows_b, sem_b)

    window = [pltpu.VMEM((SC_WINDOW,), jnp.int32), pltpu.VMEM((SC_WINDOW,), jnp.int32),
              pltpu.VMEM((SC_WINDOW, d), rows.dtype)]
    return pl.kernel(
        body, out_type=jax.ShapeDtypeStruct((n_slots, d), rows.dtype), mesh=_sc_mesh(),
        scratch_types=window + window + [pltpu.SemaphoreType.DMA, pltpu.SemaphoreType.DMA],
        name="sc_scatter_rows",
    )(rows, idx0, idx1)


def _sc_gather_rows(table, idx):
    n = idx.shape[0]
    d = table.shape[1]
    per_worker = n // SC_WORKERS
    assert per_worker % SC_WINDOW == 0

    n_win = per_worker // SC_WINDOW
    assert n_win % 2 == 0

    def body(table_hbm, idx_hbm, out_hbm, idx_a, idx_b, rows_a, rows_b, sem_a, sem_b):
        base = _sc_worker_base(per_worker)

        def offset(j):
            return pl.multiple_of(base + j * SC_WINDOW, SC_WINDOW)

        def start(j, idx_v, rows_v, sem):
            pltpu.sync_copy(idx_hbm.at[pl.ds(offset(j), SC_WINDOW)], idx_v)
            pltpu.async_copy(table_hbm.at[idx_v], rows_v, sem)

        def finish(j, idx_v, rows_v, sem):
            pltpu.make_async_copy(table_hbm.at[idx_v], rows_v, sem).wait()
            pltpu.sync_copy(rows_v, out_hbm.at[pl.ds(offset(j), SC_WINDOW)])

        start(0, idx_a, rows_a, sem_a)

        @pl.loop(0, n_win, step=2)
        def _(j):
            start(j + 1, idx_b, rows_b, sem_b)
            finish(j, idx_a, rows_a, sem_a)

            @pl.when(j + 2 < n_win)
            def _():
                start(j + 2, idx_a, rows_a, sem_a)

            finish(j + 1, idx_b, rows_b, sem_b)

    return pl.kernel(
        body, out_type=jax.ShapeDtypeStruct((n, d), table.dtype), mesh=_sc_mesh(),
        scratch_types=[pltpu.VMEM((SC_WINDOW,), jnp.int32), pltpu.VMEM((SC_WINDOW,), jnp.int32),
                       pltpu.VMEM((SC_WINDOW, d), table.dtype), pltpu.VMEM((SC_WINDOW, d), table.dtype),
                       pltpu.SemaphoreType.DMA, pltpu.SemaphoreType.DMA],
        name="sc_gather_rows",
    )(table, idx)


def _ffn_kernel(be_ref, nv_ref, xs_ref, wg_ref, wu_ref, wd_ref, ys_ref, wg_b, wu_b, wd_b):
    j = pl.program_id(0)
    valid = nv_ref[j]

    @pl.when((valid > 0) & ((j == 0) | (be_ref[j] != be_ref[jnp.maximum(j - 1, 0)])))
    def _():
        wg_b[...] = wg_ref[0, 0].astype(BF16)
        wu_b[...] = wu_ref[0, 0].astype(BF16)
        wd_b[...] = wd_ref[0, 0].astype(BF16)

    @pl.when(valid > 0)
    def _():
        row = lax.broadcasted_iota(jnp.int32, (SLOT_ROWS, 1), 0)
        xw = jnp.where(row < valid, xs_ref[...], 0)
        xb = _unpack_bf16_pairs(xw).astype(BF16)
        g = jnp.dot(xb, wg_b[...], preferred_element_type=F32)
        u = jnp.dot(xb, wu_b[...], preferred_element_type=F32)
        a = (g / (1.0 + jnp.exp(-g)) * u).astype(BF16)
        ys_ref[...] = _pack_bf16_pairs(jnp.dot(a, wd_b[...], preferred_element_type=F32))

    @pl.when(valid <= 0)
    def _():
        ys_ref[...] = jnp.zeros_like(ys_ref)


def _expert_ffn(block_e, n_valid, xs, w_gate, w_up, w_down, layer):
    n_slots, dw = xs.shape
    d, de = w_gate.shape[2:]
    w_idx = lambda j, be, nv: (layer, be[j], 0, 0)
    return pl.pallas_call(
        _ffn_kernel,
        grid_spec=pltpu.PrefetchScalarGridSpec(
            num_scalar_prefetch=2,
            grid=(n_slots // SLOT_ROWS,),
            in_specs=[pl.BlockSpec((SLOT_ROWS, dw), lambda j, be, nv: (j, 0)),
                      pl.BlockSpec((1, 1, d, de), w_idx),
                      pl.BlockSpec((1, 1, d, de), w_idx),
                      pl.BlockSpec((1, 1, de, d), w_idx)],
            out_specs=pl.BlockSpec((SLOT_ROWS, dw), lambda j, be, nv: (j, 0)),
            scratch_shapes=[pltpu.VMEM((d, de), BF16), pltpu.VMEM((d, de), BF16), pltpu.VMEM((de, d), BF16)]),
        out_shape=jax.ShapeDtypeStruct((n_slots, dw), jnp.int32),
        compiler_params=_params("arbitrary"),
        name="expert_ffn",
    )(block_e, n_valid, xs, w_gate, w_up, w_down)


def _combine_kernel(y0_ref, y1_ref, gcol_ref, x_ref, g2_ref, *rest, pool_in):
    if pool_in:
        sh_ref, sc_ref, n1_ref, wi_ref, x_out, u_out = rest
    else:
        (x_out,) = rest
    gc = gcol_ref[...]
    out = gc[:, 0:1] * _unpack_bf16_pairs(y0_ref[0]) + gc[:, 1:2] * _unpack_bf16_pairs(y1_ref[0])
    x2 = x_ref[0] + g2_ref[0] * out
    x_out[0] = x2
    if pool_in:
        hb = _rms_mod(x2, n1_ref[...], sh_ref[0], sc_ref[0]).astype(BF16)
        u_out[0] = jnp.dot(hb, wi_ref[...], preferred_element_type=F32).astype(BF16)


def _combine(yg, gcol, x, g2, pool_args=None, *, tc):
    b, l, d = x.shape
    nt = l // tc
    pool_in = pool_args is not None
    kern = functools.partial(_combine_kernel, pool_in=pool_in)
    row = lambda bi, i: (bi, 0, 0)
    fix = lambda bi, i: (0, 0)
    tile = lambda bi, i: (bi, i, 0)
    in_specs = [pl.BlockSpec((1, tc, d // 2), lambda bi, i: (0, bi * nt + i, 0)),
                pl.BlockSpec((1, tc, d // 2), lambda bi, i: (1, bi * nt + i, 0)),
                pl.BlockSpec((tc, LANES), lambda bi, i: (bi * nt + i, 0)),
                pl.BlockSpec((1, tc, d), tile),
                pl.BlockSpec((1, 1, d), row)]
    out_specs = [pl.BlockSpec((1, tc, d), tile)]
    out_shape = [jax.ShapeDtypeStruct((b, l, d), F32)]
    args = [yg, yg, gcol, x, g2]
    if pool_in:
        in_specs += [pl.BlockSpec((1, 1, d), row), pl.BlockSpec((1, 1, d), row),
                     pl.BlockSpec((1, d), fix), pl.BlockSpec((d, d), fix)]
        out_specs.append(pl.BlockSpec((1, tc, d), tile))
        out_shape.append(jax.ShapeDtypeStruct((b, l, d), BF16))
        args += list(pool_args)
    return pl.pallas_call(
        kern,
        grid=(b, nt),
        in_specs=in_specs, out_specs=out_specs, out_shape=out_shape,
        compiler_params=_params("arbitrary", "arbitrary"),
        name="moe_combine",
    )(*args)


def _moe(h2, ridx, gcol, counts, x1, g2, w_gate, w_up, w_down, layer, pool_args=None):
    b, l, d = x1.shape
    n = b * l
    n_blocks = (n * TOP_K) // SLOT_ROWS + N_EXPERTS
    cnt = counts[:, 0].astype(jnp.int32)
    padded = (cnt + SLOT_ROWS - 1) // SLOT_ROWS * SLOT_ROWS
    earlier = jnp.arange(N_EXPERTS)[None, :] < jnp.arange(N_EXPERTS)[:, None]
    pad_start = jnp.sum(jnp.where(earlier, padded[None, :], 0), axis=1).astype(jnp.int32)
    pad_end = pad_start + padded
    block_start = jnp.arange(n_blocks, dtype=jnp.int32) * SLOT_ROWS
    block_e = jnp.minimum(jnp.sum(pad_end[None, :] <= block_start[:, None], axis=1), N_EXPERTS - 1).astype(jnp.int32)
    own = block_e[:, None] == jnp.arange(N_EXPERTS)[None, :]
    data_end = jnp.sum(jnp.where(own, (pad_start + cnt)[None, :], 0), axis=1)
    n_valid = jnp.clip(data_end - block_start, 0, SLOT_ROWS).astype(jnp.int32)
    dest = _slot_index(pad_start, ridx, tn=2048)
    xs = _sc_scatter_rows(h2.reshape(n, d // 2), dest[0], dest[1], n_blocks * SLOT_ROWS)
    ys = _expert_ffn(block_e, n_valid, xs, w_gate, w_up, w_down, layer)
    yg = _sc_gather_rows(ys, dest[:TOP_K].reshape(TOP_K * n)).reshape(TOP_K, n, d // 2)
    return _combine(yg, gcol, x1, g2, pool_args, tc=512)


def kernel(x, c, ctx, c_ctx, ada_w, ada_b, norm1_g, norm2_g, attn_w_in, attn_w_out, attn_q_gain, attn_k_gain,
           attn_lq1, attn_lk1, attn_lq2, attn_lk2, attn_sub_gain, pool_w_in, pool_w_group, pool_scale, pool_w_out,
           router_w, router_b, moe_w_gate, moe_w_up, moe_w_down):
    b, l, d = x.shape
    n_ctx = ctx.shape[1]
    depth = ada_w.shape[0]
    assert depth == 2 and d == N_HEADS * V_DIM
    tm = 512

    mod = _adaln_mod(c, c_ctx, ada_w, ada_b)
    mods = [[mod[i, :b, None, j * d:(j + 1) * d] for j in range(N_MOD)] for i in range(depth)]
    mod_ctx = [jnp.broadcast_to(mod[0, b, j * d:(j + 1) * d], (b, 1, d)) for j in range(2)]

    rwt = router_w.T
    rwh = rwt.astype(BF16)
    rwl = (rwt - rwh.astype(F32)).astype(BF16)
    rb = router_b.reshape(N_EXPERTS, 1)

    sh1, s1, g1, sh2, s2, g2 = mods[0]
    cos, sin = _rope_tables(l)
    pair_up = (jnp.arange(LANES) & 16) == 0

    def gain_rows(g, factor):
        g2 = jnp.concatenate([g, g]) * factor
        return [g2, jnp.where(pair_up, jnp.roll(g2, -16), jnp.roll(g2, 16))]

    q_scale = HEAD_DIM ** -0.5 * math.log2(math.e)
    q_max = jnp.maximum(jnp.max(jnp.abs(attn_q_gain[0])) * (HEAD_DIM ** 0.5 * q_scale), F32_TINY)
    k_max = jnp.maximum(jnp.max(jnp.abs(attn_k_gain[0])) * HEAD_DIM ** 0.5, F32_TINY)
    need = jnp.ceil(jnp.log2(k_max / F8_MAX))
    room = jnp.floor(jnp.log2(F8_MAX / q_max))
    trade = jnp.exp2(jnp.clip(jnp.clip(0.0, need, jnp.maximum(need, room)), -60.0, 60.0))
    h_max = jnp.float32(0.0)
    for shift_, scale_ in ((sh1, s1), (mod_ctx[0], mod_ctx[1])):
        h_max = jnp.maximum(h_max, jnp.max(d ** 0.5 * jnp.max(jnp.abs(norm1_g[0] * (1.0 + scale_)), axis=-1)
                                           + jnp.sqrt(jnp.sum(shift_ * shift_, axis=-1))))
    w_v = attn_w_in[0][:, 2 * d:]
    v_max = 1.02 * h_max * jnp.sqrt(jnp.max(jnp.sum(w_v * w_v, axis=0)))
    v_grow = jnp.exp2(jnp.clip(jnp.ceil(jnp.log2(jnp.maximum(v_max, F32_TINY) / F8_MAX)), 0.0, 60.0))

    gains = jnp.stack(gain_rows(attn_q_gain[0], q_scale * trade) + gain_rows(attn_k_gain[0], 1.0 / trade))
    w_in = jnp.concatenate([attn_w_in[0][:, :2 * d], w_v / v_grow], axis=1).astype(BF16)
    assert l % tm == 0 and l % n_ctx == 0
    q, k_all, vt_all = _qkv_proj(x, sh1, s1, norm1_g[0][None], w_in, gains, cos, sin, n_qk=2, rope=True,
                                 tm=tm, n_keys=l + n_ctx, key_row0=0)
    k_all, vt_all = _qkv_proj(ctx, mod_ctx[0], mod_ctx[1], norm1_g[0][None], w_in[:, d:], gains,
                              cos[:n_ctx], sin[:n_ctx], n_qk=1, rope=False,
                              tm=n_ctx, n_keys=l + n_ctx, key_row0=l, kv=(k_all, vt_all))
    lam_init = 0.8 - 0.6 * math.exp(-0.3 * 0)
    lam_params = jnp.stack([attn_lq1[0], attn_lk1[0], attn_lq2[0], attn_lk2[0],
                            jnp.full((HEAD_DIM,), v_grow, F32)])
    o = _diff_attention(q, k_all, vt_all, lam_params, attn_sub_gain[0][:, None], lam_init=lam_init)

    fix = lambda bi, i: (0, 0)
    x1, h2, ridx, gcol, counts = _mixer_tail(
        (o, attn_w_out[0].astype(BF16)),
        (pl.BlockSpec((1, tm, d), lambda bi, i: (bi, i, 0)), pl.BlockSpec((d, d), fix)),
        _attn_tail_kernel, x, g1, sh2, s2, norm2_g[0][None], rwh, rwl, rb, tm=tm)

    sh1b, s1b, g1b, sh2b, s2b, g2b = mods[1]
    x2, u = _moe(h2, ridx, gcol, counts, x1, g2, moe_w_gate, moe_w_up, moe_w_down, 0,
                 pool_args=(sh1b, s1b, norm1_g[1][None], pool_w_in[0].astype(BF16)))
    gd = pool_w_group.shape[2]
    nh = tm // HALO_ROWS
    front_specs = (
        pl.BlockSpec((1, tm, d), lambda bi, i: (bi, i, 0)),
        pl.BlockSpec((1, HALO_ROWS, d), lambda bi, i: (bi, jnp.maximum(i * nh - 1, 0), 0)),
        pl.BlockSpec((1, HALO_ROWS, d), lambda bi, i: (bi, jnp.minimum((i + 1) * nh, l // HALO_ROWS - 1), 0)),
        pl.BlockSpec((len(POOL_WINDOWS), gd, gd), lambda bi, i: (0, 0, 0)),
        pl.BlockSpec((1, d), fix),
        pl.BlockSpec((d, d), fix))
    x3, h2b, ridx_b, gcol_b, counts_b = _mixer_tail(
        (u, u, u, pool_w_group[0].astype(BF16), pool_scale[0][None], pool_w_out[0].astype(BF16)),
        front_specs, functools.partial(_pool_tail_kernel, seq_len=l),
        x2, g1b, sh2b, s2b, norm2_g[1][None], rwh, rwl, rb, tm=tm,
        scratch=[pltpu.VMEM((tm + 2 * POOL_HALO, d), F32)] * 2)
    (out,) = _moe(h2b, ridx_b, gcol_b, counts_b, x3, g2b, moe_w_gate, moe_w_up, moe_w_down, 1)
    return out
```

```python
import functools
import math

import jax
import jax.numpy as jnp
import numpy as np
from jax import lax
from jax.experimental import pallas as pl
from jax.experimental.pallas import tpu as pltpu
from jax.experimental.pallas import tpu_sc as plsc

F32 = jnp.float32
BF16 = jnp.bfloat16
F8 = jnp.float8_e4m3fn
F8_MAX = float(jnp.finfo(F8).max)
F32_TINY = float(jnp.finfo(F32).tiny)

LANES = 128
SUBLANES = 8
N_HEADS = 8
HEAD_DIM = 64
V_DIM = 2 * HEAD_DIM
V_ROWS = V_DIM + 32
P_SHIFT = 8.0
GRID_W = 64
ROPE_THETA = 10000.0
NORM_EPS = 1e-6
N_MOD = 6
POOL_WINDOWS = (2, 4, 8, 16)
POOL_HALO = max(POOL_WINDOWS) // 2
HALO_ROWS = 16
N_EXPERTS = 32
N_EXPERT_GROUPS = 4
EXPERTS_PER_GROUP = N_EXPERTS // N_EXPERT_GROUPS
TOP_K = 2
SLOT_ROWS = 512
SC_CORES = 2
SC_SUBCORES = 16
SC_WORKERS = SC_CORES * SC_SUBCORES
SC_WINDOW = 64
ATTN_GROUP_CHUNKS = 1
ATTN_SCORE_AHEAD = 1
VMEM_LIMIT = 48 * 1024 * 1024
NT_DIMS = (((1,), (1,)), ((), ()))


def _params(*sem):
    return pltpu.CompilerParams(dimension_semantics=sem, vmem_limit_bytes=VMEM_LIMIT)


def _rms_mod(x, gain, shift, scale):
    h = x * lax.rsqrt(jnp.mean(x * x, axis=-1, keepdims=True) + NORM_EPS) * gain
    return h * (1.0 + scale) + shift


def _pack_bf16_pairs(x):
    c = x.shape[1] // 2
    hi = lax.bitcast_convert_type(x[:, :c].astype(BF16).astype(F32), jnp.uint32)
    lo = lax.bitcast_convert_type(x[:, c:].astype(BF16).astype(F32), jnp.uint32)
    return lax.bitcast_convert_type(hi | (lo >> 16), jnp.int32)


def _unpack_bf16_pairs(w):
    u = lax.bitcast_convert_type(w, jnp.uint32)
    hi = lax.bitcast_convert_type(u & jnp.uint32(0xFFFF0000), F32)
    lo = lax.bitcast_convert_type(u << 16, F32)
    return jnp.concatenate([hi, lo], axis=1)


def _mod_kernel(c_ref, w_ref, b_ref, o_ref):
    c = c_ref[...]
    a = c / (1.0 + jnp.exp(-c))
    o_ref[0] = jnp.dot(a, w_ref[0], precision=lax.Precision.HIGHEST,
                       preferred_element_type=F32) + b_ref[0]


def _adaln_mod(c, c_ctx, ada_w, ada_b):
    depth, d, n_out = ada_w.shape
    b = c.shape[0]
    assert b + 1 <= SUBLANES
    rows = jnp.concatenate([c, c_ctx[None], jnp.zeros((SUBLANES - b - 1, d), F32)], axis=0)
    tn = n_out // 4
    return pl.pallas_call(
        _mod_kernel,
        grid=(depth, n_out // tn),
        in_specs=[pl.BlockSpec((SUBLANES, d), lambda i, j: (0, 0)),
                  pl.BlockSpec((1, d, tn), lambda i, j: (i, 0, j)),
                  pl.BlockSpec((1, 1, tn), lambda i, j: (i, 0, j))],
        out_specs=pl.BlockSpec((1, SUBLANES, tn), lambda i, j: (i, 0, j)),
        out_shape=jax.ShapeDtypeStruct((depth, SUBLANES, n_out), F32),
        compiler_params=_params("arbitrary", "arbitrary"),
        name="adaln_mod",
    )(rows, ada_w, ada_b.reshape(depth, 1, n_out))


def _qkv_kernel(x_ref, sh_ref, sc_ref, g_ref, w_ref, gains_ref, cos_ref, sin_ref, *refs, n_qk, rope):
    out_refs = refs[-(n_qk + 1):]
    tm, d = x_ref.shape[1:]
    hb = _rms_mod(x_ref[0], g_ref[...], sh_ref[0], sc_ref[0]).astype(BF16)
    lane_b4 = (lax.broadcasted_iota(jnp.int32, (1, LANES), 1) & 16) == 0
    chunk_r = lax.broadcasted_iota(jnp.int32, (2 * LANES, 2 * LANES), 0) // HEAD_DIM
    chunk_c = lax.broadcasted_iota(jnp.int32, (2 * LANES, 2 * LANES), 1) // HEAD_DIM
    same_chunk = (chunk_r == chunk_c).astype(BF16)
    gains = gains_ref[...]
    for t in range(n_qk):
        r = 2 * (t + 2 - n_qk)
        if rope:
            cos_t = cos_ref[...] * gains[r:r + 1]
            sin_t = sin_ref[...] * gains[r + 1:r + 2]
        for j in range(0, d, 2 * LANES):
            acc = jnp.dot(hb, w_ref[:, t * d + j:t * d + j + 2 * LANES], preferred_element_type=F32)
            ssq = jnp.dot((acc * acc).astype(BF16), same_chunk, preferred_element_type=F32)
            nrm = acc * lax.rsqrt(ssq * (1.0 / HEAD_DIM) + NORM_EPS)
            for half in range(2):
                blk = nrm[:, half * LANES:(half + 1) * LANES]
                if rope:
                    rot = jnp.where(lane_b4, pltpu.roll(blk, LANES - 16, 1), pltpu.roll(blk, 16, 1))
                    y = blk * cos_t + rot * sin_t
                else:
                    y = blk * gains[r:r + 1]
                c0 = j + half * LANES
                if t < n_qk - 1:
                    yt = y.T
                    first = lax.broadcasted_iota(jnp.int32, (V_DIM, 1), 0) < HEAD_DIM
                    out_refs[t][0, c0 // LANES, 0, :, 0:tm] = jnp.where(first, yt, 0.0).astype(F8)
                    out_refs[t][0, c0 // LANES, 0, :, tm:2 * tm] = jnp.where(first, 0.0, yt).astype(F8)
                else:
                    out_refs[t][0, :, c0:c0 + LANES] = y.astype(F8)
    vt_ref = out_refs[n_qk]
    ones = jnp.where(lax.broadcasted_iota(jnp.int32, (V_ROWS - V_DIM, tm), 0) == 0, 1.0, 0.0).astype(F8)
    for j in range(0, d, 2 * LANES):
        acc = jnp.dot(hb, w_ref[:, n_qk * d + j:n_qk * d + j + 2 * LANES], preferred_element_type=F32)
        acc_t = acc.T
        for half in range(2):
            h = j // LANES + half
            vt_ref[0, h, 0:V_DIM, :] = acc_t[half * V_DIM:(half + 1) * V_DIM].astype(F8)
            vt_ref[0, h, V_DIM:V_ROWS, :] = ones


def _qkv_proj(x, shift, scale, gain, w, gains, cos, sin, *, n_qk, rope, tm, n_keys, key_row0, kv=None):
    b, l, d = x.shape
    n_out = n_qk + 1
    kern = functools.partial(_qkv_kernel, n_qk=n_qk, rope=rope)
    row = lambda bi, i: (bi, 0, 0)
    fix = lambda bi, i: (0, 0)
    kb = key_row0 // tm
    tile = pl.BlockSpec((1, tm, d), lambda bi, i: (bi, i, 0))
    k_spec = pl.BlockSpec((1, tm, d), lambda bi, i: (bi, kb + i, 0))
    vt_spec = pl.BlockSpec((1, N_HEADS, V_ROWS, tm), lambda bi, i: (bi, 0, 0, kb + i))
    q_spec = pl.BlockSpec((1, N_HEADS, 1, V_DIM, 2 * tm), lambda bi, i: (bi, 0, i, 0, 0))
    q_shape = jax.ShapeDtypeStruct((b, N_HEADS, l // tm, V_DIM, 2 * tm), F8)
    k_shape = jax.ShapeDtypeStruct((b, n_keys, d), F8)
    vt_shape = jax.ShapeDtypeStruct((b, N_HEADS, V_ROWS, n_keys), F8)
    in_specs = [tile,
                pl.BlockSpec((1, 1, d), row), pl.BlockSpec((1, 1, d), row),
                pl.BlockSpec((1, d), fix),
                pl.BlockSpec((d, n_out * d), fix),
                pl.BlockSpec((4, LANES), fix),
                pl.BlockSpec((tm, LANES), lambda bi, i: (i, 0)),
                pl.BlockSpec((tm, LANES), lambda bi, i: (i, 0))]
    args = [x, shift, scale, gain, w, gains, cos, sin]
    aliases = {}
    if kv is not None:
        aliases = {len(args): n_qk - 1, len(args) + 1: n_qk}
        in_specs += [pl.BlockSpec(memory_space=pl.ANY)] * 2
        args += list(kv)
    return pl.pallas_call(
        kern,
        grid=(b, l // tm),
        in_specs=in_specs,
        out_specs=[q_spec] * (n_qk - 1) + [k_spec, vt_spec],
        out_shape=[q_shape] * (n_qk - 1) + [k_shape, vt_shape],
        input_output_aliases=aliases,
        compiler_params=_params("arbitrary", "arbitrary"),
        name="qkv_proj",
    )(*args)


def _rope_tables(n_tokens):
    rows = n_tokens // GRID_W
    row = np.repeat(np.arange(rows, dtype=np.float32), GRID_W)
    col = np.tile(np.arange(GRID_W, dtype=np.float32), rows)
    half = HEAD_DIM // 2
    inv_freq = (np.float32(ROPE_THETA) ** (-np.arange(0, half, 2, dtype=np.float32) / half)).astype(np.float32)
    ang_r = row[:, None] * inv_freq
    ang_c = col[:, None] * inv_freq
    ang = np.concatenate([ang_r, ang_r, ang_c, ang_c] * 2, axis=-1)
    sign = np.where((np.arange(LANES) & 16) == 0, -1.0, 1.0).astype(np.float32)
    return jnp.asarray(np.cos(ang), F32), jnp.asarray(np.sin(ang) * sign, F32)


def _attn_kernel(q_ref, k_ref, vt_ref, lp_ref, sg_ref, o_ref, s_ref, *, tk, group, ahead, lam_init):
    tq = q_ref.shape[4] // 2
    n_chunks = k_ref.shape[1] // tk
    qz = q_ref[0, 0, 0]
    n_slots = (ahead + 1) * group

    def score_chunk(c, m_grp):
        st = jnp.dot(k_ref[0, c * tk:(c + 1) * tk, :], qz, preferred_element_type=F32).astype(BF16)
        slot = c % n_slots
        s_ref[slot * tk:(slot + 1) * tk, :] = st
        mc = jnp.max(st, axis=0, keepdims=True)
        return mc if m_grp is None else jnp.maximum(m_grp, mc)

    def value_chunk(c, m_ref, part):
        slot = c % n_slots
        p = jnp.exp2(s_ref[slot * tk:(slot + 1) * tk, :] - (m_ref - P_SHIFT)).astype(F8)
        pv = jnp.dot(vt_ref[0, 0, :, c * tk:(c + 1) * tk], p, preferred_element_type=F32)
        return pv if part is None else part + pv

    groups = [list(range(g0, min(g0 + group, n_chunks))) for g0 in range(0, n_chunks, group)]
    m_of = {}
    for g in range(min(ahead, len(groups))):
        for c in groups[g]:
            m_of[g] = score_chunk(c, m_of.get(g))
    m = None
    acc = None
    for gi, cur in enumerate(groups):
        nxt = groups[gi + ahead] if gi + ahead < len(groups) else []
        m_new = m_of[gi] if m is None else jnp.maximum(m, m_of[gi])
        part = None
        for i in range(max(len(cur), len(nxt))):
            if i < len(nxt):
                m_of[gi + ahead] = score_chunk(nxt[i], m_of.get(gi + ahead))
            if i < len(cur):
                part = value_chunk(cur[i], m_new, part)
        acc = part if acc is None else acc * jnp.exp2(m.astype(F32) - m_new.astype(F32)) + part
        m = m_new
    acc = acc[:V_DIM] / acc[V_DIM:V_DIM + 1]
    lp = lp_ref[...]
    lam = (jnp.exp(jnp.sum(lp[0:1] * lp[1:2], axis=-1, keepdims=True))
           - jnp.exp(jnp.sum(lp[2:3] * lp[3:4], axis=-1, keepdims=True)) + lam_init)
    o = (acc[:, :tq] - lam * acc[:, tq:]) * lp[4:5, 0:1]
    o = o * lax.rsqrt(jnp.mean(o * o, axis=0, keepdims=True) + NORM_EPS) * sg_ref[...] * (1.0 - lam_init)
    o_ref[0] = o.T.astype(BF16)


def _attn_chunk(n_keys):
    for tk in (768, 512, 256, 128):
        if n_keys % tk == 0:
            return tk
    raise ValueError(f"key count {n_keys} is not a multiple of {LANES}")


def _diff_attention(qz, k_all, vt_all, lam_params, sub_gain, *, lam_init):
    b, _, n_tiles, _, tq2 = qz.shape
    tq = tq2 // 2
    l, d = n_tiles * tq, N_HEADS * V_DIM
    n_keys = k_all.shape[1]
    tk = _attn_chunk(n_keys)
    kern = functools.partial(_attn_kernel, tk=tk, group=ATTN_GROUP_CHUNKS, ahead=ATTN_SCORE_AHEAD, lam_init=lam_init)
    return pl.pallas_call(
        kern,
        grid=(b, N_HEADS, l // tq),
        in_specs=[pl.BlockSpec((1, 1, 1, V_DIM, 2 * tq), lambda bi, h, i: (bi, h, i, 0, 0)),
                  pl.BlockSpec((1, n_keys, V_DIM), lambda bi, h, i: (bi, 0, h)),
                  pl.BlockSpec((1, 1, V_ROWS, n_keys), lambda bi, h, i: (bi, h, 0, 0)),
                  pl.BlockSpec((5, HEAD_DIM), lambda bi, h, i: (0, 0)),
                  pl.BlockSpec((V_DIM, 1), lambda bi, h, i: (0, 0))],
        out_specs=pl.BlockSpec((1, tq, V_DIM), lambda bi, h, i: (bi, i, h)),
        out_shape=jax.ShapeDtypeStruct((b, l, d), BF16),
        scratch_shapes=[pltpu.VMEM(((ATTN_SCORE_AHEAD + 1) * ATTN_GROUP_CHUNKS * tk, 2 * tq), BF16)],
        compiler_params=_params("arbitrary", "arbitrary", "arbitrary"),
        name="diff_attention",
    )(qz, k_all, vt_all, lam_params, sub_gain)


def _route(h2, rwh_ref, rwl_ref, rb_ref, before_ref, carry_ref, ridx_ref, gcol_ref, cnt_ref, is_first):
    tm = h2.shape[0]
    hh = h2.astype(BF16)
    hl = (h2 - hh.astype(F32)).astype(BF16)
    rw2 = jnp.concatenate([rwh_ref[...], rwl_ref[...]], axis=0)
    part = lax.dot_general(rw2, hh, NT_DIMS, preferred_element_type=F32)
    logits = (part[:N_EXPERTS] + part[N_EXPERTS:]
              + lax.dot_general(rwh_ref[...], hl, NT_DIMS, preferred_element_type=F32) + rb_ref[...])
    groups = [logits[g * EXPERTS_PER_GROUP:(g + 1) * EXPERTS_PER_GROUP] for g in range(N_EXPERT_GROUPS)]
    top = groups[0]
    for g in range(1, N_EXPERT_GROUPS):
        top = jnp.maximum(top, groups[g])
    top = jnp.max(top, axis=0, keepdims=True)
    sub = lax.broadcasted_iota(jnp.int32, (EXPERTS_PER_GROUP, tm), 0)
    best = None
    for g in range(N_EXPERT_GROUPS):
        ex = jnp.exp(groups[g] - top)
        v1 = jnp.max(ex, axis=0, keepdims=True)
        i1 = jnp.min(jnp.where(ex == v1, sub, EXPERTS_PER_GROUP), axis=0, keepdims=True)
        rest = jnp.where(sub == i1, -1.0, ex)
        v2 = jnp.max(rest, axis=0, keepdims=True)
        i2 = jnp.min(jnp.where(rest == v2, sub, EXPERTS_PER_GROUP), axis=0, keepdims=True)
        cand = (v1 + v2, v1, v2, i1 + g * EXPERTS_PER_GROUP, i2 + g * EXPERTS_PER_GROUP)
        if best is None:
            best = cand
        else:
            better = cand[0] > best[0]
            best = tuple(jnp.where(better, new, old) for new, old in zip(cand, best))
    _, v1, v2, e0, e1 = best
    gate0 = v1 / (v1 + v2)
    gate1 = v2 / (v1 + v2)

    @pl.when(is_first)
    def _():
        carry_ref[...] = jnp.zeros_like(carry_ref)

    erow = lax.broadcasted_iota(jnp.int32, (N_EXPERTS, tm), 0)
    oh0 = erow == e0
    oh1 = erow == e1
    chosen = jnp.where(oh0 | oh1, 1.0, 0.0)
    prior = carry_ref[:, 0:1] + jnp.dot(chosen.astype(BF16), before_ref[...], preferred_element_type=F32)
    r0 = jnp.sum(jnp.where(oh0, prior, 0.0), axis=0, keepdims=True).astype(jnp.int32)
    r1 = jnp.sum(jnp.where(oh1, prior, 0.0), axis=0, keepdims=True).astype(jnp.int32)
    carry_ref[...] = carry_ref[...] + jnp.sum(chosen, axis=1, keepdims=True)
    cnt_ref[...] = carry_ref[...]
    rid = lax.broadcasted_iota(jnp.int32, (SUBLANES, tm), 0)
    ridx_ref[...] = jnp.where(rid == 0, e0, jnp.where(rid == 1, e1, jnp.where(rid == 2, r0, jnp.where(rid == 3, r1, 0))))
    gid = lax.broadcasted_iota(jnp.int32, (LANES, tm), 0)
    gcol_ref[...] = jnp.where(gid == 0, gate0, jnp.where(gid == 1, gate1, 0.0)).T


def _tail(y, x_ref, g1_ref, sh2_ref, s2_ref, n2_ref, rwh_ref, rwl_ref, rb_ref, before_ref,
          x_out, h2_out, ridx_ref, gcol_ref, cnt_ref, carry_ref):
    x1 = x_ref[0] + g1_ref[0] * y
    x_out[0] = x1
    h2 = _rms_mod(x1, n2_ref[...], sh2_ref[0], s2_ref[0])
    h2_out[0] = _pack_bf16_pairs(h2)
    is_first = (pl.program_id(0) == 0) & (pl.program_id(1) == 0)
    _route(h2, rwh_ref, rwl_ref, rb_ref, before_ref, carry_ref, ridx_ref, gcol_ref, cnt_ref, is_first)


def _attn_tail_kernel(a_ref, wo_ref, *rest):
    y = jnp.dot(a_ref[0], wo_ref[...], preferred_element_type=F32)
    _tail(y, *rest)


def _pool_tail_kernel(u_ref, up_ref, un_ref, wg_ref, cs_ref, wo_ref, *rest, seq_len):
    *tail_refs, ubuf, abuf = rest
    tm = u_ref.shape[1]
    i = pl.program_id(1)
    u = u_ref[0].astype(F32)
    ubuf[0:POOL_HALO] = jnp.where(i > 0, up_ref[0].astype(F32)[HALO_ROWS - POOL_HALO:], 0.0)
    ubuf[POOL_HALO:POOL_HALO + tm] = u
    ubuf[POOL_HALO + tm:2 * POOL_HALO + tm] = jnp.where(i < pl.num_programs(1) - 1,
                                                        un_ref[0].astype(F32)[:POOL_HALO], 0.0)
    pos = i * tm + lax.broadcasted_iota(jnp.int32, (tm, 1), 0)
    gd = wg_ref.shape[1]
    assert all(win == 2 ** (g + 1) for g, win in enumerate(POOL_WINDOWS))
    n_ext = tm + 2 * POOL_HALO
    bufs = (ubuf, abuf)
    abuf[1:n_ext, :] = ubuf[0:n_ext - 1, :] + ubuf[1:n_ext, :]
    lo, hi = 1, n_ext
    for g in range(1, len(POOL_WINDOWS)):
        src, dst = bufs[g % 2], bufs[(g + 1) % 2]
        sh = POOL_WINDOWS[g] // 4
        dst[lo + sh:hi - sh, g * gd:] = src[lo:hi - 2 * sh, g * gd:] + src[lo + 2 * sh:hi, g * gd:]
        lo, hi = lo + sh, hi - sh
    outs = []
    for g, win in enumerate(POOL_WINDOWS):
        half = win // 2
        cols = slice(g * gd, (g + 1) * gd)
        s = bufs[(g + 1) % 2][POOL_HALO:POOL_HALO + tm, cols]
        inv_cnt = 1.0 / (jnp.minimum(pos + half, seq_len) - jnp.maximum(pos - half, 0)).astype(F32)
        dlt = (s * inv_cnt - u[:, cols]).astype(BF16)
        outs.append(jnp.dot(dlt, wg_ref[g], preferred_element_type=F32))
    z = (jnp.concatenate(outs, axis=-1) * cs_ref[...]).astype(BF16)
    y = jnp.dot(z, wo_ref[...], preferred_element_type=F32)
    _tail(y, *tail_refs)


def _mixer_tail(front_args, front_specs, kern, x, g1, sh2, s2, n2g, rwh, rwl, rb, *, tm, scratch=()):
    b, l, d = x.shape
    nt = l // tm
    n = b * l
    row = lambda bi, i: (bi, 0, 0)
    fix = lambda bi, i: (0, 0)
    tile = lambda bi, i: (bi, i, 0)
    in_specs = list(front_specs) + [
        pl.BlockSpec((1, tm, d), tile),
        pl.BlockSpec((1, 1, d), row), pl.BlockSpec((1, 1, d), row), pl.BlockSpec((1, 1, d), row),
        pl.BlockSpec((1, d), fix),
        pl.BlockSpec((N_EXPERTS, d), fix), pl.BlockSpec((N_EXPERTS, d), fix),
        pl.BlockSpec((N_EXPERTS, 1), fix),
        pl.BlockSpec((tm, tm), fix)]
    before = jnp.asarray(np.triu(np.ones((tm, tm), np.float32), k=1), BF16)
    out_specs = [pl.BlockSpec((1, tm, d), tile), pl.BlockSpec((1, tm, d // 2), tile),
                 pl.BlockSpec((SUBLANES, tm), lambda bi, i: (0, bi * nt + i)),
                 pl.BlockSpec((tm, LANES), lambda bi, i: (bi * nt + i, 0)),
                 pl.BlockSpec((N_EXPERTS, LANES), fix)]
    out_shape = [jax.ShapeDtypeStruct((b, l, d), F32), jax.ShapeDtypeStruct((b, l, d // 2), jnp.int32),
                 jax.ShapeDtypeStruct((SUBLANES, n), jnp.int32), jax.ShapeDtypeStruct((n, LANES), F32),
                 jax.ShapeDtypeStruct((N_EXPERTS, LANES), F32)]
    return pl.pallas_call(
        kern,
        grid=(b, nt),
        in_specs=in_specs, out_specs=out_specs, out_shape=out_shape,
        scratch_shapes=[pltpu.VMEM((N_EXPERTS, LANES), F32)] + list(scratch),
        compiler_params=_params("arbitrary", "arbitrary"),
        name="mixer_tail",
    )(*front_args, x, g1, sh2, s2, n2g, rwh, rwl, rb, before)


def _slot_kernel(ps_ref, ridx_ref, dest_ref):
    ridx = ridx_ref[...]
    ps = ps_ref[...]
    erow = lax.broadcasted_iota(jnp.int32, (N_EXPERTS, ridx.shape[1]), 0)
    rows = []
    for k in range(TOP_K):
        start = jnp.sum(jnp.where(erow == ridx[k:k + 1], ps, 0), axis=0, keepdims=True)
        rows.append(start + ridx[TOP_K + k:TOP_K + k + 1])
    rid = lax.broadcasted_iota(jnp.int32, ridx.shape, 0)
    dest_ref[...] = jnp.where(rid == 0, rows[0], jnp.where(rid == 1, rows[1], 0))


def _slot_index(pad_start, ridx, *, tn):
    n = ridx.shape[1]
    return pl.pallas_call(
        _slot_kernel,
        grid=(n // tn,),
        in_specs=[pl.BlockSpec((N_EXPERTS, 1), lambda i: (0, 0)),
                  pl.BlockSpec((SUBLANES, tn), lambda i: (0, i))],
        out_specs=pl.BlockSpec((SUBLANES, tn), lambda i: (0, i)),
        out_shape=jax.ShapeDtypeStruct((SUBLANES, n), jnp.int32),
        compiler_params=_params("arbitrary"),
        name="slot_index",
    )(pad_start.reshape(N_EXPERTS, 1), ridx)


def _sc_mesh():
    return plsc.VectorSubcoreMesh(core_axis_name="c", subcore_axis_name="s",
                                  num_cores=SC_CORES, num_subcores=SC_SUBCORES)


def _sc_worker_base(per_worker):
    return (lax.axis_index("s") * SC_CORES + lax.axis_index("c")) * per_worker


def _sc_scatter_rows(rows, idx0, idx1, n_slots):
    n, d = rows.shape
    per_worker = n // SC_WORKERS
    assert per_worker % SC_WINDOW == 0

    n_win = per_worker // SC_WINDOW
    assert n_win % 2 == 0

    def body(rows_hbm, i0_hbm, i1_hbm, out_hbm, i0_a, i1_a, rows_a, i0_b, i1_b, rows_b, sem_a, sem_b):
        base = _sc_worker_base(per_worker)

        def offset(j):
            return pl.multiple_of(base + j * SC_WINDOW, SC_WINDOW)

        def start(j, i0_v, i1_v, rows_v, sem):
            pltpu.sync_copy(i0_hbm.at[pl.ds(offset(j), SC_WINDOW)], i0_v)
            pltpu.sync_copy(i1_hbm.at[pl.ds(offset(j), SC_WINDOW)], i1_v)
            pltpu.async_copy(rows_hbm.at[pl.ds(offset(j), SC_WINDOW)], rows_v, sem)

        def finish(j, i0_v, i1_v, rows_v, sem):
            pltpu.make_async_copy(rows_hbm.at[pl.ds(offset(j), SC_WINDOW)], rows_v, sem).wait()
            pltpu.sync_copy(rows_v, out_hbm.at[i0_v])
            pltpu.sync_copy(rows_v, out_hbm.at[i1_v])

        start(0, i0_a, i1_a, rows_a, sem_a)

        @pl.loop(0, n_win, step=2)
        def _(j):
            start(j + 1, i0_b, i1_b, rows_b, sem_b)
            finish(j, i0_a, i1_a, rows_a, sem_a)

            @pl.when(j + 2 < n_win)
            def _():
                start(j + 2, i0_a, i1_a, rows_a, sem_a)

            finish(j + 1, i0_b, i1_b, rows_b, sem_b)

    window = [pltpu.VMEM((SC_WINDOW,), jnp.int32), pltpu.VMEM((SC_WINDOW,), jnp.int32),
              pltpu.VMEM((SC_WINDOW, d), rows.dtype)]
    return pl.kernel(
        body, out_type=jax.ShapeDtypeStruct((n_slots, d), rows.dtype), mesh=_sc_mesh(),
        scratch_types=window + window + [pltpu.SemaphoreType.DMA, pltpu.SemaphoreType.DMA],
        name="sc_scatter_rows",
    )(rows, idx0, idx1)


def _sc_gather_rows(table, idx):
    n = idx.shape[0]
    d = table.shape[1]
    per_worker = n // SC_WORKERS
    assert per_worker % SC_WINDOW == 0

    n_win = per_worker // SC_WINDOW
    assert n_win % 2 == 0

    def body(table_hbm, idx_hbm, out_hbm, idx_a, idx_b, rows_a, rows_b, sem_a, sem_b):
        base = _sc_worker_base(per_worker)

        def offset(j):
            return pl.multiple_of(base + j * SC_WINDOW, SC_WINDOW)

        def start(j, idx_v, rows_v, sem):
            pltpu.sync_copy(idx_hbm.at[pl.ds(offset(j), SC_WINDOW)], idx_v)
            pltpu.async_copy(table_hbm.at[idx_v], rows_v, sem)

        def finish(j, idx_v, rows_v, sem):
            pltpu.make_async_copy(table_hbm.at[idx_v], rows_v, sem).wait()
            pltpu.sync_copy(rows_v, out_hbm.at[pl.ds(offset(j), SC_WINDOW)])

        start(0, idx_a, rows_a, sem_a)

        @pl.loop(0, n_win, step=2)
        def _(j):
            start(j + 1, idx_b, rows_b, sem_b)
            finish(j, idx_a, rows_a, sem_a)

            @pl.when(j + 2 < n_win)
            def _():
                start(j + 2, idx_a, rows_a, sem_a)

            finish(j + 1, idx_b, rows_b, sem_b)

    return pl.kernel(
        body, out_type=jax.ShapeDtypeStruct((n, d), table.dtype), mesh=_sc_mesh(),
        scratch_types=[pltpu.VMEM((SC_WINDOW,), jnp.int32), pltpu.VMEM((SC_WINDOW,), jnp.int32),
                       pltpu.VMEM((SC_WINDOW, d), table.dtype), pltpu.VMEM((SC_WINDOW, d), table.dtype),
                       pltpu.SemaphoreType.DMA, pltpu.SemaphoreType.DMA],
        name="sc_gather_rows",
    )(table, idx)


def _ffn_kernel(be_ref, nv_ref, xs_ref, wg_ref, wu_ref, wd_ref, ys_ref, wg_b, wu_b, wd_b):
    j = pl.program_id(0)
    valid = nv_ref[j]

    @pl.when((valid > 0) & ((j == 0) | (be_ref[j] != be_ref[jnp.maximum(j - 1, 0)])))
    def _():
        wg_b[...] = wg_ref[0, 0].astype(BF16)
        wu_b[...] = wu_ref[0, 0].astype(BF16)
        wd_b[...] = wd_ref[0, 0].astype(BF16)

    def experts_on(rows):
        row = lax.broadcasted_iota(jnp.int32, (rows, 1), 0)
        xw = jnp.where(row < valid, xs_ref[0:rows, :], 0)
        xb = _unpack_bf16_pairs(xw).astype(BF16)
        g = jnp.dot(xb, wg_b[...], preferred_element_type=F32)
        u = jnp.dot(xb, wu_b[...], preferred_element_type=F32)
        a = (g / (1.0 + jnp.exp(-g)) * u).astype(BF16)
        ys_ref[0:rows, :] = _pack_bf16_pairs(jnp.dot(a, wd_b[...], preferred_element_type=F32))
        if rows < SLOT_ROWS:
            ys_ref[rows:, :] = jnp.zeros((SLOT_ROWS - rows, ys_ref.shape[1]), ys_ref.dtype)

    @pl.when(valid > SLOT_ROWS // 2)
    def _():
        experts_on(SLOT_ROWS)

    @pl.when((valid > 0) & (valid <= SLOT_ROWS // 2))
    def _():
        experts_on(SLOT_ROWS // 2)

    @pl.when(valid <= 0)
    def _():
        ys_ref[...] = jnp.zeros_like(ys_ref)


def _expert_ffn(block_e, n_valid, xs, w_gate, w_up, w_down, layer):
    n_slots, dw = xs.shape
    d, de = w_gate.shape[2:]
    w_idx = lambda j, be, nv: (layer, be[j], 0, 0)
    return pl.pallas_call(
        _ffn_kernel,
        grid_spec=pltpu.PrefetchScalarGridSpec(
            num_scalar_prefetch=2,
            grid=(n_slots // SLOT_ROWS,),
            in_specs=[pl.BlockSpec((SLOT_ROWS, dw), lambda j, be, nv: (j, 0)),
                      pl.BlockSpec((1, 1, d, de), w_idx),
                      pl.BlockSpec((1, 1, d, de), w_idx),
                      pl.BlockSpec((1, 1, de, d), w_idx)],
            out_specs=pl.BlockSpec((SLOT_ROWS, dw), lambda j, be, nv: (j, 0)),
            scratch_shapes=[pltpu.VMEM((d, de), BF16), pltpu.VMEM((d, de), BF16), pltpu.VMEM((de, d), BF16)]),
        out_shape=jax.ShapeDtypeStruct((n_slots, dw), jnp.int32),
        compiler_params=_params("arbitrary"),
        name="expert_ffn",
    )(block_e, n_valid, xs, w_gate, w_up, w_down)


def _combine_kernel(y0_ref, y1_ref, gcol_ref, x_ref, g2_ref, *rest, pool_in):
    if pool_in:
        sh_ref, sc_ref, n1_ref, wi_ref, x_out, u_out = rest
    else:
        (x_out,) = rest
    gc = gcol_ref[...]
    out = gc[:, 0:1] * _unpack_bf16_pairs(y0_ref[0]) + gc[:, 1:2] * _unpack_bf16_pairs(y1_ref[0])
    x2 = x_ref[0] + g2_ref[0] * out
    x_out[0] = x2
    if pool_in:
        hb = _rms_mod(x2, n1_ref[...], sh_ref[0], sc_ref[0]).astype(BF16)
        u_out[0] = jnp.dot(hb, wi_ref[...], preferred_element_type=F32).astype(BF16)


def _combine(yg, gcol, x, g2, pool_args=None, *, tc):
    b, l, d = x.shape
    nt = l // tc
    pool_in = pool_args is not None
    kern = functools.partial(_combine_kernel, pool_in=pool_in)
    row = lambda bi, i: (bi, 0, 0)
    fix = lambda bi, i: (0, 0)
    tile = lambda bi, i: (bi, i, 0)
    in_specs = [pl.BlockSpec((1, tc, d // 2), lambda bi, i: (0, bi * nt + i, 0)),
                pl.BlockSpec((1, tc, d // 2), lambda bi, i: (1, bi * nt + i, 0)),
                pl.BlockSpec((tc, LANES), lambda bi, i: (bi * nt + i, 0)),
                pl.BlockSpec((1, tc, d), tile),
                pl.BlockSpec((1, 1, d), row)]
    out_specs = [pl.BlockSpec((1, tc, d), tile)]
    out_shape = [jax.ShapeDtypeStruct((b, l, d), F32)]
    args = [yg, yg, gcol, x, g2]
    if pool_in:
        in_specs += [pl.BlockSpec((1, 1, d), row), pl.BlockSpec((1, 1, d), row),
                     pl.BlockSpec((1, d), fix), pl.BlockSpec((d, d), fix)]
        out_specs.append(pl.BlockSpec((1, tc, d), tile))
        out_shape.append(jax.ShapeDtypeStruct((b, l, d), BF16))
        args += list(pool_args)
    return pl.pallas_call(
        kern,
        grid=(b, nt),
        in_specs=in_specs, out_specs=out_specs, out_shape=out_shape,
        compiler_params=_params("arbitrary", "arbitrary"),
        name="moe_combine",
    )(*args)


def _moe(h2, ridx, gcol, counts, x1, g2, w_gate, w_up, w_down, layer, pool_args=None):
    b, l, d = x1.shape
    n = b * l
    n_blocks = (n * TOP_K) // SLOT_ROWS + N_EXPERTS
    cnt = counts[:, 0].astype(jnp.int32)
    padded = (cnt + SLOT_ROWS - 1) // SLOT_ROWS * SLOT_ROWS
    earlier = jnp.arange(N_EXPERTS)[None, :] < jnp.arange(N_EXPERTS)[:, None]
    pad_start = jnp.sum(jnp.where(earlier, padded[None, :], 0), axis=1).astype(jnp.int32)
    pad_end = pad_start + padded
    block_start = jnp.arange(n_blocks, dtype=jnp.int32) * SLOT_ROWS
    block_e = jnp.minimum(jnp.sum(pad_end[None, :] <= block_start[:, None], axis=1), N_EXPERTS - 1).astype(jnp.int32)
    own = block_e[:, None] == jnp.arange(N_EXPERTS)[None, :]
    data_end = jnp.sum(jnp.where(own, (pad_start + cnt)[None, :], 0), axis=1)
    n_valid = jnp.clip(data_end - block_start, 0, SLOT_ROWS).astype(jnp.int32)
    dest = _slot_index(pad_start, ridx, tn=2048)
    xs = _sc_scatter_rows(h2.reshape(n, d // 2), dest[0], dest[1], n_blocks * SLOT_ROWS)
    ys = _expert_ffn(block_e, n_valid, xs, w_gate, w_up, w_down, layer)
    yg = _sc_gather_rows(ys, dest[:TOP_K].reshape(TOP_K * n)).reshape(TOP_K, n, d // 2)
    return _combine(yg, gcol, x1, g2, pool_args, tc=512)


def kernel(x, c, ctx, c_ctx, ada_w, ada_b, norm1_g, norm2_g, attn_w_in, attn_w_out, attn_q_gain, attn_k_gain,
           attn_lq1, attn_lk1, attn_lq2, attn_lk2, attn_sub_gain, pool_w_in, pool_w_group, pool_scale, pool_w_out,
           router_w, router_b, moe_w_gate, moe_w_up, moe_w_down):
    b, l, d = x.shape
    n_ctx = ctx.shape[1]
    depth = ada_w.shape[0]
    assert depth == 2 and d == N_HEADS * V_DIM
    tm = 512

    mod = _adaln_mod(c, c_ctx, ada_w, ada_b)
    mods = [[mod[i, :b, None, j * d:(j + 1) * d] for j in range(N_MOD)] for i in range(depth)]
    mod_ctx = [jnp.broadcast_to(mod[0, b, j * d:(j + 1) * d], (b, 1, d)) for j in range(2)]

    rwt = router_w.T
    rwh = rwt.astype(BF16)
    rwl = (rwt - rwh.astype(F32)).astype(BF16)
    rb = router_b.reshape(N_EXPERTS, 1)

    sh1, s1, g1, sh2, s2, g2 = mods[0]
    cos, sin = _rope_tables(l)
    pair_up = (jnp.arange(LANES) & 16) == 0

    def gain_rows(g, factor):
        g2 = jnp.concatenate([g, g]) * factor
        return [g2, jnp.where(pair_up, jnp.roll(g2, -16), jnp.roll(g2, 16))]

    q_scale = HEAD_DIM ** -0.5 * math.log2(math.e)
    q_max = jnp.maximum(jnp.max(jnp.abs(attn_q_gain[0])) * (HEAD_DIM ** 0.5 * q_scale), F32_TINY)
    k_max = jnp.maximum(jnp.max(jnp.abs(attn_k_gain[0])) * HEAD_DIM ** 0.5, F32_TINY)
    need = jnp.ceil(jnp.log2(k_max / F8_MAX))
    room = jnp.floor(jnp.log2(F8_MAX / q_max))
    trade = jnp.exp2(jnp.clip(jnp.clip(0.0, need, jnp.maximum(need, room)), -60.0, 60.0))
    h_max = jnp.float32(0.0)
    for shift_, scale_ in ((sh1, s1), (mod_ctx[0], mod_ctx[1])):
        h_max = jnp.maximum(h_max, jnp.max(d ** 0.5 * jnp.max(jnp.abs(norm1_g[0] * (1.0 + scale_)), axis=-1)
                                           + jnp.sqrt(jnp.sum(shift_ * shift_, axis=-1))))
    w_v = attn_w_in[0][:, 2 * d:]
    v_max = 1.02 * h_max * jnp.sqrt(jnp.max(jnp.sum(w_v * w_v, axis=0)))
    v_grow = jnp.exp2(jnp.clip(jnp.ceil(jnp.log2(jnp.maximum(v_max, F32_TINY) / F8_MAX)), 0.0, 60.0))

    gains = jnp.stack(gain_rows(attn_q_gain[0], q_scale * trade) + gain_rows(attn_k_gain[0], 1.0 / trade))
    w_in = jnp.concatenate([attn_w_in[0][:, :2 * d], w_v / v_grow], axis=1).astype(BF16)
    assert l % tm == 0 and l % n_ctx == 0
    q, k_all, vt_all = _qkv_proj(x, sh1, s1, norm1_g[0][None], w_in, gains, cos, sin, n_qk=2, rope=True,
                                 tm=tm, n_keys=l + n_ctx, key_row0=0)
    k_all, vt_all = _qkv_proj(ctx, mod_ctx[0], mod_ctx[1], norm1_g[0][None], w_in[:, d:], gains,
                              cos[:n_ctx], sin[:n_ctx], n_qk=1, rope=False,
                              tm=n_ctx, n_keys=l + n_ctx, key_row0=l, kv=(k_all, vt_all))
    lam_init = 0.8 - 0.6 * math.exp(-0.3 * 0)
    lam_params = jnp.stack([attn_lq1[0], attn_lk1[0], attn_lq2[0], attn_lk2[0],
                            jnp.full((HEAD_DIM,), v_grow, F32)])
    o = _diff_attention(q, k_all, vt_all, lam_params, attn_sub_gain[0][:, None], lam_init=lam_init)

    fix = lambda bi, i: (0, 0)
    x1, h2, ridx, gcol, counts = _mixer_tail(
        (o, attn_w_out[0].astype(BF16)),
        (pl.BlockSpec((1, tm, d), lambda bi, i: (bi, i, 0)), pl.BlockSpec((d, d), fix)),
        _attn_tail_kernel, x, g1, sh2, s2, norm2_g[0][None], rwh, rwl, rb, tm=tm)

    sh1b, s1b, g1b, sh2b, s2b, g2b = mods[1]
    x2, u = _moe(h2, ridx, gcol, counts, x1, g2, moe_w_gate, moe_w_up, moe_w_down, 0,
                 pool_args=(sh1b, s1b, norm1_g[1][None], pool_w_in[0].astype(BF16)))
    gd = pool_w_group.shape[2]
    nh = tm // HALO_ROWS
    front_specs = (
        pl.BlockSpec((1, tm, d), lambda bi, i: (bi, i, 0)),
        pl.BlockSpec((1, HALO_ROWS, d), lambda bi, i: (bi, jnp.maximum(i * nh - 1, 0), 0)),
        pl.BlockSpec((1, HALO_ROWS, d), lambda bi, i: (bi, jnp.minimum((i + 1) * nh, l // HALO_ROWS - 1), 0)),
        pl.BlockSpec((len(POOL_WINDOWS), gd, gd), lambda bi, i: (0, 0, 0)),
        pl.BlockSpec((1, d), fix),
        pl.BlockSpec((d, d), fix))
    x3, h2b, ridx_b, gcol_b, counts_b = _mixer_tail(
        (u, u, u, pool_w_group[0].astype(BF16), pool_scale[0][None], pool_w_out[0].astype(BF16)),
        front_specs, functools.partial(_pool_tail_kernel, seq_len=l),
        x2, g1b, sh2b, s2b, norm2_g[1][None], rwh, rwl, rb, tm=tm,
        scratch=[pltpu.VMEM((tm + 2 * POOL_HALO, d), F32)] * 2)
    (out,) = _moe(h2b, ridx_b, gcol_b, counts_b, x3, g2b, moe_w_gate, moe_w_up, moe_w_down, 1)
    return out
```

```python
import functools
import math

import jax
import jax.numpy as jnp
import numpy as np
from jax import lax
from jax.experimental import pallas as pl
from jax.experimental.pallas import tpu as pltpu
from jax.experimental.pallas import tpu_sc as plsc

F32 = jnp.float32
BF16 = jnp.bfloat16
F8 = jnp.float8_e4m3fn
F8_MAX = float(jnp.finfo(F8).max)
F32_TINY = float(jnp.finfo(F32).tiny)

LANES = 128
SUBLANES = 8
N_HEADS = 8
HEAD_DIM = 64
V_DIM = 2 * HEAD_DIM
V_ROWS = V_DIM + 32
P_SHIFT = 8.0
GRID_W = 64
ROPE_THETA = 10000.0
NORM_EPS = 1e-6
N_MOD = 6
POOL_WINDOWS = (2, 4, 8, 16)
POOL_HALO = max(POOL_WINDOWS) // 2
HALO_ROWS = 16
N_EXPERTS = 32
N_EXPERT_GROUPS = 4
EXPERTS_PER_GROUP = N_EXPERTS // N_EXPERT_GROUPS
TOP_K = 2
SLOT_ROWS = 512
SC_CORES = 2
SC_SUBCORES = 16
SC_WORKERS = SC_CORES * SC_SUBCORES
SC_WINDOW = 64
ATTN_GROUP_CHUNKS = 1
ATTN_SCORE_AHEAD = 1
VMEM_LIMIT = 48 * 1024 * 1024
NT_DIMS = (((1,), (1,)), ((), ()))


def _params(*sem):
    return pltpu.CompilerParams(dimension_semantics=sem, vmem_limit_bytes=VMEM_LIMIT)


def _rms_mod(x, gain, shift, scale):
    inv_rms = lax.rsqrt(jnp.mean(x * x, axis=-1, keepdims=True) + NORM_EPS)
    return x * inv_rms * (gain * (1.0 + scale)) + shift


def _pack_bf16_pairs(x):
    c = x.shape[1] // 2
    hi = lax.bitcast_convert_type(x[:, :c].astype(BF16).astype(F32), jnp.uint32)
    lo = lax.bitcast_convert_type(x[:, c:].astype(BF16).astype(F32), jnp.uint32)
    return lax.bitcast_convert_type(hi | (lo >> 16), jnp.int32)


def _unpack_bf16_pairs(w):
    u = lax.bitcast_convert_type(w, jnp.uint32)
    hi = lax.bitcast_convert_type(u & jnp.uint32(0xFFFF0000), F32)
    lo = lax.bitcast_convert_type(u << 16, F32)
    return jnp.concatenate([hi, lo], axis=1)


def _mod_kernel(c_ref, w_ref, b_ref, o_ref):
    c = c_ref[...]
    a = c / (1.0 + jnp.exp(-c))
    o_ref[0] = jnp.dot(a, w_ref[0], precision=lax.Precision.HIGHEST,
                       preferred_element_type=F32) + b_ref[0]


def _adaln_mod(c, c_ctx, ada_w, ada_b):
    depth, d, n_out = ada_w.shape
    b = c.shape[0]
    assert b + 1 <= SUBLANES
    rows = jnp.concatenate([c, c_ctx[None], jnp.zeros((SUBLANES - b - 1, d), F32)], axis=0)
    tn = n_out // 4
    return pl.pallas_call(
        _mod_kernel,
        grid=(depth, n_out // tn),
        in_specs=[pl.BlockSpec((SUBLANES, d), lambda i, j: (0, 0)),
                  pl.BlockSpec((1, d, tn), lambda i, j: (i, 0, j)),
                  pl.BlockSpec((1, 1, tn), lambda i, j: (i, 0, j))],
        out_specs=pl.BlockSpec((1, SUBLANES, tn), lambda i, j: (i, 0, j)),
        out_shape=jax.ShapeDtypeStruct((depth, SUBLANES, n_out), F32),
        compiler_params=_params("arbitrary", "arbitrary"),
        name="adaln_mod",
    )(rows, ada_w, ada_b.reshape(depth, 1, n_out))


def _qkv_kernel(x_ref, sh_ref, sc_ref, g_ref, w_ref, gains_ref, cos_ref, sin_ref, *refs, n_qk, rope):
    out_refs = refs[-(n_qk + 1):]
    tm, d = x_ref.shape[1:]
    hb = _rms_mod(x_ref[0], g_ref[...], sh_ref[0], sc_ref[0]).astype(BF16)
    lane_b4 = (lax.broadcasted_iota(jnp.int32, (1, LANES), 1) & 16) == 0
    chunk_r = lax.broadcasted_iota(jnp.int32, (2 * LANES, 2 * LANES), 0) // HEAD_DIM
    chunk_c = lax.broadcasted_iota(jnp.int32, (2 * LANES, 2 * LANES), 1) // HEAD_DIM
    same_chunk = (chunk_r == chunk_c).astype(BF16)
    gains = gains_ref[...]
    for t in range(n_qk):
        r = 2 * (t + 2 - n_qk)
        if rope:
            cos_t = cos_ref[...] * gains[r:r + 1]
            sin_t = sin_ref[...] * gains[r + 1:r + 2]
        for j in range(0, d, 2 * LANES):
            acc = jnp.dot(hb, w_ref[:, t * d + j:t * d + j + 2 * LANES], preferred_element_type=F32)
            ssq = jnp.dot((acc * acc).astype(BF16), same_chunk, preferred_element_type=F32)
            nrm = acc * lax.rsqrt(ssq * (1.0 / HEAD_DIM) + NORM_EPS)
            for half in range(2):
                blk = nrm[:, half * LANES:(half + 1) * LANES]
                if rope:
                    rot = jnp.where(lane_b4, pltpu.roll(blk, LANES - 16, 1), pltpu.roll(blk, 16, 1))
                    y = blk * cos_t + rot * sin_t
                else:
                    y = blk * gains[r:r + 1]
                c0 = j + half * LANES
                if t < n_qk - 1:
                    yt = y.T
                    first = lax.broadcasted_iota(jnp.int32, (V_DIM, 1), 0) < HEAD_DIM
                    out_refs[t][0, c0 // LANES, 0, :, 0:tm] = jnp.where(first, yt, 0.0).astype(F8)
                    out_refs[t][0, c0 // LANES, 0, :, tm:2 * tm] = jnp.where(first, 0.0, yt).astype(F8)
                else:
                    out_refs[t][0, :, c0:c0 + LANES] = y.astype(F8)
    vt_ref = out_refs[n_qk]
    ones = jnp.where(lax.broadcasted_iota(jnp.int32, (V_ROWS - V_DIM, tm), 0) == 0, 1.0, 0.0).astype(F8)
    for j in range(0, d, 2 * LANES):
        acc = jnp.dot(hb, w_ref[:, n_qk * d + j:n_qk * d + j + 2 * LANES], preferred_element_type=F32)
        acc_t = acc.T
        for half in range(2):
            h = j // LANES + half
            vt_ref[0, h, 0:V_DIM, :] = acc_t[half * V_DIM:(half + 1) * V_DIM].astype(F8)
            vt_ref[0, h, V_DIM:V_ROWS, :] = ones


def _qkv_proj(x, shift, scale, gain, w, gains, cos, sin, *, n_qk, rope, tm, n_keys, key_row0, kv=None):
    b, l, d = x.shape
    n_out = n_qk + 1
    kern = functools.partial(_qkv_kernel, n_qk=n_qk, rope=rope)
    row = lambda bi, i: (bi, 0, 0)
    fix = lambda bi, i: (0, 0)
    kb = key_row0 // tm
    tile = pl.BlockSpec((1, tm, d), lambda bi, i: (bi, i, 0))
    k_spec = pl.BlockSpec((1, tm, d), lambda bi, i: (bi, kb + i, 0))
    vt_spec = pl.BlockSpec((1, N_HEADS, V_ROWS, tm), lambda bi, i: (bi, 0, 0, kb + i))
    q_spec = pl.BlockSpec((1, N_HEADS, 1, V_DIM, 2 * tm), lambda bi, i: (bi, 0, i, 0, 0))
    q_shape = jax.ShapeDtypeStruct((b, N_HEADS, l // tm, V_DIM, 2 * tm), F8)
    k_shape = jax.ShapeDtypeStruct((b, n_keys, d), F8)
    vt_shape = jax.ShapeDtypeStruct((b, N_HEADS, V_ROWS, n_keys), F8)
    in_specs = [tile,
                pl.BlockSpec((1, 1, d), row), pl.BlockSpec((1, 1, d), row),
                pl.BlockSpec((1, d), fix),
                pl.BlockSpec((d, n_out * d), fix),
                pl.BlockSpec((4, LANES), fix),
                pl.BlockSpec((tm, LANES), lambda bi, i: (i, 0)),
                pl.BlockSpec((tm, LANES), lambda bi, i: (i, 0))]
    args = [x, shift, scale, gain, w, gains, cos, sin]
    aliases = {}
    if kv is not None:
        aliases = {len(args): n_qk - 1, len(args) + 1: n_qk}
        in_specs += [pl.BlockSpec(memory_space=pl.ANY)] * 2
        args += list(kv)
    return pl.pallas_call(
        kern,
        grid=(b, l // tm),
        in_specs=in_specs,
        out_specs=[q_spec] * (n_qk - 1) + [k_spec, vt_spec],
        out_shape=[q_shape] * (n_qk - 1) + [k_shape, vt_shape],
        input_output_aliases=aliases,
        compiler_params=_params("arbitrary", "arbitrary"),
        name="qkv_proj",
    )(*args)


def _rope_tables(n_tokens):
    rows = n_tokens // GRID_W
    row = np.repeat(np.arange(rows, dtype=np.float32), GRID_W)
    col = np.tile(np.arange(GRID_W, dtype=np.float32), rows)
    half = HEAD_DIM // 2
    inv_freq = (np.float32(ROPE_THETA) ** (-np.arange(0, half, 2, dtype=np.float32) / half)).astype(np.float32)
    ang_r = row[:, None] * inv_freq
    ang_c = col[:, None] * inv_freq
    ang = np.concatenate([ang_r, ang_r, ang_c, ang_c] * 2, axis=-1)
    sign = np.where((np.arange(LANES) & 16) == 0, -1.0, 1.0).astype(np.float32)
    return jnp.asarray(np.cos(ang), F32), jnp.asarray(np.sin(ang) * sign, F32)


def _attn_kernel(q_ref, k_ref, vt_ref, lp_ref, sg_ref, o_ref, s_ref, *, tk, group, ahead, lam_init):
    tq = q_ref.shape[4] // 2
    n_chunks = k_ref.shape[1] // tk
    qz = q_ref[0, 0, 0]
    n_slots = (ahead + 1) * group

    def score_chunk(c, m_grp):
        st = jnp.dot(k_ref[0, c * tk:(c + 1) * tk, :], qz, preferred_element_type=F32).astype(BF16)
        slot = c % n_slots
        s_ref[slot * tk:(slot + 1) * tk, :] = st
        mc = jnp.max(st, axis=0, keepdims=True)
        return mc if m_grp is None else jnp.maximum(m_grp, mc)

    def value_chunk(c, m_ref, part):
        slot = c % n_slots
        p = jnp.exp2(s_ref[slot * tk:(slot + 1) * tk, :] - (m_ref - P_SHIFT)).astype(F8)
        pv = jnp.dot(vt_ref[0, 0, :, c * tk:(c + 1) * tk], p, preferred_element_type=F32)
        return pv if part is None else part + pv

    groups = [list(range(g0, min(g0 + group, n_chunks))) for g0 in range(0, n_chunks, group)]
    m_of = {}
    for g in range(min(ahead, len(groups))):
        for c in groups[g]:
            m_of[g] = score_chunk(c, m_of.get(g))
    m = None
    acc = None
    for gi, cur in enumerate(groups):
        nxt = groups[gi + ahead] if gi + ahead < len(groups) else []
        m_new = m_of[gi] if m is None else jnp.maximum(m, m_of[gi])
        part = None
        for i in range(max(len(cur), len(nxt))):
            if i < len(nxt):
                m_of[gi + ahead] = score_chunk(nxt[i], m_of.get(gi + ahead))
            if i < len(cur):
                part = value_chunk(cur[i], m_new, part)
        acc = part if acc is None else acc * jnp.exp2(m.astype(F32) - m_new.astype(F32)) + part
        m = m_new
    acc = acc[:V_DIM] / acc[V_DIM:V_DIM + 1]
    lp = lp_ref[...]
    lam = (jnp.exp(jnp.sum(lp[0:1] * lp[1:2], axis=-1, keepdims=True))
           - jnp.exp(jnp.sum(lp[2:3] * lp[3:4], axis=-1, keepdims=True)) + lam_init)
    o = (acc[:, :tq] - lam * acc[:, tq:]) * lp[4:5, 0:1]
    o = o * lax.rsqrt(jnp.mean(o * o, axis=0, keepdims=True) + NORM_EPS) * sg_ref[...] * (1.0 - lam_init)
    o_ref[0] = o.T.astype(BF16)


def _attn_chunk(n_keys):
    for tk in (768, 512, 256, 128):
        if n_keys % tk == 0:
            return tk
    raise ValueError(f"key count {n_keys} is not a multiple of {LANES}")


def _diff_attention(qz, k_all, vt_all, lam_params, sub_gain, *, lam_init):
    b, _, n_tiles, _, tq2 = qz.shape
    tq = tq2 // 2
    l, d = n_tiles * tq, N_HEADS * V_DIM
    n_keys = k_all.shape[1]
    tk = _attn_chunk(n_keys)
    kern = functools.partial(_attn_kernel, tk=tk, group=ATTN_GROUP_CHUNKS, ahead=ATTN_SCORE_AHEAD, lam_init=lam_init)
    return pl.pallas_call(
        kern,
        grid=(b, N_HEADS, l // tq),
        in_specs=[pl.BlockSpec((1, 1, 1, V_DIM, 2 * tq), lambda bi, h, i: (bi, h, i, 0, 0)),
                  pl.BlockSpec((1, n_keys, V_DIM), lambda bi, h, i: (bi, 0, h)),
                  pl.BlockSpec((1, 1, V_ROWS, n_keys), lambda bi, h, i: (bi, h, 0, 0)),
                  pl.BlockSpec((5, HEAD_DIM), lambda bi, h, i: (0, 0)),
                  pl.BlockSpec((V_DIM, 1), lambda bi, h, i: (0, 0))],
        out_specs=pl.BlockSpec((1, tq, V_DIM), lambda bi, h, i: (bi, i, h)),
        out_shape=jax.ShapeDtypeStruct((b, l, d), BF16),
        scratch_shapes=[pltpu.VMEM(((ATTN_SCORE_AHEAD + 1) * ATTN_GROUP_CHUNKS * tk, 2 * tq), BF16)],
        compiler_params=_params("arbitrary", "arbitrary", "arbitrary"),
        name="diff_attention",
    )(qz, k_all, vt_all, lam_params, sub_gain)


def _route(h2, rwh_ref, rwl_ref, rb_ref, before_ref, carry_ref, ridx_ref, gcol_ref, cnt_ref, is_first):
    tm = h2.shape[0]
    hh = h2.astype(BF16)
    hl = (h2 - hh.astype(F32)).astype(BF16)
    rw2 = jnp.concatenate([rwh_ref[...], rwl_ref[...]], axis=0)
    part = lax.dot_general(rw2, hh, NT_DIMS, preferred_element_type=F32)
    logits = (part[:N_EXPERTS] + part[N_EXPERTS:]
              + lax.dot_general(rwh_ref[...], hl, NT_DIMS, preferred_element_type=F32) + rb_ref[...])
    groups = [logits[g * EXPERTS_PER_GROUP:(g + 1) * EXPERTS_PER_GROUP] for g in range(N_EXPERT_GROUPS)]
    top = groups[0]
    for g in range(1, N_EXPERT_GROUPS):
        top = jnp.maximum(top, groups[g])
    top = jnp.max(top, axis=0, keepdims=True)
    sub = lax.broadcasted_iota(jnp.int32, (EXPERTS_PER_GROUP, tm), 0)
    best = None
    for g in range(N_EXPERT_GROUPS):
        ex = jnp.exp(groups[g] - top)
        v1 = jnp.max(ex, axis=0, keepdims=True)
        i1 = jnp.min(jnp.where(ex == v1, sub, EXPERTS_PER_GROUP), axis=0, keepdims=True)
        rest = jnp.where(sub == i1, -1.0, ex)
        v2 = jnp.max(rest, axis=0, keepdims=True)
        i2 = jnp.min(jnp.where(rest == v2, sub, EXPERTS_PER_GROUP), axis=0, keepdims=True)
        cand = (v1 + v2, v1, v2, i1 + g * EXPERTS_PER_GROUP, i2 + g * EXPERTS_PER_GROUP)
        if best is None:
            best = cand
        else:
            better = cand[0] > best[0]
            best = tuple(jnp.where(better, new, old) for new, old in zip(cand, best))
    _, v1, v2, e0, e1 = best
    gate0 = v1 / (v1 + v2)
    gate1 = v2 / (v1 + v2)

    @pl.when(is_first)
    def _():
        carry_ref[...] = jnp.zeros_like(carry_ref)

    erow = lax.broadcasted_iota(jnp.int32, (N_EXPERTS, tm), 0)
    oh0 = erow == e0
    oh1 = erow == e1
    chosen = jnp.where(oh0 | oh1, 1.0, 0.0)
    prior = carry_ref[:, 0:1] + jnp.dot(chosen.astype(BF16), before_ref[...], preferred_element_type=F32)
    r0 = jnp.sum(jnp.where(oh0, prior, 0.0), axis=0, keepdims=True).astype(jnp.int32)
    r1 = jnp.sum(jnp.where(oh1, prior, 0.0), axis=0, keepdims=True).astype(jnp.int32)
    carry_ref[...] = carry_ref[...] + jnp.sum(chosen, axis=1, keepdims=True)
    cnt_ref[...] = carry_ref[...]
    rid = lax.broadcasted_iota(jnp.int32, (SUBLANES, tm), 0)
    ridx_ref[...] = jnp.where(rid == 0, e0, jnp.where(rid == 1, e1, jnp.where(rid == 2, r0, jnp.where(rid == 3, r1, 0))))
    gid = lax.broadcasted_iota(jnp.int32, (LANES, tm), 0)
    gcol_ref[...] = jnp.where(gid == 0, gate0, jnp.where(gid == 1, gate1, 0.0)).T


def _tail(y, x_ref, g1_ref, sh2_ref, s2_ref, n2_ref, rwh_ref, rwl_ref, rb_ref, before_ref,
          x_out, h2_out, ridx_ref, gcol_ref, cnt_ref, carry_ref):
    x1 = x_ref[0] + g1_ref[0] * y
    x_out[0] = x1
    h2 = _rms_mod(x1, n2_ref[...], sh2_ref[0], s2_ref[0])
    h2_out[0] = _pack_bf16_pairs(h2)
    is_first = (pl.program_id(0) == 0) & (pl.program_id(1) == 0)
    _route(h2, rwh_ref, rwl_ref, rb_ref, before_ref, carry_ref, ridx_ref, gcol_ref, cnt_ref, is_first)


def _attn_tail_kernel(a_ref, wo_ref, *rest):
    y = jnp.dot(a_ref[0], wo_ref[...], preferred_element_type=F32)
    _tail(y, *rest)


def _pool_tail_kernel(u_ref, up_ref, un_ref, wg_ref, cs_ref, wo_ref, *rest, seq_len):
    *tail_refs, ubuf, abuf = rest
    tm = u_ref.shape[1]
    i = pl.program_id(1)
    u = u_ref[0].astype(F32)
    ubuf[0:POOL_HALO] = jnp.where(i > 0, up_ref[0].astype(F32)[HALO_ROWS - POOL_HALO:], 0.0)
    ubuf[POOL_HALO:POOL_HALO + tm] = u
    ubuf[POOL_HALO + tm:2 * POOL_HALO + tm] = jnp.where(i < pl.num_programs(1) - 1,
                                                        un_ref[0].astype(F32)[:POOL_HALO], 0.0)
    pos = i * tm + lax.broadcasted_iota(jnp.int32, (tm, 1), 0)
    gd = wg_ref.shape[1]
    assert all(win == 2 ** (g + 1) for g, win in enumerate(POOL_WINDOWS))
    n_ext = tm + 2 * POOL_HALO
    bufs = (ubuf, abuf)
    abuf[1:n_ext, :] = ubuf[0:n_ext - 1, :] + ubuf[1:n_ext, :]
    lo, hi = 1, n_ext
    for g in range(1, len(POOL_WINDOWS)):
        src, dst = bufs[g % 2], bufs[(g + 1) % 2]
        sh = POOL_WINDOWS[g] // 4
        dst[lo + sh:hi - sh, g * gd:] = src[lo:hi - 2 * sh, g * gd:] + src[lo + 2 * sh:hi, g * gd:]
        lo, hi = lo + sh, hi - sh
    outs = []
    for g, win in enumerate(POOL_WINDOWS):
        half = win // 2
        cols = slice(g * gd, (g + 1) * gd)
        s = bufs[(g + 1) % 2][POOL_HALO:POOL_HALO + tm, cols]
        inv_cnt = 1.0 / (jnp.minimum(pos + half, seq_len) - jnp.maximum(pos - half, 0)).astype(F32)
        dlt = (s * inv_cnt - u[:, cols]).astype(BF16)
        outs.append(jnp.dot(dlt, wg_ref[g], preferred_element_type=F32))
    z = (jnp.concatenate(outs, axis=-1) * cs_ref[...]).astype(BF16)
    y = jnp.dot(z, wo_ref[...], preferred_element_type=F32)
    _tail(y, *tail_refs)


def _mixer_tail(front_args, front_specs, kern, x, g1, sh2, s2, n2g, rwh, rwl, rb, *, tm, scratch=()):
    b, l, d = x.shape
    nt = l // tm
    n = b * l
    row = lambda bi, i: (bi, 0, 0)
    fix = lambda bi, i: (0, 0)
    tile = lambda bi, i: (bi, i, 0)
    in_specs = list(front_specs) + [
        pl.BlockSpec((1, tm, d), tile),
        pl.BlockSpec((1, 1, d), row), pl.BlockSpec((1, 1, d), row), pl.BlockSpec((1, 1, d), row),
        pl.BlockSpec((1, d), fix),
        pl.BlockSpec((N_EXPERTS, d), fix), pl.BlockSpec((N_EXPERTS, d), fix),
        pl.BlockSpec((N_EXPERTS, 1), fix),
        pl.BlockSpec((tm, tm), fix)]
    before = jnp.asarray(np.triu(np.ones((tm, tm), np.float32), k=1), BF16)
    out_specs = [pl.BlockSpec((1, tm, d), tile), pl.BlockSpec((1, tm, d // 2), tile),
                 pl.BlockSpec((SUBLANES, tm), lambda bi, i: (0, bi * nt + i)),
                 pl.BlockSpec((tm, LANES), lambda bi, i: (bi * nt + i, 0)),
                 pl.BlockSpec((N_EXPERTS, LANES), fix)]
    out_shape = [jax.ShapeDtypeStruct((b, l, d), F32), jax.ShapeDtypeStruct((b, l, d // 2), jnp.int32),
                 jax.ShapeDtypeStruct((SUBLANES, n), jnp.int32), jax.ShapeDtypeStruct((n, LANES), F32),
                 jax.ShapeDtypeStruct((N_EXPERTS, LANES), F32)]
    return pl.pallas_call(
        kern,
        grid=(b, nt),
        in_specs=in_specs, out_specs=out_specs, out_shape=out_shape,
        scratch_shapes=[pltpu.VMEM((N_EXPERTS, LANES), F32)] + list(scratch),
        compiler_params=_params("arbitrary", "arbitrary"),
        name="mixer_tail",
    )(*front_args, x, g1, sh2, s2, n2g, rwh, rwl, rb, before)


def _slot_kernel(ps_ref, ridx_ref, dest_ref):
    ridx = ridx_ref[...]
    ps = ps_ref[...]
    erow = lax.broadcasted_iota(jnp.int32, (N_EXPERTS, ridx.shape[1]), 0)
    rows = []
    for k in range(TOP_K):
        start = jnp.sum(jnp.where(erow == ridx[k:k + 1], ps, 0), axis=0, keepdims=True)
        rows.append(start + ridx[TOP_K + k:TOP_K + k + 1])
    rid = lax.broadcasted_iota(jnp.int32, ridx.shape, 0)
    dest_ref[...] = jnp.where(rid == 0, rows[0], jnp.where(rid == 1, rows[1], 0))


def _slot_index(pad_start, ridx, *, tn):
    n = ridx.shape[1]
    return pl.pallas_call(
        _slot_kernel,
        grid=(n // tn,),
        in_specs=[pl.BlockSpec((N_EXPERTS, 1), lambda i: (0, 0)),
                  pl.BlockSpec((SUBLANES, tn), lambda i: (0, i))],
        out_specs=pl.BlockSpec((SUBLANES, tn), lambda i: (0, i)),
        out_shape=jax.ShapeDtypeStruct((SUBLANES, n), jnp.int32),
        compiler_params=_params("arbitrary"),
        name="slot_index",
    )(pad_start.reshape(N_EXPERTS, 1), ridx)


def _sc_mesh():
    return plsc.VectorSubcoreMesh(core_axis_name="c", subcore_axis_name="s",
                                  num_cores=SC_CORES, num_subcores=SC_SUBCORES)


def _sc_worker_base(per_worker):
    return (lax.axis_index("s") * SC_CORES + lax.axis_index("c")) * per_worker


def _sc_scatter_rows(rows, idx0, idx1, n_slots):
    n, d = rows.shape
    per_worker = n // SC_WORKERS
    assert per_worker % SC_WINDOW == 0

    n_win = per_worker // SC_WINDOW
    assert n_win % 2 == 0

    def body(rows_hbm, i0_hbm, i1_hbm, out_hbm, i0_a, i1_a, rows_a, i0_b, i1_b, rows_b, sem_a, sem_b):
        base = _sc_worker_base(per_worker)

        def offset(j):
            return pl.multiple_of(base + j * SC_WINDOW, SC_WINDOW)

        def start(j, i0_v, i1_v, rows_v, sem):
            pltpu.sync_copy(i0_hbm.at[pl.ds(offset(j), SC_WINDOW)], i0_v)
            pltpu.sync_copy(i1_hbm.at[pl.ds(offset(j), SC_WINDOW)], i1_v)
            pltpu.async_copy(rows_hbm.at[pl.ds(offset(j), SC_WINDOW)], rows_v, sem)

        def finish(j, i0_v, i1_v, rows_v, sem):
            pltpu.make_async_copy(rows_hbm.at[pl.ds(offset(j), SC_WINDOW)], rows_v, sem).wait()
            pltpu.sync_copy(rows_v, out_hbm.at[i0_v])
            pltpu.sync_copy(rows_v, out_hbm.at[i1_v])

        start(0, i0_a, i1_a, rows_a, sem_a)

        @pl.loop(0, n_win, step=2)
        def _(j):
            start(j + 1, i0_b, i1_b, rows_b, sem_b)
            finish(j, i0_a, i1_a, rows_a, sem_a)

            @pl.when(j + 2 < n_win)
            def _():
                start(j + 2, i0_a, i1_a, rows_a, sem_a)

            finish(j + 1, i0_b, i1_b, rows_b, sem_b)

    window = [pltpu.VMEM((SC_WINDOW,), jnp.int32), pltpu.VMEM((SC_WINDOW,), jnp.int32),
              pltpu.VMEM((SC_WINDOW, d), rows.dtype)]
    return pl.kernel(
        body, out_type=jax.ShapeDtypeStruct((n_slots, d), rows.dtype), mesh=_sc_mesh(),
        scratch_types=window + window + [pltpu.SemaphoreType.DMA, pltpu.SemaphoreType.DMA],
        name="sc_scatter_rows",
    )(rows, idx0, idx1)


def _sc_gather_rows(table, idx):
    n = idx.shape[0]
    d = table.shape[1]
    per_worker = n // SC_WORKERS
    assert per_worker % SC_WINDOW == 0

    n_win = per_worker // SC_WINDOW
    assert n_win % 2 == 0

    def body(table_hbm, idx_hbm, out_hbm, idx_a, idx_b, rows_a, rows_b, sem_a, sem_b):
        base = _sc_worker_base(per_worker)

        def offset(j):
            return pl.multiple_of(base + j * SC_WINDOW, SC_WINDOW)

        def start(j, idx_v, rows_v, sem):
            pltpu.sync_copy(idx_hbm.at[pl.ds(offset(j), SC_WINDOW)], idx_v)
            pltpu.async_copy(table_hbm.at[idx_v], rows_v, sem)

        def finish(j, idx_v, rows_v, sem):
            pltpu.make_async_copy(table_hbm.at[idx_v], rows_v, sem).wait()
            pltpu.sync_copy(rows_v, out_hbm.at[pl.ds(offset(j), SC_WINDOW)])

        start(0, idx_a, rows_a, sem_a)

        @pl.loop(0, n_win, step=2)
        def _(j):
            start(j + 1, idx_b, rows_b, sem_b)
            finish(j, idx_a, rows_a, sem_a)

            @pl.when(j + 2 < n_win)
            def _():
                start(j + 2, idx_a, rows_a, sem_a)

            finish(j + 1, idx_b, rows_b, sem_b)

    return pl.kernel(
        body, out_type=jax.ShapeDtypeStruct((n, d), table.dtype), mesh=_sc_mesh(),
        scratch_types=[pltpu.VMEM((SC_WINDOW,), jnp.int32), pltpu.VMEM((SC_WINDOW,), jnp.int32),
                       pltpu.VMEM((SC_WINDOW, d), table.dtype), pltpu.VMEM((SC_WINDOW, d), table.dtype),
                       pltpu.SemaphoreType.DMA, pltpu.SemaphoreType.DMA],
        name="sc_gather_rows",
    )(table, idx)


def _ffn_kernel(be_ref, nv_ref, xs_ref, wg_ref, wu_ref, wd_ref, ys_ref, wg_b, wu_b, wd_b):
    j = pl.program_id(0)
    valid = nv_ref[j]

    @pl.when((valid > 0) & ((j == 0) | (be_ref[j] != be_ref[jnp.maximum(j - 1, 0)])))
    def _():
        wg_b[...] = wg_ref[0, 0].astype(BF16)
        wu_b[...] = wu_ref[0, 0].astype(BF16)
        wd_b[...] = wd_ref[0, 0].astype(BF16)

    @pl.when(valid > 0)
    def _():
        row = lax.broadcasted_iota(jnp.int32, (SLOT_ROWS, 1), 0)
        xw = jnp.where(row < valid, xs_ref[...], 0)
        xb = _unpack_bf16_pairs(xw).astype(BF16)
        g = jnp.dot(xb, wg_b[...], preferred_element_type=F32)
        u = jnp.dot(xb, wu_b[...], preferred_element_type=F32)
        a = (g / (1.0 + jnp.exp(-g)) * u).astype(BF16)
        ys_ref[...] = _pack_bf16_pairs(jnp.dot(a, wd_b[...], preferred_element_type=F32))

    @pl.when(valid <= 0)
    def _():
        ys_ref[...] = jnp.zeros_like(ys_ref)


def _expert_ffn(block_e, n_valid, xs, w_gate, w_up, w_down, layer):
    n_slots, dw = xs.shape
    d, de = w_gate.shape[2:]
    w_idx = lambda j, be, nv: (layer, be[j], 0, 0)
    return pl.pallas_call(
        _ffn_kernel,
        grid_spec=pltpu.PrefetchScalarGridSpec(
            num_scalar_prefetch=2,
            grid=(n_slots // SLOT_ROWS,),
            in_specs=[pl.BlockSpec((SLOT_ROWS, dw), lambda j, be, nv: (j, 0)),
                      pl.BlockSpec((1, 1, d, de), w_idx),
                      pl.BlockSpec((1, 1, d, de), w_idx),
                      pl.BlockSpec((1, 1, de, d), w_idx)],
            out_specs=pl.BlockSpec((SLOT_ROWS, dw), lambda j, be, nv: (j, 0)),
            scratch_shapes=[pltpu.VMEM((d, de), BF16), pltpu.VMEM((d, de), BF16), pltpu.VMEM((de, d), BF16)]),
        out_shape=jax.ShapeDtypeStruct((n_slots, dw), jnp.int32),
        compiler_params=_params("arbitrary"),
        name="expert_ffn",
    )(block_e, n_valid, xs, w_gate, w_up, w_down)


def _combine_kernel(y0_ref, y1_ref, gcol_ref, x_ref, g2_ref, *rest, pool_in):
    if pool_in:
        sh_ref, sc_ref, n1_ref, wi_ref, x_out, u_out = rest
    else:
        (x_out,) = rest
    gc = gcol_ref[...]
    out = gc[:, 0:1] * _unpack_bf16_pairs(y0_ref[0]) + gc[:, 1:2] * _unpack_bf16_pairs(y1_ref[0])
    x2 = x_ref[0] + g2_ref[0] * out
    x_out[0] = x2
    if pool_in:
        hb = _rms_mod(x2, n1_ref[...], sh_ref[0], sc_ref[0]).astype(BF16)
        u_out[0] = jnp.dot(hb, wi_ref[...], preferred_element_type=F32).astype(BF16)


def _combine(yg, gcol, x, g2, pool_args=None, *, tc):
    b, l, d = x.shape
    nt = l // tc
    pool_in = pool_args is not None
    kern = functools.partial(_combine_kernel, pool_in=pool_in)
    row = lambda bi, i: (bi, 0, 0)
    fix = lambda bi, i: (0, 0)
    tile = lambda bi, i: (bi, i, 0)
    in_specs = [pl.BlockSpec((1, tc, d // 2), lambda bi, i: (0, bi * nt + i, 0)),
                pl.BlockSpec((1, tc, d // 2), lambda bi, i: (1, bi * nt + i, 0)),
                pl.BlockSpec((tc, LANES), lambda bi, i: (bi * nt + i, 0)),
                pl.BlockSpec((1, tc, d), tile),
                pl.BlockSpec((1, 1, d), row)]
    out_specs = [pl.BlockSpec((1, tc, d), tile)]
    out_shape = [jax.ShapeDtypeStruct((b, l, d), F32)]
    args = [yg, yg, gcol, x, g2]
    if pool_in:
        in_specs += [pl.BlockSpec((1, 1, d), row), pl.BlockSpec((1, 1, d), row),
                     pl.BlockSpec((1, d), fix), pl.BlockSpec((d, d), fix)]
        out_specs.append(pl.BlockSpec((1, tc, d), tile))
        out_shape.append(jax.ShapeDtypeStruct((b, l, d), BF16))
        args += list(pool_args)
    return pl.pallas_call(
        kern,
        grid=(b, nt),
        in_specs=in_specs, out_specs=out_specs, out_shape=out_shape,
        compiler_params=_params("arbitrary", "arbitrary"),
        name="moe_combine",
    )(*args)


def _moe(h2, ridx, gcol, counts, x1, g2, w_gate, w_up, w_down, layer, pool_args=None):
    b, l, d = x1.shape
    n = b * l
    n_blocks = (n * TOP_K) // SLOT_ROWS + N_EXPERTS
    cnt = counts[:, 0].astype(jnp.int32)
    padded = (cnt + SLOT_ROWS - 1) // SLOT_ROWS * SLOT_ROWS
    earlier = jnp.arange(N_EXPERTS)[None, :] < jnp.arange(N_EXPERTS)[:, None]
    pad_start = jnp.sum(jnp.where(earlier, padded[None, :], 0), axis=1).astype(jnp.int32)
    pad_end = pad_start + padded
    block_start = jnp.arange(n_blocks, dtype=jnp.int32) * SLOT_ROWS
    block_e = jnp.minimum(jnp.sum(pad_end[None, :] <= block_start[:, None], axis=1), N_EXPERTS - 1).astype(jnp.int32)
    own = block_e[:, None] == jnp.arange(N_EXPERTS)[None, :]
    data_end = jnp.sum(jnp.where(own, (pad_start + cnt)[None, :], 0), axis=1)
    n_valid = jnp.clip(data_end - block_start, 0, SLOT_ROWS).astype(jnp.int32)
    dest = _slot_index(pad_start, ridx, tn=2048)
    xs = _sc_scatter_rows(h2.reshape(n, d // 2), dest[0], dest[1], n_blocks * SLOT_ROWS)
    ys = _expert_ffn(block_e, n_valid, xs, w_gate, w_up, w_down, layer)
    yg = _sc_gather_rows(ys, dest[:TOP_K].reshape(TOP_K * n)).reshape(TOP_K, n, d // 2)
    return _combine(yg, gcol, x1, g2, pool_args, tc=512)


def kernel(x, c, ctx, c_ctx, ada_w, ada_b, norm1_g, norm2_g, attn_w_in, attn_w_out, attn_q_gain, attn_k_gain,
           attn_lq1, attn_lk1, attn_lq2, attn_lk2, attn_sub_gain, pool_w_in, pool_w_group, pool_scale, pool_w_out,
           router_w, router_b, moe_w_gate, moe_w_up, moe_w_down):
    b, l, d = x.shape
    n_ctx = ctx.shape[1]
    depth = ada_w.shape[0]
    assert depth == 2 and d == N_HEADS * V_DIM
    tm = 512

    mod = _adaln_mod(c, c_ctx, ada_w, ada_b)
    mods = [[mod[i, :b, None, j * d:(j + 1) * d] for j in range(N_MOD)] for i in range(depth)]
    mod_ctx = [jnp.broadcast_to(mod[0, b, j * d:(j + 1) * d], (b, 1, d)) for j in range(2)]

    rwt = router_w.T
    rwh = rwt.astype(BF16)
    rwl = (rwt - rwh.astype(F32)).astype(BF16)
    rb = router_b.reshape(N_EXPERTS, 1)

    sh1, s1, g1, sh2, s2, g2 = mods[0]
    cos, sin = _rope_tables(l)
    pair_up = (jnp.arange(LANES) & 16) == 0

    def gain_rows(g, factor):
        g2 = jnp.concatenate([g, g]) * factor
        return [g2, jnp.where(pair_up, jnp.roll(g2, -16), jnp.roll(g2, 16))]

    q_scale = HEAD_DIM ** -0.5 * math.log2(math.e)
    q_max = jnp.maximum(jnp.max(jnp.abs(attn_q_gain[0])) * (HEAD_DIM ** 0.5 * q_scale), F32_TINY)
    k_max = jnp.maximum(jnp.max(jnp.abs(attn_k_gain[0])) * HEAD_DIM ** 0.5, F32_TINY)
    need = jnp.ceil(jnp.log2(k_max / F8_MAX))
    room = jnp.floor(jnp.log2(F8_MAX / q_max))
    trade = jnp.exp2(jnp.clip(jnp.clip(0.0, need, jnp.maximum(need, room)), -60.0, 60.0))
    h_max = jnp.float32(0.0)
    for shift_, scale_ in ((sh1, s1), (mod_ctx[0], mod_ctx[1])):
        h_max = jnp.maximum(h_max, jnp.max(d ** 0.5 * jnp.max(jnp.abs(norm1_g[0] * (1.0 + scale_)), axis=-1)
                                           + jnp.sqrt(jnp.sum(shift_ * shift_, axis=-1))))
    w_v = attn_w_in[0][:, 2 * d:]
    v_max = 1.02 * h_max * jnp.sqrt(jnp.max(jnp.sum(w_v * w_v, axis=0)))
    v_grow = jnp.exp2(jnp.clip(jnp.ceil(jnp.log2(jnp.maximum(v_max, F32_TINY) / F8_MAX)), 0.0, 60.0))

    gains = jnp.stack(gain_rows(attn_q_gain[0], q_scale * trade) + gain_rows(attn_k_gain[0], 1.0 / trade))
    w_in = jnp.concatenate([attn_w_in[0][:, :2 * d], w_v / v_grow], axis=1).astype(BF16)
    assert l % tm == 0 and l % n_ctx == 0
    q, k_all, vt_all = _qkv_proj(x, sh1, s1, norm1_g[0][None], w_in, gains, cos, sin, n_qk=2, rope=True,
                                 tm=tm, n_keys=l + n_ctx, key_row0=0)
    k_all, vt_all = _qkv_proj(ctx, mod_ctx[0], mod_ctx[1], norm1_g[0][None], w_in[:, d:], gains,
                              cos[:n_ctx], sin[:n_ctx], n_qk=1, rope=False,
                              tm=n_ctx, n_keys=l + n_ctx, key_row0=l, kv=(k_all, vt_all))
    lam_init = 0.8 - 0.6 * math.exp(-0.3 * 0)
    lam_params = jnp.stack([attn_lq1[0], attn_lk1[0], attn_lq2[0], attn_lk2[0],
                            jnp.full((HEAD_DIM,), v_grow, F32)])
    o = _diff_attention(q, k_all, vt_all, lam_params, attn_sub_gain[0][:, None], lam_init=lam_init)

    fix = lambda bi, i: (0, 0)
    x1, h2, ridx, gcol, counts = _mixer_tail(
        (o, attn_w_out[0].astype(BF16)),
        (pl.BlockSpec((1, tm, d), lambda bi, i: (bi, i, 0)), pl.BlockSpec((d, d), fix)),
        _attn_tail_kernel, x, g1, sh2, s2, norm2_g[0][None], rwh, rwl, rb, tm=tm)

    sh1b, s1b, g1b, sh2b, s2b, g2b = mods[1]
    x2, u = _moe(h2, ridx, gcol, counts, x1, g2, moe_w_gate, moe_w_up, moe_w_down, 0,
                 pool_args=(sh1b, s1b, norm1_g[1][None], pool_w_in[0].astype(BF16)))
    gd = pool_w_group.shape[2]
    nh = tm // HALO_ROWS
    front_specs = (
        pl.BlockSpec((1, tm, d), lambda bi, i: (bi, i, 0)),
        pl.BlockSpec((1, HALO_ROWS, d), lambda bi, i: (bi, jnp.maximum(i * nh - 1, 0), 0)),
        pl.BlockSpec((1, HALO_ROWS, d), lambda bi, i: (bi, jnp.minimum((i + 1) * nh, l // HALO_ROWS - 1), 0)),
        pl.BlockSpec((len(POOL_WINDOWS), gd, gd), lambda bi, i: (0, 0, 0)),
        pl.BlockSpec((1, d), fix),
        pl.BlockSpec((d, d), fix))
    x3, h2b, ridx_b, gcol_b, counts_b = _mixer_tail(
        (u, u, u, pool_w_group[0].astype(BF16), pool_scale[0][None], pool_w_out[0].astype(BF16)),
        front_specs, functools.partial(_pool_tail_kernel, seq_len=l),
        x2, g1b, sh2b, s2b, norm2_g[1][None], rwh, rwl, rb, tm=tm,
        scratch=[pltpu.VMEM((tm + 2 * POOL_HALO, d), F32)] * 2)
    (out,) = _moe(h2b, ridx_b, gcol_b, counts_b, x3, g2b, moe_w_gate, moe_w_up, moe_w_down, 1)
    return out
```

```python
import functools
import math

import jax
import jax.numpy as jnp
import numpy as np
from jax import lax
from jax.experimental import pallas as pl
from jax.experimental.pallas import tpu as pltpu
from jax.experimental.pallas import tpu_sc as plsc

F32 = jnp.float32
BF16 = jnp.bfloat16
F8 = jnp.float8_e4m3fn
F8_MAX = float(jnp.finfo(F8).max)
F32_TINY = float(jnp.finfo(F32).tiny)

LANES = 128
SUBLANES = 8
N_HEADS = 8
HEAD_DIM = 64
V_DIM = 2 * HEAD_DIM
V_ROWS = V_DIM + 32
P_SHIFT = 8.0
GRID_W = 64
ROPE_THETA = 10000.0
NORM_EPS = 1e-6
N_MOD = 6
POOL_WINDOWS = (2, 4, 8, 16)
POOL_HALO = max(POOL_WINDOWS) // 2
HALO_ROWS = 16
N_EXPERTS = 32
N_EXPERT_GROUPS = 4
EXPERTS_PER_GROUP = N_EXPERTS // N_EXPERT_GROUPS
TOP_K = 2
SLOT_ROWS = 512
SC_CORES = 2
SC_SUBCORES = 16
SC_WORKERS = SC_CORES * SC_SUBCORES
SC_WINDOW = 64
ATTN_GROUP_CHUNKS = 1
ATTN_SCORE_AHEAD = 1
VMEM_LIMIT = 48 * 1024 * 1024
NT_DIMS = (((1,), (1,)), ((), ()))


def _params(*sem):
    return pltpu.CompilerParams(dimension_semantics=sem, vmem_limit_bytes=VMEM_LIMIT)


def _rms_mod(x, gain, shift, scale):
    inv_rms = lax.rsqrt(jnp.mean(x * x, axis=-1, keepdims=True) + NORM_EPS)
    return x * inv_rms * (gain * (1.0 + scale)) + shift


def _pack_bf16_pairs(x):
    c = x.shape[1] // 2
    hi = lax.bitcast_convert_type(x[:, :c].astype(BF16).astype(F32), jnp.uint32)
    lo = lax.bitcast_convert_type(x[:, c:].astype(BF16).astype(F32), jnp.uint32)
    return lax.bitcast_convert_type(hi | (lo >> 16), jnp.int32)


def _unpack_bf16_pairs(w):
    u = lax.bitcast_convert_type(w, jnp.uint32)
    hi = lax.bitcast_convert_type(u & jnp.uint32(0xFFFF0000), F32)
    lo = lax.bitcast_convert_type(u << 16, F32)
    return jnp.concatenate([hi, lo], axis=1)


def _mod_kernel(c_ref, w_ref, b_ref, o_ref):
    c = c_ref[...]
    a = c / (1.0 + jnp.exp(-c))
    o_ref[0] = jnp.dot(a, w_ref[0], precision=lax.Precision.HIGHEST,
                       preferred_element_type=F32) + b_ref[0]


def _adaln_mod(c, c_ctx, ada_w, ada_b):
    depth, d, n_out = ada_w.shape
    b = c.shape[0]
    assert b + 1 <= SUBLANES
    rows = jnp.concatenate([c, c_ctx[None], jnp.zeros((SUBLANES - b - 1, d), F32)], axis=0)
    tn = n_out // 4
    return pl.pallas_call(
        _mod_kernel,
        grid=(depth, n_out // tn),
        in_specs=[pl.BlockSpec((SUBLANES, d), lambda i, j: (0, 0)),
                  pl.BlockSpec((1, d, tn), lambda i, j: (i, 0, j)),
                  pl.BlockSpec((1, 1, tn), lambda i, j: (i, 0, j))],
        out_specs=pl.BlockSpec((1, SUBLANES, tn), lambda i, j: (i, 0, j)),
        out_shape=jax.ShapeDtypeStruct((depth, SUBLANES, n_out), F32),
        compiler_params=_params("arbitrary", "arbitrary"),
        name="adaln_mod",
    )(rows, ada_w, ada_b.reshape(depth, 1, n_out))


def _qkv_kernel(x_ref, sh_ref, sc_ref, g_ref, w_ref, wvt_ref, gains_ref, cos_ref, sin_ref, *refs, n_qk, rope):
    out_refs = refs[-(n_qk + 1):]
    tm, d = x_ref.shape[1:]
    hb = _rms_mod(x_ref[0], g_ref[...], sh_ref[0], sc_ref[0]).astype(BF16)
    lane_b4 = (lax.broadcasted_iota(jnp.int32, (1, LANES), 1) & 16) == 0
    chunk_r = lax.broadcasted_iota(jnp.int32, (2 * LANES, 2 * LANES), 0) // HEAD_DIM
    chunk_c = lax.broadcasted_iota(jnp.int32, (2 * LANES, 2 * LANES), 1) // HEAD_DIM
    same_chunk = (chunk_r == chunk_c).astype(BF16)
    gains = gains_ref[...]
    tables = []
    for t in range(n_qk):
        r = 2 * (t + 2 - n_qk)
        tables.append((cos_ref[...] * gains[r:r + 1], sin_ref[...] * gains[r + 1:r + 2]) if rope
                      else (gains[r:r + 1], None))

    def project(t, j):
        return jnp.dot(hb, w_ref[:, t * d + j:t * d + j + 2 * LANES], preferred_element_type=F32)

    def chunk_sums(acc):
        return jnp.dot((acc * acc).astype(BF16), same_chunk, preferred_element_type=F32)

    def finish(t, j, acc, ssq):
        cos_t, sin_t = tables[t]
        nrm = acc * lax.rsqrt(ssq * (1.0 / HEAD_DIM) + NORM_EPS)
        for half in range(2):
            blk = nrm[:, half * LANES:(half + 1) * LANES]
            if rope:
                rot = jnp.where(lane_b4, pltpu.roll(blk, LANES - 16, 1), pltpu.roll(blk, 16, 1))
                y = blk * cos_t + rot * sin_t
            else:
                y = blk * cos_t
            c0 = j + half * LANES
            if t < n_qk - 1:
                yt = y.T
                first = lax.broadcasted_iota(jnp.int32, (V_DIM, 1), 0) < HEAD_DIM
                out_refs[t][0, c0 // LANES, 0, :, 0:tm] = jnp.where(first, yt, 0.0).astype(F8)
                out_refs[t][0, c0 // LANES, 0, :, tm:2 * tm] = jnp.where(first, 0.0, yt).astype(F8)
            else:
                out_refs[t][0, :, c0:c0 + LANES] = y.astype(F8)

    blocks = [(t, j) for t in range(n_qk) for j in range(0, d, 2 * LANES)]
    accs, sums = {}, {}
    for i in range(-2, len(blocks)):
        if i + 2 < len(blocks):
            accs[i + 2] = project(*blocks[i + 2])
        if 0 <= i + 1 < len(blocks):
            sums[i + 1] = chunk_sums(accs[i + 1])
        if i >= 0:
            finish(*blocks[i], accs.pop(i), sums.pop(i))
    vt_ref = out_refs[n_qk]
    ones = jnp.where(lax.broadcasted_iota(jnp.int32, (V_ROWS - V_DIM, tm), 0) == 0, 1.0, 0.0).astype(F8)
    for j in range(0, d, 2 * LANES):
        acc_t = lax.dot_general(wvt_ref[j:j + 2 * LANES, :], hb, NT_DIMS, preferred_element_type=F32)
        for half in range(2):
            h = j // LANES + half
            vt_ref[0, h, 0:V_DIM, :] = acc_t[half * V_DIM:(half + 1) * V_DIM].astype(F8)
            vt_ref[0, h, V_DIM:V_ROWS, :] = ones


def _qkv_proj(x, shift, scale, gain, w, wv_t, gains, cos, sin, *, n_qk, rope, tm, n_keys, key_row0, kv=None):
    b, l, d = x.shape
    n_out = n_qk + 1
    kern = functools.partial(_qkv_kernel, n_qk=n_qk, rope=rope)
    row = lambda bi, i: (bi, 0, 0)
    fix = lambda bi, i: (0, 0)
    kb = key_row0 // tm
    tile = pl.BlockSpec((1, tm, d), lambda bi, i: (bi, i, 0))
    k_spec = pl.BlockSpec((1, tm, d), lambda bi, i: (bi, kb + i, 0))
    vt_spec = pl.BlockSpec((1, N_HEADS, V_ROWS, tm), lambda bi, i: (bi, 0, 0, kb + i))
    q_spec = pl.BlockSpec((1, N_HEADS, 1, V_DIM, 2 * tm), lambda bi, i: (bi, 0, i, 0, 0))
    q_shape = jax.ShapeDtypeStruct((b, N_HEADS, l // tm, V_DIM, 2 * tm), F8)
    k_shape = jax.ShapeDtypeStruct((b, n_keys, d), F8)
    vt_shape = jax.ShapeDtypeStruct((b, N_HEADS, V_ROWS, n_keys), F8)
    in_specs = [tile,
                pl.BlockSpec((1, 1, d), row), pl.BlockSpec((1, 1, d), row),
                pl.BlockSpec((1, d), fix),
                pl.BlockSpec((d, n_qk * d), fix),
                pl.BlockSpec((d, d), fix),
                pl.BlockSpec((4, LANES), fix),
                pl.BlockSpec((tm, LANES), lambda bi, i: (i, 0)),
                pl.BlockSpec((tm, LANES), lambda bi, i: (i, 0))]
    args = [x, shift, scale, gain, w, wv_t, gains, cos, sin]
    aliases = {}
    if kv is not None:
        aliases = {len(args): n_qk - 1, len(args) + 1: n_qk}
        in_specs += [pl.BlockSpec(memory_space=pl.ANY)] * 2
        args += list(kv)
    return pl.pallas_call(
        kern,
        grid=(b, l // tm),
        in_specs=in_specs,
        out_specs=[q_spec] * (n_qk - 1) + [k_spec, vt_spec],
        out_shape=[q_shape] * (n_qk - 1) + [k_shape, vt_shape],
        input_output_aliases=aliases,
        compiler_params=_params("arbitrary", "arbitrary"),
        name="qkv_proj",
    )(*args)


def _rope_tables(n_tokens):
    rows = n_tokens // GRID_W
    row = np.repeat(np.arange(rows, dtype=np.float32), GRID_W)
    col = np.tile(np.arange(GRID_W, dtype=np.float32), rows)
    half = HEAD_DIM // 2
    inv_freq = (np.float32(ROPE_THETA) ** (-np.arange(0, half, 2, dtype=np.float32) / half)).astype(np.float32)
    ang_r = row[:, None] * inv_freq
    ang_c = col[:, None] * inv_freq
    ang = np.concatenate([ang_r, ang_r, ang_c, ang_c] * 2, axis=-1)
    sign = np.where((np.arange(LANES) & 16) == 0, -1.0, 1.0).astype(np.float32)
    return jnp.asarray(np.cos(ang), F32), jnp.asarray(np.sin(ang) * sign, F32)


def _attn_kernel(q_ref, k_ref, vt_ref, lp_ref, sg_ref, o_ref, s_ref, *, tk, group, ahead, lam_init):
    tq = q_ref.shape[4] // 2
    n_chunks = k_ref.shape[1] // tk
    qz = q_ref[0, 0, 0]
    n_slots = (ahead + 1) * group

    def score_chunk(c, m_grp):
        st = jnp.dot(k_ref[0, c * tk:(c + 1) * tk, :], qz, preferred_element_type=F32).astype(BF16)
        slot = c % n_slots
        s_ref[slot * tk:(slot + 1) * tk, :] = st
        mc = jnp.max(st, axis=0, keepdims=True)
        return mc if m_grp is None else jnp.maximum(m_grp, mc)

    def value_chunk(c, m_ref, part):
        slot = c % n_slots
        p = jnp.exp2(s_ref[slot * tk:(slot + 1) * tk, :] - (m_ref - P_SHIFT)).astype(F8)
        pv = jnp.dot(vt_ref[0, 0, :, c * tk:(c + 1) * tk], p, preferred_element_type=F32)
        return pv if part is None else part + pv

    groups = [list(range(g0, min(g0 + group, n_chunks))) for g0 in range(0, n_chunks, group)]
    m_of = {}
    for g in range(min(ahead, len(groups))):
        for c in groups[g]:
            m_of[g] = score_chunk(c, m_of.get(g))
    m = None
    acc = None
    for gi, cur in enumerate(groups):
        nxt = groups[gi + ahead] if gi + ahead < len(groups) else []
        m_new = m_of[gi] if m is None else jnp.maximum(m, m_of[gi])
        part = None
        for i in range(max(len(cur), len(nxt))):
            if i < len(nxt):
                m_of[gi + ahead] = score_chunk(nxt[i], m_of.get(gi + ahead))
            if i < len(cur):
                part = value_chunk(cur[i], m_new, part)
        acc = part if acc is None else acc * jnp.exp2(m.astype(F32) - m_new.astype(F32)) + part
        m = m_new
    acc = acc[:V_DIM] / acc[V_DIM:V_DIM + 1]
    lp = lp_ref[...]
    lam = (jnp.exp(jnp.sum(lp[0:1] * lp[1:2], axis=-1, keepdims=True))
           - jnp.exp(jnp.sum(lp[2:3] * lp[3:4], axis=-1, keepdims=True)) + lam_init)
    o = (acc[:, :tq] - lam * acc[:, tq:]) * lp[4:5, 0:1]
    o = o * lax.rsqrt(jnp.mean(o * o, axis=0, keepdims=True) + NORM_EPS) * sg_ref[...] * (1.0 - lam_init)
    o_ref[0] = o.T.astype(BF16)


def _attn_chunk(n_keys):
    for tk in (768, 512, 256, 128):
        if n_keys % tk == 0:
            return tk
    raise ValueError(f"key count {n_keys} is not a multiple of {LANES}")


def _diff_attention(qz, k_all, vt_all, lam_params, sub_gain, *, lam_init):
    b, _, n_tiles, _, tq2 = qz.shape
    tq = tq2 // 2
    l, d = n_tiles * tq, N_HEADS * V_DIM
    n_keys = k_all.shape[1]
    tk = _attn_chunk(n_keys)
    kern = functools.partial(_attn_kernel, tk=tk, group=ATTN_GROUP_CHUNKS, ahead=ATTN_SCORE_AHEAD, lam_init=lam_init)
    return pl.pallas_call(
        kern,
        grid=(b, N_HEADS, l // tq),
        in_specs=[pl.BlockSpec((1, 1, 1, V_DIM, 2 * tq), lambda bi, h, i: (bi, h, i, 0, 0)),
                  pl.BlockSpec((1, n_keys, V_DIM), lambda bi, h, i: (bi, 0, h)),
                  pl.BlockSpec((1, 1, V_ROWS, n_keys), lambda bi, h, i: (bi, h, 0, 0)),
                  pl.BlockSpec((5, HEAD_DIM), lambda bi, h, i: (0, 0)),
                  pl.BlockSpec((V_DIM, 1), lambda bi, h, i: (0, 0))],
        out_specs=pl.BlockSpec((1, tq, V_DIM), lambda bi, h, i: (bi, i, h)),
        out_shape=jax.ShapeDtypeStruct((b, l, d), BF16),
        scratch_shapes=[pltpu.VMEM(((ATTN_SCORE_AHEAD + 1) * ATTN_GROUP_CHUNKS * tk, 2 * tq), BF16)],
        compiler_params=_params("arbitrary", "arbitrary", "arbitrary"),
        name="diff_attention",
    )(qz, k_all, vt_all, lam_params, sub_gain)


def _route(h2, rwh_ref, rwl_ref, rb_ref, before_ref, carry_ref, ridx_ref, gcol_ref, cnt_ref, is_first):
    tm = h2.shape[0]
    hh = h2.astype(BF16)
    hl = (h2 - hh.astype(F32)).astype(BF16)
    rw2 = jnp.concatenate([rwh_ref[...], rwl_ref[...]], axis=0)
    part = lax.dot_general(rw2, hh, NT_DIMS, preferred_element_type=F32)
    logits = (part[:N_EXPERTS] + part[N_EXPERTS:]
              + lax.dot_general(rwh_ref[...], hl, NT_DIMS, preferred_element_type=F32) + rb_ref[...])
    groups = [logits[g * EXPERTS_PER_GROUP:(g + 1) * EXPERTS_PER_GROUP] for g in range(N_EXPERT_GROUPS)]
    top = groups[0]
    for g in range(1, N_EXPERT_GROUPS):
        top = jnp.maximum(top, groups[g])
    top = jnp.max(top, axis=0, keepdims=True)
    sub = lax.broadcasted_iota(jnp.int32, (EXPERTS_PER_GROUP, tm), 0)
    best = None
    for g in range(N_EXPERT_GROUPS):
        ex = jnp.exp(groups[g] - top)
        v1 = jnp.max(ex, axis=0, keepdims=True)
        i1 = jnp.min(jnp.where(ex == v1, sub, EXPERTS_PER_GROUP), axis=0, keepdims=True)
        rest = jnp.where(sub == i1, -1.0, ex)
        v2 = jnp.max(rest, axis=0, keepdims=True)
        i2 = jnp.min(jnp.where(rest == v2, sub, EXPERTS_PER_GROUP), axis=0, keepdims=True)
        cand = (v1 + v2, v1, v2, i1 + g * EXPERTS_PER_GROUP, i2 + g * EXPERTS_PER_GROUP)
        if best is None:
            best = cand
        else:
            better = cand[0] > best[0]
            best = tuple(jnp.where(better, new, old) for new, old in zip(cand, best))
    _, v1, v2, e0, e1 = best
    gate0 = v1 / (v1 + v2)
    gate1 = v2 / (v1 + v2)

    @pl.when(is_first)
    def _():
        carry_ref[...] = jnp.zeros_like(carry_ref)

    erow = lax.broadcasted_iota(jnp.int32, (N_EXPERTS, tm), 0)
    oh0 = erow == e0
    oh1 = erow == e1
    chosen = jnp.where(oh0 | oh1, 1.0, 0.0)
    prior = carry_ref[:, 0:1] + jnp.dot(chosen.astype(BF16), before_ref[...], preferred_element_type=F32)
    r0 = jnp.sum(jnp.where(oh0, prior, 0.0), axis=0, keepdims=True).astype(jnp.int32)
    r1 = jnp.sum(jnp.where(oh1, prior, 0.0), axis=0, keepdims=True).astype(jnp.int32)
    carry_ref[...] = carry_ref[...] + jnp.sum(chosen, axis=1, keepdims=True)
    cnt_ref[...] = carry_ref[...]
    rid = lax.broadcasted_iota(jnp.int32, (SUBLANES, tm), 0)
    ridx_ref[...] = jnp.where(rid == 0, e0, jnp.where(rid == 1, e1, jnp.where(rid == 2, r0, jnp.where(rid == 3, r1, 0))))
    gid = lax.broadcasted_iota(jnp.int32, (LANES, tm), 0)
    gcol_ref[...] = jnp.where(gid == 0, gate0, jnp.where(gid == 1, gate1, 0.0)).T


def _tail(y, x_ref, g1_ref, sh2_ref, s2_ref, n2_ref, rwh_ref, rwl_ref, rb_ref, before_ref,
          x_out, h2_out, ridx_ref, gcol_ref, cnt_ref, carry_ref):
    x1 = x_ref[0] + g1_ref[0] * y
    x_out[0] = x1
    h2 = _rms_mod(x1, n2_ref[...], sh2_ref[0], s2_ref[0])
    h2_out[0] = _pack_bf16_pairs(h2)
    is_first = (pl.program_id(0) == 0) & (pl.program_id(1) == 0)
    _route(h2, rwh_ref, rwl_ref, rb_ref, before_ref, carry_ref, ridx_ref, gcol_ref, cnt_ref, is_first)


def _attn_tail_kernel(a_ref, wo_ref, *rest):
    y = jnp.dot(a_ref[0], wo_ref[...], preferred_element_type=F32)
    _tail(y, *rest)


def _pool_tail_kernel(u_ref, up_ref, un_ref, wg_ref, cs_ref, wo_ref, *rest, seq_len):
    *tail_refs, ubuf, abuf = rest
    tm = u_ref.shape[1]
    i = pl.program_id(1)
    u = u_ref[0].astype(F32)
    ubuf[0:POOL_HALO] = jnp.where(i > 0, up_ref[0].astype(F32)[HALO_ROWS - POOL_HALO:], 0.0)
    ubuf[POOL_HALO:POOL_HALO + tm] = u
    ubuf[POOL_HALO + tm:2 * POOL_HALO + tm] = jnp.where(i < pl.num_programs(1) - 1,
                                                        un_ref[0].astype(F32)[:POOL_HALO], 0.0)
    pos = i * tm + lax.broadcasted_iota(jnp.int32, (tm, 1), 0)
    gd = wg_ref.shape[1]
    assert all(win == 2 ** (g + 1) for g, win in enumerate(POOL_WINDOWS))
    n_ext = tm + 2 * POOL_HALO
    bufs = (ubuf, abuf)
    abuf[1:n_ext, :] = ubuf[0:n_ext - 1, :] + ubuf[1:n_ext, :]
    lo, hi = 1, n_ext
    for g in range(1, len(POOL_WINDOWS)):
        src, dst = bufs[g % 2], bufs[(g + 1) % 2]
        sh = POOL_WINDOWS[g] // 4
        dst[lo + sh:hi - sh, g * gd:] = src[lo:hi - 2 * sh, g * gd:] + src[lo + 2 * sh:hi, g * gd:]
        lo, hi = lo + sh, hi - sh
    outs = []
    for g, win in enumerate(POOL_WINDOWS):
        half = win // 2
        cols = slice(g * gd, (g + 1) * gd)
        s = bufs[(g + 1) % 2][POOL_HALO:POOL_HALO + tm, cols]
        inv_cnt = 1.0 / (jnp.minimum(pos + half, seq_len) - jnp.maximum(pos - half, 0)).astype(F32)
        dlt = (s * inv_cnt - u[:, cols]).astype(BF16)
        outs.append(jnp.dot(dlt, wg_ref[g], preferred_element_type=F32))
    z = (jnp.concatenate(outs, axis=-1) * cs_ref[...]).astype(BF16)
    y = jnp.dot(z, wo_ref[...], preferred_element_type=F32)
    _tail(y, *tail_refs)


def _mixer_tail(front_args, front_specs, kern, x, g1, sh2, s2, n2g, rwh, rwl, rb, *, tm, scratch=()):
    b, l, d = x.shape
    nt = l // tm
    n = b * l
    row = lambda bi, i: (bi, 0, 0)
    fix = lambda bi, i: (0, 0)
    tile = lambda bi, i: (bi, i, 0)
    in_specs = list(front_specs) + [
        pl.BlockSpec((1, tm, d), tile),
        pl.BlockSpec((1, 1, d), row), pl.BlockSpec((1, 1, d), row), pl.BlockSpec((1, 1, d), row),
        pl.BlockSpec((1, d), fix),
        pl.BlockSpec((N_EXPERTS, d), fix), pl.BlockSpec((N_EXPERTS, d), fix),
        pl.BlockSpec((N_EXPERTS, 1), fix),
        pl.BlockSpec((tm, tm), fix)]
    before = jnp.asarray(np.triu(np.ones((tm, tm), np.float32), k=1), BF16)
    out_specs = [pl.BlockSpec((1, tm, d), tile), pl.BlockSpec((1, tm, d // 2), tile),
                 pl.BlockSpec((SUBLANES, tm), lambda bi, i: (0, bi * nt + i)),
                 pl.BlockSpec((tm, LANES), lambda bi, i: (bi * nt + i, 0)),
                 pl.BlockSpec((N_EXPERTS, LANES), fix)]
    out_shape = [jax.ShapeDtypeStruct((b, l, d), F32), jax.ShapeDtypeStruct((b, l, d // 2), jnp.int32),
                 jax.ShapeDtypeStruct((SUBLANES, n), jnp.int32), jax.ShapeDtypeStruct((n, LANES), F32),
                 jax.ShapeDtypeStruct((N_EXPERTS, LANES), F32)]
    return pl.pallas_call(
        kern,
        grid=(b, nt),
        in_specs=in_specs, out_specs=out_specs, out_shape=out_shape,
        scratch_shapes=[pltpu.VMEM((N_EXPERTS, LANES), F32)] + list(scratch),
        compiler_params=_params("arbitrary", "arbitrary"),
        name="mixer_tail",
    )(*front_args, x, g1, sh2, s2, n2g, rwh, rwl, rb, before)


def _slot_kernel(ps_ref, ridx_ref, dest_ref):
    ridx = ridx_ref[...]
    ps = ps_ref[...]
    erow = lax.broadcasted_iota(jnp.int32, (N_EXPERTS, ridx.shape[1]), 0)
    rows = []
    for k in range(TOP_K):
        start = jnp.sum(jnp.where(erow == ridx[k:k + 1], ps, 0), axis=0, keepdims=True)
        rows.append(start + ridx[TOP_K + k:TOP_K + k + 1])
    rid = lax.broadcasted_iota(jnp.int32, ridx.shape, 0)
    dest_ref[...] = jnp.where(rid == 0, rows[0], jnp.where(rid == 1, rows[1], 0))


def _slot_index(pad_start, ridx, *, tn):
    n = ridx.shape[1]
    return pl.pallas_call(
        _slot_kernel,
        grid=(n // tn,),
        in_specs=[pl.BlockSpec((N_EXPERTS, 1), lambda i: (0, 0)),
                  pl.BlockSpec((SUBLANES, tn), lambda i: (0, i))],
        out_specs=pl.BlockSpec((SUBLANES, tn), lambda i: (0, i)),
        out_shape=jax.ShapeDtypeStruct((SUBLANES, n), jnp.int32),
        compiler_params=_params("arbitrary"),
        name="slot_index",
    )(pad_start.reshape(N_EXPERTS, 1), ridx)


def _sc_mesh():
    return plsc.VectorSubcoreMesh(core_axis_name="c", subcore_axis_name="s",
                                  num_cores=SC_CORES, num_subcores=SC_SUBCORES)


def _sc_worker_base(per_worker):
    return (lax.axis_index("s") * SC_CORES + lax.axis_index("c")) * per_worker


def _sc_scatter_rows(rows, idx0, idx1, n_slots):
    n, d = rows.shape
    per_worker = n // SC_WORKERS
    assert per_worker % SC_WINDOW == 0

    n_win = per_worker // SC_WINDOW
    assert n_win % 2 == 0

    def body(rows_hbm, i0_hbm, i1_hbm, out_hbm, i0_a, i1_a, rows_a, i0_b, i1_b, rows_b, sem_a, sem_b):
        base = _sc_worker_base(per_worker)

        def offset(j):
            return pl.multiple_of(base + j * SC_WINDOW, SC_WINDOW)

        def start(j, i0_v, i1_v, rows_v, sem):
            pltpu.sync_copy(i0_hbm.at[pl.ds(offset(j), SC_WINDOW)], i0_v)
            pltpu.sync_copy(i1_hbm.at[pl.ds(offset(j), SC_WINDOW)], i1_v)
            pltpu.async_copy(rows_hbm.at[pl.ds(offset(j), SC_WINDOW)], rows_v, sem)

        def finish(j, i0_v, i1_v, rows_v, sem):
            pltpu.make_async_copy(rows_hbm.at[pl.ds(offset(j), SC_WINDOW)], rows_v, sem).wait()
            pltpu.sync_copy(rows_v, out_hbm.at[i0_v])
            pltpu.sync_copy(rows_v, out_hbm.at[i1_v])

        start(0, i0_a, i1_a, rows_a, sem_a)

        @pl.loop(0, n_win, step=2)
        def _(j):
            start(j + 1, i0_b, i1_b, rows_b, sem_b)
            finish(j, i0_a, i1_a, rows_a, sem_a)

            @pl.when(j + 2 < n_win)
            def _():
                start(j + 2, i0_a, i1_a, rows_a, sem_a)

            finish(j + 1, i0_b, i1_b, rows_b, sem_b)

    window = [pltpu.VMEM((SC_WINDOW,), jnp.int32), pltpu.VMEM((SC_WINDOW,), jnp.int32),
              pltpu.VMEM((SC_WINDOW, d), rows.dtype)]
    return pl.kernel(
        body, out_type=jax.ShapeDtypeStruct((n_slots, d), rows.dtype), mesh=_sc_mesh(),
        scratch_types=window + window + [pltpu.SemaphoreType.DMA, pltpu.SemaphoreType.DMA],
        name="sc_scatter_rows",
    )(rows, idx0, idx1)


def _sc_gather_rows(table, idx):
    n = idx.shape[0]
    d = table.shape[1]
    per_worker = n // SC_WORKERS
    assert per_worker % SC_WINDOW == 0

    n_win = per_worker // SC_WINDOW
    assert n_win % 2 == 0

    def body(table_hbm, idx_hbm, out_hbm, idx_a, idx_b, rows_a, rows_b, sem_a, sem_b):
        base = _sc_worker_base(per_worker)

        def offset(j):
            return pl.multiple_of(base + j * SC_WINDOW, SC_WINDOW)

        def start(j, idx_v, rows_v, sem):
            pltpu.sync_copy(idx_hbm.at[pl.ds(offset(j), SC_WINDOW)], idx_v)
            pltpu.async_copy(table_hbm.at[idx_v], rows_v, sem)

        def finish(j, idx_v, rows_v, sem):
            pltpu.make_async_copy(table_hbm.at[idx_v], rows_v, sem).wait()
            pltpu.sync_copy(rows_v, out_hbm.at[pl.ds(offset(j), SC_WINDOW)])

        start(0, idx_a, rows_a, sem_a)

        @pl.loop(0, n_win, step=2)
        def _(j):
            start(j + 1, idx_b, rows_b, sem_b)
            finish(j, idx_a, rows_a, sem_a)

            @pl.when(j + 2 < n_win)
            def _():
                start(j + 2, idx_a, rows_a, sem_a)

            finish(j + 1, idx_b, rows_b, sem_b)

    return pl.kernel(
        body, out_type=jax.ShapeDtypeStruct((n, d), table.dtype), mesh=_sc_mesh(),
        scratch_types=[pltpu.VMEM((SC_WINDOW,), jnp.int32), pltpu.VMEM((SC_WINDOW,), jnp.int32),
                       pltpu.VMEM((SC_WINDOW, d), table.dtype), pltpu.VMEM((SC_WINDOW, d), table.dtype),
                       pltpu.SemaphoreType.DMA, pltpu.SemaphoreType.DMA],
        name="sc_gather_rows",
    )(table, idx)


def _ffn_kernel(be_ref, nv_ref, xs_ref, wg_ref, wu_ref, wd_ref, ys_ref, wg_b, wu_b, wd_b):
    j = pl.program_id(0)
    valid = nv_ref[j]

    @pl.when((valid > 0) & ((j == 0) | (be_ref[j] != be_ref[jnp.maximum(j - 1, 0)])))
    def _():
        wg_b[...] = wg_ref[0, 0].astype(BF16)
        wu_b[...] = wu_ref[0, 0].astype(BF16)
        wd_b[...] = wd_ref[0, 0].astype(BF16)

    @pl.when(valid > 0)
    def _():
        row = lax.broadcasted_iota(jnp.int32, (SLOT_ROWS, 1), 0)
        xw = jnp.where(row < valid, xs_ref[...], 0)
        xb = _unpack_bf16_pairs(xw).astype(BF16)
        g = jnp.dot(xb, wg_b[...], preferred_element_type=F32)
        u = jnp.dot(xb, wu_b[...], preferred_element_type=F32)
        a = (g / (1.0 + jnp.exp(-g)) * u).astype(BF16)
        ys_ref[...] = _pack_bf16_pairs(jnp.dot(a, wd_b[...], preferred_element_type=F32))

    @pl.when(valid <= 0)
    def _():
        ys_ref[...] = jnp.zeros_like(ys_ref)


def _expert_ffn(block_e, n_valid, xs, w_gate, w_up, w_down, layer):
    n_slots, dw = xs.shape
    d, de = w_gate.shape[2:]
    w_idx = lambda j, be, nv: (layer, be[j], 0, 0)
    return pl.pallas_call(
        _ffn_kernel,
        grid_spec=pltpu.PrefetchScalarGridSpec(
            num_scalar_prefetch=2,
            grid=(n_slots // SLOT_ROWS,),
            in_specs=[pl.BlockSpec((SLOT_ROWS, dw), lambda j, be, nv: (j, 0)),
                      pl.BlockSpec((1, 1, d, de), w_idx),
                      pl.BlockSpec((1, 1, d, de), w_idx),
                      pl.BlockSpec((1, 1, de, d), w_idx)],
            out_specs=pl.BlockSpec((SLOT_ROWS, dw), lambda j, be, nv: (j, 0)),
            scratch_shapes=[pltpu.VMEM((d, de), BF16), pltpu.VMEM((d, de), BF16), pltpu.VMEM((de, d), BF16)]),
        out_shape=jax.ShapeDtypeStruct((n_slots, dw), jnp.int32),
        compiler_params=_params("arbitrary"),
        name="expert_ffn",
    )(block_e, n_valid, xs, w_gate, w_up, w_down)


def _combine_kernel(y0_ref, y1_ref, gcol_ref, x_ref, g2_ref, *rest, pool_in):
    if pool_in:
        sh_ref, sc_ref, n1_ref, wi_ref, x_out, u_out = rest
    else:
        (x_out,) = rest
    gc = gcol_ref[...]
    out = gc[:, 0:1] * _unpack_bf16_pairs(y0_ref[0]) + gc[:, 1:2] * _unpack_bf16_pairs(y1_ref[0])
    x2 = x_ref[0] + g2_ref[0] * out
    x_out[0] = x2
    if pool_in:
        hb = _rms_mod(x2, n1_ref[...], sh_ref[0], sc_ref[0]).astype(BF16)
        u_out[0] = jnp.dot(hb, wi_ref[...], preferred_element_type=F32).astype(BF16)


def _combine(yg, gcol, x, g2, pool_args=None, *, tc):
    b, l, d = x.shape
    nt = l // tc
    pool_in = pool_args is not None
    kern = functools.partial(_combine_kernel, pool_in=pool_in)
    row = lambda bi, i: (bi, 0, 0)
    fix = lambda bi, i: (0, 0)
    tile = lambda bi, i: (bi, i, 0)
    in_specs = [pl.BlockSpec((1, tc, d // 2), lambda bi, i: (0, bi * nt + i, 0)),
                pl.BlockSpec((1, tc, d // 2), lambda bi, i: (1, bi * nt + i, 0)),
                pl.BlockSpec((tc, LANES), lambda bi, i: (bi * nt + i, 0)),
                pl.BlockSpec((1, tc, d), tile),
                pl.BlockSpec((1, 1, d), row)]
    out_specs = [pl.BlockSpec((1, tc, d), tile)]
    out_shape = [jax.ShapeDtypeStruct((b, l, d), F32)]
    args = [yg, yg, gcol, x, g2]
    if pool_in:
        in_specs += [pl.BlockSpec((1, 1, d), row), pl.BlockSpec((1, 1, d), row),
                     pl.BlockSpec((1, d), fix), pl.BlockSpec((d, d), fix)]
        out_specs.append(pl.BlockSpec((1, tc, d), tile))
        out_shape.append(jax.ShapeDtypeStruct((b, l, d), BF16))
        args += list(pool_args)
    return pl.pallas_call(
        kern,
        grid=(b, nt),
        in_specs=in_specs, out_specs=out_specs, out_shape=out_shape,
        compiler_params=_params("arbitrary", "arbitrary"),
        name="moe_combine",
    )(*args)


def _moe(h2, ridx, gcol, counts, x1, g2, w_gate, w_up, w_down, layer, pool_args=None):
    b, l, d = x1.shape
    n = b * l
    n_blocks = (n * TOP_K) // SLOT_ROWS + N_EXPERTS
    cnt = counts[:, 0].astype(jnp.int32)
    padded = (cnt + SLOT_ROWS - 1) // SLOT_ROWS * SLOT_ROWS
    earlier = jnp.arange(N_EXPERTS)[None, :] < jnp.arange(N_EXPERTS)[:, None]
    pad_start = jnp.sum(jnp.where(earlier, padded[None, :], 0), axis=1).astype(jnp.int32)
    pad_end = pad_start + padded
    block_start = jnp.arange(n_blocks, dtype=jnp.int32) * SLOT_ROWS
    block_e = jnp.minimum(jnp.sum(pad_end[None, :] <= block_start[:, None], axis=1), N_EXPERTS - 1).astype(jnp.int32)
    own = block_e[:, None] == jnp.arange(N_EXPERTS)[None, :]
    data_end = jnp.sum(jnp.where(own, (pad_start + cnt)[None, :], 0), axis=1)
    n_valid = jnp.clip(data_end - block_start, 0, SLOT_ROWS).astype(jnp.int32)
    dest = _slot_index(pad_start, ridx, tn=2048)
    xs = _sc_scatter_rows(h2.reshape(n, d // 2), dest[0], dest[1], n_blocks * SLOT_ROWS)
    ys = _expert_ffn(block_e, n_valid, xs, w_gate, w_up, w_down, layer)
    yg = _sc_gather_rows(ys, dest[:TOP_K].reshape(TOP_K * n)).reshape(TOP_K, n, d // 2)
    return _combine(yg, gcol, x1, g2, pool_args, tc=512)


def kernel(x, c, ctx, c_ctx, ada_w, ada_b, norm1_g, norm2_g, attn_w_in, attn_w_out, attn_q_gain, attn_k_gain,
           attn_lq1, attn_lk1, attn_lq2, attn_lk2, attn_sub_gain, pool_w_in, pool_w_group, pool_scale, pool_w_out,
           router_w, router_b, moe_w_gate, moe_w_up, moe_w_down):
    b, l, d = x.shape
    n_ctx = ctx.shape[1]
    depth = ada_w.shape[0]
    assert depth == 2 and d == N_HEADS * V_DIM
    tm = 512

    mod = _adaln_mod(c, c_ctx, ada_w, ada_b)
    mods = [[mod[i, :b, None, j * d:(j + 1) * d] for j in range(N_MOD)] for i in range(depth)]
    mod_ctx = [jnp.broadcast_to(mod[0, b, j * d:(j + 1) * d], (b, 1, d)) for j in range(2)]

    rwt = router_w.T
    rwh = rwt.astype(BF16)
    rwl = (rwt - rwh.astype(F32)).astype(BF16)
    rb = router_b.reshape(N_EXPERTS, 1)

    sh1, s1, g1, sh2, s2, g2 = mods[0]
    cos, sin = _rope_tables(l)
    pair_up = (jnp.arange(LANES) & 16) == 0

    def gain_rows(g, factor):
        g2 = jnp.concatenate([g, g]) * factor
        return [g2, jnp.where(pair_up, jnp.roll(g2, -16), jnp.roll(g2, 16))]

    q_scale = HEAD_DIM ** -0.5 * math.log2(math.e)
    q_max = jnp.maximum(jnp.max(jnp.abs(attn_q_gain[0])) * (HEAD_DIM ** 0.5 * q_scale), F32_TINY)
    k_max = jnp.maximum(jnp.max(jnp.abs(attn_k_gain[0])) * HEAD_DIM ** 0.5, F32_TINY)
    need = jnp.ceil(jnp.log2(k_max / F8_MAX))
    room = jnp.floor(jnp.log2(F8_MAX / q_max))
    trade = jnp.exp2(jnp.clip(jnp.clip(0.0, need, jnp.maximum(need, room)), -60.0, 60.0))
    h_max = jnp.float32(0.0)
    for shift_, scale_ in ((sh1, s1), (mod_ctx[0], mod_ctx[1])):
        h_max = jnp.maximum(h_max, jnp.max(d ** 0.5 * jnp.max(jnp.abs(norm1_g[0] * (1.0 + scale_)), axis=-1)
                                           + jnp.sqrt(jnp.sum(shift_ * shift_, axis=-1))))
    w_v = attn_w_in[0][:, 2 * d:]
    v_max = 1.02 * h_max * jnp.sqrt(jnp.max(jnp.sum(w_v * w_v, axis=0)))
    v_grow = jnp.exp2(jnp.clip(jnp.ceil(jnp.log2(jnp.maximum(v_max, F32_TINY) / F8_MAX)), 0.0, 60.0))

    gains = jnp.stack(gain_rows(attn_q_gain[0], q_scale * trade) + gain_rows(attn_k_gain[0], 1.0 / trade))
    w_qk = attn_w_in[0][:, :2 * d].astype(BF16)
    wv_t = (w_v / v_grow).T.astype(BF16)
    assert l % tm == 0 and l % n_ctx == 0
    q, k_all, vt_all = _qkv_proj(x, sh1, s1, norm1_g[0][None], w_qk, wv_t, gains, cos, sin, n_qk=2, rope=True,
                                 tm=tm, n_keys=l + n_ctx, key_row0=0)
    k_all, vt_all = _qkv_proj(ctx, mod_ctx[0], mod_ctx[1], norm1_g[0][None], w_qk[:, d:], wv_t, gains,
                              cos[:n_ctx], sin[:n_ctx], n_qk=1, rope=False,
                              tm=n_ctx, n_keys=l + n_ctx, key_row0=l, kv=(k_all, vt_all))
    lam_init = 0.8 - 0.6 * math.exp(-0.3 * 0)
    lam_params = jnp.stack([attn_lq1[0], attn_lk1[0], attn_lq2[0], attn_lk2[0],
                            jnp.full((HEAD_DIM,), v_grow, F32)])
    o = _diff_attention(q, k_all, vt_all, lam_params, attn_sub_gain[0][:, None], lam_init=lam_init)

    fix = lambda bi, i: (0, 0)
    x1, h2, ridx, gcol, counts = _mixer_tail(
        (o, attn_w_out[0].astype(BF16)),
        (pl.BlockSpec((1, tm, d), lambda bi, i: (bi, i, 0)), pl.BlockSpec((d, d), fix)),
        _attn_tail_kernel, x, g1, sh2, s2, norm2_g[0][None], rwh, rwl, rb, tm=tm)

    sh1b, s1b, g1b, sh2b, s2b, g2b = mods[1]
    x2, u = _moe(h2, ridx, gcol, counts, x1, g2, moe_w_gate, moe_w_up, moe_w_down, 0,
                 pool_args=(sh1b, s1b, norm1_g[1][None], pool_w_in[0].astype(BF16)))
    gd = pool_w_group.shape[2]
    nh = tm // HALO_ROWS
    front_specs = (
        pl.BlockSpec((1, tm, d), lambda bi, i: (bi, i, 0)),
        pl.BlockSpec((1, HALO_ROWS, d), lambda bi, i: (bi, jnp.maximum(i * nh - 1, 0), 0)),
        pl.BlockSpec((1, HALO_ROWS, d), lambda bi, i: (bi, jnp.minimum((i + 1) * nh, l // HALO_ROWS - 1), 0)),
        pl.BlockSpec((len(POOL_WINDOWS), gd, gd), lambda bi, i: (0, 0, 0)),
        pl.BlockSpec((1, d), fix),
        pl.BlockSpec((d, d), fix))
    x3, h2b, ridx_b, gcol_b, counts_b = _mixer_tail(
        (u, u, u, pool_w_group[0].astype(BF16), pool_scale[0][None], pool_w_out[0].astype(BF16)),
        front_specs, functools.partial(_pool_tail_kernel, seq_len=l),
        x2, g1b, sh2b, s2b, norm2_g[1][None], rwh, rwl, rb, tm=tm,
        scratch=[pltpu.VMEM((tm + 2 * POOL_HALO, d), F32)] * 2)
    (out,) = _moe(h2b, ridx_b, gcol_b, counts_b, x3, g2b, moe_w_gate, moe_w_up, moe_w_down, 1)
    return out
```

```python
import functools
import math

import jax
import jax.numpy as jnp
import numpy as np
from jax import lax
from jax.experimental import pallas as pl
from jax.experimental.pallas import tpu as pltpu
from jax.experimental.pallas import tpu_sc as plsc

F32 = jnp.float32
BF16 = jnp.bfloat16
F8 = jnp.float8_e4m3fn
F8_MAX = float(jnp.finfo(F8).max)
F32_TINY = float(jnp.finfo(F32).tiny)

LANES = 128
SUBLANES = 8
N_HEADS = 8
HEAD_DIM = 64
V_DIM = 2 * HEAD_DIM
V_ROWS = V_DIM + 32
P_SHIFT = 8.0
GRID_W = 64
ROPE_THETA = 10000.0
NORM_EPS = 1e-6
N_MOD = 6
POOL_WINDOWS = (2, 4, 8, 16)
POOL_HALO = max(POOL_WINDOWS) // 2
HALO_ROWS = 16
N_EXPERTS = 32
N_EXPERT_GROUPS = 4
EXPERTS_PER_GROUP = N_EXPERTS // N_EXPERT_GROUPS
TOP_K = 2
SLOT_ROWS = 512
TAIL_ROWS = 512
SC_CORES = 2
SC_SUBCORES = 16
SC_WORKERS = SC_CORES * SC_SUBCORES
SC_WINDOW = 64
ATTN_GROUP_CHUNKS = 1
ATTN_SCORE_AHEAD = 1
VMEM_LIMIT = 48 * 1024 * 1024
NT_DIMS = (((1,), (1,)), ((), ()))


def _params(*sem):
    return pltpu.CompilerParams(dimension_semantics=sem, vmem_limit_bytes=VMEM_LIMIT)


def _rms_mod(x, gain, shift, scale):
    inv_rms = lax.rsqrt(jnp.mean(x * x, axis=-1, keepdims=True) + NORM_EPS)
    return x * inv_rms * (gain * (1.0 + scale)) + shift


def _pack_bf16_pairs(x):
    c = x.shape[1] // 2
    hi = lax.bitcast_convert_type(x[:, :c].astype(BF16).astype(F32), jnp.uint32)
    lo = lax.bitcast_convert_type(x[:, c:].astype(BF16).astype(F32), jnp.uint32)
    return lax.bitcast_convert_type(hi | (lo >> 16), jnp.int32)


def _unpack_bf16_pairs(w):
    u = lax.bitcast_convert_type(w, jnp.uint32)
    hi = lax.bitcast_convert_type(u & jnp.uint32(0xFFFF0000), F32)
    lo = lax.bitcast_convert_type(u << 16, F32)
    return jnp.concatenate([hi, lo], axis=1)


def _mod_kernel(c_ref, w_ref, b_ref, o_ref):
    c = c_ref[...]
    a = c / (1.0 + jnp.exp(-c))
    o_ref[0] = jnp.dot(a, w_ref[0], precision=lax.Precision.HIGHEST,
                       preferred_element_type=F32) + b_ref[0]


def _adaln_mod(c, c_ctx, ada_w, ada_b):
    depth, d, n_out = ada_w.shape
    b = c.shape[0]
    assert b + 1 <= SUBLANES
    rows = jnp.concatenate([c, c_ctx[None], jnp.zeros((SUBLANES - b - 1, d), F32)], axis=0)
    tn = n_out // 4
    return pl.pallas_call(
        _mod_kernel,
        grid=(depth, n_out // tn),
        in_specs=[pl.BlockSpec((SUBLANES, d), lambda i, j: (0, 0)),
                  pl.BlockSpec((1, d, tn), lambda i, j: (i, 0, j)),
                  pl.BlockSpec((1, 1, tn), lambda i, j: (i, 0, j))],
        out_specs=pl.BlockSpec((1, SUBLANES, tn), lambda i, j: (i, 0, j)),
        out_shape=jax.ShapeDtypeStruct((depth, SUBLANES, n_out), F32),
        compiler_params=_params("arbitrary", "arbitrary"),
        name="adaln_mod",
    )(rows, ada_w, ada_b.reshape(depth, 1, n_out))


def _qkv_kernel(x_ref, sh_ref, sc_ref, g_ref, w_ref, wvt_ref, gains_ref, cos_ref, sin_ref, *refs, n_qk, rope):
    out_refs = refs[-(n_qk + 1):]
    tm, d = x_ref.shape[1:]
    hb = _rms_mod(x_ref[0], g_ref[...], sh_ref[0], sc_ref[0]).astype(BF16)
    lane_b4 = (lax.broadcasted_iota(jnp.int32, (1, LANES), 1) & 16) == 0
    chunk_r = lax.broadcasted_iota(jnp.int32, (2 * LANES, 2 * LANES), 0) // HEAD_DIM
    chunk_c = lax.broadcasted_iota(jnp.int32, (2 * LANES, 2 * LANES), 1) // HEAD_DIM
    same_chunk = (chunk_r == chunk_c).astype(BF16)
    gains = gains_ref[...]
    tables = []
    for t in range(n_qk):
        r = 2 * (t + 2 - n_qk)
        tables.append((cos_ref[...] * gains[r:r + 1], sin_ref[...] * gains[r + 1:r + 2]) if rope
                      else (gains[r:r + 1], None))

    def project(t, j):
        return jnp.dot(hb, w_ref[:, t * d + j:t * d + j + 2 * LANES], preferred_element_type=F32)

    def chunk_sums(acc):
        return jnp.dot((acc * acc).astype(BF16), same_chunk, preferred_element_type=F32)

    def finish(t, j, acc, ssq):
        cos_t, sin_t = tables[t]
        nrm = acc * lax.rsqrt(ssq * (1.0 / HEAD_DIM) + NORM_EPS)
        for half in range(2):
            blk = nrm[:, half * LANES:(half + 1) * LANES]
            if rope:
                rot = jnp.where(lane_b4, pltpu.roll(blk, LANES - 16, 1), pltpu.roll(blk, 16, 1))
                y = blk * cos_t + rot * sin_t
            else:
                y = blk * cos_t
            c0 = j + half * LANES
            if t < n_qk - 1:
                yt = y.T
                first = lax.broadcasted_iota(jnp.int32, (V_DIM, 1), 0) < HEAD_DIM
                out_refs[t][0, c0 // LANES, 0, :, 0:tm] = jnp.where(first, yt, 0.0).astype(F8)
                out_refs[t][0, c0 // LANES, 0, :, tm:2 * tm] = jnp.where(first, 0.0, yt).astype(F8)
            else:
                out_refs[t][0, :, c0:c0 + LANES] = y.astype(F8)

    blocks = [(t, j) for t in range(n_qk) for j in range(0, d, 2 * LANES)]
    accs, sums = {}, {}
    for i in range(-2, len(blocks)):
        if i + 2 < len(blocks):
            accs[i + 2] = project(*blocks[i + 2])
        if 0 <= i + 1 < len(blocks):
            sums[i + 1] = chunk_sums(accs[i + 1])
        if i >= 0:
            finish(*blocks[i], accs.pop(i), sums.pop(i))
    vt_ref = out_refs[n_qk]
    ones = jnp.where(lax.broadcasted_iota(jnp.int32, (V_ROWS - V_DIM, tm), 0) == 0, 1.0, 0.0).astype(F8)
    for j in range(0, d, 2 * LANES):
        acc_t = lax.dot_general(wvt_ref[j:j + 2 * LANES, :], hb, NT_DIMS, preferred_element_type=F32)
        for half in range(2):
            h = j // LANES + half
            vt_ref[0, h, 0:V_DIM, :] = acc_t[half * V_DIM:(half + 1) * V_DIM].astype(F8)
            vt_ref[0, h, V_DIM:V_ROWS, :] = ones


def _qkv_proj(x, shift, scale, gain, w, wv_t, gains, cos, sin, *, n_qk, rope, tm, n_keys, key_row0, kv=None):
    b, l, d = x.shape
    n_out = n_qk + 1
    kern = functools.partial(_qkv_kernel, n_qk=n_qk, rope=rope)
    row = lambda bi, i: (bi, 0, 0)
    fix = lambda bi, i: (0, 0)
    kb = key_row0 // tm
    tile = pl.BlockSpec((1, tm, d), lambda bi, i: (bi, i, 0))
    k_spec = pl.BlockSpec((1, tm, d), lambda bi, i: (bi, kb + i, 0))
    vt_spec = pl.BlockSpec((1, N_HEADS, V_ROWS, tm), lambda bi, i: (bi, 0, 0, kb + i))
    q_spec = pl.BlockSpec((1, N_HEADS, 1, V_DIM, 2 * tm), lambda bi, i: (bi, 0, i, 0, 0))
    q_shape = jax.ShapeDtypeStruct((b, N_HEADS, l // tm, V_DIM, 2 * tm), F8)
    k_shape = jax.ShapeDtypeStruct((b, n_keys, d), F8)
    vt_shape = jax.ShapeDtypeStruct((b, N_HEADS, V_ROWS, n_keys), F8)
    in_specs = [tile,
                pl.BlockSpec((1, 1, d), row), pl.BlockSpec((1, 1, d), row),
                pl.BlockSpec((1, d), fix),
                pl.BlockSpec((d, n_qk * d), fix),
                pl.BlockSpec((d, d), fix),
                pl.BlockSpec((4, LANES), fix),
                pl.BlockSpec((tm, LANES), lambda bi, i: (i, 0)),
                pl.BlockSpec((tm, LANES), lambda bi, i: (i, 0))]
    args = [x, shift, scale, gain, w, wv_t, gains, cos, sin]
    aliases = {}
    if kv is not None:
        aliases = {len(args): n_qk - 1, len(args) + 1: n_qk}
        in_specs += [pl.BlockSpec(memory_space=pl.ANY)] * 2
        args += list(kv)
    return pl.pallas_call(
        kern,
        grid=(b, l // tm),
        in_specs=in_specs,
        out_specs=[q_spec] * (n_qk - 1) + [k_spec, vt_spec],
        out_shape=[q_shape] * (n_qk - 1) + [k_shape, vt_shape],
        input_output_aliases=aliases,
        compiler_params=_params("arbitrary", "arbitrary"),
        name="qkv_proj",
    )(*args)


def _rope_tables(n_tokens):
    rows = n_tokens // GRID_W
    row = np.repeat(np.arange(rows, dtype=np.float32), GRID_W)
    col = np.tile(np.arange(GRID_W, dtype=np.float32), rows)
    half = HEAD_DIM // 2
    inv_freq = (np.float32(ROPE_THETA) ** (-np.arange(0, half, 2, dtype=np.float32) / half)).astype(np.float32)
    ang_r = row[:, None] * inv_freq
    ang_c = col[:, None] * inv_freq
    ang = np.concatenate([ang_r, ang_r, ang_c, ang_c] * 2, axis=-1)
    sign = np.where((np.arange(LANES) & 16) == 0, -1.0, 1.0).astype(np.float32)
    return jnp.asarray(np.cos(ang), F32), jnp.asarray(np.sin(ang) * sign, F32)


def _attn_kernel(q_ref, k_ref, vt_ref, lp_ref, sg_ref, o_ref, s_ref, *, tk, group, ahead, lam_init):
    tq = q_ref.shape[4] // 2
    n_chunks = k_ref.shape[1] // tk
    qz = q_ref[0, 0, 0]
    n_slots = (ahead + 1) * group

    def score_chunk(c, m_grp):
        st = jnp.dot(k_ref[0, c * tk:(c + 1) * tk, :], qz, preferred_element_type=F32).astype(BF16)
        slot = c % n_slots
        s_ref[slot * tk:(slot + 1) * tk, :] = st
        mc = jnp.max(st, axis=0, keepdims=True)
        return mc if m_grp is None else jnp.maximum(m_grp, mc)

    def value_chunk(c, m_ref, part):
        slot = c % n_slots
        p = jnp.exp2(s_ref[slot * tk:(slot + 1) * tk, :] - (m_ref - P_SHIFT)).astype(F8)
        pv = jnp.dot(vt_ref[0, 0, :, c * tk:(c + 1) * tk], p, preferred_element_type=F32)
        return pv if part is None else part + pv

    groups = [list(range(g0, min(g0 + group, n_chunks))) for g0 in range(0, n_chunks, group)]
    m_of = {}
    for g in range(min(ahead, len(groups))):
        for c in groups[g]:
            m_of[g] = score_chunk(c, m_of.get(g))
    m = None
    acc = None
    for gi, cur in enumerate(groups):
        nxt = groups[gi + ahead] if gi + ahead < len(groups) else []
        m_new = m_of[gi] if m is None else jnp.maximum(m, m_of[gi])
        part = None
        for i in range(max(len(cur), len(nxt))):
            if i < len(nxt):
                m_of[gi + ahead] = score_chunk(nxt[i], m_of.get(gi + ahead))
            if i < len(cur):
                part = value_chunk(cur[i], m_new, part)
        acc = part if acc is None else acc * jnp.exp2(m.astype(F32) - m_new.astype(F32)) + part
        m = m_new
    acc = acc[:V_DIM] / acc[V_DIM:V_DIM + 1]
    lp = lp_ref[...]
    lam = (jnp.exp(jnp.sum(lp[0:1] * lp[1:2], axis=-1, keepdims=True))
           - jnp.exp(jnp.sum(lp[2:3] * lp[3:4], axis=-1, keepdims=True)) + lam_init)
    o = (acc[:, :tq] - lam * acc[:, tq:]) * lp[4:5, 0:1]
    o = o * lax.rsqrt(jnp.mean(o * o, axis=0, keepdims=True) + NORM_EPS) * sg_ref[...] * (1.0 - lam_init)
    o_ref[0] = o.T.astype(BF16)


def _attn_chunk(n_keys):
    for tk in (768, 512, 256, 128):
        if n_keys % tk == 0:
            return tk
    raise ValueError(f"key count {n_keys} is not a multiple of {LANES}")


def _diff_attention(qz, k_all, vt_all, lam_params, sub_gain, *, lam_init):
    b, _, n_tiles, _, tq2 = qz.shape
    tq = tq2 // 2
    l, d = n_tiles * tq, N_HEADS * V_DIM
    n_keys = k_all.shape[1]
    tk = _attn_chunk(n_keys)
    kern = functools.partial(_attn_kernel, tk=tk, group=ATTN_GROUP_CHUNKS, ahead=ATTN_SCORE_AHEAD, lam_init=lam_init)
    return pl.pallas_call(
        kern,
        grid=(b, N_HEADS, l // tq),
        in_specs=[pl.BlockSpec((1, 1, 1, V_DIM, 2 * tq), lambda bi, h, i: (bi, h, i, 0, 0)),
                  pl.BlockSpec((1, n_keys, V_DIM), lambda bi, h, i: (bi, 0, h)),
                  pl.BlockSpec((1, 1, V_ROWS, n_keys), lambda bi, h, i: (bi, h, 0, 0)),
                  pl.BlockSpec((5, HEAD_DIM), lambda bi, h, i: (0, 0)),
                  pl.BlockSpec((V_DIM, 1), lambda bi, h, i: (0, 0))],
        out_specs=pl.BlockSpec((1, tq, V_DIM), lambda bi, h, i: (bi, i, h)),
        out_shape=jax.ShapeDtypeStruct((b, l, d), BF16),
        scratch_shapes=[pltpu.VMEM(((ATTN_SCORE_AHEAD + 1) * ATTN_GROUP_CHUNKS * tk, 2 * tq), BF16)],
        compiler_params=_params("arbitrary", "arbitrary", "arbitrary"),
        name="diff_attention",
    )(qz, k_all, vt_all, lam_params, sub_gain)


def _route(h2, row0, rwh_ref, rwl_ref, rb_ref, before_ref, carry_ref, ridx_ref, gcol_ref, cnt_ref, is_first):
    tm = h2.shape[0]
    hh = h2.astype(BF16)
    hl = (h2 - hh.astype(F32)).astype(BF16)
    rw2 = jnp.concatenate([rwh_ref[...], rwl_ref[...]], axis=0)
    part = lax.dot_general(rw2, hh, NT_DIMS, preferred_element_type=F32)
    logits = (part[:N_EXPERTS] + part[N_EXPERTS:]
              + lax.dot_general(rwh_ref[...], hl, NT_DIMS, preferred_element_type=F32) + rb_ref[...])
    groups = [logits[g * EXPERTS_PER_GROUP:(g + 1) * EXPERTS_PER_GROUP] for g in range(N_EXPERT_GROUPS)]
    top = groups[0]
    for g in range(1, N_EXPERT_GROUPS):
        top = jnp.maximum(top, groups[g])
    top = jnp.max(top, axis=0, keepdims=True)
    sub = lax.broadcasted_iota(jnp.int32, (EXPERTS_PER_GROUP, tm), 0)
    best = None
    for g in range(N_EXPERT_GROUPS):
        ex = jnp.exp(groups[g] - top)
        v1 = jnp.max(ex, axis=0, keepdims=True)
        i1 = jnp.min(jnp.where(ex == v1, sub, EXPERTS_PER_GROUP), axis=0, keepdims=True)
        rest = jnp.where(sub == i1, -1.0, ex)
        v2 = jnp.max(rest, axis=0, keepdims=True)
        i2 = jnp.min(jnp.where(rest == v2, sub, EXPERTS_PER_GROUP), axis=0, keepdims=True)
        cand = (v1 + v2, v1, v2, i1 + g * EXPERTS_PER_GROUP, i2 + g * EXPERTS_PER_GROUP)
        if best is None:
            best = cand
        else:
            better = cand[0] > best[0]
            best = tuple(jnp.where(better, new, old) for new, old in zip(cand, best))
    _, v1, v2, e0, e1 = best
    gate0 = v1 / (v1 + v2)
    gate1 = v2 / (v1 + v2)

    @pl.when(is_first)
    def _():
        carry_ref[...] = jnp.zeros_like(carry_ref)

    erow = lax.broadcasted_iota(jnp.int32, (N_EXPERTS, tm), 0)
    oh0 = erow == e0
    oh1 = erow == e1
    chosen = jnp.where(oh0 | oh1, 1.0, 0.0)
    prior = carry_ref[:, 0:1] + jnp.dot(chosen.astype(BF16), before_ref[...], preferred_element_type=F32)
    r0 = jnp.sum(jnp.where(oh0, prior, 0.0), axis=0, keepdims=True).astype(jnp.int32)
    r1 = jnp.sum(jnp.where(oh1, prior, 0.0), axis=0, keepdims=True).astype(jnp.int32)
    carry_ref[...] = carry_ref[...] + jnp.sum(chosen, axis=1, keepdims=True)
    cnt_ref[...] = carry_ref[...]
    rid = lax.broadcasted_iota(jnp.int32, (SUBLANES, tm), 0)
    ridx_ref[:, row0:row0 + tm] = jnp.where(rid == 0, e0, jnp.where(rid == 1, e1, jnp.where(rid == 2, r0, jnp.where(rid == 3, r1, 0))))
    gid = lax.broadcasted_iota(jnp.int32, (LANES, tm), 0)
    gcol_ref[row0:row0 + tm, :] = jnp.where(gid == 0, gate0, jnp.where(gid == 1, gate1, 0.0)).T


def _tail(y, row0, x_ref, g1_ref, sh2_ref, s2_ref, n2_ref, rwh_ref, rwl_ref, rb_ref, before_ref,
          x_out, h2_out, ridx_ref, gcol_ref, cnt_ref, carry_ref):
    rows = slice(row0, row0 + y.shape[0])
    x1 = x_ref[0, rows] + g1_ref[0] * y
    x_out[0, rows] = x1
    h2 = _rms_mod(x1, n2_ref[...], sh2_ref[0], s2_ref[0])
    h2_out[0, rows] = _pack_bf16_pairs(h2)
    is_first = (pl.program_id(0) == 0) & (pl.program_id(1) == 0) & (row0 == 0)
    _route(h2, row0, rwh_ref, rwl_ref, rb_ref, before_ref, carry_ref, ridx_ref, gcol_ref, cnt_ref, is_first)


def _attn_tail_kernel(a_ref, wo_ref, *rest):
    starts = range(0, a_ref.shape[1], TAIL_ROWS)
    ys = [jnp.dot(a_ref[0, r0:r0 + TAIL_ROWS], wo_ref[...], preferred_element_type=F32) for r0 in starts]
    for r0, y in zip(starts, ys):
        _tail(y, r0, *rest)


def _pool_tail_kernel(u_ref, up_ref, un_ref, wg_ref, cs_ref, wo_ref, *rest, seq_len):
    *tail_refs, ubuf, abuf = rest
    tm = u_ref.shape[1]
    i = pl.program_id(1)
    u = u_ref[0].astype(F32)
    ubuf[0:POOL_HALO] = jnp.where(i > 0, up_ref[0].astype(F32)[HALO_ROWS - POOL_HALO:], 0.0)
    ubuf[POOL_HALO:POOL_HALO + tm] = u
    ubuf[POOL_HALO + tm:2 * POOL_HALO + tm] = jnp.where(i < pl.num_programs(1) - 1,
                                                        un_ref[0].astype(F32)[:POOL_HALO], 0.0)
    pos = i * tm + lax.broadcasted_iota(jnp.int32, (tm, 1), 0)
    gd = wg_ref.shape[1]
    assert all(win == 2 ** (g + 1) for g, win in enumerate(POOL_WINDOWS))
    n_ext = tm + 2 * POOL_HALO
    bufs = (ubuf, abuf)
    abuf[1:n_ext, :] = ubuf[0:n_ext - 1, :] + ubuf[1:n_ext, :]
    lo, hi = 1, n_ext
    for g in range(1, len(POOL_WINDOWS)):
        src, dst = bufs[g % 2], bufs[(g + 1) % 2]
        sh = POOL_WINDOWS[g] // 4
        dst[lo + sh:hi - sh, g * gd:] = src[lo:hi - 2 * sh, g * gd:] + src[lo + 2 * sh:hi, g * gd:]
        lo, hi = lo + sh, hi - sh
    outs = []
    for g, win in enumerate(POOL_WINDOWS):
        half = win // 2
        cols = slice(g * gd, (g + 1) * gd)
        s = bufs[(g + 1) % 2][POOL_HALO:POOL_HALO + tm, cols]
        inv_cnt = 1.0 / (jnp.minimum(pos + half, seq_len) - jnp.maximum(pos - half, 0)).astype(F32)
        dlt = (s * inv_cnt - u[:, cols]).astype(BF16)
        outs.append(jnp.dot(dlt, wg_ref[g], preferred_element_type=F32))
    z = (jnp.concatenate(outs, axis=-1) * cs_ref[...]).astype(BF16)
    y = jnp.dot(z, wo_ref[...], preferred_element_type=F32)
    _tail(y, 0, *tail_refs)


def _mixer_tail(front_args, front_specs, kern, x, g1, sh2, s2, n2g, rwh, rwl, rb, *, tm, scratch=()):
    b, l, d = x.shape
    nt = l // tm
    n = b * l
    row = lambda bi, i: (bi, 0, 0)
    fix = lambda bi, i: (0, 0)
    tile = lambda bi, i: (bi, i, 0)
    in_specs = list(front_specs) + [
        pl.BlockSpec((1, tm, d), tile),
        pl.BlockSpec((1, 1, d), row), pl.BlockSpec((1, 1, d), row), pl.BlockSpec((1, 1, d), row),
        pl.BlockSpec((1, d), fix),
        pl.BlockSpec((N_EXPERTS, d), fix), pl.BlockSpec((N_EXPERTS, d), fix),
        pl.BlockSpec((N_EXPERTS, 1), fix),
        pl.BlockSpec((TAIL_ROWS, TAIL_ROWS), fix)]
    before = jnp.asarray(np.triu(np.ones((TAIL_ROWS, TAIL_ROWS), np.float32), k=1), BF16)
    out_specs = [pl.BlockSpec((1, tm, d), tile), pl.BlockSpec((1, tm, d // 2), tile),
                 pl.BlockSpec((SUBLANES, tm), lambda bi, i: (0, bi * nt + i)),
                 pl.BlockSpec((tm, LANES), lambda bi, i: (bi * nt + i, 0)),
                 pl.BlockSpec((N_EXPERTS, LANES), fix)]
    out_shape = [jax.ShapeDtypeStruct((b, l, d), F32), jax.ShapeDtypeStruct((b, l, d // 2), jnp.int32),
                 jax.ShapeDtypeStruct((SUBLANES, n), jnp.int32), jax.ShapeDtypeStruct((n, LANES), F32),
                 jax.ShapeDtypeStruct((N_EXPERTS, LANES), F32)]
    return pl.pallas_call(
        kern,
        grid=(b, nt),
        in_specs=in_specs, out_specs=out_specs, out_shape=out_shape,
        scratch_shapes=[pltpu.VMEM((N_EXPERTS, LANES), F32)] + list(scratch),
        compiler_params=_params("arbitrary", "arbitrary"),
        name="mixer_tail",
    )(*front_args, x, g1, sh2, s2, n2g, rwh, rwl, rb, before)


def _slot_kernel(ps_ref, ridx_ref, dest_ref):
    ridx = ridx_ref[...]
    ps = ps_ref[...]
    erow = lax.broadcasted_iota(jnp.int32, (N_EXPERTS, ridx.shape[1]), 0)
    rows = []
    for k in range(TOP_K):
        start = jnp.sum(jnp.where(erow == ridx[k:k + 1], ps, 0), axis=0, keepdims=True)
        rows.append(start + ridx[TOP_K + k:TOP_K + k + 1])
    rid = lax.broadcasted_iota(jnp.int32, ridx.shape, 0)
    dest_ref[...] = jnp.where(rid == 0, rows[0], jnp.where(rid == 1, rows[1], 0))


def _slot_index(pad_start, ridx, *, tn):
    n = ridx.shape[1]
    return pl.pallas_call(
        _slot_kernel,
        grid=(n // tn,),
        in_specs=[pl.BlockSpec((N_EXPERTS, 1), lambda i: (0, 0)),
                  pl.BlockSpec((SUBLANES, tn), lambda i: (0, i))],
        out_specs=pl.BlockSpec((SUBLANES, tn), lambda i: (0, i)),
        out_shape=jax.ShapeDtypeStruct((SUBLANES, n), jnp.int32),
        compiler_params=_params("arbitrary"),
        name="slot_index",
    )(pad_start.reshape(N_EXPERTS, 1), ridx)


def _sc_mesh():
    return plsc.VectorSubcoreMesh(core_axis_name="c", subcore_axis_name="s",
                                  num_cores=SC_CORES, num_subcores=SC_SUBCORES)


def _sc_worker_base(per_worker):
    return (lax.axis_index("s") * SC_CORES + lax.axis_index("c")) * per_worker


def _sc_scatter_rows(rows, idx0, idx1, n_slots):
    n, d = rows.shape
    per_worker = n // SC_WORKERS
    assert per_worker % SC_WINDOW == 0

    n_win = per_worker // SC_WINDOW
    assert n_win % 2 == 0

    def body(rows_hbm, i0_hbm, i1_hbm, out_hbm, i0_a, i1_a, rows_a, i0_b, i1_b, rows_b, sem_a, sem_b):
        base = _sc_worker_base(per_worker)

        def offset(j):
            return pl.multiple_of(base + j * SC_WINDOW, SC_WINDOW)

        def start(j, i0_v, i1_v, rows_v, sem):
            pltpu.sync_copy(i0_hbm.at[pl.ds(offset(j), SC_WINDOW)], i0_v)
            pltpu.sync_copy(i1_hbm.at[pl.ds(offset(j), SC_WINDOW)], i1_v)
            pltpu.async_copy(rows_hbm.at[pl.ds(offset(j), SC_WINDOW)], rows_v, sem)

        def finish(j, i0_v, i1_v, rows_v, sem):
            pltpu.make_async_copy(rows_hbm.at[pl.ds(offset(j), SC_WINDOW)], rows_v, sem).wait()
            pltpu.sync_copy(rows_v, out_hbm.at[i0_v])
            pltpu.sync_copy(rows_v, out_hbm.at[i1_v])

        start(0, i0_a, i1_a, rows_a, sem_a)

        @pl.loop(0, n_win, step=2)
        def _(j):
            start(j + 1, i0_b, i1_b, rows_b, sem_b)
            finish(j, i0_a, i1_a, rows_a, sem_a)

            @pl.when(j + 2 < n_win)
            def _():
                start(j + 2, i0_a, i1_a, rows_a, sem_a)

            finish(j + 1, i0_b, i1_b, rows_b, sem_b)

    window = [pltpu.VMEM((SC_WINDOW,), jnp.int32), pltpu.VMEM((SC_WINDOW,), jnp.int32),
              pltpu.VMEM((SC_WINDOW, d), rows.dtype)]
    return pl.kernel(
        body, out_type=jax.ShapeDtypeStruct((n_slots, d), rows.dtype), mesh=_sc_mesh(),
        scratch_types=window + window + [pltpu.SemaphoreType.DMA, pltpu.SemaphoreType.DMA],
        name="sc_scatter_rows",
    )(rows, idx0, idx1)


def _sc_gather_rows(table, idx):
    n = idx.shape[0]
    d = table.shape[1]
    per_worker = n // SC_WORKERS
    assert per_worker % SC_WINDOW == 0

    n_win = per_worker // SC_WINDOW
    assert n_win % 2 == 0

    def body(table_hbm, idx_hbm, out_hbm, idx_a, idx_b, rows_a, rows_b, sem_a, sem_b):
        base = _sc_worker_base(per_worker)

        def offset(j):
            return pl.multiple_of(base + j * SC_WINDOW, SC_WINDOW)

        def start(j, idx_v, rows_v, sem):
            pltpu.sync_copy(idx_hbm.at[pl.ds(offset(j), SC_WINDOW)], idx_v)
            pltpu.async_copy(table_hbm.at[idx_v], rows_v, sem)

        def finish(j, idx_v, rows_v, sem):
            pltpu.make_async_copy(table_hbm.at[idx_v], rows_v, sem).wait()
            pltpu.sync_copy(rows_v, out_hbm.at[pl.ds(offset(j), SC_WINDOW)])

        start(0, idx_a, rows_a, sem_a)

        @pl.loop(0, n_win, step=2)
        def _(j):
            start(j + 1, idx_b, rows_b, sem_b)
            finish(j, idx_a, rows_a, sem_a)

            @pl.when(j + 2 < n_win)
            def _():
                start(j + 2, idx_a, rows_a, sem_a)

            finish(j + 1, idx_b, rows_b, sem_b)

    return pl.kernel(
        body, out_type=jax.ShapeDtypeStruct((n, d), table.dtype), mesh=_sc_mesh(),
        scratch_types=[pltpu.VMEM((SC_WINDOW,), jnp.int32), pltpu.VMEM((SC_WINDOW,), jnp.int32),
                       pltpu.VMEM((SC_WINDOW, d), table.dtype), pltpu.VMEM((SC_WINDOW, d), table.dtype),
                       pltpu.SemaphoreType.DMA, pltpu.SemaphoreType.DMA],
        name="sc_gather_rows",
    )(table, idx)


def _ffn_kernel(be_ref, nv_ref, xs_ref, wg_ref, wu_ref, wd_ref, ys_ref, wg_b, wu_b, wd_b):
    j = pl.program_id(0)
    valid = nv_ref[j]

    @pl.when((valid > 0) & ((j == 0) | (be_ref[j] != be_ref[jnp.maximum(j - 1, 0)])))
    def _():
        wg_b[...] = wg_ref[0, 0].astype(BF16)
        wu_b[...] = wu_ref[0, 0].astype(BF16)
        wd_b[...] = wd_ref[0, 0].astype(BF16)

    @pl.when(valid > 0)
    def _():
        row = lax.broadcasted_iota(jnp.int32, (SLOT_ROWS, 1), 0)
        xw = jnp.where(row < valid, xs_ref[...], 0)
        xb = _unpack_bf16_pairs(xw).astype(BF16)
        g = jnp.dot(xb, wg_b[...], preferred_element_type=F32)
        u = jnp.dot(xb, wu_b[...], preferred_element_type=F32)
        a = (g / (1.0 + jnp.exp(-g)) * u).astype(BF16)
        ys_ref[...] = _pack_bf16_pairs(jnp.dot(a, wd_b[...], preferred_element_type=F32))

    @pl.when(valid <= 0)
    def _():
        ys_ref[...] = jnp.zeros_like(ys_ref)


def _expert_ffn(block_e, n_valid, xs, w_gate, w_up, w_down, layer):
    n_slots, dw = xs.shape
    d, de = w_gate.shape[2:]
    w_idx = lambda j, be, nv: (layer, be[j], 0, 0)
    return pl.pallas_call(
        _ffn_kernel,
        grid_spec=pltpu.PrefetchScalarGridSpec(
            num_scalar_prefetch=2,
            grid=(n_slots // SLOT_ROWS,),
            in_specs=[pl.BlockSpec((SLOT_ROWS, dw), lambda j, be, nv: (j, 0)),
                      pl.BlockSpec((1, 1, d, de), w_idx),
                      pl.BlockSpec((1, 1, d, de), w_idx),
                      pl.BlockSpec((1, 1, de, d), w_idx)],
            out_specs=pl.BlockSpec((SLOT_ROWS, dw), lambda j, be, nv: (j, 0)),
            scratch_shapes=[pltpu.VMEM((d, de), BF16), pltpu.VMEM((d, de), BF16), pltpu.VMEM((de, d), BF16)]),
        out_shape=jax.ShapeDtypeStruct((n_slots, dw), jnp.int32),
        compiler_params=_params("arbitrary"),
        name="expert_ffn",
    )(block_e, n_valid, xs, w_gate, w_up, w_down)


def _combine_kernel(y0_ref, y1_ref, gcol_ref, x_ref, g2_ref, *rest, pool_in):
    if pool_in:
        sh_ref, sc_ref, n1_ref, wi_ref, x_out, u_out = rest
    else:
        (x_out,) = rest
    gc = gcol_ref[...]
    out = gc[:, 0:1] * _unpack_bf16_pairs(y0_ref[0]) + gc[:, 1:2] * _unpack_bf16_pairs(y1_ref[0])
    x2 = x_ref[0] + g2_ref[0] * out
    x_out[0] = x2
    if pool_in:
        hb = _rms_mod(x2, n1_ref[...], sh_ref[0], sc_ref[0]).astype(BF16)
        u_out[0] = jnp.dot(hb, wi_ref[...], preferred_element_type=F32).astype(BF16)


def _combine(yg, gcol, x, g2, pool_args=None, *, tc):
    b, l, d = x.shape
    nt = l // tc
    pool_in = pool_args is not None
    kern = functools.partial(_combine_kernel, pool_in=pool_in)
    row = lambda bi, i: (bi, 0, 0)
    fix = lambda bi, i: (0, 0)
    tile = lambda bi, i: (bi, i, 0)
    in_specs = [pl.BlockSpec((1, tc, d // 2), lambda bi, i: (0, bi * nt + i, 0)),
                pl.BlockSpec((1, tc, d // 2), lambda bi, i: (1, bi * nt + i, 0)),
                pl.BlockSpec((tc, LANES), lambda bi, i: (bi * nt + i, 0)),
                pl.BlockSpec((1, tc, d), tile),
                pl.BlockSpec((1, 1, d), row)]
    out_specs = [pl.BlockSpec((1, tc, d), tile)]
    out_shape = [jax.ShapeDtypeStruct((b, l, d), F32)]
    args = [yg, yg, gcol, x, g2]
    if pool_in:
        in_specs += [pl.BlockSpec((1, 1, d), row), pl.BlockSpec((1, 1, d), row),
                     pl.BlockSpec((1, d), fix), pl.BlockSpec((d, d), fix)]
        out_specs.append(pl.BlockSpec((1, tc, d), tile))
        out_shape.append(jax.ShapeDtypeStruct((b, l, d), BF16))
        args += list(pool_args)
    return pl.pallas_call(
        kern,
        grid=(b, nt),
        in_specs=in_specs, out_specs=out_specs, out_shape=out_shape,
        compiler_params=_params("arbitrary", "arbitrary"),
        name="moe_combine",
    )(*args)


def _moe(h2, ridx, gcol, counts, x1, g2, w_gate, w_up, w_down, layer, pool_args=None):
    b, l, d = x1.shape
    n = b * l
    n_blocks = (n * TOP_K) // SLOT_ROWS + N_EXPERTS
    cnt = counts[:, 0].astype(jnp.int32)
    padded = (cnt + SLOT_ROWS - 1) // SLOT_ROWS * SLOT_ROWS
    earlier = jnp.arange(N_EXPERTS)[None, :] < jnp.arange(N_EXPERTS)[:, None]
    pad_start = jnp.sum(jnp.where(earlier, padded[None, :], 0), axis=1).astype(jnp.int32)
    pad_end = pad_start + padded
    block_start = jnp.arange(n_blocks, dtype=jnp.int32) * SLOT_ROWS
    block_e = jnp.minimum(jnp.sum(pad_end[None, :] <= block_start[:, None], axis=1), N_EXPERTS - 1).astype(jnp.int32)
    own = block_e[:, None] == jnp.arange(N_EXPERTS)[None, :]
    data_end = jnp.sum(jnp.where(own, (pad_start + cnt)[None, :], 0), axis=1)
    n_valid = jnp.clip(data_end - block_start, 0, SLOT_ROWS).astype(jnp.int32)
    dest = _slot_index(pad_start, ridx, tn=2048)
    xs = _sc_scatter_rows(h2.reshape(n, d // 2), dest[0], dest[1], n_blocks * SLOT_ROWS)
    ys = _expert_ffn(block_e, n_valid, xs, w_gate, w_up, w_down, layer)
    yg = _sc_gather_rows(ys, dest[:TOP_K].reshape(TOP_K * n)).reshape(TOP_K, n, d // 2)
    return _combine(yg, gcol, x1, g2, pool_args, tc=512)


def kernel(x, c, ctx, c_ctx, ada_w, ada_b, norm1_g, norm2_g, attn_w_in, attn_w_out, attn_q_gain, attn_k_gain,
           attn_lq1, attn_lk1, attn_lq2, attn_lk2, attn_sub_gain, pool_w_in, pool_w_group, pool_scale, pool_w_out,
           router_w, router_b, moe_w_gate, moe_w_up, moe_w_down):
    b, l, d = x.shape
    n_ctx = ctx.shape[1]
    depth = ada_w.shape[0]
    assert depth == 2 and d == N_HEADS * V_DIM
    tm = 512

    mod = _adaln_mod(c, c_ctx, ada_w, ada_b)
    mods = [[mod[i, :b, None, j * d:(j + 1) * d] for j in range(N_MOD)] for i in range(depth)]
    mod_ctx = [jnp.broadcast_to(mod[0, b, j * d:(j + 1) * d], (b, 1, d)) for j in range(2)]

    rwt = router_w.T
    rwh = rwt.astype(BF16)
    rwl = (rwt - rwh.astype(F32)).astype(BF16)
    rb = router_b.reshape(N_EXPERTS, 1)

    sh1, s1, g1, sh2, s2, g2 = mods[0]
    cos, sin = _rope_tables(l)
    pair_up = (jnp.arange(LANES) & 16) == 0

    def gain_rows(g, factor):
        g2 = jnp.concatenate([g, g]) * factor
        return [g2, jnp.where(pair_up, jnp.roll(g2, -16), jnp.roll(g2, 16))]

    q_scale = HEAD_DIM ** -0.5 * math.log2(math.e)
    q_max = jnp.maximum(jnp.max(jnp.abs(attn_q_gain[0])) * (HEAD_DIM ** 0.5 * q_scale), F32_TINY)
    k_max = jnp.maximum(jnp.max(jnp.abs(attn_k_gain[0])) * HEAD_DIM ** 0.5, F32_TINY)
    need = jnp.ceil(jnp.log2(k_max / F8_MAX))
    room = jnp.floor(jnp.log2(F8_MAX / q_max))
    trade = jnp.exp2(jnp.clip(jnp.clip(0.0, need, jnp.maximum(need, room)), -60.0, 60.0))
    h_max = jnp.float32(0.0)
    for shift_, scale_ in ((sh1, s1), (mod_ctx[0], mod_ctx[1])):
        h_max = jnp.maximum(h_max, jnp.max(d ** 0.5 * jnp.max(jnp.abs(norm1_g[0] * (1.0 + scale_)), axis=-1)
                                           + jnp.sqrt(jnp.sum(shift_ * shift_, axis=-1))))
    w_v = attn_w_in[0][:, 2 * d:]
    v_max = 1.02 * h_max * jnp.sqrt(jnp.max(jnp.sum(w_v * w_v, axis=0)))
    v_grow = jnp.exp2(jnp.clip(jnp.ceil(jnp.log2(jnp.maximum(v_max, F32_TINY) / F8_MAX)), 0.0, 60.0))

    gains = jnp.stack(gain_rows(attn_q_gain[0], q_scale * trade) + gain_rows(attn_k_gain[0], 1.0 / trade))
    w_qk = attn_w_in[0][:, :2 * d].astype(BF16)
    wv_t = (w_v / v_grow).T.astype(BF16)
    assert l % tm == 0 and l % n_ctx == 0
    q, k_all, vt_all = _qkv_proj(x, sh1, s1, norm1_g[0][None], w_qk, wv_t, gains, cos, sin, n_qk=2, rope=True,
                                 tm=tm, n_keys=l + n_ctx, key_row0=0)
    k_all, vt_all = _qkv_proj(ctx, mod_ctx[0], mod_ctx[1], norm1_g[0][None], w_qk[:, d:], wv_t, gains,
                              cos[:n_ctx], sin[:n_ctx], n_qk=1, rope=False,
                              tm=n_ctx, n_keys=l + n_ctx, key_row0=l, kv=(k_all, vt_all))
    lam_init = 0.8 - 0.6 * math.exp(-0.3 * 0)
    lam_params = jnp.stack([attn_lq1[0], attn_lk1[0], attn_lq2[0], attn_lk2[0],
                            jnp.full((HEAD_DIM,), v_grow, F32)])
    o = _diff_attention(q, k_all, vt_all, lam_params, attn_sub_gain[0][:, None], lam_init=lam_init)

    fix = lambda bi, i: (0, 0)
    x1, h2, ridx, gcol, counts = _mixer_tail(
        (o, attn_w_out[0].astype(BF16)),
        (pl.BlockSpec((1, 2 * TAIL_ROWS, d), lambda bi, i: (bi, i, 0)), pl.BlockSpec((d, d), fix)),
        _attn_tail_kernel, x, g1, sh2, s2, norm2_g[0][None], rwh, rwl, rb, tm=2 * TAIL_ROWS)

    sh1b, s1b, g1b, sh2b, s2b, g2b = mods[1]
    x2, u = _moe(h2, ridx, gcol, counts, x1, g2, moe_w_gate, moe_w_up, moe_w_down, 0,
                 pool_args=(sh1b, s1b, norm1_g[1][None], pool_w_in[0].astype(BF16)))
    gd = pool_w_group.shape[2]
    nh = tm // HALO_ROWS
    front_specs = (
        pl.BlockSpec((1, tm, d), lambda bi, i: (bi, i, 0)),
        pl.BlockSpec((1, HALO_ROWS, d), lambda bi, i: (bi, jnp.maximum(i * nh - 1, 0), 0)),
        pl.BlockSpec((1, HALO_ROWS, d), lambda bi, i: (bi, jnp.minimum((i + 1) * nh, l // HALO_ROWS - 1), 0)),
        pl.BlockSpec((len(POOL_WINDOWS), gd, gd), lambda bi, i: (0, 0, 0)),
        pl.BlockSpec((1, d), fix),
        pl.BlockSpec((d, d), fix))
    x3, h2b, ridx_b, gcol_b, counts_b = _mixer_tail(
        (u, u, u, pool_w_group[0].astype(BF16), pool_scale[0][None], pool_w_out[0].astype(BF16)),
        front_specs, functools.partial(_pool_tail_kernel, seq_len=l),
        x2, g1b, sh2b, s2b, norm2_g[1][None], rwh, rwl, rb, tm=tm,
        scratch=[pltpu.VMEM((tm + 2 * POOL_HALO, d), F32)] * 2)
    (out,) = _moe(h2b, ridx_b, gcol_b, counts_b, x3, g2b, moe_w_gate, moe_w_up, moe_w_down, 1)
    return out
```

```python
import functools
import math

import jax
import jax.numpy as jnp
import numpy as np
from jax import lax
from jax.experimental import pallas as pl
from jax.experimental.pallas import tpu as pltpu
from jax.experimental.pallas import tpu_sc as plsc

F32 = jnp.float32
BF16 = jnp.bfloat16
F8 = jnp.float8_e4m3fn
F8_MAX = float(jnp.finfo(F8).max)
F32_TINY = float(jnp.finfo(F32).tiny)

LANES = 128
SUBLANES = 8
N_HEADS = 8
HEAD_DIM = 64
V_DIM = 2 * HEAD_DIM
V_ROWS = V_DIM + 32
P_SHIFT = 8.0
GRID_W = 64
ROPE_THETA = 10000.0
NORM_EPS = 1e-6
N_MOD = 6
POOL_WINDOWS = (2, 4, 8, 16)
POOL_HALO = max(POOL_WINDOWS) // 2
HALO_ROWS = 16
N_EXPERTS = 32
N_EXPERT_GROUPS = 4
EXPERTS_PER_GROUP = N_EXPERTS // N_EXPERT_GROUPS
TOP_K = 2
SLOT_ROWS = 512
TAIL_ROWS = 512
SC_CORES = 2
SC_SUBCORES = 16
SC_WORKERS = SC_CORES * SC_SUBCORES
SC_WINDOW = 64
ATTN_GROUP_CHUNKS = 1
ATTN_SCORE_AHEAD = 1
VMEM_LIMIT = 48 * 1024 * 1024
NT_DIMS = (((1,), (1,)), ((), ()))


def _params(*sem):
    return pltpu.CompilerParams(dimension_semantics=sem, vmem_limit_bytes=VMEM_LIMIT)


def _rms_mod(x, gain, shift, scale):
    inv_rms = lax.rsqrt(jnp.mean(x * x, axis=-1, keepdims=True) + NORM_EPS)
    return x * inv_rms * (gain * (1.0 + scale)) + shift


def _pack_bf16_pairs(x):
    c = x.shape[1] // 2
    hi = lax.bitcast_convert_type(x[:, :c].astype(BF16).astype(F32), jnp.uint32)
    lo = lax.bitcast_convert_type(x[:, c:].astype(BF16).astype(F32), jnp.uint32)
    return lax.bitcast_convert_type(hi | (lo >> 16), jnp.int32)


def _unpack_bf16_pairs(w):
    u = lax.bitcast_convert_type(w, jnp.uint32)
    hi = lax.bitcast_convert_type(u & jnp.uint32(0xFFFF0000), F32)
    lo = lax.bitcast_convert_type(u << 16, F32)
    return jnp.concatenate([hi, lo], axis=1)


def _mod_kernel(c_ref, w_ref, b_ref, o_ref):
    c = c_ref[...]
    a = c / (1.0 + jnp.exp(-c))
    o_ref[0] = jnp.dot(a, w_ref[0], precision=lax.Precision.HIGHEST,
                       preferred_element_type=F32) + b_ref[0]


def _adaln_mod(c, c_ctx, ada_w, ada_b):
    depth, d, n_out = ada_w.shape
    b = c.shape[0]
    assert b + 1 <= SUBLANES
    rows = jnp.concatenate([c, c_ctx[None], jnp.zeros((SUBLANES - b - 1, d), F32)], axis=0)
    tn = n_out // 4
    return pl.pallas_call(
        _mod_kernel,
        grid=(depth, n_out // tn),
        in_specs=[pl.BlockSpec((SUBLANES, d), lambda i, j: (0, 0)),
                  pl.BlockSpec((1, d, tn), lambda i, j: (i, 0, j)),
                  pl.BlockSpec((1, 1, tn), lambda i, j: (i, 0, j))],
        out_specs=pl.BlockSpec((1, SUBLANES, tn), lambda i, j: (i, 0, j)),
        out_shape=jax.ShapeDtypeStruct((depth, SUBLANES, n_out), F32),
        compiler_params=_params("arbitrary", "arbitrary"),
        name="adaln_mod",
    )(rows, ada_w, ada_b.reshape(depth, 1, n_out))


def _qkv_kernel(x_ref, sh_ref, sc_ref, g_ref, w_ref, wvt_ref, gains_ref, cos_ref, sin_ref, *refs, n_qk, rope):
    out_refs = refs[-(n_qk + 1):]
    tm, d = x_ref.shape[1:]
    hb = _rms_mod(x_ref[0], g_ref[...], sh_ref[0], sc_ref[0]).astype(BF16)
    lane_b4 = (lax.broadcasted_iota(jnp.int32, (1, LANES), 1) & 16) == 0
    chunk_r = lax.broadcasted_iota(jnp.int32, (2 * LANES, 2 * LANES), 0) // HEAD_DIM
    chunk_c = lax.broadcasted_iota(jnp.int32, (2 * LANES, 2 * LANES), 1) // HEAD_DIM
    same_chunk = (chunk_r == chunk_c).astype(BF16)
    gains = gains_ref[...]
    tables = []
    for t in range(n_qk):
        r = 2 * (t + 2 - n_qk)
        tables.append((cos_ref[...] * gains[r:r + 1], sin_ref[...] * gains[r + 1:r + 2]) if rope
                      else (gains[r:r + 1], None))

    def project(t, j):
        return jnp.dot(hb, w_ref[:, t * d + j:t * d + j + 2 * LANES], preferred_element_type=F32)

    def chunk_sums(acc):
        return jnp.dot((acc * acc).astype(BF16), same_chunk, preferred_element_type=F32)

    def finish(t, j, acc, ssq):
        cos_t, sin_t = tables[t]
        nrm = acc * lax.rsqrt(ssq * (1.0 / HEAD_DIM) + NORM_EPS)
        for half in range(2):
            blk = nrm[:, half * LANES:(half + 1) * LANES]
            if rope:
                rot = jnp.where(lane_b4, pltpu.roll(blk, LANES - 16, 1), pltpu.roll(blk, 16, 1))
                y = blk * cos_t + rot * sin_t
            else:
                y = blk * cos_t
            c0 = j + half * LANES
            if t < n_qk - 1:
                yt = y.T
                first = lax.broadcasted_iota(jnp.int32, (V_DIM, 1), 0) < HEAD_DIM
                out_refs[t][0, c0 // LANES, 0, :, 0:tm] = jnp.where(first, yt, 0.0).astype(F8)
                out_refs[t][0, c0 // LANES, 0, :, tm:2 * tm] = jnp.where(first, 0.0, yt).astype(F8)
            else:
                out_refs[t][0, :, c0:c0 + LANES] = y.astype(F8)

    blocks = [(t, j) for t in range(n_qk) for j in range(0, d, 2 * LANES)]
    accs, sums = {}, {}
    for i in range(-2, len(blocks)):
        if i + 2 < len(blocks):
            accs[i + 2] = project(*blocks[i + 2])
        if 0 <= i + 1 < len(blocks):
            sums[i + 1] = chunk_sums(accs[i + 1])
        if i >= 0:
            finish(*blocks[i], accs.pop(i), sums.pop(i))
    vt_ref = out_refs[n_qk]
    ones = jnp.where(lax.broadcasted_iota(jnp.int32, (V_ROWS - V_DIM, tm), 0) == 0, 1.0, 0.0).astype(F8)
    for j in range(0, d, 2 * LANES):
        acc_t = lax.dot_general(wvt_ref[j:j + 2 * LANES, :], hb, NT_DIMS, preferred_element_type=F32)
        for half in range(2):
            h = j // LANES + half
            vt_ref[0, h, 0:V_DIM, :] = acc_t[half * V_DIM:(half + 1) * V_DIM].astype(F8)
            vt_ref[0, h, V_DIM:V_ROWS, :] = ones


def _qkv_proj(x, shift, scale, gain, w, wv_t, gains, cos, sin, *, n_qk, rope, tm, n_keys, key_row0, kv=None):
    b, l, d = x.shape
    n_out = n_qk + 1
    kern = functools.partial(_qkv_kernel, n_qk=n_qk, rope=rope)
    row = lambda bi, i: (bi, 0, 0)
    fix = lambda bi, i: (0, 0)
    kb = key_row0 // tm
    tile = pl.BlockSpec((1, tm, d), lambda bi, i: (bi, i, 0))
    k_spec = pl.BlockSpec((1, tm, d), lambda bi, i: (bi, kb + i, 0))
    vt_spec = pl.BlockSpec((1, N_HEADS, V_ROWS, tm), lambda bi, i: (bi, 0, 0, kb + i))
    q_spec = pl.BlockSpec((1, N_HEADS, 1, V_DIM, 2 * tm), lambda bi, i: (bi, 0, i, 0, 0))
    q_shape = jax.ShapeDtypeStruct((b, N_HEADS, l // tm, V_DIM, 2 * tm), F8)
    k_shape = jax.ShapeDtypeStruct((b, n_keys, d), F8)
    vt_shape = jax.ShapeDtypeStruct((b, N_HEADS, V_ROWS, n_keys), F8)
    in_specs = [tile,
                pl.BlockSpec((1, 1, d), row), pl.BlockSpec((1, 1, d), row),
                pl.BlockSpec((1, d), fix),
                pl.BlockSpec((d, n_qk * d), fix),
                pl.BlockSpec((d, d), fix),
                pl.BlockSpec((4, LANES), fix),
                pl.BlockSpec((tm, LANES), lambda bi, i: (i, 0)),
                pl.BlockSpec((tm, LANES), lambda bi, i: (i, 0))]
    args = [x, shift, scale, gain, w, wv_t, gains, cos, sin]
    aliases = {}
    if kv is not None:
        aliases = {len(args): n_qk - 1, len(args) + 1: n_qk}
        in_specs += [pl.BlockSpec(memory_space=pl.ANY)] * 2
        args += list(kv)
    return pl.pallas_call(
        kern,
        grid=(b, l // tm),
        in_specs=in_specs,
        out_specs=[q_spec] * (n_qk - 1) + [k_spec, vt_spec],
        out_shape=[q_shape] * (n_qk - 1) + [k_shape, vt_shape],
        input_output_aliases=aliases,
        compiler_params=_params("arbitrary", "arbitrary"),
        name="qkv_proj",
    )(*args)


def _rope_tables(n_tokens):
    rows = n_tokens // GRID_W
    row = np.repeat(np.arange(rows, dtype=np.float32), GRID_W)
    col = np.tile(np.arange(GRID_W, dtype=np.float32), rows)
    half = HEAD_DIM // 2
    inv_freq = (np.float32(ROPE_THETA) ** (-np.arange(0, half, 2, dtype=np.float32) / half)).astype(np.float32)
    ang_r = row[:, None] * inv_freq
    ang_c = col[:, None] * inv_freq
    ang = np.concatenate([ang_r, ang_r, ang_c, ang_c] * 2, axis=-1)
    sign = np.where((np.arange(LANES) & 16) == 0, -1.0, 1.0).astype(np.float32)
    return jnp.asarray(np.cos(ang), F32), jnp.asarray(np.sin(ang) * sign, F32)


def _attn_kernel(q_ref, k_ref, vt_ref, lp_ref, sg_ref, o_ref, s_ref, *, tk, group, ahead, lam_init):
    tq = q_ref.shape[4] // 2
    n_chunks = k_ref.shape[1] // tk
    qz = q_ref[0, 0, 0]
    n_slots = (ahead + 1) * group

    def score_chunk(c, m_grp):
        st = jnp.dot(k_ref[0, c * tk:(c + 1) * tk, :], qz, preferred_element_type=F32).astype(BF16)
        slot = c % n_slots
        s_ref[slot * tk:(slot + 1) * tk, :] = st
        mc = jnp.max(st, axis=0, keepdims=True)
        return mc if m_grp is None else jnp.maximum(m_grp, mc)

    def value_chunk(c, m_ref, part):
        slot = c % n_slots
        p = jnp.exp2(s_ref[slot * tk:(slot + 1) * tk, :] - (m_ref - P_SHIFT)).astype(F8)
        pv = jnp.dot(vt_ref[0, 0, :, c * tk:(c + 1) * tk], p, preferred_element_type=F32)
        return pv if part is None else part + pv

    groups = [list(range(g0, min(g0 + group, n_chunks))) for g0 in range(0, n_chunks, group)]
    m_of = {}
    for g in range(min(ahead, len(groups))):
        for c in groups[g]:
            m_of[g] = score_chunk(c, m_of.get(g))
    m = None
    acc = None
    for gi, cur in enumerate(groups):
        nxt = groups[gi + ahead] if gi + ahead < len(groups) else []
        m_new = m_of[gi] if m is None else jnp.maximum(m, m_of[gi])
        part = None
        for i in range(max(len(cur), len(nxt))):
            if i < len(nxt):
                m_of[gi + ahead] = score_chunk(nxt[i], m_of.get(gi + ahead))
            if i < len(cur):
                part = value_chunk(cur[i], m_new, part)
        acc = part if acc is None else acc * jnp.exp2(m.astype(F32) - m_new.astype(F32)) + part
        m = m_new
    acc = acc[:V_DIM] / acc[V_DIM:V_DIM + 1]
    lp = lp_ref[...]
    lam = (jnp.exp(jnp.sum(lp[0:1] * lp[1:2], axis=-1, keepdims=True))
           - jnp.exp(jnp.sum(lp[2:3] * lp[3:4], axis=-1, keepdims=True)) + lam_init)
    o = (acc[:, :tq] - lam * acc[:, tq:]) * lp[4:5, 0:1]
    o = o * lax.rsqrt(jnp.mean(o * o, axis=0, keepdims=True) + NORM_EPS) * sg_ref[...] * (1.0 - lam_init)
    o_ref[0] = o.T.astype(BF16)


def _attn_chunk(n_keys):
    for tk in (768, 512, 256, 128):
        if n_keys % tk == 0:
            return tk
    raise ValueError(f"key count {n_keys} is not a multiple of {LANES}")


def _diff_attention(qz, k_all, vt_all, lam_params, sub_gain, *, lam_init):
    b, _, n_tiles, _, tq2 = qz.shape
    tq = tq2 // 2
    l, d = n_tiles * tq, N_HEADS * V_DIM
    n_keys = k_all.shape[1]
    tk = _attn_chunk(n_keys)
    kern = functools.partial(_attn_kernel, tk=tk, group=ATTN_GROUP_CHUNKS, ahead=ATTN_SCORE_AHEAD, lam_init=lam_init)
    return pl.pallas_call(
        kern,
        grid=(b, N_HEADS, l // tq),
        in_specs=[pl.BlockSpec((1, 1, 1, V_DIM, 2 * tq), lambda bi, h, i: (bi, h, i, 0, 0)),
                  pl.BlockSpec((1, n_keys, V_DIM), lambda bi, h, i: (bi, 0, h)),
                  pl.BlockSpec((1, 1, V_ROWS, n_keys), lambda bi, h, i: (bi, h, 0, 0)),
                  pl.BlockSpec((5, HEAD_DIM), lambda bi, h, i: (0, 0)),
                  pl.BlockSpec((V_DIM, 1), lambda bi, h, i: (0, 0))],
        out_specs=pl.BlockSpec((1, tq, V_DIM), lambda bi, h, i: (bi, i, h)),
        out_shape=jax.ShapeDtypeStruct((b, l, d), BF16),
        scratch_shapes=[pltpu.VMEM(((ATTN_SCORE_AHEAD + 1) * ATTN_GROUP_CHUNKS * tk, 2 * tq), BF16)],
        compiler_params=_params("arbitrary", "arbitrary", "arbitrary"),
        name="diff_attention",
    )(qz, k_all, vt_all, lam_params, sub_gain)


def _route(h2, row0, rwh_ref, rwl_ref, rb_ref, before_ref, carry_ref, ridx_ref, gcol_ref, cnt_ref, is_first):
    tm = h2.shape[0]
    hh = h2.astype(BF16)
    hl = (h2 - hh.astype(F32)).astype(BF16)
    rw2 = jnp.concatenate([rwh_ref[...], rwl_ref[...]], axis=0)
    part = lax.dot_general(rw2, hh, NT_DIMS, preferred_element_type=F32)
    logits = (part[:N_EXPERTS] + part[N_EXPERTS:]
              + lax.dot_general(rwh_ref[...], hl, NT_DIMS, preferred_element_type=F32) + rb_ref[...])
    groups = [logits[g * EXPERTS_PER_GROUP:(g + 1) * EXPERTS_PER_GROUP] for g in range(N_EXPERT_GROUPS)]
    top = groups[0]
    for g in range(1, N_EXPERT_GROUPS):
        top = jnp.maximum(top, groups[g])
    top = jnp.max(top, axis=0, keepdims=True)
    sub = lax.broadcasted_iota(jnp.int32, (EXPERTS_PER_GROUP, tm), 0)
    best = None
    for g in range(N_EXPERT_GROUPS):
        ex = jnp.exp(groups[g] - top)
        v1 = jnp.max(ex, axis=0, keepdims=True)
        i1 = jnp.min(jnp.where(ex == v1, sub, EXPERTS_PER_GROUP), axis=0, keepdims=True)
        rest = jnp.where(sub == i1, -1.0, ex)
        v2 = jnp.max(rest, axis=0, keepdims=True)
        i2 = jnp.min(jnp.where(rest == v2, sub, EXPERTS_PER_GROUP), axis=0, keepdims=True)
        cand = (v1 + v2, v1, v2, i1 + g * EXPERTS_PER_GROUP, i2 + g * EXPERTS_PER_GROUP)
        if best is None:
            best = cand
        else:
            better = cand[0] > best[0]
            best = tuple(jnp.where(better, new, old) for new, old in zip(cand, best))
    _, v1, v2, e0, e1 = best
    gate0 = v1 / (v1 + v2)
    gate1 = v2 / (v1 + v2)

    @pl.when(is_first)
    def _():
        carry_ref[...] = jnp.zeros_like(carry_ref)

    erow = lax.broadcasted_iota(jnp.int32, (N_EXPERTS, tm), 0)
    oh0 = erow == e0
    oh1 = erow == e1
    chosen = jnp.where(oh0 | oh1, 1.0, 0.0)
    prior = carry_ref[:, 0:1] + jnp.dot(chosen.astype(BF16), before_ref[...], preferred_element_type=F32)
    r0 = jnp.sum(jnp.where(oh0, prior, 0.0), axis=0, keepdims=True).astype(jnp.int32)
    r1 = jnp.sum(jnp.where(oh1, prior, 0.0), axis=0, keepdims=True).astype(jnp.int32)
    carry_ref[...] = carry_ref[...] + jnp.sum(chosen, axis=1, keepdims=True)
    cnt_ref[...] = carry_ref[...]
    rid = lax.broadcasted_iota(jnp.int32, (SUBLANES, tm), 0)
    ridx_ref[:, row0:row0 + tm] = jnp.where(rid == 0, e0, jnp.where(rid == 1, e1, jnp.where(rid == 2, r0, jnp.where(rid == 3, r1, 0))))
    gid = lax.broadcasted_iota(jnp.int32, (LANES, tm), 0)
    gcol_ref[row0:row0 + tm, :] = jnp.where(gid == 0, gate0, jnp.where(gid == 1, gate1, 0.0)).T


def _tail(y, row0, x_ref, g1_ref, sh2_ref, s2_ref, n2_ref, rwh_ref, rwl_ref, rb_ref, before_ref,
          x_out, h2_out, ridx_ref, gcol_ref, cnt_ref, carry_ref):
    rows = slice(row0, row0 + y.shape[0])
    x1 = x_ref[0, rows] + g1_ref[0] * y
    x_out[0, rows] = x1
    h2 = _rms_mod(x1, n2_ref[...], sh2_ref[0], s2_ref[0])
    h2_out[0, rows] = _pack_bf16_pairs(h2)
    is_first = (pl.program_id(0) == 0) & (pl.program_id(1) == 0) & (row0 == 0)
    _route(h2, row0, rwh_ref, rwl_ref, rb_ref, before_ref, carry_ref, ridx_ref, gcol_ref, cnt_ref, is_first)


def _attn_tail_kernel(a_ref, wo_ref, *rest):
    starts = range(0, a_ref.shape[1], TAIL_ROWS)
    ys = [jnp.dot(a_ref[0, r0:r0 + TAIL_ROWS], wo_ref[...], preferred_element_type=F32) for r0 in starts]
    for r0, y in zip(starts, ys):
        _tail(y, r0, *rest)


def _pool_tail_kernel(u_ref, up_ref, un_ref, wg_ref, cs_ref, wo_ref, *rest, seq_len):
    *tail_refs, ubuf, abuf = rest
    tm = u_ref.shape[1]
    i = pl.program_id(1)
    u = u_ref[0].astype(F32)
    ubuf[0:POOL_HALO] = jnp.where(i > 0, up_ref[0].astype(F32)[HALO_ROWS - POOL_HALO:], 0.0)
    ubuf[POOL_HALO:POOL_HALO + tm] = u
    ubuf[POOL_HALO + tm:2 * POOL_HALO + tm] = jnp.where(i < pl.num_programs(1) - 1,
                                                        un_ref[0].astype(F32)[:POOL_HALO], 0.0)
    pos = i * tm + lax.broadcasted_iota(jnp.int32, (tm, 1), 0)
    gd = wg_ref.shape[1]
    assert all(win == 2 ** (g + 1) for g, win in enumerate(POOL_WINDOWS))
    n_ext = tm + 2 * POOL_HALO
    bufs = (ubuf, abuf)
    abuf[1:n_ext, :] = ubuf[0:n_ext - 1, :] + ubuf[1:n_ext, :]
    lo, hi = 1, n_ext
    for g in range(1, len(POOL_WINDOWS)):
        src, dst = bufs[g % 2], bufs[(g + 1) % 2]
        sh = POOL_WINDOWS[g] // 4
        dst[lo + sh:hi - sh, g * gd:] = src[lo:hi - 2 * sh, g * gd:] + src[lo + 2 * sh:hi, g * gd:]
        lo, hi = lo + sh, hi - sh
    outs = []
    for g, win in enumerate(POOL_WINDOWS):
        half = win // 2
        cols = slice(g * gd, (g + 1) * gd)
        s = bufs[(g + 1) % 2][POOL_HALO:POOL_HALO + tm, cols]
        inv_cnt = 1.0 / (jnp.minimum(pos + half, seq_len) - jnp.maximum(pos - half, 0)).astype(F32)
        dlt = (s * inv_cnt - u[:, cols]).astype(BF16)
        outs.append(jnp.dot(dlt, wg_ref[g], preferred_element_type=F32))
    z = (jnp.concatenate(outs, axis=-1) * cs_ref[...]).astype(BF16)
    starts = range(0, tm, TAIL_ROWS)
    ys = [jnp.dot(z[r0:r0 + TAIL_ROWS], wo_ref[...], preferred_element_type=F32) for r0 in starts]
    for r0, y in zip(starts, ys):
        _tail(y, r0, *tail_refs)


def _mixer_tail(front_args, front_specs, kern, x, g1, sh2, s2, n2g, rwh, rwl, rb, *, tm, scratch=()):
    b, l, d = x.shape
    nt = l // tm
    n = b * l
    row = lambda bi, i: (bi, 0, 0)
    fix = lambda bi, i: (0, 0)
    tile = lambda bi, i: (bi, i, 0)
    in_specs = list(front_specs) + [
        pl.BlockSpec((1, tm, d), tile),
        pl.BlockSpec((1, 1, d), row), pl.BlockSpec((1, 1, d), row), pl.BlockSpec((1, 1, d), row),
        pl.BlockSpec((1, d), fix),
        pl.BlockSpec((N_EXPERTS, d), fix), pl.BlockSpec((N_EXPERTS, d), fix),
        pl.BlockSpec((N_EXPERTS, 1), fix),
        pl.BlockSpec((TAIL_ROWS, TAIL_ROWS), fix)]
    before = jnp.asarray(np.triu(np.ones((TAIL_ROWS, TAIL_ROWS), np.float32), k=1), BF16)
    out_specs = [pl.BlockSpec((1, tm, d), tile), pl.BlockSpec((1, tm, d // 2), tile),
                 pl.BlockSpec((SUBLANES, tm), lambda bi, i: (0, bi * nt + i)),
                 pl.BlockSpec((tm, LANES), lambda bi, i: (bi * nt + i, 0)),
                 pl.BlockSpec((N_EXPERTS, LANES), fix)]
    out_shape = [jax.ShapeDtypeStruct((b, l, d), F32), jax.ShapeDtypeStruct((b, l, d // 2), jnp.int32),
                 jax.ShapeDtypeStruct((SUBLANES, n), jnp.int32), jax.ShapeDtypeStruct((n, LANES), F32),
                 jax.ShapeDtypeStruct((N_EXPERTS, LANES), F32)]
    return pl.pallas_call(
        kern,
        grid=(b, nt),
        in_specs=in_specs, out_specs=out_specs, out_shape=out_shape,
        scratch_shapes=[pltpu.VMEM((N_EXPERTS, LANES), F32)] + list(scratch),
        compiler_params=_params("arbitrary", "arbitrary"),
        name="mixer_tail",
    )(*front_args, x, g1, sh2, s2, n2g, rwh, rwl, rb, before)


def _slot_kernel(ps_ref, ridx_ref, dest_ref):
    ridx = ridx_ref[...]
    ps = ps_ref[...]
    erow = lax.broadcasted_iota(jnp.int32, (N_EXPERTS, ridx.shape[1]), 0)
    rows = []
    for k in range(TOP_K):
        start = jnp.sum(jnp.where(erow == ridx[k:k + 1], ps, 0), axis=0, keepdims=True)
        rows.append(start + ridx[TOP_K + k:TOP_K + k + 1])
    rid = lax.broadcasted_iota(jnp.int32, ridx.shape, 0)
    dest_ref[...] = jnp.where(rid == 0, rows[0], jnp.where(rid == 1, rows[1], 0))


def _slot_index(pad_start, ridx, *, tn):
    n = ridx.shape[1]
    return pl.pallas_call(
        _slot_kernel,
        grid=(n // tn,),
        in_specs=[pl.BlockSpec((N_EXPERTS, 1), lambda i: (0, 0)),
                  pl.BlockSpec((SUBLANES, tn), lambda i: (0, i))],
        out_specs=pl.BlockSpec((SUBLANES, tn), lambda i: (0, i)),
        out_shape=jax.ShapeDtypeStruct((SUBLANES, n), jnp.int32),
        compiler_params=_params("arbitrary"),
        name="slot_index",
    )(pad_start.reshape(N_EXPERTS, 1), ridx)


def _sc_mesh():
    return plsc.VectorSubcoreMesh(core_axis_name="c", subcore_axis_name="s",
                                  num_cores=SC_CORES, num_subcores=SC_SUBCORES)


def _sc_worker_base(per_worker):
    return (lax.axis_index("s") * SC_CORES + lax.axis_index("c")) * per_worker


def _sc_scatter_rows(rows, idx0, idx1, n_slots):
    n, d = rows.shape
    per_worker = n // SC_WORKERS
    assert per_worker % SC_WINDOW == 0

    n_win = per_worker // SC_WINDOW
    assert n_win % 2 == 0

    def body(rows_hbm, i0_hbm, i1_hbm, out_hbm, i0_a, i1_a, rows_a, i0_b, i1_b, rows_b, sem_a, sem_b):
        base = _sc_worker_base(per_worker)

        def offset(j):
            return pl.multiple_of(base + j * SC_WINDOW, SC_WINDOW)

        def start(j, i0_v, i1_v, rows_v, sem):
            pltpu.sync_copy(i0_hbm.at[pl.ds(offset(j), SC_WINDOW)], i0_v)
            pltpu.sync_copy(i1_hbm.at[pl.ds(offset(j), SC_WINDOW)], i1_v)
            pltpu.async_copy(rows_hbm.at[pl.ds(offset(j), SC_WINDOW)], rows_v, sem)

        def finish(j, i0_v, i1_v, rows_v, sem):
            pltpu.make_async_copy(rows_hbm.at[pl.ds(offset(j), SC_WINDOW)], rows_v, sem).wait()
            pltpu.sync_copy(rows_v, out_hbm.at[i0_v])
            pltpu.sync_copy(rows_v, out_hbm.at[i1_v])

        start(0, i0_a, i1_a, rows_a, sem_a)

        @pl.loop(0, n_win, step=2)
        def _(j):
            start(j + 1, i0_b, i1_b, rows_b, sem_b)
            finish(j, i0_a, i1_a, rows_a, sem_a)

            @pl.when(j + 2 < n_win)
            def _():
                start(j + 2, i0_a, i1_a, rows_a, sem_a)

            finish(j + 1, i0_b, i1_b, rows_b, sem_b)

    window = [pltpu.VMEM((SC_WINDOW,), jnp.int32), pltpu.VMEM((SC_WINDOW,), jnp.int32),
              pltpu.VMEM((SC_WINDOW, d), rows.dtype)]
    return pl.kernel(
        body, out_type=jax.ShapeDtypeStruct((n_slots, d), rows.dtype), mesh=_sc_mesh(),
        scratch_types=window + window + [pltpu.SemaphoreType.DMA, pltpu.SemaphoreType.DMA],
        name="sc_scatter_rows",
    )(rows, idx0, idx1)


def _sc_gather_rows(table, idx):
    n = idx.shape[0]
    d = table.shape[1]
    per_worker = n // SC_WORKERS
    assert per_worker % SC_WINDOW == 0

    n_win = per_worker // SC_WINDOW
    assert n_win % 2 == 0

    def body(table_hbm, idx_hbm, out_hbm, idx_a, idx_b, rows_a, rows_b, sem_a, sem_b):
        base = _sc_worker_base(per_worker)

        def offset(j):
            return pl.multiple_of(base + j * SC_WINDOW, SC_WINDOW)

        def start(j, idx_v, rows_v, sem):
            pltpu.sync_copy(idx_hbm.at[pl.ds(offset(j), SC_WINDOW)], idx_v)
            pltpu.async_copy(table_hbm.at[idx_v], rows_v, sem)

        def finish(j, idx_v, rows_v, sem):
            pltpu.make_async_copy(table_hbm.at[idx_v], rows_v, sem).wait()
            pltpu.sync_copy(rows_v, out_hbm.at[pl.ds(offset(j), SC_WINDOW)])

        start(0, idx_a, rows_a, sem_a)

        @pl.loop(0, n_win, step=2)
        def _(j):
            start(j + 1, idx_b, rows_b, sem_b)
            finish(j, idx_a, rows_a, sem_a)

            @pl.when(j + 2 < n_win)
            def _():
                start(j + 2, idx_a, rows_a, sem_a)

            finish(j + 1, idx_b, rows_b, sem_b)

    return pl.kernel(
        body, out_type=jax.ShapeDtypeStruct((n, d), table.dtype), mesh=_sc_mesh(),
        scratch_types=[pltpu.VMEM((SC_WINDOW,), jnp.int32), pltpu.VMEM((SC_WINDOW,), jnp.int32),
                       pltpu.VMEM((SC_WINDOW, d), table.dtype), pltpu.VMEM((SC_WINDOW, d), table.dtype),
                       pltpu.SemaphoreType.DMA, pltpu.SemaphoreType.DMA],
        name="sc_gather_rows",
    )(table, idx)


def _ffn_kernel(be_ref, nv_ref, xs_ref, wg_ref, wu_ref, wd_ref, ys_ref, wg_b, wu_b, wd_b):
    j = pl.program_id(0)
    valid = nv_ref[j]

    @pl.when((valid > 0) & ((j == 0) | (be_ref[j] != be_ref[jnp.maximum(j - 1, 0)])))
    def _():
        wg_b[...] = wg_ref[0, 0].astype(BF16)
        wu_b[...] = wu_ref[0, 0].astype(BF16)
        wd_b[...] = wd_ref[0, 0].astype(BF16)

    @pl.when(valid > 0)
    def _():
        row = lax.broadcasted_iota(jnp.int32, (SLOT_ROWS, 1), 0)
        xw = jnp.where(row < valid, xs_ref[...], 0)
        xb = _unpack_bf16_pairs(xw).astype(BF16)
        g = jnp.dot(xb, wg_b[...], preferred_element_type=F32)
        u = jnp.dot(xb, wu_b[...], preferred_element_type=F32)
        a = (g / (1.0 + jnp.exp(-g)) * u).astype(BF16)
        ys_ref[...] = _pack_bf16_pairs(jnp.dot(a, wd_b[...], preferred_element_type=F32))

    @pl.when(valid <= 0)
    def _():
        ys_ref[...] = jnp.zeros_like(ys_ref)


def _expert_ffn(block_e, n_valid, xs, w_gate, w_up, w_down, layer):
    n_slots, dw = xs.shape
    d, de = w_gate.shape[2:]
    w_idx = lambda j, be, nv: (layer, be[j], 0, 0)
    return pl.pallas_call(
        _ffn_kernel,
        grid_spec=pltpu.PrefetchScalarGridSpec(
            num_scalar_prefetch=2,
            grid=(n_slots // SLOT_ROWS,),
            in_specs=[pl.BlockSpec((SLOT_ROWS, dw), lambda j, be, nv: (j, 0)),
                      pl.BlockSpec((1, 1, d, de), w_idx),
                      pl.BlockSpec((1, 1, d, de), w_idx),
                      pl.BlockSpec((1, 1, de, d), w_idx)],
            out_specs=pl.BlockSpec((SLOT_ROWS, dw), lambda j, be, nv: (j, 0)),
            scratch_shapes=[pltpu.VMEM((d, de), BF16), pltpu.VMEM((d, de), BF16), pltpu.VMEM((de, d), BF16)]),
        out_shape=jax.ShapeDtypeStruct((n_slots, dw), jnp.int32),
        compiler_params=_params("arbitrary"),
        name="expert_ffn",
    )(block_e, n_valid, xs, w_gate, w_up, w_down)


def _combine_kernel(y0_ref, y1_ref, gcol_ref, x_ref, g2_ref, *rest, pool_in):
    if pool_in:
        sh_ref, sc_ref, n1_ref, wi_ref, x_out, u_out = rest
    else:
        (x_out,) = rest
    gc = gcol_ref[...]
    out = gc[:, 0:1] * _unpack_bf16_pairs(y0_ref[0]) + gc[:, 1:2] * _unpack_bf16_pairs(y1_ref[0])
    x2 = x_ref[0] + g2_ref[0] * out
    x_out[0] = x2
    if pool_in:
        hb = _rms_mod(x2, n1_ref[...], sh_ref[0], sc_ref[0]).astype(BF16)
        u_out[0] = jnp.dot(hb, wi_ref[...], preferred_element_type=F32).astype(BF16)


def _combine(yg, gcol, x, g2, pool_args=None, *, tc):
    b, l, d = x.shape
    nt = l // tc
    pool_in = pool_args is not None
    kern = functools.partial(_combine_kernel, pool_in=pool_in)
    row = lambda bi, i: (bi, 0, 0)
    fix = lambda bi, i: (0, 0)
    tile = lambda bi, i: (bi, i, 0)
    in_specs = [pl.BlockSpec((1, tc, d // 2), lambda bi, i: (0, bi * nt + i, 0)),
                pl.BlockSpec((1, tc, d // 2), lambda bi, i: (1, bi * nt + i, 0)),
                pl.BlockSpec((tc, LANES), lambda bi, i: (bi * nt + i, 0)),
                pl.BlockSpec((1, tc, d), tile),
                pl.BlockSpec((1, 1, d), row)]
    out_specs = [pl.BlockSpec((1, tc, d), tile)]
    out_shape = [jax.ShapeDtypeStruct((b, l, d), F32)]
    args = [yg, yg, gcol, x, g2]
    if pool_in:
        in_specs += [pl.BlockSpec((1, 1, d), row), pl.BlockSpec((1, 1, d), row),
                     pl.BlockSpec((1, d), fix), pl.BlockSpec((d, d), fix)]
        out_specs.append(pl.BlockSpec((1, tc, d), tile))
        out_shape.append(jax.ShapeDtypeStruct((b, l, d), BF16))
        args += list(pool_args)
    return pl.pallas_call(
        kern,
        grid=(b, nt),
        in_specs=in_specs, out_specs=out_specs, out_shape=out_shape,
        compiler_params=_params("arbitrary", "arbitrary"),
        name="moe_combine",
    )(*args)


def _moe(h2, ridx, gcol, counts, x1, g2, w_gate, w_up, w_down, layer, pool_args=None):
    b, l, d = x1.shape
    n = b * l
    n_blocks = (n * TOP_K) // SLOT_ROWS + N_EXPERTS
    cnt = counts[:, 0].astype(jnp.int32)
    padded = (cnt + SLOT_ROWS - 1) // SLOT_ROWS * SLOT_ROWS
    earlier = jnp.arange(N_EXPERTS)[None, :] < jnp.arange(N_EXPERTS)[:, None]
    pad_start = jnp.sum(jnp.where(earlier, padded[None, :], 0), axis=1).astype(jnp.int32)
    pad_end = pad_start + padded
    block_start = jnp.arange(n_blocks, dtype=jnp.int32) * SLOT_ROWS
    block_e = jnp.minimum(jnp.sum(pad_end[None, :] <= block_start[:, None], axis=1), N_EXPERTS - 1).astype(jnp.int32)
    own = block_e[:, None] == jnp.arange(N_EXPERTS)[None, :]
    data_end = jnp.sum(jnp.where(own, (pad_start + cnt)[None, :], 0), axis=1)
    n_valid = jnp.clip(data_end - block_start, 0, SLOT_ROWS).astype(jnp.int32)
    dest = _slot_index(pad_start, ridx, tn=2048)
    xs = _sc_scatter_rows(h2.reshape(n, d // 2), dest[0], dest[1], n_blocks * SLOT_ROWS)
    ys = _expert_ffn(block_e, n_valid, xs, w_gate, w_up, w_down, layer)
    yg = _sc_gather_rows(ys, dest[:TOP_K].reshape(TOP_K * n)).reshape(TOP_K, n, d // 2)
    return _combine(yg, gcol, x1, g2, pool_args, tc=512)


def kernel(x, c, ctx, c_ctx, ada_w, ada_b, norm1_g, norm2_g, attn_w_in, attn_w_out, attn_q_gain, attn_k_gain,
           attn_lq1, attn_lk1, attn_lq2, attn_lk2, attn_sub_gain, pool_w_in, pool_w_group, pool_scale, pool_w_out,
           router_w, router_b, moe_w_gate, moe_w_up, moe_w_down):
    b, l, d = x.shape
    n_ctx = ctx.shape[1]
    depth = ada_w.shape[0]
    assert depth == 2 and d == N_HEADS * V_DIM
    tm = 512

    mod = _adaln_mod(c, c_ctx, ada_w, ada_b)
    mods = [[mod[i, :b, None, j * d:(j + 1) * d] for j in range(N_MOD)] for i in range(depth)]
    mod_ctx = [jnp.broadcast_to(mod[0, b, j * d:(j + 1) * d], (b, 1, d)) for j in range(2)]

    rwt = router_w.T
    rwh = rwt.astype(BF16)
    rwl = (rwt - rwh.astype(F32)).astype(BF16)
    rb = router_b.reshape(N_EXPERTS, 1)

    sh1, s1, g1, sh2, s2, g2 = mods[0]
    cos, sin = _rope_tables(l)
    pair_up = (jnp.arange(LANES) & 16) == 0

    def gain_rows(g, factor):
        g2 = jnp.concatenate([g, g]) * factor
        return [g2, jnp.where(pair_up, jnp.roll(g2, -16), jnp.roll(g2, 16))]

    q_scale = HEAD_DIM ** -0.5 * math.log2(math.e)
    q_max = jnp.maximum(jnp.max(jnp.abs(attn_q_gain[0])) * (HEAD_DIM ** 0.5 * q_scale), F32_TINY)
    k_max = jnp.maximum(jnp.max(jnp.abs(attn_k_gain[0])) * HEAD_DIM ** 0.5, F32_TINY)
    need = jnp.ceil(jnp.log2(k_max / F8_MAX))
    room = jnp.floor(jnp.log2(F8_MAX / q_max))
    trade = jnp.exp2(jnp.clip(jnp.clip(0.0, need, jnp.maximum(need, room)), -60.0, 60.0))
    h_max = jnp.float32(0.0)
    for shift_, scale_ in ((sh1, s1), (mod_ctx[0], mod_ctx[1])):
        h_max = jnp.maximum(h_max, jnp.max(d ** 0.5 * jnp.max(jnp.abs(norm1_g[0] * (1.0 + scale_)), axis=-1)
                                           + jnp.sqrt(jnp.sum(shift_ * shift_, axis=-1))))
    w_v = attn_w_in[0][:, 2 * d:]
    v_max = 1.02 * h_max * jnp.sqrt(jnp.max(jnp.sum(w_v * w_v, axis=0)))
    v_grow = jnp.exp2(jnp.clip(jnp.ceil(jnp.log2(jnp.maximum(v_max, F32_TINY) / F8_MAX)), 0.0, 60.0))

    gains = jnp.stack(gain_rows(attn_q_gain[0], q_scale * trade) + gain_rows(attn_k_gain[0], 1.0 / trade))
    w_qk = attn_w_in[0][:, :2 * d].astype(BF16)
    wv_t = (w_v / v_grow).T.astype(BF16)
    assert l % tm == 0 and l % n_ctx == 0
    q, k_all, vt_all = _qkv_proj(x, sh1, s1, norm1_g[0][None], w_qk, wv_t, gains, cos, sin, n_qk=2, rope=True,
                                 tm=tm, n_keys=l + n_ctx, key_row0=0)
    k_all, vt_all = _qkv_proj(ctx, mod_ctx[0], mod_ctx[1], norm1_g[0][None], w_qk[:, d:], wv_t, gains,
                              cos[:n_ctx], sin[:n_ctx], n_qk=1, rope=False,
                              tm=n_ctx, n_keys=l + n_ctx, key_row0=l, kv=(k_all, vt_all))
    lam_init = 0.8 - 0.6 * math.exp(-0.3 * 0)
    lam_params = jnp.stack([attn_lq1[0], attn_lk1[0], attn_lq2[0], attn_lk2[0],
                            jnp.full((HEAD_DIM,), v_grow, F32)])
    o = _diff_attention(q, k_all, vt_all, lam_params, attn_sub_gain[0][:, None], lam_init=lam_init)

    fix = lambda bi, i: (0, 0)
    x1, h2, ridx, gcol, counts = _mixer_tail(
        (o, attn_w_out[0].astype(BF16)),
        (pl.BlockSpec((1, 2 * TAIL_ROWS, d), lambda bi, i: (bi, i, 0)), pl.BlockSpec((d, d), fix)),
        _attn_tail_kernel, x, g1, sh2, s2, norm2_g[0][None], rwh, rwl, rb, tm=2 * TAIL_ROWS)

    sh1b, s1b, g1b, sh2b, s2b, g2b = mods[1]
    x2, u = _moe(h2, ridx, gcol, counts, x1, g2, moe_w_gate, moe_w_up, moe_w_down, 0,
                 pool_args=(sh1b, s1b, norm1_g[1][None], pool_w_in[0].astype(BF16)))
    gd = pool_w_group.shape[2]
    tp = 2 * TAIL_ROWS
    nh = tp // HALO_ROWS
    front_specs = (
        pl.BlockSpec((1, tp, d), lambda bi, i: (bi, i, 0)),
        pl.BlockSpec((1, HALO_ROWS, d), lambda bi, i: (bi, jnp.maximum(i * nh - 1, 0), 0)),
        pl.BlockSpec((1, HALO_ROWS, d), lambda bi, i: (bi, jnp.minimum((i + 1) * nh, l // HALO_ROWS - 1), 0)),
        pl.BlockSpec((len(POOL_WINDOWS), gd, gd), lambda bi, i: (0, 0, 0)),
        pl.BlockSpec((1, d), fix),
        pl.BlockSpec((d, d), fix))
    x3, h2b, ridx_b, gcol_b, counts_b = _mixer_tail(
        (u, u, u, pool_w_group[0].astype(BF16), pool_scale[0][None], pool_w_out[0].astype(BF16)),
        front_specs, functools.partial(_pool_tail_kernel, seq_len=l),
        x2, g1b, sh2b, s2b, norm2_g[1][None], rwh, rwl, rb, tm=tp,
        scratch=[pltpu.VMEM((tp + 2 * POOL_HALO, d), F32)] * 2)
    (out,) = _moe(h2b, ridx_b, gcol_b, counts_b, x3, g2b, moe_w_gate, moe_w_up, moe_w_down, 1)
    return out
```

```python
import functools
import math

import jax
import jax.numpy as jnp
import numpy as np
from jax import lax
from jax.experimental import pallas as pl
from jax.experimental.pallas import tpu as pltpu
from jax.experimental.pallas import tpu_sc as plsc

F32 = jnp.float32
BF16 = jnp.bfloat16
F8 = jnp.float8_e4m3fn
F8_MAX = float(jnp.finfo(F8).max)
F32_TINY = float(jnp.finfo(F32).tiny)

LANES = 128
SUBLANES = 8
N_HEADS = 8
HEAD_DIM = 64
V_DIM = 2 * HEAD_DIM
V_ROWS = V_DIM + 32
P_SHIFT = 8.0
GRID_W = 64
ROPE_THETA = 10000.0
NORM_EPS = 1e-6
N_MOD = 6
POOL_WINDOWS = (2, 4, 8, 16)
POOL_HALO = max(POOL_WINDOWS) // 2
HALO_ROWS = 16
N_EXPERTS = 32
N_EXPERT_GROUPS = 4
EXPERTS_PER_GROUP = N_EXPERTS // N_EXPERT_GROUPS
TOP_K = 2
SLOT_ROWS = 512
TAIL_ROWS = 512
SC_CORES = 2
SC_SUBCORES = 16
SC_WORKERS = SC_CORES * SC_SUBCORES
SC_WINDOW = 64
ATTN_TILES = 2
ATTN_GROUP_CHUNKS = 1
ATTN_SCORE_AHEAD = 1
VMEM_LIMIT = 48 * 1024 * 1024
NT_DIMS = (((1,), (1,)), ((), ()))


def _params(*sem):
    return pltpu.CompilerParams(dimension_semantics=sem, vmem_limit_bytes=VMEM_LIMIT)


def _rms_mod(x, gain, shift, scale):
    inv_rms = lax.rsqrt(jnp.mean(x * x, axis=-1, keepdims=True) + NORM_EPS)
    return x * inv_rms * (gain * (1.0 + scale)) + shift


def _pack_bf16_pairs(x):
    c = x.shape[1] // 2
    hi = lax.bitcast_convert_type(x[:, :c].astype(BF16).astype(F32), jnp.uint32)
    lo = lax.bitcast_convert_type(x[:, c:].astype(BF16).astype(F32), jnp.uint32)
    return lax.bitcast_convert_type(hi | (lo >> 16), jnp.int32)


def _unpack_bf16_pairs(w):
    u = lax.bitcast_convert_type(w, jnp.uint32)
    hi = lax.bitcast_convert_type(u & jnp.uint32(0xFFFF0000), F32)
    lo = lax.bitcast_convert_type(u << 16, F32)
    return jnp.concatenate([hi, lo], axis=1)


def _mod_kernel(c_ref, w_ref, b_ref, o_ref):
    c = c_ref[...]
    a = c / (1.0 + jnp.exp(-c))
    o_ref[0] = jnp.dot(a, w_ref[0], precision=lax.Precision.HIGHEST,
                       preferred_element_type=F32) + b_ref[0]


def _adaln_mod(c, c_ctx, ada_w, ada_b):
    depth, d, n_out = ada_w.shape
    b = c.shape[0]
    assert b + 1 <= SUBLANES
    rows = jnp.concatenate([c, c_ctx[None], jnp.zeros((SUBLANES - b - 1, d), F32)], axis=0)
    tn = n_out // 4
    return pl.pallas_call(
        _mod_kernel,
        grid=(depth, n_out // tn),
        in_specs=[pl.BlockSpec((SUBLANES, d), lambda i, j: (0, 0)),
                  pl.BlockSpec((1, d, tn), lambda i, j: (i, 0, j)),
                  pl.BlockSpec((1, 1, tn), lambda i, j: (i, 0, j))],
        out_specs=pl.BlockSpec((1, SUBLANES, tn), lambda i, j: (i, 0, j)),
        out_shape=jax.ShapeDtypeStruct((depth, SUBLANES, n_out), F32),
        compiler_params=_params("arbitrary", "arbitrary"),
        name="adaln_mod",
    )(rows, ada_w, ada_b.reshape(depth, 1, n_out))


def _qkv_kernel(x_ref, sh_ref, sc_ref, g_ref, w_ref, wvt_ref, gains_ref, cos_ref, sin_ref, *refs, n_qk, rope):
    out_refs = refs[-(n_qk + 1):]
    tm, d = x_ref.shape[1:]
    hb = _rms_mod(x_ref[0], g_ref[...], sh_ref[0], sc_ref[0]).astype(BF16)
    lane_b4 = (lax.broadcasted_iota(jnp.int32, (1, LANES), 1) & 16) == 0
    chunk_r = lax.broadcasted_iota(jnp.int32, (2 * LANES, 2 * LANES), 0) // HEAD_DIM
    chunk_c = lax.broadcasted_iota(jnp.int32, (2 * LANES, 2 * LANES), 1) // HEAD_DIM
    same_chunk = (chunk_r == chunk_c).astype(BF16)
    gains = gains_ref[...]
    tables = []
    for t in range(n_qk):
        r = 2 * (t + 2 - n_qk)
        tables.append((cos_ref[...] * gains[r:r + 1], sin_ref[...] * gains[r + 1:r + 2]) if rope
                      else (gains[r:r + 1], None))

    def project(t, j):
        return jnp.dot(hb, w_ref[:, t * d + j:t * d + j + 2 * LANES], preferred_element_type=F32)

    def chunk_sums(acc):
        return jnp.dot((acc * acc).astype(BF16), same_chunk, preferred_element_type=F32)

    def finish(t, j, acc, ssq):
        cos_t, sin_t = tables[t]
        nrm = acc * lax.rsqrt(ssq * (1.0 / HEAD_DIM) + NORM_EPS)
        for half in range(2):
            blk = nrm[:, half * LANES:(half + 1) * LANES]
            if rope:
                rot = jnp.where(lane_b4, pltpu.roll(blk, LANES - 16, 1), pltpu.roll(blk, 16, 1))
                y = blk * cos_t + rot * sin_t
            else:
                y = blk * cos_t
            c0 = j + half * LANES
            if t < n_qk - 1:
                yt = y.T
                first = lax.broadcasted_iota(jnp.int32, (V_DIM, 1), 0) < HEAD_DIM
                out_refs[t][0, c0 // LANES, 0, :, 0:tm] = jnp.where(first, yt, 0.0).astype(F8)
                out_refs[t][0, c0 // LANES, 0, :, tm:2 * tm] = jnp.where(first, 0.0, yt).astype(F8)
            else:
                out_refs[t][0, :, c0:c0 + LANES] = y.astype(F8)

    blocks = [(t, j) for t in range(n_qk) for j in range(0, d, 2 * LANES)]
    accs, sums = {}, {}
    for i in range(-2, len(blocks)):
        if i + 2 < len(blocks):
            accs[i + 2] = project(*blocks[i + 2])
        if 0 <= i + 1 < len(blocks):
            sums[i + 1] = chunk_sums(accs[i + 1])
        if i >= 0:
            finish(*blocks[i], accs.pop(i), sums.pop(i))
    vt_ref = out_refs[n_qk]
    ones = jnp.where(lax.broadcasted_iota(jnp.int32, (V_ROWS - V_DIM, tm), 0) == 0, 1.0, 0.0).astype(F8)
    for j in range(0, d, 2 * LANES):
        acc_t = lax.dot_general(wvt_ref[j:j + 2 * LANES, :], hb, NT_DIMS, preferred_element_type=F32)
        for half in range(2):
            h = j // LANES + half
            vt_ref[0, h, 0:V_DIM, :] = acc_t[half * V_DIM:(half + 1) * V_DIM].astype(F8)
            vt_ref[0, h, V_DIM:V_ROWS, :] = ones


def _qkv_proj(x, shift, scale, gain, w, wv_t, gains, cos, sin, *, n_qk, rope, tm, n_keys, key_row0, kv=None):
    b, l, d = x.shape
    n_out = n_qk + 1
    kern = functools.partial(_qkv_kernel, n_qk=n_qk, rope=rope)
    row = lambda bi, i: (bi, 0, 0)
    fix = lambda bi, i: (0, 0)
    kb = key_row0 // tm
    tile = pl.BlockSpec((1, tm, d), lambda bi, i: (bi, i, 0))
    k_spec = pl.BlockSpec((1, tm, d), lambda bi, i: (bi, kb + i, 0))
    vt_spec = pl.BlockSpec((1, N_HEADS, V_ROWS, tm), lambda bi, i: (bi, 0, 0, kb + i))
    q_spec = pl.BlockSpec((1, N_HEADS, 1, V_DIM, 2 * tm), lambda bi, i: (bi, 0, i, 0, 0))
    q_shape = jax.ShapeDtypeStruct((b, N_HEADS, l // tm, V_DIM, 2 * tm), F8)
    k_shape = jax.ShapeDtypeStruct((b, n_keys, d), F8)
    vt_shape = jax.ShapeDtypeStruct((b, N_HEADS, V_ROWS, n_keys), F8)
    in_specs = [tile,
                pl.BlockSpec((1, 1, d), row), pl.BlockSpec((1, 1, d), row),
                pl.BlockSpec((1, d), fix),
                pl.BlockSpec((d, n_qk * d), fix),
                pl.BlockSpec((d, d), fix),
                pl.BlockSpec((4, LANES), fix),
                pl.BlockSpec((tm, LANES), lambda bi, i: (i, 0)),
                pl.BlockSpec((tm, LANES), lambda bi, i: (i, 0))]
    args = [x, shift, scale, gain, w, wv_t, gains, cos, sin]
    aliases = {}
    if kv is not None:
        aliases = {len(args): n_qk - 1, len(args) + 1: n_qk}
        in_specs += [pl.BlockSpec(memory_space=pl.ANY)] * 2
        args += list(kv)
    return pl.pallas_call(
        kern,
        grid=(b, l // tm),
        in_specs=in_specs,
        out_specs=[q_spec] * (n_qk - 1) + [k_spec, vt_spec],
        out_shape=[q_shape] * (n_qk - 1) + [k_shape, vt_shape],
        input_output_aliases=aliases,
        compiler_params=_params("arbitrary", "arbitrary"),
        name="qkv_proj",
    )(*args)


def _rope_tables(n_tokens):
    rows = n_tokens // GRID_W
    row = np.repeat(np.arange(rows, dtype=np.float32), GRID_W)
    col = np.tile(np.arange(GRID_W, dtype=np.float32), rows)
    half = HEAD_DIM // 2
    inv_freq = (np.float32(ROPE_THETA) ** (-np.arange(0, half, 2, dtype=np.float32) / half)).astype(np.float32)
    ang_r = row[:, None] * inv_freq
    ang_c = col[:, None] * inv_freq
    ang = np.concatenate([ang_r, ang_r, ang_c, ang_c] * 2, axis=-1)
    sign = np.where((np.arange(LANES) & 16) == 0, -1.0, 1.0).astype(np.float32)
    return jnp.asarray(np.cos(ang), F32), jnp.asarray(np.sin(ang) * sign, F32)


def _attn_kernel(q_ref, k_ref, vt_ref, lp_ref, sg_ref, o_ref, s_ref, *, tk, group, ahead, lam_init):
    del group, ahead
    n_tiles = q_ref.shape[2]
    tq = q_ref.shape[4] // 2
    n_chunks = k_ref.shape[1] // tk
    lp = lp_ref[...]
    lam = (jnp.exp(jnp.sum(lp[0:1] * lp[1:2], axis=-1, keepdims=True))
           - jnp.exp(jnp.sum(lp[2:3] * lp[3:4], axis=-1, keepdims=True)) + lam_init)
    items = [(t, c) for t in range(n_tiles) for c in range(n_chunks)]

    def score_item(idx):
        t, c = items[idx]
        st = jnp.dot(k_ref[0, c * tk:(c + 1) * tk, :], q_ref[0, 0, t], preferred_element_type=F32).astype(BF16)
        slot = idx % 2
        s_ref[slot * tk:(slot + 1) * tk, :] = st
        return jnp.max(st, axis=0, keepdims=True)

    def value_item(idx, m_ref):
        _, c = items[idx]
        slot = idx % 2
        p = jnp.exp2(s_ref[slot * tk:(slot + 1) * tk, :] - (m_ref - P_SHIFT)).astype(F8)
        return jnp.dot(vt_ref[0, 0, :, c * tk:(c + 1) * tk], p, preferred_element_type=F32)

    def finish_tile(t, acc):
        acc = acc[:V_DIM] / acc[V_DIM:V_DIM + 1]
        o = (acc[:, :tq] - lam * acc[:, tq:]) * lp[4:5, 0:1]
        o = o * lax.rsqrt(jnp.mean(o * o, axis=0, keepdims=True) + NORM_EPS) * sg_ref[...] * (1.0 - lam_init)
        o_ref[0, t * tq:(t + 1) * tq, :] = o.T.astype(BF16)

    m_next = score_item(0)
    m = acc = None
    for idx, (t, c) in enumerate(items):
        m_item = m_next
        if idx + 1 < len(items):
            m_next = score_item(idx + 1)
        m_new = m_item if c == 0 else jnp.maximum(m, m_item)
        pv = value_item(idx, m_new)
        acc = pv if c == 0 else acc * jnp.exp2(m.astype(F32) - m_new.astype(F32)) + pv
        m = m_new
        if c == n_chunks - 1:
            finish_tile(t, acc)


def _attn_chunk(n_keys):
    for tk in (768, 512, 256, 128):
        if n_keys % tk == 0:
            return tk
    raise ValueError(f"key count {n_keys} is not a multiple of {LANES}")


def _diff_attention(qz, k_all, vt_all, lam_params, sub_gain, *, lam_init):
    b, _, n_tiles, _, tq2 = qz.shape
    tq = tq2 // 2
    l, d = n_tiles * tq, N_HEADS * V_DIM
    n_keys = k_all.shape[1]
    tk = _attn_chunk(n_keys)
    kern = functools.partial(_attn_kernel, tk=tk, group=ATTN_GROUP_CHUNKS, ahead=ATTN_SCORE_AHEAD, lam_init=lam_init)
    return pl.pallas_call(
        kern,
        grid=(b, N_HEADS, n_tiles // ATTN_TILES),
        in_specs=[pl.BlockSpec((1, 1, ATTN_TILES, V_DIM, 2 * tq), lambda bi, h, i: (bi, h, i, 0, 0)),
                  pl.BlockSpec((1, n_keys, V_DIM), lambda bi, h, i: (bi, 0, h)),
                  pl.BlockSpec((1, 1, V_ROWS, n_keys), lambda bi, h, i: (bi, h, 0, 0)),
                  pl.BlockSpec((5, HEAD_DIM), lambda bi, h, i: (0, 0)),
                  pl.BlockSpec((V_DIM, 1), lambda bi, h, i: (0, 0))],
        out_specs=pl.BlockSpec((1, ATTN_TILES * tq, V_DIM), lambda bi, h, i: (bi, i, h)),
        out_shape=jax.ShapeDtypeStruct((b, l, d), BF16),
        scratch_shapes=[pltpu.VMEM((2 * tk, 2 * tq), BF16)],
        compiler_params=_params("arbitrary", "arbitrary", "arbitrary"),
        name="diff_attention",
    )(qz, k_all, vt_all, lam_params, sub_gain)


def _route(h2, row0, rwh_ref, rwl_ref, rb_ref, before_ref, carry_ref, ridx_ref, gcol_ref, cnt_ref, is_first):
    tm = h2.shape[0]
    hh = h2.astype(BF16)
    hl = (h2 - hh.astype(F32)).astype(BF16)
    rw2 = jnp.concatenate([rwh_ref[...], rwl_ref[...]], axis=0)
    part = lax.dot_general(rw2, hh, NT_DIMS, preferred_element_type=F32)
    logits = (part[:N_EXPERTS] + part[N_EXPERTS:]
              + lax.dot_general(rwh_ref[...], hl, NT_DIMS, preferred_element_type=F32) + rb_ref[...])
    groups = [logits[g * EXPERTS_PER_GROUP:(g + 1) * EXPERTS_PER_GROUP] for g in range(N_EXPERT_GROUPS)]
    top = groups[0]
    for g in range(1, N_EXPERT_GROUPS):
        top = jnp.maximum(top, groups[g])
    top = jnp.max(top, axis=0, keepdims=True)
    sub = lax.broadcasted_iota(jnp.int32, (EXPERTS_PER_GROUP, tm), 0)
    best = None
    for g in range(N_EXPERT_GROUPS):
        ex = jnp.exp(groups[g] - top)
        v1 = jnp.max(ex, axis=0, keepdims=True)
        i1 = jnp.min(jnp.where(ex == v1, sub, EXPERTS_PER_GROUP), axis=0, keepdims=True)
        rest = jnp.where(sub == i1, -1.0, ex)
        v2 = jnp.max(rest, axis=0, keepdims=True)
        i2 = jnp.min(jnp.where(rest == v2, sub, EXPERTS_PER_GROUP), axis=0, keepdims=True)
        cand = (v1 + v2, v1, v2, i1 + g * EXPERTS_PER_GROUP, i2 + g * EXPERTS_PER_GROUP)
        if best is None:
            best = cand
        else:
            better = cand[0] > best[0]
            best = tuple(jnp.where(better, new, old) for new, old in zip(cand, best))
    _, v1, v2, e0, e1 = best
    gate0 = v1 / (v1 + v2)
    gate1 = v2 / (v1 + v2)

    @pl.when(is_first)
    def _():
        carry_ref[...] = jnp.zeros_like(carry_ref)

    erow = lax.broadcasted_iota(jnp.int32, (N_EXPERTS, tm), 0)
    oh0 = erow == e0
    oh1 = erow == e1
    chosen = jnp.where(oh0 | oh1, 1.0, 0.0)
    prior = carry_ref[:, 0:1] + jnp.dot(chosen.astype(BF16), before_ref[...], preferred_element_type=F32)
    r0 = jnp.sum(jnp.where(oh0, prior, 0.0), axis=0, keepdims=True).astype(jnp.int32)
    r1 = jnp.sum(jnp.where(oh1, prior, 0.0), axis=0, keepdims=True).astype(jnp.int32)
    carry_ref[...] = carry_ref[...] + jnp.sum(chosen, axis=1, keepdims=True)
    cnt_ref[...] = carry_ref[...]
    rid = lax.broadcasted_iota(jnp.int32, (SUBLANES, tm), 0)
    ridx_ref[:, row0:row0 + tm] = jnp.where(rid == 0, e0, jnp.where(rid == 1, e1, jnp.where(rid == 2, r0, jnp.where(rid == 3, r1, 0))))
    gid = lax.broadcasted_iota(jnp.int32, (LANES, tm), 0)
    gcol_ref[row0:row0 + tm, :] = jnp.where(gid == 0, gate0, jnp.where(gid == 1, gate1, 0.0)).T


def _tail(y, row0, x_ref, g1_ref, sh2_ref, s2_ref, n2_ref, rwh_ref, rwl_ref, rb_ref, before_ref,
          x_out, h2_out, ridx_ref, gcol_ref, cnt_ref, carry_ref):
    rows = slice(row0, row0 + y.shape[0])
    x1 = x_ref[0, rows] + g1_ref[0] * y
    x_out[0, rows] = x1
    h2 = _rms_mod(x1, n2_ref[...], sh2_ref[0], s2_ref[0])
    h2_out[0, rows] = _pack_bf16_pairs(h2)
    is_first = (pl.program_id(0) == 0) & (pl.program_id(1) == 0) & (row0 == 0)
    _route(h2, row0, rwh_ref, rwl_ref, rb_ref, before_ref, carry_ref, ridx_ref, gcol_ref, cnt_ref, is_first)


def _attn_tail_kernel(a_ref, wo_ref, *rest):
    starts = range(0, a_ref.shape[1], TAIL_ROWS)
    ys = [jnp.dot(a_ref[0, r0:r0 + TAIL_ROWS], wo_ref[...], preferred_element_type=F32) for r0 in starts]
    for r0, y in zip(starts, ys):
        _tail(y, r0, *rest)


def _pool_tail_kernel(u_ref, up_ref, un_ref, wg_ref, cs_ref, wo_ref, *rest, seq_len):
    *tail_refs, ubuf, abuf = rest
    tm = u_ref.shape[1]
    i = pl.program_id(1)
    u = u_ref[0].astype(F32)
    ubuf[0:POOL_HALO] = jnp.where(i > 0, up_ref[0].astype(F32)[HALO_ROWS - POOL_HALO:], 0.0)
    ubuf[POOL_HALO:POOL_HALO + tm] = u
    ubuf[POOL_HALO + tm:2 * POOL_HALO + tm] = jnp.where(i < pl.num_programs(1) - 1,
                                                        un_ref[0].astype(F32)[:POOL_HALO], 0.0)
    pos = i * tm + lax.broadcasted_iota(jnp.int32, (tm, 1), 0)
    gd = wg_ref.shape[1]
    assert all(win == 2 ** (g + 1) for g, win in enumerate(POOL_WINDOWS))
    n_ext = tm + 2 * POOL_HALO
    bufs = (ubuf, abuf)
    abuf[1:n_ext, :] = ubuf[0:n_ext - 1, :] + ubuf[1:n_ext, :]
    lo, hi = 1, n_ext
    for g in range(1, len(POOL_WINDOWS)):
        src, dst = bufs[g % 2], bufs[(g + 1) % 2]
        sh = POOL_WINDOWS[g] // 4
        dst[lo + sh:hi - sh, g * gd:] = src[lo:hi - 2 * sh, g * gd:] + src[lo + 2 * sh:hi, g * gd:]
        lo, hi = lo + sh, hi - sh
    outs = []
    for g, win in enumerate(POOL_WINDOWS):
        half = win // 2
        cols = slice(g * gd, (g + 1) * gd)
        s = bufs[(g + 1) % 2][POOL_HALO:POOL_HALO + tm, cols]
        inv_cnt = 1.0 / (jnp.minimum(pos + half, seq_len) - jnp.maximum(pos - half, 0)).astype(F32)
        dlt = (s * inv_cnt - u[:, cols]).astype(BF16)
        outs.append(jnp.dot(dlt, wg_ref[g], preferred_element_type=F32))
    z = (jnp.concatenate(outs, axis=-1) * cs_ref[...]).astype(BF16)
    starts = range(0, tm, TAIL_ROWS)
    ys = [jnp.dot(z[r0:r0 + TAIL_ROWS], wo_ref[...], preferred_element_type=F32) for r0 in starts]
    for r0, y in zip(starts, ys):
        _tail(y, r0, *tail_refs)


def _mixer_tail(front_args, front_specs, kern, x, g1, sh2, s2, n2g, rwh, rwl, rb, *, tm, scratch=()):
    b, l, d = x.shape
    nt = l // tm
    n = b * l
    row = lambda bi, i: (bi, 0, 0)
    fix = lambda bi, i: (0, 0)
    tile = lambda bi, i: (bi, i, 0)
    in_specs = list(front_specs) + [
        pl.BlockSpec((1, tm, d), tile),
        pl.BlockSpec((1, 1, d), row), pl.BlockSpec((1, 1, d), row), pl.BlockSpec((1, 1, d), row),
        pl.BlockSpec((1, d), fix),
        pl.BlockSpec((N_EXPERTS, d), fix), pl.BlockSpec((N_EXPERTS, d), fix),
        pl.BlockSpec((N_EXPERTS, 1), fix),
        pl.BlockSpec((TAIL_ROWS, TAIL_ROWS), fix)]
    before = jnp.asarray(np.triu(np.ones((TAIL_ROWS, TAIL_ROWS), np.float32), k=1), BF16)
    out_specs = [pl.BlockSpec((1, tm, d), tile), pl.BlockSpec((1, tm, d // 2), tile),
                 pl.BlockSpec((SUBLANES, tm), lambda bi, i: (0, bi * nt + i)),
                 pl.BlockSpec((tm, LANES), lambda bi, i: (bi * nt + i, 0)),
                 pl.BlockSpec((N_EXPERTS, LANES), fix)]
    out_shape = [jax.ShapeDtypeStruct((b, l, d), F32), jax.ShapeDtypeStruct((b, l, d // 2), jnp.int32),
                 jax.ShapeDtypeStruct((SUBLANES, n), jnp.int32), jax.ShapeDtypeStruct((n, LANES), F32),
                 jax.ShapeDtypeStruct((N_EXPERTS, LANES), F32)]
    return pl.pallas_call(
        kern,
        grid=(b, nt),
        in_specs=in_specs, out_specs=out_specs, out_shape=out_shape,
        scratch_shapes=[pltpu.VMEM((N_EXPERTS, LANES), F32)] + list(scratch),
        compiler_params=_params("arbitrary", "arbitrary"),
        name="mixer_tail",
    )(*front_args, x, g1, sh2, s2, n2g, rwh, rwl, rb, before)


def _slot_kernel(ps_ref, ridx_ref, dest_ref):
    ridx = ridx_ref[...]
    ps = ps_ref[...]
    erow = lax.broadcasted_iota(jnp.int32, (N_EXPERTS, ridx.shape[1]), 0)
    rows = []
    for k in range(TOP_K):
        start = jnp.sum(jnp.where(erow == ridx[k:k + 1], ps, 0), axis=0, keepdims=True)
        rows.append(start + ridx[TOP_K + k:TOP_K + k + 1])
    rid = lax.broadcasted_iota(jnp.int32, ridx.shape, 0)
    dest_ref[...] = jnp.where(rid == 0, rows[0], jnp.where(rid == 1, rows[1], 0))


def _slot_index(pad_start, ridx, *, tn):
    n = ridx.shape[1]
    return pl.pallas_call(
        _slot_kernel,
        grid=(n // tn,),
        in_specs=[pl.BlockSpec((N_EXPERTS, 1), lambda i: (0, 0)),
                  pl.BlockSpec((SUBLANES, tn), lambda i: (0, i))],
        out_specs=pl.BlockSpec((SUBLANES, tn), lambda i: (0, i)),
        out_shape=jax.ShapeDtypeStruct((SUBLANES, n), jnp.int32),
        compiler_params=_params("arbitrary"),
        name="slot_index",
    )(pad_start.reshape(N_EXPERTS, 1), ridx)


def _sc_mesh():
    return plsc.VectorSubcoreMesh(core_axis_name="c", subcore_axis_name="s",
                                  num_cores=SC_CORES, num_subcores=SC_SUBCORES)


def _sc_worker_base(per_worker):
    return (lax.axis_index("s") * SC_CORES + lax.axis_index("c")) * per_worker


def _sc_scatter_rows(rows, idx0, idx1, n_slots):
    n, d = rows.shape
    per_worker = n // SC_WORKERS
    assert per_worker % SC_WINDOW == 0

    n_win = per_worker // SC_WINDOW
    assert n_win % 2 == 0

    def body(rows_hbm, i0_hbm, i1_hbm, out_hbm, i0_a, i1_a, rows_a, i0_b, i1_b, rows_b, sem_a, sem_b):
        base = _sc_worker_base(per_worker)

        def offset(j):
            return pl.multiple_of(base + j * SC_WINDOW, SC_WINDOW)

        def start(j, i0_v, i1_v, rows_v, sem):
            pltpu.sync_copy(i0_hbm.at[pl.ds(offset(j), SC_WINDOW)], i0_v)
            pltpu.sync_copy(i1_hbm.at[pl.ds(offset(j), SC_WINDOW)], i1_v)
            pltpu.async_copy(rows_hbm.at[pl.ds(offset(j), SC_WINDOW)], rows_v, sem)

        def finish(j, i0_v, i1_v, rows_v, sem):
            pltpu.make_async_copy(rows_hbm.at[pl.ds(offset(j), SC_WINDOW)], rows_v, sem).wait()
            pltpu.sync_copy(rows_v, out_hbm.at[i0_v])
            pltpu.sync_copy(rows_v, out_hbm.at[i1_v])

        start(0, i0_a, i1_a, rows_a, sem_a)

        @pl.loop(0, n_win, step=2)
        def _(j):
            start(j + 1, i0_b, i1_b, rows_b, sem_b)
            finish(j, i0_a, i1_a, rows_a, sem_a)

            @pl.when(j + 2 < n_win)
            def _():
                start(j + 2, i0_a, i1_a, rows_a, sem_a)

            finish(j + 1, i0_b, i1_b, rows_b, sem_b)

    window = [pltpu.VMEM((SC_WINDOW,), jnp.int32), pltpu.VMEM((SC_WINDOW,), jnp.int32),
              pltpu.VMEM((SC_WINDOW, d), rows.dtype)]
    return pl.kernel(
        body, out_type=jax.ShapeDtypeStruct((n_slots, d), rows.dtype), mesh=_sc_mesh(),
        scratch_types=window + window + [pltpu.SemaphoreType.DMA, pltpu.SemaphoreType.DMA],
        name="sc_scatter_rows",
    )(rows, idx0, idx1)


def _sc_gather_rows(table, idx):
    n = idx.shape[0]
    d = table.shape[1]
    per_worker = n // SC_WORKERS
    assert per_worker % SC_WINDOW == 0

    n_win = per_worker // SC_WINDOW
    assert n_win % 2 == 0

    def body(table_hbm, idx_hbm, out_hbm, idx_a, idx_b, rows_a, rows_b, sem_a, sem_b):
        base = _sc_worker_base(per_worker)

        def offset(j):
            return pl.multiple_of(base + j * SC_WINDOW, SC_WINDOW)

        def start(j, idx_v, rows_v, sem):
            pltpu.sync_copy(idx_hbm.at[pl.ds(offset(j), SC_WINDOW)], idx_v)
            pltpu.async_copy(table_hbm.at[idx_v], rows_v, sem)

        def finish(j, idx_v, rows_v, sem):
            pltpu.make_async_copy(table_hbm.at[idx_v], rows_v, sem).wait()
            pltpu.sync_copy(rows_v, out_hbm.at[pl.ds(offset(j), SC_WINDOW)])

        start(0, idx_a, rows_a, sem_a)

        @pl.loop(0, n_win, step=2)
        def _(j):
            start(j + 1, idx_b, rows_b, sem_b)
            finish(j, idx_a, rows_a, sem_a)

            @pl.when(j + 2 < n_win)
            def _():
                start(j + 2, idx_a, rows_a, sem_a)

            finish(j + 1, idx_b, rows_b, sem_b)

    return pl.kernel(
        body, out_type=jax.ShapeDtypeStruct((n, d), table.dtype), mesh=_sc_mesh(),
        scratch_types=[pltpu.VMEM((SC_WINDOW,), jnp.int32), pltpu.VMEM((SC_WINDOW,), jnp.int32),
                       pltpu.VMEM((SC_WINDOW, d), table.dtype), pltpu.VMEM((SC_WINDOW, d), table.dtype),
                       pltpu.SemaphoreType.DMA, pltpu.SemaphoreType.DMA],
        name="sc_gather_rows",
    )(table, idx)


def _ffn_kernel(be_ref, nv_ref, xs_ref, wg_ref, wu_ref, wd_ref, ys_ref, wg_b, wu_b, wd_b):
    j = pl.program_id(0)
    valid = nv_ref[j]

    @pl.when((valid > 0) & ((j == 0) | (be_ref[j] != be_ref[jnp.maximum(j - 1, 0)])))
    def _():
        wg_b[...] = wg_ref[0, 0].astype(BF16)
        wu_b[...] = wu_ref[0, 0].astype(BF16)
        wd_b[...] = wd_ref[0, 0].astype(BF16)

    @pl.when(valid > 0)
    def _():
        row = lax.broadcasted_iota(jnp.int32, (SLOT_ROWS, 1), 0)
        xw = jnp.where(row < valid, xs_ref[...], 0)
        xb = _unpack_bf16_pairs(xw).astype(BF16)
        g = jnp.dot(xb, wg_b[...], preferred_element_type=F32)
        u = jnp.dot(xb, wu_b[...], preferred_element_type=F32)
        a = (g / (1.0 + jnp.exp(-g)) * u).astype(BF16)
        ys_ref[...] = _pack_bf16_pairs(jnp.dot(a, wd_b[...], preferred_element_type=F32))

    @pl.when(valid <= 0)
    def _():
        ys_ref[...] = jnp.zeros_like(ys_ref)


def _expert_ffn(block_e, n_valid, xs, w_gate, w_up, w_down, layer):
    n_slots, dw = xs.shape
    d, de = w_gate.shape[2:]
    w_idx = lambda j, be, nv: (layer, be[j], 0, 0)
    return pl.pallas_call(
        _ffn_kernel,
        grid_spec=pltpu.PrefetchScalarGridSpec(
            num_scalar_prefetch=2,
            grid=(n_slots // SLOT_ROWS,),
            in_specs=[pl.BlockSpec((SLOT_ROWS, dw), lambda j, be, nv: (j, 0)),
                      pl.BlockSpec((1, 1, d, de), w_idx),
                      pl.BlockSpec((1, 1, d, de), w_idx),
                      pl.BlockSpec((1, 1, de, d), w_idx)],
            out_specs=pl.BlockSpec((SLOT_ROWS, dw), lambda j, be, nv: (j, 0)),
            scratch_shapes=[pltpu.VMEM((d, de), BF16), pltpu.VMEM((d, de), BF16), pltpu.VMEM((de, d), BF16)]),
        out_shape=jax.ShapeDtypeStruct((n_slots, dw), jnp.int32),
        compiler_params=_params("arbitrary"),
        name="expert_ffn",
    )(block_e, n_valid, xs, w_gate, w_up, w_down)


def _combine_kernel(y0_ref, y1_ref, gcol_ref, x_ref, g2_ref, *rest, pool_in):
    if pool_in:
        sh_ref, sc_ref, n1_ref, wi_ref, x_out, u_out = rest
    else:
        (x_out,) = rest
    gc = gcol_ref[...]
    out = gc[:, 0:1] * _unpack_bf16_pairs(y0_ref[0]) + gc[:, 1:2] * _unpack_bf16_pairs(y1_ref[0])
    x2 = x_ref[0] + g2_ref[0] * out
    x_out[0] = x2
    if pool_in:
        hb = _rms_mod(x2, n1_ref[...], sh_ref[0], sc_ref[0]).astype(BF16)
        u_out[0] = jnp.dot(hb, wi_ref[...], preferred_element_type=F32).astype(BF16)


def _combine(yg, gcol, x, g2, pool_args=None, *, tc):
    b, l, d = x.shape
    nt = l // tc
    pool_in = pool_args is not None
    kern = functools.partial(_combine_kernel, pool_in=pool_in)
    row = lambda bi, i: (bi, 0, 0)
    fix = lambda bi, i: (0, 0)
    tile = lambda bi, i: (bi, i, 0)
    in_specs = [pl.BlockSpec((1, tc, d // 2), lambda bi, i: (0, bi * nt + i, 0)),
                pl.BlockSpec((1, tc, d // 2), lambda bi, i: (1, bi * nt + i, 0)),
                pl.BlockSpec((tc, LANES), lambda bi, i: (bi * nt + i, 0)),
                pl.BlockSpec((1, tc, d), tile),
                pl.BlockSpec((1, 1, d), row)]
    out_specs = [pl.BlockSpec((1, tc, d), tile)]
    out_shape = [jax.ShapeDtypeStruct((b, l, d), F32)]
    args = [yg, yg, gcol, x, g2]
    if pool_in:
        in_specs += [pl.BlockSpec((1, 1, d), row), pl.BlockSpec((1, 1, d), row),
                     pl.BlockSpec((1, d), fix), pl.BlockSpec((d, d), fix)]
        out_specs.append(pl.BlockSpec((1, tc, d), tile))
        out_shape.append(jax.ShapeDtypeStruct((b, l, d), BF16))
        args += list(pool_args)
    return pl.pallas_call(
        kern,
        grid=(b, nt),
        in_specs=in_specs, out_specs=out_specs, out_shape=out_shape,
        compiler_params=_params("arbitrary", "arbitrary"),
        name="moe_combine",
    )(*args)


def _moe(h2, ridx, gcol, counts, x1, g2, w_gate, w_up, w_down, layer, pool_args=None):
    b, l, d = x1.shape
    n = b * l
    n_blocks = (n * TOP_K) // SLOT_ROWS + N_EXPERTS
    cnt = counts[:, 0].astype(jnp.int32)
    padded = (cnt + SLOT_ROWS - 1) // SLOT_ROWS * SLOT_ROWS
    earlier = jnp.arange(N_EXPERTS)[None, :] < jnp.arange(N_EXPERTS)[:, None]
    pad_start = jnp.sum(jnp.where(earlier, padded[None, :], 0), axis=1).astype(jnp.int32)
    pad_end = pad_start + padded
    block_start = jnp.arange(n_blocks, dtype=jnp.int32) * SLOT_ROWS
    block_e = jnp.minimum(jnp.sum(pad_end[None, :] <= block_start[:, None], axis=1), N_EXPERTS - 1).astype(jnp.int32)
    own = block_e[:, None] == jnp.arange(N_EXPERTS)[None, :]
    data_end = jnp.sum(jnp.where(own, (pad_start + cnt)[None, :], 0), axis=1)
    n_valid = jnp.clip(data_end - block_start, 0, SLOT_ROWS).astype(jnp.int32)
    dest = _slot_index(pad_start, ridx, tn=2048)
    xs = _sc_scatter_rows(h2.reshape(n, d // 2), dest[0], dest[1], n_blocks * SLOT_ROWS)
    ys = _expert_ffn(block_e, n_valid, xs, w_gate, w_up, w_down, layer)
    yg = _sc_gather_rows(ys, dest[:TOP_K].reshape(TOP_K * n)).reshape(TOP_K, n, d // 2)
    return _combine(yg, gcol, x1, g2, pool_args, tc=512)


def kernel(x, c, ctx, c_ctx, ada_w, ada_b, norm1_g, norm2_g, attn_w_in, attn_w_out, attn_q_gain, attn_k_gain,
           attn_lq1, attn_lk1, attn_lq2, attn_lk2, attn_sub_gain, pool_w_in, pool_w_group, pool_scale, pool_w_out,
           router_w, router_b, moe_w_gate, moe_w_up, moe_w_down):
    b, l, d = x.shape
    n_ctx = ctx.shape[1]
    depth = ada_w.shape[0]
    assert depth == 2 and d == N_HEADS * V_DIM
    tm = 512

    mod = _adaln_mod(c, c_ctx, ada_w, ada_b)
    mods = [[mod[i, :b, None, j * d:(j + 1) * d] for j in range(N_MOD)] for i in range(depth)]
    mod_ctx = [jnp.broadcast_to(mod[0, b, j * d:(j + 1) * d], (b, 1, d)) for j in range(2)]

    rwt = router_w.T
    rwh = rwt.astype(BF16)
    rwl = (rwt - rwh.astype(F32)).astype(BF16)
    rb = router_b.reshape(N_EXPERTS, 1)

    sh1, s1, g1, sh2, s2, g2 = mods[0]
    cos, sin = _rope_tables(l)
    pair_up = (jnp.arange(LANES) & 16) == 0

    def gain_rows(g, factor):
        g2 = jnp.concatenate([g, g]) * factor
        return [g2, jnp.where(pair_up, jnp.roll(g2, -16), jnp.roll(g2, 16))]

    q_scale = HEAD_DIM ** -0.5 * math.log2(math.e)
    q_max = jnp.maximum(jnp.max(jnp.abs(attn_q_gain[0])) * (HEAD_DIM ** 0.5 * q_scale), F32_TINY)
    k_max = jnp.maximum(jnp.max(jnp.abs(attn_k_gain[0])) * HEAD_DIM ** 0.5, F32_TINY)
    need = jnp.ceil(jnp.log2(k_max / F8_MAX))
    room = jnp.floor(jnp.log2(F8_MAX / q_max))
    trade = jnp.exp2(jnp.clip(jnp.clip(0.0, need, jnp.maximum(need, room)), -60.0, 60.0))
    h_max = jnp.float32(0.0)
    for shift_, scale_ in ((sh1, s1), (mod_ctx[0], mod_ctx[1])):
        h_max = jnp.maximum(h_max, jnp.max(d ** 0.5 * jnp.max(jnp.abs(norm1_g[0] * (1.0 + scale_)), axis=-1)
                                           + jnp.sqrt(jnp.sum(shift_ * shift_, axis=-1))))
    w_v = attn_w_in[0][:, 2 * d:]
    v_max = 1.02 * h_max * jnp.sqrt(jnp.max(jnp.sum(w_v * w_v, axis=0)))
    v_grow = jnp.exp2(jnp.clip(jnp.ceil(jnp.log2(jnp.maximum(v_max, F32_TINY) / F8_MAX)), 0.0, 60.0))

    gains = jnp.stack(gain_rows(attn_q_gain[0], q_scale * trade) + gain_rows(attn_k_gain[0], 1.0 / trade))
    w_qk = attn_w_in[0][:, :2 * d].astype(BF16)
    wv_t = (w_v / v_grow).T.astype(BF16)
    assert l % tm == 0 and l % n_ctx == 0
    q, k_all, vt_all = _qkv_proj(x, sh1, s1, norm1_g[0][None], w_qk, wv_t, gains, cos, sin, n_qk=2, rope=True,
                                 tm=tm, n_keys=l + n_ctx, key_row0=0)
    k_all, vt_all = _qkv_proj(ctx, mod_ctx[0], mod_ctx[1], norm1_g[0][None], w_qk[:, d:], wv_t, gains,
                              cos[:n_ctx], sin[:n_ctx], n_qk=1, rope=False,
                              tm=n_ctx, n_keys=l + n_ctx, key_row0=l, kv=(k_all, vt_all))
    lam_init = 0.8 - 0.6 * math.exp(-0.3 * 0)
    lam_params = jnp.stack([attn_lq1[0], attn_lk1[0], attn_lq2[0], attn_lk2[0],
                            jnp.full((HEAD_DIM,), v_grow, F32)])
    o = _diff_attention(q, k_all, vt_all, lam_params, attn_sub_gain[0][:, None], lam_init=lam_init)

    fix = lambda bi, i: (0, 0)
    x1, h2, ridx, gcol, counts = _mixer_tail(
        (o, attn_w_out[0].astype(BF16)),
        (pl.BlockSpec((1, 2 * TAIL_ROWS, d), lambda bi, i: (bi, i, 0)), pl.BlockSpec((d, d), fix)),
        _attn_tail_kernel, x, g1, sh2, s2, norm2_g[0][None], rwh, rwl, rb, tm=2 * TAIL_ROWS)

    sh1b, s1b, g1b, sh2b, s2b, g2b = mods[1]
    x2, u = _moe(h2, ridx, gcol, counts, x1, g2, moe_w_gate, moe_w_up, moe_w_down, 0,
                 pool_args=(sh1b, s1b, norm1_g[1][None], pool_w_in[0].astype(BF16)))
    gd = pool_w_group.shape[2]
    tp = 2 * TAIL_ROWS
    nh = tp // HALO_ROWS
    front_specs = (
        pl.BlockSpec((1, tp, d), lambda bi, i: (bi, i, 0)),
        pl.BlockSpec((1, HALO_ROWS, d), lambda bi, i: (bi, jnp.maximum(i * nh - 1, 0), 0)),
        pl.BlockSpec((1, HALO_ROWS, d), lambda bi, i: (bi, jnp.minimum((i + 1) * nh, l // HALO_ROWS - 1), 0)),
        pl.BlockSpec((len(POOL_WINDOWS), gd, gd), lambda bi, i: (0, 0, 0)),
        pl.BlockSpec((1, d), fix),
        pl.BlockSpec((d, d), fix))
    x3, h2b, ridx_b, gcol_b, counts_b = _mixer_tail(
        (u, u, u, pool_w_group[0].astype(BF16), pool_scale[0][None], pool_w_out[0].astype(BF16)),
        front_specs, functools.partial(_pool_tail_kernel, seq_len=l),
        x2, g1b, sh2b, s2b, norm2_g[1][None], rwh, rwl, rb, tm=tp,
        scratch=[pltpu.VMEM((tp + 2 * POOL_HALO, d), F32)] * 2)
    (out,) = _moe(h2b, ridx_b, gcol_b, counts_b, x3, g2b, moe_w_gate, moe_w_up, moe_w_down, 1)
    return out
```

```python
import functools
import math

import jax
import jax.numpy as jnp
import numpy as np
from jax import lax
from jax.experimental import pallas as pl
from jax.experimental.pallas import tpu as pltpu
from jax.experimental.pallas import tpu_sc as plsc

F32 = jnp.float32
BF16 = jnp.bfloat16
F8 = jnp.float8_e4m3fn
F8_MAX = float(jnp.finfo(F8).max)
F32_TINY = float(jnp.finfo(F32).tiny)

LANES = 128
SUBLANES = 8
N_HEADS = 8
HEAD_DIM = 64
V_DIM = 2 * HEAD_DIM
V_ROWS = V_DIM + 32
P_SHIFT = 8.0
GRID_W = 64
ROPE_THETA = 10000.0
NORM_EPS = 1e-6
N_MOD = 6
POOL_WINDOWS = (2, 4, 8, 16)
POOL_HALO = max(POOL_WINDOWS) // 2
HALO_ROWS = 16
N_EXPERTS = 32
N_EXPERT_GROUPS = 4
EXPERTS_PER_GROUP = N_EXPERTS // N_EXPERT_GROUPS
TOP_K = 2
SLOT_ROWS = 512
MOE_COMBINE_PARTS = 2
TAIL_ROWS = 512
SC_CORES = 2
SC_SUBCORES = 16
SC_WORKERS = SC_CORES * SC_SUBCORES
SC_WINDOW = 64
ATTN_GROUP_CHUNKS = 1
ATTN_SCORE_AHEAD = 1
VMEM_LIMIT = 48 * 1024 * 1024
NT_DIMS = (((1,), (1,)), ((), ()))


def _params(*sem):
    return pltpu.CompilerParams(dimension_semantics=sem, vmem_limit_bytes=VMEM_LIMIT)


def _rms_mod(x, gain, shift, scale):
    inv_rms = lax.rsqrt(jnp.mean(x * x, axis=-1, keepdims=True) + NORM_EPS)
    return x * inv_rms * (gain * (1.0 + scale)) + shift


def _pack_bf16_pairs(x):
    c = x.shape[1] // 2
    hi = lax.bitcast_convert_type(x[:, :c].astype(BF16).astype(F32), jnp.uint32)
    lo = lax.bitcast_convert_type(x[:, c:].astype(BF16).astype(F32), jnp.uint32)
    return lax.bitcast_convert_type(hi | (lo >> 16), jnp.int32)


def _unpack_bf16_pairs(w):
    u = lax.bitcast_convert_type(w, jnp.uint32)
    hi = lax.bitcast_convert_type(u & jnp.uint32(0xFFFF0000), F32)
    lo = lax.bitcast_convert_type(u << 16, F32)
    return jnp.concatenate([hi, lo], axis=1)


def _mod_kernel(c_ref, w_ref, b_ref, o_ref):
    c = c_ref[...]
    a = c / (1.0 + jnp.exp(-c))
    o_ref[0] = jnp.dot(a, w_ref[0], precision=lax.Precision.HIGHEST,
                       preferred_element_type=F32) + b_ref[0]


def _adaln_mod(c, c_ctx, ada_w, ada_b):
    depth, d, n_out = ada_w.shape
    b = c.shape[0]
    assert b + 1 <= SUBLANES
    rows = jnp.concatenate([c, c_ctx[None], jnp.zeros((SUBLANES - b - 1, d), F32)], axis=0)
    tn = n_out // 4
    return pl.pallas_call(
        _mod_kernel,
        grid=(depth, n_out // tn),
        in_specs=[pl.BlockSpec((SUBLANES, d), lambda i, j: (0, 0)),
                  pl.BlockSpec((1, d, tn), lambda i, j: (i, 0, j)),
                  pl.BlockSpec((1, 1, tn), lambda i, j: (i, 0, j))],
        out_specs=pl.BlockSpec((1, SUBLANES, tn), lambda i, j: (i, 0, j)),
        out_shape=jax.ShapeDtypeStruct((depth, SUBLANES, n_out), F32),
        compiler_params=_params("arbitrary", "arbitrary"),
        name="adaln_mod",
    )(rows, ada_w, ada_b.reshape(depth, 1, n_out))


def _qkv_kernel(x_ref, sh_ref, sc_ref, g_ref, w_ref, wvt_ref, gains_ref, cos_ref, sin_ref, *refs, n_qk, rope):
    out_refs = refs[-(n_qk + 1):]
    tm, d = x_ref.shape[1:]
    hb = _rms_mod(x_ref[0], g_ref[...], sh_ref[0], sc_ref[0]).astype(BF16)
    lane_b4 = (lax.broadcasted_iota(jnp.int32, (1, LANES), 1) & 16) == 0
    chunk_r = lax.broadcasted_iota(jnp.int32, (2 * LANES, 2 * LANES), 0) // HEAD_DIM
    chunk_c = lax.broadcasted_iota(jnp.int32, (2 * LANES, 2 * LANES), 1) // HEAD_DIM
    same_chunk = (chunk_r == chunk_c).astype(BF16)
    gains = gains_ref[...]
    tables = []
    for t in range(n_qk):
        r = 2 * (t + 2 - n_qk)
        tables.append((cos_ref[...] * gains[r:r + 1], sin_ref[...] * gains[r + 1:r + 2]) if rope
                      else (gains[r:r + 1], None))

    def project(t, j):
        return jnp.dot(hb, w_ref[:, t * d + j:t * d + j + 2 * LANES], preferred_element_type=F32)

    def chunk_sums(acc):
        return jnp.dot((acc * acc).astype(BF16), same_chunk, preferred_element_type=F32)

    def finish(t, j, acc, ssq):
        cos_t, sin_t = tables[t]
        nrm = acc * lax.rsqrt(ssq * (1.0 / HEAD_DIM) + NORM_EPS)
        for half in range(2):
            blk = nrm[:, half * LANES:(half + 1) * LANES]
            if rope:
                rot = jnp.where(lane_b4, pltpu.roll(blk, LANES - 16, 1), pltpu.roll(blk, 16, 1))
                y = blk * cos_t + rot * sin_t
            else:
                y = blk * cos_t
            c0 = j + half * LANES
            if t < n_qk - 1:
                yt = y.T
                first = lax.broadcasted_iota(jnp.int32, (V_DIM, 1), 0) < HEAD_DIM
                out_refs[t][0, c0 // LANES, 0, :, 0:tm] = jnp.where(first, yt, 0.0).astype(F8)
                out_refs[t][0, c0 // LANES, 0, :, tm:2 * tm] = jnp.where(first, 0.0, yt).astype(F8)
            else:
                out_refs[t][0, :, c0:c0 + LANES] = y.astype(F8)

    blocks = [(t, j) for t in range(n_qk) for j in range(0, d, 2 * LANES)]
    accs, sums = {}, {}
    for i in range(-2, len(blocks)):
        if i + 2 < len(blocks):
            accs[i + 2] = project(*blocks[i + 2])
        if 0 <= i + 1 < len(blocks):
            sums[i + 1] = chunk_sums(accs[i + 1])
        if i >= 0:
            finish(*blocks[i], accs.pop(i), sums.pop(i))
    vt_ref = out_refs[n_qk]
    ones = jnp.where(lax.broadcasted_iota(jnp.int32, (V_ROWS - V_DIM, tm), 0) == 0, 1.0, 0.0).astype(F8)
    for j in range(0, d, 2 * LANES):
        acc_t = lax.dot_general(wvt_ref[j:j + 2 * LANES, :], hb, NT_DIMS, preferred_element_type=F32)
        for half in range(2):
            h = j // LANES + half
            vt_ref[0, h, 0:V_DIM, :] = acc_t[half * V_DIM:(half + 1) * V_DIM].astype(F8)
            vt_ref[0, h, V_DIM:V_ROWS, :] = ones


def _qkv_proj(x, shift, scale, gain, w, wv_t, gains, cos, sin, *, n_qk, rope, tm, n_keys, key_row0, kv=None):
    b, l, d = x.shape
    n_out = n_qk + 1
    kern = functools.partial(_qkv_kernel, n_qk=n_qk, rope=rope)
    row = lambda bi, i: (bi, 0, 0)
    fix = lambda bi, i: (0, 0)
    kb = key_row0 // tm
    tile = pl.BlockSpec((1, tm, d), lambda bi, i: (bi, i, 0))
    k_spec = pl.BlockSpec((1, tm, d), lambda bi, i: (bi, kb + i, 0))
    vt_spec = pl.BlockSpec((1, N_HEADS, V_ROWS, tm), lambda bi, i: (bi, 0, 0, kb + i))
    q_spec = pl.BlockSpec((1, N_HEADS, 1, V_DIM, 2 * tm), lambda bi, i: (bi, 0, i, 0, 0))
    q_shape = jax.ShapeDtypeStruct((b, N_HEADS, l // tm, V_DIM, 2 * tm), F8)
    k_shape = jax.ShapeDtypeStruct((b, n_keys, d), F8)
    vt_shape = jax.ShapeDtypeStruct((b, N_HEADS, V_ROWS, n_keys), F8)
    in_specs = [tile,
                pl.BlockSpec((1, 1, d), row), pl.BlockSpec((1, 1, d), row),
                pl.BlockSpec((1, d), fix),
                pl.BlockSpec((d, n_qk * d), fix),
                pl.BlockSpec((d, d), fix),
                pl.BlockSpec((4, LANES), fix),
                pl.BlockSpec((tm, LANES), lambda bi, i: (i, 0)),
                pl.BlockSpec((tm, LANES), lambda bi, i: (i, 0))]
    args = [x, shift, scale, gain, w, wv_t, gains, cos, sin]
    aliases = {}
    if kv is not None:
        aliases = {len(args): n_qk - 1, len(args) + 1: n_qk}
        in_specs += [pl.BlockSpec(memory_space=pl.ANY)] * 2
        args += list(kv)
    return pl.pallas_call(
        kern,
        grid=(b, l // tm),
        in_specs=in_specs,
        out_specs=[q_spec] * (n_qk - 1) + [k_spec, vt_spec],
        out_shape=[q_shape] * (n_qk - 1) + [k_shape, vt_shape],
        input_output_aliases=aliases,
        compiler_params=_params("arbitrary", "arbitrary"),
        name="qkv_proj",
    )(*args)


def _rope_tables(n_tokens):
    rows = n_tokens // GRID_W
    row = np.repeat(np.arange(rows, dtype=np.float32), GRID_W)
    col = np.tile(np.arange(GRID_W, dtype=np.float32), rows)
    half = HEAD_DIM // 2
    inv_freq = (np.float32(ROPE_THETA) ** (-np.arange(0, half, 2, dtype=np.float32) / half)).astype(np.float32)
    ang_r = row[:, None] * inv_freq
    ang_c = col[:, None] * inv_freq
    ang = np.concatenate([ang_r, ang_r, ang_c, ang_c] * 2, axis=-1)
    sign = np.where((np.arange(LANES) & 16) == 0, -1.0, 1.0).astype(np.float32)
    return jnp.asarray(np.cos(ang), F32), jnp.asarray(np.sin(ang) * sign, F32)


def _attn_kernel(q_ref, k_ref, vt_ref, lp_ref, sg_ref, o_ref, s_ref, *, tk, group, ahead, lam_init):
    tq = q_ref.shape[4] // 2
    n_chunks = k_ref.shape[1] // tk
    qz = q_ref[0, 0, 0]
    n_slots = (ahead + 1) * group

    def score_chunk(c, m_grp):
        st = jnp.dot(k_ref[0, c * tk:(c + 1) * tk, :], qz, preferred_element_type=F32).astype(BF16)
        slot = c % n_slots
        s_ref[slot * tk:(slot + 1) * tk, :] = st
        mc = jnp.max(st, axis=0, keepdims=True)
        return mc if m_grp is None else jnp.maximum(m_grp, mc)

    def value_chunk(c, m_ref, part):
        slot = c % n_slots
        p = jnp.exp2(s_ref[slot * tk:(slot + 1) * tk, :] - (m_ref - P_SHIFT)).astype(F8)
        pv = jnp.dot(vt_ref[0, 0, :, c * tk:(c + 1) * tk], p, preferred_element_type=F32)
        return pv if part is None else part + pv

    groups = [list(range(g0, min(g0 + group, n_chunks))) for g0 in range(0, n_chunks, group)]
    m_of = {}
    for g in range(min(ahead, len(groups))):
        for c in groups[g]:
            m_of[g] = score_chunk(c, m_of.get(g))
    m = None
    acc = None
    for gi, cur in enumerate(groups):
        nxt = groups[gi + ahead] if gi + ahead < len(groups) else []
        m_new = m_of[gi] if m is None else jnp.maximum(m, m_of[gi])
        part = None
        for i in range(max(len(cur), len(nxt))):
            if i < len(nxt):
                m_of[gi + ahead] = score_chunk(nxt[i], m_of.get(gi + ahead))
            if i < len(cur):
                part = value_chunk(cur[i], m_new, part)
        acc = part if acc is None else acc * jnp.exp2(m.astype(F32) - m_new.astype(F32)) + part
        m = m_new
    acc = acc[:V_DIM] / acc[V_DIM:V_DIM + 1]
    lp = lp_ref[...]
    lam = (jnp.exp(jnp.sum(lp[0:1] * lp[1:2], axis=-1, keepdims=True))
           - jnp.exp(jnp.sum(lp[2:3] * lp[3:4], axis=-1, keepdims=True)) + lam_init)
    o = (acc[:, :tq] - lam * acc[:, tq:]) * lp[4:5, 0:1]
    o = o * lax.rsqrt(jnp.mean(o * o, axis=0, keepdims=True) + NORM_EPS) * sg_ref[...] * (1.0 - lam_init)
    o_ref[0] = o.T.astype(BF16)


def _attn_chunk(n_keys):
    for tk in (768, 512, 256, 128):
        if n_keys % tk == 0:
            return tk
    raise ValueError(f"key count {n_keys} is not a multiple of {LANES}")


def _diff_attention(qz, k_all, vt_all, lam_params, sub_gain, *, lam_init):
    b, _, n_tiles, _, tq2 = qz.shape
    tq = tq2 // 2
    l, d = n_tiles * tq, N_HEADS * V_DIM
    n_keys = k_all.shape[1]
    tk = _attn_chunk(n_keys)
    kern = functools.partial(_attn_kernel, tk=tk, group=ATTN_GROUP_CHUNKS, ahead=ATTN_SCORE_AHEAD, lam_init=lam_init)
    return pl.pallas_call(
        kern,
        grid=(b, N_HEADS, l // tq),
        in_specs=[pl.BlockSpec((1, 1, 1, V_DIM, 2 * tq), lambda bi, h, i: (bi, h, i, 0, 0)),
                  pl.BlockSpec((1, n_keys, V_DIM), lambda bi, h, i: (bi, 0, h)),
                  pl.BlockSpec((1, 1, V_ROWS, n_keys), lambda bi, h, i: (bi, h, 0, 0)),
                  pl.BlockSpec((5, HEAD_DIM), lambda bi, h, i: (0, 0)),
                  pl.BlockSpec((V_DIM, 1), lambda bi, h, i: (0, 0))],
        out_specs=pl.BlockSpec((1, tq, V_DIM), lambda bi, h, i: (bi, i, h)),
        out_shape=jax.ShapeDtypeStruct((b, l, d), BF16),
        scratch_shapes=[pltpu.VMEM(((ATTN_SCORE_AHEAD + 1) * ATTN_GROUP_CHUNKS * tk, 2 * tq), BF16)],
        compiler_params=_params("arbitrary", "arbitrary", "arbitrary"),
        name="diff_attention",
    )(qz, k_all, vt_all, lam_params, sub_gain)


def _route(h2, row0, rwh_ref, rwl_ref, rb_ref, before_ref, carry_ref, ridx_ref, gcol_ref, cnt_ref, is_first):
    tm = h2.shape[0]
    hh = h2.astype(BF16)
    hl = (h2 - hh.astype(F32)).astype(BF16)
    rw2 = jnp.concatenate([rwh_ref[...], rwl_ref[...]], axis=0)
    part = lax.dot_general(rw2, hh, NT_DIMS, preferred_element_type=F32)
    logits = (part[:N_EXPERTS] + part[N_EXPERTS:]
              + lax.dot_general(rwh_ref[...], hl, NT_DIMS, preferred_element_type=F32) + rb_ref[...])
    groups = [logits[g * EXPERTS_PER_GROUP:(g + 1) * EXPERTS_PER_GROUP] for g in range(N_EXPERT_GROUPS)]
    top = groups[0]
    for g in range(1, N_EXPERT_GROUPS):
        top = jnp.maximum(top, groups[g])
    top = jnp.max(top, axis=0, keepdims=True)
    sub = lax.broadcasted_iota(jnp.int32, (EXPERTS_PER_GROUP, tm), 0)
    best = None
    for g in range(N_EXPERT_GROUPS):
        ex = jnp.exp(groups[g] - top)
        v1 = jnp.max(ex, axis=0, keepdims=True)
        i1 = jnp.min(jnp.where(ex == v1, sub, EXPERTS_PER_GROUP), axis=0, keepdims=True)
        rest = jnp.where(sub == i1, -1.0, ex)
        v2 = jnp.max(rest, axis=0, keepdims=True)
        i2 = jnp.min(jnp.where(rest == v2, sub, EXPERTS_PER_GROUP), axis=0, keepdims=True)
        cand = (v1 + v2, v1, v2, i1 + g * EXPERTS_PER_GROUP, i2 + g * EXPERTS_PER_GROUP)
        if best is None:
            best = cand
        else:
            better = cand[0] > best[0]
            best = tuple(jnp.where(better, new, old) for new, old in zip(cand, best))
    _, v1, v2, e0, e1 = best
    gate0 = v1 / (v1 + v2)
    gate1 = v2 / (v1 + v2)

    @pl.when(is_first)
    def _():
        carry_ref[...] = jnp.zeros_like(carry_ref)

    erow = lax.broadcasted_iota(jnp.int32, (N_EXPERTS, tm), 0)
    oh0 = erow == e0
    oh1 = erow == e1
    chosen = jnp.where(oh0 | oh1, 1.0, 0.0)
    prior = carry_ref[:, 0:1] + jnp.dot(chosen.astype(BF16), before_ref[...], preferred_element_type=F32)
    r0 = jnp.sum(jnp.where(oh0, prior, 0.0), axis=0, keepdims=True).astype(jnp.int32)
    r1 = jnp.sum(jnp.where(oh1, prior, 0.0), axis=0, keepdims=True).astype(jnp.int32)
    carry_ref[...] = carry_ref[...] + jnp.sum(chosen, axis=1, keepdims=True)
    cnt_ref[...] = carry_ref[...]
    rid = lax.broadcasted_iota(jnp.int32, (SUBLANES, tm), 0)
    ridx_ref[:, row0:row0 + tm] = jnp.where(rid == 0, e0, jnp.where(rid == 1, e1, jnp.where(rid == 2, r0, jnp.where(rid == 3, r1, 0))))
    gid = lax.broadcasted_iota(jnp.int32, (LANES, tm), 0)
    gcol_ref[row0:row0 + tm, :] = jnp.where(gid == 0, gate0, jnp.where(gid == 1, gate1, 0.0)).T


def _tail(y, row0, x_ref, g1_ref, sh2_ref, s2_ref, n2_ref, rwh_ref, rwl_ref, rb_ref, before_ref,
          x_out, h2_out, ridx_ref, gcol_ref, cnt_ref, carry_ref):
    rows = slice(row0, row0 + y.shape[0])
    x1 = x_ref[0, rows] + g1_ref[0] * y
    x_out[0, rows] = x1
    h2 = _rms_mod(x1, n2_ref[...], sh2_ref[0], s2_ref[0])
    h2_out[0, rows] = _pack_bf16_pairs(h2)
    is_first = (pl.program_id(0) == 0) & (pl.program_id(1) == 0) & (row0 == 0)
    _route(h2, row0, rwh_ref, rwl_ref, rb_ref, before_ref, carry_ref, ridx_ref, gcol_ref, cnt_ref, is_first)


def _attn_tail_kernel(a_ref, wo_ref, *rest):
    starts = range(0, a_ref.shape[1], TAIL_ROWS)
    ys = [jnp.dot(a_ref[0, r0:r0 + TAIL_ROWS], wo_ref[...], preferred_element_type=F32) for r0 in starts]
    for r0, y in zip(starts, ys):
        _tail(y, r0, *rest)


def _pool_tail_kernel(u_ref, up_ref, un_ref, wg_ref, cs_ref, wo_ref, *rest, seq_len):
    *tail_refs, ubuf, abuf = rest
    tm = u_ref.shape[1]
    i = pl.program_id(1)
    u = u_ref[0].astype(F32)
    ubuf[0:POOL_HALO] = jnp.where(i > 0, up_ref[0].astype(F32)[HALO_ROWS - POOL_HALO:], 0.0)
    ubuf[POOL_HALO:POOL_HALO + tm] = u
    ubuf[POOL_HALO + tm:2 * POOL_HALO + tm] = jnp.where(i < pl.num_programs(1) - 1,
                                                        un_ref[0].astype(F32)[:POOL_HALO], 0.0)
    pos = i * tm + lax.broadcasted_iota(jnp.int32, (tm, 1), 0)
    gd = wg_ref.shape[1]
    assert all(win == 2 ** (g + 1) for g, win in enumerate(POOL_WINDOWS))
    n_ext = tm + 2 * POOL_HALO
    bufs = (ubuf, abuf)
    abuf[1:n_ext, :] = ubuf[0:n_ext - 1, :] + ubuf[1:n_ext, :]
    lo, hi = 1, n_ext
    for g in range(1, len(POOL_WINDOWS)):
        src, dst = bufs[g % 2], bufs[(g + 1) % 2]
        sh = POOL_WINDOWS[g] // 4
        dst[lo + sh:hi - sh, g * gd:] = src[lo:hi - 2 * sh, g * gd:] + src[lo + 2 * sh:hi, g * gd:]
        lo, hi = lo + sh, hi - sh
    outs = []
    for g, win in enumerate(POOL_WINDOWS):
        half = win // 2
        cols = slice(g * gd, (g + 1) * gd)
        s = bufs[(g + 1) % 2][POOL_HALO:POOL_HALO + tm, cols]
        inv_cnt = 1.0 / (jnp.minimum(pos + half, seq_len) - jnp.maximum(pos - half, 0)).astype(F32)
        dlt = (s * inv_cnt - u[:, cols]).astype(BF16)
        outs.append(jnp.dot(dlt, wg_ref[g], preferred_element_type=F32))
    z = (jnp.concatenate(outs, axis=-1) * cs_ref[...]).astype(BF16)
    starts = range(0, tm, TAIL_ROWS)
    ys = [jnp.dot(z[r0:r0 + TAIL_ROWS], wo_ref[...], preferred_element_type=F32) for r0 in starts]
    for r0, y in zip(starts, ys):
        _tail(y, r0, *tail_refs)


def _mixer_tail(front_args, front_specs, kern, x, g1, sh2, s2, n2g, rwh, rwl, rb, *, tm, scratch=()):
    b, l, d = x.shape
    nt = l // tm
    n = b * l
    row = lambda bi, i: (bi, 0, 0)
    fix = lambda bi, i: (0, 0)
    tile = lambda bi, i: (bi, i, 0)
    in_specs = list(front_specs) + [
        pl.BlockSpec((1, tm, d), tile),
        pl.BlockSpec((1, 1, d), row), pl.BlockSpec((1, 1, d), row), pl.BlockSpec((1, 1, d), row),
        pl.BlockSpec((1, d), fix),
        pl.BlockSpec((N_EXPERTS, d), fix), pl.BlockSpec((N_EXPERTS, d), fix),
        pl.BlockSpec((N_EXPERTS, 1), fix),
        pl.BlockSpec((TAIL_ROWS, TAIL_ROWS), fix)]
    before = jnp.asarray(np.triu(np.ones((TAIL_ROWS, TAIL_ROWS), np.float32), k=1), BF16)
    out_specs = [pl.BlockSpec((1, tm, d), tile), pl.BlockSpec((1, tm, d // 2), tile),
                 pl.BlockSpec((SUBLANES, tm), lambda bi, i: (0, bi * nt + i)),
                 pl.BlockSpec((tm, LANES), lambda bi, i: (bi * nt + i, 0)),
                 pl.BlockSpec((N_EXPERTS, LANES), fix)]
    out_shape = [jax.ShapeDtypeStruct((b, l, d), F32), jax.ShapeDtypeStruct((b, l, d // 2), jnp.int32),
                 jax.ShapeDtypeStruct((SUBLANES, n), jnp.int32), jax.ShapeDtypeStruct((n, LANES), F32),
                 jax.ShapeDtypeStruct((N_EXPERTS, LANES), F32)]
    return pl.pallas_call(
        kern,
        grid=(b, nt),
        in_specs=in_specs, out_specs=out_specs, out_shape=out_shape,
        scratch_shapes=[pltpu.VMEM((N_EXPERTS, LANES), F32)] + list(scratch),
        compiler_params=_params("arbitrary", "arbitrary"),
        name="mixer_tail",
    )(*front_args, x, g1, sh2, s2, n2g, rwh, rwl, rb, before)


def _slot_kernel(ps_ref, ridx_ref, dest_ref):
    ridx = ridx_ref[...]
    ps = ps_ref[...]
    erow = lax.broadcasted_iota(jnp.int32, (N_EXPERTS, ridx.shape[1]), 0)
    rows = []
    for k in range(TOP_K):
        start = jnp.sum(jnp.where(erow == ridx[k:k + 1], ps, 0), axis=0, keepdims=True)
        rows.append(start + ridx[TOP_K + k:TOP_K + k + 1])
    rid = lax.broadcasted_iota(jnp.int32, ridx.shape, 0)
    dest_ref[...] = jnp.where(rid == 0, rows[0], jnp.where(rid == 1, rows[1], 0))


def _slot_index(pad_start, ridx, *, tn):
    n = ridx.shape[1]
    return pl.pallas_call(
        _slot_kernel,
        grid=(n // tn,),
        in_specs=[pl.BlockSpec((N_EXPERTS, 1), lambda i: (0, 0)),
                  pl.BlockSpec((SUBLANES, tn), lambda i: (0, i))],
        out_specs=pl.BlockSpec((SUBLANES, tn), lambda i: (0, i)),
        out_shape=jax.ShapeDtypeStruct((SUBLANES, n), jnp.int32),
        compiler_params=_params("arbitrary"),
        name="slot_index",
    )(pad_start.reshape(N_EXPERTS, 1), ridx)


def _sc_mesh():
    return plsc.VectorSubcoreMesh(core_axis_name="c", subcore_axis_name="s",
                                  num_cores=SC_CORES, num_subcores=SC_SUBCORES)


def _sc_worker_base(per_worker):
    return (lax.axis_index("s") * SC_CORES + lax.axis_index("c")) * per_worker


def _sc_scatter_rows(rows, idx0, idx1, n_slots):
    n, d = rows.shape
    per_worker = n // SC_WORKERS
    assert per_worker % SC_WINDOW == 0

    n_win = per_worker // SC_WINDOW
    assert n_win % 2 == 0

    def body(rows_hbm, i0_hbm, i1_hbm, out_hbm, i0_a, i1_a, rows_a, i0_b, i1_b, rows_b, sem_a, sem_b):
        base = _sc_worker_base(per_worker)

        def offset(j):
            return pl.multiple_of(base + j * SC_WINDOW, SC_WINDOW)

        def start(j, i0_v, i1_v, rows_v, sem):
            pltpu.sync_copy(i0_hbm.at[pl.ds(offset(j), SC_WINDOW)], i0_v)
            pltpu.sync_copy(i1_hbm.at[pl.ds(offset(j), SC_WINDOW)], i1_v)
            pltpu.async_copy(rows_hbm.at[pl.ds(offset(j), SC_WINDOW)], rows_v, sem)

        def finish(j, i0_v, i1_v, rows_v, sem):
            pltpu.make_async_copy(rows_hbm.at[pl.ds(offset(j), SC_WINDOW)], rows_v, sem).wait()
            pltpu.sync_copy(rows_v, out_hbm.at[i0_v])
            pltpu.sync_copy(rows_v, out_hbm.at[i1_v])

        start(0, i0_a, i1_a, rows_a, sem_a)

        @pl.loop(0, n_win, step=2)
        def _(j):
            start(j + 1, i0_b, i1_b, rows_b, sem_b)
            finish(j, i0_a, i1_a, rows_a, sem_a)

            @pl.when(j + 2 < n_win)
            def _():
                start(j + 2, i0_a, i1_a, rows_a, sem_a)

            finish(j + 1, i0_b, i1_b, rows_b, sem_b)

    window = [pltpu.VMEM((SC_WINDOW,), jnp.int32), pltpu.VMEM((SC_WINDOW,), jnp.int32),
              pltpu.VMEM((SC_WINDOW, d), rows.dtype)]
    return pl.kernel(
        body, out_type=jax.ShapeDtypeStruct((n_slots, d), rows.dtype), mesh=_sc_mesh(),
        scratch_types=window + window + [pltpu.SemaphoreType.DMA, pltpu.SemaphoreType.DMA],
        name="sc_scatter_rows",
    )(rows, idx0, idx1)


def _sc_gather_rows(table, idx):
    n = idx.shape[0]
    d = table.shape[1]
    per_worker = n // SC_WORKERS
    assert per_worker % SC_WINDOW == 0

    n_win = per_worker // SC_WINDOW
    assert n_win % 2 == 0

    def body(table_hbm, idx_hbm, out_hbm, idx_a, idx_b, rows_a, rows_b, sem_a, sem_b):
        base = _sc_worker_base(per_worker)

        def offset(j):
            return pl.multiple_of(base + j * SC_WINDOW, SC_WINDOW)

        def start(j, idx_v, rows_v, sem):
            pltpu.sync_copy(idx_hbm.at[pl.ds(offset(j), SC_WINDOW)], idx_v)
            pltpu.async_copy(table_hbm.at[idx_v], rows_v, sem)

        def finish(j, idx_v, rows_v, sem):
            pltpu.make_async_copy(table_hbm.at[idx_v], rows_v, sem).wait()
            pltpu.sync_copy(rows_v, out_hbm.at[pl.ds(offset(j), SC_WINDOW)])

        start(0, idx_a, rows_a, sem_a)

        @pl.loop(0, n_win, step=2)
        def _(j):
            start(j + 1, idx_b, rows_b, sem_b)
            finish(j, idx_a, rows_a, sem_a)

            @pl.when(j + 2 < n_win)
            def _():
                start(j + 2, idx_a, rows_a, sem_a)

            finish(j + 1, idx_b, rows_b, sem_b)

    return pl.kernel(
        body, out_type=jax.ShapeDtypeStruct((n, d), table.dtype), mesh=_sc_mesh(),
        scratch_types=[pltpu.VMEM((SC_WINDOW,), jnp.int32), pltpu.VMEM((SC_WINDOW,), jnp.int32),
                       pltpu.VMEM((SC_WINDOW, d), table.dtype), pltpu.VMEM((SC_WINDOW, d), table.dtype),
                       pltpu.SemaphoreType.DMA, pltpu.SemaphoreType.DMA],
        name="sc_gather_rows",
    )(table, idx)


def _ffn_kernel(be_ref, nv_ref, xs_ref, wg_ref, wu_ref, wd_ref, ys_ref, wg_b, wu_b, wd_b):
    j = pl.program_id(0)
    valid = nv_ref[j]

    @pl.when((valid > 0) & ((j == 0) | (be_ref[j] != be_ref[jnp.maximum(j - 1, 0)])))
    def _():
        wg_b[...] = wg_ref[0, 0].astype(BF16)
        wu_b[...] = wu_ref[0, 0].astype(BF16)
        wd_b[...] = wd_ref[0, 0].astype(BF16)

    @pl.when(valid > 0)
    def _():
        row = lax.broadcasted_iota(jnp.int32, (SLOT_ROWS, 1), 0)
        xw = jnp.where(row < valid, xs_ref[...], 0)
        xb = _unpack_bf16_pairs(xw).astype(BF16)
        g = jnp.dot(xb, wg_b[...], preferred_element_type=F32)
        u = jnp.dot(xb, wu_b[...], preferred_element_type=F32)
        a = (g / (1.0 + jnp.exp(-g)) * u).astype(BF16)
        ys_ref[...] = _pack_bf16_pairs(jnp.dot(a, wd_b[...], preferred_element_type=F32))

    @pl.when(valid <= 0)
    def _():
        ys_ref[...] = jnp.zeros_like(ys_ref)


def _expert_ffn(block_e, n_valid, xs, w_gate, w_up, w_down, layer):
    n_slots, dw = xs.shape
    d, de = w_gate.shape[2:]
    w_idx = lambda j, be, nv: (layer, be[j], 0, 0)
    return pl.pallas_call(
        _ffn_kernel,
        grid_spec=pltpu.PrefetchScalarGridSpec(
            num_scalar_prefetch=2,
            grid=(n_slots // SLOT_ROWS,),
            in_specs=[pl.BlockSpec((SLOT_ROWS, dw), lambda j, be, nv: (j, 0)),
                      pl.BlockSpec((1, 1, d, de), w_idx),
                      pl.BlockSpec((1, 1, d, de), w_idx),
                      pl.BlockSpec((1, 1, de, d), w_idx)],
            out_specs=pl.BlockSpec((SLOT_ROWS, dw), lambda j, be, nv: (j, 0)),
            scratch_shapes=[pltpu.VMEM((d, de), BF16), pltpu.VMEM((d, de), BF16), pltpu.VMEM((de, d), BF16)]),
        out_shape=jax.ShapeDtypeStruct((n_slots, dw), jnp.int32),
        compiler_params=_params("arbitrary"),
        name="expert_ffn",
    )(block_e, n_valid, xs, w_gate, w_up, w_down)


def _combine_kernel(y0_ref, y1_ref, gcol_ref, x_ref, g2_ref, *rest, pool_in):
    if pool_in:
        sh_ref, sc_ref, n1_ref, wi_ref = rest[:4]
        x_out, u_out = rest[-2:]
    else:
        x_out = rest[-1]
    gc = gcol_ref[...]
    out = gc[:, 0:1] * _unpack_bf16_pairs(y0_ref[0]) + gc[:, 1:2] * _unpack_bf16_pairs(y1_ref[0])
    x2 = x_ref[0] + g2_ref[0] * out
    x_out[0] = x2
    if pool_in:
        hb = _rms_mod(x2, n1_ref[...], sh_ref[0], sc_ref[0]).astype(BF16)
        u_out[0] = jnp.dot(hb, wi_ref[...], preferred_element_type=F32).astype(BF16)


def _combine(yg, gcol, x, g2, pool_args=None, *, tc, b0, prev=None):
    b, l, d = x.shape
    nt = l // tc
    nb = yg.shape[1] // l
    pool_in = pool_args is not None
    kern = functools.partial(_combine_kernel, pool_in=pool_in)
    row = lambda bi, i: (b0 + bi, 0, 0)
    fix = lambda bi, i: (0, 0)
    tile = lambda bi, i: (b0 + bi, i, 0)
    in_specs = [pl.BlockSpec((1, tc, d // 2), lambda bi, i: (0, bi * nt + i, 0)),
                pl.BlockSpec((1, tc, d // 2), lambda bi, i: (1, bi * nt + i, 0)),
                pl.BlockSpec((tc, LANES), lambda bi, i: ((b0 + bi) * nt + i, 0)),
                pl.BlockSpec((1, tc, d), tile),
                pl.BlockSpec((1, 1, d), row)]
    out_specs = [pl.BlockSpec((1, tc, d), tile)]
    out_shape = [jax.ShapeDtypeStruct((b, l, d), F32)]
    args = [yg, yg, gcol, x, g2]
    if pool_in:
        in_specs += [pl.BlockSpec((1, 1, d), row), pl.BlockSpec((1, 1, d), row),
                     pl.BlockSpec((1, d), fix), pl.BlockSpec((d, d), fix)]
        out_specs.append(pl.BlockSpec((1, tc, d), tile))
        out_shape.append(jax.ShapeDtypeStruct((b, l, d), BF16))
        args += list(pool_args)
    aliases = {}
    if prev is not None:
        aliases = {len(args) + j: j for j in range(len(prev))}
        in_specs += [pl.BlockSpec(memory_space=pl.ANY)] * len(prev)
        args += list(prev)
    return pl.pallas_call(
        kern,
        grid=(nb, nt),
        in_specs=in_specs, out_specs=out_specs, out_shape=out_shape,
        input_output_aliases=aliases,
        compiler_params=_params("arbitrary", "arbitrary"),
        name="moe_combine",
    )(*args)


def _moe(h2, ridx, gcol, counts, x1, g2, w_gate, w_up, w_down, layer, pool_args=None):
    b, l, d = x1.shape
    n = b * l
    n_blocks = (n * TOP_K) // SLOT_ROWS + N_EXPERTS
    cnt = counts[:, 0].astype(jnp.int32)
    padded = (cnt + SLOT_ROWS - 1) // SLOT_ROWS * SLOT_ROWS
    earlier = jnp.arange(N_EXPERTS)[None, :] < jnp.arange(N_EXPERTS)[:, None]
    pad_start = jnp.sum(jnp.where(earlier, padded[None, :], 0), axis=1).astype(jnp.int32)
    pad_end = pad_start + padded
    block_start = jnp.arange(n_blocks, dtype=jnp.int32) * SLOT_ROWS
    block_e = jnp.minimum(jnp.sum(pad_end[None, :] <= block_start[:, None], axis=1), N_EXPERTS - 1).astype(jnp.int32)
    own = block_e[:, None] == jnp.arange(N_EXPERTS)[None, :]
    data_end = jnp.sum(jnp.where(own, (pad_start + cnt)[None, :], 0), axis=1)
    n_valid = jnp.clip(data_end - block_start, 0, SLOT_ROWS).astype(jnp.int32)
    dest = _slot_index(pad_start, ridx, tn=2048)
    xs = _sc_scatter_rows(h2.reshape(n, d // 2), dest[0], dest[1], n_blocks * SLOT_ROWS)
    ys = _expert_ffn(block_e, n_valid, xs, w_gate, w_up, w_down, layer)
    assert b % MOE_COMBINE_PARTS == 0
    nb = b // MOE_COMBINE_PARTS
    outs = None
    for part in range(MOE_COMBINE_PARTS):
        idx = dest[:TOP_K, part * nb * l:(part + 1) * nb * l].reshape(TOP_K * nb * l)
        yg = _sc_gather_rows(ys, idx).reshape(TOP_K, nb * l, d // 2)
        outs = _combine(yg, gcol, x1, g2, pool_args, tc=512, b0=part * nb, prev=outs)
    return outs


def kernel(x, c, ctx, c_ctx, ada_w, ada_b, norm1_g, norm2_g, attn_w_in, attn_w_out, attn_q_gain, attn_k_gain,
           attn_lq1, attn_lk1, attn_lq2, attn_lk2, attn_sub_gain, pool_w_in, pool_w_group, pool_scale, pool_w_out,
           router_w, router_b, moe_w_gate, moe_w_up, moe_w_down):
    b, l, d = x.shape
    n_ctx = ctx.shape[1]
    depth = ada_w.shape[0]
    assert depth == 2 and d == N_HEADS * V_DIM
    tm = 512

    mod = _adaln_mod(c, c_ctx, ada_w, ada_b)
    mods = [[mod[i, :b, None, j * d:(j + 1) * d] for j in range(N_MOD)] for i in range(depth)]
    mod_ctx = [jnp.broadcast_to(mod[0, b, j * d:(j + 1) * d], (b, 1, d)) for j in range(2)]

    rwt = router_w.T
    rwh = rwt.astype(BF16)
    rwl = (rwt - rwh.astype(F32)).astype(BF16)
    rb = router_b.reshape(N_EXPERTS, 1)

    sh1, s1, g1, sh2, s2, g2 = mods[0]
    cos, sin = _rope_tables(l)
    pair_up = (jnp.arange(LANES) & 16) == 0

    def gain_rows(g, factor):
        g2 = jnp.concatenate([g, g]) * factor
        return [g2, jnp.where(pair_up, jnp.roll(g2, -16), jnp.roll(g2, 16))]

    q_scale = HEAD_DIM ** -0.5 * math.log2(math.e)
    q_max = jnp.maximum(jnp.max(jnp.abs(attn_q_gain[0])) * (HEAD_DIM ** 0.5 * q_scale), F32_TINY)
    k_max = jnp.maximum(jnp.max(jnp.abs(attn_k_gain[0])) * HEAD_DIM ** 0.5, F32_TINY)
    need = jnp.ceil(jnp.log2(k_max / F8_MAX))
    room = jnp.floor(jnp.log2(F8_MAX / q_max))
    trade = jnp.exp2(jnp.clip(jnp.clip(0.0, need, jnp.maximum(need, room)), -60.0, 60.0))
    h_max = jnp.float32(0.0)
    for shift_, scale_ in ((sh1, s1), (mod_ctx[0], mod_ctx[1])):
        h_max = jnp.maximum(h_max, jnp.max(d ** 0.5 * jnp.max(jnp.abs(norm1_g[0] * (1.0 + scale_)), axis=-1)
                                           + jnp.sqrt(jnp.sum(shift_ * shift_, axis=-1))))
    w_v = attn_w_in[0][:, 2 * d:]
    v_max = 1.02 * h_max * jnp.sqrt(jnp.max(jnp.sum(w_v * w_v, axis=0)))
    v_grow = jnp.exp2(jnp.clip(jnp.ceil(jnp.log2(jnp.maximum(v_max, F32_TINY) / F8_MAX)), 0.0, 60.0))

    gains = jnp.stack(gain_rows(attn_q_gain[0], q_scale * trade) + gain_rows(attn_k_gain[0], 1.0 / trade))
    w_qk = attn_w_in[0][:, :2 * d].astype(BF16)
    wv_t = (w_v / v_grow).T.astype(BF16)
    assert l % tm == 0 and l % n_ctx == 0
    q, k_all, vt_all = _qkv_proj(x, sh1, s1, norm1_g[0][None], w_qk, wv_t, gains, cos, sin, n_qk=2, rope=True,
                                 tm=tm, n_keys=l + n_ctx, key_row0=0)
    k_all, vt_all = _qkv_proj(ctx, mod_ctx[0], mod_ctx[1], norm1_g[0][None], w_qk[:, d:], wv_t, gains,
                              cos[:n_ctx], sin[:n_ctx], n_qk=1, rope=False,
                              tm=n_ctx, n_keys=l + n_ctx, key_row0=l, kv=(k_all, vt_all))
    lam_init = 0.8 - 0.6 * math.exp(-0.3 * 0)
    lam_params = jnp.stack([attn_lq1[0], attn_lk1[0], attn_lq2[0], attn_lk2[0],
                            jnp.full((HEAD_DIM,), v_grow, F32)])
    o = _diff_attention(q, k_all, vt_all, lam_params, attn_sub_gain[0][:, None], lam_init=lam_init)

    fix = lambda bi, i: (0, 0)
    x1, h2, ridx, gcol, counts = _mixer_tail(
        (o, attn_w_out[0].astype(BF16)),
        (pl.BlockSpec((1, 2 * TAIL_ROWS, d), lambda bi, i: (bi, i, 0)), pl.BlockSpec((d, d), fix)),
        _attn_tail_kernel, x, g1, sh2, s2, norm2_g[0][None], rwh, rwl, rb, tm=2 * TAIL_ROWS)

    sh1b, s1b, g1b, sh2b, s2b, g2b = mods[1]
    x2, u = _moe(h2, ridx, gcol, counts, x1, g2, moe_w_gate, moe_w_up, moe_w_down, 0,
                 pool_args=(sh1b, s1b, norm1_g[1][None], pool_w_in[0].astype(BF16)))
    gd = pool_w_group.shape[2]
    tp = 2 * TAIL_ROWS
    nh = tp // HALO_ROWS
    front_specs = (
        pl.BlockSpec((1, tp, d), lambda bi, i: (bi, i, 0)),
        pl.BlockSpec((1, HALO_ROWS, d), lambda bi, i: (bi, jnp.maximum(i * nh - 1, 0), 0)),
        pl.BlockSpec((1, HALO_ROWS, d), lambda bi, i: (bi, jnp.minimum((i + 1) * nh, l // HALO_ROWS - 1), 0)),
        pl.BlockSpec((len(POOL_WINDOWS), gd, gd), lambda bi, i: (0, 0, 0)),
        pl.BlockSpec((1, d), fix),
        pl.BlockSpec((d, d), fix))
    x3, h2b, ridx_b, gcol_b, counts_b = _mixer_tail(
        (u, u, u, pool_w_group[0].astype(BF16), pool_scale[0][None], pool_w_out[0].astype(BF16)),
        front_specs, functools.partial(_pool_tail_kernel, seq_len=l),
        x2, g1b, sh2b, s2b, norm2_g[1][None], rwh, rwl, rb, tm=tp,
        scratch=[pltpu.VMEM((tp + 2 * POOL_HALO, d), F32)] * 2)
    (out,) = _moe(h2b, ridx_b, gcol_b, counts_b, x3, g2b, moe_w_gate, moe_w_up, moe_w_down, 1)
    return out
```

```python
import functools
import math

import jax
import jax.numpy as jnp
import numpy as np
from jax import lax
from jax.experimental import pallas as pl
from jax.experimental.pallas import tpu as pltpu
from jax.experimental.pallas import tpu_sc as plsc

F32 = jnp.float32
BF16 = jnp.bfloat16
F8 = jnp.float8_e4m3fn
F8_MAX = float(jnp.finfo(F8).max)
F32_TINY = float(jnp.finfo(F32).tiny)

LANES = 128
SUBLANES = 8
N_HEADS = 8
HEAD_DIM = 64
V_DIM = 2 * HEAD_DIM
V_ROWS = V_DIM + 32
P_SHIFT = 8.0
GRID_W = 64
ROPE_THETA = 10000.0
NORM_EPS = 1e-6
N_MOD = 6
POOL_WINDOWS = (2, 4, 8, 16)
POOL_HALO = max(POOL_WINDOWS) // 2
HALO_ROWS = 16
N_EXPERTS = 32
N_EXPERT_GROUPS = 4
EXPERTS_PER_GROUP = N_EXPERTS // N_EXPERT_GROUPS
TOP_K = 2
SLOT_ROWS = 512
TAIL_ROWS = 512
SC_CORES = 2
SC_SUBCORES = 16
SC_WORKERS = SC_CORES * SC_SUBCORES
SC_WINDOW = 64
ATTN_GROUP_CHUNKS = 1
ATTN_SCORE_AHEAD = 1
VMEM_LIMIT = 48 * 1024 * 1024
NT_DIMS = (((1,), (1,)), ((), ()))


def _params(*sem):
    return pltpu.CompilerParams(dimension_semantics=sem, vmem_limit_bytes=VMEM_LIMIT)


def _rms_mod(x, gain, shift, scale):
    inv_rms = lax.rsqrt(jnp.mean(x * x, axis=-1, keepdims=True) + NORM_EPS)
    return x * inv_rms * (gain * (1.0 + scale)) + shift


def _pack_bf16_pairs(x):
    c = x.shape[1] // 2
    hi = lax.bitcast_convert_type(x[:, :c].astype(BF16).astype(F32), jnp.uint32)
    lo = lax.bitcast_convert_type(x[:, c:].astype(BF16).astype(F32), jnp.uint32)
    return lax.bitcast_convert_type(hi | (lo >> 16), jnp.int32)


def _unpack_bf16_pairs(w):
    u = lax.bitcast_convert_type(w, jnp.uint32)
    hi = lax.bitcast_convert_type(u & jnp.uint32(0xFFFF0000), F32)
    lo = lax.bitcast_convert_type(u << 16, F32)
    return jnp.concatenate([hi, lo], axis=1)


def _mod_kernel(c_ref, w_ref, b_ref, o_ref):
    c = c_ref[...]
    a = c / (1.0 + jnp.exp(-c))
    o_ref[0] = jnp.dot(a, w_ref[0], precision=lax.Precision.HIGHEST,
                       preferred_element_type=F32) + b_ref[0]


def _adaln_mod(c, c_ctx, ada_w, ada_b):
    depth, d, n_out = ada_w.shape
    b = c.shape[0]
    assert b + 1 <= SUBLANES
    rows = jnp.concatenate([c, c_ctx[None], jnp.zeros((SUBLANES - b - 1, d), F32)], axis=0)
    tn = n_out // 4
    return pl.pallas_call(
        _mod_kernel,
        grid=(depth, n_out // tn),
        in_specs=[pl.BlockSpec((SUBLANES, d), lambda i, j: (0, 0)),
                  pl.BlockSpec((1, d, tn), lambda i, j: (i, 0, j)),
                  pl.BlockSpec((1, 1, tn), lambda i, j: (i, 0, j))],
        out_specs=pl.BlockSpec((1, SUBLANES, tn), lambda i, j: (i, 0, j)),
        out_shape=jax.ShapeDtypeStruct((depth, SUBLANES, n_out), F32),
        compiler_params=_params("arbitrary", "arbitrary"),
        name="adaln_mod",
    )(rows, ada_w, ada_b.reshape(depth, 1, n_out))


def _qkv_kernel(x_ref, sh_ref, sc_ref, g_ref, w_ref, wvt_ref, gains_ref, cos_ref, sin_ref, *refs, n_qk, rope):
    out_refs = refs[-(n_qk + 1):]
    tm, d = x_ref.shape[1:]
    hb = _rms_mod(x_ref[0], g_ref[...], sh_ref[0], sc_ref[0]).astype(BF16)
    lane_b4 = (lax.broadcasted_iota(jnp.int32, (1, LANES), 1) & 16) == 0
    chunk_r = lax.broadcasted_iota(jnp.int32, (2 * LANES, 2 * LANES), 0) // HEAD_DIM
    chunk_c = lax.broadcasted_iota(jnp.int32, (2 * LANES, 2 * LANES), 1) // HEAD_DIM
    same_chunk = (chunk_r == chunk_c).astype(BF16)
    gains = gains_ref[...]
    tables = []
    for t in range(n_qk):
        r = 2 * (t + 2 - n_qk)
        tables.append((cos_ref[...] * gains[r:r + 1], sin_ref[...] * gains[r + 1:r + 2]) if rope
                      else (gains[r:r + 1], None))

    def project(t, j):
        return jnp.dot(hb, w_ref[:, t * d + j:t * d + j + 2 * LANES], preferred_element_type=F32)

    def chunk_sums(acc):
        return jnp.dot((acc * acc).astype(BF16), same_chunk, preferred_element_type=F32)

    def finish(t, j, acc, ssq):
        cos_t, sin_t = tables[t]
        nrm = acc * lax.rsqrt(ssq * (1.0 / HEAD_DIM) + NORM_EPS)
        for half in range(2):
            blk = nrm[:, half * LANES:(half + 1) * LANES]
            if rope:
                rot = jnp.where(lane_b4, pltpu.roll(blk, LANES - 16, 1), pltpu.roll(blk, 16, 1))
                y = blk * cos_t + rot * sin_t
            else:
                y = blk * cos_t
            c0 = j + half * LANES
            if t < n_qk - 1:
                yt = y.T
                first = lax.broadcasted_iota(jnp.int32, (V_DIM, 1), 0) < HEAD_DIM
                out_refs[t][0, c0 // LANES, 0, :, 0:tm] = jnp.where(first, yt, 0.0).astype(F8)
                out_refs[t][0, c0 // LANES, 0, :, tm:2 * tm] = jnp.where(first, 0.0, yt).astype(F8)
            else:
                out_refs[t][0, :, c0:c0 + LANES] = y.astype(F8)

    blocks = [(t, j) for t in range(n_qk) for j in range(0, d, 2 * LANES)]
    accs, sums = {}, {}
    for i in range(-2, len(blocks)):
        if i + 2 < len(blocks):
            accs[i + 2] = project(*blocks[i + 2])
        if 0 <= i + 1 < len(blocks):
            sums[i + 1] = chunk_sums(accs[i + 1])
        if i >= 0:
            finish(*blocks[i], accs.pop(i), sums.pop(i))
    vt_ref = out_refs[n_qk]
    ones = jnp.where(lax.broadcasted_iota(jnp.int32, (V_ROWS - V_DIM, tm), 0) == 0, 1.0, 0.0).astype(F8)
    for j in range(0, d, 2 * LANES):
        acc_t = lax.dot_general(wvt_ref[j:j + 2 * LANES, :], hb, NT_DIMS, preferred_element_type=F32)
        for half in range(2):
            h = j // LANES + half
            vt_ref[0, h, 0:V_DIM, :] = acc_t[half * V_DIM:(half + 1) * V_DIM].astype(F8)
            vt_ref[0, h, V_DIM:V_ROWS, :] = ones


def _qkv_proj(x, shift, scale, gain, w, wv_t, gains, cos, sin, *, n_qk, rope, tm, n_keys, key_row0, kv=None):
    b, l, d = x.shape
    n_out = n_qk + 1
    kern = functools.partial(_qkv_kernel, n_qk=n_qk, rope=rope)
    row = lambda bi, i: (bi, 0, 0)
    fix = lambda bi, i: (0, 0)
    kb = key_row0 // tm
    tile = pl.BlockSpec((1, tm, d), lambda bi, i: (bi, i, 0))
    k_spec = pl.BlockSpec((1, tm, d), lambda bi, i: (bi, kb + i, 0))
    vt_spec = pl.BlockSpec((1, N_HEADS, V_ROWS, tm), lambda bi, i: (bi, 0, 0, kb + i))
    q_spec = pl.BlockSpec((1, N_HEADS, 1, V_DIM, 2 * tm), lambda bi, i: (bi, 0, i, 0, 0))
    q_shape = jax.ShapeDtypeStruct((b, N_HEADS, l // tm, V_DIM, 2 * tm), F8)
    k_shape = jax.ShapeDtypeStruct((b, n_keys, d), F8)
    vt_shape = jax.ShapeDtypeStruct((b, N_HEADS, V_ROWS, n_keys), F8)
    in_specs = [tile,
                pl.BlockSpec((1, 1, d), row), pl.BlockSpec((1, 1, d), row),
                pl.BlockSpec((1, d), fix),
                pl.BlockSpec((d, n_qk * d), fix),
                pl.BlockSpec((d, d), fix),
                pl.BlockSpec((4, LANES), fix),
                pl.BlockSpec((tm, LANES), lambda bi, i: (i, 0)),
                pl.BlockSpec((tm, LANES), lambda bi, i: (i, 0))]
    args = [x, shift, scale, gain, w, wv_t, gains, cos, sin]
    aliases = {}
    if kv is not None:
        aliases = {len(args): n_qk - 1, len(args) + 1: n_qk}
        in_specs += [pl.BlockSpec(memory_space=pl.ANY)] * 2
        args += list(kv)
    return pl.pallas_call(
        kern,
        grid=(b, l // tm),
        in_specs=in_specs,
        out_specs=[q_spec] * (n_qk - 1) + [k_spec, vt_spec],
        out_shape=[q_shape] * (n_qk - 1) + [k_shape, vt_shape],
        input_output_aliases=aliases,
        compiler_params=_params("arbitrary", "arbitrary"),
        name="qkv_proj",
    )(*args)


def _rope_tables(n_tokens):
    rows = n_tokens // GRID_W
    row = np.repeat(np.arange(rows, dtype=np.float32), GRID_W)
    col = np.tile(np.arange(GRID_W, dtype=np.float32), rows)
    half = HEAD_DIM // 2
    inv_freq = (np.float32(ROPE_THETA) ** (-np.arange(0, half, 2, dtype=np.float32) / half)).astype(np.float32)
    ang_r = row[:, None] * inv_freq
    ang_c = col[:, None] * inv_freq
    ang = np.concatenate([ang_r, ang_r, ang_c, ang_c] * 2, axis=-1)
    sign = np.where((np.arange(LANES) & 16) == 0, -1.0, 1.0).astype(np.float32)
    return jnp.asarray(np.cos(ang), F32), jnp.asarray(np.sin(ang) * sign, F32)


def _attn_kernel(q_ref, k_ref, vt_ref, lp_ref, sg_ref, o_ref, s_ref, *, tk, group, ahead, lam_init):
    tq = q_ref.shape[4] // 2
    n_chunks = k_ref.shape[1] // tk
    qz = q_ref[0, 0, 0]
    n_slots = (ahead + 1) * group

    def score_chunk(c, m_grp):
        st = jnp.dot(k_ref[0, c * tk:(c + 1) * tk, :], qz, preferred_element_type=F32).astype(BF16)
        slot = c % n_slots
        s_ref[slot * tk:(slot + 1) * tk, :] = st
        mc = jnp.max(st, axis=0, keepdims=True)
        return mc if m_grp is None else jnp.maximum(m_grp, mc)

    def value_chunk(c, m_ref, part):
        slot = c % n_slots
        p = jnp.exp2(s_ref[slot * tk:(slot + 1) * tk, :] - (m_ref - P_SHIFT)).astype(F8)
        pv = jnp.dot(vt_ref[0, 0, :, c * tk:(c + 1) * tk], p, preferred_element_type=F32)
        return pv if part is None else part + pv

    groups = [list(range(g0, min(g0 + group, n_chunks))) for g0 in range(0, n_chunks, group)]
    m_of = {}
    for g in range(min(ahead, len(groups))):
        for c in groups[g]:
            m_of[g] = score_chunk(c, m_of.get(g))
    m = None
    acc = None
    for gi, cur in enumerate(groups):
        nxt = groups[gi + ahead] if gi + ahead < len(groups) else []
        m_new = m_of[gi] if m is None else jnp.maximum(m, m_of[gi])
        part = None
        for i in range(max(len(cur), len(nxt))):
            if i < len(nxt):
                m_of[gi + ahead] = score_chunk(nxt[i], m_of.get(gi + ahead))
            if i < len(cur):
                part = value_chunk(cur[i], m_new, part)
        acc = part if acc is None else acc * jnp.exp2(m.astype(F32) - m_new.astype(F32)) + part
        m = m_new
    acc = acc[:V_DIM] / acc[V_DIM:V_DIM + 1]
    lp = lp_ref[...]
    lam = (jnp.exp(jnp.sum(lp[0:1] * lp[1:2], axis=-1, keepdims=True))
           - jnp.exp(jnp.sum(lp[2:3] * lp[3:4], axis=-1, keepdims=True)) + lam_init)
    o = (acc[:, :tq] - lam * acc[:, tq:]) * lp[4:5, 0:1]
    o = o * lax.rsqrt(jnp.mean(o * o, axis=0, keepdims=True) + NORM_EPS) * sg_ref[...] * (1.0 - lam_init)
    o_ref[0] = o.T.astype(BF16)


def _attn_chunk(n_keys):
    for tk in (768, 512, 256, 128):
        if n_keys % tk == 0:
            return tk
    raise ValueError(f"key count {n_keys} is not a multiple of {LANES}")


def _diff_attention(qz, k_all, vt_all, lam_params, sub_gain, *, lam_init):
    b, _, n_tiles, _, tq2 = qz.shape
    tq = tq2 // 2
    l, d = n_tiles * tq, N_HEADS * V_DIM
    n_keys = k_all.shape[1]
    tk = _attn_chunk(n_keys)
    kern = functools.partial(_attn_kernel, tk=tk, group=ATTN_GROUP_CHUNKS, ahead=ATTN_SCORE_AHEAD, lam_init=lam_init)
    return pl.pallas_call(
        kern,
        grid=(b, N_HEADS, l // tq),
        in_specs=[pl.BlockSpec((1, 1, 1, V_DIM, 2 * tq), lambda bi, h, i: (bi, h, i, 0, 0)),
                  pl.BlockSpec((1, n_keys, V_DIM), lambda bi, h, i: (bi, 0, h)),
                  pl.BlockSpec((1, 1, V_ROWS, n_keys), lambda bi, h, i: (bi, h, 0, 0)),
                  pl.BlockSpec((5, HEAD_DIM), lambda bi, h, i: (0, 0)),
                  pl.BlockSpec((V_DIM, 1), lambda bi, h, i: (0, 0))],
        out_specs=pl.BlockSpec((1, tq, V_DIM), lambda bi, h, i: (bi, i, h)),
        out_shape=jax.ShapeDtypeStruct((b, l, d), BF16),
        scratch_shapes=[pltpu.VMEM(((ATTN_SCORE_AHEAD + 1) * ATTN_GROUP_CHUNKS * tk, 2 * tq), BF16)],
        compiler_params=_params("arbitrary", "arbitrary", "arbitrary"),
        name="diff_attention",
    )(qz, k_all, vt_all, lam_params, sub_gain)


def _route(h2, row0, rwh_ref, rwl_ref, rb_ref, before_ref, carry_ref, ridx_ref, gcol_ref, cnt_ref, is_first):
    tm = h2.shape[0]
    hh = h2.astype(BF16)
    hl = (h2 - hh.astype(F32)).astype(BF16)
    rw2 = jnp.concatenate([rwh_ref[...], rwl_ref[...]], axis=0)
    part = lax.dot_general(rw2, hh, NT_DIMS, preferred_element_type=F32)
    logits = (part[:N_EXPERTS] + part[N_EXPERTS:]
              + lax.dot_general(rwh_ref[...], hl, NT_DIMS, preferred_element_type=F32) + rb_ref[...])
    groups = [logits[g * EXPERTS_PER_GROUP:(g + 1) * EXPERTS_PER_GROUP] for g in range(N_EXPERT_GROUPS)]
    top = groups[0]
    for g in range(1, N_EXPERT_GROUPS):
        top = jnp.maximum(top, groups[g])
    top = jnp.max(top, axis=0, keepdims=True)
    sub = lax.broadcasted_iota(jnp.int32, (EXPERTS_PER_GROUP, tm), 0)
    best = None
    for g in range(N_EXPERT_GROUPS):
        ex = jnp.exp(groups[g] - top)
        v1 = jnp.max(ex, axis=0, keepdims=True)
        i1 = jnp.min(jnp.where(ex == v1, sub, EXPERTS_PER_GROUP), axis=0, keepdims=True)
        rest = jnp.where(sub == i1, -1.0, ex)
        v2 = jnp.max(rest, axis=0, keepdims=True)
        i2 = jnp.min(jnp.where(rest == v2, sub, EXPERTS_PER_GROUP), axis=0, keepdims=True)
        cand = (v1 + v2, v1, v2, i1 + g * EXPERTS_PER_GROUP, i2 + g * EXPERTS_PER_GROUP)
        if best is None:
            best = cand
        else:
            better = cand[0] > best[0]
            best = tuple(jnp.where(better, new, old) for new, old in zip(cand, best))
    _, v1, v2, e0, e1 = best
    gate0 = v1 / (v1 + v2)
    gate1 = v2 / (v1 + v2)

    @pl.when(is_first)
    def _():
        carry_ref[...] = jnp.zeros_like(carry_ref)

    erow = lax.broadcasted_iota(jnp.int32, (N_EXPERTS, tm), 0)
    oh0 = erow == e0
    oh1 = erow == e1
    chosen = jnp.where(oh0 | oh1, 1.0, 0.0)
    prior = carry_ref[:, 0:1] + jnp.dot(chosen.astype(BF16), before_ref[...], preferred_element_type=F32)
    r0 = jnp.sum(jnp.where(oh0, prior, 0.0), axis=0, keepdims=True).astype(jnp.int32)
    r1 = jnp.sum(jnp.where(oh1, prior, 0.0), axis=0, keepdims=True).astype(jnp.int32)
    carry_ref[...] = carry_ref[...] + jnp.sum(chosen, axis=1, keepdims=True)
    cnt_ref[...] = carry_ref[...]
    rid = lax.broadcasted_iota(jnp.int32, (SUBLANES, tm), 0)
    ridx_ref[:, row0:row0 + tm] = jnp.where(rid == 0, e0, jnp.where(rid == 1, e1, jnp.where(rid == 2, r0, jnp.where(rid == 3, r1, 0))))
    gcol_ref[:, row0:row0 + tm] = jnp.where(rid == 0, gate0, jnp.where(rid == 1, gate1, 0.0))


def _tail(y, row0, x_ref, g1_ref, sh2_ref, s2_ref, n2_ref, rwh_ref, rwl_ref, rb_ref, before_ref,
          x_out, h2_out, ridx_ref, gcol_ref, cnt_ref, carry_ref):
    rows = slice(row0, row0 + y.shape[0])
    x1 = x_ref[0, rows] + g1_ref[0] * y
    x_out[0, rows] = x1
    h2 = _rms_mod(x1, n2_ref[...], sh2_ref[0], s2_ref[0])
    h2_out[0, rows] = _pack_bf16_pairs(h2)
    is_first = (pl.program_id(0) == 0) & (pl.program_id(1) == 0) & (row0 == 0)
    _route(h2, row0, rwh_ref, rwl_ref, rb_ref, before_ref, carry_ref, ridx_ref, gcol_ref, cnt_ref, is_first)


def _attn_tail_kernel(a_ref, wo_ref, *rest):
    starts = range(0, a_ref.shape[1], TAIL_ROWS)
    ys = [jnp.dot(a_ref[0, r0:r0 + TAIL_ROWS], wo_ref[...], preferred_element_type=F32) for r0 in starts]
    for r0, y in zip(starts, ys):
        _tail(y, r0, *rest)


def _pool_tail_kernel(u_ref, up_ref, un_ref, wg_ref, cs_ref, wo_ref, *rest, seq_len):
    *tail_refs, ubuf, abuf = rest
    tm = u_ref.shape[1]
    i = pl.program_id(1)
    u = u_ref[0].astype(F32)
    ubuf[0:POOL_HALO] = jnp.where(i > 0, up_ref[0].astype(F32)[HALO_ROWS - POOL_HALO:], 0.0)
    ubuf[POOL_HALO:POOL_HALO + tm] = u
    ubuf[POOL_HALO + tm:2 * POOL_HALO + tm] = jnp.where(i < pl.num_programs(1) - 1,
                                                        un_ref[0].astype(F32)[:POOL_HALO], 0.0)
    pos = i * tm + lax.broadcasted_iota(jnp.int32, (tm, 1), 0)
    gd = wg_ref.shape[1]
    assert all(win == 2 ** (g + 1) for g, win in enumerate(POOL_WINDOWS))
    n_ext = tm + 2 * POOL_HALO
    bufs = (ubuf, abuf)
    abuf[1:n_ext, :] = ubuf[0:n_ext - 1, :] + ubuf[1:n_ext, :]
    lo, hi = 1, n_ext
    for g in range(1, len(POOL_WINDOWS)):
        src, dst = bufs[g % 2], bufs[(g + 1) % 2]
        sh = POOL_WINDOWS[g] // 4
        dst[lo + sh:hi - sh, g * gd:] = src[lo:hi - 2 * sh, g * gd:] + src[lo + 2 * sh:hi, g * gd:]
        lo, hi = lo + sh, hi - sh
    outs = []
    for g, win in enumerate(POOL_WINDOWS):
        half = win // 2
        cols = slice(g * gd, (g + 1) * gd)
        s = bufs[(g + 1) % 2][POOL_HALO:POOL_HALO + tm, cols]
        inv_cnt = 1.0 / (jnp.minimum(pos + half, seq_len) - jnp.maximum(pos - half, 0)).astype(F32)
        dlt = (s * inv_cnt - u[:, cols]).astype(BF16)
        outs.append(jnp.dot(dlt, wg_ref[g], preferred_element_type=F32))
    z = (jnp.concatenate(outs, axis=-1) * cs_ref[...]).astype(BF16)
    starts = range(0, tm, TAIL_ROWS)
    ys = [jnp.dot(z[r0:r0 + TAIL_ROWS], wo_ref[...], preferred_element_type=F32) for r0 in starts]
    for r0, y in zip(starts, ys):
        _tail(y, r0, *tail_refs)


def _mixer_tail(front_args, front_specs, kern, x, g1, sh2, s2, n2g, rwh, rwl, rb, *, tm, scratch=()):
    b, l, d = x.shape
    nt = l // tm
    n = b * l
    row = lambda bi, i: (bi, 0, 0)
    fix = lambda bi, i: (0, 0)
    tile = lambda bi, i: (bi, i, 0)
    in_specs = list(front_specs) + [
        pl.BlockSpec((1, tm, d), tile),
        pl.BlockSpec((1, 1, d), row), pl.BlockSpec((1, 1, d), row), pl.BlockSpec((1, 1, d), row),
        pl.BlockSpec((1, d), fix),
        pl.BlockSpec((N_EXPERTS, d), fix), pl.BlockSpec((N_EXPERTS, d), fix),
        pl.BlockSpec((N_EXPERTS, 1), fix),
        pl.BlockSpec((TAIL_ROWS, TAIL_ROWS), fix)]
    before = jnp.asarray(np.triu(np.ones((TAIL_ROWS, TAIL_ROWS), np.float32), k=1), BF16)
    out_specs = [pl.BlockSpec((1, tm, d), tile), pl.BlockSpec((1, tm, d // 2), tile),
                 pl.BlockSpec((SUBLANES, tm), lambda bi, i: (0, bi * nt + i)),
                 pl.BlockSpec((SUBLANES, tm), lambda bi, i: (0, bi * nt + i)),
                 pl.BlockSpec((N_EXPERTS, LANES), fix)]
    out_shape = [jax.ShapeDtypeStruct((b, l, d), F32), jax.ShapeDtypeStruct((b, l, d // 2), jnp.int32),
                 jax.ShapeDtypeStruct((SUBLANES, n), jnp.int32), jax.ShapeDtypeStruct((SUBLANES, n), F32),
                 jax.ShapeDtypeStruct((N_EXPERTS, LANES), F32)]
    return pl.pallas_call(
        kern,
        grid=(b, nt),
        in_specs=in_specs, out_specs=out_specs, out_shape=out_shape,
        scratch_shapes=[pltpu.VMEM((N_EXPERTS, LANES), F32)] + list(scratch),
        compiler_params=_params("arbitrary", "arbitrary"),
        name="mixer_tail",
    )(*front_args, x, g1, sh2, s2, n2g, rwh, rwl, rb, before)


def _slot_kernel(ps_ref, ridx_ref, dest_ref):
    ridx = ridx_ref[...]
    ps = ps_ref[...]
    erow = lax.broadcasted_iota(jnp.int32, (N_EXPERTS, ridx.shape[1]), 0)
    rows = []
    for k in range(TOP_K):
        start = jnp.sum(jnp.where(erow == ridx[k:k + 1], ps, 0), axis=0, keepdims=True)
        rows.append(start + ridx[TOP_K + k:TOP_K + k + 1])
    rid = lax.broadcasted_iota(jnp.int32, ridx.shape, 0)
    dest_ref[...] = jnp.where(rid == 0, rows[0], jnp.where(rid == 1, rows[1], 0))


def _slot_index(pad_start, ridx, *, tn):
    n = ridx.shape[1]
    return pl.pallas_call(
        _slot_kernel,
        grid=(n // tn,),
        in_specs=[pl.BlockSpec((N_EXPERTS, 1), lambda i: (0, 0)),
                  pl.BlockSpec((SUBLANES, tn), lambda i: (0, i))],
        out_specs=pl.BlockSpec((SUBLANES, tn), lambda i: (0, i)),
        out_shape=jax.ShapeDtypeStruct((SUBLANES, n), jnp.int32),
        compiler_params=_params("arbitrary"),
        name="slot_index",
    )(pad_start.reshape(N_EXPERTS, 1), ridx)


def _sc_mesh():
    return plsc.VectorSubcoreMesh(core_axis_name="c", subcore_axis_name="s",
                                  num_cores=SC_CORES, num_subcores=SC_SUBCORES)


def _sc_worker_base(per_worker):
    return (lax.axis_index("s") * SC_CORES + lax.axis_index("c")) * per_worker


def _sc_scatter_rows(rows, idx0, idx1, n_slots):
    n, d = rows.shape
    per_worker = n // SC_WORKERS
    assert per_worker % SC_WINDOW == 0

    n_win = per_worker // SC_WINDOW
    assert n_win % 2 == 0

    def body(rows_hbm, i0_hbm, i1_hbm, out_hbm, i0_a, i1_a, rows_a, i0_b, i1_b, rows_b, sem_a, sem_b):
        base = _sc_worker_base(per_worker)

        def offset(j):
            return pl.multiple_of(base + j * SC_WINDOW, SC_WINDOW)

        def start(j, i0_v, i1_v, rows_v, sem):
            pltpu.sync_copy(i0_hbm.at[pl.ds(offset(j), SC_WINDOW)], i0_v)
            pltpu.sync_copy(i1_hbm.at[pl.ds(offset(j), SC_WINDOW)], i1_v)
            pltpu.async_copy(rows_hbm.at[pl.ds(offset(j), SC_WINDOW)], rows_v, sem)

        def finish(j, i0_v, i1_v, rows_v, sem):
            pltpu.make_async_copy(rows_hbm.at[pl.ds(offset(j), SC_WINDOW)], rows_v, sem).wait()
            pltpu.sync_copy(rows_v, out_hbm.at[i0_v])
            pltpu.sync_copy(rows_v, out_hbm.at[i1_v])

        start(0, i0_a, i1_a, rows_a, sem_a)

        @pl.loop(0, n_win, step=2)
        def _(j):
            start(j + 1, i0_b, i1_b, rows_b, sem_b)
            finish(j, i0_a, i1_a, rows_a, sem_a)

            @pl.when(j + 2 < n_win)
            def _():
                start(j + 2, i0_a, i1_a, rows_a, sem_a)

            finish(j + 1, i0_b, i1_b, rows_b, sem_b)

    window = [pltpu.VMEM((SC_WINDOW,), jnp.int32), pltpu.VMEM((SC_WINDOW,), jnp.int32),
              pltpu.VMEM((SC_WINDOW, d), rows.dtype)]
    return pl.kernel(
        body, out_type=jax.ShapeDtypeStruct((n_slots, d), rows.dtype), mesh=_sc_mesh(),
        scratch_types=window + window + [pltpu.SemaphoreType.DMA, pltpu.SemaphoreType.DMA],
        name="sc_scatter_rows",
    )(rows, idx0, idx1)


def _sc_gather_rows(table, idx):
    n = idx.shape[0]
    d = table.shape[1]
    per_worker = n // SC_WORKERS
    assert per_worker % SC_WINDOW == 0

    n_win = per_worker // SC_WINDOW
    assert n_win % 2 == 0

    def body(table_hbm, idx_hbm, out_hbm, idx_a, idx_b, rows_a, rows_b, sem_a, sem_b):
        base = _sc_worker_base(per_worker)

        def offset(j):
            return pl.multiple_of(base + j * SC_WINDOW, SC_WINDOW)

        def start(j, idx_v, rows_v, sem):
            pltpu.sync_copy(idx_hbm.at[pl.ds(offset(j), SC_WINDOW)], idx_v)
            pltpu.async_copy(table_hbm.at[idx_v], rows_v, sem)

        def finish(j, idx_v, rows_v, sem):
            pltpu.make_async_copy(table_hbm.at[idx_v], rows_v, sem).wait()
            pltpu.sync_copy(rows_v, out_hbm.at[pl.ds(offset(j), SC_WINDOW)])

        start(0, idx_a, rows_a, sem_a)

        @pl.loop(0, n_win, step=2)
        def _(j):
            start(j + 1, idx_b, rows_b, sem_b)
            finish(j, idx_a, rows_a, sem_a)

            @pl.when(j + 2 < n_win)
            def _():
                start(j + 2, idx_a, rows_a, sem_a)

            finish(j + 1, idx_b, rows_b, sem_b)

    return pl.kernel(
        body, out_type=jax.ShapeDtypeStruct((n, d), table.dtype), mesh=_sc_mesh(),
        scratch_types=[pltpu.VMEM((SC_WINDOW,), jnp.int32), pltpu.VMEM((SC_WINDOW,), jnp.int32),
                       pltpu.VMEM((SC_WINDOW, d), table.dtype), pltpu.VMEM((SC_WINDOW, d), table.dtype),
                       pltpu.SemaphoreType.DMA, pltpu.SemaphoreType.DMA],
        name="sc_gather_rows",
    )(table, idx)


def _ffn_kernel(be_ref, nv_ref, xs_ref, wg_ref, wu_ref, wd_ref, ys_ref, wg_b, wu_b, wd_b):
    j = pl.program_id(0)
    valid = nv_ref[j]

    @pl.when((valid > 0) & ((j == 0) | (be_ref[j] != be_ref[jnp.maximum(j - 1, 0)])))
    def _():
        wg_b[...] = wg_ref[0, 0].astype(BF16)
        wu_b[...] = wu_ref[0, 0].astype(BF16)
        wd_b[...] = wd_ref[0, 0].astype(BF16)

    @pl.when(valid > 0)
    def _():
        row = lax.broadcasted_iota(jnp.int32, (SLOT_ROWS, 1), 0)
        xw = jnp.where(row < valid, xs_ref[...], 0)
        xb = _unpack_bf16_pairs(xw).astype(BF16)
        g = jnp.dot(xb, wg_b[...], preferred_element_type=F32)
        u = jnp.dot(xb, wu_b[...], preferred_element_type=F32)
        a = (g / (1.0 + jnp.exp(-g)) * u).astype(BF16)
        ys_ref[...] = _pack_bf16_pairs(jnp.dot(a, wd_b[...], preferred_element_type=F32))

    @pl.when(valid <= 0)
    def _():
        ys_ref[...] = jnp.zeros_like(ys_ref)


def _expert_ffn(block_e, n_valid, xs, w_gate, w_up, w_down, layer):
    n_slots, dw = xs.shape
    d, de = w_gate.shape[2:]
    w_idx = lambda j, be, nv: (layer, be[j], 0, 0)
    return pl.pallas_call(
        _ffn_kernel,
        grid_spec=pltpu.PrefetchScalarGridSpec(
            num_scalar_prefetch=2,
            grid=(n_slots // SLOT_ROWS,),
            in_specs=[pl.BlockSpec((SLOT_ROWS, dw), lambda j, be, nv: (j, 0)),
                      pl.BlockSpec((1, 1, d, de), w_idx),
                      pl.BlockSpec((1, 1, d, de), w_idx),
                      pl.BlockSpec((1, 1, de, d), w_idx)],
            out_specs=pl.BlockSpec((SLOT_ROWS, dw), lambda j, be, nv: (j, 0)),
            scratch_shapes=[pltpu.VMEM((d, de), BF16), pltpu.VMEM((d, de), BF16), pltpu.VMEM((de, d), BF16)]),
        out_shape=jax.ShapeDtypeStruct((n_slots, dw), jnp.int32),
        compiler_params=_params("arbitrary"),
        name="expert_ffn",
    )(block_e, n_valid, xs, w_gate, w_up, w_down)


def _combine_kernel(y0_ref, y1_ref, gcol_ref, x_ref, g2_ref, *rest, pool_in):
    if pool_in:
        sh_ref, sc_ref, n1_ref, wi_ref, x_out, u_out = rest
    else:
        (x_out,) = rest
    g_rows = gcol_ref[...]
    gid = lax.broadcasted_iota(jnp.int32, (LANES, g_rows.shape[1]), 0)
    gc = jnp.where(gid == 0, g_rows[0:1], jnp.where(gid == 1, g_rows[1:2], 0.0)).T
    out = gc[:, 0:1] * _unpack_bf16_pairs(y0_ref[0]) + gc[:, 1:2] * _unpack_bf16_pairs(y1_ref[0])
    x2 = x_ref[0] + g2_ref[0] * out
    x_out[0] = x2
    if pool_in:
        hb = _rms_mod(x2, n1_ref[...], sh_ref[0], sc_ref[0]).astype(BF16)
        u_out[0] = jnp.dot(hb, wi_ref[...], preferred_element_type=F32).astype(BF16)


def _combine(yg, gcol, x, g2, pool_args=None, *, tc):
    b, l, d = x.shape
    nt = l // tc
    pool_in = pool_args is not None
    kern = functools.partial(_combine_kernel, pool_in=pool_in)
    row = lambda bi, i: (bi, 0, 0)
    fix = lambda bi, i: (0, 0)
    tile = lambda bi, i: (bi, i, 0)
    in_specs = [pl.BlockSpec((1, tc, d // 2), lambda bi, i: (0, bi * nt + i, 0)),
                pl.BlockSpec((1, tc, d // 2), lambda bi, i: (1, bi * nt + i, 0)),
                pl.BlockSpec((SUBLANES, tc), lambda bi, i: (0, bi * nt + i)),
                pl.BlockSpec((1, tc, d), tile),
                pl.BlockSpec((1, 1, d), row)]
    out_specs = [pl.BlockSpec((1, tc, d), tile)]
    out_shape = [jax.ShapeDtypeStruct((b, l, d), F32)]
    args = [yg, yg, gcol, x, g2]
    if pool_in:
        in_specs += [pl.BlockSpec((1, 1, d), row), pl.BlockSpec((1, 1, d), row),
                     pl.BlockSpec((1, d), fix), pl.BlockSpec((d, d), fix)]
        out_specs.append(pl.BlockSpec((1, tc, d), tile))
        out_shape.append(jax.ShapeDtypeStruct((b, l, d), BF16))
        args += list(pool_args)
    return pl.pallas_call(
        kern,
        grid=(b, nt),
        in_specs=in_specs, out_specs=out_specs, out_shape=out_shape,
        compiler_params=_params("arbitrary", "arbitrary"),
        name="moe_combine",
    )(*args)


def _moe(h2, ridx, gcol, counts, x1, g2, w_gate, w_up, w_down, layer, pool_args=None):
    b, l, d = x1.shape
    n = b * l
    n_blocks = (n * TOP_K) // SLOT_ROWS + N_EXPERTS
    cnt = counts[:, 0].astype(jnp.int32)
    padded = (cnt + SLOT_ROWS - 1) // SLOT_ROWS * SLOT_ROWS
    earlier = jnp.arange(N_EXPERTS)[None, :] < jnp.arange(N_EXPERTS)[:, None]
    pad_start = jnp.sum(jnp.where(earlier, padded[None, :], 0), axis=1).astype(jnp.int32)
    pad_end = pad_start + padded
    block_start = jnp.arange(n_blocks, dtype=jnp.int32) * SLOT_ROWS
    block_e = jnp.minimum(jnp.sum(pad_end[None, :] <= block_start[:, None], axis=1), N_EXPERTS - 1).astype(jnp.int32)
    own = block_e[:, None] == jnp.arange(N_EXPERTS)[None, :]
    data_end = jnp.sum(jnp.where(own, (pad_start + cnt)[None, :], 0), axis=1)
    n_valid = jnp.clip(data_end - block_start, 0, SLOT_ROWS).astype(jnp.int32)
    dest = _slot_index(pad_start, ridx, tn=2048)
    xs = _sc_scatter_rows(h2.reshape(n, d // 2), dest[0], dest[1], n_blocks * SLOT_ROWS)
    ys = _expert_ffn(block_e, n_valid, xs, w_gate, w_up, w_down, layer)
    yg = _sc_gather_rows(ys, dest[:TOP_K].reshape(TOP_K * n)).reshape(TOP_K, n, d // 2)
    return _combine(yg, gcol, x1, g2, pool_args, tc=512)


def kernel(x, c, ctx, c_ctx, ada_w, ada_b, norm1_g, norm2_g, attn_w_in, attn_w_out, attn_q_gain, attn_k_gain,
           attn_lq1, attn_lk1, attn_lq2, attn_lk2, attn_sub_gain, pool_w_in, pool_w_group, pool_scale, pool_w_out,
           router_w, router_b, moe_w_gate, moe_w_up, moe_w_down):
    b, l, d = x.shape
    n_ctx = ctx.shape[1]
    depth = ada_w.shape[0]
    assert depth == 2 and d == N_HEADS * V_DIM
    tm = 512

    mod = _adaln_mod(c, c_ctx, ada_w, ada_b)
    mods = [[mod[i, :b, None, j * d:(j + 1) * d] for j in range(N_MOD)] for i in range(depth)]
    mod_ctx = [jnp.broadcast_to(mod[0, b, j * d:(j + 1) * d], (b, 1, d)) for j in range(2)]

    rwt = router_w.T
    rwh = rwt.astype(BF16)
    rwl = (rwt - rwh.astype(F32)).astype(BF16)
    rb = router_b.reshape(N_EXPERTS, 1)

    sh1, s1, g1, sh2, s2, g2 = mods[0]
    cos, sin = _rope_tables(l)
    pair_up = (jnp.arange(LANES) & 16) == 0

    def gain_rows(g, factor):
        g2 = jnp.concatenate([g, g]) * factor
        return [g2, jnp.where(pair_up, jnp.roll(g2, -16), jnp.roll(g2, 16))]

    q_scale = HEAD_DIM ** -0.5 * math.log2(math.e)
    q_max = jnp.maximum(jnp.max(jnp.abs(attn_q_gain[0])) * (HEAD_DIM ** 0.5 * q_scale), F32_TINY)
    k_max = jnp.maximum(jnp.max(jnp.abs(attn_k_gain[0])) * HEAD_DIM ** 0.5, F32_TINY)
    need = jnp.ceil(jnp.log2(k_max / F8_MAX))
    room = jnp.floor(jnp.log2(F8_MAX / q_max))
    trade = jnp.exp2(jnp.clip(jnp.clip(0.0, need, jnp.maximum(need, room)), -60.0, 60.0))
    h_max = jnp.float32(0.0)
    for shift_, scale_ in ((sh1, s1), (mod_ctx[0], mod_ctx[1])):
        h_max = jnp.maximum(h_max, jnp.max(d ** 0.5 * jnp.max(jnp.abs(norm1_g[0] * (1.0 + scale_)), axis=-1)
                                           + jnp.sqrt(jnp.sum(shift_ * shift_, axis=-1))))
    w_v = attn_w_in[0][:, 2 * d:]
    v_max = 1.02 * h_max * jnp.sqrt(jnp.max(jnp.sum(w_v * w_v, axis=0)))
    v_grow = jnp.exp2(jnp.clip(jnp.ceil(jnp.log2(jnp.maximum(v_max, F32_TINY) / F8_MAX)), 0.0, 60.0))

    gains = jnp.stack(gain_rows(attn_q_gain[0], q_scale * trade) + gain_rows(attn_k_gain[0], 1.0 / trade))
    w_qk = attn_w_in[0][:, :2 * d].astype(BF16)
    wv_t = (w_v / v_grow).T.astype(BF16)
    assert l % tm == 0 and l % n_ctx == 0
    q, k_all, vt_all = _qkv_proj(x, sh1, s1, norm1_g[0][None], w_qk, wv_t, gains, cos, sin, n_qk=2, rope=True,
                                 tm=tm, n_keys=l + n_ctx, key_row0=0)
    k_all, vt_all = _qkv_proj(ctx, mod_ctx[0], mod_ctx[1], norm1_g[0][None], w_qk[:, d:], wv_t, gains,
                              cos[:n_ctx], sin[:n_ctx], n_qk=1, rope=False,
                              tm=n_ctx, n_keys=l + n_ctx, key_row0=l, kv=(k_all, vt_all))
    lam_init = 0.8 - 0.6 * math.exp(-0.3 * 0)
    lam_params = jnp.stack([attn_lq1[0], attn_lk1[0], attn_lq2[0], attn_lk2[0],
                            jnp.full((HEAD_DIM,), v_grow, F32)])
    o = _diff_attention(q, k_all, vt_all, lam_params, attn_sub_gain[0][:, None], lam_init=lam_init)

    fix = lambda bi, i: (0, 0)
    x1, h2, ridx, gcol, counts = _mixer_tail(
        (o, attn_w_out[0].astype(BF16)),
        (pl.BlockSpec((1, 2 * TAIL_ROWS, d), lambda bi, i: (bi, i, 0)), pl.BlockSpec((d, d), fix)),
        _attn_tail_kernel, x, g1, sh2, s2, norm2_g[0][None], rwh, rwl, rb, tm=2 * TAIL_ROWS)

    sh1b, s1b, g1b, sh2b, s2b, g2b = mods[1]
    x2, u = _moe(h2, ridx, gcol, counts, x1, g2, moe_w_gate, moe_w_up, moe_w_down, 0,
                 pool_args=(sh1b, s1b, norm1_g[1][None], pool_w_in[0].astype(BF16)))
    gd = pool_w_group.shape[2]
    tp = 2 * TAIL_ROWS
    nh = tp // HALO_ROWS
    front_specs = (
        pl.BlockSpec((1, tp, d), lambda bi, i: (bi, i, 0)),
        pl.BlockSpec((1, HALO_ROWS, d), lambda bi, i: (bi, jnp.maximum(i * nh - 1, 0), 0)),
        pl.BlockSpec((1, HALO_ROWS, d), lambda bi, i: (bi, jnp.minimum((i + 1) * nh, l // HALO_ROWS - 1), 0)),
        pl.BlockSpec((len(POOL_WINDOWS), gd, gd), lambda bi, i: (0, 0, 0)),
        pl.BlockSpec((1, d), fix),
        pl.BlockSpec((d, d), fix))
    x3, h2b, ridx_b, gcol_b, counts_b = _mixer_tail(
        (u, u, u, pool_w_group[0].astype(BF16), pool_scale[0][None], pool_w_out[0].astype(BF16)),
        front_specs, functools.partial(_pool_tail_kernel, seq_len=l),
        x2, g1b, sh2b, s2b, norm2_g[1][None], rwh, rwl, rb, tm=tp,
        scratch=[pltpu.VMEM((tp + 2 * POOL_HALO, d), F32)] * 2)
    (out,) = _moe(h2b, ridx_b, gcol_b, counts_b, x3, g2b, moe_w_gate, moe_w_up, moe_w_down, 1)
    return out
```

```python
import functools
import math

import jax
import jax.numpy as jnp
import numpy as np
from jax import lax
from jax.experimental import pallas as pl
from jax.experimental.pallas import tpu as pltpu
from jax.experimental.pallas import tpu_sc as plsc

F32 = jnp.float32
BF16 = jnp.bfloat16
F8 = jnp.float8_e4m3fn
F8_MAX = float(jnp.finfo(F8).max)
F32_TINY = float(jnp.finfo(F32).tiny)

LANES = 128
SUBLANES = 8
N_HEADS = 8
HEAD_DIM = 64
V_DIM = 2 * HEAD_DIM
V_ROWS = V_DIM + 32
P_SHIFT = 8.0
GRID_W = 64
ROPE_THETA = 10000.0
NORM_EPS = 1e-6
N_MOD = 6
POOL_WINDOWS = (2, 4, 8, 16)
POOL_HALO = max(POOL_WINDOWS) // 2
HALO_ROWS = 16
N_EXPERTS = 32
N_EXPERT_GROUPS = 4
EXPERTS_PER_GROUP = N_EXPERTS // N_EXPERT_GROUPS
TOP_K = 2
SLOT_ROWS = 512
TAIL_ROWS = 512
SC_CORES = 2
SC_SUBCORES = 16
SC_WORKERS = SC_CORES * SC_SUBCORES
SC_WINDOW = 64
ATTN_GROUP_CHUNKS = 1
ATTN_SCORE_AHEAD = 1
VMEM_LIMIT = 48 * 1024 * 1024
NT_DIMS = (((1,), (1,)), ((), ()))


def _params(*sem):
    return pltpu.CompilerParams(dimension_semantics=sem, vmem_limit_bytes=VMEM_LIMIT)


def _rms_mod(x, gain, shift, scale):
    inv_rms = lax.rsqrt(jnp.mean(x * x, axis=-1, keepdims=True) + NORM_EPS)
    return x * inv_rms * (gain * (1.0 + scale)) + shift


def _pack_bf16_pairs(x):
    c = x.shape[1] // 2
    hi = lax.bitcast_convert_type(x[:, :c].astype(BF16).astype(F32), jnp.uint32)
    lo = lax.bitcast_convert_type(x[:, c:].astype(BF16).astype(F32), jnp.uint32)
    return lax.bitcast_convert_type(hi | (lo >> 16), jnp.int32)


def _unpack_bf16_pairs(w):
    u = lax.bitcast_convert_type(w, jnp.uint32)
    hi = lax.bitcast_convert_type(u & jnp.uint32(0xFFFF0000), F32)
    lo = lax.bitcast_convert_type(u << 16, F32)
    return jnp.concatenate([hi, lo], axis=1)


def _mod_kernel(c_ref, w_ref, b_ref, o_ref):
    c = c_ref[...]
    a = c / (1.0 + jnp.exp(-c))
    o_ref[0] = jnp.dot(a, w_ref[0], precision=lax.Precision.HIGHEST,
                       preferred_element_type=F32) + b_ref[0]


def _adaln_mod(c, c_ctx, ada_w, ada_b):
    depth, d, n_out = ada_w.shape
    b = c.shape[0]
    assert b + 1 <= SUBLANES
    rows = jnp.concatenate([c, c_ctx[None], jnp.zeros((SUBLANES - b - 1, d), F32)], axis=0)
    tn = n_out // 4
    return pl.pallas_call(
        _mod_kernel,
        grid=(depth, n_out // tn),
        in_specs=[pl.BlockSpec((SUBLANES, d), lambda i, j: (0, 0)),
                  pl.BlockSpec((1, d, tn), lambda i, j: (i, 0, j)),
                  pl.BlockSpec((1, 1, tn), lambda i, j: (i, 0, j))],
        out_specs=pl.BlockSpec((1, SUBLANES, tn), lambda i, j: (i, 0, j)),
        out_shape=jax.ShapeDtypeStruct((depth, SUBLANES, n_out), F32),
        compiler_params=_params("arbitrary", "arbitrary"),
        name="adaln_mod",
    )(rows, ada_w, ada_b.reshape(depth, 1, n_out))


def _qkv_kernel(x_ref, sh_ref, sc_ref, g_ref, w_ref, wvt_ref, gains_ref, cos_ref, sin_ref, *refs, n_qk, rope):
    out_refs = refs[-(n_qk + 1):]
    tm, d = x_ref.shape[1:]
    hb = _rms_mod(x_ref[0], g_ref[...], sh_ref[0], sc_ref[0]).astype(BF16)
    lane_b4 = (lax.broadcasted_iota(jnp.int32, (1, LANES), 1) & 16) == 0
    chunk_r = lax.broadcasted_iota(jnp.int32, (2 * LANES, 2 * LANES), 0) // HEAD_DIM
    chunk_c = lax.broadcasted_iota(jnp.int32, (2 * LANES, 2 * LANES), 1) // HEAD_DIM
    same_chunk = (chunk_r == chunk_c).astype(BF16)
    gains = gains_ref[...]
    tables = []
    for t in range(n_qk):
        r = 2 * (t + 2 - n_qk)
        tables.append((cos_ref[...] * gains[r:r + 1], sin_ref[...] * gains[r + 1:r + 2]) if rope
                      else (gains[r:r + 1], None))

    def project(t, j):
        return jnp.dot(hb, w_ref[:, t * d + j:t * d + j + 2 * LANES], preferred_element_type=F32)

    def chunk_sums(acc):
        return jnp.dot((acc * acc).astype(BF16), same_chunk, preferred_element_type=F32)

    def finish(t, j, acc, ssq):
        cos_t, sin_t = tables[t]
        nrm = acc * lax.rsqrt(ssq * (1.0 / HEAD_DIM) + NORM_EPS)
        for half in range(2):
            blk = nrm[:, half * LANES:(half + 1) * LANES]
            if rope:
                rot = jnp.where(lane_b4, pltpu.roll(blk, LANES - 16, 1), pltpu.roll(blk, 16, 1))
                y = blk * cos_t + rot * sin_t
            else:
                y = blk * cos_t
            c0 = j + half * LANES
            if t < n_qk - 1:
                yt = y.T
                first = lax.broadcasted_iota(jnp.int32, (V_DIM, 1), 0) < HEAD_DIM
                out_refs[t][0, c0 // LANES, 0, :, 0:tm] = jnp.where(first, yt, 0.0).astype(F8)
                out_refs[t][0, c0 // LANES, 0, :, tm:2 * tm] = jnp.where(first, 0.0, yt).astype(F8)
            else:
                out_refs[t][0, :, c0:c0 + LANES] = y.astype(F8)

    blocks = [(t, j) for t in range(n_qk) for j in range(0, d, 2 * LANES)]
    accs, sums = {}, {}
    for i in range(-2, len(blocks)):
        if i + 2 < len(blocks):
            accs[i + 2] = project(*blocks[i + 2])
        if 0 <= i + 1 < len(blocks):
            sums[i + 1] = chunk_sums(accs[i + 1])
        if i >= 0:
            finish(*blocks[i], accs.pop(i), sums.pop(i))
    vt_ref = out_refs[n_qk]
    ones = jnp.where(lax.broadcasted_iota(jnp.int32, (V_ROWS - V_DIM, tm), 0) == 0, 1.0, 0.0).astype(F8)
    for j in range(0, d, 2 * LANES):
        acc_t = lax.dot_general(wvt_ref[j:j + 2 * LANES, :], hb, NT_DIMS, preferred_element_type=F32)
        for half in range(2):
            h = j // LANES + half
            vt_ref[0, h, 0:V_DIM, :] = acc_t[half * V_DIM:(half + 1) * V_DIM].astype(F8)
            vt_ref[0, h, V_DIM:V_ROWS, :] = ones


def _qkv_proj(x, shift, scale, gain, w, wv_t, gains, cos, sin, *, n_qk, rope, tm, n_keys, key_row0, kv=None):
    b, l, d = x.shape
    n_out = n_qk + 1
    kern = functools.partial(_qkv_kernel, n_qk=n_qk, rope=rope)
    row = lambda bi, i: (bi, 0, 0)
    fix = lambda bi, i: (0, 0)
    kb = key_row0 // tm
    tile = pl.BlockSpec((1, tm, d), lambda bi, i: (bi, i, 0))
    k_spec = pl.BlockSpec((1, tm, d), lambda bi, i: (bi, kb + i, 0))
    vt_spec = pl.BlockSpec((1, N_HEADS, V_ROWS, tm), lambda bi, i: (bi, 0, 0, kb + i))
    q_spec = pl.BlockSpec((1, N_HEADS, 1, V_DIM, 2 * tm), lambda bi, i: (bi, 0, i, 0, 0))
    q_shape = jax.ShapeDtypeStruct((b, N_HEADS, l // tm, V_DIM, 2 * tm), F8)
    k_shape = jax.ShapeDtypeStruct((b, n_keys, d), F8)
    vt_shape = jax.ShapeDtypeStruct((b, N_HEADS, V_ROWS, n_keys), F8)
    in_specs = [tile,
                pl.BlockSpec((1, 1, d), row), pl.BlockSpec((1, 1, d), row),
                pl.BlockSpec((1, d), fix),
                pl.BlockSpec((d, n_qk * d), fix),
                pl.BlockSpec((d, d), fix),
                pl.BlockSpec((4, LANES), fix),
                pl.BlockSpec((tm, LANES), lambda bi, i: (i, 0)),
                pl.BlockSpec((tm, LANES), lambda bi, i: (i, 0))]
    args = [x, shift, scale, gain, w, wv_t, gains, cos, sin]
    aliases = {}
    if kv is not None:
        aliases = {len(args): n_qk - 1, len(args) + 1: n_qk}
        in_specs += [pl.BlockSpec(memory_space=pl.ANY)] * 2
        args += list(kv)
    return pl.pallas_call(
        kern,
        grid=(b, l // tm),
        in_specs=in_specs,
        out_specs=[q_spec] * (n_qk - 1) + [k_spec, vt_spec],
        out_shape=[q_shape] * (n_qk - 1) + [k_shape, vt_shape],
        input_output_aliases=aliases,
        compiler_params=_params("arbitrary", "arbitrary"),
        name="qkv_proj",
    )(*args)


def _rope_tables(n_tokens):
    rows = n_tokens // GRID_W
    row = np.repeat(np.arange(rows, dtype=np.float32), GRID_W)
    col = np.tile(np.arange(GRID_W, dtype=np.float32), rows)
    half = HEAD_DIM // 2
    inv_freq = (np.float32(ROPE_THETA) ** (-np.arange(0, half, 2, dtype=np.float32) / half)).astype(np.float32)
    ang_r = row[:, None] * inv_freq
    ang_c = col[:, None] * inv_freq
    ang = np.concatenate([ang_r, ang_r, ang_c, ang_c] * 2, axis=-1)
    sign = np.where((np.arange(LANES) & 16) == 0, -1.0, 1.0).astype(np.float32)
    return jnp.asarray(np.cos(ang), F32), jnp.asarray(np.sin(ang) * sign, F32)


def _attn_kernel(q_ref, k_ref, vt_ref, lp_ref, sg_ref, o_ref, s_ref, *, tk, group, ahead, lam_init):
    tq = q_ref.shape[4] // 2
    n_chunks = k_ref.shape[1] // tk
    qz = q_ref[0, 0, 0]
    n_slots = (ahead + 1) * group

    def score_chunk(c, m_grp):
        st = jnp.dot(k_ref[0, c * tk:(c + 1) * tk, :], qz, preferred_element_type=F32).astype(BF16)
        slot = c % n_slots
        s_ref[slot * tk:(slot + 1) * tk, :] = st
        mc = jnp.max(st, axis=0, keepdims=True)
        return mc if m_grp is None else jnp.maximum(m_grp, mc)

    def value_chunk(c, m_ref, part):
        slot = c % n_slots
        p = jnp.exp2(s_ref[slot * tk:(slot + 1) * tk, :] - (m_ref - P_SHIFT)).astype(F8)
        pv = jnp.dot(vt_ref[0, 0, :, c * tk:(c + 1) * tk], p, preferred_element_type=F32)
        return pv if part is None else part + pv

    groups = [list(range(g0, min(g0 + group, n_chunks))) for g0 in range(0, n_chunks, group)]
    m_of = {}
    for g in range(min(ahead, len(groups))):
        for c in groups[g]:
            m_of[g] = score_chunk(c, m_of.get(g))
    m = None
    acc = None
    for gi, cur in enumerate(groups):
        nxt = groups[gi + ahead] if gi + ahead < len(groups) else []
        m_new = m_of[gi] if m is None else jnp.maximum(m, m_of[gi])
        part = None
        for i in range(max(len(cur), len(nxt))):
            if i < len(nxt):
                m_of[gi + ahead] = score_chunk(nxt[i], m_of.get(gi + ahead))
            if i < len(cur):
                part = value_chunk(cur[i], m_new, part)
        acc = part if acc is None else acc * jnp.exp2(m.astype(F32) - m_new.astype(F32)) + part
        m = m_new
    acc = acc[:V_DIM] / acc[V_DIM:V_DIM + 1]
    lp = lp_ref[...]
    lam = (jnp.exp(jnp.sum(lp[0:1] * lp[1:2], axis=-1, keepdims=True))
           - jnp.exp(jnp.sum(lp[2:3] * lp[3:4], axis=-1, keepdims=True)) + lam_init)
    o = (acc[:, :tq] - lam * acc[:, tq:]) * lp[4:5, 0:1]
    o = o * lax.rsqrt(jnp.mean(o * o, axis=0, keepdims=True) + NORM_EPS) * sg_ref[...] * (1.0 - lam_init)
    o_ref[0] = o.T.astype(BF16)


def _attn_chunk(n_keys):
    for tk in (768, 512, 256, 128):
        if n_keys % tk == 0:
            return tk
    raise ValueError(f"key count {n_keys} is not a multiple of {LANES}")


def _diff_attention(qz, k_all, vt_all, lam_params, sub_gain, *, lam_init):
    b, _, n_tiles, _, tq2 = qz.shape
    tq = tq2 // 2
    l, d = n_tiles * tq, N_HEADS * V_DIM
    n_keys = k_all.shape[1]
    tk = _attn_chunk(n_keys)
    kern = functools.partial(_attn_kernel, tk=tk, group=ATTN_GROUP_CHUNKS, ahead=ATTN_SCORE_AHEAD, lam_init=lam_init)
    return pl.pallas_call(
        kern,
        grid=(b, N_HEADS, l // tq),
        in_specs=[pl.BlockSpec((1, 1, 1, V_DIM, 2 * tq), lambda bi, h, i: (bi, h, i, 0, 0)),
                  pl.BlockSpec((1, n_keys, V_DIM), lambda bi, h, i: (bi, 0, h)),
                  pl.BlockSpec((1, 1, V_ROWS, n_keys), lambda bi, h, i: (bi, h, 0, 0)),
                  pl.BlockSpec((5, HEAD_DIM), lambda bi, h, i: (0, 0)),
                  pl.BlockSpec((V_DIM, 1), lambda bi, h, i: (0, 0))],
        out_specs=pl.BlockSpec((1, tq, V_DIM), lambda bi, h, i: (bi, i, h)),
        out_shape=jax.ShapeDtypeStruct((b, l, d), BF16),
        scratch_shapes=[pltpu.VMEM(((ATTN_SCORE_AHEAD + 1) * ATTN_GROUP_CHUNKS * tk, 2 * tq), BF16)],
        compiler_params=_params("arbitrary", "arbitrary", "arbitrary"),
        name="diff_attention",
    )(qz, k_all, vt_all, lam_params, sub_gain)


def _route(h2, row0, rwh_ref, rwl_ref, rb_ref, before_ref, carry_ref, ridx_ref, gcol_ref, cnt_ref, is_first):
    tm = h2.shape[0]
    hh = h2.astype(BF16)
    hl = (h2 - hh.astype(F32)).astype(BF16)
    rw2 = jnp.concatenate([rwh_ref[...], rwl_ref[...]], axis=0)
    part = lax.dot_general(rw2, hh, NT_DIMS, preferred_element_type=F32)
    logits = (part[:N_EXPERTS] + part[N_EXPERTS:]
              + lax.dot_general(rwh_ref[...], hl, NT_DIMS, preferred_element_type=F32) + rb_ref[...])
    groups = [logits[g * EXPERTS_PER_GROUP:(g + 1) * EXPERTS_PER_GROUP] for g in range(N_EXPERT_GROUPS)]
    top = groups[0]
    for g in range(1, N_EXPERT_GROUPS):
        top = jnp.maximum(top, groups[g])
    top = jnp.max(top, axis=0, keepdims=True)
    sub = lax.broadcasted_iota(jnp.int32, (EXPERTS_PER_GROUP, tm), 0)
    best = None
    for g in range(N_EXPERT_GROUPS):
        ex = jnp.exp(groups[g] - top)
        v1 = jnp.max(ex, axis=0, keepdims=True)
        i1 = jnp.min(jnp.where(ex == v1, sub, EXPERTS_PER_GROUP), axis=0, keepdims=True)
        rest = jnp.where(sub == i1, -1.0, ex)
        v2 = jnp.max(rest, axis=0, keepdims=True)
        i2 = jnp.min(jnp.where(rest == v2, sub, EXPERTS_PER_GROUP), axis=0, keepdims=True)
        cand = (v1 + v2, v1, v2, i1 + g * EXPERTS_PER_GROUP, i2 + g * EXPERTS_PER_GROUP)
        if best is None:
            best = cand
        else:
            better = cand[0] > best[0]
            best = tuple(jnp.where(better, new, old) for new, old in zip(cand, best))
    _, v1, v2, e0, e1 = best
    gate0 = v1 / (v1 + v2)
    gate1 = v2 / (v1 + v2)

    @pl.when(is_first)
    def _():
        carry_ref[...] = jnp.zeros_like(carry_ref)

    erow = lax.broadcasted_iota(jnp.int32, (N_EXPERTS, tm), 0)
    oh0 = erow == e0
    oh1 = erow == e1
    chosen = jnp.where(oh0 | oh1, 1.0, 0.0)
    prior = carry_ref[:, 0:1] + jnp.dot(chosen.astype(BF16), before_ref[...], preferred_element_type=F32)
    r0 = jnp.sum(jnp.where(oh0, prior, 0.0), axis=0, keepdims=True).astype(jnp.int32)
    r1 = jnp.sum(jnp.where(oh1, prior, 0.0), axis=0, keepdims=True).astype(jnp.int32)
    carry_ref[...] = carry_ref[...] + jnp.sum(chosen, axis=1, keepdims=True)
    cnt_ref[...] = carry_ref[...]
    rid = lax.broadcasted_iota(jnp.int32, (SUBLANES, tm), 0)
    ridx_ref[:, row0:row0 + tm] = jnp.where(rid == 0, e0, jnp.where(rid == 1, e1, jnp.where(rid == 2, r0, jnp.where(rid == 3, r1, 0))))
    gid = lax.broadcasted_iota(jnp.int32, (LANES, tm), 0)
    gcol_ref[row0:row0 + tm, :] = jnp.where(gid == 0, gate0, jnp.where(gid == 1, gate1, 0.0)).T


def _tail(y, row0, x_ref, g1_ref, sh2_ref, s2_ref, n2_ref, rwh_ref, rwl_ref, rb_ref, before_ref,
          x_out, h2_out, ridx_ref, gcol_ref, cnt_ref, carry_ref):
    rows = slice(row0, row0 + y.shape[0])
    x1 = x_ref[0, rows] + g1_ref[0] * y
    x_out[0, rows] = x1
    h2 = _rms_mod(x1, n2_ref[...], sh2_ref[0], s2_ref[0])
    h2_out[0, rows] = _pack_bf16_pairs(h2)
    is_first = (pl.program_id(0) == 0) & (pl.program_id(1) == 0) & (row0 == 0)
    _route(h2, row0, rwh_ref, rwl_ref, rb_ref, before_ref, carry_ref, ridx_ref, gcol_ref, cnt_ref, is_first)


def _attn_tail_kernel(a_ref, wo_ref, *rest):
    starts = range(0, a_ref.shape[1], TAIL_ROWS)
    ys = [jnp.dot(a_ref[0, r0:r0 + TAIL_ROWS], wo_ref[...], preferred_element_type=F32) for r0 in starts]
    for r0, y in zip(starts, ys):
        _tail(y, r0, *rest)


def _pool_tail_kernel(u_ref, up_ref, un_ref, wg_ref, cs_ref, wo_ref, *rest, seq_len):
    *tail_refs, ubuf, abuf = rest
    tm = u_ref.shape[1]
    i = pl.program_id(1)
    u = u_ref[0].astype(F32)
    ubuf[0:POOL_HALO] = jnp.where(i > 0, up_ref[0].astype(F32)[HALO_ROWS - POOL_HALO:], 0.0)
    ubuf[POOL_HALO:POOL_HALO + tm] = u
    ubuf[POOL_HALO + tm:2 * POOL_HALO + tm] = jnp.where(i < pl.num_programs(1) - 1,
                                                        un_ref[0].astype(F32)[:POOL_HALO], 0.0)
    pos = i * tm + lax.broadcasted_iota(jnp.int32, (tm, 1), 0)
    gd = wg_ref.shape[1]
    assert all(win == 2 ** (g + 1) for g, win in enumerate(POOL_WINDOWS))
    n_ext = tm + 2 * POOL_HALO
    bufs = (ubuf, abuf)
    abuf[1:n_ext, :] = ubuf[0:n_ext - 1, :] + ubuf[1:n_ext, :]
    lo, hi = 1, n_ext
    for g in range(1, len(POOL_WINDOWS)):
        src, dst = bufs[g % 2], bufs[(g + 1) % 2]
        sh = POOL_WINDOWS[g] // 4
        dst[lo + sh:hi - sh, g * gd:] = src[lo:hi - 2 * sh, g * gd:] + src[lo + 2 * sh:hi, g * gd:]
        lo, hi = lo + sh, hi - sh
    outs = []
    for g, win in enumerate(POOL_WINDOWS):
        half = win // 2
        cols = slice(g * gd, (g + 1) * gd)
        s = bufs[(g + 1) % 2][POOL_HALO:POOL_HALO + tm, cols]
        inv_cnt = 1.0 / (jnp.minimum(pos + half, seq_len) - jnp.maximum(pos - half, 0)).astype(F32)
        dlt = (s * inv_cnt - u[:, cols]).astype(BF16)
        outs.append(jnp.dot(dlt, wg_ref[g], preferred_element_type=F32))
    z = (jnp.concatenate(outs, axis=-1) * cs_ref[...]).astype(BF16)
    starts = range(0, tm, TAIL_ROWS)
    ys = [jnp.dot(z[r0:r0 + TAIL_ROWS], wo_ref[...], preferred_element_type=F32) for r0 in starts]
    for r0, y in zip(starts, ys):
        _tail(y, r0, *tail_refs)


def _mixer_tail(front_args, front_specs, kern, x, g1, sh2, s2, n2g, rwh, rwl, rb, *, tm, scratch=()):
    b, l, d = x.shape
    nt = l // tm
    n = b * l
    row = lambda bi, i: (bi, 0, 0)
    fix = lambda bi, i: (0, 0)
    tile = lambda bi, i: (bi, i, 0)
    in_specs = list(front_specs) + [
        pl.BlockSpec((1, tm, d), tile),
        pl.BlockSpec((1, 1, d), row), pl.BlockSpec((1, 1, d), row), pl.BlockSpec((1, 1, d), row),
        pl.BlockSpec((1, d), fix),
        pl.BlockSpec((N_EXPERTS, d), fix), pl.BlockSpec((N_EXPERTS, d), fix),
        pl.BlockSpec((N_EXPERTS, 1), fix),
        pl.BlockSpec((TAIL_ROWS, TAIL_ROWS), fix)]
    before = jnp.asarray(np.triu(np.ones((TAIL_ROWS, TAIL_ROWS), np.float32), k=1), BF16)
    out_specs = [pl.BlockSpec((1, tm, d), tile), pl.BlockSpec((1, tm, d // 2), tile),
                 pl.BlockSpec((SUBLANES, tm), lambda bi, i: (0, bi * nt + i)),
                 pl.BlockSpec((tm, LANES), lambda bi, i: (bi * nt + i, 0)),
                 pl.BlockSpec((N_EXPERTS, LANES), fix)]
    out_shape = [jax.ShapeDtypeStruct((b, l, d), F32), jax.ShapeDtypeStruct((b, l, d // 2), jnp.int32),
                 jax.ShapeDtypeStruct((SUBLANES, n), jnp.int32), jax.ShapeDtypeStruct((n, LANES), F32),
                 jax.ShapeDtypeStruct((N_EXPERTS, LANES), F32)]
    return pl.pallas_call(
        kern,
        grid=(b, nt),
        in_specs=in_specs, out_specs=out_specs, out_shape=out_shape,
        scratch_shapes=[pltpu.VMEM((N_EXPERTS, LANES), F32)] + list(scratch),
        compiler_params=_params("arbitrary", "arbitrary"),
        name="mixer_tail",
    )(*front_args, x, g1, sh2, s2, n2g, rwh, rwl, rb, before)


def _slot_kernel(ps_ref, ridx_ref, dest_ref):
    ridx = ridx_ref[...]
    ps = ps_ref[...]
    erow = lax.broadcasted_iota(jnp.int32, (N_EXPERTS, ridx.shape[1]), 0)
    rows = []
    for k in range(TOP_K):
        start = jnp.sum(jnp.where(erow == ridx[k:k + 1], ps, 0), axis=0, keepdims=True)
        rows.append(start + ridx[TOP_K + k:TOP_K + k + 1])
    rid = lax.broadcasted_iota(jnp.int32, ridx.shape, 0)
    dest_ref[...] = jnp.where(rid == 0, rows[0], jnp.where(rid == 1, rows[1], 0))


def _slot_index(pad_start, ridx, *, tn):
    n = ridx.shape[1]
    return pl.pallas_call(
        _slot_kernel,
        grid=(n // tn,),
        in_specs=[pl.BlockSpec((N_EXPERTS, 1), lambda i: (0, 0)),
                  pl.BlockSpec((SUBLANES, tn), lambda i: (0, i))],
        out_specs=pl.BlockSpec((SUBLANES, tn), lambda i: (0, i)),
        out_shape=jax.ShapeDtypeStruct((SUBLANES, n), jnp.int32),
        compiler_params=_params("arbitrary"),
        name="slot_index",
    )(pad_start.reshape(N_EXPERTS, 1), ridx)


def _sc_mesh():
    return plsc.VectorSubcoreMesh(core_axis_name="c", subcore_axis_name="s",
                                  num_cores=SC_CORES, num_subcores=SC_SUBCORES)


def _sc_worker_base(per_worker):
    return (lax.axis_index("s") * SC_CORES + lax.axis_index("c")) * per_worker


def _sc_scatter_rows(rows, idx0, idx1, n_slots):
    n, d = rows.shape
    per_worker = n // SC_WORKERS
    assert per_worker % SC_WINDOW == 0

    n_win = per_worker // SC_WINDOW
    assert n_win % 2 == 0

    def body(rows_hbm, i0_hbm, i1_hbm, out_hbm, i0_a, i1_a, rows_a, i0_b, i1_b, rows_b, sem_a, sem_b):
        base = _sc_worker_base(per_worker)

        def offset(j):
            return pl.multiple_of(base + j * SC_WINDOW, SC_WINDOW)

        def start(j, i0_v, i1_v, rows_v, sem):
            pltpu.sync_copy(i0_hbm.at[pl.ds(offset(j), SC_WINDOW)], i0_v)
            pltpu.sync_copy(i1_hbm.at[pl.ds(offset(j), SC_WINDOW)], i1_v)
            pltpu.async_copy(rows_hbm.at[pl.ds(offset(j), SC_WINDOW)], rows_v, sem)

        def finish(j, i0_v, i1_v, rows_v, sem):
            pltpu.make_async_copy(rows_hbm.at[pl.ds(offset(j), SC_WINDOW)], rows_v, sem).wait()
            pltpu.sync_copy(rows_v, out_hbm.at[i0_v])
            pltpu.sync_copy(rows_v, out_hbm.at[i1_v])

        start(0, i0_a, i1_a, rows_a, sem_a)

        @pl.loop(0, n_win, step=2)
        def _(j):
            start(j + 1, i0_b, i1_b, rows_b, sem_b)
            finish(j, i0_a, i1_a, rows_a, sem_a)

            @pl.when(j + 2 < n_win)
            def _():
                start(j + 2, i0_a, i1_a, rows_a, sem_a)

            finish(j + 1, i0_b, i1_b, rows_b, sem_b)

    window = [pltpu.VMEM((SC_WINDOW,), jnp.int32), pltpu.VMEM((SC_WINDOW,), jnp.int32),
              pltpu.VMEM((SC_WINDOW, d), rows.dtype)]
    return pl.kernel(
        body, out_type=jax.ShapeDtypeStruct((n_slots, d), rows.dtype), mesh=_sc_mesh(),
        scratch_types=window + window + [pltpu.SemaphoreType.DMA, pltpu.SemaphoreType.DMA],
        name="sc_scatter_rows",
    )(rows, idx0, idx1)


def _sc_gather_rows(table, idx):
    n = idx.shape[0]
    d = table.shape[1]
    per_worker = n // SC_WORKERS
    assert per_worker % SC_WINDOW == 0

    n_win = per_worker // SC_WINDOW
    assert n_win % 2 == 0

    def body(table_hbm, idx_hbm, out_hbm, idx_a, idx_b, rows_a, rows_b, sem_a, sem_b):
        base = _sc_worker_base(per_worker)

        def offset(j):
            return pl.multiple_of(base + j * SC_WINDOW, SC_WINDOW)

        def start(j, idx_v, rows_v, sem):
            pltpu.sync_copy(idx_hbm.at[pl.ds(offset(j), SC_WINDOW)], idx_v)
            pltpu.async_copy(table_hbm.at[idx_v], rows_v, sem)

        def finish(j, idx_v, rows_v, sem):
            pltpu.make_async_copy(table_hbm.at[idx_v], rows_v, sem).wait()
            pltpu.sync_copy(rows_v, out_hbm.at[pl.ds(offset(j), SC_WINDOW)])

        start(0, idx_a, rows_a, sem_a)

        @pl.loop(0, n_win, step=2)
        def _(j):
            start(j + 1, idx_b, rows_b, sem_b)
            finish(j, idx_a, rows_a, sem_a)

            @pl.when(j + 2 < n_win)
            def _():
                start(j + 2, idx_a, rows_a, sem_a)

            finish(j + 1, idx_b, rows_b, sem_b)

    return pl.kernel(
        body, out_type=jax.ShapeDtypeStruct((n, d), table.dtype), mesh=_sc_mesh(),
        scratch_types=[pltpu.VMEM((SC_WINDOW,), jnp.int32), pltpu.VMEM((SC_WINDOW,), jnp.int32),
                       pltpu.VMEM((SC_WINDOW, d), table.dtype), pltpu.VMEM((SC_WINDOW, d), table.dtype),
                       pltpu.SemaphoreType.DMA, pltpu.SemaphoreType.DMA],
        name="sc_gather_rows",
    )(table, idx)


def _ffn_kernel(be_ref, nv_ref, xs_ref, wg_ref, wu_ref, wd_ref, ys_ref, wg_b, wu_b, wd_b):
    j = pl.program_id(0)
    valid = nv_ref[j]

    @pl.when((valid > 0) & ((j == 0) | (be_ref[j] != be_ref[jnp.maximum(j - 1, 0)])))
    def _():
        wg_b[...] = wg_ref[0, 0].astype(BF16)
        wu_b[...] = wu_ref[0, 0].astype(BF16)
        wd_b[...] = wd_ref[0, 0].astype(BF16)

    @pl.when(valid > 0)
    def _():
        row = lax.broadcasted_iota(jnp.int32, (SLOT_ROWS, 1), 0)
        xw = jnp.where(row < valid, xs_ref[...], 0)
        xb = _unpack_bf16_pairs(xw).astype(BF16)
        g = jnp.dot(xb, wg_b[...], preferred_element_type=F32)
        u = jnp.dot(xb, wu_b[...], preferred_element_type=F32)
        a = (g / (1.0 + jnp.exp(-g)) * u).astype(BF16)
        ys_ref[...] = _pack_bf16_pairs(jnp.dot(a, wd_b[...], preferred_element_type=F32))

    @pl.when(valid <= 0)
    def _():
        ys_ref[...] = jnp.zeros_like(ys_ref)


def _expert_ffn(block_e, n_valid, xs, w_gate, w_up, w_down, layer):
    n_slots, dw = xs.shape
    d, de = w_gate.shape[2:]
    w_idx = lambda j, be, nv: (layer, be[j], 0, 0)
    return pl.pallas_call(
        _ffn_kernel,
        grid_spec=pltpu.PrefetchScalarGridSpec(
            num_scalar_prefetch=2,
            grid=(n_slots // SLOT_ROWS,),
            in_specs=[pl.BlockSpec((SLOT_ROWS, dw), lambda j, be, nv: (j, 0)),
                      pl.BlockSpec((1, 1, d, de), w_idx),
                      pl.BlockSpec((1, 1, d, de), w_idx),
                      pl.BlockSpec((1, 1, de, d), w_idx)],
            out_specs=pl.BlockSpec((SLOT_ROWS, dw), lambda j, be, nv: (j, 0)),
            scratch_shapes=[pltpu.VMEM((d, de), BF16), pltpu.VMEM((d, de), BF16), pltpu.VMEM((de, d), BF16)]),
        out_shape=jax.ShapeDtypeStruct((n_slots, dw), jnp.int32),
        compiler_params=_params("arbitrary"),
        name="expert_ffn",
    )(block_e, n_valid, xs, w_gate, w_up, w_down)


def _combine_kernel(y0_ref, y1_ref, gcol_ref, x_ref, g2_ref, *rest, pool_in):
    if pool_in:
        sh_ref, sc_ref, n1_ref, wi_ref, x_out, u_out = rest
    else:
        (x_out,) = rest
    gc = gcol_ref[...]
    out = gc[:, 0:1] * _unpack_bf16_pairs(y0_ref[0]) + gc[:, 1:2] * _unpack_bf16_pairs(y1_ref[0])
    x2 = x_ref[0] + g2_ref[0] * out
    x_out[0] = x2
    if pool_in:
        hb = _rms_mod(x2, n1_ref[...], sh_ref[0], sc_ref[0]).astype(BF16)
        u_out[0] = jnp.dot(hb, wi_ref[...], preferred_element_type=F32).astype(BF16)


def _combine(yg, gcol, x, g2, pool_args=None, *, tc):
    b, l, d = x.shape
    nt = l // tc
    pool_in = pool_args is not None
    kern = functools.partial(_combine_kernel, pool_in=pool_in)
    row = lambda bi, i: (bi, 0, 0)
    fix = lambda bi, i: (0, 0)
    tile = lambda bi, i: (bi, i, 0)
    in_specs = [pl.BlockSpec((1, tc, d // 2), lambda bi, i: (0, bi * nt + i, 0)),
                pl.BlockSpec((1, tc, d // 2), lambda bi, i: (1, bi * nt + i, 0)),
                pl.BlockSpec((tc, LANES), lambda bi, i: (bi * nt + i, 0)),
                pl.BlockSpec((1, tc, d), tile),
                pl.BlockSpec((1, 1, d), row)]
    out_specs = [pl.BlockSpec((1, tc, d), tile)]
    out_shape = [jax.ShapeDtypeStruct((b, l, d), F32)]
    args = [yg, yg, gcol, x, g2]
    if pool_in:
        in_specs += [pl.BlockSpec((1, 1, d), row), pl.BlockSpec((1, 1, d), row),
                     pl.BlockSpec((1, d), fix), pl.BlockSpec((d, d), fix)]
        out_specs.append(pl.BlockSpec((1, tc, d), tile))
        out_shape.append(jax.ShapeDtypeStruct((b, l, d), BF16))
        args += list(pool_args)
    return pl.pallas_call(
        kern,
        grid=(b, nt),
        in_specs=in_specs, out_specs=out_specs, out_shape=out_shape,
        compiler_params=_params("arbitrary", "arbitrary"),
        name="moe_combine",
    )(*args)


def _moe(h2, ridx, gcol, counts, x1, g2, w_gate, w_up, w_down, layer, pool_args=None):
    b, l, d = x1.shape
    n = b * l
    n_blocks = (n * TOP_K) // SLOT_ROWS + N_EXPERTS
    cnt = counts[:, 0].astype(jnp.int32)
    padded = (cnt + SLOT_ROWS - 1) // SLOT_ROWS * SLOT_ROWS
    earlier = jnp.arange(N_EXPERTS)[None, :] < jnp.arange(N_EXPERTS)[:, None]
    pad_start = jnp.sum(jnp.where(earlier, padded[None, :], 0), axis=1).astype(jnp.int32)
    pad_end = pad_start + padded
    block_start = jnp.arange(n_blocks, dtype=jnp.int32) * SLOT_ROWS
    block_e = jnp.minimum(jnp.sum(pad_end[None, :] <= block_start[:, None], axis=1), N_EXPERTS - 1).astype(jnp.int32)
    own = block_e[:, None] == jnp.arange(N_EXPERTS)[None, :]
    data_end = jnp.sum(jnp.where(own, (pad_start + cnt)[None, :], 0), axis=1)
    n_valid = jnp.clip(data_end - block_start, 0, SLOT_ROWS).astype(jnp.int32)
    dest = _slot_index(pad_start, ridx, tn=2048)
    xs = _sc_scatter_rows(h2.reshape(n, d // 2), dest[0], dest[1], n_blocks * SLOT_ROWS)
    ys = _expert_ffn(block_e, n_valid, xs, w_gate, w_up, w_down, layer)
    yg = _sc_gather_rows(ys, dest[:TOP_K].reshape(TOP_K * n)).reshape(TOP_K, n, d // 2)
    return _combine(yg, gcol, x1, g2, pool_args, tc=1024)


def kernel(x, c, ctx, c_ctx, ada_w, ada_b, norm1_g, norm2_g, attn_w_in, attn_w_out, attn_q_gain, attn_k_gain,
           attn_lq1, attn_lk1, attn_lq2, attn_lk2, attn_sub_gain, pool_w_in, pool_w_group, pool_scale, pool_w_out,
           router_w, router_b, moe_w_gate, moe_w_up, moe_w_down):
    b, l, d = x.shape
    n_ctx = ctx.shape[1]
    depth = ada_w.shape[0]
    assert depth == 2 and d == N_HEADS * V_DIM
    tm = 512

    mod = _adaln_mod(c, c_ctx, ada_w, ada_b)
    mods = [[mod[i, :b, None, j * d:(j + 1) * d] for j in range(N_MOD)] for i in range(depth)]
    mod_ctx = [jnp.broadcast_to(mod[0, b, j * d:(j + 1) * d], (b, 1, d)) for j in range(2)]

    rwt = router_w.T
    rwh = rwt.astype(BF16)
    rwl = (rwt - rwh.astype(F32)).astype(BF16)
    rb = router_b.reshape(N_EXPERTS, 1)

    sh1, s1, g1, sh2, s2, g2 = mods[0]
    cos, sin = _rope_tables(l)
    pair_up = (jnp.arange(LANES) & 16) == 0

    def gain_rows(g, factor):
        g2 = jnp.concatenate([g, g]) * factor
        return [g2, jnp.where(pair_up, jnp.roll(g2, -16), jnp.roll(g2, 16))]

    q_scale = HEAD_DIM ** -0.5 * math.log2(math.e)
    q_max = jnp.maximum(jnp.max(jnp.abs(attn_q_gain[0])) * (HEAD_DIM ** 0.5 * q_scale), F32_TINY)
    k_max = jnp.maximum(jnp.max(jnp.abs(attn_k_gain[0])) * HEAD_DIM ** 0.5, F32_TINY)
    need = jnp.ceil(jnp.log2(k_max / F8_MAX))
    room = jnp.floor(jnp.log2(F8_MAX / q_max))
    trade = jnp.exp2(jnp.clip(jnp.clip(0.0, need, jnp.maximum(need, room)), -60.0, 60.0))
    h_max = jnp.float32(0.0)
    for shift_, scale_ in ((sh1, s1), (mod_ctx[0], mod_ctx[1])):
        h_max = jnp.maximum(h_max, jnp.max(d ** 0.5 * jnp.max(jnp.abs(norm1_g[0] * (1.0 + scale_)), axis=-1)
                                           + jnp.sqrt(jnp.sum(shift_ * shift_, axis=-1))))
    w_v = attn_w_in[0][:, 2 * d:]
    v_max = 1.02 * h_max * jnp.sqrt(jnp.max(jnp.sum(w_v * w_v, axis=0)))
    v_grow = jnp.exp2(jnp.clip(jnp.ceil(jnp.log2(jnp.maximum(v_max, F32_TINY) / F8_MAX)), 0.0, 60.0))

    gains = jnp.stack(gain_rows(attn_q_gain[0], q_scale * trade) + gain_rows(attn_k_gain[0], 1.0 / trade))
    w_qk = attn_w_in[0][:, :2 * d].astype(BF16)
    wv_t = (w_v / v_grow).T.astype(BF16)
    assert l % tm == 0 and l % n_ctx == 0
    q, k_all, vt_all = _qkv_proj(x, sh1, s1, norm1_g[0][None], w_qk, wv_t, gains, cos, sin, n_qk=2, rope=True,
                                 tm=tm, n_keys=l + n_ctx, key_row0=0)
    k_all, vt_all = _qkv_proj(ctx, mod_ctx[0], mod_ctx[1], norm1_g[0][None], w_qk[:, d:], wv_t, gains,
                              cos[:n_ctx], sin[:n_ctx], n_qk=1, rope=False,
                              tm=n_ctx, n_keys=l + n_ctx, key_row0=l, kv=(k_all, vt_all))
    lam_init = 0.8 - 0.6 * math.exp(-0.3 * 0)
    lam_params = jnp.stack([attn_lq1[0], attn_lk1[0], attn_lq2[0], attn_lk2[0],
                            jnp.full((HEAD_DIM,), v_grow, F32)])
    o = _diff_attention(q, k_all, vt_all, lam_params, attn_sub_gain[0][:, None], lam_init=lam_init)

    fix = lambda bi, i: (0, 0)
    x1, h2, ridx, gcol, counts = _mixer_tail(
        (o, attn_w_out[0].astype(BF16)),
        (pl.BlockSpec((1, 2 * TAIL_ROWS, d), lambda bi, i: (bi, i, 0)), pl.BlockSpec((d, d), fix)),
        _attn_tail_kernel, x, g1, sh2, s2, norm2_g[0][None], rwh, rwl, rb, tm=2 * TAIL_ROWS)

    sh1b, s1b, g1b, sh2b, s2b, g2b = mods[1]
    x2, u = _moe(h2, ridx, gcol, counts, x1, g2, moe_w_gate, moe_w_up, moe_w_down, 0,
                 pool_args=(sh1b, s1b, norm1_g[1][None], pool_w_in[0].astype(BF16)))
    gd = pool_w_group.shape[2]
    tp = 2 * TAIL_ROWS
    nh = tp // HALO_ROWS
    front_specs = (
        pl.BlockSpec((1, tp, d), lambda bi, i: (bi, i, 0)),
        pl.BlockSpec((1, HALO_ROWS, d), lambda bi, i: (bi, jnp.maximum(i * nh - 1, 0), 0)),
        pl.BlockSpec((1, HALO_ROWS, d), lambda bi, i: (bi, jnp.minimum((i + 1) * nh, l // HALO_ROWS - 1), 0)),
        pl.BlockSpec((len(POOL_WINDOWS), gd, gd), lambda bi, i: (0, 0, 0)),
        pl.BlockSpec((1, d), fix),
        pl.BlockSpec((d, d), fix))
    x3, h2b, ridx_b, gcol_b, counts_b = _mixer_tail(
        (u, u, u, pool_w_group[0].astype(BF16), pool_scale[0][None], pool_w_out[0].astype(BF16)),
        front_specs, functools.partial(_pool_tail_kernel, seq_len=l),
        x2, g1b, sh2b, s2b, norm2_g[1][None], rwh, rwl, rb, tm=tp,
        scratch=[pltpu.VMEM((tp + 2 * POOL_HALO, d), F32)] * 2)
    (out,) = _moe(h2b, ridx_b, gcol_b, counts_b, x3, g2b, moe_w_gate, moe_w_up, moe_w_down, 1)
    return out
```

```python
import functools
import math

import jax
import jax.numpy as jnp
import numpy as np
from jax import lax
from jax.experimental import pallas as pl
from jax.experimental.pallas import tpu as pltpu
from jax.experimental.pallas import tpu_sc as plsc

F32 = jnp.float32
BF16 = jnp.bfloat16
F8 = jnp.float8_e4m3fn
F8_MAX = float(jnp.finfo(F8).max)
F32_TINY = float(jnp.finfo(F32).tiny)

LANES = 128
SUBLANES = 8
N_HEADS = 8
HEAD_DIM = 64
V_DIM = 2 * HEAD_DIM
V_ROWS = V_DIM + 32
P_SHIFT = 8.0
GRID_W = 64
ROPE_THETA = 10000.0
NORM_EPS = 1e-6
N_MOD = 6
POOL_WINDOWS = (2, 4, 8, 16)
POOL_HALO = max(POOL_WINDOWS) // 2
HALO_ROWS = 16
N_EXPERTS = 32
N_EXPERT_GROUPS = 4
EXPERTS_PER_GROUP = N_EXPERTS // N_EXPERT_GROUPS
TOP_K = 2
SLOT_ROWS = 512
TAIL_ROWS = 512
SC_CORES = 2
SC_SUBCORES = 16
SC_WORKERS = SC_CORES * SC_SUBCORES
SC_WINDOW = 64
ATTN_GROUP_CHUNKS = 1
ATTN_SCORE_AHEAD = 1
VMEM_LIMIT = 48 * 1024 * 1024
NT_DIMS = (((1,), (1,)), ((), ()))


def _params(*sem):
    return pltpu.CompilerParams(dimension_semantics=sem, vmem_limit_bytes=VMEM_LIMIT)


def _rms_mod(x, gain, shift, scale):
    inv_rms = lax.rsqrt(jnp.mean(x * x, axis=-1, keepdims=True) + NORM_EPS)
    return x * inv_rms * (gain * (1.0 + scale)) + shift


def _pack_bf16_pairs(x):
    c = x.shape[1] // 2
    hi = lax.bitcast_convert_type(x[:, :c].astype(BF16).astype(F32), jnp.uint32)
    lo = lax.bitcast_convert_type(x[:, c:].astype(BF16).astype(F32), jnp.uint32)
    return lax.bitcast_convert_type(hi | (lo >> 16), jnp.int32)


def _unpack_bf16_pairs(w):
    u = lax.bitcast_convert_type(w, jnp.uint32)
    hi = lax.bitcast_convert_type(u & jnp.uint32(0xFFFF0000), F32)
    lo = lax.bitcast_convert_type(u << 16, F32)
    return jnp.concatenate([hi, lo], axis=1)


def _mod_kernel(c_ref, w_ref, b_ref, o_ref):
    c = c_ref[...]
    a = c / (1.0 + jnp.exp(-c))
    o_ref[0] = jnp.dot(a, w_ref[0], precision=lax.Precision.HIGHEST,
                       preferred_element_type=F32) + b_ref[0]


def _adaln_mod(c, c_ctx, ada_w, ada_b):
    depth, d, n_out = ada_w.shape
    b = c.shape[0]
    assert b + 1 <= SUBLANES
    rows = jnp.concatenate([c, c_ctx[None], jnp.zeros((SUBLANES - b - 1, d), F32)], axis=0)
    tn = n_out // 4
    return pl.pallas_call(
        _mod_kernel,
        grid=(depth, n_out // tn),
        in_specs=[pl.BlockSpec((SUBLANES, d), lambda i, j: (0, 0)),
                  pl.BlockSpec((1, d, tn), lambda i, j: (i, 0, j)),
                  pl.BlockSpec((1, 1, tn), lambda i, j: (i, 0, j))],
        out_specs=pl.BlockSpec((1, SUBLANES, tn), lambda i, j: (i, 0, j)),
        out_shape=jax.ShapeDtypeStruct((depth, SUBLANES, n_out), F32),
        compiler_params=_params("arbitrary", "arbitrary"),
        name="adaln_mod",
    )(rows, ada_w, ada_b.reshape(depth, 1, n_out))


def _qkv_kernel(x_ref, sh_ref, sc_ref, g_ref, w_ref, wvt_ref, gains_ref, cos_ref, sin_ref, *refs, n_qk, rope):
    out_refs = refs[-(n_qk + 1):]
    tm, d = x_ref.shape[1:]
    hb = _rms_mod(x_ref[0], g_ref[...], sh_ref[0], sc_ref[0]).astype(BF16)
    lane_b4 = (lax.broadcasted_iota(jnp.int32, (1, LANES), 1) & 16) == 0
    chunk_r = lax.broadcasted_iota(jnp.int32, (2 * LANES, 2 * LANES), 0) // HEAD_DIM
    chunk_c = lax.broadcasted_iota(jnp.int32, (2 * LANES, 2 * LANES), 1) // HEAD_DIM
    same_chunk = (chunk_r == chunk_c).astype(BF16)
    gains = gains_ref[...]
    tables = []
    for t in range(n_qk):
        r = 2 * (t + 2 - n_qk)
        tables.append((cos_ref[...] * gains[r:r + 1], sin_ref[...] * gains[r + 1:r + 2]) if rope
                      else (gains[r:r + 1], None))

    def project(t, j):
        return jnp.dot(hb, w_ref[:, t * d + j:t * d + j + 2 * LANES], preferred_element_type=F32)

    def chunk_sums(acc):
        return jnp.dot((acc * acc).astype(BF16), same_chunk, preferred_element_type=F32)

    def finish(t, j, acc, ssq):
        cos_t, sin_t = tables[t]
        nrm = acc * lax.rsqrt(ssq * (1.0 / HEAD_DIM) + NORM_EPS)
        for half in range(2):
            blk = nrm[:, half * LANES:(half + 1) * LANES]
            if rope:
                rot = jnp.where(lane_b4, pltpu.roll(blk, LANES - 16, 1), pltpu.roll(blk, 16, 1))
                y = blk * cos_t + rot * sin_t
            else:
                y = blk * cos_t
            c0 = j + half * LANES
            if t < n_qk - 1:
                yt = y.T
                first = lax.broadcasted_iota(jnp.int32, (V_DIM, 1), 0) < HEAD_DIM
                out_refs[t][0, c0 // LANES, 0, :, 0:tm] = jnp.where(first, yt, 0.0).astype(F8)
                out_refs[t][0, c0 // LANES, 0, :, tm:2 * tm] = jnp.where(first, 0.0, yt).astype(F8)
            else:
                out_refs[t][0, :, c0:c0 + LANES] = y.astype(F8)

    blocks = [(t, j) for t in range(n_qk) for j in range(0, d, 2 * LANES)]
    accs, sums = {}, {}
    for i in range(-2, len(blocks)):
        if i + 2 < len(blocks):
            accs[i + 2] = project(*blocks[i + 2])
        if 0 <= i + 1 < len(blocks):
            sums[i + 1] = chunk_sums(accs[i + 1])
        if i >= 0:
            finish(*blocks[i], accs.pop(i), sums.pop(i))
    vt_ref = out_refs[n_qk]
    ones = jnp.where(lax.broadcasted_iota(jnp.int32, (V_ROWS - V_DIM, tm), 0) == 0, 1.0, 0.0).astype(F8)
    for j in range(0, d, 2 * LANES):
        acc_t = lax.dot_general(wvt_ref[j:j + 2 * LANES, :], hb, NT_DIMS, preferred_element_type=F32)
        for half in range(2):
            h = j // LANES + half
            vt_ref[0, h, 0:V_DIM, :] = acc_t[half * V_DIM:(half + 1) * V_DIM].astype(F8)
            vt_ref[0, h, V_DIM:V_ROWS, :] = ones


def _qkv_proj(x, shift, scale, gain, w, wv_t, gains, cos, sin, *, n_qk, rope, tm, n_keys, key_row0, kv=None):
    b, l, d = x.shape
    n_out = n_qk + 1
    kern = functools.partial(_qkv_kernel, n_qk=n_qk, rope=rope)
    row = lambda bi, i: (bi, 0, 0)
    fix = lambda bi, i: (0, 0)
    kb = key_row0 // tm
    tile = pl.BlockSpec((1, tm, d), lambda bi, i: (bi, i, 0))
    k_spec = pl.BlockSpec((1, tm, d), lambda bi, i: (bi, kb + i, 0))
    vt_spec = pl.BlockSpec((1, N_HEADS, V_ROWS, tm), lambda bi, i: (bi, 0, 0, kb + i))
    q_spec = pl.BlockSpec((1, N_HEADS, 1, V_DIM, 2 * tm), lambda bi, i: (bi, 0, i, 0, 0))
    q_shape = jax.ShapeDtypeStruct((b, N_HEADS, l // tm, V_DIM, 2 * tm), F8)
    k_shape = jax.ShapeDtypeStruct((b, n_keys, d), F8)
    vt_shape = jax.ShapeDtypeStruct((b, N_HEADS, V_ROWS, n_keys), F8)
    in_specs = [tile,
                pl.BlockSpec((1, 1, d), row), pl.BlockSpec((1, 1, d), row),
                pl.BlockSpec((1, d), fix),
                pl.BlockSpec((d, n_qk * d), fix),
                pl.BlockSpec((d, d), fix),
                pl.BlockSpec((4, LANES), fix),
                pl.BlockSpec((tm, LANES), lambda bi, i: (i, 0)),
                pl.BlockSpec((tm, LANES), lambda bi, i: (i, 0))]
    args = [x, shift, scale, gain, w, wv_t, gains, cos, sin]
    aliases = {}
    if kv is not None:
        aliases = {len(args): n_qk - 1, len(args) + 1: n_qk}
        in_specs += [pl.BlockSpec(memory_space=pl.ANY)] * 2
        args += list(kv)
    return pl.pallas_call(
        kern,
        grid=(b, l // tm),
        in_specs=in_specs,
        out_specs=[q_spec] * (n_qk - 1) + [k_spec, vt_spec],
        out_shape=[q_shape] * (n_qk - 1) + [k_shape, vt_shape],
        input_output_aliases=aliases,
        compiler_params=_params("arbitrary", "arbitrary"),
        name="qkv_proj",
    )(*args)


def _rope_tables(n_tokens):
    rows = n_tokens // GRID_W
    row = np.repeat(np.arange(rows, dtype=np.float32), GRID_W)
    col = np.tile(np.arange(GRID_W, dtype=np.float32), rows)
    half = HEAD_DIM // 2
    inv_freq = (np.float32(ROPE_THETA) ** (-np.arange(0, half, 2, dtype=np.float32) / half)).astype(np.float32)
    ang_r = row[:, None] * inv_freq
    ang_c = col[:, None] * inv_freq
    ang = np.concatenate([ang_r, ang_r, ang_c, ang_c] * 2, axis=-1)
    sign = np.where((np.arange(LANES) & 16) == 0, -1.0, 1.0).astype(np.float32)
    return jnp.asarray(np.cos(ang), F32), jnp.asarray(np.sin(ang) * sign, F32)


def _attn_kernel(q_ref, k_ref, vt_ref, lp_ref, sg_ref, o_ref, s_ref, *, tk, group, ahead, lam_init):
    tq = q_ref.shape[4] // 2
    n_chunks = k_ref.shape[1] // tk
    qz = q_ref[0, 0, 0]
    n_slots = (ahead + 1) * group

    def score_chunk(c, m_grp):
        st = jnp.dot(k_ref[0, c * tk:(c + 1) * tk, :], qz, preferred_element_type=F32).astype(BF16)
        slot = c % n_slots
        s_ref[slot * tk:(slot + 1) * tk, :] = st
        mc = jnp.max(st, axis=0, keepdims=True)
        return mc if m_grp is None else jnp.maximum(m_grp, mc)

    def value_chunk(c, m_ref, part):
        slot = c % n_slots
        p = jnp.exp2(s_ref[slot * tk:(slot + 1) * tk, :] - (m_ref - P_SHIFT)).astype(F8)
        pv = jnp.dot(vt_ref[0, 0, :, c * tk:(c + 1) * tk], p, preferred_element_type=F32)
        return pv if part is None else part + pv

    groups = [list(range(g0, min(g0 + group, n_chunks))) for g0 in range(0, n_chunks, group)]
    m_of = {}
    for g in range(min(ahead, len(groups))):
        for c in groups[g]:
            m_of[g] = score_chunk(c, m_of.get(g))
    m = None
    acc = None
    for gi, cur in enumerate(groups):
        nxt = groups[gi + ahead] if gi + ahead < len(groups) else []
        m_new = m_of[gi] if m is None else jnp.maximum(m, m_of[gi])
        part = None
        for i in range(max(len(cur), len(nxt))):
            if i < len(nxt):
                m_of[gi + ahead] = score_chunk(nxt[i], m_of.get(gi + ahead))
            if i < len(cur):
                part = value_chunk(cur[i], m_new, part)
        acc = part if acc is None else acc * jnp.exp2(m.astype(F32) - m_new.astype(F32)) + part
        m = m_new
    acc = acc[:V_DIM] / acc[V_DIM:V_DIM + 1]
    lp = lp_ref[...]
    lam = (jnp.exp(jnp.sum(lp[0:1] * lp[1:2], axis=-1, keepdims=True))
           - jnp.exp(jnp.sum(lp[2:3] * lp[3:4], axis=-1, keepdims=True)) + lam_init)
    o = (acc[:, :tq] - lam * acc[:, tq:]) * lp[4:5, 0:1]
    o = o * lax.rsqrt(jnp.mean(o * o, axis=0, keepdims=True) + NORM_EPS) * sg_ref[...] * (1.0 - lam_init)
    o_ref[0] = o.T.astype(BF16)


def _attn_chunk(n_keys):
    for tk in (768, 512, 256, 128):
        if n_keys % tk == 0:
            return tk
    raise ValueError(f"key count {n_keys} is not a multiple of {LANES}")


def _diff_attention(qz, k_all, vt_all, lam_params, sub_gain, *, lam_init):
    b, _, n_tiles, _, tq2 = qz.shape
    tq = tq2 // 2
    l, d = n_tiles * tq, N_HEADS * V_DIM
    n_keys = k_all.shape[1]
    tk = _attn_chunk(n_keys)
    kern = functools.partial(_attn_kernel, tk=tk, group=ATTN_GROUP_CHUNKS, ahead=ATTN_SCORE_AHEAD, lam_init=lam_init)
    return pl.pallas_call(
        kern,
        grid=(b, N_HEADS, l // tq),
        in_specs=[pl.BlockSpec((1, 1, 1, V_DIM, 2 * tq), lambda bi, h, i: (bi, h, i, 0, 0)),
                  pl.BlockSpec((1, n_keys, V_DIM), lambda bi, h, i: (bi, 0, h)),
                  pl.BlockSpec((1, 1, V_ROWS, n_keys), lambda bi, h, i: (bi, h, 0, 0)),
                  pl.BlockSpec((5, HEAD_DIM), lambda bi, h, i: (0, 0)),
                  pl.BlockSpec((V_DIM, 1), lambda bi, h, i: (0, 0))],
        out_specs=pl.BlockSpec((1, tq, V_DIM), lambda bi, h, i: (bi, i, h)),
        out_shape=jax.ShapeDtypeStruct((b, l, d), BF16),
        scratch_shapes=[pltpu.VMEM(((ATTN_SCORE_AHEAD + 1) * ATTN_GROUP_CHUNKS * tk, 2 * tq), BF16)],
        compiler_params=_params("arbitrary", "arbitrary", "arbitrary"),
        name="diff_attention",
    )(qz, k_all, vt_all, lam_params, sub_gain)


def _route(h2, row0, rwh_ref, rwl_ref, rb_ref, before_ref, carry_ref, ridx_ref, gcol_ref, cnt_ref, is_first):
    tm = h2.shape[0]
    hh = h2.astype(BF16)
    hl = (h2 - hh.astype(F32)).astype(BF16)
    rw2 = jnp.concatenate([rwh_ref[...], rwl_ref[...]], axis=0)
    part = lax.dot_general(rw2, hh, NT_DIMS, preferred_element_type=F32)
    logits = (part[:N_EXPERTS] + part[N_EXPERTS:]
              + lax.dot_general(rwh_ref[...], hl, NT_DIMS, preferred_element_type=F32) + rb_ref[...])
    groups = [logits[g * EXPERTS_PER_GROUP:(g + 1) * EXPERTS_PER_GROUP] for g in range(N_EXPERT_GROUPS)]
    top = groups[0]
    for g in range(1, N_EXPERT_GROUPS):
        top = jnp.maximum(top, groups[g])
    top = jnp.max(top, axis=0, keepdims=True)
    sub = lax.broadcasted_iota(jnp.int32, (EXPERTS_PER_GROUP, tm), 0)
    best = None
    for g in range(N_EXPERT_GROUPS):
        ex = jnp.exp(groups[g] - top)
        v1 = jnp.max(ex, axis=0, keepdims=True)
        i1 = jnp.min(jnp.where(ex == v1, sub, EXPERTS_PER_GROUP), axis=0, keepdims=True)
        rest = jnp.where(sub == i1, -1.0, ex)
        v2 = jnp.max(rest, axis=0, keepdims=True)
        i2 = jnp.min(jnp.where(rest == v2, sub, EXPERTS_PER_GROUP), axis=0, keepdims=True)
        cand = (v1 + v2, v1, v2, i1 + g * EXPERTS_PER_GROUP, i2 + g * EXPERTS_PER_GROUP)
        if best is None:
            best = cand
        else:
            better = cand[0] > best[0]
            best = tuple(jnp.where(better, new, old) for new, old in zip(cand, best))
    _, v1, v2, e0, e1 = best
    gate0 = v1 / (v1 + v2)
    gate1 = v2 / (v1 + v2)

    @pl.when(is_first)
    def _():
        carry_ref[...] = jnp.zeros_like(carry_ref)

    erow = lax.broadcasted_iota(jnp.int32, (N_EXPERTS, tm), 0)
    oh0 = erow == e0
    oh1 = erow == e1
    chosen = jnp.where(oh0 | oh1, 1.0, 0.0)
    prior = carry_ref[:, 0:1] + jnp.dot(chosen.astype(BF16), before_ref[...], preferred_element_type=F32)
    r0 = jnp.sum(jnp.where(oh0, prior, 0.0), axis=0, keepdims=True).astype(jnp.int32)
    r1 = jnp.sum(jnp.where(oh1, prior, 0.0), axis=0, keepdims=True).astype(jnp.int32)
    carry_ref[...] = carry_ref[...] + jnp.sum(chosen, axis=1, keepdims=True)
    cnt_ref[...] = carry_ref[...]
    rid = lax.broadcasted_iota(jnp.int32, (SUBLANES, tm), 0)
    ridx_ref[:, row0:row0 + tm] = jnp.where(rid == 0, e0, jnp.where(rid == 1, e1, jnp.where(rid == 2, r0, jnp.where(rid == 3, r1, 0))))
    gcol_ref[:, row0:row0 + tm] = jnp.where(rid == 0, gate0, jnp.where(rid == 1, gate1, 0.0))


def _tail(y, row0, x_ref, g1_ref, sh2_ref, s2_ref, n2_ref, rwh_ref, rwl_ref, rb_ref, before_ref,
          x_out, h2_out, ridx_ref, gcol_ref, cnt_ref, carry_ref):
    rows = slice(row0, row0 + y.shape[0])
    x1 = x_ref[0, rows] + g1_ref[0] * y
    x_out[0, rows] = x1
    h2 = _rms_mod(x1, n2_ref[...], sh2_ref[0], s2_ref[0])
    h2_out[0, rows] = _pack_bf16_pairs(h2)
    is_first = (pl.program_id(0) == 0) & (pl.program_id(1) == 0) & (row0 == 0)
    _route(h2, row0, rwh_ref, rwl_ref, rb_ref, before_ref, carry_ref, ridx_ref, gcol_ref, cnt_ref, is_first)


def _attn_tail_kernel(a_ref, wo_ref, *rest):
    starts = range(0, a_ref.shape[1], TAIL_ROWS)
    ys = [jnp.dot(a_ref[0, r0:r0 + TAIL_ROWS], wo_ref[...], preferred_element_type=F32) for r0 in starts]
    for r0, y in zip(starts, ys):
        _tail(y, r0, *rest)


def _pool_tail_kernel(u_ref, up_ref, un_ref, wg_ref, cs_ref, wo_ref, *rest, seq_len):
    *tail_refs, ubuf, abuf = rest
    tm = u_ref.shape[1]
    i = pl.program_id(1)
    u = u_ref[0].astype(F32)
    ubuf[0:POOL_HALO] = jnp.where(i > 0, up_ref[0].astype(F32)[HALO_ROWS - POOL_HALO:], 0.0)
    ubuf[POOL_HALO:POOL_HALO + tm] = u
    ubuf[POOL_HALO + tm:2 * POOL_HALO + tm] = jnp.where(i < pl.num_programs(1) - 1,
                                                        un_ref[0].astype(F32)[:POOL_HALO], 0.0)
    pos = i * tm + lax.broadcasted_iota(jnp.int32, (tm, 1), 0)
    gd = wg_ref.shape[1]
    assert all(win == 2 ** (g + 1) for g, win in enumerate(POOL_WINDOWS))
    n_ext = tm + 2 * POOL_HALO
    bufs = (ubuf, abuf)
    abuf[1:n_ext, :] = ubuf[0:n_ext - 1, :] + ubuf[1:n_ext, :]
    lo, hi = 1, n_ext
    for g in range(1, len(POOL_WINDOWS)):
        src, dst = bufs[g % 2], bufs[(g + 1) % 2]
        sh = POOL_WINDOWS[g] // 4
        dst[lo + sh:hi - sh, g * gd:] = src[lo:hi - 2 * sh, g * gd:] + src[lo + 2 * sh:hi, g * gd:]
        lo, hi = lo + sh, hi - sh
    outs = []
    for g, win in enumerate(POOL_WINDOWS):
        half = win // 2
        cols = slice(g * gd, (g + 1) * gd)
        s = bufs[(g + 1) % 2][POOL_HALO:POOL_HALO + tm, cols]
        inv_cnt = 1.0 / (jnp.minimum(pos + half, seq_len) - jnp.maximum(pos - half, 0)).astype(F32)
        dlt = (s * inv_cnt - u[:, cols]).astype(BF16)
        outs.append(jnp.dot(dlt, wg_ref[g], preferred_element_type=F32))
    z = (jnp.concatenate(outs, axis=-1) * cs_ref[...]).astype(BF16)
    starts = range(0, tm, TAIL_ROWS)
    ys = [jnp.dot(z[r0:r0 + TAIL_ROWS], wo_ref[...], preferred_element_type=F32) for r0 in starts]
    for r0, y in zip(starts, ys):
        _tail(y, r0, *tail_refs)


def _mixer_tail(front_args, front_specs, kern, x, g1, sh2, s2, n2g, rwh, rwl, rb, *, tm, scratch=()):
    b, l, d = x.shape
    nt = l // tm
    n = b * l
    row = lambda bi, i: (bi, 0, 0)
    fix = lambda bi, i: (0, 0)
    tile = lambda bi, i: (bi, i, 0)
    in_specs = list(front_specs) + [
        pl.BlockSpec((1, tm, d), tile),
        pl.BlockSpec((1, 1, d), row), pl.BlockSpec((1, 1, d), row), pl.BlockSpec((1, 1, d), row),
        pl.BlockSpec((1, d), fix),
        pl.BlockSpec((N_EXPERTS, d), fix), pl.BlockSpec((N_EXPERTS, d), fix),
        pl.BlockSpec((N_EXPERTS, 1), fix),
        pl.BlockSpec((TAIL_ROWS, TAIL_ROWS), fix)]
    before = jnp.asarray(np.triu(np.ones((TAIL_ROWS, TAIL_ROWS), np.float32), k=1), BF16)
    out_specs = [pl.BlockSpec((1, tm, d), tile), pl.BlockSpec((1, tm, d // 2), tile),
                 pl.BlockSpec((SUBLANES, tm), lambda bi, i: (0, bi * nt + i)),
                 pl.BlockSpec((SUBLANES, tm), lambda bi, i: (0, bi * nt + i)),
                 pl.BlockSpec((N_EXPERTS, LANES), fix)]
    out_shape = [jax.ShapeDtypeStruct((b, l, d), F32), jax.ShapeDtypeStruct((b, l, d // 2), jnp.int32),
                 jax.ShapeDtypeStruct((SUBLANES, n), jnp.int32), jax.ShapeDtypeStruct((SUBLANES, n), F32),
                 jax.ShapeDtypeStruct((N_EXPERTS, LANES), F32)]
    return pl.pallas_call(
        kern,
        grid=(b, nt),
        in_specs=in_specs, out_specs=out_specs, out_shape=out_shape,
        scratch_shapes=[pltpu.VMEM((N_EXPERTS, LANES), F32)] + list(scratch),
        compiler_params=_params("arbitrary", "arbitrary"),
        name="mixer_tail",
    )(*front_args, x, g1, sh2, s2, n2g, rwh, rwl, rb, before)


def _slot_kernel(ps_ref, ridx_ref, dest_ref):
    ridx = ridx_ref[...]
    ps = ps_ref[...]
    erow = lax.broadcasted_iota(jnp.int32, (N_EXPERTS, ridx.shape[1]), 0)
    rows = []
    for k in range(TOP_K):
        start = jnp.sum(jnp.where(erow == ridx[k:k + 1], ps, 0), axis=0, keepdims=True)
        rows.append(start + ridx[TOP_K + k:TOP_K + k + 1])
    rid = lax.broadcasted_iota(jnp.int32, ridx.shape, 0)
    dest_ref[...] = jnp.where(rid == 0, rows[0], jnp.where(rid == 1, rows[1], 0))


def _slot_index(pad_start, ridx, *, tn):
    n = ridx.shape[1]
    return pl.pallas_call(
        _slot_kernel,
        grid=(n // tn,),
        in_specs=[pl.BlockSpec((N_EXPERTS, 1), lambda i: (0, 0)),
                  pl.BlockSpec((SUBLANES, tn), lambda i: (0, i))],
        out_specs=pl.BlockSpec((SUBLANES, tn), lambda i: (0, i)),
        out_shape=jax.ShapeDtypeStruct((SUBLANES, n), jnp.int32),
        compiler_params=_params("arbitrary"),
        name="slot_index",
    )(pad_start.reshape(N_EXPERTS, 1), ridx)


def _sc_mesh():
    return plsc.VectorSubcoreMesh(core_axis_name="c", subcore_axis_name="s",
                                  num_cores=SC_CORES, num_subcores=SC_SUBCORES)


def _sc_worker_base(per_worker):
    return (lax.axis_index("s") * SC_CORES + lax.axis_index("c")) * per_worker


def _sc_scatter_rows(rows, idx0, idx1, n_slots):
    n, d = rows.shape
    per_worker = n // SC_WORKERS
    assert per_worker % SC_WINDOW == 0

    n_win = per_worker // SC_WINDOW
    assert n_win % 2 == 0

    def body(rows_hbm, i0_hbm, i1_hbm, out_hbm, i0_a, i1_a, rows_a, i0_b, i1_b, rows_b, sem_a, sem_b):
        base = _sc_worker_base(per_worker)

        def offset(j):
            return pl.multiple_of(base + j * SC_WINDOW, SC_WINDOW)

        def start(j, i0_v, i1_v, rows_v, sem):
            pltpu.sync_copy(i0_hbm.at[pl.ds(offset(j), SC_WINDOW)], i0_v)
            pltpu.sync_copy(i1_hbm.at[pl.ds(offset(j), SC_WINDOW)], i1_v)
            pltpu.async_copy(rows_hbm.at[pl.ds(offset(j), SC_WINDOW)], rows_v, sem)

        def finish(j, i0_v, i1_v, rows_v, sem):
            pltpu.make_async_copy(rows_hbm.at[pl.ds(offset(j), SC_WINDOW)], rows_v, sem).wait()
            pltpu.sync_copy(rows_v, out_hbm.at[i0_v])
            pltpu.sync_copy(rows_v, out_hbm.at[i1_v])

        start(0, i0_a, i1_a, rows_a, sem_a)

        @pl.loop(0, n_win, step=2)
        def _(j):
            start(j + 1, i0_b, i1_b, rows_b, sem_b)
            finish(j, i0_a, i1_a, rows_a, sem_a)

            @pl.when(j + 2 < n_win)
            def _():
                start(j + 2, i0_a, i1_a, rows_a, sem_a)

            finish(j + 1, i0_b, i1_b, rows_b, sem_b)

    window = [pltpu.VMEM((SC_WINDOW,), jnp.int32), pltpu.VMEM((SC_WINDOW,), jnp.int32),
              pltpu.VMEM((SC_WINDOW, d), rows.dtype)]
    return pl.kernel(
        body, out_type=jax.ShapeDtypeStruct((n_slots, d), rows.dtype), mesh=_sc_mesh(),
        scratch_types=window + window + [pltpu.SemaphoreType.DMA, pltpu.SemaphoreType.DMA],
        name="sc_scatter_rows",
    )(rows, idx0, idx1)


def _sc_gather_rows(table, idx):
    n = idx.shape[0]
    d = table.shape[1]
    per_worker = n // SC_WORKERS
    assert per_worker % SC_WINDOW == 0

    n_win = per_worker // SC_WINDOW
    assert n_win % 2 == 0

    def body(table_hbm, idx_hbm, out_hbm, idx_a, idx_b, rows_a, rows_b, sem_a, sem_b):
        base = _sc_worker_base(per_worker)

        def offset(j):
            return pl.multiple_of(base + j * SC_WINDOW, SC_WINDOW)

        def start(j, idx_v, rows_v, sem):
            pltpu.sync_copy(idx_hbm.at[pl.ds(offset(j), SC_WINDOW)], idx_v)
            pltpu.async_copy(table_hbm.at[idx_v], rows_v, sem)

        def finish(j, idx_v, rows_v, sem):
            pltpu.make_async_copy(table_hbm.at[idx_v], rows_v, sem).wait()
            pltpu.sync_copy(rows_v, out_hbm.at[pl.ds(offset(j), SC_WINDOW)])

        start(0, idx_a, rows_a, sem_a)

        @pl.loop(0, n_win, step=2)
        def _(j):
            start(j + 1, idx_b, rows_b, sem_b)
            finish(j, idx_a, rows_a, sem_a)

            @pl.when(j + 2 < n_win)
            def _():
                start(j + 2, idx_a, rows_a, sem_a)

            finish(j + 1, idx_b, rows_b, sem_b)

    return pl.kernel(
        body, out_type=jax.ShapeDtypeStruct((n, d), table.dtype), mesh=_sc_mesh(),
        scratch_types=[pltpu.VMEM((SC_WINDOW,), jnp.int32), pltpu.VMEM((SC_WINDOW,), jnp.int32),
                       pltpu.VMEM((SC_WINDOW, d), table.dtype), pltpu.VMEM((SC_WINDOW, d), table.dtype),
                       pltpu.SemaphoreType.DMA, pltpu.SemaphoreType.DMA],
        name="sc_gather_rows",
    )(table, idx)


def _ffn_kernel(be_ref, nv_ref, xs_ref, wg_ref, wu_ref, wd_ref, ys_ref, wg_b, wu_b, wd_b):
    j = pl.program_id(0)
    valid = nv_ref[j]

    @pl.when((valid > 0) & ((j == 0) | (be_ref[j] != be_ref[jnp.maximum(j - 1, 0)])))
    def _():
        wg_b[...] = wg_ref[0, 0].astype(BF16)
        wu_b[...] = wu_ref[0, 0].astype(BF16)
        wd_b[...] = wd_ref[0, 0].astype(BF16)

    @pl.when(valid > 0)
    def _():
        row = lax.broadcasted_iota(jnp.int32, (SLOT_ROWS, 1), 0)
        xw = jnp.where(row < valid, xs_ref[...], 0)
        xb = _unpack_bf16_pairs(xw).astype(BF16)
        g = jnp.dot(xb, wg_b[...], preferred_element_type=F32)
        u = jnp.dot(xb, wu_b[...], preferred_element_type=F32)
        a = (g / (1.0 + jnp.exp(-g)) * u).astype(BF16)
        ys_ref[...] = _pack_bf16_pairs(jnp.dot(a, wd_b[...], preferred_element_type=F32))

    @pl.when(valid <= 0)
    def _():
        ys_ref[...] = jnp.zeros_like(ys_ref)


def _expert_ffn(block_e, n_valid, xs, w_gate, w_up, w_down, layer):
    n_slots, dw = xs.shape
    d, de = w_gate.shape[2:]
    w_idx = lambda j, be, nv: (layer, be[j], 0, 0)
    return pl.pallas_call(
        _ffn_kernel,
        grid_spec=pltpu.PrefetchScalarGridSpec(
            num_scalar_prefetch=2,
            grid=(n_slots // SLOT_ROWS,),
            in_specs=[pl.BlockSpec((SLOT_ROWS, dw), lambda j, be, nv: (j, 0)),
                      pl.BlockSpec((1, 1, d, de), w_idx),
                      pl.BlockSpec((1, 1, d, de), w_idx),
                      pl.BlockSpec((1, 1, de, d), w_idx)],
            out_specs=pl.BlockSpec((SLOT_ROWS, dw), lambda j, be, nv: (j, 0)),
            scratch_shapes=[pltpu.VMEM((d, de), BF16), pltpu.VMEM((d, de), BF16), pltpu.VMEM((de, d), BF16)]),
        out_shape=jax.ShapeDtypeStruct((n_slots, dw), jnp.int32),
        compiler_params=_params("arbitrary"),
        name="expert_ffn",
    )(block_e, n_valid, xs, w_gate, w_up, w_down)


def _combine_kernel(y0_ref, y1_ref, gcol_ref, x_ref, g2_ref, *rest, pool_in):
    if pool_in:
        sh_ref, sc_ref, n1_ref, wi_ref, x_out, u_out = rest
    else:
        (x_out,) = rest
    g_rows = gcol_ref[...]
    gid = lax.broadcasted_iota(jnp.int32, (LANES, g_rows.shape[1]), 0)
    gc = jnp.where(gid == 0, g_rows[0:1], jnp.where(gid == 1, g_rows[1:2], 0.0)).T
    out = gc[:, 0:1] * _unpack_bf16_pairs(y0_ref[0]) + gc[:, 1:2] * _unpack_bf16_pairs(y1_ref[0])
    x2 = x_ref[0] + g2_ref[0] * out
    x_out[0] = x2
    if pool_in:
        hb = _rms_mod(x2, n1_ref[...], sh_ref[0], sc_ref[0]).astype(BF16)
        u_out[0] = jnp.dot(hb, wi_ref[...], preferred_element_type=F32).astype(BF16)


def _combine(yg, gcol, x, g2, pool_args=None, *, tc):
    b, l, d = x.shape
    nt = l // tc
    pool_in = pool_args is not None
    kern = functools.partial(_combine_kernel, pool_in=pool_in)
    row = lambda bi, i: (bi, 0, 0)
    fix = lambda bi, i: (0, 0)
    tile = lambda bi, i: (bi, i, 0)
    in_specs = [pl.BlockSpec((1, tc, d // 2), lambda bi, i: (0, bi * nt + i, 0)),
                pl.BlockSpec((1, tc, d // 2), lambda bi, i: (1, bi * nt + i, 0)),
                pl.BlockSpec((SUBLANES, tc), lambda bi, i: (0, bi * nt + i)),
                pl.BlockSpec((1, tc, d), tile),
                pl.BlockSpec((1, 1, d), row)]
    out_specs = [pl.BlockSpec((1, tc, d), tile)]
    out_shape = [jax.ShapeDtypeStruct((b, l, d), F32)]
    args = [yg, yg, gcol, x, g2]
    if pool_in:
        in_specs += [pl.BlockSpec((1, 1, d), row), pl.BlockSpec((1, 1, d), row),
                     pl.BlockSpec((1, d), fix), pl.BlockSpec((d, d), fix)]
        out_specs.append(pl.BlockSpec((1, tc, d), tile))
        out_shape.append(jax.ShapeDtypeStruct((b, l, d), BF16))
        args += list(pool_args)
    return pl.pallas_call(
        kern,
        grid=(b, nt),
        in_specs=in_specs, out_specs=out_specs, out_shape=out_shape,
        compiler_params=_params("arbitrary", "arbitrary"),
        name="moe_combine",
    )(*args)


def _moe(h2, ridx, gcol, counts, x1, g2, w_gate, w_up, w_down, layer, pool_args=None):
    b, l, d = x1.shape
    n = b * l
    n_blocks = (n * TOP_K) // SLOT_ROWS + N_EXPERTS
    cnt = counts[:, 0].astype(jnp.int32)
    padded = (cnt + SLOT_ROWS - 1) // SLOT_ROWS * SLOT_ROWS
    earlier = jnp.arange(N_EXPERTS)[None, :] < jnp.arange(N_EXPERTS)[:, None]
    pad_start = jnp.sum(jnp.where(earlier, padded[None, :], 0), axis=1).astype(jnp.int32)
    pad_end = pad_start + padded
    block_start = jnp.arange(n_blocks, dtype=jnp.int32) * SLOT_ROWS
    block_e = jnp.minimum(jnp.sum(pad_end[None, :] <= block_start[:, None], axis=1), N_EXPERTS - 1).astype(jnp.int32)
    own = block_e[:, None] == jnp.arange(N_EXPERTS)[None, :]
    data_end = jnp.sum(jnp.where(own, (pad_start + cnt)[None, :], 0), axis=1)
    n_valid = jnp.clip(data_end - block_start, 0, SLOT_ROWS).astype(jnp.int32)
    dest = _slot_index(pad_start, ridx, tn=2048)
    xs = _sc_scatter_rows(h2.reshape(n, d // 2), dest[0], dest[1], n_blocks * SLOT_ROWS)
    ys = _expert_ffn(block_e, n_valid, xs, w_gate, w_up, w_down, layer)
    yg = _sc_gather_rows(ys, dest[:TOP_K].reshape(TOP_K * n)).reshape(TOP_K, n, d // 2)
    return _combine(yg, gcol, x1, g2, pool_args, tc=1024)


def kernel(x, c, ctx, c_ctx, ada_w, ada_b, norm1_g, norm2_g, attn_w_in, attn_w_out, attn_q_gain, attn_k_gain,
           attn_lq1, attn_lk1, attn_lq2, attn_lk2, attn_sub_gain, pool_w_in, pool_w_group, pool_scale, pool_w_out,
           router_w, router_b, moe_w_gate, moe_w_up, moe_w_down):
    b, l, d = x.shape
    n_ctx = ctx.shape[1]
    depth = ada_w.shape[0]
    assert depth == 2 and d == N_HEADS * V_DIM
    tm = 512

    mod = _adaln_mod(c, c_ctx, ada_w, ada_b)
    mods = [[mod[i, :b, None, j * d:(j + 1) * d] for j in range(N_MOD)] for i in range(depth)]
    mod_ctx = [jnp.broadcast_to(mod[0, b, j * d:(j + 1) * d], (b, 1, d)) for j in range(2)]

    rwt = router_w.T
    rwh = rwt.astype(BF16)
    rwl = (rwt - rwh.astype(F32)).astype(BF16)
    rb = router_b.reshape(N_EXPERTS, 1)

    sh1, s1, g1, sh2, s2, g2 = mods[0]
    cos, sin = _rope_tables(l)
    pair_up = (jnp.arange(LANES) & 16) == 0

    def gain_rows(g, factor):
        g2 = jnp.concatenate([g, g]) * factor
        return [g2, jnp.where(pair_up, jnp.roll(g2, -16), jnp.roll(g2, 16))]

    q_scale = HEAD_DIM ** -0.5 * math.log2(math.e)
    q_max = jnp.maximum(jnp.max(jnp.abs(attn_q_gain[0])) * (HEAD_DIM ** 0.5 * q_scale), F32_TINY)
    k_max = jnp.maximum(jnp.max(jnp.abs(attn_k_gain[0])) * HEAD_DIM ** 0.5, F32_TINY)
    need = jnp.ceil(jnp.log2(k_max / F8_MAX))
    room = jnp.floor(jnp.log2(F8_MAX / q_max))
    trade = jnp.exp2(jnp.clip(jnp.clip(0.0, need, jnp.maximum(need, room)), -60.0, 60.0))
    h_max = jnp.float32(0.0)
    for shift_, scale_ in ((sh1, s1), (mod_ctx[0], mod_ctx[1])):
        h_max = jnp.maximum(h_max, jnp.max(d ** 0.5 * jnp.max(jnp.abs(norm1_g[0] * (1.0 + scale_)), axis=-1)
                                           + jnp.sqrt(jnp.sum(shift_ * shift_, axis=-1))))
    w_v = attn_w_in[0][:, 2 * d:]
    v_max = 1.02 * h_max * jnp.sqrt(jnp.max(jnp.sum(w_v * w_v, axis=0)))
    v_grow = jnp.exp2(jnp.clip(jnp.ceil(jnp.log2(jnp.maximum(v_max, F32_TINY) / F8_MAX)), 0.0, 60.0))

    gains = jnp.stack(gain_rows(attn_q_gain[0], q_scale * trade) + gain_rows(attn_k_gain[0], 1.0 / trade))
    w_qk = attn_w_in[0][:, :2 * d].astype(BF16)
    wv_t = (w_v / v_grow).T.astype(BF16)
    assert l % tm == 0 and l % n_ctx == 0
    q, k_all, vt_all = _qkv_proj(x, sh1, s1, norm1_g[0][None], w_qk, wv_t, gains, cos, sin, n_qk=2, rope=True,
                                 tm=tm, n_keys=l + n_ctx, key_row0=0)
    k_all, vt_all = _qkv_proj(ctx, mod_ctx[0], mod_ctx[1], norm1_g[0][None], w_qk[:, d:], wv_t, gains,
                              cos[:n_ctx], sin[:n_ctx], n_qk=1, rope=False,
                              tm=n_ctx, n_keys=l + n_ctx, key_row0=l, kv=(k_all, vt_all))
    lam_init = 0.8 - 0.6 * math.exp(-0.3 * 0)
    lam_params = jnp.stack([attn_lq1[0], attn_lk1[0], attn_lq2[0], attn_lk2[0],
                            jnp.full((HEAD_DIM,), v_grow, F32)])
    o = _diff_attention(q, k_all, vt_all, lam_params, attn_sub_gain[0][:, None], lam_init=lam_init)

    fix = lambda bi, i: (0, 0)
    x1, h2, ridx, gcol, counts = _mixer_tail(
        (o, attn_w_out[0].astype(BF16)),
        (pl.BlockSpec((1, 2 * TAIL_ROWS, d), lambda bi, i: (bi, i, 0)), pl.BlockSpec((d, d), fix)),
        _attn_tail_kernel, x, g1, sh2, s2, norm2_g[0][None], rwh, rwl, rb, tm=2 * TAIL_ROWS)

    sh1b, s1b, g1b, sh2b, s2b, g2b = mods[1]
    x2, u = _moe(h2, ridx, gcol, counts, x1, g2, moe_w_gate, moe_w_up, moe_w_down, 0,
                 pool_args=(sh1b, s1b, norm1_g[1][None], pool_w_in[0].astype(BF16)))
    gd = pool_w_group.shape[2]
    tp = 2 * TAIL_ROWS
    nh = tp // HALO_ROWS
    front_specs = (
        pl.BlockSpec((1, tp, d), lambda bi, i: (bi, i, 0)),
        pl.BlockSpec((1, HALO_ROWS, d), lambda bi, i: (bi, jnp.maximum(i * nh - 1, 0), 0)),
        pl.BlockSpec((1, HALO_ROWS, d), lambda bi, i: (bi, jnp.minimum((i + 1) * nh, l // HALO_ROWS - 1), 0)),
        pl.BlockSpec((len(POOL_WINDOWS), gd, gd), lambda bi, i: (0, 0, 0)),
        pl.BlockSpec((1, d), fix),
        pl.BlockSpec((d, d), fix))
    x3, h2b, ridx_b, gcol_b, counts_b = _mixer_tail(
        (u, u, u, pool_w_group[0].astype(BF16), pool_scale[0][None], pool_w_out[0].astype(BF16)),
        front_specs, functools.partial(_pool_tail_kernel, seq_len=l),
        x2, g1b, sh2b, s2b, norm2_g[1][None], rwh, rwl, rb, tm=tp,
        scratch=[pltpu.VMEM((tp + 2 * POOL_HALO, d), F32)] * 2)
    (out,) = _moe(h2b, ridx_b, gcol_b, counts_b, x3, g2b, moe_w_gate, moe_w_up, moe_w_down, 1)
    return out
```
